```python
import math
import jax, jax.numpy as jnp
from jax import lax
import numpy as np

D_MODEL = 1024
BATCH = 2
SEQ = 8192
DEPTH = 2

GRID_W = 64
CTX_LEN = 256

S5_WIDTH = 512
S5_GROUP_CH = 16
S5_GROUPS = S5_WIDTH // S5_GROUP_CH
S5_STATE = 64
S5_DT_MIN = 0.001
S5_DT_MAX = 0.1
SG_WIDTH = 512
SG_HEADS = 8
SG_CHUNK = 128
N_HEADS = 16
N_KV_HEADS = 4
HEAD_DIM = 64
WINDOW = 128
ATT_BLOCK = 128
ROPE_BASE = 10000.0
NEG_INF = -1e30

LN_EPS = 1e-5
DEEPNORM_ALPHA = (2 * DEPTH) ** 0.25
DEEPNORM_BETA = (8 * DEPTH) ** -0.25

EVEN_IN_WIDTH = 2 * S5_WIDTH + 3 * SG_WIDTH
EVEN_MIX_WIDTH = S5_WIDTH + SG_WIDTH
ODD_IN_WIDTH = 2 * N_HEADS * HEAD_DIM + 2 * N_KV_HEADS * HEAD_DIM
ODD_MIX_WIDTH = N_HEADS * HEAD_DIM
N_EVEN = (DEPTH + 1) // 2
N_ODD = DEPTH // 2

kernel_name = "hybrid_s5_gmlp_swa_prefix_dit"


def _layer_norm(x, g, b):
    xf = x.astype(jnp.float32)
    mu = jnp.mean(xf, axis=-1, keepdims=True)
    var = jnp.mean(jnp.square(xf - mu), axis=-1, keepdims=True)
    y = (xf - mu) * lax.rsqrt(var + LN_EPS) * g.astype(jnp.float32) + b.astype(jnp.float32)
    return y.astype(x.dtype)


def _flip_segments(a, n_ctx):
    return jnp.concatenate([a[:, :n_ctx][:, ::-1], a[:, n_ctx:][:, ::-1]], axis=1)


def _s5_scan(u, lam_re, lam_im, log_dt, b_re, b_im, c_re, c_im):
    dt = jnp.exp(log_dt)[:, None]
    mag = jnp.exp(lam_re * dt)
    ang = lam_im * dt
    a_re = mag * jnp.cos(ang)
    a_im = mag * jnp.sin(ang)
    den = lam_re * lam_re + lam_im * lam_im
    nr = a_re - 1.0
    ni = a_im
    f_re = (nr * lam_re + ni * lam_im) / den
    f_im = (ni * lam_re - nr * lam_im) / den
    bb_re = f_re[..., None] * b_re - f_im[..., None] * b_im
    bb_im = f_re[..., None] * b_im + f_im[..., None] * b_re
    bu_re = jnp.einsum('bngh,gph->bngp', u, bb_re)
    bu_im = jnp.einsum('bngh,gph->bngp', u, bb_im)
    ar = jnp.broadcast_to(a_re, bu_re.shape)
    ai = jnp.broadcast_to(a_im, bu_re.shape)

    def combine(e1, e2):
        a1r, a1i, b1r, b1i = e1
        a2r, a2i, b2r, b2i = e2
        return (a2r * a1r - a2i * a1i,
                a2r * a1i + a2i * a1r,
                a2r * b1r - a2i * b1i + b2r,
                a2r * b1i + a2i * b1r + b2i)

    _, _, s_re, s_im = lax.associative_scan(combine, (ar, ai, bu_re, bu_im), axis=1)
    return (jnp.einsum('bngp,ghp->bngh', s_re, c_re)
            - jnp.einsum('bngp,ghp->bngh', s_im, c_im))


def _s5_bidirectional(xa, n_ctx, lam_re, lam_im, log_dt, b_re, b_im, c_re, c_im, d_skip):
    bsz, n, w = xa.shape
    f = lambda t: t.astype(jnp.float32)
    u = f(xa).reshape(bsz, n, S5_GROUPS, S5_GROUP_CH)
    y_fwd = _s5_scan(u, f(lam_re[0]), f(lam_im[0]), f(log_dt[0]),
                     f(b_re[0]), f(b_im[0]), f(c_re[0]), f(c_im[0]))
    y_bwd = _flip_segments(
        _s5_scan(_flip_segments(u, n_ctx), f(lam_re[1]), f(lam_im[1]), f(log_dt[1]),
                 f(b_re[1]), f(b_im[1]), f(c_re[1]), f(c_im[1])), n_ctx)
    y = y_fwd + y_bwd + f(d_skip).reshape(S5_GROUPS, S5_GROUP_CH) * u
    return y.reshape(bsz, n, w).astype(xa.dtype)


def _spatial_gate(u, v, ln_g, ln_b, w_s, b_s):
    bsz, n, w = v.shape
    vn = _layer_norm(v, ln_g, ln_b)
    vc = vn.reshape(bsz, n // SG_CHUNK, SG_CHUNK, SG_HEADS, w // SG_HEADS)
    s = jnp.einsum('hqk,bckhd->bcqhd', w_s, vc) + jnp.transpose(b_s)[None, None, :, :, None]
    return u * s.reshape(bsz, n, w)


def _even_mixer(h, n_ctx, w_in, w_out, lam_re, lam_im, log_dt, b_re, b_im, c_re, c_im,
                d_skip, glu_w, glu_b, sg_ln_g, sg_ln_b, sg_w, sg_b):
    p = h @ w_in
    xa, ga, u, v, gb = jnp.split(
        p, [S5_WIDTH, 2 * S5_WIDTH, 2 * S5_WIDTH + SG_WIDTH, 2 * S5_WIDTH + 2 * SG_WIDTH], axis=-1)
    ya = _s5_bidirectional(xa, n_ctx, lam_re, lam_im, log_dt, b_re, b_im, c_re, c_im, d_skip)
    ya = jax.nn.gelu(ya)
    ya = ya * jax.nn.sigmoid(ya @ glu_w + glu_b)
    ya = ya * jax.nn.silu(ga)
    yb = _spatial_gate(u, v, sg_ln_g, sg_ln_b, sg_w, sg_b) * jax.nn.silu(gb)
    return jnp.concatenate([ya, yb], axis=-1) @ w_out


def _axial_rope(t, row_pos, col_pos):
    half = HEAD_DIM // 2
    nf = half // 2
    inv = ROPE_BASE ** (-jnp.arange(nf, dtype=jnp.float32) / nf)

    def rot(xh, pos):
        ang = pos.astype(jnp.float32)[:, None] * inv[None, :]
        cos = jnp.cos(ang)[None, :, None, :]
        sin = jnp.sin(ang)[None, :, None, :]
        x1, x2 = xh[..., :nf], xh[..., nf:]
        return jnp.concatenate([x1 * cos - x2 * sin, x2 * cos + x1 * sin], axis=-1)

    tf = t.astype(jnp.float32)
    out = jnp.concatenate([rot(tf[..., :half], row_pos), rot(tf[..., half:], col_pos)], axis=-1)
    return out.astype(t.dtype)


def _softmax_with_sink(logits, sink):
    full = jnp.concatenate([logits, jnp.broadcast_to(sink, logits.shape[:-1] + (1,))], axis=-1)
    return jax.nn.softmax(full, axis=-1)[..., :-1]


def _window_attention(q, k, v, sink, n_ctx):
    bsz, n, _, _ = q.shape
    n_lat = n - n_ctx
    grp = N_HEADS // N_KV_HEADS
    q = (q * (HEAD_DIM ** -0.5)).reshape(bsz, n, N_KV_HEADS, grp, HEAD_DIM)
    qc, ql = q[:, :n_ctx], q[:, n_ctx:]
    kc, kl = k[:, :n_ctx], k[:, n_ctx:]
    vc, vl = v[:, :n_ctx], v[:, n_ctx:]
    sink_f = sink.astype(jnp.float32).reshape(N_KV_HEADS, grp)[None, :, :, None, None]

    lc = jnp.einsum('bqhgd,bshd->bhgqs', qc, kc, preferred_element_type=jnp.float32)
    pc = _softmax_with_sink(lc, sink_f)
    oc = jnp.einsum('bhgqs,bshd->bqhgd', pc.astype(vc.dtype), vc)

    nblk = n_lat // ATT_BLOCK
    qb = jnp.moveaxis(ql.reshape(bsz, nblk, ATT_BLOCK, N_KV_HEADS, grp, HEAD_DIM), 1, 0)

    def windows(t):
        tp = jnp.pad(t, ((0, 0), (ATT_BLOCK, ATT_BLOCK), (0, 0), (0, 0)))
        tp = tp.reshape(bsz, nblk + 2, ATT_BLOCK, N_KV_HEADS, HEAD_DIM)
        tw = jnp.concatenate([tp[:, :-2], tp[:, 1:-1], tp[:, 2:]], axis=2)
        return jnp.moveaxis(tw, 1, 0)

    kw, vw = windows(kl), windows(vl)
    offs_q = jnp.arange(ATT_BLOCK)
    offs_k = jnp.arange(3 * ATT_BLOCK) - ATT_BLOCK
    rel_ok = jnp.abs(offs_k[None, :] - offs_q[:, None]) <= WINDOW

    def block(args):
        qi, ki, vi, bi = args
        kpos = bi * ATT_BLOCK + offs_k
        ok = rel_ok & ((kpos >= 0) & (kpos < n_lat))[None, :]
        ll = jnp.einsum('bqhgd,bshd->bhgqs', qi, ki, preferred_element_type=jnp.float32)
        ll = jnp.where(ok, ll, NEG_INF)
        lx = jnp.einsum('bqhgd,bshd->bhgqs', qi, kc, preferred_element_type=jnp.float32)
        p = _softmax_with_sink(jnp.concatenate([ll, lx], axis=-1), sink_f)
        pl = p[..., :3 * ATT_BLOCK].astype(vi.dtype)
        px = p[..., 3 * ATT_BLOCK:].astype(vc.dtype)
        return (jnp.einsum('bhgqs,bshd->bqhgd', pl, vi)
                + jnp.einsum('bhgqs,bshd->bqhgd', px, vc))

    ob = lax.map(block, (qb, kw, vw, jnp.arange(nblk)))
    ol = jnp.moveaxis(ob, 0, 1).reshape(bsz, n_lat, N_HEADS * HEAD_DIM)
    return jnp.concatenate([oc.reshape(bsz, n_ctx, N_HEADS * HEAD_DIM), ol], axis=1)


def _odd_mixer(h, n_ctx, row_pos, col_pos, w_in, w_out, sink):
    bsz, n, _ = h.shape
    dq = N_HEADS * HEAD_DIM
    dkv = N_KV_HEADS * HEAD_DIM
    p = h @ w_in
    q, k, v, g = jnp.split(p, [dq, dq + dkv, dq + 2 * dkv], axis=-1)
    q = q.reshape(bsz, n, N_HEADS, HEAD_DIM)
    k = k.reshape(bsz, n, N_KV_HEADS, HEAD_DIM)
    v = v.reshape(bsz, n, N_KV_HEADS, HEAD_DIM)
    q = jnp.concatenate([q[:, :n_ctx], _axial_rope(q[:, n_ctx:], row_pos, col_pos)], axis=1)
    k = jnp.concatenate([k[:, :n_ctx], _axial_rope(k[:, n_ctx:], row_pos, col_pos)], axis=1)
    o = _window_attention(q, k, v, sink, n_ctx)
    return (o * jax.nn.silu(g)) @ w_out


def setup_inputs(seed: int = 0) -> dict:
    key = jax.random.key(seed)
    ks = jax.random.split(key, 32)
    f32 = jnp.float32

    def nrm(k, shape, scale):
        return scale * jax.random.normal(k, shape, f32)

    D = D_MODEL
    lam_im_base = jnp.pi * jnp.arange(S5_STATE, dtype=f32)
    return {
        "x": nrm(ks[0], (BATCH, SEQ, D), 1.0),
        "c": nrm(ks[1], (BATCH, D), 1.0),
        "ctx": nrm(ks[2], (BATCH, CTX_LEN, D), 1.0),
        "c_ctx": nrm(ks[3], (D,), 1.0),
        "mod_w": nrm(ks[4], (DEPTH, D, 3 * D), 0.5 * D ** -0.5),
        "mod_b": nrm(ks[5], (DEPTH, 3 * D), 0.01),
        "ln_g": 1.0 + nrm(ks[6], (DEPTH, D), 0.02),
        "ln_b": nrm(ks[7], (DEPTH, D), 0.02),
        "e_w_in": nrm(ks[8], (N_EVEN, D, EVEN_IN_WIDTH), D ** -0.5),
        "e_w_out": nrm(ks[9], (N_EVEN, EVEN_MIX_WIDTH, D), DEEPNORM_BETA * EVEN_MIX_WIDTH ** -0.5),
        "s5_lam_re": -0.5 + nrm(ks[10], (N_EVEN, 2, S5_GROUPS, S5_STATE), 0.01),
        "s5_lam_im": lam_im_base + nrm(ks[11], (N_EVEN, 2, S5_GROUPS, S5_STATE), 0.01),
        "s5_log_dt": jax.random.uniform(ks[12], (N_EVEN, 2, S5_GROUPS), f32,
                                        minval=math.log(S5_DT_MIN), maxval=math.log(S5_DT_MAX)),
        "s5_b_re": nrm(ks[13], (N_EVEN, 2, S5_GROUPS, S5_STATE, S5_GROUP_CH), (2 * S5_GROUP_CH) ** -0.5),
        "s5_b_im": nrm(ks[14], (N_EVEN, 2, S5_GROUPS, S5_STATE, S5_GROUP_CH), (2 * S5_GROUP_CH) ** -0.5),
        "s5_c_re": nrm(ks[15], (N_EVEN, 2, S5_GROUPS, S5_GROUP_CH, S5_STATE), S5_STATE ** -0.5),
        "s5_c_im": nrm(ks[16], (N_EVEN, 2, S5_GROUPS, S5_GROUP_CH, S5_STATE), S5_STATE ** -0.5),
        "s5_d": nrm(ks[17], (N_EVEN, S5_WIDTH), 1.0),
        "glu_w": nrm(ks[18], (N_EVEN, S5_WIDTH, S5_WIDTH), S5_WIDTH ** -0.5),
        "glu_b": nrm(ks[19], (N_EVEN, S5_WIDTH), 0.01),
        "sg_ln_g": 1.0 + nrm(ks[20], (N_EVEN, SG_WIDTH), 0.02),
        "sg_ln_b": nrm(ks[21], (N_EVEN, SG_WIDTH), 0.02),
        "sg_w": nrm(ks[22], (N_EVEN, SG_HEADS, SG_CHUNK, SG_CHUNK), 0.5 * SG_CHUNK ** -0.5),
        "sg_b": 1.0 + nrm(ks[23], (N_EVEN, SG_HEADS, SG_CHUNK), 0.1),
        "o_w_in": nrm(ks[24], (N_ODD, D, ODD_IN_WIDTH), D ** -0.5),
        "o_w_out": nrm(ks[25], (N_ODD, ODD_MIX_WIDTH, D), DEEPNORM_BETA * ODD_MIX_WIDTH ** -0.5),
        "o_sink": nrm(ks[26], (N_ODD, N_HEADS), 0.5),
    }


def reference(x, c, ctx, c_ctx, mod_w, mod_b, ln_g, ln_b, e_w_in, e_w_out,
              s5_lam_re, s5_lam_im, s5_log_dt, s5_b_re, s5_b_im, s5_c_re, s5_c_im, s5_d,
              glu_w, glu_b, sg_ln_g, sg_ln_b, sg_w, sg_b, o_w_in, o_w_out, o_sink):
    n_ctx = ctx.shape[1]
    n_lat = x.shape[1]
    ROWS = n_lat // GRID_W
    row_pos = jnp.repeat(jnp.arange(ROWS), GRID_W)
    col_pos = jnp.arange(ROWS * GRID_W) % GRID_W

    z = jnp.concatenate([ctx, x], axis=1)
    silu_c = jax.nn.silu(c)
    silu_cc = jax.nn.silu(c_ctx)
    for i in range(DEPTH):
        mod_l = silu_c @ mod_w[i] + mod_b[i]
        mod_c = silu_cc @ mod_w[i] + mod_b[i]
        sh_l, sc_l, g_l = jnp.split(mod_l, 3, axis=-1)
        sh_c, sc_c, g_c = jnp.split(mod_c, 3, axis=-1)
        h = jnp.concatenate([z[:, :n_ctx] * (1.0 + sc_c) + sh_c,
                             z[:, n_ctx:] * (1.0 + sc_l[:, None]) + sh_l[:, None]], axis=1)
        j = i // 2
        if i % 2 == 0:
            y = _even_mixer(h, n_ctx, e_w_in[j], e_w_out[j], s5_lam_re[j], s5_lam_im[j],
                            s5_log_dt[j], s5_b_re[j], s5_b_im[j], s5_c_re[j], s5_c_im[j],
                            s5_d[j], glu_w[j], glu_b[j], sg_ln_g[j], sg_ln_b[j], sg_w[j], sg_b[j])
        else:
            y = _odd_mixer(h, n_ctx, row_pos, col_pos, o_w_in[j], o_w_out[j], o_sink[j])
        y = jnp.concatenate([y[:, :n_ctx] * g_c, y[:, n_ctx:] * g_l[:, None]], axis=1)
        z = _layer_norm(DEEPNORM_ALPHA * z + y, ln_g[i], ln_b[i])
    return z[:, n_ctx:]
```

```python
import functools
import math

import jax
import jax.numpy as jnp
from jax import lax
from jax.experimental import pallas as pl
from jax.experimental.pallas import tpu as pltpu

F32 = jnp.float32
BF16 = jnp.bfloat16
HIGHEST = lax.Precision.HIGHEST

D = 1024
B = 2
N_LAT = 8192
N_CTX = 256
DEPTH = 2
GRID_W = 64
S5_W = 512
S5_GC = 16
S5_G = 32
S5_P = 64
SG_W = 512
SG_HEADS = 8
SG_CHUNK = 128
N_HEADS = 16
N_KV = 4
HD = 64
WINDOW = 128
ATT_BLK = 128
ROPE_BASE = 10000.0
NEG_INF = -1e30
LN_EPS = 1e-5
ALPHA = (2 * DEPTH) ** 0.25
EVEN_IN = 2 * S5_W + 3 * SG_W
ODD_IN = 2 * N_HEADS * HD + 2 * N_KV * HD

LANES = 128
VMEM_LIMIT = 56 * 1024 * 1024

R_LAT = B * N_LAT
R_CTX = B * N_CTX
R_ALL = R_LAT + R_CTX
TN = 512
N_LAT_BLK = R_LAT // TN
N_BLK = R_ALL // TN
LAT_BLK_PER_BATCH = N_LAT // TN
CTX_MOD_ROW = B

S5_CHUNK = 8
S5_QT = S5_W // LANES
S5_GPT = LANES // S5_GC
S5_ST = S5_GPT * S5_P
N_CHUNK = R_ALL // S5_CHUNK
CHUNK_BLK = 528
N_CHUNK_BLK = N_CHUNK // CHUNK_BLK
LAT_CHUNKS = N_LAT // S5_CHUNK
CTX_CHUNKS = N_CTX // S5_CHUNK
CTX_CHUNK0 = R_LAT // S5_CHUNK


def _sigmoid(x):
    return 1.0 / (1.0 + jnp.exp(-x))


def _silu(x):
    return x * _sigmoid(x)


def _gelu_tanh(x):
    return 0.5 * x * (1.0 + jnp.tanh(math.sqrt(2.0 / math.pi) * (x + 0.044715 * (x * x * x))))


def _layer_norm(x, g, b):
    mu = jnp.mean(x, axis=-1, keepdims=True)
    xc = x - mu
    var = jnp.mean(xc * xc, axis=-1, keepdims=True)
    return xc * lax.rsqrt(var + LN_EPS) * g + b


def _block_mod_row(i):
    return jnp.where(i < N_LAT_BLK, i // LAT_BLK_PER_BATCH, CTX_MOD_ROW)


def _mod_kernel(cv_ref, w_ref, b_ref, o_ref):
    s = _silu(cv_ref[...])
    o_ref[0] = jnp.dot(s, w_ref[0], preferred_element_type=F32, precision=HIGHEST) + b_ref[0]


def _modulation(c, c_ctx, mod_w, mod_b):
    cv = jnp.zeros((8, D), F32).at[:B].set(c).at[CTX_MOD_ROW].set(c_ctx)
    tn = 1024
    return pl.pallas_call(
        _mod_kernel,
        grid=(DEPTH, 3 * D // tn),
        in_specs=[
            pl.BlockSpec((8, D), lambda l, j: (0, 0)),
            pl.BlockSpec((1, D, tn), lambda l, j: (l, 0, j)),
            pl.BlockSpec((1, 1, tn), lambda l, j: (l, 0, j)),
        ],
        out_specs=pl.BlockSpec((1, 8, tn), lambda l, j: (l, 0, j)),
        out_shape=jax.ShapeDtypeStruct((DEPTH, 8, 3 * D), F32),
        compiler_params=pltpu.CompilerParams(vmem_limit_bytes=VMEM_LIMIT),
        name="modulation",
    )(cv, mod_w, mod_b.reshape(DEPTH, 1, 3 * D))


def _in0_kernel(x_ref, ctx_ref, mod_ref, w_ref, xa_ref, rest_ref):
    i = pl.program_id(0)
    z = jnp.where(i < N_LAT_BLK, x_ref[...], ctx_ref[...])
    m = mod_ref[pl.ds(_block_mod_row(i), 1), :]
    h = z * (1.0 + m[:, D:2 * D]) + m[:, :D]
    p = jnp.dot(h.astype(BF16), w_ref[...], preferred_element_type=F32)
    xa_ref[...] = p[:, :S5_W]
    rest_ref[...] = p[:, S5_W:].astype(BF16)


def _in0(x2, ctx2, mod, w_in):
    return pl.pallas_call(
        _in0_kernel,
        grid=(N_BLK,),
        in_specs=[
            pl.BlockSpec((TN, D), lambda i: (jnp.minimum(i, N_LAT_BLK - 1), 0)),
            pl.BlockSpec((R_CTX, D), lambda i: (0, 0)),
            pl.BlockSpec((8, 3 * D), lambda i: (0, 0)),
            pl.BlockSpec((D, EVEN_IN), lambda i: (0, 0)),
        ],
        out_specs=[
            pl.BlockSpec((TN, S5_W), lambda i: (i, 0)),
            pl.BlockSpec((TN, EVEN_IN - S5_W), lambda i: (i, 0)),
        ],
        out_shape=[
            jax.ShapeDtypeStruct((R_ALL, S5_W), F32),
            jax.ShapeDtypeStruct((R_ALL, EVEN_IN - S5_W), BF16),
        ],
        compiler_params=pltpu.CompilerParams(vmem_limit_bytes=VMEM_LIMIT),
        name="in0",
    )(x2, ctx2, mod, w_in)


def _s5_matrices(lam_re, lam_im, log_dt, b_re, b_im, c_re, c_im, d_skip):
    L = S5_CHUNK
    dt = jnp.exp(log_dt)[None, :, :, None]
    ks = jnp.arange(L + 1, dtype=F32)[:, None, None, None]
    mag = jnp.exp(lam_re[None] * dt * ks)
    ang = lam_im[None] * dt * ks
    pw_re = mag * jnp.cos(ang)
    pw_im = mag * jnp.sin(ang)
    a_re, a_im = pw_re[1], pw_im[1]
    den = lam_re * lam_re + lam_im * lam_im
    nr = a_re - 1.0
    ni = a_im
    f_re = (nr * lam_re + ni * lam_im) / den
    f_im = (ni * lam_re - nr * lam_im) / den
    bb_re = f_re[..., None] * b_re - f_im[..., None] * b_im
    bb_im = f_re[..., None] * b_im + f_im[..., None] * b_re
    cp_re = c_re[None] * pw_re[:, :, :, None, :] - c_im[None] * pw_im[:, :, :, None, :]
    cp_im = c_re[None] * pw_im[:, :, :, None, :] + c_im[None] * pw_re[:, :, :, None, :]
    kern = (jnp.einsum('kdghp,dgpj->kdghj', cp_re, bb_re, precision=HIGHEST)
            - jnp.einsum('kdghp,dgpj->kdghj', cp_im, bb_im, precision=HIGHEST))

    s_idx = jnp.arange(L)[:, None]
    t_idx = jnp.arange(L)[None, :]
    kf = kern[jnp.clip(t_idx - s_idx, 0, L - 1), 0]
    kb = kern[jnp.clip(s_idx - t_idx, 0, L - 1), 1]
    kd = kern[0, 0] + kern[0, 1] + (d_skip.reshape(S5_G, S5_GC)[:, :, None]
                                    * jnp.eye(S5_GC, dtype=F32)[None])
    sel = lambda m: m[:, :, None, None, None]
    t5 = (jnp.where(sel(t_idx > s_idx), kf, 0.0) + jnp.where(sel(s_idx > t_idx), kb, 0.0)
          + jnp.where(sel(s_idx == t_idx), kd[None, None], 0.0))
    eye = jnp.eye(S5_GPT, dtype=F32)
    t5 = t5.reshape(L, L, S5_QT, S5_GPT, S5_GC, S5_GC)
    t_mat = jnp.einsum('stqghj,gG->qsgjtGh', t5, eye).reshape(S5_QT, L * LANES, L * LANES)

    idx_f = L - 1 - jnp.arange(L)
    idx_b = jnp.arange(L)

    def e_comp(idx, d):
        pr = pw_re[idx, d][:, :, :, None]
        pi = pw_im[idx, d][:, :, :, None]
        return pr * bb_re[d][None] - pi * bb_im[d][None], pr * bb_im[d][None] + pi * bb_re[d][None]

    ef_re, ef_im = e_comp(idx_f, 0)
    eb_re, eb_im = e_comp(idx_b, 1)
    e5 = jnp.stack([ef_re, ef_im, eb_re, eb_im]).reshape(4, L, S5_QT, S5_GPT, S5_P, S5_GC)
    e_mat = jnp.einsum('csqgpj,gG->qsgjcGp', e5, eye).reshape(S5_QT, L * LANES, 4 * S5_ST)

    tf = jnp.arange(L) + 1
    tb = L - jnp.arange(L)
    f5 = jnp.stack([cp_re[tf, 0], -cp_im[tf, 0], cp_re[tb, 1], -cp_im[tb, 1]])
    f5 = f5.reshape(4, L, S5_QT, S5_GPT, S5_GC, S5_P)
    f_mat = jnp.einsum('ctqghp,gG->qcgptGh', f5, eye).reshape(S5_QT, 4 * S5_ST, L * LANES)

    al_re = pw_re[L].reshape(2, S5_QT, S5_ST).transpose(1, 0, 2).reshape(2 * S5_QT, 1, S5_ST)
    al_im = pw_im[L].reshape(2, S5_QT, S5_ST).transpose(1, 0, 2).reshape(2 * S5_QT, 1, S5_ST)
    return t_mat.astype(BF16), e_mat.astype(BF16), f_mat.astype(BF16), al_re, al_im


def _chunk_rows(xa_ref):
    return jnp.concatenate(
        [xa_ref[pl.ds(t, CHUNK_BLK, stride=S5_CHUNK), :].astype(BF16) for t in range(S5_CHUNK)],
        axis=-1)


def _s5_states_kernel(xa_ref, e_ref, st_ref):
    st_ref[...] = jnp.dot(_chunk_rows(xa_ref), e_ref[0], preferred_element_type=F32)


def _s5_states(xa, e_mat):
    L = S5_CHUNK
    return pl.pallas_call(
        _s5_states_kernel,
        grid=(S5_QT, N_CHUNK_BLK),
        in_specs=[
            pl.BlockSpec((CHUNK_BLK * L, LANES), lambda q, i: (i, q)),
            pl.BlockSpec((1, L * LANES, 4 * S5_ST), lambda q, i: (q, 0, 0)),
        ],
        out_specs=pl.BlockSpec((CHUNK_BLK, 4 * S5_ST), lambda q, i: (i, q)),
        out_shape=jax.ShapeDtypeStruct((N_CHUNK, S5_QT * 4 * S5_ST), F32),
        compiler_params=pltpu.CompilerParams(vmem_limit_bytes=VMEM_LIMIT),
        name="s5_states",
    )(xa, e_mat)


def _s5_scan_kernel(loc_ref, ar_ref, ai_ref, out_ref):
    backward = pl.program_id(0) % 2
    ar = ar_ref[0]
    ai = ai_ref[0]

    def run(base, n, carry):
        def step(j, carry):
            new = []
            for b in range(B):
                cre, cim = carry[b]
                row = base[b] + jnp.where(backward == 1, n - 1 - j, j)
                x = loc_ref[pl.ds(row, 1), :]
                out_ref[pl.ds(row, 1), :] = jnp.concatenate([cre, cim], axis=-1)
                nre = ar * cre - ai * cim + x[:, :S5_ST]
                nim = ar * cim + ai * cre + x[:, S5_ST:]
                new.append((nre, nim))
            return tuple(new)
        return lax.fori_loop(0, n, step, carry)

    zero = jnp.zeros((1, S5_ST), F32)
    carry = tuple((zero, zero) for _ in range(B))
    carry = run([CTX_CHUNK0 + b * CTX_CHUNKS for b in range(B)], CTX_CHUNKS, carry)
    run([b * LAT_CHUNKS for b in range(B)], LAT_CHUNKS, carry)


def _s5_scan(st_loc, al_re, al_im):
    return pl.pallas_call(
        _s5_scan_kernel,
        grid=(2 * S5_QT,),
        in_specs=[
            pl.BlockSpec((N_CHUNK, 2 * S5_ST), lambda j: (0, j)),
            pl.BlockSpec((1, 1, S5_ST), lambda j: (j, 0, 0)),
            pl.BlockSpec((1, 1, S5_ST), lambda j: (j, 0, 0)),
        ],
        out_specs=pl.BlockSpec((N_CHUNK, 2 * S5_ST), lambda j: (0, j)),
        out_shape=jax.ShapeDtypeStruct((N_CHUNK, S5_QT * 4 * S5_ST), F32),
        compiler_params=pltpu.CompilerParams(vmem_limit_bytes=VMEM_LIMIT),
        name="s5_scan",
    )(st_loc, al_re, al_im)


def _s5_out_kernel(xa_ref, st_ref, t_ref, f_ref, y_ref):
    y = (jnp.dot(_chunk_rows(xa_ref), t_ref[0], preferred_element_type=F32)
         + jnp.dot(st_ref[...].astype(BF16), f_ref[0], preferred_element_type=F32))
    for t in range(S5_CHUNK):
        y_ref[pl.ds(t, CHUNK_BLK, stride=S5_CHUNK), :] = y[:, t * LANES:(t + 1) * LANES]


def _s5_out(xa, st_in, t_mat, f_mat):
    L = S5_CHUNK
    return pl.pallas_call(
        _s5_out_kernel,
        grid=(S5_QT, N_CHUNK_BLK),
        in_specs=[
            pl.BlockSpec((CHUNK_BLK * L, LANES), lambda q, i: (i, q)),
            pl.BlockSpec((CHUNK_BLK, 4 * S5_ST), lambda q, i: (i, q)),
            pl.BlockSpec((1, L * LANES, L * LANES), lambda q, i: (q, 0, 0)),
            pl.BlockSpec((1, 4 * S5_ST, L * LANES), lambda q, i: (q, 0, 0)),
        ],
        out_specs=pl.BlockSpec((CHUNK_BLK * L, LANES), lambda q, i: (i, q)),
        out_shape=jax.ShapeDtypeStruct((R_ALL, S5_W), F32),
        compiler_params=pltpu.CompilerParams(vmem_limit_bytes=VMEM_LIMIT),
        name="s5_out",
    )(xa, st_in, t_mat, f_mat)


def _tail0_kernel(x_ref, ctx_ref, mod_ref, ys_ref, rest_ref, gluw_ref, glub_ref, sg_g_ref,
                  sg_b_ref, sgw_ref, sgbias_ref, wout_ref, lng_ref, lnb_ref, o_ref):
    i = pl.program_id(0)
    z = jnp.where(i < N_LAT_BLK, x_ref[...], ctx_ref[...])
    gate = mod_ref[pl.ds(_block_mod_row(i), 1), 2 * D:3 * D]

    ga = rest_ref[:, 0:S5_W].astype(F32)
    u = rest_ref[:, S5_W:2 * S5_W].astype(F32)
    v = rest_ref[:, 2 * S5_W:3 * S5_W].astype(F32)
    gb = rest_ref[:, 3 * S5_W:4 * S5_W].astype(F32)

    ya = _gelu_tanh(ys_ref[...])
    glu = jnp.dot(ya.astype(BF16), gluw_ref[...], preferred_element_type=F32) + glub_ref[...]
    ya = ya * _sigmoid(glu) * _silu(ga)

    vn = _layer_norm(v, sg_g_ref[...], sg_b_ref[...]).astype(BF16)
    lane = lax.broadcasted_iota(jnp.int32, (SG_CHUNK, LANES), 1)
    first_head = lane < (SG_W // SG_HEADS)
    chunks = []
    for c in range(TN // SG_CHUNK):
        tiles = []
        for j in range(SG_W // LANES):
            vt = vn[c * SG_CHUNK:(c + 1) * SG_CHUNK, j * LANES:(j + 1) * LANES]
            r0 = jnp.dot(sgw_ref[2 * j], vt, preferred_element_type=F32)
            r1 = jnp.dot(sgw_ref[2 * j + 1], vt, preferred_element_type=F32)
            tiles.append(jnp.where(first_head, r0, r1))
        chunks.append(jnp.concatenate(tiles, axis=-1) + sgbias_ref[...])
    s = jnp.concatenate(chunks, axis=0)
    yb = u * s * _silu(gb)

    mix = jnp.concatenate([ya, yb], axis=-1).astype(BF16)
    y = jnp.dot(mix, wout_ref[...], preferred_element_type=F32) * gate
    o_ref[...] = _layer_norm(ALPHA * z + y, lng_ref[...], lnb_ref[...])


def _tail0(x2, ctx2, mod, y_s5, rest, glu_w, glu_b, sg_g, sg_b, sg_w, sg_bias, w_out, ln_g, ln_b):
    row = lambda n: pl.BlockSpec((1, n), lambda i: (0, 0))
    return pl.pallas_call(
        _tail0_kernel,
        grid=(N_BLK,),
        in_specs=[
            pl.BlockSpec((TN, D), lambda i: (jnp.minimum(i, N_LAT_BLK - 1), 0)),
            pl.BlockSpec((R_CTX, D), lambda i: (0, 0)),
            pl.BlockSpec((8, 3 * D), lambda i: (0, 0)),
            pl.BlockSpec((TN, S5_W), lambda i: (i, 0)),
            pl.BlockSpec((TN, EVEN_IN - S5_W), lambda i: (i, 0)),
            pl.BlockSpec((S5_W, S5_W), lambda i: (0, 0)),
            row(S5_W), row(SG_W), row(SG_W),
            pl.BlockSpec((SG_HEADS, SG_CHUNK, SG_CHUNK), lambda i: (0, 0, 0)),
            pl.BlockSpec((SG_CHUNK, SG_W), lambda i: (0, 0)),
            pl.BlockSpec((S5_W + SG_W, D), lambda i: (0, 0)),
            row(D), row(D),
        ],
        out_specs=pl.BlockSpec((TN, D), lambda i: (i, 0)),
        out_shape=jax.ShapeDtypeStruct((R_ALL, D), F32),
        compiler_params=pltpu.CompilerParams(vmem_limit_bytes=VMEM_LIMIT),
        name="tail0",
    )(x2, ctx2, mod, y_s5, rest, glu_w, glu_b, sg_g, sg_b, sg_w, sg_bias, w_out, ln_g, ln_b)


def _rope_tables():
    half = HD // 2
    nf = half // 2
    inv = ROPE_BASE ** (-jnp.arange(nf, dtype=F32) / nf)
    t = jnp.arange(N_LAT)
    row_ang = (t // GRID_W).astype(F32)[:, None] * inv[None, :]
    col_ang = (t % GRID_W).astype(F32)[:, None] * inv[None, :]
    cos_h = jnp.concatenate([jnp.cos(row_ang)] * 2 + [jnp.cos(col_ang)] * 2, axis=-1)
    sin_h = jnp.concatenate([-jnp.sin(row_ang), jnp.sin(row_ang),
                             -jnp.sin(col_ang), jnp.sin(col_ang)], axis=-1)
    cos_t = jnp.concatenate([jnp.tile(cos_h, (1, LANES // HD)), jnp.ones((TN, LANES), F32)], axis=0)
    sin_t = jnp.concatenate([jnp.tile(sin_h, (1, LANES // HD)), jnp.zeros((TN, LANES), F32)], axis=0)
    return cos_t, sin_t


def _rope_tile(x, cos, sin, first_half):
    nf = HD // 4
    partner = jnp.where(first_half, pltpu.roll(x, LANES - nf, axis=1), pltpu.roll(x, nf, axis=1))
    return x * cos + partner * sin


def _in1_kernel(z_ref, mod_ref, w_ref, cos_ref, sin_ref, q_ref, kv_ref, g_ref):
    i = pl.program_id(0)
    m = mod_ref[pl.ds(_block_mod_row(i), 1), :]
    h = z_ref[...] * (1.0 + m[:, D:2 * D]) + m[:, :D]
    p = jnp.dot(h.astype(BF16), w_ref[...], preferred_element_type=F32)
    dq = N_HEADS * HD
    dkv = N_KV * HD
    cos = cos_ref[...]
    sin = sin_ref[...]
    lane = lax.broadcasted_iota(jnp.int32, (TN, LANES), 1)
    first_half = (lane % (HD // 2)) < (HD // 4)
    scale = HD ** -0.5
    for j in range(dq // LANES):
        q_ref[:, j * LANES:(j + 1) * LANES] = (
            _rope_tile(p[:, j * LANES:(j + 1) * LANES], cos, sin, first_half) * scale).astype(BF16)
    for j in range(dkv // LANES):
        kv_ref[:, j * LANES:(j + 1) * LANES] = _rope_tile(
            p[:, dq + j * LANES:dq + (j + 1) * LANES], cos, sin, first_half).astype(BF16)
    kv_ref[:, dkv:2 * dkv] = p[:, dq + dkv:dq + 2 * dkv].astype(BF16)
    g_ref[...] = p[:, dq + 2 * dkv:].astype(BF16)


def _in1(z1, mod, w_in, cos_t, sin_t):
    dq = N_HEADS * HD
    dkv = N_KV * HD
    tab = lambda i: (jnp.where(i < N_LAT_BLK, i % LAT_BLK_PER_BATCH, LAT_BLK_PER_BATCH), 0)
    return pl.pallas_call(
        _in1_kernel,
        grid=(N_BLK,),
        in_specs=[
            pl.BlockSpec((TN, D), lambda i: (i, 0)),
            pl.BlockSpec((8, 3 * D), lambda i: (0, 0)),
            pl.BlockSpec((D, ODD_IN), lambda i: (0, 0)),
            pl.BlockSpec((TN, LANES), tab),
            pl.BlockSpec((TN, LANES), tab),
        ],
        out_specs=[
            pl.BlockSpec((TN, dq), lambda i: (i, 0)),
            pl.BlockSpec((TN, 2 * dkv), lambda i: (i, 0)),
            pl.BlockSpec((TN, dq), lambda i: (i, 0)),
        ],
        out_shape=[
            jax.ShapeDtypeStruct((R_ALL, dq), BF16),
            jax.ShapeDtypeStruct((R_ALL, 2 * dkv), BF16),
            jax.ShapeDtypeStruct((R_ALL, dq), BF16),
        ],
        compiler_params=pltpu.CompilerParams(vmem_limit_bytes=VMEM_LIMIT),
        name="in1",
    )(z1, mod, w_in, cos_t, sin_t)


N_QBLK = N_LAT // ATT_BLK
GRP = N_HEADS // N_KV


def _attn_kernel(sink_ref, q_ref, kvp_ref, kvc_ref, kvn_ref, kvx_ref, o_ref):
    i = pl.program_id(1)
    dkv = N_KV * HD
    n_win = 3 * ATT_BLK
    n_keys = n_win + N_CTX
    kv = jnp.concatenate([kvp_ref[...], kvc_ref[...], kvn_ref[...], kvx_ref[...]], axis=0)

    qpos = lax.broadcasted_iota(jnp.int32, (ATT_BLK, n_keys), 0)
    col = lax.broadcasted_iota(jnp.int32, (ATT_BLK, n_keys), 1)
    rel = col - ATT_BLK - qpos
    lo = jnp.where(i == 0, ATT_BLK, 0)
    hi = jnp.where(i == N_QBLK - 1, 2 * ATT_BLK, n_win)
    ok = (col >= n_win) | ((jnp.abs(rel) <= WINDOW) & (col >= lo) & (col < hi))
    ok = jnp.concatenate([ok] * GRP, axis=0)
    rgrp = lax.broadcasted_iota(jnp.int32, (GRP * ATT_BLK, 1), 0) // ATT_BLK
    q = q_ref[...]
    outs = []

    for h in range(N_KV):
        kh = kv[:, h * HD:(h + 1) * HD]
        vh = kv[:, dkv + h * HD:dkv + (h + 1) * HD]
        qh = jnp.concatenate(
            [q[:, (h * GRP + g) * HD:(h * GRP + g + 1) * HD] for g in range(GRP)], axis=0)
        s = lax.dot_general(qh, kh, (((1,), (1,)), ((), ())), preferred_element_type=F32)
        s = jnp.where(ok, s, NEG_INF)
        sink = jnp.zeros((GRP * ATT_BLK, 1), F32)
        for g in range(GRP):
            sink = jnp.where(rgrp == g, sink_ref[h * GRP + g], sink)
        m = jnp.maximum(jnp.max(s, axis=-1, keepdims=True), sink)
        p = jnp.exp(s - m)
        den = jnp.sum(p, axis=-1, keepdims=True) + jnp.exp(sink - m)
        o = jnp.dot(p.astype(BF16), vh, preferred_element_type=F32) / den
        outs.extend(o[g * ATT_BLK:(g + 1) * ATT_BLK] for g in range(GRP))
    o_ref[...] = jnp.concatenate(outs, axis=-1).astype(BF16)


def _attention(sink, q, kv):
    dq = N_HEADS * HD
    dkv = N_KV * HD
    ctx_blk0 = R_LAT // N_CTX
    return pl.pallas_call(
        _attn_kernel,
        grid_spec=pltpu.PrefetchScalarGridSpec(
            num_scalar_prefetch=1,
            grid=(B, N_QBLK),
            in_specs=[
                pl.BlockSpec((ATT_BLK, dq), lambda b, i, s: (b * N_QBLK + i, 0)),
                pl.BlockSpec((ATT_BLK, 2 * dkv), lambda b, i, s: (b * N_QBLK + jnp.maximum(i - 1, 0), 0)),
                pl.BlockSpec((ATT_BLK, 2 * dkv), lambda b, i, s: (b * N_QBLK + i, 0)),
                pl.BlockSpec((ATT_BLK, 2 * dkv),
                             lambda b, i, s: (b * N_QBLK + jnp.minimum(i + 1, N_QBLK - 1), 0)),
                pl.BlockSpec((N_CTX, 2 * dkv), lambda b, i, s: (ctx_blk0 + b, 0)),
            ],
            out_specs=pl.BlockSpec((ATT_BLK, dq), lambda b, i, s: (b * N_QBLK + i, 0)),
        ),
        out_shape=jax.ShapeDtypeStruct((R_LAT, dq), BF16),
        compiler_params=pltpu.CompilerParams(vmem_limit_bytes=VMEM_LIMIT),
        name="attention",
    )(sink, q, kv, kv, kv, kv)


def _tail1_kernel(z_ref, mod_ref, o_ref_in, g_ref, wout_ref, lng_ref, lnb_ref, out_ref):
    i = pl.program_id(0)
    gate = mod_ref[pl.ds(_block_mod_row(i), 1), 2 * D:3 * D]
    g = g_ref[...].astype(F32)
    mix = (o_ref_in[...].astype(F32) * _silu(g)).astype(BF16)
    y = jnp.dot(mix, wout_ref[...], preferred_element_type=F32) * gate
    out_ref[...] = _layer_norm(ALPHA * z_ref[...] + y, lng_ref[...], lnb_ref[...])


def _tail1(z1, mod, o, g, w_out, ln_g, ln_b):
    row = lambda n: pl.BlockSpec((1, n), lambda i: (0, 0))
    return pl.pallas_call(
        _tail1_kernel,
        grid=(N_LAT_BLK,),
        in_specs=[
            pl.BlockSpec((TN, D), lambda i: (i, 0)),
            pl.BlockSpec((8, 3 * D), lambda i: (0, 0)),
            pl.BlockSpec((TN, D), lambda i: (i, 0)),
            pl.BlockSpec((TN, D), lambda i: (i, 0)),
            pl.BlockSpec((D, D), lambda i: (0, 0)),
            row(D), row(D),
        ],
        out_specs=pl.BlockSpec((TN, D), lambda i: (i, 0)),
        out_shape=jax.ShapeDtypeStruct((R_LAT, D), F32),
        compiler_params=pltpu.CompilerParams(vmem_limit_bytes=VMEM_LIMIT),
        name="tail1",
    )(z1, mod, o, g, w_out, ln_g, ln_b)


def kernel(x, c, ctx, c_ctx, mod_w, mod_b, ln_g, ln_b, e_w_in, e_w_out, s5_lam_re, s5_lam_im,
           s5_log_dt, s5_b_re, s5_b_im, s5_c_re, s5_c_im, s5_d, glu_w, glu_b, sg_ln_g, sg_ln_b,
           sg_w, sg_b, o_w_in, o_w_out, o_sink):
    x2 = x.reshape(R_LAT, D)
    ctx2 = ctx.reshape(R_CTX, D)
    mod = _modulation(c, c_ctx, mod_w, mod_b)

    xa, rest = _in0(x2, ctx2, mod[0], e_w_in[0].astype(BF16))
    t_mat, e_mat, f_mat, al_re, al_im = _s5_matrices(
        s5_lam_re[0], s5_lam_im[0], s5_log_dt[0], s5_b_re[0], s5_b_im[0], s5_c_re[0], s5_c_im[0],
        s5_d[0])
    st_loc = _s5_states(xa, e_mat)
    st_in = _s5_scan(st_loc, al_re, al_im)
    y_s5 = _s5_out(xa, st_in, t_mat, f_mat)
    sg_bias = jnp.repeat(jnp.transpose(sg_b[0]), SG_W // SG_HEADS, axis=1)
    z1 = _tail0(x2, ctx2, mod[0], y_s5, rest, glu_w[0].astype(BF16), glu_b[0].reshape(1, S5_W),
                sg_ln_g[0].reshape(1, SG_W), sg_ln_b[0].reshape(1, SG_W), sg_w[0].astype(BF16),
                sg_bias, e_w_out[0].astype(BF16), ln_g[0].reshape(1, D), ln_b[0].reshape(1, D))

    cos_t, sin_t = _rope_tables()
    q, kv, g = _in1(z1, mod[1], o_w_in[0].astype(BF16), cos_t, sin_t)
    o = _attention(o_sink[0], q, kv)
    out = _tail1(z1, mod[1], o, g, o_w_out[0].astype(BF16), ln_g[1].reshape(1, D),
                 ln_b[1].reshape(1, D))
    return out.reshape(B, N_LAT, D)
```

```python
import functools
import math

import jax
import jax.numpy as jnp
from jax import lax
from jax.experimental import pallas as pl
from jax.experimental.pallas import tpu as pltpu

F32 = jnp.float32
BF16 = jnp.bfloat16
HIGHEST = lax.Precision.HIGHEST

D = 1024
B = 2
N_LAT = 8192
N_CTX = 256
DEPTH = 2
GRID_W = 64
S5_W = 512
S5_GC = 16
S5_G = 32
S5_P = 64
SG_W = 512
SG_HEADS = 8
SG_CHUNK = 128
N_HEADS = 16
N_KV = 4
HD = 64
WINDOW = 128
ATT_BLK = 128
ROPE_BASE = 10000.0
NEG_INF = -1e30
LN_EPS = 1e-5
ALPHA = (2 * DEPTH) ** 0.25
EVEN_IN = 2 * S5_W + 3 * SG_W
ODD_IN = 2 * N_HEADS * HD + 2 * N_KV * HD

LANES = 128
VMEM_LIMIT = 56 * 1024 * 1024

R_LAT = B * N_LAT
R_CTX = B * N_CTX
R_ALL = R_LAT + R_CTX
TN = 512
N_LAT_BLK = R_LAT // TN
N_BLK = R_ALL // TN
LAT_BLK_PER_BATCH = N_LAT // TN
CTX_MOD_ROW = B

S5_CHUNK = 8
S5_QT = S5_W // LANES
S5_GPT = LANES // S5_GC
S5_ST = S5_GPT * S5_P
N_CHUNK = R_ALL // S5_CHUNK
CHUNK_BLK = 528
N_CHUNK_BLK = N_CHUNK // CHUNK_BLK
LAT_CHUNKS = N_LAT // S5_CHUNK
CTX_CHUNKS = N_CTX // S5_CHUNK
CTX_CHUNK0 = R_LAT // S5_CHUNK


def _sigmoid(x):
    return 1.0 / (1.0 + jnp.exp(-x))


def _silu(x):
    return x * _sigmoid(x)


def _gelu_tanh(x):
    return 0.5 * x * (1.0 + jnp.tanh(math.sqrt(2.0 / math.pi) * (x + 0.044715 * (x * x * x))))


def _layer_norm(x, g, b):
    mu = jnp.mean(x, axis=-1, keepdims=True)
    xc = x - mu
    var = jnp.mean(xc * xc, axis=-1, keepdims=True)
    return xc * lax.rsqrt(var + LN_EPS) * g + b


def _block_mod_row(i):
    return jnp.where(i < N_LAT_BLK, i // LAT_BLK_PER_BATCH, CTX_MOD_ROW)


def _mod_kernel(cv_ref, w_ref, b_ref, o_ref):
    s = _silu(cv_ref[...])
    o_ref[0] = jnp.dot(s, w_ref[0], preferred_element_type=F32, precision=HIGHEST) + b_ref[0]


def _modulation(c, c_ctx, mod_w, mod_b):
    cv = jnp.concatenate([c, c_ctx[None], jnp.zeros((8 - B - 1, D), F32)], axis=0)
    tn = 1024
    return pl.pallas_call(
        _mod_kernel,
        grid=(DEPTH, 3 * D // tn),
        in_specs=[
            pl.BlockSpec((8, D), lambda l, j: (0, 0)),
            pl.BlockSpec((1, D, tn), lambda l, j: (l, 0, j)),
            pl.BlockSpec((1, 1, tn), lambda l, j: (l, 0, j)),
        ],
        out_specs=pl.BlockSpec((1, 8, tn), lambda l, j: (l, 0, j)),
        out_shape=jax.ShapeDtypeStruct((DEPTH, 8, 3 * D), F32),
        compiler_params=pltpu.CompilerParams(vmem_limit_bytes=VMEM_LIMIT),
        name="modulation",
    )(cv, mod_w, mod_b.reshape(DEPTH, 1, 3 * D))


def _in0_kernel(x_ref, ctx_ref, mod_ref, w_ref, xa_ref, rest_ref):
    i = pl.program_id(0)
    z = jnp.where(i < N_LAT_BLK, x_ref[...], ctx_ref[...])
    m = mod_ref[pl.ds(_block_mod_row(i), 1), :]
    h = z * (1.0 + m[:, D:2 * D]) + m[:, :D]
    p = jnp.dot(h.astype(BF16), w_ref[...], preferred_element_type=F32)
    xa_ref[...] = p[:, :S5_W]
    rest_ref[...] = p[:, S5_W:].astype(BF16)


def _in0(x2, ctx2, mod, w_in):
    return pl.pallas_call(
        _in0_kernel,
        grid=(N_BLK,),
        in_specs=[
            pl.BlockSpec((TN, D), lambda i: (jnp.minimum(i, N_LAT_BLK - 1), 0)),
            pl.BlockSpec((R_CTX, D), lambda i: (0, 0)),
            pl.BlockSpec((8, 3 * D), lambda i: (0, 0)),
            pl.BlockSpec((D, EVEN_IN), lambda i: (0, 0)),
        ],
        out_specs=[
            pl.BlockSpec((TN, S5_W), lambda i: (i, 0)),
            pl.BlockSpec((TN, EVEN_IN - S5_W), lambda i: (i, 0)),
        ],
        out_shape=[
            jax.ShapeDtypeStruct((R_ALL, S5_W), F32),
            jax.ShapeDtypeStruct((R_ALL, EVEN_IN - S5_W), BF16),
        ],
        compiler_params=pltpu.CompilerParams(vmem_limit_bytes=VMEM_LIMIT),
        name="in0",
    )(x2, ctx2, mod, w_in)


def _s5_prep_kernel(rows_ref, bt_ref, ct_ref, dsk_ref, t_ref, e_ref, ft_ref, alr_ref, ali_ref):
    L = S5_CHUNK
    tile = (LANES, S5_ST)
    same_group = (lax.broadcasted_iota(jnp.int32, tile, 0) // S5_GC
                  == lax.broadcasted_iota(jnp.int32, tile, 1) // S5_P)
    nt_dims = (((1,), (1,)), ((), ()))
    lag = []
    for d in range(2):
        lam_re = rows_ref[d, 0]
        lam_im = rows_ref[d, 1]
        dt = jnp.exp(rows_ref[d, 2])
        pw = []
        for k in range(L + 1):
            mag = jnp.exp(lam_re * dt * float(k))
            ang = lam_im * dt * float(k)
            pw.append((mag * jnp.cos(ang), mag * jnp.sin(ang)))
        alr_ref[d] = pw[L][0]
        ali_ref[d] = pw[L][1]
        den = lam_re * lam_re + lam_im * lam_im
        nr = pw[1][0] - 1.0
        ni = pw[1][1]
        f_re = (nr * lam_re + ni * lam_im) / den
        f_im = (ni * lam_re - nr * lam_im) / den
        b_re = jnp.where(same_group, bt_ref[0, d], 0.0)
        b_im = jnp.where(same_group, bt_ref[1, d], 0.0)
        bb_re = f_re * b_re - f_im * b_im
        bb_im = f_re * b_im + f_im * b_re
        c_re = jnp.where(same_group, ct_ref[0, d], 0.0)
        c_im = jnp.where(same_group, ct_ref[1, d], 0.0)
        c_cat = jnp.concatenate([c_re, -c_im], axis=-1)
        col_re = 2 * d * S5_ST
        col_im = (2 * d + 1) * S5_ST
        kd = []
        for k in range(L):
            x_re = pw[k][0] * bb_re - pw[k][1] * bb_im
            x_im = pw[k][0] * bb_im + pw[k][1] * bb_re
            s = (L - 1 - k) if d == 0 else k
            e_ref[0, s * LANES:(s + 1) * LANES, col_re:col_re + S5_ST] = x_re.astype(BF16)
            e_ref[0, s * LANES:(s + 1) * LANES, col_im:col_im + S5_ST] = x_im.astype(BF16)
            kd.append(lax.dot_general(jnp.concatenate([x_re, x_im], axis=-1), c_cat, nt_dims,
                                      precision=HIGHEST, preferred_element_type=F32))
        lag.append(kd)
        for t in range(L):
            k = (t + 1) if d == 0 else (L - t)
            w_re = c_re * pw[k][0] - c_im * pw[k][1]
            w_im = c_re * pw[k][1] + c_im * pw[k][0]
            ft_ref[0, t * LANES:(t + 1) * LANES, col_re:col_re + S5_ST] = w_re.astype(BF16)
            ft_ref[0, t * LANES:(t + 1) * LANES, col_im:col_im + S5_ST] = (-w_im).astype(BF16)

    sq = (LANES, LANES)
    on_diag = lax.broadcasted_iota(jnp.int32, sq, 0) == lax.broadcasted_iota(jnp.int32, sq, 1)
    diag = lag[0][0] + lag[1][0] + jnp.where(on_diag, dsk_ref[0], 0.0)
    for s in range(L):
        for t in range(L):
            blk = lag[0][t - s] if t > s else (lag[1][s - t] if s > t else diag)
            t_ref[0, s * LANES:(s + 1) * LANES, t * LANES:(t + 1) * LANES] = blk.astype(BF16)


def _s5_prep(lam_re, lam_im, log_dt, b_re, b_im, c_re, c_im, d_skip):
    L = S5_CHUNK
    n_st = S5_G * S5_P
    rows = jnp.stack([lam_re.reshape(2, 1, n_st), lam_im.reshape(2, 1, n_st),
                      jnp.repeat(log_dt, S5_P, axis=1).reshape(2, 1, n_st)], axis=1)
    rep = lambda a: jnp.tile(a.reshape(2, S5_W, S5_P), (1, 1, S5_GPT))
    bt = jnp.stack([rep(jnp.swapaxes(b_re, 2, 3)), rep(jnp.swapaxes(b_im, 2, 3))])
    ct = jnp.stack([rep(c_re), rep(c_im)])
    return pl.pallas_call(
        _s5_prep_kernel,
        grid=(S5_QT,),
        in_specs=[
            pl.BlockSpec((2, 3, 1, S5_ST), lambda q: (0, 0, 0, q)),
            pl.BlockSpec((2, 2, LANES, S5_ST), lambda q: (0, 0, q, 0)),
            pl.BlockSpec((2, 2, LANES, S5_ST), lambda q: (0, 0, q, 0)),
            pl.BlockSpec((1, 1, LANES), lambda q: (q, 0, 0)),
        ],
        out_specs=[
            pl.BlockSpec((1, L * LANES, L * LANES), lambda q: (q, 0, 0)),
            pl.BlockSpec((1, L * LANES, 4 * S5_ST), lambda q: (q, 0, 0)),
            pl.BlockSpec((1, L * LANES, 4 * S5_ST), lambda q: (q, 0, 0)),
            pl.BlockSpec((2, 1, S5_ST), lambda q: (q, 0, 0)),
            pl.BlockSpec((2, 1, S5_ST), lambda q: (q, 0, 0)),
        ],
        out_shape=[
            jax.ShapeDtypeStruct((S5_QT, L * LANES, L * LANES), BF16),
            jax.ShapeDtypeStruct((S5_QT, L * LANES, 4 * S5_ST), BF16),
            jax.ShapeDtypeStruct((S5_QT, L * LANES, 4 * S5_ST), BF16),
            jax.ShapeDtypeStruct((2 * S5_QT, 1, S5_ST), F32),
            jax.ShapeDtypeStruct((2 * S5_QT, 1, S5_ST), F32),
        ],
        compiler_params=pltpu.CompilerParams(vmem_limit_bytes=VMEM_LIMIT),
        name="s5_prep",
    )(rows, bt, ct, d_skip.reshape(S5_QT, 1, LANES))


def _chunk_rows(xa_ref):
    return jnp.concatenate(
        [xa_ref[pl.ds(t, CHUNK_BLK, stride=S5_CHUNK), :].astype(BF16) for t in range(S5_CHUNK)],
        axis=-1)


def _s5_states_kernel(xa_ref, e_ref, st_ref):
    st_ref[...] = jnp.dot(_chunk_rows(xa_ref), e_ref[0], preferred_element_type=F32)


def _s5_states(xa, e_mat):
    L = S5_CHUNK
    return pl.pallas_call(
        _s5_states_kernel,
        grid=(S5_QT, N_CHUNK_BLK),
        in_specs=[
            pl.BlockSpec((CHUNK_BLK * L, LANES), lambda q, i: (i, q)),
            pl.BlockSpec((1, L * LANES, 4 * S5_ST), lambda q, i: (q, 0, 0)),
        ],
        out_specs=pl.BlockSpec((CHUNK_BLK, 4 * S5_ST), lambda q, i: (i, q)),
        out_shape=jax.ShapeDtypeStruct((N_CHUNK, S5_QT * 4 * S5_ST), F32),
        compiler_params=pltpu.CompilerParams(vmem_limit_bytes=VMEM_LIMIT),
        name="s5_states",
    )(xa, e_mat)


def _s5_scan_kernel(loc_ref, ar_ref, ai_ref, out_ref):
    backward = pl.program_id(0) % 2
    ar = ar_ref[0]
    ai = ai_ref[0]

    def run(base, n, carry):
        def step(j, carry):
            new = []
            for b in range(B):
                cre, cim = carry[b]
                row = base[b] + jnp.where(backward == 1, n - 1 - j, j)
                x = loc_ref[pl.ds(row, 1), :]
                out_ref[pl.ds(row, 1), :] = jnp.concatenate([cre, cim], axis=-1)
                nre = ar * cre - ai * cim + x[:, :S5_ST]
                nim = ar * cim + ai * cre + x[:, S5_ST:]
                new.append((nre, nim))
            return tuple(new)
        return lax.fori_loop(0, n, step, carry)

    zero = jnp.zeros((1, S5_ST), F32)
    carry = tuple((zero, zero) for _ in range(B))
    carry = run([CTX_CHUNK0 + b * CTX_CHUNKS for b in range(B)], CTX_CHUNKS, carry)
    run([b * LAT_CHUNKS for b in range(B)], LAT_CHUNKS, carry)


def _s5_scan(st_loc, al_re, al_im):
    return pl.pallas_call(
        _s5_scan_kernel,
        grid=(2 * S5_QT,),
        in_specs=[
            pl.BlockSpec((N_CHUNK, 2 * S5_ST), lambda j: (0, j)),
            pl.BlockSpec((1, 1, S5_ST), lambda j: (j, 0, 0)),
            pl.BlockSpec((1, 1, S5_ST), lambda j: (j, 0, 0)),
        ],
        out_specs=pl.BlockSpec((N_CHUNK, 2 * S5_ST), lambda j: (0, j)),
        out_shape=jax.ShapeDtypeStruct((N_CHUNK, S5_QT * 4 * S5_ST), F32),
        compiler_params=pltpu.CompilerParams(vmem_limit_bytes=VMEM_LIMIT),
        name="s5_scan",
    )(st_loc, al_re, al_im)


def _s5_out_kernel(xa_ref, st_ref, t_ref, ft_ref, y_ref):
    y = (jnp.dot(_chunk_rows(xa_ref), t_ref[0], preferred_element_type=F32)
         + lax.dot_general(st_ref[...].astype(BF16), ft_ref[0], (((1,), (1,)), ((), ())),
                           preferred_element_type=F32))
    for t in range(S5_CHUNK):
        y_ref[pl.ds(t, CHUNK_BLK, stride=S5_CHUNK), :] = y[:, t * LANES:(t + 1) * LANES]


def _s5_out(xa, st_in, t_mat, ft_mat):
    L = S5_CHUNK
    return pl.pallas_call(
        _s5_out_kernel,
        grid=(S5_QT, N_CHUNK_BLK),
        in_specs=[
            pl.BlockSpec((CHUNK_BLK * L, LANES), lambda q, i: (i, q)),
            pl.BlockSpec((CHUNK_BLK, 4 * S5_ST), lambda q, i: (i, q)),
            pl.BlockSpec((1, L * LANES, L * LANES), lambda q, i: (q, 0, 0)),
            pl.BlockSpec((1, L * LANES, 4 * S5_ST), lambda q, i: (q, 0, 0)),
        ],
        out_specs=pl.BlockSpec((CHUNK_BLK * L, LANES), lambda q, i: (i, q)),
        out_shape=jax.ShapeDtypeStruct((R_ALL, S5_W), F32),
        compiler_params=pltpu.CompilerParams(vmem_limit_bytes=VMEM_LIMIT),
        name="s5_out",
    )(xa, st_in, t_mat, ft_mat)


def _tail0_kernel(x_ref, ctx_ref, mod_ref, ys_ref, rest_ref, gluw_ref, glub_ref, sg_g_ref,
                  sg_b_ref, sgw_ref, sgbias_ref, wout_ref, lng_ref, lnb_ref, o_ref):
    i = pl.program_id(0)
    z = jnp.where(i < N_LAT_BLK, x_ref[...], ctx_ref[...])
    gate = mod_ref[pl.ds(_block_mod_row(i), 1), 2 * D:3 * D]

    ga = rest_ref[:, 0:S5_W].astype(F32)
    u = rest_ref[:, S5_W:2 * S5_W].astype(F32)
    v = rest_ref[:, 2 * S5_W:3 * S5_W].astype(F32)
    gb = rest_ref[:, 3 * S5_W:4 * S5_W].astype(F32)

    ya = _gelu_tanh(ys_ref[...])
    glu = jnp.dot(ya.astype(BF16), gluw_ref[...], preferred_element_type=F32) + glub_ref[...]
    ya = ya * _sigmoid(glu) * _silu(ga)

    vn = _layer_norm(v, sg_g_ref[...], sg_b_ref[...]).astype(BF16)
    lane = lax.broadcasted_iota(jnp.int32, (SG_CHUNK, LANES), 1)
    first_head = lane < (SG_W // SG_HEADS)
    chunks = []
    for c in range(TN // SG_CHUNK):
        tiles = []
        for j in range(SG_W // LANES):
            vt = vn[c * SG_CHUNK:(c + 1) * SG_CHUNK, j * LANES:(j + 1) * LANES]
            r0 = jnp.dot(sgw_ref[2 * j], vt, preferred_element_type=F32)
            r1 = jnp.dot(sgw_ref[2 * j + 1], vt, preferred_element_type=F32)
            tiles.append(jnp.where(first_head, r0, r1))
        chunks.append(jnp.concatenate(tiles, axis=-1) + sgbias_ref[...])
    s = jnp.concatenate(chunks, axis=0)
    yb = u * s * _silu(gb)

    mix = jnp.concatenate([ya, yb], axis=-1).astype(BF16)
    y = jnp.dot(mix, wout_ref[...], preferred_element_type=F32) * gate
    o_ref[...] = _layer_norm(ALPHA * z + y, lng_ref[...], lnb_ref[...])


def _tail0(x2, ctx2, mod, y_s5, rest, glu_w, glu_b, sg_g, sg_b, sg_w, sg_bias, w_out, ln_g, ln_b):
    row = lambda n: pl.BlockSpec((1, n), lambda i: (0, 0))
    return pl.pallas_call(
        _tail0_kernel,
        grid=(N_BLK,),
        in_specs=[
            pl.BlockSpec((TN, D), lambda i: (jnp.minimum(i, N_LAT_BLK - 1), 0)),
            pl.BlockSpec((R_CTX, D), lambda i: (0, 0)),
            pl.BlockSpec((8, 3 * D), lambda i: (0, 0)),
            pl.BlockSpec((TN, S5_W), lambda i: (i, 0)),
            pl.BlockSpec((TN, EVEN_IN - S5_W), lambda i: (i, 0)),
            pl.BlockSpec((S5_W, S5_W), lambda i: (0, 0)),
            row(S5_W), row(SG_W), row(SG_W),
            pl.BlockSpec((SG_HEADS, SG_CHUNK, SG_CHUNK), lambda i: (0, 0, 0)),
            pl.BlockSpec((SG_CHUNK, SG_W), lambda i: (0, 0)),
            pl.BlockSpec((S5_W + SG_W, D), lambda i: (0, 0)),
            row(D), row(D),
        ],
        out_specs=pl.BlockSpec((TN, D), lambda i: (i, 0)),
        out_shape=jax.ShapeDtypeStruct((R_ALL, D), F32),
        compiler_params=pltpu.CompilerParams(vmem_limit_bytes=VMEM_LIMIT),
        name="tail0",
    )(x2, ctx2, mod, y_s5, rest, glu_w, glu_b, sg_g, sg_b, sg_w, sg_bias, w_out, ln_g, ln_b)


def _rope_tables():
    nf = HD // 4
    n_rows = N_LAT // GRID_W
    inv = ROPE_BASE ** (-jnp.arange(nf, dtype=F32) / nf)
    row_ang = jnp.arange(n_rows, dtype=F32)[:, None] * inv[None, :]
    col_ang = jnp.arange(GRID_W, dtype=F32)[:, None] * inv[None, :]
    by_row = lambda a: jnp.broadcast_to(a[:, None, :], (n_rows, GRID_W, nf))
    by_col = lambda a: jnp.broadcast_to(a[None, :, :], (n_rows, GRID_W, nf))
    cr, sr = by_row(jnp.cos(row_ang)), by_row(jnp.sin(row_ang))
    cc, sc = by_col(jnp.cos(col_ang)), by_col(jnp.sin(col_ang))
    reps = LANES // HD
    cos_t = jnp.concatenate([cr, cr, cc, cc] * reps, axis=-1).reshape(N_LAT, LANES)
    sin_t = jnp.concatenate([-sr, sr, -sc, sc] * reps, axis=-1).reshape(N_LAT, LANES)
    cos_t = jnp.concatenate([cos_t, jnp.ones((TN, LANES), F32)], axis=0)
    sin_t = jnp.concatenate([sin_t, jnp.zeros((TN, LANES), F32)], axis=0)
    return cos_t, sin_t


def _rope_tile(x, cos, sin, first_half):
    nf = HD // 4
    partner = jnp.where(first_half, pltpu.roll(x, LANES - nf, axis=1), pltpu.roll(x, nf, axis=1))
    return x * cos + partner * sin


def _in1_kernel(z_ref, mod_ref, w_ref, cos_ref, sin_ref, q_ref, kv_ref, g_ref):
    i = pl.program_id(0)
    m = mod_ref[pl.ds(_block_mod_row(i), 1), :]
    h = z_ref[...] * (1.0 + m[:, D:2 * D]) + m[:, :D]
    p = jnp.dot(h.astype(BF16), w_ref[...], preferred_element_type=F32)
    dq = N_HEADS * HD
    dkv = N_KV * HD
    cos = cos_ref[...]
    sin = sin_ref[...]
    lane = lax.broadcasted_iota(jnp.int32, (TN, LANES), 1)
    first_half = (lane % (HD // 2)) < (HD // 4)
    scale = HD ** -0.5
    for j in range(dq // LANES):
        q_ref[:, j * LANES:(j + 1) * LANES] = (
            _rope_tile(p[:, j * LANES:(j + 1) * LANES], cos, sin, first_half) * scale).astype(BF16)
    for j in range(dkv // LANES):
        kv_ref[:, j * LANES:(j + 1) * LANES] = _rope_tile(
            p[:, dq + j * LANES:dq + (j + 1) * LANES], cos, sin, first_half).astype(BF16)
    kv_ref[:, dkv:2 * dkv] = p[:, dq + dkv:dq + 2 * dkv].astype(BF16)
    g_ref[...] = p[:, dq + 2 * dkv:].astype(BF16)


def _in1(z1, mod, w_in, cos_t, sin_t):
    dq = N_HEADS * HD
    dkv = N_KV * HD
    tab = lambda i: (jnp.where(i < N_LAT_BLK, i % LAT_BLK_PER_BATCH, LAT_BLK_PER_BATCH), 0)
    return pl.pallas_call(
        _in1_kernel,
        grid=(N_BLK,),
        in_specs=[
            pl.BlockSpec((TN, D), lambda i: (i, 0)),
            pl.BlockSpec((8, 3 * D), lambda i: (0, 0)),
            pl.BlockSpec((D, ODD_IN), lambda i: (0, 0)),
            pl.BlockSpec((TN, LANES), tab),
            pl.BlockSpec((TN, LANES), tab),
        ],
        out_specs=[
            pl.BlockSpec((TN, dq), lambda i: (i, 0)),
            pl.BlockSpec((TN, 2 * dkv), lambda i: (i, 0)),
            pl.BlockSpec((TN, dq), lambda i: (i, 0)),
        ],
        out_shape=[
            jax.ShapeDtypeStruct((R_ALL, dq), BF16),
            jax.ShapeDtypeStruct((R_ALL, 2 * dkv), BF16),
            jax.ShapeDtypeStruct((R_ALL, dq), BF16),
        ],
        compiler_params=pltpu.CompilerParams(vmem_limit_bytes=VMEM_LIMIT),
        name="in1",
    )(z1, mod, w_in, cos_t, sin_t)


N_QBLK = N_LAT // ATT_BLK
GRP = N_HEADS // N_KV


def _attn_kernel(sink_ref, q_ref, kvp_ref, kvc_ref, kvn_ref, kvx_ref, o_ref):
    i = pl.program_id(1)
    dkv = N_KV * HD
    n_win = 3 * ATT_BLK
    n_keys = n_win + N_CTX
    kv = jnp.concatenate([kvp_ref[...], kvc_ref[...], kvn_ref[...], kvx_ref[...]], axis=0)

    qpos = lax.broadcasted_iota(jnp.int32, (ATT_BLK, n_keys), 0)
    col = lax.broadcasted_iota(jnp.int32, (ATT_BLK, n_keys), 1)
    rel = col - ATT_BLK - qpos
    lo = jnp.where(i == 0, ATT_BLK, 0)
    hi = jnp.where(i == N_QBLK - 1, 2 * ATT_BLK, n_win)
    ok = (col >= n_win) | ((jnp.abs(rel) <= WINDOW) & (col >= lo) & (col < hi))
    ok = jnp.concatenate([ok] * GRP, axis=0)
    rgrp = lax.broadcasted_iota(jnp.int32, (GRP * ATT_BLK, 1), 0) // ATT_BLK
    q = q_ref[...]
    outs = []

    for h in range(N_KV):
        kh = kv[:, h * HD:(h + 1) * HD]
        vh = kv[:, dkv + h * HD:dkv + (h + 1) * HD]
        qh = jnp.concatenate(
            [q[:, (h * GRP + g) * HD:(h * GRP + g + 1) * HD] for g in range(GRP)], axis=0)
        s = lax.dot_general(qh, kh, (((1,), (1,)), ((), ())), preferred_element_type=F32)
        s = jnp.where(ok, s, NEG_INF)
        sink = jnp.zeros((GRP * ATT_BLK, 1), F32)
        for g in range(GRP):
            sink = jnp.where(rgrp == g, sink_ref[h * GRP + g], sink)
        m = jnp.maximum(jnp.max(s, axis=-1, keepdims=True), sink)
        p = jnp.exp(s - m)
        den = jnp.sum(p, axis=-1, keepdims=True) + jnp.exp(sink - m)
        o = jnp.dot(p.astype(BF16), vh, preferred_element_type=F32) / den
        outs.extend(o[g * ATT_BLK:(g + 1) * ATT_BLK] for g in range(GRP))
    o_ref[...] = jnp.concatenate(outs, axis=-1).astype(BF16)


def _attention(sink, q, kv):
    dq = N_HEADS * HD
    dkv = N_KV * HD
    ctx_blk0 = R_LAT // N_CTX
    return pl.pallas_call(
        _attn_kernel,
        grid_spec=pltpu.PrefetchScalarGridSpec(
            num_scalar_prefetch=1,
            grid=(B, N_QBLK),
            in_specs=[
                pl.BlockSpec((ATT_BLK, dq), lambda b, i, s: (b * N_QBLK + i, 0)),
                pl.BlockSpec((ATT_BLK, 2 * dkv), lambda b, i, s: (b * N_QBLK + jnp.maximum(i - 1, 0), 0)),
                pl.BlockSpec((ATT_BLK, 2 * dkv), lambda b, i, s: (b * N_QBLK + i, 0)),
                pl.BlockSpec((ATT_BLK, 2 * dkv),
                             lambda b, i, s: (b * N_QBLK + jnp.minimum(i + 1, N_QBLK - 1), 0)),
                pl.BlockSpec((N_CTX, 2 * dkv), lambda b, i, s: (ctx_blk0 + b, 0)),
            ],
            out_specs=pl.BlockSpec((ATT_BLK, dq), lambda b, i, s: (b * N_QBLK + i, 0)),
        ),
        out_shape=jax.ShapeDtypeStruct((R_LAT, dq), BF16),
        compiler_params=pltpu.CompilerParams(vmem_limit_bytes=VMEM_LIMIT),
        name="attention",
    )(sink, q, kv, kv, kv, kv)


def _tail1_kernel(z_ref, mod_ref, o_ref_in, g_ref, wout_ref, lng_ref, lnb_ref, out_ref):
    i = pl.program_id(0)
    gate = mod_ref[pl.ds(_block_mod_row(i), 1), 2 * D:3 * D]
    g = g_ref[...].astype(F32)
    mix = (o_ref_in[...].astype(F32) * _silu(g)).astype(BF16)
    y = jnp.dot(mix, wout_ref[...], preferred_element_type=F32) * gate
    out_ref[...] = _layer_norm(ALPHA * z_ref[...] + y, lng_ref[...], lnb_ref[...])


def _tail1(z1, mod, o, g, w_out, ln_g, ln_b):
    row = lambda n: pl.BlockSpec((1, n), lambda i: (0, 0))
    return pl.pallas_call(
        _tail1_kernel,
        grid=(N_LAT_BLK,),
        in_specs=[
            pl.BlockSpec((TN, D), lambda i: (i, 0)),
            pl.BlockSpec((8, 3 * D), lambda i: (0, 0)),
            pl.BlockSpec((TN, D), lambda i: (i, 0)),
            pl.BlockSpec((TN, D), lambda i: (i, 0)),
            pl.BlockSpec((D, D), lambda i: (0, 0)),
            row(D), row(D),
        ],
        out_specs=pl.BlockSpec((TN, D), lambda i: (i, 0)),
        out_shape=jax.ShapeDtypeStruct((R_LAT, D), F32),
        compiler_params=pltpu.CompilerParams(vmem_limit_bytes=VMEM_LIMIT),
        name="tail1",
    )(z1, mod, o, g, w_out, ln_g, ln_b)


def kernel(x, c, ctx, c_ctx, mod_w, mod_b, ln_g, ln_b, e_w_in, e_w_out, s5_lam_re, s5_lam_im,
           s5_log_dt, s5_b_re, s5_b_im, s5_c_re, s5_c_im, s5_d, glu_w, glu_b, sg_ln_g, sg_ln_b,
           sg_w, sg_b, o_w_in, o_w_out, o_sink):
    x2 = x.reshape(R_LAT, D)
    ctx2 = ctx.reshape(R_CTX, D)
    mod = _modulation(c, c_ctx, mod_w, mod_b)

    xa, rest = _in0(x2, ctx2, mod[0], e_w_in[0].astype(BF16))
    t_mat, e_mat, ft_mat, al_re, al_im = _s5_prep(
        s5_lam_re[0], s5_lam_im[0], s5_log_dt[0], s5_b_re[0], s5_b_im[0], s5_c_re[0], s5_c_im[0],
        s5_d[0])
    st_loc = _s5_states(xa, e_mat)
    st_in = _s5_scan(st_loc, al_re, al_im)
    y_s5 = _s5_out(xa, st_in, t_mat, ft_mat)
    sg_bias = jnp.repeat(jnp.transpose(sg_b[0]), SG_W // SG_HEADS, axis=1)
    z1 = _tail0(x2, ctx2, mod[0], y_s5, rest, glu_w[0].astype(BF16), glu_b[0].reshape(1, S5_W),
                sg_ln_g[0].reshape(1, SG_W), sg_ln_b[0].reshape(1, SG_W), sg_w[0].astype(BF16),
                sg_bias, e_w_out[0].astype(BF16), ln_g[0].reshape(1, D), ln_b[0].reshape(1, D))

    cos_t, sin_t = _rope_tables()
    q, kv, g = _in1(z1, mod[1], o_w_in[0].astype(BF16), cos_t, sin_t)
    o = _attention(o_sink[0], q, kv)
    out = _tail1(z1, mod[1], o, g, o_w_out[0].astype(BF16), ln_g[1].reshape(1, D),
                 ln_b[1].reshape(1, D))
    return out.reshape(B, N_LAT, D)
```

```python
import functools
import math

import jax
import jax.numpy as jnp
from jax import lax
from jax.experimental import pallas as pl
from jax.experimental.pallas import tpu as pltpu

F32 = jnp.float32
BF16 = jnp.bfloat16
HIGHEST = lax.Precision.HIGHEST

D = 1024
B = 2
N_LAT = 8192
N_CTX = 256
DEPTH = 2
GRID_W = 64
S5_W = 512
S5_GC = 16
S5_G = 32
S5_P = 64
SG_W = 512
SG_HEADS = 8
SG_CHUNK = 128
N_HEADS = 16
N_KV = 4
HD = 64
WINDOW = 128
ATT_BLK = 128
ROPE_BASE = 10000.0
NEG_INF = -1e30
LN_EPS = 1e-5
ALPHA = (2 * DEPTH) ** 0.25
LOG2E = math.log2(math.e)
EVEN_IN = 2 * S5_W + 3 * SG_W
ODD_IN = 2 * N_HEADS * HD + 2 * N_KV * HD

LANES = 128
VMEM_LIMIT = 56 * 1024 * 1024

R_LAT = B * N_LAT
R_CTX = B * N_CTX
R_ALL = R_LAT + R_CTX
TN = 512
N_LAT_BLK = R_LAT // TN
N_BLK = R_ALL // TN
LAT_BLK_PER_BATCH = N_LAT // TN
CTX_MOD_ROW = B

S5_CHUNK = 8
S5_QT = S5_W // LANES
S5_GPT = LANES // S5_GC
S5_ST = S5_GPT * S5_P
LAT_CHUNKS = N_LAT // S5_CHUNK
CTX_CHUNKS = N_CTX // S5_CHUNK
BATCH_CHUNKS = CTX_CHUNKS + LAT_CHUNKS
N_CHUNK = B * BATCH_CHUNKS
N_SEG = 8
SEG_CHUNKS = BATCH_CHUNKS // N_SEG
BWD_RESET_STEP = LAT_CHUNKS - (N_SEG - 1) * SEG_CHUNKS


def _sigmoid(x):
    return 1.0 / (1.0 + jnp.exp(-x))


def _silu(x):
    return x * _sigmoid(x)


def _gelu_tanh(x):
    return 0.5 * x * (1.0 + jnp.tanh(math.sqrt(2.0 / math.pi) * (x + 0.044715 * (x * x * x))))


def _layer_norm(x, g, b):
    mu = jnp.mean(x, axis=-1, keepdims=True)
    xc = x - mu
    var = jnp.mean(xc * xc, axis=-1, keepdims=True)
    return xc * lax.rsqrt(var + LN_EPS) * g + b


def _block_mod_row(i):
    return jnp.where(i < N_LAT_BLK, i // LAT_BLK_PER_BATCH, CTX_MOD_ROW)


def _mod_kernel(cv_ref, w_ref, b_ref, o_ref):
    s = _silu(cv_ref[...])
    o_ref[0] = jnp.dot(s, w_ref[0], preferred_element_type=F32, precision=HIGHEST) + b_ref[0]


def _modulation(c, c_ctx, mod_w, mod_b):
    cv = jnp.concatenate([c, c_ctx[None], jnp.zeros((8 - B - 1, D), F32)], axis=0)
    tn = 1024
    return pl.pallas_call(
        _mod_kernel,
        grid=(DEPTH, 3 * D // tn),
        in_specs=[
            pl.BlockSpec((8, D), lambda l, j: (0, 0)),
            pl.BlockSpec((1, D, tn), lambda l, j: (l, 0, j)),
            pl.BlockSpec((1, 1, tn), lambda l, j: (l, 0, j)),
        ],
        out_specs=pl.BlockSpec((1, 8, tn), lambda l, j: (l, 0, j)),
        out_shape=jax.ShapeDtypeStruct((DEPTH, 8, 3 * D), F32),
        compiler_params=pltpu.CompilerParams(vmem_limit_bytes=VMEM_LIMIT),
        name="modulation",
    )(cv, mod_w, mod_b.reshape(DEPTH, 1, 3 * D))


def _in0_kernel(x_ref, ctx_ref, mod_ref, w_ref, xa_ref, rest_ref):
    i = pl.program_id(0)
    z = jnp.where(i < N_LAT_BLK, x_ref[...], ctx_ref[...])
    m = mod_ref[pl.ds(_block_mod_row(i), 1), :]
    h = z * (1.0 + m[:, D:2 * D]) + m[:, :D]
    p = jnp.dot(h.astype(BF16), w_ref[...], preferred_element_type=F32)
    xa_ref[...] = p[:, :S5_W]
    rest_ref[...] = p[:, S5_W:].astype(BF16)


def _in0(x2, ctx2, mod, w_in):
    return pl.pallas_call(
        _in0_kernel,
        grid=(N_BLK,),
        in_specs=[
            pl.BlockSpec((TN, D), lambda i: (jnp.minimum(i, N_LAT_BLK - 1), 0)),
            pl.BlockSpec((R_CTX, D), lambda i: (0, 0)),
            pl.BlockSpec((8, 3 * D), lambda i: (0, 0)),
            pl.BlockSpec((D, EVEN_IN), lambda i: (0, 0)),
        ],
        out_specs=[
            pl.BlockSpec((TN, S5_W), lambda i: (i, 0)),
            pl.BlockSpec((TN, EVEN_IN - S5_W), lambda i: (i, 0)),
        ],
        out_shape=[
            jax.ShapeDtypeStruct((R_ALL, S5_W), F32),
            jax.ShapeDtypeStruct((R_ALL, EVEN_IN - S5_W), BF16),
        ],
        compiler_params=pltpu.CompilerParams(vmem_limit_bytes=VMEM_LIMIT),
        name="in0",
    )(x2, ctx2, mod, w_in)


def _s5_prep_kernel(rows_ref, bt_ref, ct_ref, dsk_ref, t_ref, e_ref, ft_ref, alr_ref, ali_ref):
    L = S5_CHUNK
    tile = (LANES, S5_ST)
    same_group = (lax.broadcasted_iota(jnp.int32, tile, 0) // S5_GC
                  == lax.broadcasted_iota(jnp.int32, tile, 1) // S5_P)
    nt_dims = (((1,), (1,)), ((), ()))
    lag = []
    for d in range(2):
        lam_re = rows_ref[d, 0]
        lam_im = rows_ref[d, 1]
        dt = jnp.exp(rows_ref[d, 2])
        pw = []
        for k in range(L + 1):
            mag = jnp.exp(lam_re * dt * float(k))
            ang = lam_im * dt * float(k)
            pw.append((mag * jnp.cos(ang), mag * jnp.sin(ang)))
        alr_ref[d] = pw[L][0]
        ali_ref[d] = pw[L][1]
        den = lam_re * lam_re + lam_im * lam_im
        nr = pw[1][0] - 1.0
        ni = pw[1][1]
        f_re = (nr * lam_re + ni * lam_im) / den
        f_im = (ni * lam_re - nr * lam_im) / den
        b_re = jnp.where(same_group, bt_ref[0, d], 0.0)
        b_im = jnp.where(same_group, bt_ref[1, d], 0.0)
        bb_re = f_re * b_re - f_im * b_im
        bb_im = f_re * b_im + f_im * b_re
        c_re = jnp.where(same_group, ct_ref[0, d], 0.0)
        c_im = jnp.where(same_group, ct_ref[1, d], 0.0)
        c_cat = jnp.concatenate([c_re, -c_im], axis=-1)
        col_re = 2 * d * S5_ST
        col_im = (2 * d + 1) * S5_ST
        kd = []
        for k in range(L):
            x_re = pw[k][0] * bb_re - pw[k][1] * bb_im
            x_im = pw[k][0] * bb_im + pw[k][1] * bb_re
            s = (L - 1 - k) if d == 0 else k
            e_ref[0, s * LANES:(s + 1) * LANES, col_re:col_re + S5_ST] = x_re.astype(BF16)
            e_ref[0, s * LANES:(s + 1) * LANES, col_im:col_im + S5_ST] = x_im.astype(BF16)
            kd.append(lax.dot_general(jnp.concatenate([x_re, x_im], axis=-1), c_cat, nt_dims,
                                      precision=HIGHEST, preferred_element_type=F32))
        lag.append(kd)
        for t in range(L):
            k = (t + 1) if d == 0 else (L - t)
            w_re = c_re * pw[k][0] - c_im * pw[k][1]
            w_im = c_re * pw[k][1] + c_im * pw[k][0]
            ft_ref[0, t * LANES:(t + 1) * LANES, col_re:col_re + S5_ST] = w_re.astype(BF16)
            ft_ref[0, t * LANES:(t + 1) * LANES, col_im:col_im + S5_ST] = (-w_im).astype(BF16)

    sq = (LANES, LANES)
    on_diag = lax.broadcasted_iota(jnp.int32, sq, 0) == lax.broadcasted_iota(jnp.int32, sq, 1)
    diag = lag[0][0] + lag[1][0] + jnp.where(on_diag, dsk_ref[0], 0.0)
    for s in range(L):
        for t in range(L):
            blk = lag[0][t - s] if t > s else (lag[1][s - t] if s > t else diag)
            t_ref[0, s * LANES:(s + 1) * LANES, t * LANES:(t + 1) * LANES] = blk.astype(BF16)


def _s5_prep(lam_re, lam_im, log_dt, b_re, b_im, c_re, c_im, d_skip):
    L = S5_CHUNK
    n_st = S5_G * S5_P
    rows = jnp.stack([lam_re.reshape(2, 1, n_st), lam_im.reshape(2, 1, n_st),
                      jnp.repeat(log_dt, S5_P, axis=1).reshape(2, 1, n_st)], axis=1)
    rep = lambda a: jnp.tile(a.reshape(2, S5_W, S5_P), (1, 1, S5_GPT))
    bt = jnp.stack([rep(jnp.swapaxes(b_re, 2, 3)), rep(jnp.swapaxes(b_im, 2, 3))])
    ct = jnp.stack([rep(c_re), rep(c_im)])
    return pl.pallas_call(
        _s5_prep_kernel,
        grid=(S5_QT,),
        in_specs=[
            pl.BlockSpec((2, 3, 1, S5_ST), lambda q: (0, 0, 0, q)),
            pl.BlockSpec((2, 2, LANES, S5_ST), lambda q: (0, 0, q, 0)),
            pl.BlockSpec((2, 2, LANES, S5_ST), lambda q: (0, 0, q, 0)),
            pl.BlockSpec((1, 1, LANES), lambda q: (q, 0, 0)),
        ],
        out_specs=[
            pl.BlockSpec((1, L * LANES, L * LANES), lambda q: (q, 0, 0)),
            pl.BlockSpec((1, L * LANES, 4 * S5_ST), lambda q: (q, 0, 0)),
            pl.BlockSpec((1, L * LANES, 4 * S5_ST), lambda q: (q, 0, 0)),
            pl.BlockSpec((2, 1, S5_ST), lambda q: (q, 0, 0)),
            pl.BlockSpec((2, 1, S5_ST), lambda q: (q, 0, 0)),
        ],
        out_shape=[
            jax.ShapeDtypeStruct((S5_QT, L * LANES, L * LANES), BF16),
            jax.ShapeDtypeStruct((S5_QT, L * LANES, 4 * S5_ST), BF16),
            jax.ShapeDtypeStruct((S5_QT, L * LANES, 4 * S5_ST), BF16),
            jax.ShapeDtypeStruct((2 * S5_QT, 1, S5_ST), F32),
            jax.ShapeDtypeStruct((2 * S5_QT, 1, S5_ST), F32),
        ],
        compiler_params=pltpu.CompilerParams(vmem_limit_bytes=VMEM_LIMIT),
        name="s5_prep",
    )(rows, bt, ct, d_skip.reshape(S5_QT, 1, LANES))


def _chunk_rows(xa_ref, n_chunks):
    return jnp.concatenate(
        [xa_ref[pl.ds(t, n_chunks, stride=S5_CHUNK), :].astype(BF16) for t in range(S5_CHUNK)],
        axis=-1)


def _batch_chunk_rows(xl_ref, xc_ref):
    return jnp.concatenate(
        [_chunk_rows(xc_ref, CTX_CHUNKS), _chunk_rows(xl_ref, LAT_CHUNKS)], axis=0)


def _xa_specs(index_of):
    ctx_blk0 = R_LAT // N_CTX
    return [
        pl.BlockSpec((N_LAT, LANES), lambda *g: (index_of(*g)[1], index_of(*g)[0])),
        pl.BlockSpec((N_CTX, LANES), lambda *g: (ctx_blk0 + index_of(*g)[1], index_of(*g)[0])),
    ]


N_SLAB = S5_ST // LANES
ST_SHAPE = (S5_QT * 2 * 2 * N_SLAB, N_CHUNK, LANES)
ST_SPEC = pl.BlockSpec((2 * N_SLAB, BATCH_CHUNKS, LANES), lambda q, b, d: (2 * q + d, b, 0))


def _s5_states_kernel(xl_ref, xc_ref, e_ref, st_ref):
    st = jnp.dot(_batch_chunk_rows(xl_ref, xc_ref), e_ref[0], preferred_element_type=F32)
    for k in range(2 * N_SLAB):
        st_ref[k] = st[:, k * LANES:(k + 1) * LANES]


def _s5_states(xa, e_mat):
    L = S5_CHUNK
    return pl.pallas_call(
        _s5_states_kernel,
        grid=(S5_QT, B, 2),
        in_specs=_xa_specs(lambda q, b, d: (q, b)) + [
            pl.BlockSpec((1, L * LANES, 2 * S5_ST), lambda q, b, d: (q, 0, d)),
        ],
        out_specs=ST_SPEC,
        out_shape=jax.ShapeDtypeStruct(ST_SHAPE, F32),
        compiler_params=pltpu.CompilerParams(vmem_limit_bytes=VMEM_LIMIT),
        name="s5_states",
    )(xa, xa, e_mat)


def _cmul(ar, ai, br, bi):
    return ar * br - ai * bi, ar * bi + ai * br


def _cpow(ar, ai, n):
    res = None
    while n:
        if n & 1:
            res = (ar, ai) if res is None else _cmul(res[0], res[1], ar, ai)
        n >>= 1
        if n:
            ar, ai = _cmul(ar, ai, ar, ai)
    return res


def _s5_scan_kernel(loc_ref, ar_ref, ai_ref, out_ref):
    chains = [(b, k) for b in range(B) for k in range(N_SLAB)]

    def run(backward):
        a_row = [(ar_ref[0][:, k * LANES:(k + 1) * LANES], ai_ref[0][:, k * LANES:(k + 1) * LANES])
                 for k in range(N_SLAB)]
        a_tile = [(jnp.broadcast_to(r, (N_SEG, LANES)), jnp.broadcast_to(i, (N_SEG, LANES)))
                  for r, i in a_row]

        def sweep(lo, hi, carry, store):
            def step(i, carry):
                off = (SEG_CHUNKS - 1 - i) if backward else i
                new = []
                for (b, k), (cr, ci) in zip(chains, carry):
                    rows = pl.ds(b * BATCH_CHUNKS + off, N_SEG, stride=SEG_CHUNKS)
                    xr = loc_ref[k, rows, :]
                    xi = loc_ref[N_SLAB + k, rows, :]
                    if store:
                        out_ref[k, rows, :] = cr
                        out_ref[N_SLAB + k, rows, :] = ci
                    nr, ni = _cmul(a_tile[k][0], a_tile[k][1], cr, ci)
                    new.append((nr + xr, ni + xi))
                return tuple(new)
            return lax.fori_loop(lo, hi, step, carry)

        def full(carry, store):
            if not backward:
                return sweep(0, SEG_CHUNKS, carry, store)
            carry = sweep(0, BWD_RESET_STEP, carry, store)
            first = lax.broadcasted_iota(jnp.int32, (N_SEG, LANES), 0) == 0
            carry = tuple((jnp.where(first, 0.0, cr), jnp.where(first, 0.0, ci)) for cr, ci in carry)
            return sweep(BWD_RESET_STEP, SEG_CHUNKS, carry, store)

        zero_tile = jnp.zeros((N_SEG, LANES), F32)
        ends = full(tuple((zero_tile, zero_tile) for _ in chains), False)

        starts = []
        for (b, k), (er, ei) in zip(chains, ends):
            pr, pi = _cpow(a_row[k][0], a_row[k][1], SEG_CHUNKS)
            if backward:
                order = range(N_SEG - 1, -1, -1)
                g = (er[0:1], ei[0:1])
            else:
                order = range(N_SEG)
                g = (jnp.zeros((1, LANES), F32), jnp.zeros((1, LANES), F32))
            rows_r = [None] * N_SEG
            rows_i = [None] * N_SEG
            for j in order:
                rows_r[j], rows_i[j] = g
                nr, ni = _cmul(pr, pi, g[0], g[1])
                g = (nr + er[j:j + 1], ni + ei[j:j + 1])
            starts.append((jnp.concatenate(rows_r, axis=0), jnp.concatenate(rows_i, axis=0)))
        full(tuple(starts), True)

    backward = pl.program_id(0) % 2
    pl.when(backward == 0)(lambda: run(False))
    pl.when(backward == 1)(lambda: run(True))


def _s5_scan(st_loc, al_re, al_im):
    slabs = pl.BlockSpec((2 * N_SLAB, N_CHUNK, LANES), lambda j: (j, 0, 0))
    return pl.pallas_call(
        _s5_scan_kernel,
        grid=(2 * S5_QT,),
        in_specs=[
            slabs,
            pl.BlockSpec((1, 1, S5_ST), lambda j: (j, 0, 0)),
            pl.BlockSpec((1, 1, S5_ST), lambda j: (j, 0, 0)),
        ],
        out_specs=slabs,
        out_shape=jax.ShapeDtypeStruct(ST_SHAPE, F32),
        compiler_params=pltpu.CompilerParams(vmem_limit_bytes=VMEM_LIMIT),
        name="s5_scan",
    )(st_loc, al_re, al_im)


def _s5_out_kernel(xl_ref, xc_ref, st_ref, t_ref, ft_ref, yl_ref, yc_ref):
    d = pl.program_id(2)
    st = jnp.concatenate([st_ref[k] for k in range(2 * N_SLAB)], axis=-1).astype(BF16)
    y = lax.dot_general(st, ft_ref[0], (((1,), (1,)), ((), ())), preferred_element_type=F32)

    def tiles(t):
        return (pl.ds(t, CTX_CHUNKS, stride=S5_CHUNK), pl.ds(t, LAT_CHUNKS, stride=S5_CHUNK),
                slice(t * LANES, (t + 1) * LANES))

    @pl.when(d == 0)
    def _():
        y0 = y + jnp.dot(_batch_chunk_rows(xl_ref, xc_ref), t_ref[0], preferred_element_type=F32)
        for t in range(S5_CHUNK):
            rc, rl, cols = tiles(t)
            yc_ref[rc, :] = y0[:CTX_CHUNKS, cols]
            yl_ref[rl, :] = y0[CTX_CHUNKS:, cols]

    @pl.when(d == 1)
    def _():
        for t in range(S5_CHUNK):
            rc, rl, cols = tiles(t)
            yc_ref[rc, :] += y[:CTX_CHUNKS, cols]
            yl_ref[rl, :] += y[CTX_CHUNKS:, cols]


def _s5_out(xa, st_in, t_mat, ft_mat):
    L = S5_CHUNK
    return pl.pallas_call(
        _s5_out_kernel,
        grid=(S5_QT, B, 2),
        in_specs=_xa_specs(lambda q, b, d: (q, b)) + [
            ST_SPEC,
            pl.BlockSpec((1, L * LANES, L * LANES), lambda q, b, d: (q, 0, 0)),
            pl.BlockSpec((1, L * LANES, 2 * S5_ST), lambda q, b, d: (q, 0, d)),
        ],
        out_specs=[
            pl.BlockSpec((N_LAT, LANES), lambda q, b, d: (b, q)),
            pl.BlockSpec((N_CTX, LANES), lambda q, b, d: (b, q)),
        ],
        out_shape=[
            jax.ShapeDtypeStruct((R_LAT, S5_W), F32),
            jax.ShapeDtypeStruct((R_CTX, S5_W), F32),
        ],
        compiler_params=pltpu.CompilerParams(vmem_limit_bytes=VMEM_LIMIT),
        name="s5_out",
    )(xa, xa, st_in, t_mat, ft_mat)


def _tail0_kernel(x_ref, ctx_ref, mod_ref, ysl_ref, ysc_ref, rest_ref, gluw_ref, glub_ref,
                  sg_g_ref, sg_b_ref, sgw_ref, sgbias_ref, wout_ref, lng_ref, lnb_ref, o_ref):
    i = pl.program_id(0)
    z = jnp.where(i < N_LAT_BLK, x_ref[...], ctx_ref[...])
    ys = jnp.where(i < N_LAT_BLK, ysl_ref[...], ysc_ref[...])
    gate = mod_ref[pl.ds(_block_mod_row(i), 1), 2 * D:3 * D]

    ga = rest_ref[:, 0:S5_W].astype(F32)
    u = rest_ref[:, S5_W:2 * S5_W].astype(F32)
    v = rest_ref[:, 2 * S5_W:3 * S5_W].astype(F32)
    gb = rest_ref[:, 3 * S5_W:4 * S5_W].astype(F32)

    ya = _gelu_tanh(ys)
    glu = jnp.dot(ya.astype(BF16), gluw_ref[...], preferred_element_type=F32) + glub_ref[...]
    ya = ya * _sigmoid(glu) * _silu(ga)

    vn = _layer_norm(v, sg_g_ref[...], sg_b_ref[...]).astype(BF16)
    lane = lax.broadcasted_iota(jnp.int32, (SG_CHUNK, LANES), 1)
    first_head = lane < (SG_W // SG_HEADS)
    chunks = []
    for c in range(TN // SG_CHUNK):
        tiles = []
        for j in range(SG_W // LANES):
            vt = vn[c * SG_CHUNK:(c + 1) * SG_CHUNK, j * LANES:(j + 1) * LANES]
            r0 = jnp.dot(sgw_ref[2 * j], vt, preferred_element_type=F32)
            r1 = jnp.dot(sgw_ref[2 * j + 1], vt, preferred_element_type=F32)
            tiles.append(jnp.where(first_head, r0, r1))
        chunks.append(jnp.concatenate(tiles, axis=-1) + sgbias_ref[...])
    s = jnp.concatenate(chunks, axis=0)
    yb = u * s * _silu(gb)

    mix = jnp.concatenate([ya, yb], axis=-1).astype(BF16)
    y = jnp.dot(mix, wout_ref[...], preferred_element_type=F32) * gate
    o_ref[...] = _layer_norm(ALPHA * z + y, lng_ref[...], lnb_ref[...])


def _tail0(x2, ctx2, mod, ys_lat, ys_ctx, rest, glu_w, glu_b, sg_g, sg_b, sg_w, sg_bias, w_out,
           ln_g, ln_b):
    row = lambda n: pl.BlockSpec((1, n), lambda i: (0, 0))
    return pl.pallas_call(
        _tail0_kernel,
        grid=(N_BLK,),
        in_specs=[
            pl.BlockSpec((TN, D), lambda i: (jnp.minimum(i, N_LAT_BLK - 1), 0)),
            pl.BlockSpec((R_CTX, D), lambda i: (0, 0)),
            pl.BlockSpec((8, 3 * D), lambda i: (0, 0)),
            pl.BlockSpec((TN, S5_W), lambda i: (jnp.minimum(i, N_LAT_BLK - 1), 0)),
            pl.BlockSpec((R_CTX, S5_W), lambda i: (0, 0)),
            pl.BlockSpec((TN, EVEN_IN - S5_W), lambda i: (i, 0)),
            pl.BlockSpec((S5_W, S5_W), lambda i: (0, 0)),
            row(S5_W), row(SG_W), row(SG_W),
            pl.BlockSpec((SG_HEADS, SG_CHUNK, SG_CHUNK), lambda i: (0, 0, 0)),
            pl.BlockSpec((SG_CHUNK, SG_W), lambda i: (0, 0)),
            pl.BlockSpec((S5_W + SG_W, D), lambda i: (0, 0)),
            row(D), row(D),
        ],
        out_specs=pl.BlockSpec((TN, D), lambda i: (i, 0)),
        out_shape=jax.ShapeDtypeStruct((R_ALL, D), F32),
        compiler_params=pltpu.CompilerParams(vmem_limit_bytes=VMEM_LIMIT),
        name="tail0",
    )(x2, ctx2, mod, ys_lat, ys_ctx, rest, glu_w, glu_b, sg_g, sg_b, sg_w, sg_bias, w_out, ln_g,
      ln_b)


def _rope_tables():
    nf = HD // 4
    n_rows = N_LAT // GRID_W
    lane = jnp.arange(LANES)
    inv = ROPE_BASE ** (-(lane % nf).astype(F32) / nf)
    by_row = ((lane % HD) // (HD // 2) == 0)[None, :]
    sign = jnp.where((lane % (HD // 2)) < nf, -1.0, 1.0)[None, :]
    row_ang = jnp.arange(n_rows, dtype=F32)[:, None] * inv[None, :]
    col_ang = jnp.arange(GRID_W, dtype=F32)[:, None] * inv[None, :]
    zero = jnp.zeros((), F32)
    cos_t = (jnp.where(by_row, jnp.cos(row_ang), zero)[:, None, :]
             + jnp.where(by_row, zero, jnp.cos(col_ang))[None, :, :]).reshape(N_LAT, LANES)
    sin_t = (jnp.where(by_row, sign * jnp.sin(row_ang), zero)[:, None, :]
             + jnp.where(by_row, zero, sign * jnp.sin(col_ang))[None, :, :]).reshape(N_LAT, LANES)
    return cos_t, sin_t


def _rope_tile(x, cos, sin, first_half):
    nf = HD // 4
    partner = jnp.where(first_half, pltpu.roll(x, LANES - nf, axis=1), pltpu.roll(x, nf, axis=1))
    return x * cos + partner * sin


def _in1_kernel(z_ref, mod_ref, w_ref, cos_ref, sin_ref, q_ref, kv_ref, g_ref):
    i = pl.program_id(0)
    m = mod_ref[pl.ds(_block_mod_row(i), 1), :]
    h = z_ref[...] * (1.0 + m[:, D:2 * D]) + m[:, :D]
    p = jnp.dot(h.astype(BF16), w_ref[...], preferred_element_type=F32)
    dq = N_HEADS * HD
    dkv = N_KV * HD
    cos = jnp.where(i < N_LAT_BLK, cos_ref[...], 1.0)
    sin = jnp.where(i < N_LAT_BLK, sin_ref[...], 0.0)
    lane = lax.broadcasted_iota(jnp.int32, (TN, LANES), 1)
    first_half = (lane % (HD // 2)) < (HD // 4)
    scale = HD ** -0.5 * LOG2E
    for j in range(dq // LANES):
        q_ref[:, j * LANES:(j + 1) * LANES] = (
            _rope_tile(p[:, j * LANES:(j + 1) * LANES], cos, sin, first_half) * scale).astype(BF16)
    for j in range(dkv // LANES):
        kv_ref[:, j * LANES:(j + 1) * LANES] = _rope_tile(
            p[:, dq + j * LANES:dq + (j + 1) * LANES], cos, sin, first_half).astype(BF16)
    kv_ref[:, dkv:2 * dkv] = p[:, dq + dkv:dq + 2 * dkv].astype(BF16)
    g_ref[...] = p[:, dq + 2 * dkv:].astype(BF16)


def _in1(z1, mod, w_in, cos_t, sin_t):
    dq = N_HEADS * HD
    dkv = N_KV * HD
    tab = lambda i: (i % LAT_BLK_PER_BATCH, 0)
    return pl.pallas_call(
        _in1_kernel,
        grid=(N_BLK,),
        in_specs=[
            pl.BlockSpec((TN, D), lambda i: (i, 0)),
            pl.BlockSpec((8, 3 * D), lambda i: (0, 0)),
            pl.BlockSpec((D, ODD_IN), lambda i: (0, 0)),
            pl.BlockSpec((TN, LANES), tab),
            pl.BlockSpec((TN, LANES), tab),
        ],
        out_specs=[
            pl.BlockSpec((TN, dq), lambda i: (i, 0)),
            pl.BlockSpec((TN, 2 * dkv), lambda i: (i, 0)),
            pl.BlockSpec((TN, dq), lambda i: (i, 0)),
        ],
        out_shape=[
            jax.ShapeDtypeStruct((R_ALL, dq), BF16),
            jax.ShapeDtypeStruct((R_ALL, 2 * dkv), BF16),
            jax.ShapeDtypeStruct((R_ALL, dq), BF16),
        ],
        compiler_params=pltpu.CompilerParams(vmem_limit_bytes=VMEM_LIMIT),
        name="in1",
    )(z1, mod, w_in, cos_t, sin_t)


N_QBLK = N_LAT // ATT_BLK
GRP = N_HEADS // N_KV


def _attn_kernel(sink_ref, q_ref, kvp_ref, kvc_ref, kvn_ref, kvx_ref, o_ref):
    i = pl.program_id(1)
    dkv = N_KV * HD
    n_win = 3 * ATT_BLK
    n_keys = n_win + N_CTX
    kv = jnp.concatenate([kvp_ref[...], kvc_ref[...], kvn_ref[...], kvx_ref[...]], axis=0)

    qpos = lax.broadcasted_iota(jnp.int32, (ATT_BLK, ATT_BLK), 0)
    kpos = lax.broadcasted_iota(jnp.int32, (ATT_BLK, ATT_BLK), 1)
    bias_prev = jnp.where((kpos >= qpos) & (i > 0), 0.0, NEG_INF)
    bias_next = jnp.where((kpos <= qpos) & (i < N_QBLK - 1), 0.0, NEG_INF)
    bias_prev = jnp.concatenate([bias_prev] * GRP, axis=0)
    bias_next = jnp.concatenate([bias_next] * GRP, axis=0)
    rgrp = lax.broadcasted_iota(jnp.int32, (GRP * ATT_BLK, 1), 0) // ATT_BLK
    ones = jnp.ones((n_keys, HD), BF16)
    q = q_ref[...]
    outs = []

    for h in range(N_KV):
        kh = kv[:, h * HD:(h + 1) * HD]
        vh = jnp.concatenate([kv[:, dkv + h * HD:dkv + (h + 1) * HD], ones], axis=-1)
        qh = jnp.concatenate(
            [q[:, (h * GRP + g) * HD:(h * GRP + g + 1) * HD] for g in range(GRP)], axis=0)
        s = lax.dot_general(qh, kh, (((1,), (1,)), ((), ())), preferred_element_type=F32)
        s = jnp.concatenate([s[:, :ATT_BLK] + bias_prev, s[:, ATT_BLK:2 * ATT_BLK],
                             s[:, 2 * ATT_BLK:n_win] + bias_next, s[:, n_win:]], axis=-1)
        sink = jnp.zeros((GRP * ATT_BLK, 1), F32)
        for g in range(GRP):
            sink = jnp.where(rgrp == g, sink_ref[h * GRP + g] * LOG2E, sink)
        m = jnp.maximum(jnp.max(s, axis=-1, keepdims=True), sink)
        p = jnp.exp2(s - m).astype(BF16)
        ov = jnp.dot(p, vh, preferred_element_type=F32)
        den = ov[:, HD:HD + 1] + jnp.exp2(sink - m)
        o = ov[:, :HD] / den
        outs.extend(o[g * ATT_BLK:(g + 1) * ATT_BLK] for g in range(GRP))
    o_ref[...] = jnp.concatenate(outs, axis=-1).astype(BF16)


def _attention(sink, q, kv):
    dq = N_HEADS * HD
    dkv = N_KV * HD
    ctx_blk0 = R_LAT // N_CTX
    return pl.pallas_call(
        _attn_kernel,
        grid_spec=pltpu.PrefetchScalarGridSpec(
            num_scalar_prefetch=1,
            grid=(B, N_QBLK),
            in_specs=[
                pl.BlockSpec((ATT_BLK, dq), lambda b, i, s: (b * N_QBLK + i, 0)),
                pl.BlockSpec((ATT_BLK, 2 * dkv), lambda b, i, s: (b * N_QBLK + jnp.maximum(i - 1, 0), 0)),
                pl.BlockSpec((ATT_BLK, 2 * dkv), lambda b, i, s: (b * N_QBLK + i, 0)),
                pl.BlockSpec((ATT_BLK, 2 * dkv),
                             lambda b, i, s: (b * N_QBLK + jnp.minimum(i + 1, N_QBLK - 1), 0)),
                pl.BlockSpec((N_CTX, 2 * dkv), lambda b, i, s: (ctx_blk0 + b, 0)),
            ],
            out_specs=pl.BlockSpec((ATT_BLK, dq), lambda b, i, s: (b * N_QBLK + i, 0)),
        ),
        out_shape=jax.ShapeDtypeStruct((R_LAT, dq), BF16),
        compiler_params=pltpu.CompilerParams(vmem_limit_bytes=VMEM_LIMIT),
        name="attention",
    )(sink, q, kv, kv, kv, kv)


def _tail1_kernel(z_ref, mod_ref, o_ref_in, g_ref, wout_ref, lng_ref, lnb_ref, out_ref):
    i = pl.program_id(0)
    gate = mod_ref[pl.ds(_block_mod_row(i), 1), 2 * D:3 * D]
    g = g_ref[...].astype(F32)
    mix = (o_ref_in[...].astype(F32) * _silu(g)).astype(BF16)
    y = jnp.dot(mix, wout_ref[...], preferred_element_type=F32) * gate
    out_ref[...] = _layer_norm(ALPHA * z_ref[...] + y, lng_ref[...], lnb_ref[...])


def _tail1(z1, mod, o, g, w_out, ln_g, ln_b):
    row = lambda n: pl.BlockSpec((1, n), lambda i: (0, 0))
    return pl.pallas_call(
        _tail1_kernel,
        grid=(N_LAT_BLK,),
        in_specs=[
            pl.BlockSpec((TN, D), lambda i: (i, 0)),
            pl.BlockSpec((8, 3 * D), lambda i: (0, 0)),
            pl.BlockSpec((TN, D), lambda i: (i, 0)),
            pl.BlockSpec((TN, D), lambda i: (i, 0)),
            pl.BlockSpec((D, D), lambda i: (0, 0)),
            row(D), row(D),
        ],
        out_specs=pl.BlockSpec((TN, D), lambda i: (i, 0)),
        out_shape=jax.ShapeDtypeStruct((R_LAT, D), F32),
        compiler_params=pltpu.CompilerParams(vmem_limit_bytes=VMEM_LIMIT),
        name="tail1",
    )(z1, mod, o, g, w_out, ln_g, ln_b)


def kernel(x, c, ctx, c_ctx, mod_w, mod_b, ln_g, ln_b, e_w_in, e_w_out, s5_lam_re, s5_lam_im,
           s5_log_dt, s5_b_re, s5_b_im, s5_c_re, s5_c_im, s5_d, glu_w, glu_b, sg_ln_g, sg_ln_b,
           sg_w, sg_b, o_w_in, o_w_out, o_sink):
    x2 = x.reshape(R_LAT, D)
    ctx2 = ctx.reshape(R_CTX, D)
    mod = _modulation(c, c_ctx, mod_w, mod_b)

    xa, rest = _in0(x2, ctx2, mod[0], e_w_in[0].astype(BF16))
    t_mat, e_mat, ft_mat, al_re, al_im = _s5_prep(
        s5_lam_re[0], s5_lam_im[0], s5_log_dt[0], s5_b_re[0], s5_b_im[0], s5_c_re[0], s5_c_im[0],
        s5_d[0])
    st_loc = _s5_states(xa, e_mat)
    st_in = _s5_scan(st_loc, al_re, al_im)
    ys_lat, ys_ctx = _s5_out(xa, st_in, t_mat, ft_mat)
    sg_bias = jnp.repeat(jnp.transpose(sg_b[0]), SG_W // SG_HEADS, axis=1)
    z1 = _tail0(x2, ctx2, mod[0], ys_lat, ys_ctx, rest, glu_w[0].astype(BF16), glu_b[0].reshape(1, S5_W),
                sg_ln_g[0].reshape(1, SG_W), sg_ln_b[0].reshape(1, SG_W), sg_w[0].astype(BF16),
                sg_bias, e_w_out[0].astype(BF16), ln_g[0].reshape(1, D), ln_b[0].reshape(1, D))

    cos_t, sin_t = _rope_tables()
    q, kv, g = _in1(z1, mod[1], o_w_in[0].astype(BF16), cos_t, sin_t)
    o = _attention(o_sink[0], q, kv)
    out = _tail1(z1, mod[1], o, g, o_w_out[0].astype(BF16), ln_g[1].reshape(1, D),
                 ln_b[1].reshape(1, D))
    return out.reshape(B, N_LAT, D)
```

```python
import functools
import math

import jax
import jax.numpy as jnp
from jax import lax
from jax.experimental import pallas as pl
from jax.experimental.pallas import tpu as pltpu

F32 = jnp.float32
BF16 = jnp.bfloat16
HIGHEST = lax.Precision.HIGHEST

D = 1024
B = 2
N_LAT = 8192
N_CTX = 256
DEPTH = 2
GRID_W = 64
S5_W = 512
S5_GC = 16
S5_G = 32
S5_P = 64
SG_W = 512
SG_HEADS = 8
SG_CHUNK = 128
N_HEADS = 16
N_KV = 4
HD = 64
WINDOW = 128
ATT_BLK = 128
ROPE_BASE = 10000.0
NEG_INF = -1e30
LN_EPS = 1e-5
ALPHA = (2 * DEPTH) ** 0.25
LOG2E = math.log2(math.e)
EVEN_IN = 2 * S5_W + 3 * SG_W
ODD_IN = 2 * N_HEADS * HD + 2 * N_KV * HD

LANES = 128
VMEM_LIMIT = 56 * 1024 * 1024

R_LAT = B * N_LAT
R_CTX = B * N_CTX
R_ALL = R_LAT + R_CTX
TN = 512
N_LAT_BLK = R_LAT // TN
N_BLK = R_ALL // TN
LAT_BLK_PER_BATCH = N_LAT // TN
CTX_MOD_ROW = B

S5_CHUNK = 8
S5_QT = S5_W // LANES
S5_GPT = LANES // S5_GC
S5_ST = S5_GPT * S5_P
LAT_CHUNKS = N_LAT // S5_CHUNK
CTX_CHUNKS = N_CTX // S5_CHUNK
BATCH_CHUNKS = CTX_CHUNKS + LAT_CHUNKS
N_CHUNK = B * BATCH_CHUNKS
N_SEG = 8
SEG_CHUNKS = BATCH_CHUNKS // N_SEG
BWD_RESET_STEP = LAT_CHUNKS - (N_SEG - 1) * SEG_CHUNKS


def _sigmoid(x):
    return 1.0 / (1.0 + jnp.exp(-x))


def _silu(x):
    return x * _sigmoid(x)


def _gelu_tanh(x):
    return 0.5 * x * (1.0 + jnp.tanh(math.sqrt(2.0 / math.pi) * (x + 0.044715 * (x * x * x))))


def _layer_norm(x, g, b):
    mu = jnp.mean(x, axis=-1, keepdims=True)
    xc = x - mu
    var = jnp.mean(xc * xc, axis=-1, keepdims=True)
    return xc * lax.rsqrt(var + LN_EPS) * g + b


def _block_mod_row(i):
    return jnp.where(i < N_LAT_BLK, i // LAT_BLK_PER_BATCH, CTX_MOD_ROW)


def _mod_kernel(cv_ref, w_ref, b_ref, o_ref):
    s = _silu(cv_ref[...])
    o_ref[0] = jnp.dot(s, w_ref[0], preferred_element_type=F32, precision=HIGHEST) + b_ref[0]


def _modulation(c, c_ctx, mod_w, mod_b):
    cv = jnp.concatenate([c, c_ctx[None], jnp.zeros((8 - B - 1, D), F32)], axis=0)
    tn = 1024
    return pl.pallas_call(
        _mod_kernel,
        grid=(DEPTH, 3 * D // tn),
        in_specs=[
            pl.BlockSpec((8, D), lambda l, j: (0, 0)),
            pl.BlockSpec((1, D, tn), lambda l, j: (l, 0, j)),
            pl.BlockSpec((1, 1, tn), lambda l, j: (l, 0, j)),
        ],
        out_specs=pl.BlockSpec((1, 8, tn), lambda l, j: (l, 0, j)),
        out_shape=jax.ShapeDtypeStruct((DEPTH, 8, 3 * D), F32),
        compiler_params=pltpu.CompilerParams(vmem_limit_bytes=VMEM_LIMIT),
        name="modulation",
    )(cv, mod_w, mod_b.reshape(DEPTH, 1, 3 * D))


def _in0_kernel(x_ref, ctx_ref, mod_ref, w_ref, xa_ref, rest_ref):
    i = pl.program_id(0)
    z = jnp.where(i < N_LAT_BLK, x_ref[...], ctx_ref[...])
    m = mod_ref[pl.ds(_block_mod_row(i), 1), :]
    h = z * (1.0 + m[:, D:2 * D]) + m[:, :D]
    p = jnp.dot(h.astype(BF16), w_ref[...], preferred_element_type=F32)
    xa_ref[...] = p[:, :S5_W]
    rest_ref[...] = p[:, S5_W:].astype(BF16)


def _in0(x2, ctx2, mod, w_in):
    return pl.pallas_call(
        _in0_kernel,
        grid=(N_BLK,),
        in_specs=[
            pl.BlockSpec((TN, D), lambda i: (jnp.minimum(i, N_LAT_BLK - 1), 0)),
            pl.BlockSpec((R_CTX, D), lambda i: (0, 0)),
            pl.BlockSpec((8, 3 * D), lambda i: (0, 0)),
            pl.BlockSpec((D, EVEN_IN), lambda i: (0, 0)),
        ],
        out_specs=[
            pl.BlockSpec((TN, S5_W), lambda i: (i, 0)),
            pl.BlockSpec((TN, EVEN_IN - S5_W), lambda i: (i, 0)),
        ],
        out_shape=[
            jax.ShapeDtypeStruct((R_ALL, S5_W), F32),
            jax.ShapeDtypeStruct((R_ALL, EVEN_IN - S5_W), BF16),
        ],
        compiler_params=pltpu.CompilerParams(vmem_limit_bytes=VMEM_LIMIT),
        name="in0",
    )(x2, ctx2, mod, w_in)


def _s5_prep_kernel(rows_ref, bt_ref, ct_ref, dsk_ref, t_ref, e_ref, ft_ref, alr_ref, ali_ref):
    L = S5_CHUNK
    tile = (LANES, S5_ST)
    same_group = (lax.broadcasted_iota(jnp.int32, tile, 0) // S5_GC
                  == lax.broadcasted_iota(jnp.int32, tile, 1) // S5_P)
    nt_dims = (((1,), (1,)), ((), ()))
    lag = []
    for d in range(2):
        lam_re = rows_ref[d, 0]
        lam_im = rows_ref[d, 1]
        dt = jnp.exp(rows_ref[d, 2])
        pw = []
        for k in range(L + 1):
            mag = jnp.exp(lam_re * dt * float(k))
            ang = lam_im * dt * float(k)
            pw.append((mag * jnp.cos(ang), mag * jnp.sin(ang)))
        alr_ref[d] = pw[L][0]
        ali_ref[d] = pw[L][1]
        den = lam_re * lam_re + lam_im * lam_im
        nr = pw[1][0] - 1.0
        ni = pw[1][1]
        f_re = (nr * lam_re + ni * lam_im) / den
        f_im = (ni * lam_re - nr * lam_im) / den
        b_re = jnp.where(same_group, bt_ref[0, d], 0.0)
        b_im = jnp.where(same_group, bt_ref[1, d], 0.0)
        bb_re = f_re * b_re - f_im * b_im
        bb_im = f_re * b_im + f_im * b_re
        c_re = jnp.where(same_group, ct_ref[0, d], 0.0)
        c_im = jnp.where(same_group, ct_ref[1, d], 0.0)
        c_cat = jnp.concatenate([c_re, -c_im], axis=-1)
        col_re = 2 * d * S5_ST
        col_im = (2 * d + 1) * S5_ST
        kd = []
        for k in range(L):
            x_re = pw[k][0] * bb_re - pw[k][1] * bb_im
            x_im = pw[k][0] * bb_im + pw[k][1] * bb_re
            s = (L - 1 - k) if d == 0 else k
            e_ref[0, s * LANES:(s + 1) * LANES, col_re:col_re + S5_ST] = x_re.astype(BF16)
            e_ref[0, s * LANES:(s + 1) * LANES, col_im:col_im + S5_ST] = x_im.astype(BF16)
            kd.append(lax.dot_general(jnp.concatenate([x_re, x_im], axis=-1), c_cat, nt_dims,
                                      precision=HIGHEST, preferred_element_type=F32))
        lag.append(kd)
        for t in range(L):
            k = (t + 1) if d == 0 else (L - t)
            w_re = c_re * pw[k][0] - c_im * pw[k][1]
            w_im = c_re * pw[k][1] + c_im * pw[k][0]
            ft_ref[0, t * LANES:(t + 1) * LANES, col_re:col_re + S5_ST] = w_re.astype(BF16)
            ft_ref[0, t * LANES:(t + 1) * LANES, col_im:col_im + S5_ST] = (-w_im).astype(BF16)

    sq = (LANES, LANES)
    on_diag = lax.broadcasted_iota(jnp.int32, sq, 0) == lax.broadcasted_iota(jnp.int32, sq, 1)
    diag = lag[0][0] + lag[1][0] + jnp.where(on_diag, dsk_ref[0], 0.0)
    for s in range(L):
        for t in range(L):
            blk = lag[0][t - s] if t > s else (lag[1][s - t] if s > t else diag)
            t_ref[0, s * LANES:(s + 1) * LANES, t * LANES:(t + 1) * LANES] = blk.astype(BF16)


def _s5_prep(lam_re, lam_im, log_dt, b_re, b_im, c_re, c_im, d_skip):
    L = S5_CHUNK
    n_st = S5_G * S5_P
    rows = jnp.stack([lam_re.reshape(2, 1, n_st), lam_im.reshape(2, 1, n_st),
                      jnp.repeat(log_dt, S5_P, axis=1).reshape(2, 1, n_st)], axis=1)
    rep = lambda a: jnp.tile(a.reshape(2, S5_W, S5_P), (1, 1, S5_GPT))
    bt = jnp.stack([rep(jnp.swapaxes(b_re, 2, 3)), rep(jnp.swapaxes(b_im, 2, 3))])
    ct = jnp.stack([rep(c_re), rep(c_im)])
    return pl.pallas_call(
        _s5_prep_kernel,
        grid=(S5_QT,),
        in_specs=[
            pl.BlockSpec((2, 3, 1, S5_ST), lambda q: (0, 0, 0, q)),
            pl.BlockSpec((2, 2, LANES, S5_ST), lambda q: (0, 0, q, 0)),
            pl.BlockSpec((2, 2, LANES, S5_ST), lambda q: (0, 0, q, 0)),
            pl.BlockSpec((1, 1, LANES), lambda q: (q, 0, 0)),
        ],
        out_specs=[
            pl.BlockSpec((1, L * LANES, L * LANES), lambda q: (q, 0, 0)),
            pl.BlockSpec((1, L * LANES, 4 * S5_ST), lambda q: (q, 0, 0)),
            pl.BlockSpec((1, L * LANES, 4 * S5_ST), lambda q: (q, 0, 0)),
            pl.BlockSpec((2, 1, S5_ST), lambda q: (q, 0, 0)),
            pl.BlockSpec((2, 1, S5_ST), lambda q: (q, 0, 0)),
        ],
        out_shape=[
            jax.ShapeDtypeStruct((S5_QT, L * LANES, L * LANES), BF16),
            jax.ShapeDtypeStruct((S5_QT, L * LANES, 4 * S5_ST), BF16),
            jax.ShapeDtypeStruct((S5_QT, L * LANES, 4 * S5_ST), BF16),
            jax.ShapeDtypeStruct((2 * S5_QT, 1, S5_ST), F32),
            jax.ShapeDtypeStruct((2 * S5_QT, 1, S5_ST), F32),
        ],
        compiler_params=pltpu.CompilerParams(vmem_limit_bytes=VMEM_LIMIT),
        name="s5_prep",
    )(rows, bt, ct, d_skip.reshape(S5_QT, 1, LANES))


def _chunk_rows(xa_ref, n_chunks):
    return jnp.concatenate(
        [xa_ref[pl.ds(t, n_chunks, stride=S5_CHUNK), :].astype(BF16) for t in range(S5_CHUNK)],
        axis=-1)


def _batch_chunk_rows(xl_ref, xc_ref):
    return jnp.concatenate(
        [_chunk_rows(xc_ref, CTX_CHUNKS), _chunk_rows(xl_ref, LAT_CHUNKS)], axis=0)


def _xa_specs(index_of):
    ctx_blk0 = R_LAT // N_CTX
    return [
        pl.BlockSpec((N_LAT, LANES), lambda *g: (index_of(*g)[1], index_of(*g)[0])),
        pl.BlockSpec((N_CTX, LANES), lambda *g: (ctx_blk0 + index_of(*g)[1], index_of(*g)[0])),
    ]


N_SLAB = S5_ST // LANES
ST_SHAPE = (S5_QT * 2 * 2 * N_SLAB, N_CHUNK, LANES)
ST_SPEC = pl.BlockSpec((2 * N_SLAB, BATCH_CHUNKS, LANES), lambda q, b, d: (2 * q + d, b, 0))


def _s5_states_kernel(xl_ref, xc_ref, e_ref, st_ref):
    st = jnp.dot(_batch_chunk_rows(xl_ref, xc_ref), e_ref[0], preferred_element_type=F32)
    for k in range(2 * N_SLAB):
        st_ref[k] = st[:, k * LANES:(k + 1) * LANES]


def _s5_states(xa, e_mat):
    L = S5_CHUNK
    return pl.pallas_call(
        _s5_states_kernel,
        grid=(S5_QT, B, 2),
        in_specs=_xa_specs(lambda q, b, d: (q, b)) + [
            pl.BlockSpec((1, L * LANES, 2 * S5_ST), lambda q, b, d: (q, 0, d)),
        ],
        out_specs=ST_SPEC,
        out_shape=jax.ShapeDtypeStruct(ST_SHAPE, F32),
        compiler_params=pltpu.CompilerParams(vmem_limit_bytes=VMEM_LIMIT),
        name="s5_states",
    )(xa, xa, e_mat)


def _cmul(ar, ai, br, bi):
    return ar * br - ai * bi, ar * bi + ai * br


def _cpow(ar, ai, n):
    res = None
    while n:
        if n & 1:
            res = (ar, ai) if res is None else _cmul(res[0], res[1], ar, ai)
        n >>= 1
        if n:
            ar, ai = _cmul(ar, ai, ar, ai)
    return res


def _s5_scan_kernel(loc_ref, ar_ref, ai_ref, out_ref):
    chains = [(b, k) for b in range(B) for k in range(N_SLAB)]

    def run(backward):
        a_row = [(ar_ref[0][:, k * LANES:(k + 1) * LANES], ai_ref[0][:, k * LANES:(k + 1) * LANES])
                 for k in range(N_SLAB)]
        a_tile = [(jnp.broadcast_to(r, (N_SEG, LANES)), jnp.broadcast_to(i, (N_SEG, LANES)))
                  for r, i in a_row]

        def sweep(lo, hi, carry, store):
            def step(i, carry):
                off = (SEG_CHUNKS - 1 - i) if backward else i
                new = []
                for (b, k), (cr, ci) in zip(chains, carry):
                    rows = pl.ds(b * BATCH_CHUNKS + off, N_SEG, stride=SEG_CHUNKS)
                    xr = loc_ref[k, rows, :]
                    xi = loc_ref[N_SLAB + k, rows, :]
                    if store:
                        out_ref[k, rows, :] = cr
                        out_ref[N_SLAB + k, rows, :] = ci
                    nr, ni = _cmul(a_tile[k][0], a_tile[k][1], cr, ci)
                    new.append((nr + xr, ni + xi))
                return tuple(new)
            return lax.fori_loop(lo, hi, step, carry)

        def full(carry, store):
            if not backward:
                return sweep(0, SEG_CHUNKS, carry, store)
            carry = sweep(0, BWD_RESET_STEP, carry, store)
            first = lax.broadcasted_iota(jnp.int32, (N_SEG, LANES), 0) == 0
            carry = tuple((jnp.where(first, 0.0, cr), jnp.where(first, 0.0, ci)) for cr, ci in carry)
            return sweep(BWD_RESET_STEP, SEG_CHUNKS, carry, store)

        zero_tile = jnp.zeros((N_SEG, LANES), F32)
        ends = full(tuple((zero_tile, zero_tile) for _ in chains), False)

        starts = []
        for (b, k), (er, ei) in zip(chains, ends):
            pr, pi = _cpow(a_row[k][0], a_row[k][1], SEG_CHUNKS)
            if backward:
                order = range(N_SEG - 1, -1, -1)
                g = (er[0:1], ei[0:1])
            else:
                order = range(N_SEG)
                g = (jnp.zeros((1, LANES), F32), jnp.zeros((1, LANES), F32))
            rows_r = [None] * N_SEG
            rows_i = [None] * N_SEG
            for j in order:
                rows_r[j], rows_i[j] = g
                nr, ni = _cmul(pr, pi, g[0], g[1])
                g = (nr + er[j:j + 1], ni + ei[j:j + 1])
            starts.append((jnp.concatenate(rows_r, axis=0), jnp.concatenate(rows_i, axis=0)))
        full(tuple(starts), True)

    backward = pl.program_id(0) % 2
    pl.when(backward == 0)(lambda: run(False))
    pl.when(backward == 1)(lambda: run(True))


def _s5_scan(st_loc, al_re, al_im):
    slabs = pl.BlockSpec((2 * N_SLAB, N_CHUNK, LANES), lambda j: (j, 0, 0))
    return pl.pallas_call(
        _s5_scan_kernel,
        grid=(2 * S5_QT,),
        in_specs=[
            slabs,
            pl.BlockSpec((1, 1, S5_ST), lambda j: (j, 0, 0)),
            pl.BlockSpec((1, 1, S5_ST), lambda j: (j, 0, 0)),
        ],
        out_specs=slabs,
        out_shape=jax.ShapeDtypeStruct(ST_SHAPE, F32),
        compiler_params=pltpu.CompilerParams(vmem_limit_bytes=VMEM_LIMIT),
        name="s5_scan",
    )(st_loc, al_re, al_im)


def _s5_out_kernel(xl_ref, xc_ref, st_ref, t_ref, ft_ref, yl_ref, yc_ref):
    d = pl.program_id(2)
    st = jnp.concatenate([st_ref[k] for k in range(2 * N_SLAB)], axis=-1).astype(BF16)
    y = lax.dot_general(st, ft_ref[0], (((1,), (1,)), ((), ())), preferred_element_type=F32)

    def tiles(t):
        return (pl.ds(t, CTX_CHUNKS, stride=S5_CHUNK), pl.ds(t, LAT_CHUNKS, stride=S5_CHUNK),
                slice(t * LANES, (t + 1) * LANES))

    @pl.when(d == 0)
    def _():
        y0 = y + jnp.dot(_batch_chunk_rows(xl_ref, xc_ref), t_ref[0], preferred_element_type=F32)
        for t in range(S5_CHUNK):
            rc, rl, cols = tiles(t)
            yc_ref[rc, :] = y0[:CTX_CHUNKS, cols]
            yl_ref[rl, :] = y0[CTX_CHUNKS:, cols]

    @pl.when(d == 1)
    def _():
        for t in range(S5_CHUNK):
            rc, rl, cols = tiles(t)
            yc_ref[rc, :] += y[:CTX_CHUNKS, cols]
            yl_ref[rl, :] += y[CTX_CHUNKS:, cols]


def _s5_out(xa, st_in, t_mat, ft_mat):
    L = S5_CHUNK
    return pl.pallas_call(
        _s5_out_kernel,
        grid=(S5_QT, B, 2),
        in_specs=_xa_specs(lambda q, b, d: (q, b)) + [
            ST_SPEC,
            pl.BlockSpec((1, L * LANES, L * LANES), lambda q, b, d: (q, 0, 0)),
            pl.BlockSpec((1, L * LANES, 2 * S5_ST), lambda q, b, d: (q, 0, d)),
        ],
        out_specs=[
            pl.BlockSpec((N_LAT, LANES), lambda q, b, d: (b, q)),
            pl.BlockSpec((N_CTX, LANES), lambda q, b, d: (b, q)),
        ],
        out_shape=[
            jax.ShapeDtypeStruct((R_LAT, S5_W), F32),
            jax.ShapeDtypeStruct((R_CTX, S5_W), F32),
        ],
        compiler_params=pltpu.CompilerParams(vmem_limit_bytes=VMEM_LIMIT),
        name="s5_out",
    )(xa, xa, st_in, t_mat, ft_mat)


def _tail0_kernel(x_ref, ctx_ref, mod_ref, ysl_ref, ysc_ref, rest_ref, gluw_ref, glub_ref,
                  sg_g_ref, sg_b_ref, sgw_ref, sgbias_ref, wout_ref, lng_ref, lnb_ref, o_ref):
    i = pl.program_id(0)
    z = jnp.where(i < N_LAT_BLK, x_ref[...], ctx_ref[...])
    ys = jnp.where(i < N_LAT_BLK, ysl_ref[...], ysc_ref[...])
    gate = mod_ref[pl.ds(_block_mod_row(i), 1), 2 * D:3 * D]

    ga = rest_ref[:, 0:S5_W].astype(F32)
    u = rest_ref[:, S5_W:2 * S5_W].astype(F32)
    v = rest_ref[:, 2 * S5_W:3 * S5_W].astype(F32)
    gb = rest_ref[:, 3 * S5_W:4 * S5_W].astype(F32)

    ya = _gelu_tanh(ys)
    glu = jnp.dot(ya.astype(BF16), gluw_ref[...], preferred_element_type=F32) + glub_ref[...]
    ya = ya * _sigmoid(glu) * _silu(ga)

    vn = _layer_norm(v, sg_g_ref[...], sg_b_ref[...]).astype(BF16)
    lane = lax.broadcasted_iota(jnp.int32, (SG_CHUNK, LANES), 1)
    first_head = lane < (SG_W // SG_HEADS)
    chunks = []
    for c in range(TN // SG_CHUNK):
        tiles = []
        for j in range(SG_W // LANES):
            vt = vn[c * SG_CHUNK:(c + 1) * SG_CHUNK, j * LANES:(j + 1) * LANES]
            r0 = jnp.dot(sgw_ref[2 * j], vt, preferred_element_type=F32)
            r1 = jnp.dot(sgw_ref[2 * j + 1], vt, preferred_element_type=F32)
            tiles.append(jnp.where(first_head, r0, r1))
        chunks.append(jnp.concatenate(tiles, axis=-1) + sgbias_ref[...])
    s = jnp.concatenate(chunks, axis=0)
    yb = u * s * _silu(gb)

    mix = jnp.concatenate([ya, yb], axis=-1).astype(BF16)
    y = jnp.dot(mix, wout_ref[...], preferred_element_type=F32) * gate
    o_ref[...] = _layer_norm(ALPHA * z + y, lng_ref[...], lnb_ref[...])


def _tail0(x2, ctx2, mod, ys_lat, ys_ctx, rest, glu_w, glu_b, sg_g, sg_b, sg_w, sg_bias, w_out,
           ln_g, ln_b):
    row = lambda n: pl.BlockSpec((1, n), lambda i: (0, 0))
    return pl.pallas_call(
        _tail0_kernel,
        grid=(N_BLK,),
        in_specs=[
            pl.BlockSpec((TN, D), lambda i: (jnp.minimum(i, N_LAT_BLK - 1), 0)),
            pl.BlockSpec((R_CTX, D), lambda i: (0, 0)),
            pl.BlockSpec((8, 3 * D), lambda i: (0, 0)),
            pl.BlockSpec((TN, S5_W), lambda i: (jnp.minimum(i, N_LAT_BLK - 1), 0)),
            pl.BlockSpec((R_CTX, S5_W), lambda i: (0, 0)),
            pl.BlockSpec((TN, EVEN_IN - S5_W), lambda i: (i, 0)),
            pl.BlockSpec((S5_W, S5_W), lambda i: (0, 0)),
            row(S5_W), row(SG_W), row(SG_W),
            pl.BlockSpec((SG_HEADS, SG_CHUNK, SG_CHUNK), lambda i: (0, 0, 0)),
            pl.BlockSpec((SG_CHUNK, SG_W), lambda i: (0, 0)),
            pl.BlockSpec((S5_W + SG_W, D), lambda i: (0, 0)),
            row(D), row(D),
        ],
        out_specs=pl.BlockSpec((TN, D), lambda i: (i, 0)),
        out_shape=jax.ShapeDtypeStruct((R_ALL, D), F32),
        compiler_params=pltpu.CompilerParams(vmem_limit_bytes=VMEM_LIMIT),
        name="tail0",
    )(x2, ctx2, mod, ys_lat, ys_ctx, rest, glu_w, glu_b, sg_g, sg_b, sg_w, sg_bias, w_out, ln_g,
      ln_b)


def _rope_tables():
    nf = HD // 4
    n_rows = N_LAT // GRID_W
    lane = jnp.arange(LANES)
    inv = ROPE_BASE ** (-(lane % nf).astype(F32) / nf)
    by_row = ((lane % HD) // (HD // 2) == 0)[None, :]
    sign = jnp.where((lane % (HD // 2)) < nf, -1.0, 1.0)[None, :]
    row_ang = jnp.arange(n_rows, dtype=F32)[:, None] * inv[None, :]
    col_ang = jnp.arange(GRID_W, dtype=F32)[:, None] * inv[None, :]
    zero = jnp.zeros((), F32)
    cos_t = (jnp.where(by_row, jnp.cos(row_ang), zero)[:, None, :]
             + jnp.where(by_row, zero, jnp.cos(col_ang))[None, :, :]).reshape(N_LAT, LANES)
    sin_t = (jnp.where(by_row, sign * jnp.sin(row_ang), zero)[:, None, :]
             + jnp.where(by_row, zero, sign * jnp.sin(col_ang))[None, :, :]).reshape(N_LAT, LANES)
    return cos_t, sin_t


def _rope_tile(x, cos, sin, first_half):
    nf = HD // 4
    partner = jnp.where(first_half, pltpu.roll(x, LANES - nf, axis=1), pltpu.roll(x, nf, axis=1))
    return x * cos + partner * sin


def _in1_kernel(z_ref, mod_ref, w_ref, wvt_ref, cos_ref, sin_ref, q_ref, k_ref, vt_ref, g_ref):
    i = pl.program_id(0)
    m = mod_ref[pl.ds(_block_mod_row(i), 1), :]
    h = (z_ref[...] * (1.0 + m[:, D:2 * D]) + m[:, :D]).astype(BF16)
    p = jnp.dot(h, w_ref[...], preferred_element_type=F32)
    vt_ref[...] = lax.dot_general(wvt_ref[...], h, (((1,), (1,)), ((), ())),
                                  preferred_element_type=F32).astype(BF16)
    dq = N_HEADS * HD
    dkv = N_KV * HD
    cos = jnp.where(i < N_LAT_BLK, cos_ref[...], 1.0)
    sin = jnp.where(i < N_LAT_BLK, sin_ref[...], 0.0)
    lane = lax.broadcasted_iota(jnp.int32, (TN, LANES), 1)
    first_half = (lane % (HD // 2)) < (HD // 4)
    scale = HD ** -0.5 * LOG2E
    for j in range(dq // LANES):
        q_ref[:, j * LANES:(j + 1) * LANES] = (
            _rope_tile(p[:, j * LANES:(j + 1) * LANES], cos, sin, first_half) * scale).astype(BF16)
    for j in range(dkv // LANES):
        k_ref[:, j * LANES:(j + 1) * LANES] = _rope_tile(
            p[:, dq + j * LANES:dq + (j + 1) * LANES], cos, sin, first_half).astype(BF16)
    g_ref[...] = p[:, dq + dkv:].astype(BF16)


def _in1(z1, mod, w_in, cos_t, sin_t):
    dq = N_HEADS * HD
    dkv = N_KV * HD
    w_qkg = jnp.concatenate([w_in[:, :dq + dkv], w_in[:, dq + 2 * dkv:]], axis=1).astype(BF16)
    w_vt = jnp.transpose(w_in[:, dq + dkv:dq + 2 * dkv]).astype(BF16)
    tab = lambda i: (i % LAT_BLK_PER_BATCH, 0)
    return pl.pallas_call(
        _in1_kernel,
        grid=(N_BLK,),
        in_specs=[
            pl.BlockSpec((TN, D), lambda i: (i, 0)),
            pl.BlockSpec((8, 3 * D), lambda i: (0, 0)),
            pl.BlockSpec((D, ODD_IN - dkv), lambda i: (0, 0)),
            pl.BlockSpec((dkv, D), lambda i: (0, 0)),
            pl.BlockSpec((TN, LANES), tab),
            pl.BlockSpec((TN, LANES), tab),
        ],
        out_specs=[
            pl.BlockSpec((TN, dq), lambda i: (i, 0)),
            pl.BlockSpec((TN, dkv), lambda i: (i, 0)),
            pl.BlockSpec((dkv, TN), lambda i: (0, i)),
            pl.BlockSpec((TN, dq), lambda i: (i, 0)),
        ],
        out_shape=[
            jax.ShapeDtypeStruct((R_ALL, dq), BF16),
            jax.ShapeDtypeStruct((R_ALL, dkv), BF16),
            jax.ShapeDtypeStruct((dkv, R_ALL), BF16),
            jax.ShapeDtypeStruct((R_ALL, dq), BF16),
        ],
        compiler_params=pltpu.CompilerParams(vmem_limit_bytes=VMEM_LIMIT),
        name="in1",
    )(z1, mod, w_qkg, w_vt, cos_t, sin_t)


N_QBLK = N_LAT // ATT_BLK
GRP = N_HEADS // N_KV


ONES_ROWS = 16


def _attn_kernel(sink_ref, q_ref, kp_ref, kc_ref, kn_ref, kx_ref, vp_ref, vc_ref, vn_ref, vx_ref,
                 o_ref):
    i = pl.program_id(1)
    n_win = 3 * ATT_BLK
    n_keys = n_win + N_CTX
    nq = GRP * ATT_BLK
    k_all = jnp.concatenate([kp_ref[...], kc_ref[...], kn_ref[...], kx_ref[...]], axis=0)
    vt_all = jnp.concatenate([vp_ref[...], vc_ref[...], vn_ref[...], vx_ref[...]], axis=1)

    kpos = lax.broadcasted_iota(jnp.int32, (ATT_BLK, ATT_BLK), 0)
    qpos = lax.broadcasted_iota(jnp.int32, (ATT_BLK, ATT_BLK), 1)
    bias_prev = jnp.where((kpos >= qpos) & (i > 0), 0.0, NEG_INF)
    bias_next = jnp.where((kpos <= qpos) & (i < N_QBLK - 1), 0.0, NEG_INF)
    bias_prev = jnp.concatenate([bias_prev] * GRP, axis=1)
    bias_next = jnp.concatenate([bias_next] * GRP, axis=1)
    qgrp = lax.broadcasted_iota(jnp.int32, (1, nq), 1) // ATT_BLK
    ones = jnp.ones((ONES_ROWS, n_keys), BF16)
    q = q_ref[...]

    scores = []
    for h in range(N_KV):
        kh = k_all[:, h * HD:(h + 1) * HD]
        qh = jnp.concatenate(
            [q[:, (h * GRP + g) * HD:(h * GRP + g + 1) * HD] for g in range(GRP)], axis=0)
        scores.append(lax.dot_general(kh, qh, (((1,), (1,)), ((), ())),
                                      preferred_element_type=F32))
    probs = []
    for h in range(N_KV):
        s = scores[h]
        s = jnp.concatenate([s[:ATT_BLK] + bias_prev, s[ATT_BLK:2 * ATT_BLK],
                             s[2 * ATT_BLK:n_win] + bias_next, s[n_win:]], axis=0)
        sink = jnp.zeros((1, nq), F32)
        for g in range(GRP):
            sink = jnp.where(qgrp == g, sink_ref[h * GRP + g] * LOG2E, sink)
        m = jnp.maximum(jnp.max(s, axis=0, keepdims=True), sink)
        probs.append((jnp.exp2(s - m).astype(BF16), jnp.exp2(sink - m)))
    outs = []
    for h in range(N_KV):
        p, p_sink = probs[h]
        vt1 = jnp.concatenate([vt_all[h * HD:(h + 1) * HD], ones], axis=0)
        ov = jnp.dot(vt1, p, preferred_element_type=F32)
        o_t = ov[:HD] / (ov[HD:HD + 1] + p_sink)
        outs.extend(o_t[:, g * ATT_BLK:(g + 1) * ATT_BLK] for g in range(GRP))
    o_ref[...] = jnp.transpose(jnp.concatenate(outs, axis=0)).astype(BF16)


def _attention(sink, q, k, vt):
    dq = N_HEADS * HD
    dkv = N_KV * HD
    ctx_blk0 = R_LAT // N_CTX
    prev_blk = lambda b, i: b * N_QBLK + jnp.maximum(i - 1, 0)
    next_blk = lambda b, i: b * N_QBLK + jnp.minimum(i + 1, N_QBLK - 1)
    return pl.pallas_call(
        _attn_kernel,
        grid_spec=pltpu.PrefetchScalarGridSpec(
            num_scalar_prefetch=1,
            grid=(B, N_QBLK),
            in_specs=[
                pl.BlockSpec((ATT_BLK, dq), lambda b, i, s: (b * N_QBLK + i, 0)),
                pl.BlockSpec((ATT_BLK, dkv), lambda b, i, s: (prev_blk(b, i), 0)),
                pl.BlockSpec((ATT_BLK, dkv), lambda b, i, s: (b * N_QBLK + i, 0)),
                pl.BlockSpec((ATT_BLK, dkv), lambda b, i, s: (next_blk(b, i), 0)),
                pl.BlockSpec((N_CTX, dkv), lambda b, i, s: (ctx_blk0 + b, 0)),
                pl.BlockSpec((dkv, ATT_BLK), lambda b, i, s: (0, prev_blk(b, i))),
                pl.BlockSpec((dkv, ATT_BLK), lambda b, i, s: (0, b * N_QBLK + i)),
                pl.BlockSpec((dkv, ATT_BLK), lambda b, i, s: (0, next_blk(b, i))),
                pl.BlockSpec((dkv, N_CTX), lambda b, i, s: (0, ctx_blk0 + b)),
            ],
            out_specs=pl.BlockSpec((ATT_BLK, dq), lambda b, i, s: (b * N_QBLK + i, 0)),
        ),
        out_shape=jax.ShapeDtypeStruct((R_LAT, dq), BF16),
        compiler_params=pltpu.CompilerParams(vmem_limit_bytes=VMEM_LIMIT),
        name="attention",
    )(sink, q, k, k, k, k, vt, vt, vt, vt)


def _tail1_kernel(z_ref, mod_ref, o_ref_in, g_ref, wout_ref, lng_ref, lnb_ref, out_ref):
    i = pl.program_id(0)
    gate = mod_ref[pl.ds(_block_mod_row(i), 1), 2 * D:3 * D]
    g = g_ref[...].astype(F32)
    mix = (o_ref_in[...].astype(F32) * _silu(g)).astype(BF16)
    y = jnp.dot(mix, wout_ref[...], preferred_element_type=F32) * gate
    out_ref[...] = _layer_norm(ALPHA * z_ref[...] + y, lng_ref[...], lnb_ref[...])


def _tail1(z1, mod, o, g, w_out, ln_g, ln_b):
    row = lambda n: pl.BlockSpec((1, n), lambda i: (0, 0))
    return pl.pallas_call(
        _tail1_kernel,
        grid=(N_LAT_BLK,),
        in_specs=[
            pl.BlockSpec((TN, D), lambda i: (i, 0)),
            pl.BlockSpec((8, 3 * D), lambda i: (0, 0)),
            pl.BlockSpec((TN, D), lambda i: (i, 0)),
            pl.BlockSpec((TN, D), lambda i: (i, 0)),
            pl.BlockSpec((D, D), lambda i: (0, 0)),
            row(D), row(D),
        ],
        out_specs=pl.BlockSpec((TN, D), lambda i: (i, 0)),
        out_shape=jax.ShapeDtypeStruct((R_LAT, D), F32),
        compiler_params=pltpu.CompilerParams(vmem_limit_bytes=VMEM_LIMIT),
        name="tail1",
    )(z1, mod, o, g, w_out, ln_g, ln_b)


def kernel(x, c, ctx, c_ctx, mod_w, mod_b, ln_g, ln_b, e_w_in, e_w_out, s5_lam_re, s5_lam_im,
           s5_log_dt, s5_b_re, s5_b_im, s5_c_re, s5_c_im, s5_d, glu_w, glu_b, sg_ln_g, sg_ln_b,
           sg_w, sg_b, o_w_in, o_w_out, o_sink):
    x2 = x.reshape(R_LAT, D)
    ctx2 = ctx.reshape(R_CTX, D)
    mod = _modulation(c, c_ctx, mod_w, mod_b)

    xa, rest = _in0(x2, ctx2, mod[0], e_w_in[0].astype(BF16))
    t_mat, e_mat, ft_mat, al_re, al_im = _s5_prep(
        s5_lam_re[0], s5_lam_im[0], s5_log_dt[0], s5_b_re[0], s5_b_im[0], s5_c_re[0], s5_c_im[0],
        s5_d[0])
    st_loc = _s5_states(xa, e_mat)
    st_in = _s5_scan(st_loc, al_re, al_im)
    ys_lat, ys_ctx = _s5_out(xa, st_in, t_mat, ft_mat)
    sg_bias = jnp.repeat(jnp.transpose(sg_b[0]), SG_W // SG_HEADS, axis=1)
    z1 = _tail0(x2, ctx2, mod[0], ys_lat, ys_ctx, rest, glu_w[0].astype(BF16), glu_b[0].reshape(1, S5_W),
                sg_ln_g[0].reshape(1, SG_W), sg_ln_b[0].reshape(1, SG_W), sg_w[0].astype(BF16),
                sg_bias, e_w_out[0].astype(BF16), ln_g[0].reshape(1, D), ln_b[0].reshape(1, D))

    cos_t, sin_t = _rope_tables()
    q, k, vt, g = _in1(z1, mod[1], o_w_in[0], cos_t, sin_t)
    o = _attention(o_sink[0], q, k, vt)
    out = _tail1(z1, mod[1], o, g, o_w_out[0].astype(BF16), ln_g[1].reshape(1, D),
                 ln_b[1].reshape(1, D))
    return out.reshape(B, N_LAT, D)
```

```python
import functools
import math

import jax
import jax.numpy as jnp
from jax import lax
from jax.experimental import pallas as pl
from jax.experimental.pallas import tpu as pltpu

F32 = jnp.float32
BF16 = jnp.bfloat16

D = 1024
B = 2
N_LAT = 8192
N_CTX = 256
DEPTH = 2
GRID_W = 64
S5_W = 512
S5_GC = 16
S5_G = 32
S5_P = 64
SG_W = 512
SG_HEADS = 8
SG_CHUNK = 128
N_HEADS = 16
N_KV = 4
HD = 64
WINDOW = 128
ATT_BLK = 128
ROPE_BASE = 10000.0
NEG_INF = -1e30
LN_EPS = 1e-5
ALPHA = (2 * DEPTH) ** 0.25
LOG2E = math.log2(math.e)
EVEN_IN = 2 * S5_W + 3 * SG_W
ODD_IN = 2 * N_HEADS * HD + 2 * N_KV * HD

LANES = 128
VMEM_LIMIT = 56 * 1024 * 1024

R_LAT = B * N_LAT
R_CTX = B * N_CTX
R_ALL = R_LAT + R_CTX
TN = 512
N_LAT_BLK = R_LAT // TN
N_BLK = R_ALL // TN
LAT_BLK_PER_BATCH = N_LAT // TN
CTX_MOD_ROW = B

S5_CHUNK = 8
S5_QT = S5_W // LANES
S5_GPT = LANES // S5_GC
S5_ST = S5_GPT * S5_P
LAT_CHUNKS = N_LAT // S5_CHUNK
CTX_CHUNKS = N_CTX // S5_CHUNK
BATCH_CHUNKS = CTX_CHUNKS + LAT_CHUNKS
N_CHUNK = B * BATCH_CHUNKS
N_SEG = 8
SEG_CHUNKS = BATCH_CHUNKS // N_SEG
BWD_RESET_STEP = LAT_CHUNKS - (N_SEG - 1) * SEG_CHUNKS


def _sigmoid(x):
    return 1.0 / (1.0 + jnp.exp(-x))


def _silu(x):
    return x * _sigmoid(x)


def _gelu_tanh(x):
    return 0.5 * x * (1.0 + jnp.tanh(math.sqrt(2.0 / math.pi) * (x + 0.044715 * (x * x * x))))


def _layer_norm(x, g, b):
    mu = jnp.mean(x, axis=-1, keepdims=True)
    xc = x - mu
    var = jnp.mean(xc * xc, axis=-1, keepdims=True)
    return xc * lax.rsqrt(var + LN_EPS) * g + b


def _split_bf16(x):
    hi = x.astype(BF16)
    return hi, (x - hi.astype(F32)).astype(BF16)


def _dot_split(x, y_split, dims):
    x_hi, x_lo = _split_bf16(x)
    y_hi, y_lo = y_split
    dot = lambda a, b: lax.dot_general(a, b, dims, preferred_element_type=F32)
    return dot(x_hi, y_hi) + (dot(x_hi, y_lo) + dot(x_lo, y_hi))


def _block_mod_row(i):
    return jnp.where(i < N_LAT_BLK, i // LAT_BLK_PER_BATCH, CTX_MOD_ROW)


def _mod_kernel(cv_ref, w_ref, b_ref, o_ref):
    s = _silu(cv_ref[...])
    o_ref[0] = _dot_split(s, _split_bf16(w_ref[0]), (((1,), (0,)), ((), ()))) + b_ref[0]


def _modulation(c, c_ctx, mod_w, mod_b):
    cv = jnp.concatenate([c, c_ctx[None], jnp.zeros((8 - B - 1, D), F32)], axis=0)
    tn = 1024
    return pl.pallas_call(
        _mod_kernel,
        grid=(DEPTH, 3 * D // tn),
        in_specs=[
            pl.BlockSpec((8, D), lambda l, j: (0, 0)),
            pl.BlockSpec((1, D, tn), lambda l, j: (l, 0, j)),
            pl.BlockSpec((1, 1, tn), lambda l, j: (l, 0, j)),
        ],
        out_specs=pl.BlockSpec((1, 8, tn), lambda l, j: (l, 0, j)),
        out_shape=jax.ShapeDtypeStruct((DEPTH, 8, 3 * D), F32),
        compiler_params=pltpu.CompilerParams(vmem_limit_bytes=VMEM_LIMIT),
        name="modulation",
    )(cv, mod_w, mod_b.reshape(DEPTH, 1, 3 * D))


def _in0_kernel(x_ref, ctx_ref, mod_ref, w_ref, xa_ref, rest_ref):
    i = pl.program_id(0)
    z = jnp.where(i < N_LAT_BLK, x_ref[...], ctx_ref[...])
    m = mod_ref[pl.ds(_block_mod_row(i), 1), :]
    h = z * (1.0 + m[:, D:2 * D]) + m[:, :D]
    p = jnp.dot(h.astype(BF16), w_ref[...], preferred_element_type=F32)
    xa_ref[...] = p[:, :S5_W]
    rest_ref[...] = p[:, S5_W:].astype(BF16)


def _in0(x2, ctx2, mod, w_in):
    return pl.pallas_call(
        _in0_kernel,
        grid=(N_BLK,),
        in_specs=[
            pl.BlockSpec((TN, D), lambda i: (jnp.minimum(i, N_LAT_BLK - 1), 0)),
            pl.BlockSpec((R_CTX, D), lambda i: (0, 0)),
            pl.BlockSpec((8, 3 * D), lambda i: (0, 0)),
            pl.BlockSpec((D, EVEN_IN), lambda i: (0, 0)),
        ],
        out_specs=[
            pl.BlockSpec((TN, S5_W), lambda i: (i, 0)),
            pl.BlockSpec((TN, EVEN_IN - S5_W), lambda i: (i, 0)),
        ],
        out_shape=[
            jax.ShapeDtypeStruct((R_ALL, S5_W), F32),
            jax.ShapeDtypeStruct((R_ALL, EVEN_IN - S5_W), BF16),
        ],
        compiler_params=pltpu.CompilerParams(vmem_limit_bytes=VMEM_LIMIT),
        name="in0",
    )(x2, ctx2, mod, w_in)


def _s5_prep_kernel(rows_ref, bt_ref, ct_ref, dsk_ref, t_ref, e_ref, ft_ref, alr_ref, ali_ref):
    L = S5_CHUNK
    tile = (LANES, S5_ST)
    same_group = (lax.broadcasted_iota(jnp.int32, tile, 0) // S5_GC
                  == lax.broadcasted_iota(jnp.int32, tile, 1) // S5_P)
    nt_dims = (((1,), (1,)), ((), ()))
    lag = []
    for d in range(2):
        lam_re = rows_ref[d, 0]
        lam_im = rows_ref[d, 1]
        dt = jnp.exp(rows_ref[d, 2])
        pw = []
        for k in range(L + 1):
            mag = jnp.exp(lam_re * dt * float(k))
            ang = lam_im * dt * float(k)
            pw.append((mag * jnp.cos(ang), mag * jnp.sin(ang)))
        alr_ref[d] = pw[L][0]
        ali_ref[d] = pw[L][1]
        den = lam_re * lam_re + lam_im * lam_im
        nr = pw[1][0] - 1.0
        ni = pw[1][1]
        f_re = (nr * lam_re + ni * lam_im) / den
        f_im = (ni * lam_re - nr * lam_im) / den
        b_re = jnp.where(same_group, bt_ref[0, d], 0.0)
        b_im = jnp.where(same_group, bt_ref[1, d], 0.0)
        bb_re = f_re * b_re - f_im * b_im
        bb_im = f_re * b_im + f_im * b_re
        c_re = jnp.where(same_group, ct_ref[0, d], 0.0)
        c_im = jnp.where(same_group, ct_ref[1, d], 0.0)
        c_split = _split_bf16(jnp.concatenate([c_re, -c_im], axis=-1))
        col_re = 2 * d * S5_ST
        col_im = (2 * d + 1) * S5_ST
        kd = []
        for k in range(L):
            x_re = pw[k][0] * bb_re - pw[k][1] * bb_im
            x_im = pw[k][0] * bb_im + pw[k][1] * bb_re
            s = (L - 1 - k) if d == 0 else k
            e_ref[0, s * LANES:(s + 1) * LANES, col_re:col_re + S5_ST] = x_re.astype(BF16)
            e_ref[0, s * LANES:(s + 1) * LANES, col_im:col_im + S5_ST] = x_im.astype(BF16)
            kd.append(_dot_split(jnp.concatenate([x_re, x_im], axis=-1), c_split, nt_dims))
        lag.append(kd)
        for t in range(L):
            k = (t + 1) if d == 0 else (L - t)
            w_re = c_re * pw[k][0] - c_im * pw[k][1]
            w_im = c_re * pw[k][1] + c_im * pw[k][0]
            ft_ref[0, t * LANES:(t + 1) * LANES, col_re:col_re + S5_ST] = w_re.astype(BF16)
            ft_ref[0, t * LANES:(t + 1) * LANES, col_im:col_im + S5_ST] = (-w_im).astype(BF16)

    sq = (LANES, LANES)
    on_diag = lax.broadcasted_iota(jnp.int32, sq, 0) == lax.broadcasted_iota(jnp.int32, sq, 1)
    diag = lag[0][0] + lag[1][0] + jnp.where(on_diag, dsk_ref[0], 0.0)
    for s in range(L):
        for t in range(L):
            blk = lag[0][t - s] if t > s else (lag[1][s - t] if s > t else diag)
            t_ref[0, s * LANES:(s + 1) * LANES, t * LANES:(t + 1) * LANES] = blk.astype(BF16)


def _s5_prep(lam_re, lam_im, log_dt, b_re, b_im, c_re, c_im, d_skip):
    L = S5_CHUNK
    n_st = S5_G * S5_P
    rows = jnp.stack([lam_re.reshape(2, 1, n_st), lam_im.reshape(2, 1, n_st),
                      jnp.repeat(log_dt, S5_P, axis=1).reshape(2, 1, n_st)], axis=1)
    rep = lambda a: jnp.tile(a.reshape(2, S5_W, S5_P), (1, 1, S5_GPT))
    bt = jnp.stack([rep(jnp.swapaxes(b_re, 2, 3)), rep(jnp.swapaxes(b_im, 2, 3))])
    ct = jnp.stack([rep(c_re), rep(c_im)])
    return pl.pallas_call(
        _s5_prep_kernel,
        grid=(S5_QT,),
        in_specs=[
            pl.BlockSpec((2, 3, 1, S5_ST), lambda q: (0, 0, 0, q)),
            pl.BlockSpec((2, 2, LANES, S5_ST), lambda q: (0, 0, q, 0)),
            pl.BlockSpec((2, 2, LANES, S5_ST), lambda q: (0, 0, q, 0)),
            pl.BlockSpec((1, 1, LANES), lambda q: (q, 0, 0)),
        ],
        out_specs=[
            pl.BlockSpec((1, L * LANES, L * LANES), lambda q: (q, 0, 0)),
            pl.BlockSpec((1, L * LANES, 4 * S5_ST), lambda q: (q, 0, 0)),
            pl.BlockSpec((1, L * LANES, 4 * S5_ST), lambda q: (q, 0, 0)),
            pl.BlockSpec((2, 1, S5_ST), lambda q: (q, 0, 0)),
            pl.BlockSpec((2, 1, S5_ST), lambda q: (q, 0, 0)),
        ],
        out_shape=[
            jax.ShapeDtypeStruct((S5_QT, L * LANES, L * LANES), BF16),
            jax.ShapeDtypeStruct((S5_QT, L * LANES, 4 * S5_ST), BF16),
            jax.ShapeDtypeStruct((S5_QT, L * LANES, 4 * S5_ST), BF16),
            jax.ShapeDtypeStruct((2 * S5_QT, 1, S5_ST), F32),
            jax.ShapeDtypeStruct((2 * S5_QT, 1, S5_ST), F32),
        ],
        compiler_params=pltpu.CompilerParams(vmem_limit_bytes=VMEM_LIMIT),
        name="s5_prep",
    )(rows, bt, ct, d_skip.reshape(S5_QT, 1, LANES))


def _chunk_rows(xa_ref, n_chunks):
    return jnp.concatenate(
        [xa_ref[pl.ds(t, n_chunks, stride=S5_CHUNK), :].astype(BF16) for t in range(S5_CHUNK)],
        axis=-1)


def _batch_chunk_rows(xl_ref, xc_ref):
    return jnp.concatenate(
        [_chunk_rows(xc_ref, CTX_CHUNKS), _chunk_rows(xl_ref, LAT_CHUNKS)], axis=0)


def _xa_specs(index_of):
    ctx_blk0 = R_LAT // N_CTX
    return [
        pl.BlockSpec((N_LAT, LANES), lambda *g: (index_of(*g)[1], index_of(*g)[0])),
        pl.BlockSpec((N_CTX, LANES), lambda *g: (ctx_blk0 + index_of(*g)[1], index_of(*g)[0])),
    ]


N_SLAB = S5_ST // LANES
ST_SHAPE = (S5_QT * 2 * 2 * N_SLAB, N_CHUNK, LANES)
ST_SPEC = pl.BlockSpec((2 * N_SLAB, BATCH_CHUNKS, LANES), lambda q, b, d: (2 * q + d, b, 0))


def _s5_states_kernel(xl_ref, xc_ref, e_ref, st_ref):
    st = jnp.dot(_batch_chunk_rows(xl_ref, xc_ref), e_ref[0], preferred_element_type=F32)
    for k in range(2 * N_SLAB):
        st_ref[k] = st[:, k * LANES:(k + 1) * LANES]


def _s5_states(xa, e_mat):
    L = S5_CHUNK
    return pl.pallas_call(
        _s5_states_kernel,
        grid=(S5_QT, B, 2),
        in_specs=_xa_specs(lambda q, b, d: (q, b)) + [
            pl.BlockSpec((1, L * LANES, 2 * S5_ST), lambda q, b, d: (q, 0, d)),
        ],
        out_specs=ST_SPEC,
        out_shape=jax.ShapeDtypeStruct(ST_SHAPE, F32),
        compiler_params=pltpu.CompilerParams(vmem_limit_bytes=VMEM_LIMIT),
        name="s5_states",
    )(xa, xa, e_mat)


def _cmul(ar, ai, br, bi):
    return ar * br - ai * bi, ar * bi + ai * br


def _cpow(ar, ai, n):
    res = None
    while n:
        if n & 1:
            res = (ar, ai) if res is None else _cmul(res[0], res[1], ar, ai)
        n >>= 1
        if n:
            ar, ai = _cmul(ar, ai, ar, ai)
    return res


def _s5_scan_kernel(loc_ref, ar_ref, ai_ref, out_ref):
    chains = [(b, k) for b in range(B) for k in range(N_SLAB)]

    def run(backward):
        a_row = [(ar_ref[0][:, k * LANES:(k + 1) * LANES], ai_ref[0][:, k * LANES:(k + 1) * LANES])
                 for k in range(N_SLAB)]
        a_tile = [(jnp.broadcast_to(r, (N_SEG, LANES)), jnp.broadcast_to(i, (N_SEG, LANES)))
                  for r, i in a_row]

        def sweep(lo, hi, carry, store):
            def step(i, carry):
                off = (SEG_CHUNKS - 1 - i) if backward else i
                new = []
                for (b, k), (cr, ci) in zip(chains, carry):
                    rows = pl.ds(b * BATCH_CHUNKS + off, N_SEG, stride=SEG_CHUNKS)
                    xr = loc_ref[k, rows, :]
                    xi = loc_ref[N_SLAB + k, rows, :]
                    if store:
                        out_ref[k, rows, :] = cr
                        out_ref[N_SLAB + k, rows, :] = ci
                    nr, ni = _cmul(a_tile[k][0], a_tile[k][1], cr, ci)
                    new.append((nr + xr, ni + xi))
                return tuple(new)
            return lax.fori_loop(lo, hi, step, carry)

        def full(carry, store):
            if not backward:
                return sweep(0, SEG_CHUNKS, carry, store)
            carry = sweep(0, BWD_RESET_STEP, carry, store)
            first = lax.broadcasted_iota(jnp.int32, (N_SEG, LANES), 0) == 0
            carry = tuple((jnp.where(first, 0.0, cr), jnp.where(first, 0.0, ci)) for cr, ci in carry)
            return sweep(BWD_RESET_STEP, SEG_CHUNKS, carry, store)

        zero_tile = jnp.zeros((N_SEG, LANES), F32)
        ends = full(tuple((zero_tile, zero_tile) for _ in chains), False)

        starts = []
        for (b, k), (er, ei) in zip(chains, ends):
            pr, pi = _cpow(a_row[k][0], a_row[k][1], SEG_CHUNKS)
            if backward:
                order = range(N_SEG - 1, -1, -1)
                g = (er[0:1], ei[0:1])
            else:
                order = range(N_SEG)
                g = (jnp.zeros((1, LANES), F32), jnp.zeros((1, LANES), F32))
            rows_r = [None] * N_SEG
            rows_i = [None] * N_SEG
            for j in order:
                rows_r[j], rows_i[j] = g
                nr, ni = _cmul(pr, pi, g[0], g[1])
                g = (nr + er[j:j + 1], ni + ei[j:j + 1])
            starts.append((jnp.concatenate(rows_r, axis=0), jnp.concatenate(rows_i, axis=0)))
        full(tuple(starts), True)

    backward = pl.program_id(0) % 2
    pl.when(backward == 0)(lambda: run(False))
    pl.when(backward == 1)(lambda: run(True))


def _s5_scan(st_loc, al_re, al_im):
    slabs = pl.BlockSpec((2 * N_SLAB, N_CHUNK, LANES), lambda j: (j, 0, 0))
    return pl.pallas_call(
        _s5_scan_kernel,
        grid=(2 * S5_QT,),
        in_specs=[
            slabs,
            pl.BlockSpec((1, 1, S5_ST), lambda j: (j, 0, 0)),
            pl.BlockSpec((1, 1, S5_ST), lambda j: (j, 0, 0)),
        ],
        out_specs=slabs,
        out_shape=jax.ShapeDtypeStruct(ST_SHAPE, F32),
        compiler_params=pltpu.CompilerParams(vmem_limit_bytes=VMEM_LIMIT),
        name="s5_scan",
    )(st_loc, al_re, al_im)


def _s5_out_kernel(xl_ref, xc_ref, st_ref, t_ref, ft_ref, yl_ref, yc_ref):
    d = pl.program_id(2)
    st = jnp.concatenate([st_ref[k] for k in range(2 * N_SLAB)], axis=-1).astype(BF16)
    y = lax.dot_general(st, ft_ref[0], (((1,), (1,)), ((), ())), preferred_element_type=F32)

    def tiles(t):
        return (pl.ds(t, CTX_CHUNKS, stride=S5_CHUNK), pl.ds(t, LAT_CHUNKS, stride=S5_CHUNK),
                slice(t * LANES, (t + 1) * LANES))

    @pl.when(d == 0)
    def _():
        y0 = y + jnp.dot(_batch_chunk_rows(xl_ref, xc_ref), t_ref[0], preferred_element_type=F32)
        for t in range(S5_CHUNK):
            rc, rl, cols = tiles(t)
            yc_ref[rc, :] = y0[:CTX_CHUNKS, cols]
            yl_ref[rl, :] = y0[CTX_CHUNKS:, cols]

    @pl.when(d == 1)
    def _():
        for t in range(S5_CHUNK):
            rc, rl, cols = tiles(t)
            yc_ref[rc, :] += y[:CTX_CHUNKS, cols]
            yl_ref[rl, :] += y[CTX_CHUNKS:, cols]


def _s5_out(xa, st_in, t_mat, ft_mat):
    L = S5_CHUNK
    return pl.pallas_call(
        _s5_out_kernel,
        grid=(S5_QT, B, 2),
        in_specs=_xa_specs(lambda q, b, d: (q, b)) + [
            ST_SPEC,
            pl.BlockSpec((1, L * LANES, L * LANES), lambda q, b, d: (q, 0, 0)),
            pl.BlockSpec((1, L * LANES, 2 * S5_ST), lambda q, b, d: (q, 0, d)),
        ],
        out_specs=[
            pl.BlockSpec((N_LAT, LANES), lambda q, b, d: (b, q)),
            pl.BlockSpec((N_CTX, LANES), lambda q, b, d: (b, q)),
        ],
        out_shape=[
            jax.ShapeDtypeStruct((R_LAT, S5_W), F32),
            jax.ShapeDtypeStruct((R_CTX, S5_W), F32),
        ],
        compiler_params=pltpu.CompilerParams(vmem_limit_bytes=VMEM_LIMIT),
        name="s5_out",
    )(xa, xa, st_in, t_mat, ft_mat)


def _tail0_kernel(x_ref, ctx_ref, mod_ref, ysl_ref, ysc_ref, rest_ref, gluw_ref, glub_ref,
                  sg_g_ref, sg_b_ref, sgw_ref, sgbias_ref, wout_ref, lng_ref, lnb_ref, o_ref):
    i = pl.program_id(0)
    z = jnp.where(i < N_LAT_BLK, x_ref[...], ctx_ref[...])
    ys = jnp.where(i < N_LAT_BLK, ysl_ref[...], ysc_ref[...])
    gate = mod_ref[pl.ds(_block_mod_row(i), 1), 2 * D:3 * D]

    ga = rest_ref[:, 0:S5_W].astype(F32)
    u = rest_ref[:, S5_W:2 * S5_W].astype(F32)
    v = rest_ref[:, 2 * S5_W:3 * S5_W].astype(F32)
    gb = rest_ref[:, 3 * S5_W:4 * S5_W].astype(F32)

    ya = _gelu_tanh(ys)
    glu = jnp.dot(ya.astype(BF16), gluw_ref[...], preferred_element_type=F32) + glub_ref[...]
    ya = ya * _sigmoid(glu) * _silu(ga)

    vn = _layer_norm(v, sg_g_ref[...], sg_b_ref[...]).astype(BF16)
    lane = lax.broadcasted_iota(jnp.int32, (SG_CHUNK, LANES), 1)
    first_head = lane < (SG_W // SG_HEADS)
    chunks = []
    for c in range(TN // SG_CHUNK):
        tiles = []
        for j in range(SG_W // LANES):
            vt = vn[c * SG_CHUNK:(c + 1) * SG_CHUNK, j * LANES:(j + 1) * LANES]
            r0 = jnp.dot(sgw_ref[2 * j], vt, preferred_element_type=F32)
            r1 = jnp.dot(sgw_ref[2 * j + 1], vt, preferred_element_type=F32)
            tiles.append(jnp.where(first_head, r0, r1))
        chunks.append(jnp.concatenate(tiles, axis=-1) + sgbias_ref[...])
    s = jnp.concatenate(chunks, axis=0)
    yb = u * s * _silu(gb)

    mix = jnp.concatenate([ya, yb], axis=-1).astype(BF16)
    y = jnp.dot(mix, wout_ref[...], preferred_element_type=F32) * gate
    o_ref[...] = _layer_norm(ALPHA * z + y, lng_ref[...], lnb_ref[...])


def _tail0(x2, ctx2, mod, ys_lat, ys_ctx, rest, glu_w, glu_b, sg_g, sg_b, sg_w, sg_bias, w_out,
           ln_g, ln_b):
    row = lambda n: pl.BlockSpec((1, n), lambda i: (0, 0))
    return pl.pallas_call(
        _tail0_kernel,
        grid=(N_BLK,),
        in_specs=[
            pl.BlockSpec((TN, D), lambda i: (jnp.minimum(i, N_LAT_BLK - 1), 0)),
            pl.BlockSpec((R_CTX, D), lambda i: (0, 0)),
            pl.BlockSpec((8, 3 * D), lambda i: (0, 0)),
            pl.BlockSpec((TN, S5_W), lambda i: (jnp.minimum(i, N_LAT_BLK - 1), 0)),
            pl.BlockSpec((R_CTX, S5_W), lambda i: (0, 0)),
            pl.BlockSpec((TN, EVEN_IN - S5_W), lambda i: (i, 0)),
            pl.BlockSpec((S5_W, S5_W), lambda i: (0, 0)),
            row(S5_W), row(SG_W), row(SG_W),
            pl.BlockSpec((SG_HEADS, SG_CHUNK, SG_CHUNK), lambda i: (0, 0, 0)),
            pl.BlockSpec((SG_CHUNK, SG_W), lambda i: (0, 0)),
            pl.BlockSpec((S5_W + SG_W, D), lambda i: (0, 0)),
            row(D), row(D),
        ],
        out_specs=pl.BlockSpec((TN, D), lambda i: (i, 0)),
        out_shape=jax.ShapeDtypeStruct((R_ALL, D), F32),
        compiler_params=pltpu.CompilerParams(vmem_limit_bytes=VMEM_LIMIT),
        name="tail0",
    )(x2, ctx2, mod, ys_lat, ys_ctx, rest, glu_w, glu_b, sg_g, sg_b, sg_w, sg_bias, w_out, ln_g,
      ln_b)


def _rope_tables():
    nf = HD // 4
    n_rows = N_LAT // GRID_W
    lane = jnp.arange(LANES)
    inv = ROPE_BASE ** (-(lane % nf).astype(F32) / nf)
    by_row = ((lane % HD) // (HD // 2) == 0)[None, :]
    sign = jnp.where((lane % (HD // 2)) < nf, -1.0, 1.0)[None, :]
    row_ang = jnp.arange(n_rows, dtype=F32)[:, None] * inv[None, :]
    col_ang = jnp.arange(GRID_W, dtype=F32)[:, None] * inv[None, :]
    zero = jnp.zeros((), F32)
    cos_t = (jnp.where(by_row, jnp.cos(row_ang), zero)[:, None, :]
             + jnp.where(by_row, zero, jnp.cos(col_ang))[None, :, :]).reshape(N_LAT, LANES)
    sin_t = (jnp.where(by_row, sign * jnp.sin(row_ang), zero)[:, None, :]
             + jnp.where(by_row, zero, sign * jnp.sin(col_ang))[None, :, :]).reshape(N_LAT, LANES)
    return cos_t, sin_t


def _rope_tile(x, cos, sin, first_half):
    nf = HD // 4
    partner = jnp.where(first_half, pltpu.roll(x, LANES - nf, axis=1), pltpu.roll(x, nf, axis=1))
    return x * cos + partner * sin


def _in1_kernel(z_ref, mod_ref, w_ref, wvt_ref, cos_ref, sin_ref, q_ref, k_ref, vt_ref, g_ref):
    i = pl.program_id(0)
    m = mod_ref[pl.ds(_block_mod_row(i), 1), :]
    h = (z_ref[...] * (1.0 + m[:, D:2 * D]) + m[:, :D]).astype(BF16)
    p = jnp.dot(h, w_ref[...], preferred_element_type=F32)
    vt_ref[...] = lax.dot_general(wvt_ref[...], h, (((1,), (1,)), ((), ())),
                                  preferred_element_type=F32).astype(BF16)
    dq = N_HEADS * HD
    dkv = N_KV * HD
    cos = jnp.where(i < N_LAT_BLK, cos_ref[...], 1.0)
    sin = jnp.where(i < N_LAT_BLK, sin_ref[...], 0.0)
    lane = lax.broadcasted_iota(jnp.int32, (TN, LANES), 1)
    first_half = (lane % (HD // 2)) < (HD // 4)
    scale = HD ** -0.5 * LOG2E
    for j in range(dq // LANES):
        q_ref[:, j * LANES:(j + 1) * LANES] = (
            _rope_tile(p[:, j * LANES:(j + 1) * LANES], cos, sin, first_half) * scale).astype(BF16)
    for j in range(dkv // LANES):
        k_ref[:, j * LANES:(j + 1) * LANES] = _rope_tile(
            p[:, dq + j * LANES:dq + (j + 1) * LANES], cos, sin, first_half).astype(BF16)
    g_ref[...] = p[:, dq + dkv:].astype(BF16)


def _in1(z1, mod, w_in, cos_t, sin_t):
    dq = N_HEADS * HD
    dkv = N_KV * HD
    w_qkg = jnp.concatenate([w_in[:, :dq + dkv], w_in[:, dq + 2 * dkv:]], axis=1).astype(BF16)
    w_vt = jnp.transpose(w_in[:, dq + dkv:dq + 2 * dkv]).astype(BF16)
    tab = lambda i: (i % LAT_BLK_PER_BATCH, 0)
    return pl.pallas_call(
        _in1_kernel,
        grid=(N_BLK,),
        in_specs=[
            pl.BlockSpec((TN, D), lambda i: (i, 0)),
            pl.BlockSpec((8, 3 * D), lambda i: (0, 0)),
            pl.BlockSpec((D, ODD_IN - dkv), lambda i: (0, 0)),
            pl.BlockSpec((dkv, D), lambda i: (0, 0)),
            pl.BlockSpec((TN, LANES), tab),
            pl.BlockSpec((TN, LANES), tab),
        ],
        out_specs=[
            pl.BlockSpec((TN, dq), lambda i: (i, 0)),
            pl.BlockSpec((TN, dkv), lambda i: (i, 0)),
            pl.BlockSpec((dkv, TN), lambda i: (0, i)),
            pl.BlockSpec((TN, dq), lambda i: (i, 0)),
        ],
        out_shape=[
            jax.ShapeDtypeStruct((R_ALL, dq), BF16),
            jax.ShapeDtypeStruct((R_ALL, dkv), BF16),
            jax.ShapeDtypeStruct((dkv, R_ALL), BF16),
            jax.ShapeDtypeStruct((R_ALL, dq), BF16),
        ],
        compiler_params=pltpu.CompilerParams(vmem_limit_bytes=VMEM_LIMIT),
        name="in1",
    )(z1, mod, w_qkg, w_vt, cos_t, sin_t)


N_QBLK = N_LAT // ATT_BLK
GRP = N_HEADS // N_KV
ATT_SUB = 2
ATT_ROWS = ATT_SUB * ATT_BLK
N_QSTEP = N_QBLK // ATT_SUB
ONES_ROWS = 16


def _attn_kernel(sink_ref, q_ref, kp_ref, kc_ref, kn_ref, kx_ref, vp_ref, vc_ref, vn_ref, vx_ref,
                 o_ref):
    i = pl.program_id(1)
    n_win = 3 * ATT_BLK
    n_keys = n_win + N_CTX
    nq = GRP * ATT_BLK
    k_win = jnp.concatenate([kp_ref[...], kc_ref[...], kn_ref[...]], axis=0)
    vt_win = jnp.concatenate([vp_ref[...], vc_ref[...], vn_ref[...]], axis=1)
    k_ctx = kx_ref[...]
    vt_ctx = vx_ref[...]

    kpos = lax.broadcasted_iota(jnp.int32, (ATT_BLK, ATT_BLK), 0)
    qpos = lax.broadcasted_iota(jnp.int32, (ATT_BLK, ATT_BLK), 1)
    tile_q = lambda a: jnp.concatenate([a] * GRP, axis=1)
    qgrp = lax.broadcasted_iota(jnp.int32, (1, nq), 1) // ATT_BLK
    ones = jnp.ones((ONES_ROWS, n_keys), BF16)
    q = q_ref[...]

    units = [(u, h) for u in range(ATT_SUB) for h in range(N_KV)]
    k_all, vt_all, bias = [], [], []
    for u in range(ATT_SUB):
        k_all.append(jnp.concatenate([k_win[u * ATT_BLK:u * ATT_BLK + n_win], k_ctx], axis=0))
        vt_all.append(jnp.concatenate([vt_win[:, u * ATT_BLK:u * ATT_BLK + n_win], vt_ctx], axis=1))
        has_prev = (i > 0) if u == 0 else True
        has_next = (i < N_QSTEP - 1) if u == ATT_SUB - 1 else True
        bias.append((tile_q(jnp.where((kpos >= qpos) & has_prev, 0.0, NEG_INF)),
                     tile_q(jnp.where((kpos <= qpos) & has_next, 0.0, NEG_INF))))
    scores = []
    for u, h in units:
        kh = k_all[u][:, h * HD:(h + 1) * HD]
        qh = jnp.concatenate(
            [q[u * ATT_BLK:(u + 1) * ATT_BLK, (h * GRP + g) * HD:(h * GRP + g + 1) * HD]
             for g in range(GRP)], axis=0)
        scores.append(lax.dot_general(kh, qh, (((1,), (1,)), ((), ())),
                                      preferred_element_type=F32))
    probs = []
    for (u, h), s in zip(units, scores):
        s = jnp.concatenate([s[:ATT_BLK] + bias[u][0], s[ATT_BLK:2 * ATT_BLK],
                             s[2 * ATT_BLK:n_win] + bias[u][1], s[n_win:]], axis=0)
        sink = jnp.zeros((1, nq), F32)
        for g in range(GRP):
            sink = jnp.where(qgrp == g, sink_ref[h * GRP + g] * LOG2E, sink)
        m = jnp.maximum(jnp.max(s, axis=0, keepdims=True), sink)
        probs.append((jnp.exp2(s - m).astype(BF16), jnp.exp2(sink - m)))
    outs = [[] for _ in range(ATT_SUB)]
    for (u, h), (p, p_sink) in zip(units, probs):
        vt1 = jnp.concatenate([vt_all[u][h * HD:(h + 1) * HD], ones], axis=0)
        ov = jnp.dot(vt1, p, preferred_element_type=F32)
        o_t = ov[:HD] / (ov[HD:HD + 1] + p_sink)
        outs[u].extend(o_t[:, g * ATT_BLK:(g + 1) * ATT_BLK] for g in range(GRP))
    for u in range(ATT_SUB):
        o_ref[u * ATT_BLK:(u + 1) * ATT_BLK, :] = jnp.transpose(
            jnp.concatenate(outs[u], axis=0)).astype(BF16)


def _attention(sink, q, k, vt):
    dq = N_HEADS * HD
    dkv = N_KV * HD
    ctx_blk0 = R_LAT // N_CTX
    cur = lambda b, i: b * N_QSTEP + i
    prev_blk = lambda b, i: b * N_QBLK + jnp.maximum(ATT_SUB * i - 1, 0)
    next_blk = lambda b, i: b * N_QBLK + jnp.minimum(ATT_SUB * (i + 1), N_QBLK - 1)
    return pl.pallas_call(
        _attn_kernel,
        grid_spec=pltpu.PrefetchScalarGridSpec(
            num_scalar_prefetch=1,
            grid=(B, N_QSTEP),
            in_specs=[
                pl.BlockSpec((ATT_ROWS, dq), lambda b, i, s: (cur(b, i), 0)),
                pl.BlockSpec((ATT_BLK, dkv), lambda b, i, s: (prev_blk(b, i), 0)),
                pl.BlockSpec((ATT_ROWS, dkv), lambda b, i, s: (cur(b, i), 0)),
                pl.BlockSpec((ATT_BLK, dkv), lambda b, i, s: (next_blk(b, i), 0)),
                pl.BlockSpec((N_CTX, dkv), lambda b, i, s: (ctx_blk0 + b, 0)),
                pl.BlockSpec((dkv, ATT_BLK), lambda b, i, s: (0, prev_blk(b, i))),
                pl.BlockSpec((dkv, ATT_ROWS), lambda b, i, s: (0, cur(b, i))),
                pl.BlockSpec((dkv, ATT_BLK), lambda b, i, s: (0, next_blk(b, i))),
                pl.BlockSpec((dkv, N_CTX), lambda b, i, s: (0, ctx_blk0 + b)),
            ],
            out_specs=pl.BlockSpec((ATT_ROWS, dq), lambda b, i, s: (cur(b, i), 0)),
        ),
        out_shape=jax.ShapeDtypeStruct((R_LAT, dq), BF16),
        compiler_params=pltpu.CompilerParams(vmem_limit_bytes=VMEM_LIMIT),
        name="attention",
    )(sink, q, k, k, k, k, vt, vt, vt, vt)


def _tail1_kernel(z_ref, mod_ref, o_ref_in, g_ref, wout_ref, lng_ref, lnb_ref, out_ref):
    i = pl.program_id(0)
    gate = mod_ref[pl.ds(_block_mod_row(i), 1), 2 * D:3 * D]
    g = g_ref[...].astype(F32)
    mix = (o_ref_in[...].astype(F32) * _silu(g)).astype(BF16)
    y = jnp.dot(mix, wout_ref[...], preferred_element_type=F32) * gate
    out_ref[...] = _layer_norm(ALPHA * z_ref[...] + y, lng_ref[...], lnb_ref[...])


def _tail1(z1, mod, o, g, w_out, ln_g, ln_b):
    row = lambda n: pl.BlockSpec((1, n), lambda i: (0, 0))
    return pl.pallas_call(
        _tail1_kernel,
        grid=(N_LAT_BLK,),
        in_specs=[
            pl.BlockSpec((TN, D), lambda i: (i, 0)),
            pl.BlockSpec((8, 3 * D), lambda i: (0, 0)),
            pl.BlockSpec((TN, D), lambda i: (i, 0)),
            pl.BlockSpec((TN, D), lambda i: (i, 0)),
            pl.BlockSpec((D, D), lambda i: (0, 0)),
            row(D), row(D),
        ],
        out_specs=pl.BlockSpec((TN, D), lambda i: (i, 0)),
        out_shape=jax.ShapeDtypeStruct((R_LAT, D), F32),
        compiler_params=pltpu.CompilerParams(vmem_limit_bytes=VMEM_LIMIT),
        name="tail1",
    )(z1, mod, o, g, w_out, ln_g, ln_b)


def kernel(x, c, ctx, c_ctx, mod_w, mod_b, ln_g, ln_b, e_w_in, e_w_out, s5_lam_re, s5_lam_im,
           s5_log_dt, s5_b_re, s5_b_im, s5_c_re, s5_c_im, s5_d, glu_w, glu_b, sg_ln_g, sg_ln_b,
           sg_w, sg_b, o_w_in, o_w_out, o_sink):
    x2 = x.reshape(R_LAT, D)
    ctx2 = ctx.reshape(R_CTX, D)
    mod = _modulation(c, c_ctx, mod_w, mod_b)

    xa, rest = _in0(x2, ctx2, mod[0], e_w_in[0].astype(BF16))
    t_mat, e_mat, ft_mat, al_re, al_im = _s5_prep(
        s5_lam_re[0], s5_lam_im[0], s5_log_dt[0], s5_b_re[0], s5_b_im[0], s5_c_re[0], s5_c_im[0],
        s5_d[0])
    st_loc = _s5_states(xa, e_mat)
    st_in = _s5_scan(st_loc, al_re, al_im)
    ys_lat, ys_ctx = _s5_out(xa, st_in, t_mat, ft_mat)
    sg_bias = jnp.repeat(jnp.transpose(sg_b[0]), SG_W // SG_HEADS, axis=1)
    z1 = _tail0(x2, ctx2, mod[0], ys_lat, ys_ctx, rest, glu_w[0].astype(BF16), glu_b[0].reshape(1, S5_W),
                sg_ln_g[0].reshape(1, SG_W), sg_ln_b[0].reshape(1, SG_W), sg_w[0].astype(BF16),
                sg_bias, e_w_out[0].astype(BF16), ln_g[0].reshape(1, D), ln_b[0].reshape(1, D))

    cos_t, sin_t = _rope_tables()
    q, k, vt, g = _in1(z1, mod[1], o_w_in[0], cos_t, sin_t)
    o = _attention(o_sink[0], q, k, vt)
    out = _tail1(z1, mod[1], o, g, o_w_out[0].astype(BF16), ln_g[1].reshape(1, D),
                 ln_b[1].reshape(1, D))
    return out.reshape(B, N_LAT, D)
```

```python
import functools
import math

import jax
import jax.numpy as jnp
from jax import lax
from jax.experimental import pallas as pl
from jax.experimental.pallas import tpu as pltpu

F32 = jnp.float32
BF16 = jnp.bfloat16

D = 1024
B = 2
N_LAT = 8192
N_CTX = 256
DEPTH = 2
GRID_W = 64
S5_W = 512
S5_GC = 16
S5_G = 32
S5_P = 64
SG_W = 512
SG_HEADS = 8
SG_CHUNK = 128
N_HEADS = 16
N_KV = 4
HD = 64
WINDOW = 128
ATT_BLK = 128
ROPE_BASE = 10000.0
NEG_INF = -1e30
LN_EPS = 1e-5
ALPHA = (2 * DEPTH) ** 0.25
LOG2E = math.log2(math.e)
EVEN_IN = 2 * S5_W + 3 * SG_W
ODD_IN = 2 * N_HEADS * HD + 2 * N_KV * HD

LANES = 128
VMEM_LIMIT = 56 * 1024 * 1024

R_LAT = B * N_LAT
R_CTX = B * N_CTX
R_ALL = R_LAT + R_CTX
TN = 512
N_LAT_BLK = R_LAT // TN
N_BLK = R_ALL // TN
LAT_BLK_PER_BATCH = N_LAT // TN
CTX_MOD_ROW = B

S5_CHUNK = 8
S5_QT = S5_W // LANES
S5_GPT = LANES // S5_GC
S5_ST = S5_GPT * S5_P
LAT_CHUNKS = N_LAT // S5_CHUNK
CTX_CHUNKS = N_CTX // S5_CHUNK
BATCH_CHUNKS = CTX_CHUNKS + LAT_CHUNKS
N_CHUNK = B * BATCH_CHUNKS
N_SEG = 8
SEG_CHUNKS = BATCH_CHUNKS // N_SEG
BWD_RESET_STEP = LAT_CHUNKS - (N_SEG - 1) * SEG_CHUNKS


def _sigmoid(x):
    return 1.0 / (1.0 + jnp.exp(-x))


def _silu(x):
    return x * _sigmoid(x)


def _gelu_tanh(x):
    return 0.5 * x * (1.0 + jnp.tanh(math.sqrt(2.0 / math.pi) * (x + 0.044715 * (x * x * x))))


def _layer_norm(x, g, b):
    mu = jnp.mean(x, axis=-1, keepdims=True)
    xc = x - mu
    var = jnp.mean(xc * xc, axis=-1, keepdims=True)
    return xc * lax.rsqrt(var + LN_EPS) * g + b


def _split_bf16(x):
    hi = x.astype(BF16)
    return hi, (x - hi.astype(F32)).astype(BF16)


def _dot_split(x, y_split, dims):
    x_hi, x_lo = _split_bf16(x)
    y_hi, y_lo = y_split
    dot = lambda a, b: lax.dot_general(a, b, dims, preferred_element_type=F32)
    return dot(x_hi, y_hi) + (dot(x_hi, y_lo) + dot(x_lo, y_hi))


def _block_mod_row(i):
    return jnp.where(i < N_LAT_BLK, i // LAT_BLK_PER_BATCH, CTX_MOD_ROW)


def _mod_kernel(cv_ref, w_ref, b_ref, o_ref):
    s = _silu(cv_ref[...])
    o_ref[0] = _dot_split(s, _split_bf16(w_ref[0]), (((1,), (0,)), ((), ()))) + b_ref[0]


def _modulation(c, c_ctx, mod_w, mod_b):
    cv = jnp.concatenate([c, c_ctx[None], jnp.zeros((8 - B - 1, D), F32)], axis=0)
    tn = 1024
    return pl.pallas_call(
        _mod_kernel,
        grid=(DEPTH, 3 * D // tn),
        in_specs=[
            pl.BlockSpec((8, D), lambda l, j: (0, 0)),
            pl.BlockSpec((1, D, tn), lambda l, j: (l, 0, j)),
            pl.BlockSpec((1, 1, tn), lambda l, j: (l, 0, j)),
        ],
        out_specs=pl.BlockSpec((1, 8, tn), lambda l, j: (l, 0, j)),
        out_shape=jax.ShapeDtypeStruct((DEPTH, 8, 3 * D), F32),
        compiler_params=pltpu.CompilerParams(vmem_limit_bytes=VMEM_LIMIT),
        name="modulation",
    )(cv, mod_w, mod_b.reshape(DEPTH, 1, 3 * D))


def _in0_kernel(x_ref, ctx_ref, mod_ref, w_ref, xa_ref, rest_ref):
    i = pl.program_id(0)
    z = jnp.where(i < N_LAT_BLK, x_ref[...], ctx_ref[...])
    m = mod_ref[pl.ds(_block_mod_row(i), 1), :]
    h = z * (1.0 + m[:, D:2 * D]) + m[:, :D]
    p = jnp.dot(h.astype(BF16), w_ref[...], preferred_element_type=F32)
    xa_ref[...] = p[:, :S5_W]
    rest_ref[...] = p[:, S5_W:].astype(BF16)


def _in0(x2, ctx2, mod, w_in):
    return pl.pallas_call(
        _in0_kernel,
        grid=(N_BLK,),
        in_specs=[
            pl.BlockSpec((TN, D), lambda i: (jnp.minimum(i, N_LAT_BLK - 1), 0)),
            pl.BlockSpec((R_CTX, D), lambda i: (0, 0)),
            pl.BlockSpec((8, 3 * D), lambda i: (0, 0)),
            pl.BlockSpec((D, EVEN_IN), lambda i: (0, 0)),
        ],
        out_specs=[
            pl.BlockSpec((TN, S5_W), lambda i: (i, 0)),
            pl.BlockSpec((TN, EVEN_IN - S5_W), lambda i: (i, 0)),
        ],
        out_shape=[
            jax.ShapeDtypeStruct((R_ALL, S5_W), F32),
            jax.ShapeDtypeStruct((R_ALL, EVEN_IN - S5_W), BF16),
        ],
        compiler_params=pltpu.CompilerParams(vmem_limit_bytes=VMEM_LIMIT),
        name="in0",
    )(x2, ctx2, mod, w_in)


def _s5_prep_kernel(rows_ref, bt_ref, ct_ref, dsk_ref, t_ref, e_ref, ft_ref, alr_ref, ali_ref):
    L = S5_CHUNK
    tile = (LANES, S5_ST)
    same_group = (lax.broadcasted_iota(jnp.int32, tile, 0) // S5_GC
                  == lax.broadcasted_iota(jnp.int32, tile, 1) // S5_P)
    nt_dims = (((1,), (1,)), ((), ()))
    lag = []
    for d in range(2):
        lam_re = rows_ref[d, 0]
        lam_im = rows_ref[d, 1]
        dt = jnp.exp(rows_ref[d, 2])
        pw = []
        for k in range(L + 1):
            mag = jnp.exp(lam_re * dt * float(k))
            ang = lam_im * dt * float(k)
            pw.append((mag * jnp.cos(ang), mag * jnp.sin(ang)))
        alr_ref[d] = pw[L][0]
        ali_ref[d] = pw[L][1]
        den = lam_re * lam_re + lam_im * lam_im
        nr = pw[1][0] - 1.0
        ni = pw[1][1]
        f_re = (nr * lam_re + ni * lam_im) / den
        f_im = (ni * lam_re - nr * lam_im) / den
        b_re = jnp.where(same_group, bt_ref[0, d], 0.0)
        b_im = jnp.where(same_group, bt_ref[1, d], 0.0)
        bb_re = f_re * b_re - f_im * b_im
        bb_im = f_re * b_im + f_im * b_re
        c_re = jnp.where(same_group, ct_ref[0, d], 0.0)
        c_im = jnp.where(same_group, ct_ref[1, d], 0.0)
        c_split = _split_bf16(jnp.concatenate([c_re, -c_im], axis=-1))
        col_re = 2 * d * S5_ST
        col_im = (2 * d + 1) * S5_ST
        kd = []
        for k in range(L):
            x_re = pw[k][0] * bb_re - pw[k][1] * bb_im
            x_im = pw[k][0] * bb_im + pw[k][1] * bb_re
            s = (L - 1 - k) if d == 0 else k
            e_ref[0, s * LANES:(s + 1) * LANES, col_re:col_re + S5_ST] = x_re.astype(BF16)
            e_ref[0, s * LANES:(s + 1) * LANES, col_im:col_im + S5_ST] = x_im.astype(BF16)
            kd.append(_dot_split(jnp.concatenate([x_re, x_im], axis=-1), c_split, nt_dims))
        lag.append(kd)
        for t in range(L):
            k = (t + 1) if d == 0 else (L - t)
            w_re = c_re * pw[k][0] - c_im * pw[k][1]
            w_im = c_re * pw[k][1] + c_im * pw[k][0]
            ft_ref[0, t * LANES:(t + 1) * LANES, col_re:col_re + S5_ST] = w_re.astype(BF16)
            ft_ref[0, t * LANES:(t + 1) * LANES, col_im:col_im + S5_ST] = (-w_im).astype(BF16)

    sq = (LANES, LANES)
    on_diag = lax.broadcasted_iota(jnp.int32, sq, 0) == lax.broadcasted_iota(jnp.int32, sq, 1)
    diag = lag[0][0] + lag[1][0] + jnp.where(on_diag, dsk_ref[0], 0.0)
    for s in range(L):
        for t in range(L):
            blk = lag[0][t - s] if t > s else (lag[1][s - t] if s > t else diag)
            t_ref[0, s * LANES:(s + 1) * LANES, t * LANES:(t + 1) * LANES] = blk.astype(BF16)


def _s5_prep(lam_re, lam_im, log_dt, b_re, b_im, c_re, c_im, d_skip):
    L = S5_CHUNK
    n_st = S5_G * S5_P
    rows = jnp.stack([lam_re.reshape(2, 1, n_st), lam_im.reshape(2, 1, n_st),
                      jnp.repeat(log_dt, S5_P, axis=1).reshape(2, 1, n_st)], axis=1)
    rep = lambda a: jnp.tile(a.reshape(2, S5_W, S5_P), (1, 1, S5_GPT))
    bt = jnp.stack([rep(jnp.swapaxes(b_re, 2, 3)), rep(jnp.swapaxes(b_im, 2, 3))])
    ct = jnp.stack([rep(c_re), rep(c_im)])
    return pl.pallas_call(
        _s5_prep_kernel,
        grid=(S5_QT,),
        in_specs=[
            pl.BlockSpec((2, 3, 1, S5_ST), lambda q: (0, 0, 0, q)),
            pl.BlockSpec((2, 2, LANES, S5_ST), lambda q: (0, 0, q, 0)),
            pl.BlockSpec((2, 2, LANES, S5_ST), lambda q: (0, 0, q, 0)),
            pl.BlockSpec((1, 1, LANES), lambda q: (q, 0, 0)),
        ],
        out_specs=[
            pl.BlockSpec((1, L * LANES, L * LANES), lambda q: (q, 0, 0)),
            pl.BlockSpec((1, L * LANES, 4 * S5_ST), lambda q: (q, 0, 0)),
            pl.BlockSpec((1, L * LANES, 4 * S5_ST), lambda q: (q, 0, 0)),
            pl.BlockSpec((2, 1, S5_ST), lambda q: (q, 0, 0)),
            pl.BlockSpec((2, 1, S5_ST), lambda q: (q, 0, 0)),
        ],
        out_shape=[
            jax.ShapeDtypeStruct((S5_QT, L * LANES, L * LANES), BF16),
            jax.ShapeDtypeStruct((S5_QT, L * LANES, 4 * S5_ST), BF16),
            jax.ShapeDtypeStruct((S5_QT, L * LANES, 4 * S5_ST), BF16),
            jax.ShapeDtypeStruct((2 * S5_QT, 1, S5_ST), F32),
            jax.ShapeDtypeStruct((2 * S5_QT, 1, S5_ST), F32),
        ],
        compiler_params=pltpu.CompilerParams(vmem_limit_bytes=VMEM_LIMIT),
        name="s5_prep",
    )(rows, bt, ct, d_skip.reshape(S5_QT, 1, LANES))


def _chunk_rows(xa_ref, n_chunks):
    return jnp.concatenate(
        [xa_ref[pl.ds(t, n_chunks, stride=S5_CHUNK), :].astype(BF16) for t in range(S5_CHUNK)],
        axis=-1)


def _batch_chunk_rows(xl_ref, xc_ref):
    return jnp.concatenate(
        [_chunk_rows(xc_ref, CTX_CHUNKS), _chunk_rows(xl_ref, LAT_CHUNKS)], axis=0)


def _xa_specs(index_of):
    ctx_blk0 = R_LAT // N_CTX
    return [
        pl.BlockSpec((N_LAT, LANES), lambda *g: (index_of(*g)[1], index_of(*g)[0])),
        pl.BlockSpec((N_CTX, LANES), lambda *g: (ctx_blk0 + index_of(*g)[1], index_of(*g)[0])),
    ]


N_SLAB = S5_ST // LANES
ST_SHAPE = (S5_QT * 2 * 2 * N_SLAB, N_CHUNK, LANES)
ST_SPEC = pl.BlockSpec((2 * N_SLAB, BATCH_CHUNKS, LANES), lambda q, b, d: (2 * q + d, b, 0))


def _s5_states_kernel(xl_ref, xc_ref, e_ref, st_ref):
    st = jnp.dot(_batch_chunk_rows(xl_ref, xc_ref), e_ref[0], preferred_element_type=F32)
    for k in range(2 * N_SLAB):
        st_ref[k] = st[:, k * LANES:(k + 1) * LANES]


def _s5_states(xa, e_mat):
    L = S5_CHUNK
    return pl.pallas_call(
        _s5_states_kernel,
        grid=(S5_QT, B, 2),
        in_specs=_xa_specs(lambda q, b, d: (q, b)) + [
            pl.BlockSpec((1, L * LANES, 2 * S5_ST), lambda q, b, d: (q, 0, d)),
        ],
        out_specs=ST_SPEC,
        out_shape=jax.ShapeDtypeStruct(ST_SHAPE, F32),
        compiler_params=pltpu.CompilerParams(vmem_limit_bytes=VMEM_LIMIT),
        name="s5_states",
    )(xa, xa, e_mat)


def _cmul(ar, ai, br, bi):
    return ar * br - ai * bi, ar * bi + ai * br


def _cpow(ar, ai, n):
    res = None
    while n:
        if n & 1:
            res = (ar, ai) if res is None else _cmul(res[0], res[1], ar, ai)
        n >>= 1
        if n:
            ar, ai = _cmul(ar, ai, ar, ai)
    return res


def _s5_scan_kernel(loc_ref, ar_ref, ai_ref, out_ref):
    chains = [(b, k) for b in range(B) for k in range(N_SLAB)]

    def run(backward):
        a_row = [(ar_ref[0][:, k * LANES:(k + 1) * LANES], ai_ref[0][:, k * LANES:(k + 1) * LANES])
                 for k in range(N_SLAB)]
        a_tile = [(jnp.broadcast_to(r, (N_SEG, LANES)), jnp.broadcast_to(i, (N_SEG, LANES)))
                  for r, i in a_row]

        def sweep(lo, hi, carry, store):
            def step(i, carry):
                off = (SEG_CHUNKS - 1 - i) if backward else i
                new = []
                for (b, k), (cr, ci) in zip(chains, carry):
                    rows = pl.ds(b * BATCH_CHUNKS + off, N_SEG, stride=SEG_CHUNKS)
                    xr = loc_ref[k, rows, :]
                    xi = loc_ref[N_SLAB + k, rows, :]
                    if store:
                        out_ref[k, rows, :] = cr
                        out_ref[N_SLAB + k, rows, :] = ci
                    nr, ni = _cmul(a_tile[k][0], a_tile[k][1], cr, ci)
                    new.append((nr + xr, ni + xi))
                return tuple(new)
            return lax.fori_loop(lo, hi, step, carry)

        def full(carry, store):
            if not backward:
                return sweep(0, SEG_CHUNKS, carry, store)
            carry = sweep(0, BWD_RESET_STEP, carry, store)
            first = lax.broadcasted_iota(jnp.int32, (N_SEG, LANES), 0) == 0
            carry = tuple((jnp.where(first, 0.0, cr), jnp.where(first, 0.0, ci)) for cr, ci in carry)
            return sweep(BWD_RESET_STEP, SEG_CHUNKS, carry, store)

        zero_tile = jnp.zeros((N_SEG, LANES), F32)
        ends = full(tuple((zero_tile, zero_tile) for _ in chains), False)

        starts = []
        for (b, k), (er, ei) in zip(chains, ends):
            pr, pi = _cpow(a_row[k][0], a_row[k][1], SEG_CHUNKS)
            if backward:
                order = range(N_SEG - 1, -1, -1)
                g = (er[0:1], ei[0:1])
            else:
                order = range(N_SEG)
                g = (jnp.zeros((1, LANES), F32), jnp.zeros((1, LANES), F32))
            rows_r = [None] * N_SEG
            rows_i = [None] * N_SEG
            for j in order:
                rows_r[j], rows_i[j] = g
                nr, ni = _cmul(pr, pi, g[0], g[1])
                g = (nr + er[j:j + 1], ni + ei[j:j + 1])
            starts.append((jnp.concatenate(rows_r, axis=0), jnp.concatenate(rows_i, axis=0)))
        full(tuple(starts), True)

    backward = pl.program_id(0) % 2
    pl.when(backward == 0)(lambda: run(False))
    pl.when(backward == 1)(lambda: run(True))


def _s5_scan(st_loc, al_re, al_im):
    slabs = pl.BlockSpec((2 * N_SLAB, N_CHUNK, LANES), lambda j: (j, 0, 0))
    return pl.pallas_call(
        _s5_scan_kernel,
        grid=(2 * S5_QT,),
        in_specs=[
            slabs,
            pl.BlockSpec((1, 1, S5_ST), lambda j: (j, 0, 0)),
            pl.BlockSpec((1, 1, S5_ST), lambda j: (j, 0, 0)),
        ],
        out_specs=slabs,
        out_shape=jax.ShapeDtypeStruct(ST_SHAPE, F32),
        compiler_params=pltpu.CompilerParams(vmem_limit_bytes=VMEM_LIMIT),
        name="s5_scan",
    )(st_loc, al_re, al_im)


def _s5_out_kernel(xl_ref, xc_ref, st_ref, t_ref, ft_ref, yl_ref, yc_ref):
    d = pl.program_id(2)
    st = jnp.concatenate([st_ref[k] for k in range(2 * N_SLAB)], axis=-1).astype(BF16)
    y = lax.dot_general(st, ft_ref[0], (((1,), (1,)), ((), ())), preferred_element_type=F32)

    def tiles(t):
        return (pl.ds(t, CTX_CHUNKS, stride=S5_CHUNK), pl.ds(t, LAT_CHUNKS, stride=S5_CHUNK),
                slice(t * LANES, (t + 1) * LANES))

    @pl.when(d == 0)
    def _():
        y0 = y + jnp.dot(_batch_chunk_rows(xl_ref, xc_ref), t_ref[0], preferred_element_type=F32)
        for t in range(S5_CHUNK):
            rc, rl, cols = tiles(t)
            yc_ref[rc, :] = y0[:CTX_CHUNKS, cols]
            yl_ref[rl, :] = y0[CTX_CHUNKS:, cols]

    @pl.when(d == 1)
    def _():
        for t in range(S5_CHUNK):
            rc, rl, cols = tiles(t)
            yc_ref[rc, :] += y[:CTX_CHUNKS, cols]
            yl_ref[rl, :] += y[CTX_CHUNKS:, cols]


def _s5_out(xa, st_in, t_mat, ft_mat):
    L = S5_CHUNK
    return pl.pallas_call(
        _s5_out_kernel,
        grid=(S5_QT, B, 2),
        in_specs=_xa_specs(lambda q, b, d: (q, b)) + [
            ST_SPEC,
            pl.BlockSpec((1, L * LANES, L * LANES), lambda q, b, d: (q, 0, 0)),
            pl.BlockSpec((1, L * LANES, 2 * S5_ST), lambda q, b, d: (q, 0, d)),
        ],
        out_specs=[
            pl.BlockSpec((N_LAT, LANES), lambda q, b, d: (b, q)),
            pl.BlockSpec((N_CTX, LANES), lambda q, b, d: (b, q)),
        ],
        out_shape=[
            jax.ShapeDtypeStruct((R_LAT, S5_W), F32),
            jax.ShapeDtypeStruct((R_CTX, S5_W), F32),
        ],
        compiler_params=pltpu.CompilerParams(vmem_limit_bytes=VMEM_LIMIT),
        name="s5_out",
    )(xa, xa, st_in, t_mat, ft_mat)


def _tail0_stages(z, ys, rest_ref, gate, gluw_ref, glub_ref, sg_g_ref, sg_b_ref, sgw_ref,
                  sgbias_ref, wout_ref, lng_ref, lnb_ref):
    ga = rest_ref[:, 0:S5_W].astype(F32)
    u = rest_ref[:, S5_W:2 * S5_W].astype(F32)
    v = rest_ref[:, 2 * S5_W:3 * S5_W].astype(F32)
    gb = rest_ref[:, 3 * S5_W:4 * S5_W].astype(F32)

    ya = _gelu_tanh(ys)
    glu = jnp.dot(ya.astype(BF16), gluw_ref[...], preferred_element_type=F32) + glub_ref[...]
    yield None
    ya = ya * _sigmoid(glu) * _silu(ga)
    yield None

    vn = _layer_norm(v, sg_g_ref[...], sg_b_ref[...]).astype(BF16)
    lane = lax.broadcasted_iota(jnp.int32, (SG_CHUNK, LANES), 1)
    first_head = lane < (SG_W // SG_HEADS)
    chunks = []
    for c in range(TN // SG_CHUNK):
        tiles = []
        for j in range(SG_W // LANES):
            vt = vn[c * SG_CHUNK:(c + 1) * SG_CHUNK, j * LANES:(j + 1) * LANES]
            r0 = jnp.dot(sgw_ref[2 * j], vt, preferred_element_type=F32)
            r1 = jnp.dot(sgw_ref[2 * j + 1], vt, preferred_element_type=F32)
            tiles.append(jnp.where(first_head, r0, r1))
        chunks.append(jnp.concatenate(tiles, axis=-1) + sgbias_ref[...])
    s = jnp.concatenate(chunks, axis=0)
    yb = u * s * _silu(gb)
    yield None

    mix = jnp.concatenate([ya, yb], axis=-1).astype(BF16)
    y = jnp.dot(mix, wout_ref[...], preferred_element_type=F32) * gate
    yield None
    yield _layer_norm(ALPHA * z + y, lng_ref[...], lnb_ref[...])


def _rope_tables():
    nf = HD // 4
    n_rows = N_LAT // GRID_W
    lane = jnp.arange(LANES)
    inv = ROPE_BASE ** (-(lane % nf).astype(F32) / nf)
    by_row = ((lane % HD) // (HD // 2) == 0)[None, :]
    sign = jnp.where((lane % (HD // 2)) < nf, -1.0, 1.0)[None, :]
    row_ang = jnp.arange(n_rows, dtype=F32)[:, None] * inv[None, :]
    col_ang = jnp.arange(GRID_W, dtype=F32)[:, None] * inv[None, :]
    zero = jnp.zeros((), F32)
    cos_t = (jnp.where(by_row, jnp.cos(row_ang), zero)[:, None, :]
             + jnp.where(by_row, zero, jnp.cos(col_ang))[None, :, :]).reshape(N_LAT, LANES)
    sin_t = (jnp.where(by_row, sign * jnp.sin(row_ang), zero)[:, None, :]
             + jnp.where(by_row, zero, sign * jnp.sin(col_ang))[None, :, :]).reshape(N_LAT, LANES)
    return cos_t, sin_t


def _rope_tile(x, cos, sin, first_half):
    nf = HD // 4
    partner = jnp.where(first_half, pltpu.roll(x, LANES - nf, axis=1), pltpu.roll(x, nf, axis=1))
    return x * cos + partner * sin


def _mid_kernel(x_ref, ctx_ref, mod0_ref, mod1_ref, ysl_ref, ysc_ref, rest_ref, gluw_ref, glub_ref,
                sg_g_ref, sg_b_ref, sgw_ref, sgbias_ref, wout_ref, lng_ref, lnb_ref, w_ref,
                wvt_ref, cos_ref, sin_ref, z1_ref, q_ref, k_ref, vt_ref, g_ref, zprev_ref):
    i = pl.program_id(0)
    dq = N_HEADS * HD
    dkv = N_KV * HD

    @pl.when(i == 0)
    def _():
        zprev_ref[...] = jnp.zeros((TN, D), F32)

    j = jnp.maximum(i - 1, 0)
    m1 = mod1_ref[pl.ds(_block_mod_row(j), 1), :]
    h = (zprev_ref[...] * (1.0 + m1[:, D:2 * D]) + m1[:, :D]).astype(BF16)
    cos = jnp.where(j < N_LAT_BLK, cos_ref[...], 1.0)
    sin = jnp.where(j < N_LAT_BLK, sin_ref[...], 0.0)
    lane = lax.broadcasted_iota(jnp.int32, (TN, LANES), 1)
    first_half = (lane % (HD // 2)) < (HD // 4)
    scale = HD ** -0.5 * LOG2E

    def project(c0, c1):
        return jnp.dot(h, w_ref[:, c0:c1], preferred_element_type=F32)

    def roped_tiles(p, mult):
        for c in range(p.shape[1] // LANES):
            r = _rope_tile(p[:, c * LANES:(c + 1) * LANES], cos, sin, first_half)
            yield c, (r * mult if mult != 1.0 else r).astype(BF16)

    t = jnp.minimum(i, N_BLK - 1)
    z = jnp.where(t < N_LAT_BLK, x_ref[...], ctx_ref[...])
    ys = jnp.where(t < N_LAT_BLK, ysl_ref[...], ysc_ref[...])
    gate = mod0_ref[pl.ds(_block_mod_row(t), 1), 2 * D:3 * D]
    tail = _tail0_stages(z, ys, rest_ref, gate, gluw_ref, glub_ref, sg_g_ref, sg_b_ref, sgw_ref,
                         sgbias_ref, wout_ref, lng_ref, lnb_ref)

    half_q = dq // 2
    p_q0 = project(0, half_q)
    next(tail)
    p_q1 = project(half_q, dq)
    next(tail)
    p_k = project(dq, dq + dkv)
    vt_ref[...] = lax.dot_general(wvt_ref[...], h, (((1,), (1,)), ((), ())),
                                  preferred_element_type=F32).astype(BF16)
    next(tail)
    next(tail)
    for c0, p in ((0, p_q0), (half_q, p_q1)):
        for c, r in roped_tiles(p, scale):
            q_ref[:, c0 + c * LANES:c0 + (c + 1) * LANES] = r
    for c, r in roped_tiles(p_k, 1.0):
        k_ref[:, c * LANES:(c + 1) * LANES] = r
    p_g = project(dq + dkv, ODD_IN - dkv)
    z1 = next(tail)
    z1_ref[...] = z1
    zprev_ref[...] = z1
    g_ref[...] = p_g.astype(BF16)


def _mid(x2, ctx2, mod0, mod1, ys_lat, ys_ctx, rest, glu_w, glu_b, sg_g, sg_b, sg_w, sg_bias,
         w_out, ln_g, ln_b, w_in1, cos_t, sin_t):
    dq = N_HEADS * HD
    dkv = N_KV * HD
    w_qkg = jnp.concatenate([w_in1[:, :dq + dkv], w_in1[:, dq + 2 * dkv:]], axis=1).astype(BF16)
    w_vt = jnp.transpose(w_in1[:, dq + dkv:dq + 2 * dkv]).astype(BF16)
    once = dict(pipeline_mode=pl.Buffered(1))
    row = lambda n: pl.BlockSpec((1, n), lambda i: (0, 0))
    lat = lambda i: (jnp.minimum(i, N_LAT_BLK - 1), 0)
    tail_blk = lambda i: (jnp.minimum(i, N_BLK - 1), 0)
    proj_blk = lambda i: (jnp.maximum(i - 1, 0), 0)
    tab = lambda i: (jnp.maximum(i - 1, 0) % LAT_BLK_PER_BATCH, 0)
    return pl.pallas_call(
        _mid_kernel,
        grid=(N_BLK + 1,),
        in_specs=[
            pl.BlockSpec((TN, D), lat),
            pl.BlockSpec((R_CTX, D), lambda i: (0, 0), **once),
            pl.BlockSpec((8, 3 * D), lambda i: (0, 0)),
            pl.BlockSpec((8, 3 * D), lambda i: (0, 0)),
            pl.BlockSpec((TN, S5_W), lat),
            pl.BlockSpec((R_CTX, S5_W), lambda i: (0, 0), **once),
            pl.BlockSpec((TN, EVEN_IN - S5_W), tail_blk),
            pl.BlockSpec((S5_W, S5_W), lambda i: (0, 0), **once),
            row(S5_W), row(SG_W), row(SG_W),
            pl.BlockSpec((SG_HEADS, SG_CHUNK, SG_CHUNK), lambda i: (0, 0, 0)),
            pl.BlockSpec((SG_CHUNK, SG_W), lambda i: (0, 0)),
            pl.BlockSpec((S5_W + SG_W, D), lambda i: (0, 0), **once),
            row(D), row(D),
            pl.BlockSpec((D, ODD_IN - dkv), lambda i: (0, 0), **once),
            pl.BlockSpec((dkv, D), lambda i: (0, 0), **once),
            pl.BlockSpec((TN, LANES), tab),
            pl.BlockSpec((TN, LANES), tab),
        ],
        out_specs=[
            pl.BlockSpec((TN, D), tail_blk),
            pl.BlockSpec((TN, dq), proj_blk),
            pl.BlockSpec((TN, dkv), proj_blk),
            pl.BlockSpec((dkv, TN), lambda i: (0, jnp.maximum(i - 1, 0))),
            pl.BlockSpec((TN, dq), proj_blk),
        ],
        out_shape=[
            jax.ShapeDtypeStruct((R_ALL, D), F32),
            jax.ShapeDtypeStruct((R_ALL, dq), BF16),
            jax.ShapeDtypeStruct((R_ALL, dkv), BF16),
            jax.ShapeDtypeStruct((dkv, R_ALL), BF16),
            jax.ShapeDtypeStruct((R_ALL, dq), BF16),
        ],
        scratch_shapes=[pltpu.VMEM((TN, D), F32)],
        compiler_params=pltpu.CompilerParams(vmem_limit_bytes=VMEM_LIMIT),
        name="mid",
    )(x2, ctx2, mod0, mod1, ys_lat, ys_ctx, rest, glu_w, glu_b, sg_g, sg_b, sg_w, sg_bias, w_out,
      ln_g, ln_b, w_qkg, w_vt, cos_t, sin_t)


N_QBLK = N_LAT // ATT_BLK
GRP = N_HEADS // N_KV
ATT_SUB = 2
ATT_ROWS = ATT_SUB * ATT_BLK
N_QSTEP = N_QBLK // ATT_SUB
ONES_ROWS = 16


def _attn_kernel(sink_ref, q_ref, kp_ref, kc_ref, kn_ref, kx_ref, vp_ref, vc_ref, vn_ref, vx_ref,
                 o_ref):
    i = pl.program_id(1)
    n_win = 3 * ATT_BLK
    n_keys = n_win + N_CTX
    nq = GRP * ATT_BLK
    k_win = jnp.concatenate([kp_ref[...], kc_ref[...], kn_ref[...]], axis=0)
    vt_win = jnp.concatenate([vp_ref[...], vc_ref[...], vn_ref[...]], axis=1)
    k_ctx = kx_ref[...]
    vt_ctx = vx_ref[...]

    kpos = lax.broadcasted_iota(jnp.int32, (ATT_BLK, ATT_BLK), 0)
    qpos = lax.broadcasted_iota(jnp.int32, (ATT_BLK, ATT_BLK), 1)
    tile_q = lambda a: jnp.concatenate([a] * GRP, axis=1)
    qgrp = lax.broadcasted_iota(jnp.int32, (1, nq), 1) // ATT_BLK
    ones = jnp.ones((ONES_ROWS, n_keys), BF16)
    q = q_ref[...]

    units = [(u, h) for u in range(ATT_SUB) for h in range(N_KV)]
    k_all, vt_all, bias = [], [], []
    for u in range(ATT_SUB):
        k_all.append(jnp.concatenate([k_win[u * ATT_BLK:u * ATT_BLK + n_win], k_ctx], axis=0))
        vt_all.append(jnp.concatenate([vt_win[:, u * ATT_BLK:u * ATT_BLK + n_win], vt_ctx], axis=1))
        has_prev = (i > 0) if u == 0 else True
        has_next = (i < N_QSTEP - 1) if u == ATT_SUB - 1 else True
        bias.append((tile_q(jnp.where((kpos >= qpos) & has_prev, 0.0, NEG_INF)),
                     tile_q(jnp.where((kpos <= qpos) & has_next, 0.0, NEG_INF))))
    scores = []
    for u, h in units:
        kh = k_all[u][:, h * HD:(h + 1) * HD]
        qh = jnp.concatenate(
            [q[u * ATT_BLK:(u + 1) * ATT_BLK, (h * GRP + g) * HD:(h * GRP + g + 1) * HD]
             for g in range(GRP)], axis=0)
        scores.append(lax.dot_general(kh, qh, (((1,), (1,)), ((), ())),
                                      preferred_element_type=F32))
    probs = []
    for (u, h), s in zip(units, scores):
        s = jnp.concatenate([s[:ATT_BLK] + bias[u][0], s[ATT_BLK:2 * ATT_BLK],
                             s[2 * ATT_BLK:n_win] + bias[u][1], s[n_win:]], axis=0)
        sink = jnp.zeros((1, nq), F32)
        for g in range(GRP):
            sink = jnp.where(qgrp == g, sink_ref[h * GRP + g] * LOG2E, sink)
        m = jnp.maximum(jnp.max(s, axis=0, keepdims=True), sink)
        probs.append((jnp.exp2(s - m).astype(BF16), jnp.exp2(sink - m)))
    outs = [[] for _ in range(ATT_SUB)]
    for (u, h), (p, p_sink) in zip(units, probs):
        vt1 = jnp.concatenate([vt_all[u][h * HD:(h + 1) * HD], ones], axis=0)
        ov = jnp.dot(vt1, p, preferred_element_type=F32)
        o_t = ov[:HD] / (ov[HD:HD + 1] + p_sink)
        outs[u].extend(o_t[:, g * ATT_BLK:(g + 1) * ATT_BLK] for g in range(GRP))
    for u in range(ATT_SUB):
        o_ref[u * ATT_BLK:(u + 1) * ATT_BLK, :] = jnp.transpose(
            jnp.concatenate(outs[u], axis=0)).astype(BF16)


def _attention(sink, q, k, vt):
    dq = N_HEADS * HD
    dkv = N_KV * HD
    ctx_blk0 = R_LAT // N_CTX
    cur = lambda b, i: b * N_QSTEP + i
    prev_blk = lambda b, i: b * N_QBLK + jnp.maximum(ATT_SUB * i - 1, 0)
    next_blk = lambda b, i: b * N_QBLK + jnp.minimum(ATT_SUB * (i + 1), N_QBLK - 1)
    return pl.pallas_call(
        _attn_kernel,
        grid_spec=pltpu.PrefetchScalarGridSpec(
            num_scalar_prefetch=1,
            grid=(B, N_QSTEP),
            in_specs=[
                pl.BlockSpec((ATT_ROWS, dq), lambda b, i, s: (cur(b, i), 0)),
                pl.BlockSpec((ATT_BLK, dkv), lambda b, i, s: (prev_blk(b, i), 0)),
                pl.BlockSpec((ATT_ROWS, dkv), lambda b, i, s: (cur(b, i), 0)),
                pl.BlockSpec((ATT_BLK, dkv), lambda b, i, s: (next_blk(b, i), 0)),
                pl.BlockSpec((N_CTX, dkv), lambda b, i, s: (ctx_blk0 + b, 0)),
                pl.BlockSpec((dkv, ATT_BLK), lambda b, i, s: (0, prev_blk(b, i))),
                pl.BlockSpec((dkv, ATT_ROWS), lambda b, i, s: (0, cur(b, i))),
                pl.BlockSpec((dkv, ATT_BLK), lambda b, i, s: (0, next_blk(b, i))),
                pl.BlockSpec((dkv, N_CTX), lambda b, i, s: (0, ctx_blk0 + b)),
            ],
            out_specs=pl.BlockSpec((ATT_ROWS, dq), lambda b, i, s: (cur(b, i), 0)),
        ),
        out_shape=jax.ShapeDtypeStruct((R_LAT, dq), BF16),
        compiler_params=pltpu.CompilerParams(vmem_limit_bytes=VMEM_LIMIT),
        name="attention",
    )(sink, q, k, k, k, k, vt, vt, vt, vt)


def _tail1_kernel(z_ref, mod_ref, o_ref_in, g_ref, wout_ref, lng_ref, lnb_ref, out_ref):
    i = pl.program_id(0)
    gate = mod_ref[pl.ds(_block_mod_row(i), 1), 2 * D:3 * D]
    g = g_ref[...].astype(F32)
    mix = (o_ref_in[...].astype(F32) * _silu(g)).astype(BF16)
    y = jnp.dot(mix, wout_ref[...], preferred_element_type=F32) * gate
    out_ref[...] = _layer_norm(ALPHA * z_ref[...] + y, lng_ref[...], lnb_ref[...])


def _tail1(z1, mod, o, g, w_out, ln_g, ln_b):
    row = lambda n: pl.BlockSpec((1, n), lambda i: (0, 0))
    return pl.pallas_call(
        _tail1_kernel,
        grid=(N_LAT_BLK,),
        in_specs=[
            pl.BlockSpec((TN, D), lambda i: (i, 0)),
            pl.BlockSpec((8, 3 * D), lambda i: (0, 0)),
            pl.BlockSpec((TN, D), lambda i: (i, 0)),
            pl.BlockSpec((TN, D), lambda i: (i, 0)),
            pl.BlockSpec((D, D), lambda i: (0, 0)),
            row(D), row(D),
        ],
        out_specs=pl.BlockSpec((TN, D), lambda i: (i, 0)),
        out_shape=jax.ShapeDtypeStruct((R_LAT, D), F32),
        compiler_params=pltpu.CompilerParams(vmem_limit_bytes=VMEM_LIMIT),
        name="tail1",
    )(z1, mod, o, g, w_out, ln_g, ln_b)


def kernel(x, c, ctx, c_ctx, mod_w, mod_b, ln_g, ln_b, e_w_in, e_w_out, s5_lam_re, s5_lam_im,
           s5_log_dt, s5_b_re, s5_b_im, s5_c_re, s5_c_im, s5_d, glu_w, glu_b, sg_ln_g, sg_ln_b,
           sg_w, sg_b, o_w_in, o_w_out, o_sink):
    x2 = x.reshape(R_LAT, D)
    ctx2 = ctx.reshape(R_CTX, D)
    mod = _modulation(c, c_ctx, mod_w, mod_b)

    xa, rest = _in0(x2, ctx2, mod[0], e_w_in[0].astype(BF16))
    t_mat, e_mat, ft_mat, al_re, al_im = _s5_prep(
        s5_lam_re[0], s5_lam_im[0], s5_log_dt[0], s5_b_re[0], s5_b_im[0], s5_c_re[0], s5_c_im[0],
        s5_d[0])
    st_loc = _s5_states(xa, e_mat)
    st_in = _s5_scan(st_loc, al_re, al_im)
    ys_lat, ys_ctx = _s5_out(xa, st_in, t_mat, ft_mat)
    sg_bias = jnp.repeat(jnp.transpose(sg_b[0]), SG_W // SG_HEADS, axis=1)
    cos_t, sin_t = _rope_tables()
    z1, q, k, vt, g = _mid(
        x2, ctx2, mod[0], mod[1], ys_lat, ys_ctx, rest, glu_w[0].astype(BF16),
        glu_b[0].reshape(1, S5_W), sg_ln_g[0].reshape(1, SG_W), sg_ln_b[0].reshape(1, SG_W),
        sg_w[0].astype(BF16), sg_bias, e_w_out[0].astype(BF16), ln_g[0].reshape(1, D),
        ln_b[0].reshape(1, D), o_w_in[0], cos_t, sin_t)

    o = _attention(o_sink[0], q, k, vt)
    out = _tail1(z1, mod[1], o, g, o_w_out[0].astype(BF16), ln_g[1].reshape(1, D),
                 ln_b[1].reshape(1, D))
    return out.reshape(B, N_LAT, D)
```

```python
import functools
import math

import jax
import jax.numpy as jnp
from jax import lax
from jax.experimental import pallas as pl
from jax.experimental.pallas import tpu as pltpu

F32 = jnp.float32
BF16 = jnp.bfloat16

D = 1024
B = 2
N_LAT = 8192
N_CTX = 256
DEPTH = 2
GRID_W = 64
S5_W = 512
S5_GC = 16
S5_G = 32
S5_P = 64
SG_W = 512
SG_HEADS = 8
SG_CHUNK = 128
N_HEADS = 16
N_KV = 4
HD = 64
WINDOW = 128
ATT_BLK = 128
ROPE_BASE = 10000.0
NEG_INF = -1e30
LN_EPS = 1e-5
ALPHA = (2 * DEPTH) ** 0.25
LOG2E = math.log2(math.e)
EVEN_IN = 2 * S5_W + 3 * SG_W
ODD_IN = 2 * N_HEADS * HD + 2 * N_KV * HD

LANES = 128
VMEM_LIMIT = 56 * 1024 * 1024

R_LAT = B * N_LAT
R_CTX = B * N_CTX
R_ALL = R_LAT + R_CTX
TN = 512
N_LAT_BLK = R_LAT // TN
N_BLK = R_ALL // TN
LAT_BLK_PER_BATCH = N_LAT // TN
CTX_MOD_ROW = B

S5_CHUNK = 8
S5_QT = S5_W // LANES
S5_GPT = LANES // S5_GC
S5_ST = S5_GPT * S5_P
S5_PAIRS = S5_GPT // 2
PAIR_BLK = 2 * S5_GC
LAT_CHUNKS = N_LAT // S5_CHUNK
CTX_CHUNKS = N_CTX // S5_CHUNK
BATCH_CHUNKS = CTX_CHUNKS + LAT_CHUNKS
N_CHUNK = B * BATCH_CHUNKS
N_SEG = 8
SEG_CHUNKS = BATCH_CHUNKS // N_SEG
BWD_RESET_STEP = LAT_CHUNKS - (N_SEG - 1) * SEG_CHUNKS


def _sigmoid(x):
    return 1.0 / (1.0 + jnp.exp(-x))


def _silu(x):
    return x * _sigmoid(x)


def _gelu_tanh(x):
    return 0.5 * x * (1.0 + jnp.tanh(math.sqrt(2.0 / math.pi) * (x + 0.044715 * (x * x * x))))


def _layer_norm(x, g, b):
    mu = jnp.mean(x, axis=-1, keepdims=True)
    xc = x - mu
    var = jnp.mean(xc * xc, axis=-1, keepdims=True)
    return xc * lax.rsqrt(var + LN_EPS) * g + b


def _split_bf16(x):
    hi = x.astype(BF16)
    return hi, (x - hi.astype(F32)).astype(BF16)


def _dot_split(x, y_split, dims):
    x_hi, x_lo = _split_bf16(x)
    y_hi, y_lo = y_split
    dot = lambda a, b: lax.dot_general(a, b, dims, preferred_element_type=F32)
    return dot(x_hi, y_hi) + (dot(x_hi, y_lo) + dot(x_lo, y_hi))


def _block_mod_row(i):
    return jnp.where(i < N_LAT_BLK, i // LAT_BLK_PER_BATCH, CTX_MOD_ROW)


def _mod_kernel(cv_ref, w_ref, b_ref, o_ref):
    s = _silu(cv_ref[...])
    o_ref[0] = _dot_split(s, _split_bf16(w_ref[0]), (((1,), (0,)), ((), ()))) + b_ref[0]


def _modulation(c, c_ctx, mod_w, mod_b):
    cv = jnp.concatenate([c, c_ctx[None], jnp.zeros((8 - B - 1, D), F32)], axis=0)
    tn = 1024
    return pl.pallas_call(
        _mod_kernel,
        grid=(DEPTH, 3 * D // tn),
        in_specs=[
            pl.BlockSpec((8, D), lambda l, j: (0, 0)),
            pl.BlockSpec((1, D, tn), lambda l, j: (l, 0, j)),
            pl.BlockSpec((1, 1, tn), lambda l, j: (l, 0, j)),
        ],
        out_specs=pl.BlockSpec((1, 8, tn), lambda l, j: (l, 0, j)),
        out_shape=jax.ShapeDtypeStruct((DEPTH, 8, 3 * D), F32),
        compiler_params=pltpu.CompilerParams(vmem_limit_bytes=VMEM_LIMIT),
        name="modulation",
    )(cv, mod_w, mod_b.reshape(DEPTH, 1, 3 * D))


def _in0_kernel(x_ref, ctx_ref, mod_ref, w_ref, xa_ref, rest_ref):
    i = pl.program_id(0)
    z = jnp.where(i < N_LAT_BLK, x_ref[...], ctx_ref[...])
    m = mod_ref[pl.ds(_block_mod_row(i), 1), :]
    h = z * (1.0 + m[:, D:2 * D]) + m[:, :D]
    p = jnp.dot(h.astype(BF16), w_ref[...], preferred_element_type=F32)
    xa_ref[...] = p[:, :S5_W]
    rest_ref[...] = p[:, S5_W:].astype(BF16)


def _in0(x2, ctx2, mod, w_in):
    return pl.pallas_call(
        _in0_kernel,
        grid=(N_BLK,),
        in_specs=[
            pl.BlockSpec((TN, D), lambda i: (jnp.minimum(i, N_LAT_BLK - 1), 0)),
            pl.BlockSpec((R_CTX, D), lambda i: (0, 0)),
            pl.BlockSpec((8, 3 * D), lambda i: (0, 0)),
            pl.BlockSpec((D, EVEN_IN), lambda i: (0, 0)),
        ],
        out_specs=[
            pl.BlockSpec((TN, S5_W), lambda i: (i, 0)),
            pl.BlockSpec((TN, EVEN_IN - S5_W), lambda i: (i, 0)),
        ],
        out_shape=[
            jax.ShapeDtypeStruct((R_ALL, S5_W), F32),
            jax.ShapeDtypeStruct((R_ALL, EVEN_IN - S5_W), BF16),
        ],
        compiler_params=pltpu.CompilerParams(vmem_limit_bytes=VMEM_LIMIT),
        name="in0",
    )(x2, ctx2, mod, w_in)


def _s5_prep_kernel(rows_ref, bt_ref, ct_ref, dsk_ref, t_ref, e_ref, ft_ref, alr_ref, ali_ref):
    L = S5_CHUNK
    tile = (LANES, S5_ST)
    same_group = (lax.broadcasted_iota(jnp.int32, tile, 0) // S5_GC
                  == lax.broadcasted_iota(jnp.int32, tile, 1) // S5_P)
    nt_dims = (((1,), (1,)), ((), ()))
    pair_rows = lambda pr: slice(pr * PAIR_BLK, (pr + 1) * PAIR_BLK)
    pair_lanes = lambda pr: slice(pr * LANES, (pr + 1) * LANES)
    lag_strip = []
    for d in range(2):
        lam_re = rows_ref[d, 0]
        lam_im = rows_ref[d, 1]
        dt = jnp.exp(rows_ref[d, 2])
        pw = []
        for k in range(L + 1):
            mag = jnp.exp(lam_re * dt * float(k))
            ang = lam_im * dt * float(k)
            pw.append((mag * jnp.cos(ang), mag * jnp.sin(ang)))
        alr_ref[d] = pw[L][0]
        ali_ref[d] = pw[L][1]
        den = lam_re * lam_re + lam_im * lam_im
        nr = pw[1][0] - 1.0
        ni = pw[1][1]
        f_re = (nr * lam_re + ni * lam_im) / den
        f_im = (ni * lam_re - nr * lam_im) / den
        b_re = jnp.where(same_group, bt_ref[0, d], 0.0)
        b_im = jnp.where(same_group, bt_ref[1, d], 0.0)
        bb_re = f_re * b_re - f_im * b_im
        bb_im = f_re * b_im + f_im * b_re
        c_re = jnp.where(same_group, ct_ref[0, d], 0.0)
        c_im = jnp.where(same_group, ct_ref[1, d], 0.0)
        for k in range(L):
            x_re = pw[k][0] * bb_re - pw[k][1] * bb_im
            x_im = pw[k][0] * bb_im + pw[k][1] * bb_re
            s = (L - 1 - k) if d == 0 else k
            for pr in range(S5_PAIRS):
                blk = slice(s * PAIR_BLK, (s + 1) * PAIR_BLK)
                e_ref[0, pr, blk, 2 * d * LANES:(2 * d + 1) * LANES] = (
                    x_re[pair_rows(pr), pair_lanes(pr)].astype(BF16))
                e_ref[0, pr, blk, (2 * d + 1) * LANES:(2 * d + 2) * LANES] = (
                    x_im[pair_rows(pr), pair_lanes(pr)].astype(BF16))
        cw = []
        for k in range(L + 1):
            w_re = c_re * pw[k][0] - c_im * pw[k][1]
            w_im = c_re * pw[k][1] + c_im * pw[k][0]
            cw.append((w_re, -w_im))
        for t in range(L):
            k = (t + 1) if d == 0 else (L - t)
            for pr in range(S5_PAIRS):
                blk = slice(t * PAIR_BLK, (t + 1) * PAIR_BLK)
                ft_ref[0, pr, blk, 2 * d * LANES:(2 * d + 1) * LANES] = (
                    cw[k][0][pair_rows(pr), pair_lanes(pr)].astype(BF16))
                ft_ref[0, pr, blk, (2 * d + 1) * LANES:(2 * d + 2) * LANES] = (
                    cw[k][1][pair_rows(pr), pair_lanes(pr)].astype(BF16))
        lags = range(L) if d == 0 else range(L - 1, -1, -1)
        c_stack = jnp.concatenate(
            [jnp.concatenate([cw[k][0][pair_rows(pr)], cw[k][1][pair_rows(pr)]], axis=-1)
             for pr in range(S5_PAIRS) for k in lags], axis=0)
        lag_strip.append(_dot_split(jnp.concatenate([bb_re, bb_im], axis=-1),
                                    _split_bf16(c_stack), nt_dims))

    strip_w = 2 * L * PAIR_BLK
    mid = (L - 1) * PAIR_BLK
    widen = lambda a: jnp.concatenate([a, jnp.zeros((a.shape[0], strip_w - a.shape[1]), F32)], axis=-1)
    row = lax.broadcasted_iota(jnp.int32, (PAIR_BLK, strip_w), 0)
    lane = lax.broadcasted_iota(jnp.int32, (PAIR_BLK, strip_w), 1)
    d_row = widen(dsk_ref[0])
    for pr in range(S5_PAIRS):
        lanes = slice(pr * L * PAIR_BLK, (pr + 1) * L * PAIR_BLK)
        strip = (widen(lag_strip[1][pair_rows(pr), lanes])
                 + pltpu.roll(widen(lag_strip[0][pair_rows(pr), lanes]), mid, axis=1)
                 + jnp.where(lane == row + mid,
                             pltpu.roll(d_row, mid - pr * PAIR_BLK, axis=1), 0.0))
        for s in range(L):
            shift = (strip_w - (L - 1 - s) * PAIR_BLK) % strip_w
            blk = pltpu.roll(strip, shift, axis=1) if shift else strip
            t_ref[0, pr, s * PAIR_BLK:(s + 1) * PAIR_BLK, :] = blk[:, :L * PAIR_BLK].astype(BF16)


def _s5_prep(lam_re, lam_im, log_dt, b_re, b_im, c_re, c_im, d_skip):
    L = S5_CHUNK
    n_st = S5_G * S5_P
    pair_w = L * PAIR_BLK
    rows = jnp.stack([lam_re.reshape(2, 1, n_st), lam_im.reshape(2, 1, n_st),
                      jnp.repeat(log_dt, S5_P, axis=1).reshape(2, 1, n_st)], axis=1)
    rep = lambda a: jnp.tile(a.reshape(2, S5_W, S5_P), (1, 1, S5_GPT))
    bt = jnp.stack([rep(jnp.swapaxes(b_re, 2, 3)), rep(jnp.swapaxes(b_im, 2, 3))])
    ct = jnp.stack([rep(c_re), rep(c_im)])
    return pl.pallas_call(
        _s5_prep_kernel,
        grid=(S5_QT,),
        in_specs=[
            pl.BlockSpec((2, 3, 1, S5_ST), lambda q: (0, 0, 0, q)),
            pl.BlockSpec((2, 2, LANES, S5_ST), lambda q: (0, 0, q, 0)),
            pl.BlockSpec((2, 2, LANES, S5_ST), lambda q: (0, 0, q, 0)),
            pl.BlockSpec((1, 1, LANES), lambda q: (q, 0, 0)),
        ],
        out_specs=[
            pl.BlockSpec((1, S5_PAIRS, pair_w, pair_w), lambda q: (q, 0, 0, 0)),
            pl.BlockSpec((1, S5_PAIRS, pair_w, 4 * LANES), lambda q: (q, 0, 0, 0)),
            pl.BlockSpec((1, S5_PAIRS, pair_w, 4 * LANES), lambda q: (q, 0, 0, 0)),
            pl.BlockSpec((2, 1, S5_ST), lambda q: (q, 0, 0)),
            pl.BlockSpec((2, 1, S5_ST), lambda q: (q, 0, 0)),
        ],
        out_shape=[
            jax.ShapeDtypeStruct((S5_QT, S5_PAIRS, pair_w, pair_w), BF16),
            jax.ShapeDtypeStruct((S5_QT, S5_PAIRS, pair_w, 4 * LANES), BF16),
            jax.ShapeDtypeStruct((S5_QT, S5_PAIRS, pair_w, 4 * LANES), BF16),
            jax.ShapeDtypeStruct((2 * S5_QT, 1, S5_ST), F32),
            jax.ShapeDtypeStruct((2 * S5_QT, 1, S5_ST), F32),
        ],
        compiler_params=pltpu.CompilerParams(vmem_limit_bytes=VMEM_LIMIT),
        name="s5_prep",
    )(rows, bt, ct, d_skip.reshape(S5_QT, 1, LANES))


def _block_transpose4(tiles):
    tiles = list(tiles)
    blk = lax.broadcasted_iota(jnp.int32, tiles[0].shape, 1) // PAIR_BLK
    for dist in (2, 1):
        keep = (blk & dist) == 0
        for i in range(len(tiles)):
            if i & dist:
                continue
            lo, hi = tiles[i], tiles[i + dist]
            tiles[i] = jnp.where(keep, lo, pltpu.roll(hi, dist * PAIR_BLK, axis=1))
            tiles[i + dist] = jnp.where(keep, pltpu.roll(lo, LANES - dist * PAIR_BLK, axis=1), hi)
    return tiles


def _pair_rows(xl_ref, xc_ref):
    L = S5_CHUNK
    tiles = [jnp.concatenate([xc_ref[pl.ds(t, CTX_CHUNKS, stride=L), :],
                              xl_ref[pl.ds(t, LAT_CHUNKS, stride=L), :]], axis=0) for t in range(L)]
    halves = [_block_transpose4(tiles[h * S5_PAIRS:(h + 1) * S5_PAIRS]) for h in range(L // S5_PAIRS)]
    return [jnp.concatenate([half[pr] for half in halves], axis=-1).astype(BF16)
            for pr in range(S5_PAIRS)]


def _xa_specs(index_of):
    ctx_blk0 = R_LAT // N_CTX
    return [
        pl.BlockSpec((N_LAT, LANES), lambda *g: (index_of(*g)[1], index_of(*g)[0])),
        pl.BlockSpec((N_CTX, LANES), lambda *g: (ctx_blk0 + index_of(*g)[1], index_of(*g)[0])),
    ]


N_SLAB = S5_ST // LANES
ST_SHAPE = (S5_QT * 2 * 2 * N_SLAB, N_CHUNK, LANES)
ST_SPEC = pl.BlockSpec((2 * N_SLAB, BATCH_CHUNKS, LANES), lambda q, b, d: (2 * q + d, b, 0))


def _s5_states_kernel(xl_ref, xc_ref, e_ref, st_ref):
    for pr, rows in enumerate(_pair_rows(xl_ref, xc_ref)):
        st = jnp.dot(rows, e_ref[0, pr], preferred_element_type=F32)
        for comp in range(4):
            st_ref[comp * N_SLAB + pr] = st[:, comp * LANES:(comp + 1) * LANES]


def _s5_states(xa, e_mat):
    pair_w = S5_CHUNK * PAIR_BLK
    return pl.pallas_call(
        _s5_states_kernel,
        grid=(S5_QT, B),
        in_specs=_xa_specs(lambda q, b: (q, b)) + [
            pl.BlockSpec((1, S5_PAIRS, pair_w, 4 * LANES), lambda q, b: (q, 0, 0, 0)),
        ],
        out_specs=pl.BlockSpec((4 * N_SLAB, BATCH_CHUNKS, LANES), lambda q, b: (q, b, 0)),
        out_shape=jax.ShapeDtypeStruct(ST_SHAPE, F32),
        compiler_params=pltpu.CompilerParams(vmem_limit_bytes=VMEM_LIMIT),
        name="s5_states",
    )(xa, xa, e_mat)


def _cmul(ar, ai, br, bi):
    return ar * br - ai * bi, ar * bi + ai * br


def _cpow(ar, ai, n):
    res = None
    while n:
        if n & 1:
            res = (ar, ai) if res is None else _cmul(res[0], res[1], ar, ai)
        n >>= 1
        if n:
            ar, ai = _cmul(ar, ai, ar, ai)
    return res


def _s5_scan_kernel(loc_ref, ar_ref, ai_ref, out_ref):
    chains = [(b, k) for b in range(B) for k in range(N_SLAB)]

    def run(backward):
        a_row = [(ar_ref[0][:, k * LANES:(k + 1) * LANES], ai_ref[0][:, k * LANES:(k + 1) * LANES])
                 for k in range(N_SLAB)]
        a_tile = [(jnp.broadcast_to(r, (N_SEG, LANES)), jnp.broadcast_to(i, (N_SEG, LANES)))
                  for r, i in a_row]

        def sweep(lo, hi, carry, store):
            def step(i, carry):
                off = (SEG_CHUNKS - 1 - i) if backward else i
                new = []
                for (b, k), (cr, ci) in zip(chains, carry):
                    rows = pl.ds(b * BATCH_CHUNKS + off, N_SEG, stride=SEG_CHUNKS)
                    xr = loc_ref[k, rows, :]
                    xi = loc_ref[N_SLAB + k, rows, :]
                    if store:
                        out_ref[k, rows, :] = cr
                        out_ref[N_SLAB + k, rows, :] = ci
                    nr, ni = _cmul(a_tile[k][0], a_tile[k][1], cr, ci)
                    new.append((nr + xr, ni + xi))
                return tuple(new)
            return lax.fori_loop(lo, hi, step, carry)

        def full(carry, store):
            if not backward:
                return sweep(0, SEG_CHUNKS, carry, store)
            carry = sweep(0, BWD_RESET_STEP, carry, store)
            first = lax.broadcasted_iota(jnp.int32, (N_SEG, LANES), 0) == 0
            carry = tuple((jnp.where(first, 0.0, cr), jnp.where(first, 0.0, ci)) for cr, ci in carry)
            return sweep(BWD_RESET_STEP, SEG_CHUNKS, carry, store)

        zero_tile = jnp.zeros((N_SEG, LANES), F32)
        ends = full(tuple((zero_tile, zero_tile) for _ in chains), False)

        starts = []
        for (b, k), (er, ei) in zip(chains, ends):
            pr, pi = _cpow(a_row[k][0], a_row[k][1], SEG_CHUNKS)
            if backward:
                order = range(N_SEG - 1, -1, -1)
                g = (er[0:1], ei[0:1])
            else:
                order = range(N_SEG)
                g = (jnp.zeros((1, LANES), F32), jnp.zeros((1, LANES), F32))
            rows_r = [None] * N_SEG
            rows_i = [None] * N_SEG
            for j in order:
                rows_r[j], rows_i[j] = g
                nr, ni = _cmul(pr, pi, g[0], g[1])
                g = (nr + er[j:j + 1], ni + ei[j:j + 1])
            starts.append((jnp.concatenate(rows_r, axis=0), jnp.concatenate(rows_i, axis=0)))
        full(tuple(starts), True)

    backward = pl.program_id(0) % 2
    pl.when(backward == 0)(lambda: run(False))
    pl.when(backward == 1)(lambda: run(True))


def _s5_scan(st_loc, al_re, al_im):
    slabs = pl.BlockSpec((2 * N_SLAB, N_CHUNK, LANES), lambda j: (j, 0, 0))
    return pl.pallas_call(
        _s5_scan_kernel,
        grid=(2 * S5_QT,),
        in_specs=[
            slabs,
            pl.BlockSpec((1, 1, S5_ST), lambda j: (j, 0, 0)),
            pl.BlockSpec((1, 1, S5_ST), lambda j: (j, 0, 0)),
        ],
        out_specs=slabs,
        out_shape=jax.ShapeDtypeStruct(ST_SHAPE, F32),
        compiler_params=pltpu.CompilerParams(vmem_limit_bytes=VMEM_LIMIT),
        name="s5_scan",
    )(st_loc, al_re, al_im)


def _s5_out_kernel(xl_ref, xc_ref, st_ref, t_ref, ft_ref, yl_ref, yc_ref, acc_ref):
    L = S5_CHUNK
    d = pl.program_id(2)

    def from_states(pr):
        st = jnp.concatenate([st_ref[pr], st_ref[N_SLAB + pr]], axis=-1).astype(BF16)
        return lax.dot_general(st, ft_ref[0, pr], (((1,), (1,)), ((), ())),
                               preferred_element_type=F32)

    @pl.when(d == 0)
    def _():
        for pr, rows in enumerate(_pair_rows(xl_ref, xc_ref)):
            acc_ref[pr] = from_states(pr) + jnp.dot(rows, t_ref[0, pr],
                                                    preferred_element_type=F32)

    @pl.when(d == 1)
    def _():
        ys = [acc_ref[pr] + from_states(pr) for pr in range(S5_PAIRS)]
        tiles = [tile for h in range(L // S5_PAIRS) for tile in _block_transpose4(
            [y[:, h * LANES:(h + 1) * LANES] for y in ys])]
        for t, tile in enumerate(tiles):
            yc_ref[pl.ds(t, CTX_CHUNKS, stride=L), :] = tile[:CTX_CHUNKS]
            yl_ref[pl.ds(t, LAT_CHUNKS, stride=L), :] = tile[CTX_CHUNKS:]


def _s5_out(xa, st_in, t_mat, ft_mat):
    pair_w = S5_CHUNK * PAIR_BLK
    return pl.pallas_call(
        _s5_out_kernel,
        grid=(S5_QT, B, 2),
        in_specs=_xa_specs(lambda q, b, d: (q, b)) + [
            ST_SPEC,
            pl.BlockSpec((1, S5_PAIRS, pair_w, pair_w), lambda q, b, d: (q, 0, 0, 0)),
            pl.BlockSpec((1, S5_PAIRS, pair_w, 2 * LANES), lambda q, b, d: (q, 0, 0, d)),
        ],
        out_specs=[
            pl.BlockSpec((N_LAT, LANES), lambda q, b, d: (b, q)),
            pl.BlockSpec((N_CTX, LANES), lambda q, b, d: (b, q)),
        ],
        out_shape=[
            jax.ShapeDtypeStruct((R_LAT, S5_W), F32),
            jax.ShapeDtypeStruct((R_CTX, S5_W), F32),
        ],
        scratch_shapes=[pltpu.VMEM((S5_PAIRS, BATCH_CHUNKS, pair_w), F32)],
        compiler_params=pltpu.CompilerParams(vmem_limit_bytes=VMEM_LIMIT),
        name="s5_out",
    )(xa, xa, st_in, t_mat, ft_mat)


def _tail0_stages(z, ys, rest_ref, gate, gluw_ref, glub_ref, sg_g_ref, sg_b_ref, sgw_ref,
                  sgbias_ref, wout_ref, lng_ref, lnb_ref):
    ga = rest_ref[:, 0:S5_W].astype(F32)
    u = rest_ref[:, S5_W:2 * S5_W].astype(F32)
    v = rest_ref[:, 2 * S5_W:3 * S5_W].astype(F32)
    gb = rest_ref[:, 3 * S5_W:4 * S5_W].astype(F32)

    ya = _gelu_tanh(ys)
    glu = jnp.dot(ya.astype(BF16), gluw_ref[...], preferred_element_type=F32) + glub_ref[...]
    yield None
    ya = ya * _sigmoid(glu) * _silu(ga)
    yield None

    vn = _layer_norm(v, sg_g_ref[...], sg_b_ref[...]).astype(BF16)
    lane = lax.broadcasted_iota(jnp.int32, (SG_CHUNK, LANES), 1)
    first_head = lane < (SG_W // SG_HEADS)
    chunks = []
    for c in range(TN // SG_CHUNK):
        tiles = []
        for j in range(SG_W // LANES):
            vt = vn[c * SG_CHUNK:(c + 1) * SG_CHUNK, j * LANES:(j + 1) * LANES]
            r0 = jnp.dot(sgw_ref[2 * j], vt, preferred_element_type=F32)
            r1 = jnp.dot(sgw_ref[2 * j + 1], vt, preferred_element_type=F32)
            tiles.append(jnp.where(first_head, r0, r1))
        chunks.append(jnp.concatenate(tiles, axis=-1) + sgbias_ref[...])
    s = jnp.concatenate(chunks, axis=0)
    yb = u * s * _silu(gb)
    yield None

    mix = jnp.concatenate([ya, yb], axis=-1).astype(BF16)
    y = jnp.dot(mix, wout_ref[...], preferred_element_type=F32) * gate
    yield None
    yield _layer_norm(ALPHA * z + y, lng_ref[...], lnb_ref[...])


def _rope_tables():
    nf = HD // 4
    n_rows = N_LAT // GRID_W
    lane = jnp.arange(LANES)
    inv = ROPE_BASE ** (-(lane % nf).astype(F32) / nf)
    by_row = ((lane % HD) // (HD // 2) == 0)[None, :]
    sign = jnp.where((lane % (HD // 2)) < nf, -1.0, 1.0)[None, :]
    row_ang = jnp.arange(n_rows, dtype=F32)[:, None] * inv[None, :]
    col_ang = jnp.arange(GRID_W, dtype=F32)[:, None] * inv[None, :]
    zero = jnp.zeros((), F32)
    cos_t = (jnp.where(by_row, jnp.cos(row_ang), zero)[:, None, :]
             + jnp.where(by_row, zero, jnp.cos(col_ang))[None, :, :]).reshape(N_LAT, LANES)
    sin_t = (jnp.where(by_row, sign * jnp.sin(row_ang), zero)[:, None, :]
             + jnp.where(by_row, zero, sign * jnp.sin(col_ang))[None, :, :]).reshape(N_LAT, LANES)
    return cos_t, sin_t


def _rope_tile(x, cos, sin, first_half):
    nf = HD // 4
    partner = jnp.where(first_half, pltpu.roll(x, LANES - nf, axis=1), pltpu.roll(x, nf, axis=1))
    return x * cos + partner * sin


def _mid_kernel(x_ref, ctx_ref, mod0_ref, mod1_ref, ysl_ref, ysc_ref, rest_ref, gluw_ref, glub_ref,
                sg_g_ref, sg_b_ref, sgw_ref, sgbias_ref, wout_ref, lng_ref, lnb_ref, w_ref,
                wvt_ref, cos_ref, sin_ref, z1_ref, q_ref, k_ref, vt_ref, g_ref, zprev_ref):
    i = pl.program_id(0)
    dq = N_HEADS * HD
    dkv = N_KV * HD

    @pl.when(i == 0)
    def _():
        zprev_ref[...] = jnp.zeros((TN, D), F32)

    j = jnp.maximum(i - 1, 0)
    m1 = mod1_ref[pl.ds(_block_mod_row(j), 1), :]
    h = (zprev_ref[...] * (1.0 + m1[:, D:2 * D]) + m1[:, :D]).astype(BF16)
    cos = jnp.where(j < N_LAT_BLK, cos_ref[...], 1.0)
    sin = jnp.where(j < N_LAT_BLK, sin_ref[...], 0.0)
    lane = lax.broadcasted_iota(jnp.int32, (TN, LANES), 1)
    first_half = (lane % (HD // 2)) < (HD // 4)
    scale = HD ** -0.5 * LOG2E

    def project(c0, c1):
        return jnp.dot(h, w_ref[:, c0:c1], preferred_element_type=F32)

    def roped_tiles(p, mult):
        for c in range(p.shape[1] // LANES):
            r = _rope_tile(p[:, c * LANES:(c + 1) * LANES], cos, sin, first_half)
            yield c, (r * mult if mult != 1.0 else r).astype(BF16)

    t = jnp.minimum(i, N_BLK - 1)
    z = jnp.where(t < N_LAT_BLK, x_ref[...], ctx_ref[...])
    ys = jnp.where(t < N_LAT_BLK, ysl_ref[...], ysc_ref[...])
    gate = mod0_ref[pl.ds(_block_mod_row(t), 1), 2 * D:3 * D]
    tail = _tail0_stages(z, ys, rest_ref, gate, gluw_ref, glub_ref, sg_g_ref, sg_b_ref, sgw_ref,
                         sgbias_ref, wout_ref, lng_ref, lnb_ref)

    half_q = dq // 2
    p_q0 = project(0, half_q)
    next(tail)
    p_q1 = project(half_q, dq)
    next(tail)
    p_k = project(dq, dq + dkv)
    vt_ref[...] = lax.dot_general(wvt_ref[...], h, (((1,), (1,)), ((), ())),
                                  preferred_element_type=F32).astype(BF16)
    next(tail)
    next(tail)
    for c0, p in ((0, p_q0), (half_q, p_q1)):
        for c, r in roped_tiles(p, scale):
            q_ref[:, c0 + c * LANES:c0 + (c + 1) * LANES] = r
    for c, r in roped_tiles(p_k, 1.0):
        k_ref[:, c * LANES:(c + 1) * LANES] = r
    p_g = project(dq + dkv, ODD_IN - dkv)
    z1 = next(tail)
    z1_ref[...] = z1
    zprev_ref[...] = z1
    g_ref[...] = p_g.astype(BF16)


def _mid(x2, ctx2, mod0, mod1, ys_lat, ys_ctx, rest, glu_w, glu_b, sg_g, sg_b, sg_w, sg_bias,
         w_out, ln_g, ln_b, w_in1, cos_t, sin_t):
    dq = N_HEADS * HD
    dkv = N_KV * HD
    w_qkg = jnp.concatenate([w_in1[:, :dq + dkv], w_in1[:, dq + 2 * dkv:]], axis=1).astype(BF16)
    w_vt = jnp.transpose(w_in1[:, dq + dkv:dq + 2 * dkv]).astype(BF16)
    once = dict(pipeline_mode=pl.Buffered(1))
    row = lambda n: pl.BlockSpec((1, n), lambda i: (0, 0))
    lat = lambda i: (jnp.minimum(i, N_LAT_BLK - 1), 0)
    tail_blk = lambda i: (jnp.minimum(i, N_BLK - 1), 0)
    proj_blk = lambda i: (jnp.maximum(i - 1, 0), 0)
    tab = lambda i: (jnp.maximum(i - 1, 0) % LAT_BLK_PER_BATCH, 0)
    return pl.pallas_call(
        _mid_kernel,
        grid=(N_BLK + 1,),
        in_specs=[
            pl.BlockSpec((TN, D), lat),
            pl.BlockSpec((R_CTX, D), lambda i: (0, 0), **once),
            pl.BlockSpec((8, 3 * D), lambda i: (0, 0)),
            pl.BlockSpec((8, 3 * D), lambda i: (0, 0)),
            pl.BlockSpec((TN, S5_W), lat),
            pl.BlockSpec((R_CTX, S5_W), lambda i: (0, 0), **once),
            pl.BlockSpec((TN, EVEN_IN - S5_W), tail_blk),
            pl.BlockSpec((S5_W, S5_W), lambda i: (0, 0), **once),
            row(S5_W), row(SG_W), row(SG_W),
            pl.BlockSpec((SG_HEADS, SG_CHUNK, SG_CHUNK), lambda i: (0, 0, 0)),
            pl.BlockSpec((SG_CHUNK, SG_W), lambda i: (0, 0)),
            pl.BlockSpec((S5_W + SG_W, D), lambda i: (0, 0), **once),
            row(D), row(D),
            pl.BlockSpec((D, ODD_IN - dkv), lambda i: (0, 0), **once),
            pl.BlockSpec((dkv, D), lambda i: (0, 0), **once),
            pl.BlockSpec((TN, LANES), tab),
            pl.BlockSpec((TN, LANES), tab),
        ],
        out_specs=[
            pl.BlockSpec((TN, D), tail_blk),
            pl.BlockSpec((TN, dq), proj_blk),
            pl.BlockSpec((TN, dkv), proj_blk),
            pl.BlockSpec((dkv, TN), lambda i: (0, jnp.maximum(i - 1, 0))),
            pl.BlockSpec((TN, dq), proj_blk),
        ],
        out_shape=[
            jax.ShapeDtypeStruct((R_ALL, D), F32),
            jax.ShapeDtypeStruct((R_ALL, dq), BF16),
            jax.ShapeDtypeStruct((R_ALL, dkv), BF16),
            jax.ShapeDtypeStruct((dkv, R_ALL), BF16),
            jax.ShapeDtypeStruct((R_ALL, dq), BF16),
        ],
        scratch_shapes=[pltpu.VMEM((TN, D), F32)],
        compiler_params=pltpu.CompilerParams(vmem_limit_bytes=VMEM_LIMIT),
        name="mid",
    )(x2, ctx2, mod0, mod1, ys_lat, ys_ctx, rest, glu_w, glu_b, sg_g, sg_b, sg_w, sg_bias, w_out,
      ln_g, ln_b, w_qkg, w_vt, cos_t, sin_t)


N_QBLK = N_LAT // ATT_BLK
GRP = N_HEADS // N_KV
ATT_SUB = 2
ATT_ROWS = ATT_SUB * ATT_BLK
N_QSTEP = N_QBLK // ATT_SUB
ONES_ROWS = 16


def _attn_kernel(sink_ref, q_ref, kp_ref, kc_ref, kn_ref, kx_ref, vp_ref, vc_ref, vn_ref, vx_ref,
                 o_ref):
    i = pl.program_id(1)
    n_win = 3 * ATT_BLK
    n_keys = n_win + N_CTX
    nq = GRP * ATT_BLK
    k_win = jnp.concatenate([kp_ref[...], kc_ref[...], kn_ref[...]], axis=0)
    vt_win = jnp.concatenate([vp_ref[...], vc_ref[...], vn_ref[...]], axis=1)
    k_ctx = kx_ref[...]
    vt_ctx = vx_ref[...]

    kpos = lax.broadcasted_iota(jnp.int32, (ATT_BLK, ATT_BLK), 0)
    qpos = lax.broadcasted_iota(jnp.int32, (ATT_BLK, ATT_BLK), 1)
    tile_q = lambda a: jnp.concatenate([a] * GRP, axis=1)
    qgrp = lax.broadcasted_iota(jnp.int32, (1, nq), 1) // ATT_BLK
    ones = jnp.ones((ONES_ROWS, n_keys), BF16)
    q = q_ref[...]

    units = [(u, h) for u in range(ATT_SUB) for h in range(N_KV)]
    k_all, vt_all, bias = [], [], []
    for u in range(ATT_SUB):
        k_all.append(jnp.concatenate([k_win[u * ATT_BLK:u * ATT_BLK + n_win], k_ctx], axis=0))
        vt_all.append(jnp.concatenate([vt_win[:, u * ATT_BLK:u * ATT_BLK + n_win], vt_ctx], axis=1))
        has_prev = (i > 0) if u == 0 else True
        has_next = (i < N_QSTEP - 1) if u == ATT_SUB - 1 else True
        bias.append((tile_q(jnp.where((kpos >= qpos) & has_prev, 0.0, NEG_INF)),
                     tile_q(jnp.where((kpos <= qpos) & has_next, 0.0, NEG_INF))))
    scores = []
    for u, h in units:
        kh = k_all[u][:, h * HD:(h + 1) * HD]
        qh = jnp.concatenate(
            [q[u * ATT_BLK:(u + 1) * ATT_BLK, (h * GRP + g) * HD:(h * GRP + g + 1) * HD]
             for g in range(GRP)], axis=0)
        scores.append(lax.dot_general(kh, qh, (((1,), (1,)), ((), ())),
                                      preferred_element_type=F32))
    probs = []
    for (u, h), s in zip(units, scores):
        s = jnp.concatenate([s[:ATT_BLK] + bias[u][0], s[ATT_BLK:2 * ATT_BLK],
                             s[2 * ATT_BLK:n_win] + bias[u][1], s[n_win:]], axis=0)
        sink = jnp.zeros((1, nq), F32)
        for g in range(GRP):
            sink = jnp.where(qgrp == g, sink_ref[h * GRP + g] * LOG2E, sink)
        m = jnp.maximum(jnp.max(s, axis=0, keepdims=True), sink)
        probs.append((jnp.exp2(s - m).astype(BF16), jnp.exp2(sink - m)))
    outs = [[] for _ in range(ATT_SUB)]
    for (u, h), (p, p_sink) in zip(units, probs):
        vt1 = jnp.concatenate([vt_all[u][h * HD:(h + 1) * HD], ones], axis=0)
        ov = jnp.dot(vt1, p, preferred_element_type=F32)
        o_t = ov[:HD] / (ov[HD:HD + 1] + p_sink)
        outs[u].extend(o_t[:, g * ATT_BLK:(g + 1) * ATT_BLK] for g in range(GRP))
    for u in range(ATT_SUB):
        o_ref[u * ATT_BLK:(u + 1) * ATT_BLK, :] = jnp.transpose(
            jnp.concatenate(outs[u], axis=0)).astype(BF16)


def _attention(sink, q, k, vt):
    dq = N_HEADS * HD
    dkv = N_KV * HD
    ctx_blk0 = R_LAT // N_CTX
    cur = lambda b, i: b * N_QSTEP + i
    prev_blk = lambda b, i: b * N_QBLK + jnp.maximum(ATT_SUB * i - 1, 0)
    next_blk = lambda b, i: b * N_QBLK + jnp.minimum(ATT_SUB * (i + 1), N_QBLK - 1)
    return pl.pallas_call(
        _attn_kernel,
        grid_spec=pltpu.PrefetchScalarGridSpec(
            num_scalar_prefetch=1,
            grid=(B, N_QSTEP),
            in_specs=[
                pl.BlockSpec((ATT_ROWS, dq), lambda b, i, s: (cur(b, i), 0)),
                pl.BlockSpec((ATT_BLK, dkv), lambda b, i, s: (prev_blk(b, i), 0)),
                pl.BlockSpec((ATT_ROWS, dkv), lambda b, i, s: (cur(b, i), 0)),
                pl.BlockSpec((ATT_BLK, dkv), lambda b, i, s: (next_blk(b, i), 0)),
                pl.BlockSpec((N_CTX, dkv), lambda b, i, s: (ctx_blk0 + b, 0)),
                pl.BlockSpec((dkv, ATT_BLK), lambda b, i, s: (0, prev_blk(b, i))),
                pl.BlockSpec((dkv, ATT_ROWS), lambda b, i, s: (0, cur(b, i))),
                pl.BlockSpec((dkv, ATT_BLK), lambda b, i, s: (0, next_blk(b, i))),
                pl.BlockSpec((dkv, N_CTX), lambda b, i, s: (0, ctx_blk0 + b)),
            ],
            out_specs=pl.BlockSpec((ATT_ROWS, dq), lambda b, i, s: (cur(b, i), 0)),
        ),
        out_shape=jax.ShapeDtypeStruct((R_LAT, dq), BF16),
        compiler_params=pltpu.CompilerParams(vmem_limit_bytes=VMEM_LIMIT),
        name="attention",
    )(sink, q, k, k, k, k, vt, vt, vt, vt)


def _tail1_kernel(z_ref, mod_ref, o_ref_in, g_ref, wout_ref, lng_ref, lnb_ref, out_ref):
    i = pl.program_id(0)
    gate = mod_ref[pl.ds(_block_mod_row(i), 1), 2 * D:3 * D]
    g = g_ref[...].astype(F32)
    mix = (o_ref_in[...].astype(F32) * _silu(g)).astype(BF16)
    y = jnp.dot(mix, wout_ref[...], preferred_element_type=F32) * gate
    out_ref[...] = _layer_norm(ALPHA * z_ref[...] + y, lng_ref[...], lnb_ref[...])


def _tail1(z1, mod, o, g, w_out, ln_g, ln_b):
    row = lambda n: pl.BlockSpec((1, n), lambda i: (0, 0))
    return pl.pallas_call(
        _tail1_kernel,
        grid=(N_LAT_BLK,),
        in_specs=[
            pl.BlockSpec((TN, D), lambda i: (i, 0)),
            pl.BlockSpec((8, 3 * D), lambda i: (0, 0)),
            pl.BlockSpec((TN, D), lambda i: (i, 0)),
            pl.BlockSpec((TN, D), lambda i: (i, 0)),
            pl.BlockSpec((D, D), lambda i: (0, 0)),
            row(D), row(D),
        ],
        out_specs=pl.BlockSpec((TN, D), lambda i: (i, 0)),
        out_shape=jax.ShapeDtypeStruct((R_LAT, D), F32),
        compiler_params=pltpu.CompilerParams(vmem_limit_bytes=VMEM_LIMIT),
        name="tail1",
    )(z1, mod, o, g, w_out, ln_g, ln_b)


def kernel(x, c, ctx, c_ctx, mod_w, mod_b, ln_g, ln_b, e_w_in, e_w_out, s5_lam_re, s5_lam_im,
           s5_log_dt, s5_b_re, s5_b_im, s5_c_re, s5_c_im, s5_d, glu_w, glu_b, sg_ln_g, sg_ln_b,
           sg_w, sg_b, o_w_in, o_w_out, o_sink):
    x2 = x.reshape(R_LAT, D)
    ctx2 = ctx.reshape(R_CTX, D)
    mod = _modulation(c, c_ctx, mod_w, mod_b)

    xa, rest = _in0(x2, ctx2, mod[0], e_w_in[0].astype(BF16))
    t_mat, e_mat, ft_mat, al_re, al_im = _s5_prep(
        s5_lam_re[0], s5_lam_im[0], s5_log_dt[0], s5_b_re[0], s5_b_im[0], s5_c_re[0], s5_c_im[0],
        s5_d[0])
    st_loc = _s5_states(xa, e_mat)
    st_in = _s5_scan(st_loc, al_re, al_im)
    ys_lat, ys_ctx = _s5_out(xa, st_in, t_mat, ft_mat)
    sg_bias = jnp.repeat(jnp.transpose(sg_b[0]), SG_W // SG_HEADS, axis=1)
    cos_t, sin_t = _rope_tables()
    z1, q, k, vt, g = _mid(
        x2, ctx2, mod[0], mod[1], ys_lat, ys_ctx, rest, glu_w[0].astype(BF16),
        glu_b[0].reshape(1, S5_W), sg_ln_g[0].reshape(1, SG_W), sg_ln_b[0].reshape(1, SG_W),
        sg_w[0].astype(BF16), sg_bias, e_w_out[0].astype(BF16), ln_g[0].reshape(1, D),
        ln_b[0].reshape(1, D), o_w_in[0], cos_t, sin_t)

    o = _attention(o_sink[0], q, k, vt)
    out = _tail1(z1, mod[1], o, g, o_w_out[0].astype(BF16), ln_g[1].reshape(1, D),
                 ln_b[1].reshape(1, D))
    return out.reshape(B, N_LAT, D)
```

```python
import functools
import math

import jax
import jax.numpy as jnp
from jax import lax
from jax.experimental import pallas as pl
from jax.experimental.pallas import tpu as pltpu

F32 = jnp.float32
BF16 = jnp.bfloat16

D = 1024
B = 2
N_LAT = 8192
N_CTX = 256
DEPTH = 2
GRID_W = 64
S5_W = 512
S5_GC = 16
S5_G = 32
S5_P = 64
SG_W = 512
SG_HEADS = 8
SG_CHUNK = 128
N_HEADS = 16
N_KV = 4
HD = 64
WINDOW = 128
ATT_BLK = 128
ROPE_BASE = 10000.0
NEG_INF = -1e30
LN_EPS = 1e-5
ALPHA = (2 * DEPTH) ** 0.25
LOG2E = math.log2(math.e)
EVEN_IN = 2 * S5_W + 3 * SG_W
ODD_IN = 2 * N_HEADS * HD + 2 * N_KV * HD

LANES = 128
VMEM_LIMIT = 56 * 1024 * 1024

R_LAT = B * N_LAT
R_CTX = B * N_CTX
R_ALL = R_LAT + R_CTX
TN = 512
N_LAT_BLK = R_LAT // TN
N_BLK = R_ALL // TN
LAT_BLK_PER_BATCH = N_LAT // TN
CTX_MOD_ROW = B

S5_CHUNK = 8
S5_QT = S5_W // LANES
S5_GPT = LANES // S5_GC
S5_ST = S5_GPT * S5_P
S5_PAIRS = S5_GPT // 2
PAIR_BLK = 2 * S5_GC
LAT_CHUNKS = N_LAT // S5_CHUNK
CTX_CHUNKS = N_CTX // S5_CHUNK
BATCH_CHUNKS = CTX_CHUNKS + LAT_CHUNKS
N_CHUNK = B * BATCH_CHUNKS
N_SEG = 8
SEG_CHUNKS = BATCH_CHUNKS // N_SEG
BWD_RESET_STEP = LAT_CHUNKS - (N_SEG - 1) * SEG_CHUNKS


def _sigmoid(x):
    return 1.0 / (1.0 + jnp.exp(-x))


def _silu(x):
    return x * _sigmoid(x)


def _gelu_tanh(x):
    return 0.5 * x * (1.0 + jnp.tanh(math.sqrt(2.0 / math.pi) * (x + 0.044715 * (x * x * x))))


def _layer_norm(x, g, b):
    mu = jnp.mean(x, axis=-1, keepdims=True)
    xc = x - mu
    var = jnp.mean(xc * xc, axis=-1, keepdims=True)
    return xc * lax.rsqrt(var + LN_EPS) * g + b


def _split_bf16(x):
    hi = x.astype(BF16)
    return hi, (x - hi.astype(F32)).astype(BF16)


def _dot_split(x, y_split, dims):
    x_hi, x_lo = _split_bf16(x)
    y_hi, y_lo = y_split
    dot = lambda a, b: lax.dot_general(a, b, dims, preferred_element_type=F32)
    return dot(x_hi, y_hi) + (dot(x_hi, y_lo) + dot(x_lo, y_hi))


def _block_mod_row(i):
    return jnp.where(i < N_LAT_BLK, i // LAT_BLK_PER_BATCH, CTX_MOD_ROW)


def _mod_kernel(cv_ref, w_ref, b_ref, o_ref):
    s = _silu(cv_ref[...])
    o_ref[0] = _dot_split(s, _split_bf16(w_ref[0]), (((1,), (0,)), ((), ()))) + b_ref[0]


def _modulation(c, c_ctx, mod_w, mod_b):
    cv = jnp.concatenate([c, c_ctx[None], jnp.zeros((8 - B - 1, D), F32)], axis=0)
    tn = 1024
    return pl.pallas_call(
        _mod_kernel,
        grid=(DEPTH, 3 * D // tn),
        in_specs=[
            pl.BlockSpec((8, D), lambda l, j: (0, 0)),
            pl.BlockSpec((1, D, tn), lambda l, j: (l, 0, j)),
            pl.BlockSpec((1, 1, tn), lambda l, j: (l, 0, j)),
        ],
        out_specs=pl.BlockSpec((1, 8, tn), lambda l, j: (l, 0, j)),
        out_shape=jax.ShapeDtypeStruct((DEPTH, 8, 3 * D), F32),
        compiler_params=pltpu.CompilerParams(vmem_limit_bytes=VMEM_LIMIT),
        name="modulation",
    )(cv, mod_w, mod_b.reshape(DEPTH, 1, 3 * D))


def _in0_kernel(x_ref, ctx_ref, mod_ref, w_ref, xa_ref, rest_ref):
    i = pl.program_id(0)
    z = jnp.where(i < N_LAT_BLK, x_ref[...], ctx_ref[...])
    m = mod_ref[pl.ds(_block_mod_row(i), 1), :]
    h = z * (1.0 + m[:, D:2 * D]) + m[:, :D]
    p = jnp.dot(h.astype(BF16), w_ref[...], preferred_element_type=F32)
    xa_ref[...] = p[:, :S5_W]
    rest_ref[...] = p[:, S5_W:].astype(BF16)


def _in0(x2, ctx2, mod, w_in):
    return pl.pallas_call(
        _in0_kernel,
        grid=(N_BLK,),
        in_specs=[
            pl.BlockSpec((TN, D), lambda i: (jnp.minimum(i, N_LAT_BLK - 1), 0)),
            pl.BlockSpec((R_CTX, D), lambda i: (0, 0)),
            pl.BlockSpec((8, 3 * D), lambda i: (0, 0)),
            pl.BlockSpec((D, EVEN_IN), lambda i: (0, 0)),
        ],
        out_specs=[
            pl.BlockSpec((TN, S5_W), lambda i: (i, 0)),
            pl.BlockSpec((TN, EVEN_IN - S5_W), lambda i: (i, 0)),
        ],
        out_shape=[
            jax.ShapeDtypeStruct((R_ALL, S5_W), F32),
            jax.ShapeDtypeStruct((R_ALL, EVEN_IN - S5_W), BF16),
        ],
        compiler_params=pltpu.CompilerParams(vmem_limit_bytes=VMEM_LIMIT),
        name="in0",
    )(x2, ctx2, mod, w_in)


def _s5_prep_kernel(rows_ref, bt_ref, ct_ref, dsk_ref, t_ref, e_ref, ft_ref, alr_ref, ali_ref):
    L = S5_CHUNK
    tile = (LANES, S5_ST)
    same_group = (lax.broadcasted_iota(jnp.int32, tile, 0) // S5_GC
                  == lax.broadcasted_iota(jnp.int32, tile, 1) // S5_P)
    nt_dims = (((1,), (1,)), ((), ()))
    pair_rows = lambda pr: slice(pr * PAIR_BLK, (pr + 1) * PAIR_BLK)
    pair_lanes = lambda pr: slice(pr * LANES, (pr + 1) * LANES)
    lag_strip = []
    for d in range(2):
        lam_re = rows_ref[d, 0]
        lam_im = rows_ref[d, 1]
        dt = jnp.exp(rows_ref[d, 2])
        pw = []
        for k in range(L + 1):
            mag = jnp.exp(lam_re * dt * float(k))
            ang = lam_im * dt * float(k)
            pw.append((mag * jnp.cos(ang), mag * jnp.sin(ang)))
        alr_ref[d] = pw[L][0]
        ali_ref[d] = pw[L][1]
        den = lam_re * lam_re + lam_im * lam_im
        nr = pw[1][0] - 1.0
        ni = pw[1][1]
        f_re = (nr * lam_re + ni * lam_im) / den
        f_im = (ni * lam_re - nr * lam_im) / den
        b_re = jnp.where(same_group, bt_ref[0, d], 0.0)
        b_im = jnp.where(same_group, bt_ref[1, d], 0.0)
        bb_re = f_re * b_re - f_im * b_im
        bb_im = f_re * b_im + f_im * b_re
        c_re = jnp.where(same_group, ct_ref[0, d], 0.0)
        c_im = jnp.where(same_group, ct_ref[1, d], 0.0)
        for k in range(L):
            x_re = pw[k][0] * bb_re - pw[k][1] * bb_im
            x_im = pw[k][0] * bb_im + pw[k][1] * bb_re
            s = (L - 1 - k) if d == 0 else k
            for pr in range(S5_PAIRS):
                blk = slice(s * PAIR_BLK, (s + 1) * PAIR_BLK)
                e_ref[0, pr, blk, 2 * d * LANES:(2 * d + 1) * LANES] = (
                    x_re[pair_rows(pr), pair_lanes(pr)].astype(BF16))
                e_ref[0, pr, blk, (2 * d + 1) * LANES:(2 * d + 2) * LANES] = (
                    x_im[pair_rows(pr), pair_lanes(pr)].astype(BF16))
        cw = []
        for k in range(L + 1):
            w_re = c_re * pw[k][0] - c_im * pw[k][1]
            w_im = c_re * pw[k][1] + c_im * pw[k][0]
            cw.append((w_re, -w_im))
        for t in range(L):
            k = (t + 1) if d == 0 else (L - t)
            for pr in range(S5_PAIRS):
                blk = slice(t * PAIR_BLK, (t + 1) * PAIR_BLK)
                ft_ref[0, pr, blk, 2 * d * LANES:(2 * d + 1) * LANES] = (
                    cw[k][0][pair_rows(pr), pair_lanes(pr)].astype(BF16))
                ft_ref[0, pr, blk, (2 * d + 1) * LANES:(2 * d + 2) * LANES] = (
                    cw[k][1][pair_rows(pr), pair_lanes(pr)].astype(BF16))
        lags = range(L) if d == 0 else range(L - 1, -1, -1)
        c_stack = jnp.concatenate(
            [jnp.concatenate([cw[k][0][pair_rows(pr)], cw[k][1][pair_rows(pr)]], axis=-1)
             for pr in range(S5_PAIRS) for k in lags], axis=0)
        lag_strip.append(_dot_split(jnp.concatenate([bb_re, bb_im], axis=-1),
                                    _split_bf16(c_stack), nt_dims))

    strip_w = 2 * L * PAIR_BLK
    mid = (L - 1) * PAIR_BLK
    widen = lambda a: jnp.concatenate([a, jnp.zeros((a.shape[0], strip_w - a.shape[1]), F32)], axis=-1)
    row = lax.broadcasted_iota(jnp.int32, (PAIR_BLK, strip_w), 0)
    lane = lax.broadcasted_iota(jnp.int32, (PAIR_BLK, strip_w), 1)
    d_row = widen(dsk_ref[0])
    for pr in range(S5_PAIRS):
        lanes = slice(pr * L * PAIR_BLK, (pr + 1) * L * PAIR_BLK)
        strip = (widen(lag_strip[1][pair_rows(pr), lanes])
                 + pltpu.roll(widen(lag_strip[0][pair_rows(pr), lanes]), mid, axis=1)
                 + jnp.where(lane == row + mid,
                             pltpu.roll(d_row, mid - pr * PAIR_BLK, axis=1), 0.0))
        for s in range(L):
            shift = (strip_w - (L - 1 - s) * PAIR_BLK) % strip_w
            blk = pltpu.roll(strip, shift, axis=1) if shift else strip
            t_ref[0, pr, s * PAIR_BLK:(s + 1) * PAIR_BLK, :] = blk[:, :L * PAIR_BLK].astype(BF16)


def _s5_prep(lam_re, lam_im, log_dt, b_re, b_im, c_re, c_im, d_skip):
    L = S5_CHUNK
    n_st = S5_G * S5_P
    pair_w = L * PAIR_BLK
    rows = jnp.stack([lam_re.reshape(2, 1, n_st), lam_im.reshape(2, 1, n_st),
                      jnp.repeat(log_dt, S5_P, axis=1).reshape(2, 1, n_st)], axis=1)
    rep = lambda a: jnp.tile(a.reshape(2, S5_W, S5_P), (1, 1, S5_GPT))
    bt = jnp.stack([rep(jnp.swapaxes(b_re, 2, 3)), rep(jnp.swapaxes(b_im, 2, 3))])
    ct = jnp.stack([rep(c_re), rep(c_im)])
    return pl.pallas_call(
        _s5_prep_kernel,
        grid=(S5_QT,),
        in_specs=[
            pl.BlockSpec((2, 3, 1, S5_ST), lambda q: (0, 0, 0, q)),
            pl.BlockSpec((2, 2, LANES, S5_ST), lambda q: (0, 0, q, 0)),
            pl.BlockSpec((2, 2, LANES, S5_ST), lambda q: (0, 0, q, 0)),
            pl.BlockSpec((1, 1, LANES), lambda q: (q, 0, 0)),
        ],
        out_specs=[
            pl.BlockSpec((1, S5_PAIRS, pair_w, pair_w), lambda q: (q, 0, 0, 0)),
            pl.BlockSpec((1, S5_PAIRS, pair_w, 4 * LANES), lambda q: (q, 0, 0, 0)),
            pl.BlockSpec((1, S5_PAIRS, pair_w, 4 * LANES), lambda q: (q, 0, 0, 0)),
            pl.BlockSpec((2, 1, S5_ST), lambda q: (q, 0, 0)),
            pl.BlockSpec((2, 1, S5_ST), lambda q: (q, 0, 0)),
        ],
        out_shape=[
            jax.ShapeDtypeStruct((S5_QT, S5_PAIRS, pair_w, pair_w), BF16),
            jax.ShapeDtypeStruct((S5_QT, S5_PAIRS, pair_w, 4 * LANES), BF16),
            jax.ShapeDtypeStruct((S5_QT, S5_PAIRS, pair_w, 4 * LANES), BF16),
            jax.ShapeDtypeStruct((2 * S5_QT, 1, S5_ST), F32),
            jax.ShapeDtypeStruct((2 * S5_QT, 1, S5_ST), F32),
        ],
        compiler_params=pltpu.CompilerParams(vmem_limit_bytes=VMEM_LIMIT),
        name="s5_prep",
    )(rows, bt, ct, d_skip.reshape(S5_QT, 1, LANES))


def _block_transpose4(tiles):
    tiles = list(tiles)
    blk = lax.broadcasted_iota(jnp.int32, tiles[0].shape, 1) // PAIR_BLK
    for dist in (2, 1):
        keep = (blk & dist) == 0
        for i in range(len(tiles)):
            if i & dist:
                continue
            lo, hi = tiles[i], tiles[i + dist]
            tiles[i] = jnp.where(keep, lo, pltpu.roll(hi, dist * PAIR_BLK, axis=1))
            tiles[i + dist] = jnp.where(keep, pltpu.roll(lo, LANES - dist * PAIR_BLK, axis=1), hi)
    return tiles


def _pair_rows(xl_ref, xc_ref):
    L = S5_CHUNK
    tiles = [jnp.concatenate([xc_ref[pl.ds(t, CTX_CHUNKS, stride=L), :],
                              xl_ref[pl.ds(t, LAT_CHUNKS, stride=L), :]], axis=0) for t in range(L)]
    halves = [_block_transpose4(tiles[h * S5_PAIRS:(h + 1) * S5_PAIRS]) for h in range(L // S5_PAIRS)]
    return [jnp.concatenate([half[pr] for half in halves], axis=-1).astype(BF16)
            for pr in range(S5_PAIRS)]


def _xa_specs(index_of):
    ctx_blk0 = R_LAT // N_CTX
    return [
        pl.BlockSpec((N_LAT, LANES), lambda *g: (index_of(*g)[1], index_of(*g)[0])),
        pl.BlockSpec((N_CTX, LANES), lambda *g: (ctx_blk0 + index_of(*g)[1], index_of(*g)[0])),
    ]


N_SLAB = S5_ST // LANES
ST_SHAPE = (S5_QT * 2 * 2 * N_SLAB, N_CHUNK, LANES)
ST_SPEC = pl.BlockSpec((2 * N_SLAB, BATCH_CHUNKS, LANES), lambda q, b, d: (2 * q + d, b, 0))


def _s5_states_kernel(xl_ref, xc_ref, e_ref, st_ref):
    for pr, rows in enumerate(_pair_rows(xl_ref, xc_ref)):
        st = jnp.dot(rows, e_ref[0, pr], preferred_element_type=F32)
        for comp in range(4):
            st_ref[comp * N_SLAB + pr] = st[:, comp * LANES:(comp + 1) * LANES]


def _s5_states(xa, e_mat):
    pair_w = S5_CHUNK * PAIR_BLK
    return pl.pallas_call(
        _s5_states_kernel,
        grid=(S5_QT, B),
        in_specs=_xa_specs(lambda q, b: (q, b)) + [
            pl.BlockSpec((1, S5_PAIRS, pair_w, 4 * LANES), lambda q, b: (q, 0, 0, 0)),
        ],
        out_specs=pl.BlockSpec((4 * N_SLAB, BATCH_CHUNKS, LANES), lambda q, b: (q, b, 0)),
        out_shape=jax.ShapeDtypeStruct(ST_SHAPE, F32),
        compiler_params=pltpu.CompilerParams(vmem_limit_bytes=VMEM_LIMIT),
        name="s5_states",
    )(xa, xa, e_mat)


def _cmul(ar, ai, br, bi):
    return ar * br - ai * bi, ar * bi + ai * br


def _cpow(ar, ai, n):
    res = None
    while n:
        if n & 1:
            res = (ar, ai) if res is None else _cmul(res[0], res[1], ar, ai)
        n >>= 1
        if n:
            ar, ai = _cmul(ar, ai, ar, ai)
    return res


def _s5_scan_kernel(loc_ref, ar_ref, ai_ref, out_ref):
    chains = [(b, k) for b in range(B) for k in range(N_SLAB)]

    def run(backward):
        a_row = [(ar_ref[0][:, k * LANES:(k + 1) * LANES], ai_ref[0][:, k * LANES:(k + 1) * LANES])
                 for k in range(N_SLAB)]
        a_tile = [(jnp.broadcast_to(r, (N_SEG, LANES)), jnp.broadcast_to(i, (N_SEG, LANES)))
                  for r, i in a_row]

        def sweep(lo, hi, carry, store):
            def step(i, carry):
                off = (SEG_CHUNKS - 1 - i) if backward else i
                new = []
                for (b, k), (cr, ci) in zip(chains, carry):
                    rows = pl.ds(b * BATCH_CHUNKS + off, N_SEG, stride=SEG_CHUNKS)
                    xr = loc_ref[k, rows, :]
                    xi = loc_ref[N_SLAB + k, rows, :]
                    if store:
                        out_ref[k, rows, :] = cr
                        out_ref[N_SLAB + k, rows, :] = ci
                    nr, ni = _cmul(a_tile[k][0], a_tile[k][1], cr, ci)
                    new.append((nr + xr, ni + xi))
                return tuple(new)
            return lax.fori_loop(lo, hi, step, carry)

        def full(carry, store):
            if not backward:
                return sweep(0, SEG_CHUNKS, carry, store)
            carry = sweep(0, BWD_RESET_STEP, carry, store)
            first = lax.broadcasted_iota(jnp.int32, (N_SEG, LANES), 0) == 0
            carry = tuple((jnp.where(first, 0.0, cr), jnp.where(first, 0.0, ci)) for cr, ci in carry)
            return sweep(BWD_RESET_STEP, SEG_CHUNKS, carry, store)

        zero_tile = jnp.zeros((N_SEG, LANES), F32)
        ends = full(tuple((zero_tile, zero_tile) for _ in chains), False)

        starts = []
        for (b, k), (er, ei) in zip(chains, ends):
            pr, pi = _cpow(a_row[k][0], a_row[k][1], SEG_CHUNKS)
            if backward:
                order = range(N_SEG - 1, -1, -1)
                g = (er[0:1], ei[0:1])
            else:
                order = range(N_SEG)
                g = (jnp.zeros((1, LANES), F32), jnp.zeros((1, LANES), F32))
            rows_r = [None] * N_SEG
            rows_i = [None] * N_SEG
            for j in order:
                rows_r[j], rows_i[j] = g
                nr, ni = _cmul(pr, pi, g[0], g[1])
                g = (nr + er[j:j + 1], ni + ei[j:j + 1])
            starts.append((jnp.concatenate(rows_r, axis=0), jnp.concatenate(rows_i, axis=0)))
        full(tuple(starts), True)

    backward = pl.program_id(0) % 2
    pl.when(backward == 0)(lambda: run(False))
    pl.when(backward == 1)(lambda: run(True))


def _s5_scan(st_loc, al_re, al_im):
    slabs = pl.BlockSpec((2 * N_SLAB, N_CHUNK, LANES), lambda j: (j, 0, 0))
    return pl.pallas_call(
        _s5_scan_kernel,
        grid=(2 * S5_QT,),
        in_specs=[
            slabs,
            pl.BlockSpec((1, 1, S5_ST), lambda j: (j, 0, 0)),
            pl.BlockSpec((1, 1, S5_ST), lambda j: (j, 0, 0)),
        ],
        out_specs=slabs,
        out_shape=jax.ShapeDtypeStruct(ST_SHAPE, F32),
        compiler_params=pltpu.CompilerParams(vmem_limit_bytes=VMEM_LIMIT),
        name="s5_scan",
    )(st_loc, al_re, al_im)


def _s5_out_kernel(xl_ref, xc_ref, st_ref, t_ref, ft_ref, yl_ref, yc_ref, acc_ref):
    L = S5_CHUNK
    d = pl.program_id(2)

    def from_states(pr):
        st = jnp.concatenate([st_ref[pr], st_ref[N_SLAB + pr]], axis=-1).astype(BF16)
        return lax.dot_general(st, ft_ref[0, pr], (((1,), (1,)), ((), ())),
                               preferred_element_type=F32)

    @pl.when(d == 0)
    def _():
        for pr, rows in enumerate(_pair_rows(xl_ref, xc_ref)):
            acc_ref[pr] = from_states(pr) + jnp.dot(rows, t_ref[0, pr],
                                                    preferred_element_type=F32)

    @pl.when(d == 1)
    def _():
        ys = [acc_ref[pr] + from_states(pr) for pr in range(S5_PAIRS)]
        tiles = [tile for h in range(L // S5_PAIRS) for tile in _block_transpose4(
            [y[:, h * LANES:(h + 1) * LANES] for y in ys])]
        for t, tile in enumerate(tiles):
            yc_ref[pl.ds(t, CTX_CHUNKS, stride=L), :] = tile[:CTX_CHUNKS]
            yl_ref[pl.ds(t, LAT_CHUNKS, stride=L), :] = tile[CTX_CHUNKS:]


def _s5_out(xa, st_in, t_mat, ft_mat):
    pair_w = S5_CHUNK * PAIR_BLK
    return pl.pallas_call(
        _s5_out_kernel,
        grid=(S5_QT, B, 2),
        in_specs=_xa_specs(lambda q, b, d: (q, b)) + [
            ST_SPEC,
            pl.BlockSpec((1, S5_PAIRS, pair_w, pair_w), lambda q, b, d: (q, 0, 0, 0)),
            pl.BlockSpec((1, S5_PAIRS, pair_w, 2 * LANES), lambda q, b, d: (q, 0, 0, d)),
        ],
        out_specs=[
            pl.BlockSpec((N_LAT, LANES), lambda q, b, d: (b, q)),
            pl.BlockSpec((N_CTX, LANES), lambda q, b, d: (b, q)),
        ],
        out_shape=[
            jax.ShapeDtypeStruct((R_LAT, S5_W), F32),
            jax.ShapeDtypeStruct((R_CTX, S5_W), F32),
        ],
        scratch_shapes=[pltpu.VMEM((S5_PAIRS, BATCH_CHUNKS, pair_w), F32)],
        compiler_params=pltpu.CompilerParams(vmem_limit_bytes=VMEM_LIMIT),
        name="s5_out",
    )(xa, xa, st_in, t_mat, ft_mat)


def _tail0_stages(z, ys, rest_ref, gate, gluw_ref, glub_ref, sg_g_ref, sg_b_ref, sgw_ref,
                  sgbias_ref, wout_ref, lng_ref, lnb_ref):
    ga = rest_ref[:, 0:S5_W].astype(F32)
    u = rest_ref[:, S5_W:2 * S5_W].astype(F32)
    v = rest_ref[:, 2 * S5_W:3 * S5_W].astype(F32)
    gb = rest_ref[:, 3 * S5_W:4 * S5_W].astype(F32)

    ya = _gelu_tanh(ys)
    glu = jnp.dot(ya.astype(BF16), gluw_ref[...], preferred_element_type=F32) + glub_ref[...]
    yield None
    ya = ya * _sigmoid(glu) * _silu(ga)
    yield None

    vn = _layer_norm(v, sg_g_ref[...], sg_b_ref[...]).astype(BF16)
    lane = lax.broadcasted_iota(jnp.int32, (SG_CHUNK, LANES), 1)
    first_head = lane < (SG_W // SG_HEADS)
    chunks = []
    for c in range(TN // SG_CHUNK):
        tiles = []
        for j in range(SG_W // LANES):
            vt = vn[c * SG_CHUNK:(c + 1) * SG_CHUNK, j * LANES:(j + 1) * LANES]
            r0 = jnp.dot(sgw_ref[2 * j], vt, preferred_element_type=F32)
            r1 = jnp.dot(sgw_ref[2 * j + 1], vt, preferred_element_type=F32)
            tiles.append(jnp.where(first_head, r0, r1))
        chunks.append(jnp.concatenate(tiles, axis=-1) + sgbias_ref[...])
    s = jnp.concatenate(chunks, axis=0)
    yb = u * s * _silu(gb)
    yield None

    mix = jnp.concatenate([ya, yb], axis=-1).astype(BF16)
    y = jnp.dot(mix, wout_ref[...], preferred_element_type=F32) * gate
    yield None
    yield _layer_norm(ALPHA * z + y, lng_ref[...], lnb_ref[...])


def _rope_tables():
    nf = HD // 4
    n_rows = N_LAT // GRID_W
    lane = jnp.arange(LANES)
    inv = ROPE_BASE ** (-(lane % nf).astype(F32) / nf)
    by_row = ((lane % HD) // (HD // 2) == 0)[None, :]
    sign = jnp.where((lane % (HD // 2)) < nf, -1.0, 1.0)[None, :]
    row_ang = jnp.arange(n_rows, dtype=F32)[:, None] * inv[None, :]
    col_ang = jnp.arange(GRID_W, dtype=F32)[:, None] * inv[None, :]
    zero = jnp.zeros((), F32)
    cos_t = (jnp.where(by_row, jnp.cos(row_ang), zero)[:, None, :]
             + jnp.where(by_row, zero, jnp.cos(col_ang))[None, :, :]).reshape(N_LAT, LANES)
    sin_t = (jnp.where(by_row, sign * jnp.sin(row_ang), zero)[:, None, :]
             + jnp.where(by_row, zero, sign * jnp.sin(col_ang))[None, :, :]).reshape(N_LAT, LANES)
    return cos_t, sin_t


def _rope_tile(x, cos, sin, first_half):
    nf = HD // 4
    partner = jnp.where(first_half, pltpu.roll(x, LANES - nf, axis=1), pltpu.roll(x, nf, axis=1))
    return x * cos + partner * sin


def _mid_kernel(x_ref, ctx_ref, mod0_ref, mod1_ref, ysl_ref, ysc_ref, rest_ref, gluw_ref, glub_ref,
                sg_g_ref, sg_b_ref, sgw_ref, sgbias_ref, wout_ref, lng_ref, lnb_ref, w_ref,
                wvt_ref, cos_ref, sin_ref, z1_ref, q_ref, k_ref, vt_ref, g_ref, zprev_ref):
    i = pl.program_id(0)
    dq = N_HEADS * HD
    dkv = N_KV * HD

    @pl.when(i == 0)
    def _():
        zprev_ref[...] = jnp.zeros((TN, D), F32)

    j = jnp.maximum(i - 1, 0)
    m1 = mod1_ref[pl.ds(_block_mod_row(j), 1), :]
    h = (zprev_ref[...] * (1.0 + m1[:, D:2 * D]) + m1[:, :D]).astype(BF16)
    cos = jnp.where(j < N_LAT_BLK, cos_ref[...], 1.0)
    sin = jnp.where(j < N_LAT_BLK, sin_ref[...], 0.0)
    lane = lax.broadcasted_iota(jnp.int32, (TN, LANES), 1)
    first_half = (lane % (HD // 2)) < (HD // 4)
    scale = HD ** -0.5 * LOG2E

    def project(c0, c1):
        return jnp.dot(h, w_ref[:, c0:c1], preferred_element_type=F32)

    def roped_tiles(p, mult):
        for c in range(p.shape[1] // LANES):
            r = _rope_tile(p[:, c * LANES:(c + 1) * LANES], cos, sin, first_half)
            yield c, (r * mult if mult != 1.0 else r).astype(BF16)

    t = jnp.minimum(i, N_BLK - 1)
    z = jnp.where(t < N_LAT_BLK, x_ref[...], ctx_ref[...])
    ys = jnp.where(t < N_LAT_BLK, ysl_ref[...], ysc_ref[...])
    gate = mod0_ref[pl.ds(_block_mod_row(t), 1), 2 * D:3 * D]
    tail = _tail0_stages(z, ys, rest_ref, gate, gluw_ref, glub_ref, sg_g_ref, sg_b_ref, sgw_ref,
                         sgbias_ref, wout_ref, lng_ref, lnb_ref)

    half_q = dq // 2
    p_q0 = project(0, half_q)
    next(tail)
    p_q1 = project(half_q, dq)
    next(tail)
    p_k = project(dq, dq + dkv)
    vt_ref[...] = lax.dot_general(wvt_ref[...], h, (((1,), (1,)), ((), ())),
                                  preferred_element_type=F32).astype(BF16)
    next(tail)
    next(tail)
    for c0, p in ((0, p_q0), (half_q, p_q1)):
        for c, r in roped_tiles(p, scale):
            q_ref[:, c0 + c * LANES:c0 + (c + 1) * LANES] = r
    for c, r in roped_tiles(p_k, 1.0):
        k_ref[:, c * LANES:(c + 1) * LANES] = r
    p_g = project(dq + dkv, ODD_IN - dkv)
    z1 = next(tail)
    z1_ref[...] = z1
    zprev_ref[...] = z1
    g_ref[...] = p_g.astype(BF16)


def _mid(x2, ctx2, mod0, mod1, ys_lat, ys_ctx, rest, glu_w, glu_b, sg_g, sg_b, sg_w, sg_bias,
         w_out, ln_g, ln_b, w_in1, cos_t, sin_t):
    dq = N_HEADS * HD
    dkv = N_KV * HD
    w_qkg = jnp.concatenate([w_in1[:, :dq + dkv], w_in1[:, dq + 2 * dkv:]], axis=1).astype(BF16)
    w_vt = jnp.transpose(w_in1[:, dq + dkv:dq + 2 * dkv]).astype(BF16)
    once = dict(pipeline_mode=pl.Buffered(1))
    row = lambda n: pl.BlockSpec((1, n), lambda i: (0, 0))
    lat = lambda i: (jnp.minimum(i, N_LAT_BLK - 1), 0)
    tail_blk = lambda i: (jnp.minimum(i, N_BLK - 1), 0)
    proj_blk = lambda i: (jnp.maximum(i - 1, 0), 0)
    tab = lambda i: (jnp.maximum(i - 1, 0) % LAT_BLK_PER_BATCH, 0)
    return pl.pallas_call(
        _mid_kernel,
        grid=(N_BLK + 1,),
        in_specs=[
            pl.BlockSpec((TN, D), lat),
            pl.BlockSpec((R_CTX, D), lambda i: (0, 0), **once),
            pl.BlockSpec((8, 3 * D), lambda i: (0, 0)),
            pl.BlockSpec((8, 3 * D), lambda i: (0, 0)),
            pl.BlockSpec((TN, S5_W), lat),
            pl.BlockSpec((R_CTX, S5_W), lambda i: (0, 0), **once),
            pl.BlockSpec((TN, EVEN_IN - S5_W), tail_blk),
            pl.BlockSpec((S5_W, S5_W), lambda i: (0, 0), **once),
            row(S5_W), row(SG_W), row(SG_W),
            pl.BlockSpec((SG_HEADS, SG_CHUNK, SG_CHUNK), lambda i: (0, 0, 0)),
            pl.BlockSpec((SG_CHUNK, SG_W), lambda i: (0, 0)),
            pl.BlockSpec((S5_W + SG_W, D), lambda i: (0, 0), **once),
            row(D), row(D),
            pl.BlockSpec((D, ODD_IN - dkv), lambda i: (0, 0), **once),
            pl.BlockSpec((dkv, D), lambda i: (0, 0), **once),
            pl.BlockSpec((TN, LANES), tab),
            pl.BlockSpec((TN, LANES), tab),
        ],
        out_specs=[
            pl.BlockSpec((TN, D), tail_blk),
            pl.BlockSpec((TN, dq), proj_blk),
            pl.BlockSpec((TN, dkv), proj_blk),
            pl.BlockSpec((dkv, TN), lambda i: (0, jnp.maximum(i - 1, 0))),
            pl.BlockSpec((TN, dq), proj_blk),
        ],
        out_shape=[
            jax.ShapeDtypeStruct((R_ALL, D), F32),
            jax.ShapeDtypeStruct((R_ALL, dq), BF16),
            jax.ShapeDtypeStruct((R_ALL, dkv), BF16),
            jax.ShapeDtypeStruct((dkv, R_ALL), BF16),
            jax.ShapeDtypeStruct((R_ALL, dq), BF16),
        ],
        scratch_shapes=[pltpu.VMEM((TN, D), F32)],
        compiler_params=pltpu.CompilerParams(vmem_limit_bytes=VMEM_LIMIT),
        name="mid",
    )(x2, ctx2, mod0, mod1, ys_lat, ys_ctx, rest, glu_w, glu_b, sg_g, sg_b, sg_w, sg_bias, w_out,
      ln_g, ln_b, w_qkg, w_vt, cos_t, sin_t)


N_QBLK = N_LAT // ATT_BLK
GRP = N_HEADS // N_KV
ATT_SUB = 4
ATT_ROWS = ATT_SUB * ATT_BLK
N_QSTEP = N_QBLK // ATT_SUB
N_ATT_STEPS = B * N_QSTEP
assert ATT_ROWS == TN
ONES_ROWS = 16


def _attn_kernel(sink_ref, q_ref, kp_ref, kc_ref, kn_ref, kx_ref, vp_ref, vc_ref, vn_ref, vx_ref,
                 g_ref, z_ref, mod_ref, wout_ref, lng_ref, lnb_ref, out_ref, o_scr):
    step = pl.program_id(0)

    @pl.when(step == 0)
    def _():
        o_scr[...] = jnp.zeros((ATT_ROWS, N_HEADS * HD), BF16)

    t = jnp.maximum(step - 1, 0)
    gate = mod_ref[pl.ds(_block_mod_row(t), 1), 2 * D:3 * D]
    mix = (o_scr[...].astype(F32) * _silu(g_ref[...].astype(F32))).astype(BF16)
    y = jnp.dot(mix, wout_ref[...], preferred_element_type=F32) * gate
    out_ref[...] = _layer_norm(ALPHA * z_ref[...] + y, lng_ref[...], lnb_ref[...])

    i = jnp.minimum(step, N_ATT_STEPS - 1) % N_QSTEP
    n_win = 3 * ATT_BLK
    n_keys = n_win + N_CTX
    nq = GRP * ATT_BLK
    k_win = jnp.concatenate([kp_ref[...], kc_ref[...], kn_ref[...]], axis=0)
    vt_win = jnp.concatenate([vp_ref[...], vc_ref[...], vn_ref[...]], axis=1)
    k_ctx = kx_ref[...]
    vt_ctx = vx_ref[...]

    kpos = lax.broadcasted_iota(jnp.int32, (ATT_BLK, ATT_BLK), 0)
    qpos = lax.broadcasted_iota(jnp.int32, (ATT_BLK, ATT_BLK), 1)
    tile_q = lambda a: jnp.concatenate([a] * GRP, axis=1)
    qgrp = lax.broadcasted_iota(jnp.int32, (1, nq), 1) // ATT_BLK
    ones = jnp.ones((ONES_ROWS, n_keys), BF16)
    q = q_ref[...]

    units = [(u, h) for u in range(ATT_SUB) for h in range(N_KV)]
    k_all, vt_all, bias = [], [], []
    for u in range(ATT_SUB):
        k_all.append(jnp.concatenate([k_win[u * ATT_BLK:u * ATT_BLK + n_win], k_ctx], axis=0))
        vt_all.append(jnp.concatenate([vt_win[:, u * ATT_BLK:u * ATT_BLK + n_win], vt_ctx], axis=1))
        has_prev = (i > 0) if u == 0 else True
        has_next = (i < N_QSTEP - 1) if u == ATT_SUB - 1 else True
        bias.append((tile_q(jnp.where((kpos >= qpos) & has_prev, 0.0, NEG_INF)),
                     tile_q(jnp.where((kpos <= qpos) & has_next, 0.0, NEG_INF))))
    scores = []
    for u, h in units:
        kh = k_all[u][:, h * HD:(h + 1) * HD]
        qh = jnp.concatenate(
            [q[u * ATT_BLK:(u + 1) * ATT_BLK, (h * GRP + g) * HD:(h * GRP + g + 1) * HD]
             for g in range(GRP)], axis=0)
        scores.append(lax.dot_general(kh, qh, (((1,), (1,)), ((), ())),
                                      preferred_element_type=F32))
    probs = []
    for (u, h), s in zip(units, scores):
        s = jnp.concatenate([s[:ATT_BLK] + bias[u][0], s[ATT_BLK:2 * ATT_BLK],
                             s[2 * ATT_BLK:n_win] + bias[u][1], s[n_win:]], axis=0)
        sink = jnp.zeros((1, nq), F32)
        for g in range(GRP):
            sink = jnp.where(qgrp == g, sink_ref[h * GRP + g] * LOG2E, sink)
        m = jnp.maximum(jnp.max(s, axis=0, keepdims=True), sink)
        probs.append((jnp.exp2(s - m).astype(BF16), jnp.exp2(sink - m)))
    outs = [[] for _ in range(ATT_SUB)]
    for (u, h), (p, p_sink) in zip(units, probs):
        vt1 = jnp.concatenate([vt_all[u][h * HD:(h + 1) * HD], ones], axis=0)
        ov = jnp.dot(vt1, p, preferred_element_type=F32)
        o_t = ov[:HD] / (ov[HD:HD + 1] + p_sink)
        outs[u].extend(o_t[:, g * ATT_BLK:(g + 1) * ATT_BLK] for g in range(GRP))
    for u in range(ATT_SUB):
        o_scr[u * ATT_BLK:(u + 1) * ATT_BLK, :] = jnp.transpose(
            jnp.concatenate(outs[u], axis=0)).astype(BF16)


def _attention_tail(sink, q, k, vt, g, z1, mod, w_out, ln_g, ln_b):
    dq = N_HEADS * HD
    dkv = N_KV * HD
    ctx_blk0 = R_LAT // N_CTX
    cur = lambda s: jnp.minimum(s, N_ATT_STEPS - 1)
    batch = lambda s: cur(s) // N_QSTEP
    qstep = lambda s: cur(s) % N_QSTEP
    prev_blk = lambda s: batch(s) * N_QBLK + jnp.maximum(ATT_SUB * qstep(s) - 1, 0)
    next_blk = lambda s: batch(s) * N_QBLK + jnp.minimum(ATT_SUB * (qstep(s) + 1), N_QBLK - 1)
    tail = lambda s: jnp.maximum(s - 1, 0)
    once = dict(pipeline_mode=pl.Buffered(1))
    row = lambda n: pl.BlockSpec((1, n), lambda s, sk: (0, 0))
    return pl.pallas_call(
        _attn_kernel,
        grid_spec=pltpu.PrefetchScalarGridSpec(
            num_scalar_prefetch=1,
            grid=(N_ATT_STEPS + 1,),
            in_specs=[
                pl.BlockSpec((ATT_ROWS, dq), lambda s, sk: (cur(s), 0)),
                pl.BlockSpec((ATT_BLK, dkv), lambda s, sk: (prev_blk(s), 0)),
                pl.BlockSpec((ATT_ROWS, dkv), lambda s, sk: (cur(s), 0)),
                pl.BlockSpec((ATT_BLK, dkv), lambda s, sk: (next_blk(s), 0)),
                pl.BlockSpec((N_CTX, dkv), lambda s, sk: (ctx_blk0 + batch(s), 0)),
                pl.BlockSpec((dkv, ATT_BLK), lambda s, sk: (0, prev_blk(s))),
                pl.BlockSpec((dkv, ATT_ROWS), lambda s, sk: (0, cur(s))),
                pl.BlockSpec((dkv, ATT_BLK), lambda s, sk: (0, next_blk(s))),
                pl.BlockSpec((dkv, N_CTX), lambda s, sk: (0, ctx_blk0 + batch(s))),
                pl.BlockSpec((ATT_ROWS, dq), lambda s, sk: (tail(s), 0)),
                pl.BlockSpec((ATT_ROWS, D), lambda s, sk: (tail(s), 0)),
                pl.BlockSpec((8, 3 * D), lambda s, sk: (0, 0)),
                pl.BlockSpec((dq, D), lambda s, sk: (0, 0), **once),
                row(D), row(D),
            ],
            out_specs=pl.BlockSpec((ATT_ROWS, D), lambda s, sk: (tail(s), 0)),
            scratch_shapes=[pltpu.VMEM((ATT_ROWS, dq), BF16)],
        ),
        out_shape=jax.ShapeDtypeStruct((R_LAT, D), F32),
        compiler_params=pltpu.CompilerParams(vmem_limit_bytes=VMEM_LIMIT),
        name="attention",
    )(sink, q, k, k, k, k, vt, vt, vt, vt, g, z1, mod, w_out, ln_g, ln_b)


def kernel(x, c, ctx, c_ctx, mod_w, mod_b, ln_g, ln_b, e_w_in, e_w_out, s5_lam_re, s5_lam_im,
           s5_log_dt, s5_b_re, s5_b_im, s5_c_re, s5_c_im, s5_d, glu_w, glu_b, sg_ln_g, sg_ln_b,
           sg_w, sg_b, o_w_in, o_w_out, o_sink):
    x2 = x.reshape(R_LAT, D)
    ctx2 = ctx.reshape(R_CTX, D)
    mod = _modulation(c, c_ctx, mod_w, mod_b)

    xa, rest = _in0(x2, ctx2, mod[0], e_w_in[0].astype(BF16))
    t_mat, e_mat, ft_mat, al_re, al_im = _s5_prep(
        s5_lam_re[0], s5_lam_im[0], s5_log_dt[0], s5_b_re[0], s5_b_im[0], s5_c_re[0], s5_c_im[0],
        s5_d[0])
    st_loc = _s5_states(xa, e_mat)
    st_in = _s5_scan(st_loc, al_re, al_im)
    ys_lat, ys_ctx = _s5_out(xa, st_in, t_mat, ft_mat)
    sg_bias = jnp.repeat(jnp.transpose(sg_b[0]), SG_W // SG_HEADS, axis=1)
    cos_t, sin_t = _rope_tables()
    z1, q, k, vt, g = _mid(
        x2, ctx2, mod[0], mod[1], ys_lat, ys_ctx, rest, glu_w[0].astype(BF16),
        glu_b[0].reshape(1, S5_W), sg_ln_g[0].reshape(1, SG_W), sg_ln_b[0].reshape(1, SG_W),
        sg_w[0].astype(BF16), sg_bias, e_w_out[0].astype(BF16), ln_g[0].reshape(1, D),
        ln_b[0].reshape(1, D), o_w_in[0], cos_t, sin_t)

    out = _attention_tail(o_sink[0], q, k, vt, g, z1, mod[1], o_w_out[0].astype(BF16),
                          ln_g[1].reshape(1, D), ln_b[1].reshape(1, D))
    return out.reshape(B, N_LAT, D)
```

```python
import functools
import math

import jax
import jax.numpy as jnp
from jax import lax
from jax.experimental import pallas as pl
from jax.experimental.pallas import tpu as pltpu

F32 = jnp.float32
BF16 = jnp.bfloat16

D = 1024
B = 2
N_LAT = 8192
N_CTX = 256
DEPTH = 2
GRID_W = 64
S5_W = 512
S5_GC = 16
S5_G = 32
S5_P = 64
SG_W = 512
SG_HEADS = 8
SG_CHUNK = 128
N_HEADS = 16
N_KV = 4
HD = 64
WINDOW = 128
ATT_BLK = 128
ROPE_BASE = 10000.0
NEG_INF = -1e30
LN_EPS = 1e-5
ALPHA = (2 * DEPTH) ** 0.25
LOG2E = math.log2(math.e)
EVEN_IN = 2 * S5_W + 3 * SG_W
ODD_IN = 2 * N_HEADS * HD + 2 * N_KV * HD

LANES = 128
VMEM_LIMIT = 56 * 1024 * 1024

R_LAT = B * N_LAT
R_CTX = B * N_CTX
R_ALL = R_LAT + R_CTX
TN = 512
N_LAT_BLK = R_LAT // TN
N_BLK = R_ALL // TN
LAT_BLK_PER_BATCH = N_LAT // TN
CTX_MOD_ROW = B

S5_CHUNK = 8
S5_QT = S5_W // LANES
S5_GPT = LANES // S5_GC
S5_ST = S5_GPT * S5_P
S5_PAIRS = S5_GPT // 2
PAIR_BLK = 2 * S5_GC
LAT_CHUNKS = N_LAT // S5_CHUNK
CTX_CHUNKS = N_CTX // S5_CHUNK
BATCH_CHUNKS = CTX_CHUNKS + LAT_CHUNKS
N_CHUNK = B * BATCH_CHUNKS
N_SEG = 8
SEG_CHUNKS = BATCH_CHUNKS // N_SEG
BWD_RESET_STEP = LAT_CHUNKS - (N_SEG - 1) * SEG_CHUNKS


def _sigmoid(x):
    return 1.0 / (1.0 + jnp.exp(-x))


def _silu(x):
    return x * _sigmoid(x)


def _gelu_tanh(x):
    return 0.5 * x * (1.0 + jnp.tanh(math.sqrt(2.0 / math.pi) * (x + 0.044715 * (x * x * x))))


def _layer_norm(x, g, b):
    mu = jnp.mean(x, axis=-1, keepdims=True)
    xc = x - mu
    var = jnp.mean(xc * xc, axis=-1, keepdims=True)
    return xc * lax.rsqrt(var + LN_EPS) * g + b


def _split_bf16(x):
    hi = x.astype(BF16)
    return hi, (x - hi.astype(F32)).astype(BF16)


def _dot_split(x, y_split, dims):
    x_hi, x_lo = _split_bf16(x)
    y_hi, y_lo = y_split
    dot = lambda a, b: lax.dot_general(a, b, dims, preferred_element_type=F32)
    return dot(x_hi, y_hi) + (dot(x_hi, y_lo) + dot(x_lo, y_hi))


def _block_mod_row(i):
    return jnp.where(i < N_LAT_BLK, i // LAT_BLK_PER_BATCH, CTX_MOD_ROW)


def _mod_kernel(cv_ref, w_ref, b_ref, o_ref):
    s = _silu(cv_ref[...])
    o_ref[0] = _dot_split(s, _split_bf16(w_ref[0]), (((1,), (0,)), ((), ()))) + b_ref[0]


def _modulation(c, c_ctx, mod_w, mod_b):
    cv = jnp.concatenate([c, c_ctx[None], jnp.zeros((8 - B - 1, D), F32)], axis=0)
    tn = 1024
    return pl.pallas_call(
        _mod_kernel,
        grid=(DEPTH, 3 * D // tn),
        in_specs=[
            pl.BlockSpec((8, D), lambda l, j: (0, 0)),
            pl.BlockSpec((1, D, tn), lambda l, j: (l, 0, j)),
            pl.BlockSpec((1, 1, tn), lambda l, j: (l, 0, j)),
        ],
        out_specs=pl.BlockSpec((1, 8, tn), lambda l, j: (l, 0, j)),
        out_shape=jax.ShapeDtypeStruct((DEPTH, 8, 3 * D), F32),
        compiler_params=pltpu.CompilerParams(vmem_limit_bytes=VMEM_LIMIT),
        name="modulation",
    )(cv, mod_w, mod_b.reshape(DEPTH, 1, 3 * D))


def _in0_kernel(x_ref, ctx_ref, mod_ref, w_ref, xa_ref, rest_ref):
    i = pl.program_id(0)
    z = jnp.where(i < N_LAT_BLK, x_ref[...], ctx_ref[...])
    m = mod_ref[pl.ds(_block_mod_row(i), 1), :]
    h = z * (1.0 + m[:, D:2 * D]) + m[:, :D]
    p = jnp.dot(h.astype(BF16), w_ref[...], preferred_element_type=F32)
    xa_ref[...] = p[:, :S5_W]
    rest_ref[...] = p[:, S5_W:].astype(BF16)


def _in0(x2, ctx2, mod, w_in):
    return pl.pallas_call(
        _in0_kernel,
        grid=(N_BLK,),
        in_specs=[
            pl.BlockSpec((TN, D), lambda i: (jnp.minimum(i, N_LAT_BLK - 1), 0)),
            pl.BlockSpec((R_CTX, D), lambda i: (0, 0)),
            pl.BlockSpec((8, 3 * D), lambda i: (0, 0)),
            pl.BlockSpec((D, EVEN_IN), lambda i: (0, 0)),
        ],
        out_specs=[
            pl.BlockSpec((TN, S5_W), lambda i: (i, 0)),
            pl.BlockSpec((TN, EVEN_IN - S5_W), lambda i: (i, 0)),
        ],
        out_shape=[
            jax.ShapeDtypeStruct((R_ALL, S5_W), F32),
            jax.ShapeDtypeStruct((R_ALL, EVEN_IN - S5_W), BF16),
        ],
        compiler_params=pltpu.CompilerParams(vmem_limit_bytes=VMEM_LIMIT),
        name="in0",
    )(x2, ctx2, mod, w_in)


def _s5_prep_kernel(rows_ref, bre_ref, bim_ref, cre_ref, cim_ref, dsk_ref, t_ref, e_ref, ft_ref,
                    alr_ref, ali_ref):
    L = S5_CHUNK
    tile = (LANES, S5_ST)
    same_group = (lax.broadcasted_iota(jnp.int32, tile, 0) // S5_GC
                  == lax.broadcasted_iota(jnp.int32, tile, 1) // S5_P)
    nt_dims = (((1,), (1,)), ((), ()))
    nn_dims = (((1,), (0,)), ((), ()))
    pick_ch = (lax.broadcasted_iota(jnp.int32, (LANES, S5_GC), 0) % S5_GC
               == lax.broadcasted_iota(jnp.int32, (LANES, S5_GC), 1)).astype(BF16)
    rep_st = (lax.broadcasted_iota(jnp.int32, (S5_P, S5_ST), 0)
              == lax.broadcasted_iota(jnp.int32, (S5_P, S5_ST), 1) % S5_P).astype(BF16)

    def spread_b(b):
        hi, lo = _split_bf16(b)
        dot = lambda v: lax.dot_general(pick_ch, v, nt_dims, preferred_element_type=F32)
        return jnp.where(same_group, dot(hi) + dot(lo), 0.0)

    def spread_c(c):
        hi, lo = _split_bf16(c)
        dot = lambda v: lax.dot_general(v, rep_st, nn_dims, preferred_element_type=F32)
        return jnp.where(same_group, dot(hi) + dot(lo), 0.0)

    pair_rows = lambda pr: slice(pr * PAIR_BLK, (pr + 1) * PAIR_BLK)
    pair_lanes = lambda pr: slice(pr * LANES, (pr + 1) * LANES)
    lag_strip = []
    for d in range(2):
        lam_re = rows_ref[d, 0]
        lam_im = rows_ref[d, 1]
        dt = jnp.exp(rows_ref[d, 2])
        pw = []
        for k in range(L + 1):
            mag = jnp.exp(lam_re * dt * float(k))
            ang = lam_im * dt * float(k)
            pw.append((mag * jnp.cos(ang), mag * jnp.sin(ang)))
        alr_ref[d] = pw[L][0]
        ali_ref[d] = pw[L][1]
        den = lam_re * lam_re + lam_im * lam_im
        nr = pw[1][0] - 1.0
        ni = pw[1][1]
        f_re = (nr * lam_re + ni * lam_im) / den
        f_im = (ni * lam_re - nr * lam_im) / den
        b_re = spread_b(bre_ref[d])
        b_im = spread_b(bim_ref[d])
        bb_re = f_re * b_re - f_im * b_im
        bb_im = f_re * b_im + f_im * b_re
        c_re = spread_c(cre_ref[d])
        c_im = spread_c(cim_ref[d])
        for k in range(L):
            x_re = pw[k][0] * bb_re - pw[k][1] * bb_im
            x_im = pw[k][0] * bb_im + pw[k][1] * bb_re
            s = (L - 1 - k) if d == 0 else k
            for pr in range(S5_PAIRS):
                blk = slice(s * PAIR_BLK, (s + 1) * PAIR_BLK)
                e_ref[0, pr, blk, 2 * d * LANES:(2 * d + 1) * LANES] = (
                    x_re[pair_rows(pr), pair_lanes(pr)].astype(BF16))
                e_ref[0, pr, blk, (2 * d + 1) * LANES:(2 * d + 2) * LANES] = (
                    x_im[pair_rows(pr), pair_lanes(pr)].astype(BF16))
        cw = []
        for k in range(L + 1):
            w_re = c_re * pw[k][0] - c_im * pw[k][1]
            w_im = c_re * pw[k][1] + c_im * pw[k][0]
            cw.append((w_re, -w_im))
        for t in range(L):
            k = (t + 1) if d == 0 else (L - t)
            for pr in range(S5_PAIRS):
                blk = slice(t * PAIR_BLK, (t + 1) * PAIR_BLK)
                ft_ref[0, pr, blk, 2 * d * LANES:(2 * d + 1) * LANES] = (
                    cw[k][0][pair_rows(pr), pair_lanes(pr)].astype(BF16))
                ft_ref[0, pr, blk, (2 * d + 1) * LANES:(2 * d + 2) * LANES] = (
                    cw[k][1][pair_rows(pr), pair_lanes(pr)].astype(BF16))
        lags = range(L) if d == 0 else range(L - 1, -1, -1)
        c_stack = jnp.concatenate(
            [jnp.concatenate([cw[k][0][pair_rows(pr)], cw[k][1][pair_rows(pr)]], axis=-1)
             for pr in range(S5_PAIRS) for k in lags], axis=0)
        lag_strip.append(_dot_split(jnp.concatenate([bb_re, bb_im], axis=-1),
                                    _split_bf16(c_stack), nt_dims))

    strip_w = 2 * L * PAIR_BLK
    mid = (L - 1) * PAIR_BLK
    widen = lambda a: jnp.concatenate([a, jnp.zeros((a.shape[0], strip_w - a.shape[1]), F32)], axis=-1)
    row = lax.broadcasted_iota(jnp.int32, (PAIR_BLK, strip_w), 0)
    lane = lax.broadcasted_iota(jnp.int32, (PAIR_BLK, strip_w), 1)
    d_row = widen(dsk_ref[0])
    for pr in range(S5_PAIRS):
        lanes = slice(pr * L * PAIR_BLK, (pr + 1) * L * PAIR_BLK)
        strip = (widen(lag_strip[1][pair_rows(pr), lanes])
                 + pltpu.roll(widen(lag_strip[0][pair_rows(pr), lanes]), mid, axis=1)
                 + jnp.where(lane == row + mid,
                             pltpu.roll(d_row, mid - pr * PAIR_BLK, axis=1), 0.0))
        for s in range(L):
            shift = (strip_w - (L - 1 - s) * PAIR_BLK) % strip_w
            blk = pltpu.roll(strip, shift, axis=1) if shift else strip
            t_ref[0, pr, s * PAIR_BLK:(s + 1) * PAIR_BLK, :] = blk[:, :L * PAIR_BLK].astype(BF16)


def _s5_prep(lam_re, lam_im, log_dt, b_re, b_im, c_re, c_im, d_skip):
    L = S5_CHUNK
    n_st = S5_G * S5_P
    pair_w = L * PAIR_BLK
    rows = jnp.stack([lam_re.reshape(2, 1, n_st), lam_im.reshape(2, 1, n_st),
                      jnp.repeat(log_dt, S5_P, axis=1).reshape(2, 1, n_st)], axis=1)
    b_spec = pl.BlockSpec((2, S5_ST, S5_GC), lambda q: (0, q, 0))
    c_spec = pl.BlockSpec((2, LANES, S5_P), lambda q: (0, q, 0))
    return pl.pallas_call(
        _s5_prep_kernel,
        grid=(S5_QT,),
        in_specs=[
            pl.BlockSpec((2, 3, 1, S5_ST), lambda q: (0, 0, 0, q)),
            b_spec, b_spec, c_spec, c_spec,
            pl.BlockSpec((1, 1, LANES), lambda q: (q, 0, 0)),
        ],
        out_specs=[
            pl.BlockSpec((1, S5_PAIRS, pair_w, pair_w), lambda q: (q, 0, 0, 0)),
            pl.BlockSpec((1, S5_PAIRS, pair_w, 4 * LANES), lambda q: (q, 0, 0, 0)),
            pl.BlockSpec((1, S5_PAIRS, pair_w, 4 * LANES), lambda q: (q, 0, 0, 0)),
            pl.BlockSpec((2, 1, S5_ST), lambda q: (q, 0, 0)),
            pl.BlockSpec((2, 1, S5_ST), lambda q: (q, 0, 0)),
        ],
        out_shape=[
            jax.ShapeDtypeStruct((S5_QT, S5_PAIRS, pair_w, pair_w), BF16),
            jax.ShapeDtypeStruct((S5_QT, S5_PAIRS, pair_w, 4 * LANES), BF16),
            jax.ShapeDtypeStruct((S5_QT, S5_PAIRS, pair_w, 4 * LANES), BF16),
            jax.ShapeDtypeStruct((2 * S5_QT, 1, S5_ST), F32),
            jax.ShapeDtypeStruct((2 * S5_QT, 1, S5_ST), F32),
        ],
        compiler_params=pltpu.CompilerParams(vmem_limit_bytes=VMEM_LIMIT),
        name="s5_prep",
    )(rows, b_re.reshape(2, n_st, S5_GC), b_im.reshape(2, n_st, S5_GC),
      c_re.reshape(2, S5_W, S5_P), c_im.reshape(2, S5_W, S5_P), d_skip.reshape(S5_QT, 1, LANES))


def _block_transpose4(tiles):
    tiles = list(tiles)
    blk = lax.broadcasted_iota(jnp.int32, tiles[0].shape, 1) // PAIR_BLK
    for dist in (2, 1):
        keep = (blk & dist) == 0
        for i in range(len(tiles)):
            if i & dist:
                continue
            lo, hi = tiles[i], tiles[i + dist]
            tiles[i] = jnp.where(keep, lo, pltpu.roll(hi, dist * PAIR_BLK, axis=1))
            tiles[i + dist] = jnp.where(keep, pltpu.roll(lo, LANES - dist * PAIR_BLK, axis=1), hi)
    return tiles


def _pair_rows(xl_ref, xc_ref):
    L = S5_CHUNK
    tiles = [jnp.concatenate([xc_ref[pl.ds(t, CTX_CHUNKS, stride=L), :],
                              xl_ref[pl.ds(t, LAT_CHUNKS, stride=L), :]], axis=0) for t in range(L)]
    halves = [_block_transpose4(tiles[h * S5_PAIRS:(h + 1) * S5_PAIRS]) for h in range(L // S5_PAIRS)]
    return [jnp.concatenate([half[pr] for half in halves], axis=-1).astype(BF16)
            for pr in range(S5_PAIRS)]


def _xa_specs(index_of):
    ctx_blk0 = R_LAT // N_CTX
    return [
        pl.BlockSpec((N_LAT, LANES), lambda *g: (index_of(*g)[1], index_of(*g)[0])),
        pl.BlockSpec((N_CTX, LANES), lambda *g: (ctx_blk0 + index_of(*g)[1], index_of(*g)[0])),
    ]


N_SLAB = S5_ST // LANES
ST_SHAPE = (S5_QT * 2 * 2 * N_SLAB, N_CHUNK, LANES)
ST_SPEC = pl.BlockSpec((2 * N_SLAB, BATCH_CHUNKS, LANES), lambda q, b, d: (2 * q + d, b, 0))


def _s5_states_kernel(xl_ref, xc_ref, e_ref, st_ref):
    for pr, rows in enumerate(_pair_rows(xl_ref, xc_ref)):
        st = jnp.dot(rows, e_ref[0, pr], preferred_element_type=F32)
        for comp in range(4):
            st_ref[comp * N_SLAB + pr] = st[:, comp * LANES:(comp + 1) * LANES]


def _s5_states(xa, e_mat):
    pair_w = S5_CHUNK * PAIR_BLK
    return pl.pallas_call(
        _s5_states_kernel,
        grid=(S5_QT, B),
        in_specs=_xa_specs(lambda q, b: (q, b)) + [
            pl.BlockSpec((1, S5_PAIRS, pair_w, 4 * LANES), lambda q, b: (q, 0, 0, 0)),
        ],
        out_specs=pl.BlockSpec((4 * N_SLAB, BATCH_CHUNKS, LANES), lambda q, b: (q, b, 0)),
        out_shape=jax.ShapeDtypeStruct(ST_SHAPE, F32),
        compiler_params=pltpu.CompilerParams(vmem_limit_bytes=VMEM_LIMIT),
        name="s5_states",
    )(xa, xa, e_mat)


def _cmul(ar, ai, br, bi):
    return ar * br - ai * bi, ar * bi + ai * br


def _cpow(ar, ai, n):
    res = None
    while n:
        if n & 1:
            res = (ar, ai) if res is None else _cmul(res[0], res[1], ar, ai)
        n >>= 1
        if n:
            ar, ai = _cmul(ar, ai, ar, ai)
    return res


def _scan_boundary_states(loc_ref, out_ref, ar_ref, ai_ref, backward):
    a_row = [(ar_ref[0][:, k * LANES:(k + 1) * LANES], ai_ref[0][:, k * LANES:(k + 1) * LANES])
             for k in range(N_SLAB)]
    a_tile = [(jnp.broadcast_to(r, (N_SEG, LANES)), jnp.broadcast_to(i, (N_SEG, LANES)))
              for r, i in a_row]

    def sweep(lo, hi, carry, store):
        def step(i, carry):
            off = (SEG_CHUNKS - 1 - i) if backward else i
            rows = pl.ds(off, N_SEG, stride=SEG_CHUNKS)
            new = []
            for k, (cr, ci) in enumerate(carry):
                xr = loc_ref[k, rows, :]
                xi = loc_ref[N_SLAB + k, rows, :]
                if store:
                    out_ref[k, rows, :] = cr
                    out_ref[N_SLAB + k, rows, :] = ci
                nr, ni = _cmul(a_tile[k][0], a_tile[k][1], cr, ci)
                new.append((nr + xr, ni + xi))
            return tuple(new)
        return lax.fori_loop(lo, hi, step, carry)

    def full(carry, store):
        if not backward:
            return sweep(0, SEG_CHUNKS, carry, store)
        carry = sweep(0, BWD_RESET_STEP, carry, store)
        first = lax.broadcasted_iota(jnp.int32, (N_SEG, LANES), 0) == 0
        carry = tuple((jnp.where(first, 0.0, cr), jnp.where(first, 0.0, ci)) for cr, ci in carry)
        return sweep(BWD_RESET_STEP, SEG_CHUNKS, carry, store)

    zero_tile = jnp.zeros((N_SEG, LANES), F32)
    ends = full(tuple((zero_tile, zero_tile) for _ in range(N_SLAB)), False)

    starts = []
    for k, (er, ei) in enumerate(ends):
        pr, pi = _cpow(a_row[k][0], a_row[k][1], SEG_CHUNKS)
        if backward:
            order = range(N_SEG - 1, -1, -1)
            g = (er[0:1], ei[0:1])
        else:
            order = range(N_SEG)
            g = (jnp.zeros((1, LANES), F32), jnp.zeros((1, LANES), F32))
        rows_r = [None] * N_SEG
        rows_i = [None] * N_SEG
        for j in order:
            rows_r[j], rows_i[j] = g
            nr, ni = _cmul(pr, pi, g[0], g[1])
            g = (nr + er[j:j + 1], ni + ei[j:j + 1])
        starts.append((jnp.concatenate(rows_r, axis=0), jnp.concatenate(rows_i, axis=0)))
    full(tuple(starts), True)


def _s5_out_kernel(xl_ref, xc_ref, st_ref, ar_ref, ai_ref, t_ref, ft_ref, yl_ref, yc_ref, acc_ref,
                   sc_ref):
    L = S5_CHUNK
    d = pl.program_id(2)

    def from_states(pr):
        st = jnp.concatenate([sc_ref[pr], sc_ref[N_SLAB + pr]], axis=-1).astype(BF16)
        return lax.dot_general(st, ft_ref[0, pr], (((1,), (1,)), ((), ())),
                               preferred_element_type=F32)

    @pl.when(d == 0)
    def _():
        _scan_boundary_states(st_ref, sc_ref, ar_ref, ai_ref, False)
        for pr, rows in enumerate(_pair_rows(xl_ref, xc_ref)):
            acc_ref[pr] = from_states(pr) + jnp.dot(rows, t_ref[0, pr],
                                                    preferred_element_type=F32)

    @pl.when(d == 1)
    def _():
        _scan_boundary_states(st_ref, sc_ref, ar_ref, ai_ref, True)
        ys = [acc_ref[pr] + from_states(pr) for pr in range(S5_PAIRS)]
        tiles = [tile for h in range(L // S5_PAIRS) for tile in _block_transpose4(
            [y[:, h * LANES:(h + 1) * LANES] for y in ys])]
        for t, tile in enumerate(tiles):
            yc_ref[pl.ds(t, CTX_CHUNKS, stride=L), :] = tile[:CTX_CHUNKS]
            yl_ref[pl.ds(t, LAT_CHUNKS, stride=L), :] = tile[CTX_CHUNKS:]


def _s5_out(xa, st_loc, al_re, al_im, t_mat, ft_mat):
    pair_w = S5_CHUNK * PAIR_BLK
    return pl.pallas_call(
        _s5_out_kernel,
        grid=(S5_QT, B, 2),
        in_specs=_xa_specs(lambda q, b, d: (q, b)) + [
            ST_SPEC,
            pl.BlockSpec((1, 1, S5_ST), lambda q, b, d: (2 * q + d, 0, 0)),
            pl.BlockSpec((1, 1, S5_ST), lambda q, b, d: (2 * q + d, 0, 0)),
            pl.BlockSpec((1, S5_PAIRS, pair_w, pair_w), lambda q, b, d: (q, 0, 0, 0)),
            pl.BlockSpec((1, S5_PAIRS, pair_w, 2 * LANES), lambda q, b, d: (q, 0, 0, d)),
        ],
        out_specs=[
            pl.BlockSpec((N_LAT, LANES), lambda q, b, d: (b, q)),
            pl.BlockSpec((N_CTX, LANES), lambda q, b, d: (b, q)),
        ],
        out_shape=[
            jax.ShapeDtypeStruct((R_LAT, S5_W), F32),
            jax.ShapeDtypeStruct((R_CTX, S5_W), F32),
        ],
        scratch_shapes=[pltpu.VMEM((S5_PAIRS, BATCH_CHUNKS, pair_w), F32),
                        pltpu.VMEM((2 * N_SLAB, BATCH_CHUNKS, LANES), F32)],
        compiler_params=pltpu.CompilerParams(vmem_limit_bytes=VMEM_LIMIT),
        name="s5_out",
    )(xa, xa, st_loc, al_re, al_im, t_mat, ft_mat)


def _tail0_stages(z, ys, rest_ref, gate, gluw_ref, glub_ref, sg_g_ref, sg_b_ref, sgw_ref,
                  sgbias_ref, wout_ref, lng_ref, lnb_ref):
    ga = rest_ref[:, 0:S5_W].astype(F32)
    u = rest_ref[:, S5_W:2 * S5_W].astype(F32)
    v = rest_ref[:, 2 * S5_W:3 * S5_W].astype(F32)
    gb = rest_ref[:, 3 * S5_W:4 * S5_W].astype(F32)

    ya = _gelu_tanh(ys)
    glu = jnp.dot(ya.astype(BF16), gluw_ref[...], preferred_element_type=F32) + glub_ref[...]
    yield None
    ya = ya * _sigmoid(glu) * _silu(ga)
    yield None

    vn = _layer_norm(v, sg_g_ref[...], sg_b_ref[...]).astype(BF16)
    lane = lax.broadcasted_iota(jnp.int32, (SG_CHUNK, LANES), 1)
    first_head = lane < (SG_W // SG_HEADS)
    chunks = []
    for c in range(TN // SG_CHUNK):
        tiles = []
        for j in range(SG_W // LANES):
            vt = vn[c * SG_CHUNK:(c + 1) * SG_CHUNK, j * LANES:(j + 1) * LANES]
            r0 = jnp.dot(sgw_ref[2 * j], vt, preferred_element_type=F32)
            r1 = jnp.dot(sgw_ref[2 * j + 1], vt, preferred_element_type=F32)
            tiles.append(jnp.where(first_head, r0, r1))
        chunks.append(jnp.concatenate(tiles, axis=-1) + sgbias_ref[...])
    s = jnp.concatenate(chunks, axis=0)
    yb = u * s * _silu(gb)
    yield None

    mix = jnp.concatenate([ya, yb], axis=-1).astype(BF16)
    y = jnp.dot(mix, wout_ref[...], preferred_element_type=F32) * gate
    yield None
    yield _layer_norm(ALPHA * z + y, lng_ref[...], lnb_ref[...])


def _rope_tables():
    nf = HD // 4
    n_rows = N_LAT // GRID_W
    lane = jnp.arange(LANES)
    inv = ROPE_BASE ** (-(lane % nf).astype(F32) / nf)
    by_row = ((lane % HD) // (HD // 2) == 0)[None, :]
    sign = jnp.where((lane % (HD // 2)) < nf, -1.0, 1.0)[None, :]
    row_ang = jnp.arange(n_rows, dtype=F32)[:, None] * inv[None, :]
    col_ang = jnp.arange(GRID_W, dtype=F32)[:, None] * inv[None, :]
    zero = jnp.zeros((), F32)
    cos_t = (jnp.where(by_row, jnp.cos(row_ang), zero)[:, None, :]
             + jnp.where(by_row, zero, jnp.cos(col_ang))[None, :, :]).reshape(N_LAT, LANES)
    sin_t = (jnp.where(by_row, sign * jnp.sin(row_ang), zero)[:, None, :]
             + jnp.where(by_row, zero, sign * jnp.sin(col_ang))[None, :, :]).reshape(N_LAT, LANES)
    return cos_t, sin_t


def _rope_tile(x, cos, sin, first_half):
    nf = HD // 4
    partner = jnp.where(first_half, pltpu.roll(x, LANES - nf, axis=1), pltpu.roll(x, nf, axis=1))
    return x * cos + partner * sin


def _mid_kernel(x_ref, ctx_ref, mod0_ref, mod1_ref, ysl_ref, ysc_ref, rest_ref, gluw_ref, glub_ref,
                sg_g_ref, sg_b_ref, sgw_ref, sgbias_ref, wout_ref, lng_ref, lnb_ref, w_ref,
                wvt_ref, cos_ref, sin_ref, z1_ref, q_ref, k_ref, vt_ref, g_ref, zprev_ref):
    i = pl.program_id(0)
    dq = N_HEADS * HD
    dkv = N_KV * HD

    @pl.when(i == 0)
    def _():
        zprev_ref[...] = jnp.zeros((TN, D), F32)

    j = jnp.maximum(i - 1, 0)
    m1 = mod1_ref[pl.ds(_block_mod_row(j), 1), :]
    h = (zprev_ref[...] * (1.0 + m1[:, D:2 * D]) + m1[:, :D]).astype(BF16)
    cos = jnp.where(j < N_LAT_BLK, cos_ref[...], 1.0)
    sin = jnp.where(j < N_LAT_BLK, sin_ref[...], 0.0)
    lane = lax.broadcasted_iota(jnp.int32, (TN, LANES), 1)
    first_half = (lane % (HD // 2)) < (HD // 4)
    scale = HD ** -0.5 * LOG2E

    def project(c0, c1):
        return jnp.dot(h, w_ref[:, c0:c1], preferred_element_type=F32)

    def roped_tiles(p, mult):
        for c in range(p.shape[1] // LANES):
            r = _rope_tile(p[:, c * LANES:(c + 1) * LANES], cos, sin, first_half)
            yield c, (r * mult if mult != 1.0 else r).astype(BF16)

    t = jnp.minimum(i, N_BLK - 1)
    z = jnp.where(t < N_LAT_BLK, x_ref[...], ctx_ref[...])
    ys = jnp.where(t < N_LAT_BLK, ysl_ref[...], ysc_ref[...])
    gate = mod0_ref[pl.ds(_block_mod_row(t), 1), 2 * D:3 * D]
    tail = _tail0_stages(z, ys, rest_ref, gate, gluw_ref, glub_ref, sg_g_ref, sg_b_ref, sgw_ref,
                         sgbias_ref, wout_ref, lng_ref, lnb_ref)

    half_q = dq // 2
    p_q0 = project(0, half_q)
    next(tail)
    p_q1 = project(half_q, dq)
    next(tail)
    p_k = project(dq, dq + dkv)
    vt_ref[...] = lax.dot_general(wvt_ref[...], h, (((1,), (1,)), ((), ())),
                                  preferred_element_type=F32).astype(BF16)
    next(tail)
    next(tail)
    for c0, p in ((0, p_q0), (half_q, p_q1)):
        for c, r in roped_tiles(p, scale):
            q_ref[:, c0 + c * LANES:c0 + (c + 1) * LANES] = r
    for c, r in roped_tiles(p_k, 1.0):
        k_ref[:, c * LANES:(c + 1) * LANES] = r
    p_g = project(dq + 2 * dkv, ODD_IN)
    z1 = next(tail)
    z1_ref[...] = z1
    zprev_ref[...] = z1
    g_ref[...] = p_g.astype(BF16)


def _mid(x2, ctx2, mod0, mod1, ys_lat, ys_ctx, rest, glu_w, glu_b, sg_g, sg_b, sg_w, sg_bias,
         w_out, ln_g, ln_b, w_in1, cos_t, sin_t):
    dq = N_HEADS * HD
    dkv = N_KV * HD
    w_vt = jnp.transpose(w_in1[:, dq + dkv:dq + 2 * dkv]).astype(BF16)
    once = dict(pipeline_mode=pl.Buffered(1))
    row = lambda n: pl.BlockSpec((1, n), lambda i: (0, 0))
    lat = lambda i: (jnp.minimum(i, N_LAT_BLK - 1), 0)
    tail_blk = lambda i: (jnp.minimum(i, N_BLK - 1), 0)
    proj_blk = lambda i: (jnp.maximum(i - 1, 0), 0)
    tab = lambda i: (jnp.maximum(i - 1, 0) % LAT_BLK_PER_BATCH, 0)
    return pl.pallas_call(
        _mid_kernel,
        grid=(N_BLK + 1,),
        in_specs=[
            pl.BlockSpec((TN, D), lat),
            pl.BlockSpec((R_CTX, D), lambda i: (0, 0), **once),
            pl.BlockSpec((8, 3 * D), lambda i: (0, 0)),
            pl.BlockSpec((8, 3 * D), lambda i: (0, 0)),
            pl.BlockSpec((TN, S5_W), lat),
            pl.BlockSpec((R_CTX, S5_W), lambda i: (0, 0), **once),
            pl.BlockSpec((TN, EVEN_IN - S5_W), tail_blk),
            pl.BlockSpec((S5_W, S5_W), lambda i: (0, 0), **once),
            row(S5_W), row(SG_W), row(SG_W),
            pl.BlockSpec((SG_HEADS, SG_CHUNK, SG_CHUNK), lambda i: (0, 0, 0)),
            pl.BlockSpec((SG_CHUNK, SG_W), lambda i: (0, 0)),
            pl.BlockSpec((S5_W + SG_W, D), lambda i: (0, 0), **once),
            row(D), row(D),
            pl.BlockSpec((D, ODD_IN), lambda i: (0, 0), **once),
            pl.BlockSpec((dkv, D), lambda i: (0, 0), **once),
            pl.BlockSpec((TN, LANES), tab),
            pl.BlockSpec((TN, LANES), tab),
        ],
        out_specs=[
            pl.BlockSpec((TN, D), tail_blk),
            pl.BlockSpec((TN, dq), proj_blk),
            pl.BlockSpec((TN, dkv), proj_blk),
            pl.BlockSpec((dkv, TN), lambda i: (0, jnp.maximum(i - 1, 0))),
            pl.BlockSpec((TN, dq), proj_blk),
        ],
        out_shape=[
            jax.ShapeDtypeStruct((R_ALL, D), F32),
            jax.ShapeDtypeStruct((R_ALL, dq), BF16),
            jax.ShapeDtypeStruct((R_ALL, dkv), BF16),
            jax.ShapeDtypeStruct((dkv, R_ALL), BF16),
            jax.ShapeDtypeStruct((R_ALL, dq), BF16),
        ],
        scratch_shapes=[pltpu.VMEM((TN, D), F32)],
        compiler_params=pltpu.CompilerParams(vmem_limit_bytes=VMEM_LIMIT),
        name="mid",
    )(x2, ctx2, mod0, mod1, ys_lat, ys_ctx, rest, glu_w, glu_b, sg_g, sg_b, sg_w, sg_bias, w_out,
      ln_g, ln_b, w_in1.astype(BF16), w_vt, cos_t, sin_t)


N_QBLK = N_LAT // ATT_BLK
GRP = N_HEADS // N_KV
ATT_SUB = 4
ATT_ROWS = ATT_SUB * ATT_BLK
N_QSTEP = N_QBLK // ATT_SUB
N_ATT_STEPS = B * N_QSTEP
assert ATT_ROWS == TN
ONES_ROWS = 16


def _attn_kernel(sink_ref, q_ref, kp_ref, kc_ref, kn_ref, kx_ref, vp_ref, vc_ref, vn_ref, vx_ref,
                 g_ref, z_ref, mod_ref, wout_ref, lng_ref, lnb_ref, out_ref, o_scr):
    step = pl.program_id(0)

    @pl.when(step == 0)
    def _():
        o_scr[...] = jnp.zeros((ATT_ROWS, N_HEADS * HD), BF16)

    t = jnp.maximum(step - 1, 0)
    gate = mod_ref[pl.ds(_block_mod_row(t), 1), 2 * D:3 * D]
    mix = (o_scr[...].astype(F32) * _silu(g_ref[...].astype(F32))).astype(BF16)
    y = jnp.dot(mix, wout_ref[...], preferred_element_type=F32) * gate
    out_ref[...] = _layer_norm(ALPHA * z_ref[...] + y, lng_ref[...], lnb_ref[...])

    i = jnp.minimum(step, N_ATT_STEPS - 1) % N_QSTEP
    n_win = 3 * ATT_BLK
    n_keys = n_win + N_CTX
    nq = GRP * ATT_BLK
    k_win = jnp.concatenate([kp_ref[...], kc_ref[...], kn_ref[...]], axis=0)
    vt_win = jnp.concatenate([vp_ref[...], vc_ref[...], vn_ref[...]], axis=1)
    k_ctx = kx_ref[...]
    vt_ctx = vx_ref[...]

    kpos = lax.broadcasted_iota(jnp.int32, (ATT_BLK, ATT_BLK), 0)
    qpos = lax.broadcasted_iota(jnp.int32, (ATT_BLK, ATT_BLK), 1)
    tile_q = lambda a: jnp.concatenate([a] * GRP, axis=1)
    qgrp = lax.broadcasted_iota(jnp.int32, (1, nq), 1) // ATT_BLK
    ones = jnp.ones((ONES_ROWS, n_keys), BF16)
    q = q_ref[...]

    units = [(u, h) for u in range(ATT_SUB) for h in range(N_KV)]
    k_all, vt_all, bias = [], [], []
    for u in range(ATT_SUB):
        k_all.append(jnp.concatenate([k_win[u * ATT_BLK:u * ATT_BLK + n_win], k_ctx], axis=0))
        vt_all.append(jnp.concatenate([vt_win[:, u * ATT_BLK:u * ATT_BLK + n_win], vt_ctx], axis=1))
        has_prev = (i > 0) if u == 0 else True
        has_next = (i < N_QSTEP - 1) if u == ATT_SUB - 1 else True
        bias.append((tile_q(jnp.where((kpos >= qpos) & has_prev, 0.0, NEG_INF)),
                     tile_q(jnp.where((kpos <= qpos) & has_next, 0.0, NEG_INF))))
    scores = []
    for u, h in units:
        kh = k_all[u][:, h * HD:(h + 1) * HD]
        qh = jnp.concatenate(
            [q[u * ATT_BLK:(u + 1) * ATT_BLK, (h * GRP + g) * HD:(h * GRP + g + 1) * HD]
             for g in range(GRP)], axis=0)
        scores.append(lax.dot_general(kh, qh, (((1,), (1,)), ((), ())),
                                      preferred_element_type=F32))
    probs = []
    for (u, h), s in zip(units, scores):
        s = jnp.concatenate([s[:ATT_BLK] + bias[u][0], s[ATT_BLK:2 * ATT_BLK],
                             s[2 * ATT_BLK:n_win] + bias[u][1], s[n_win:]], axis=0)
        sink = jnp.zeros((1, nq), F32)
        for g in range(GRP):
            sink = jnp.where(qgrp == g, sink_ref[h * GRP + g] * LOG2E, sink)
        m = jnp.maximum(jnp.max(s, axis=0, keepdims=True), sink)
        probs.append((jnp.exp2(s - m).astype(BF16), jnp.exp2(sink - m)))
    outs = [[] for _ in range(ATT_SUB)]
    for (u, h), (p, p_sink) in zip(units, probs):
        vt1 = jnp.concatenate([vt_all[u][h * HD:(h + 1) * HD], ones], axis=0)
        ov = jnp.dot(vt1, p, preferred_element_type=F32)
        o_t = ov[:HD] / (ov[HD:HD + 1] + p_sink)
        outs[u].extend(o_t[:, g * ATT_BLK:(g + 1) * ATT_BLK] for g in range(GRP))
    for u in range(ATT_SUB):
        o_scr[u * ATT_BLK:(u + 1) * ATT_BLK, :] = jnp.transpose(
            jnp.concatenate(outs[u], axis=0)).astype(BF16)


def _attention_tail(sink, q, k, vt, g, z1, mod, w_out, ln_g, ln_b):
    dq = N_HEADS * HD
    dkv = N_KV * HD
    ctx_blk0 = R_LAT // N_CTX
    cur = lambda s: jnp.minimum(s, N_ATT_STEPS - 1)
    batch = lambda s: cur(s) // N_QSTEP
    qstep = lambda s: cur(s) % N_QSTEP
    prev_blk = lambda s: batch(s) * N_QBLK + jnp.maximum(ATT_SUB * qstep(s) - 1, 0)
    next_blk = lambda s: batch(s) * N_QBLK + jnp.minimum(ATT_SUB * (qstep(s) + 1), N_QBLK - 1)
    tail = lambda s: jnp.maximum(s - 1, 0)
    once = dict(pipeline_mode=pl.Buffered(1))
    row = lambda n: pl.BlockSpec((1, n), lambda s, sk: (0, 0))
    return pl.pallas_call(
        _attn_kernel,
        grid_spec=pltpu.PrefetchScalarGridSpec(
            num_scalar_prefetch=1,
            grid=(N_ATT_STEPS + 1,),
            in_specs=[
                pl.BlockSpec((ATT_ROWS, dq), lambda s, sk: (cur(s), 0)),
                pl.BlockSpec((ATT_BLK, dkv), lambda s, sk: (prev_blk(s), 0)),
                pl.BlockSpec((ATT_ROWS, dkv), lambda s, sk: (cur(s), 0)),
                pl.BlockSpec((ATT_BLK, dkv), lambda s, sk: (next_blk(s), 0)),
                pl.BlockSpec((N_CTX, dkv), lambda s, sk: (ctx_blk0 + batch(s), 0)),
                pl.BlockSpec((dkv, ATT_BLK), lambda s, sk: (0, prev_blk(s))),
                pl.BlockSpec((dkv, ATT_ROWS), lambda s, sk: (0, cur(s))),
                pl.BlockSpec((dkv, ATT_BLK), lambda s, sk: (0, next_blk(s))),
                pl.BlockSpec((dkv, N_CTX), lambda s, sk: (0, ctx_blk0 + batch(s))),
                pl.BlockSpec((ATT_ROWS, dq), lambda s, sk: (tail(s), 0)),
                pl.BlockSpec((ATT_ROWS, D), lambda s, sk: (tail(s), 0)),
                pl.BlockSpec((8, 3 * D), lambda s, sk: (0, 0)),
                pl.BlockSpec((dq, D), lambda s, sk: (0, 0), **once),
                row(D), row(D),
            ],
            out_specs=pl.BlockSpec((ATT_ROWS, D), lambda s, sk: (tail(s), 0)),
            scratch_shapes=[pltpu.VMEM((ATT_ROWS, dq), BF16)],
        ),
        out_shape=jax.ShapeDtypeStruct((R_LAT, D), F32),
        compiler_params=pltpu.CompilerParams(vmem_limit_bytes=VMEM_LIMIT),
        name="attention",
    )(sink, q, k, k, k, k, vt, vt, vt, vt, g, z1, mod, w_out, ln_g, ln_b)


def kernel(x, c, ctx, c_ctx, mod_w, mod_b, ln_g, ln_b, e_w_in, e_w_out, s5_lam_re, s5_lam_im,
           s5_log_dt, s5_b_re, s5_b_im, s5_c_re, s5_c_im, s5_d, glu_w, glu_b, sg_ln_g, sg_ln_b,
           sg_w, sg_b, o_w_in, o_w_out, o_sink):
    x2 = x.reshape(R_LAT, D)
    ctx2 = ctx.reshape(R_CTX, D)
    mod = _modulation(c, c_ctx, mod_w, mod_b)

    xa, rest = _in0(x2, ctx2, mod[0], e_w_in[0].astype(BF16))
    t_mat, e_mat, ft_mat, al_re, al_im = _s5_prep(
        s5_lam_re[0], s5_lam_im[0], s5_log_dt[0], s5_b_re[0], s5_b_im[0], s5_c_re[0], s5_c_im[0],
        s5_d[0])
    st_loc = _s5_states(xa, e_mat)
    ys_lat, ys_ctx = _s5_out(xa, st_loc, al_re, al_im, t_mat, ft_mat)
    sg_bias = jnp.repeat(jnp.transpose(sg_b[0]), SG_W // SG_HEADS, axis=1)
    cos_t, sin_t = _rope_tables()
    z1, q, k, vt, g = _mid(
        x2, ctx2, mod[0], mod[1], ys_lat, ys_ctx, rest, glu_w[0].astype(BF16),
        glu_b[0].reshape(1, S5_W), sg_ln_g[0].reshape(1, SG_W), sg_ln_b[0].reshape(1, SG_W),
        sg_w[0].astype(BF16), sg_bias, e_w_out[0].astype(BF16), ln_g[0].reshape(1, D),
        ln_b[0].reshape(1, D), o_w_in[0], cos_t, sin_t)

    out = _attention_tail(o_sink[0], q, k, vt, g, z1, mod[1], o_w_out[0].astype(BF16),
                          ln_g[1].reshape(1, D), ln_b[1].reshape(1, D))
    return out.reshape(B, N_LAT, D)
```

```python
import functools
import math

import jax
import jax.numpy as jnp
from jax import lax
from jax.experimental import pallas as pl
from jax.experimental.pallas import tpu as pltpu

F32 = jnp.float32
BF16 = jnp.bfloat16

D = 1024
B = 2
N_LAT = 8192
N_CTX = 256
DEPTH = 2
GRID_W = 64
S5_W = 512
S5_GC = 16
S5_G = 32
S5_P = 64
SG_W = 512
SG_HEADS = 8
SG_CHUNK = 128
N_HEADS = 16
N_KV = 4
HD = 64
WINDOW = 128
ATT_BLK = 128
ROPE_BASE = 10000.0
NEG_INF = -1e30
LN_EPS = 1e-5
ALPHA = (2 * DEPTH) ** 0.25
LOG2E = math.log2(math.e)
EVEN_IN = 2 * S5_W + 3 * SG_W
ODD_IN = 2 * N_HEADS * HD + 2 * N_KV * HD

LANES = 128
VMEM_LIMIT = 56 * 1024 * 1024

R_LAT = B * N_LAT
R_CTX = B * N_CTX
R_ALL = R_LAT + R_CTX
TN = 512
N_LAT_BLK = R_LAT // TN
N_BLK = R_ALL // TN
LAT_BLK_PER_BATCH = N_LAT // TN
CTX_MOD_ROW = B

S5_CHUNK = 8
S5_QT = S5_W // LANES
S5_GPT = LANES // S5_GC
S5_ST = S5_GPT * S5_P
S5_PAIRS = S5_GPT // 2
PAIR_BLK = 2 * S5_GC
LAT_CHUNKS = N_LAT // S5_CHUNK
CTX_CHUNKS = N_CTX // S5_CHUNK
BATCH_CHUNKS = CTX_CHUNKS + LAT_CHUNKS
N_SEG = 8
SEG_CHUNKS = BATCH_CHUNKS // N_SEG
BWD_RESET_STEP = LAT_CHUNKS - (N_SEG - 1) * SEG_CHUNKS


def _sigmoid(x):
    return 1.0 / (1.0 + jnp.exp(-x))


def _silu(x):
    return x * _sigmoid(x)


def _gelu_tanh(x):
    return 0.5 * x * (1.0 + jnp.tanh(math.sqrt(2.0 / math.pi) * (x + 0.044715 * (x * x * x))))


def _layer_norm(x, g, b):
    mu = jnp.mean(x, axis=-1, keepdims=True)
    xc = x - mu
    var = jnp.mean(xc * xc, axis=-1, keepdims=True)
    return xc * lax.rsqrt(var + LN_EPS) * g + b


def _split_bf16(x):
    hi = x.astype(BF16)
    return hi, (x - hi.astype(F32)).astype(BF16)


def _dot_split(x, y_split, dims):
    x_hi, x_lo = _split_bf16(x)
    y_hi, y_lo = y_split
    dot = lambda a, b: lax.dot_general(a, b, dims, preferred_element_type=F32)
    return dot(x_hi, y_hi) + (dot(x_hi, y_lo) + dot(x_lo, y_hi))


def _block_mod_row(i):
    return jnp.where(i < N_LAT_BLK, i // LAT_BLK_PER_BATCH, CTX_MOD_ROW)


def _mod_kernel(cv_ref, w_ref, b_ref, o_ref):
    s = _silu(cv_ref[...])
    o_ref[0] = _dot_split(s, _split_bf16(w_ref[0]), (((1,), (0,)), ((), ()))) + b_ref[0]


def _modulation(c, c_ctx, mod_w, mod_b):
    cv = jnp.concatenate([c, c_ctx[None], jnp.zeros((8 - B - 1, D), F32)], axis=0)
    tn = 1024
    return pl.pallas_call(
        _mod_kernel,
        grid=(DEPTH, 3 * D // tn),
        in_specs=[
            pl.BlockSpec((8, D), lambda l, j: (0, 0)),
            pl.BlockSpec((1, D, tn), lambda l, j: (l, 0, j)),
            pl.BlockSpec((1, 1, tn), lambda l, j: (l, 0, j)),
        ],
        out_specs=pl.BlockSpec((1, 8, tn), lambda l, j: (l, 0, j)),
        out_shape=jax.ShapeDtypeStruct((DEPTH, 8, 3 * D), F32),
        compiler_params=pltpu.CompilerParams(vmem_limit_bytes=VMEM_LIMIT),
        name="modulation",
    )(cv, mod_w, mod_b.reshape(DEPTH, 1, 3 * D))


def _block_transpose4(tiles):
    tiles = list(tiles)
    blk = lax.broadcasted_iota(jnp.int32, tiles[0].shape, 1) // PAIR_BLK
    for dist in (2, 1):
        keep = (blk & dist) == 0
        for i in range(len(tiles)):
            if i & dist:
                continue
            lo, hi = tiles[i], tiles[i + dist]
            tiles[i] = jnp.where(keep, lo, pltpu.roll(hi, dist * PAIR_BLK, axis=1))
            tiles[i + dist] = jnp.where(keep, pltpu.roll(lo, LANES - dist * PAIR_BLK, axis=1), hi)
    return tiles


def _in0_kernel(x_ref, ctx_ref, mod_ref, w_ref, vp_ref, rest_ref, xa_scr):
    L = S5_CHUNK
    i = pl.program_id(0)
    z = jnp.where(i < N_LAT_BLK, x_ref[...], ctx_ref[...])
    m = mod_ref[pl.ds(_block_mod_row(i), 1), :]
    h = z * (1.0 + m[:, D:2 * D]) + m[:, :D]
    p = jnp.dot(h.astype(BF16), w_ref[...], preferred_element_type=F32)
    rest_ref[...] = p[:, S5_W:].astype(BF16)
    for q in range(S5_QT):
        xa_scr[q] = p[:, q * LANES:(q + 1) * LANES]
        tiles = [xa_scr[q, pl.ds(t, TN // L, stride=L), :] for t in range(L)]
        halves = [_block_transpose4(tiles[k * S5_PAIRS:(k + 1) * S5_PAIRS])
                  for k in range(L // S5_PAIRS)]
        for pr in range(S5_PAIRS):
            vp_ref[q * S5_PAIRS + pr] = jnp.concatenate(
                [half[pr] for half in halves], axis=-1).astype(BF16)


def _in0(x2, ctx2, mod, w_in):
    pair_w = S5_CHUNK * PAIR_BLK
    return pl.pallas_call(
        _in0_kernel,
        grid=(N_BLK,),
        in_specs=[
            pl.BlockSpec((TN, D), lambda i: (jnp.minimum(i, N_LAT_BLK - 1), 0)),
            pl.BlockSpec((R_CTX, D), lambda i: (0, 0)),
            pl.BlockSpec((8, 3 * D), lambda i: (0, 0)),
            pl.BlockSpec((D, EVEN_IN), lambda i: (0, 0)),
        ],
        out_specs=[
            pl.BlockSpec((S5_QT * S5_PAIRS, TN // S5_CHUNK, pair_w), lambda i: (0, i, 0)),
            pl.BlockSpec((TN, EVEN_IN - S5_W), lambda i: (i, 0)),
        ],
        out_shape=[
            jax.ShapeDtypeStruct((S5_QT * S5_PAIRS, R_ALL // S5_CHUNK, pair_w), BF16),
            jax.ShapeDtypeStruct((R_ALL, EVEN_IN - S5_W), BF16),
        ],
        scratch_shapes=[pltpu.VMEM((S5_QT, TN, LANES), F32)],
        compiler_params=pltpu.CompilerParams(vmem_limit_bytes=VMEM_LIMIT),
        name="in0",
    )(x2, ctx2, mod, w_in)


def _s5_prep_kernel(rows_ref, bre_ref, bim_ref, cre_ref, cim_ref, dsk_ref, t_ref, e_ref, ft_ref,
                    alr_ref, ali_ref):
    L = S5_CHUNK
    tile = (LANES, S5_ST)
    same_group = (lax.broadcasted_iota(jnp.int32, tile, 0) // S5_GC
                  == lax.broadcasted_iota(jnp.int32, tile, 1) // S5_P)
    nt_dims = (((1,), (1,)), ((), ()))
    nn_dims = (((1,), (0,)), ((), ()))
    pick_ch = (lax.broadcasted_iota(jnp.int32, (LANES, S5_GC), 0) % S5_GC
               == lax.broadcasted_iota(jnp.int32, (LANES, S5_GC), 1)).astype(BF16)
    rep_st = (lax.broadcasted_iota(jnp.int32, (S5_P, S5_ST), 0)
              == lax.broadcasted_iota(jnp.int32, (S5_P, S5_ST), 1) % S5_P).astype(BF16)

    def spread_b(b):
        hi, lo = _split_bf16(b)
        dot = lambda v: lax.dot_general(pick_ch, v, nt_dims, preferred_element_type=F32)
        return jnp.where(same_group, dot(hi) + dot(lo), 0.0)

    def spread_c(c):
        hi, lo = _split_bf16(c)
        dot = lambda v: lax.dot_general(v, rep_st, nn_dims, preferred_element_type=F32)
        return jnp.where(same_group, dot(hi) + dot(lo), 0.0)

    pair_rows = lambda pr: slice(pr * PAIR_BLK, (pr + 1) * PAIR_BLK)
    pair_lanes = lambda pr: slice(pr * LANES, (pr + 1) * LANES)
    lag_strip = []
    for d in range(2):
        lam_re = rows_ref[d, 0]
        lam_im = rows_ref[d, 1]
        dt = jnp.exp(rows_ref[d, 2])
        pw = []
        for k in range(L + 1):
            mag = jnp.exp(lam_re * dt * float(k))
            ang = lam_im * dt * float(k)
            pw.append((mag * jnp.cos(ang), mag * jnp.sin(ang)))
        alr_ref[d] = pw[L][0]
        ali_ref[d] = pw[L][1]
        den = lam_re * lam_re + lam_im * lam_im
        nr = pw[1][0] - 1.0
        ni = pw[1][1]
        f_re = (nr * lam_re + ni * lam_im) / den
        f_im = (ni * lam_re - nr * lam_im) / den
        b_re = spread_b(bre_ref[d])
        b_im = spread_b(bim_ref[d])
        bb_re = f_re * b_re - f_im * b_im
        bb_im = f_re * b_im + f_im * b_re
        c_re = spread_c(cre_ref[d])
        c_im = spread_c(cim_ref[d])
        for k in range(L):
            x_re = pw[k][0] * bb_re - pw[k][1] * bb_im
            x_im = pw[k][0] * bb_im + pw[k][1] * bb_re
            s = (L - 1 - k) if d == 0 else k
            for pr in range(S5_PAIRS):
                blk = slice(s * PAIR_BLK, (s + 1) * PAIR_BLK)
                e_ref[0, pr, blk, 2 * d * LANES:(2 * d + 1) * LANES] = (
                    x_re[pair_rows(pr), pair_lanes(pr)].astype(BF16))
                e_ref[0, pr, blk, (2 * d + 1) * LANES:(2 * d + 2) * LANES] = (
                    x_im[pair_rows(pr), pair_lanes(pr)].astype(BF16))
        cw = []
        for k in range(L + 1):
            w_re = c_re * pw[k][0] - c_im * pw[k][1]
            w_im = c_re * pw[k][1] + c_im * pw[k][0]
            cw.append((w_re, -w_im))
        for t in range(L):
            k = (t + 1) if d == 0 else (L - t)
            for pr in range(S5_PAIRS):
                blk = slice(t * PAIR_BLK, (t + 1) * PAIR_BLK)
                ft_ref[0, pr, blk, 2 * d * LANES:(2 * d + 1) * LANES] = (
                    cw[k][0][pair_rows(pr), pair_lanes(pr)].astype(BF16))
                ft_ref[0, pr, blk, (2 * d + 1) * LANES:(2 * d + 2) * LANES] = (
                    cw[k][1][pair_rows(pr), pair_lanes(pr)].astype(BF16))
        lags = range(L) if d == 0 else range(L - 1, -1, -1)
        c_stack = jnp.concatenate(
            [jnp.concatenate([cw[k][0][pair_rows(pr)], cw[k][1][pair_rows(pr)]], axis=-1)
             for pr in range(S5_PAIRS) for k in lags], axis=0)
        lag_strip.append(_dot_split(jnp.concatenate([bb_re, bb_im], axis=-1),
                                    _split_bf16(c_stack), nt_dims))

    strip_w = 2 * L * PAIR_BLK
    mid = (L - 1) * PAIR_BLK
    widen = lambda a: jnp.concatenate([a, jnp.zeros((a.shape[0], strip_w - a.shape[1]), F32)], axis=-1)
    row = lax.broadcasted_iota(jnp.int32, (PAIR_BLK, strip_w), 0)
    lane = lax.broadcasted_iota(jnp.int32, (PAIR_BLK, strip_w), 1)
    d_row = widen(dsk_ref[0])
    for pr in range(S5_PAIRS):
        lanes = slice(pr * L * PAIR_BLK, (pr + 1) * L * PAIR_BLK)
        strip = (widen(lag_strip[1][pair_rows(pr), lanes])
                 + pltpu.roll(widen(lag_strip[0][pair_rows(pr), lanes]), mid, axis=1)
                 + jnp.where(lane == row + mid,
                             pltpu.roll(d_row, mid - pr * PAIR_BLK, axis=1), 0.0))
        for s in range(L):
            shift = (strip_w - (L - 1 - s) * PAIR_BLK) % strip_w
            blk = pltpu.roll(strip, shift, axis=1) if shift else strip
            t_ref[0, pr, s * PAIR_BLK:(s + 1) * PAIR_BLK, :] = blk[:, :L * PAIR_BLK].astype(BF16)


def _s5_prep(lam_re, lam_im, log_dt, b_re, b_im, c_re, c_im, d_skip):
    L = S5_CHUNK
    n_st = S5_G * S5_P
    pair_w = L * PAIR_BLK
    rows = jnp.stack([lam_re.reshape(2, 1, n_st), lam_im.reshape(2, 1, n_st),
                      jnp.repeat(log_dt, S5_P, axis=1).reshape(2, 1, n_st)], axis=1)
    b_spec = pl.BlockSpec((2, S5_ST, S5_GC), lambda q: (0, q, 0))
    c_spec = pl.BlockSpec((2, LANES, S5_P), lambda q: (0, q, 0))
    return pl.pallas_call(
        _s5_prep_kernel,
        grid=(S5_QT,),
        in_specs=[
            pl.BlockSpec((2, 3, 1, S5_ST), lambda q: (0, 0, 0, q)),
            b_spec, b_spec, c_spec, c_spec,
            pl.BlockSpec((1, 1, LANES), lambda q: (q, 0, 0)),
        ],
        out_specs=[
            pl.BlockSpec((1, S5_PAIRS, pair_w, pair_w), lambda q: (q, 0, 0, 0)),
            pl.BlockSpec((1, S5_PAIRS, pair_w, 4 * LANES), lambda q: (q, 0, 0, 0)),
            pl.BlockSpec((1, S5_PAIRS, pair_w, 4 * LANES), lambda q: (q, 0, 0, 0)),
            pl.BlockSpec((2, 1, S5_ST), lambda q: (q, 0, 0)),
            pl.BlockSpec((2, 1, S5_ST), lambda q: (q, 0, 0)),
        ],
        out_shape=[
            jax.ShapeDtypeStruct((S5_QT, S5_PAIRS, pair_w, pair_w), BF16),
            jax.ShapeDtypeStruct((S5_QT, S5_PAIRS, pair_w, 4 * LANES), BF16),
            jax.ShapeDtypeStruct((S5_QT, S5_PAIRS, pair_w, 4 * LANES), BF16),
            jax.ShapeDtypeStruct((2 * S5_QT, 1, S5_ST), F32),
            jax.ShapeDtypeStruct((2 * S5_QT, 1, S5_ST), F32),
        ],
        compiler_params=pltpu.CompilerParams(vmem_limit_bytes=VMEM_LIMIT),
        name="s5_prep",
    )(rows, b_re.reshape(2, n_st, S5_GC), b_im.reshape(2, n_st, S5_GC),
      c_re.reshape(2, S5_W, S5_P), c_im.reshape(2, S5_W, S5_P), d_skip.reshape(S5_QT, 1, LANES))


N_SLAB = S5_ST // LANES


def _cmul(ar, ai, br, bi):
    return ar * br - ai * bi, ar * bi + ai * br


def _cpow(ar, ai, n):
    res = None
    while n:
        if n & 1:
            res = (ar, ai) if res is None else _cmul(res[0], res[1], ar, ai)
        n >>= 1
        if n:
            ar, ai = _cmul(ar, ai, ar, ai)
    return res


def _scan_boundary_states(loc_ref, out_ref, ar_ref, ai_ref, backward):
    a_row = [(ar_ref[0][:, k * LANES:(k + 1) * LANES], ai_ref[0][:, k * LANES:(k + 1) * LANES])
             for k in range(N_SLAB)]
    a_tile = [(jnp.broadcast_to(r, (N_SEG, LANES)), jnp.broadcast_to(i, (N_SEG, LANES)))
              for r, i in a_row]

    def sweep(lo, hi, carry, store):
        def step(i, carry):
            off = (SEG_CHUNKS - 1 - i) if backward else i
            rows = pl.ds(off, N_SEG, stride=SEG_CHUNKS)
            new = []
            for k, (cr, ci) in enumerate(carry):
                xr = loc_ref[k, rows, :]
                xi = loc_ref[N_SLAB + k, rows, :]
                if store:
                    out_ref[k, rows, :] = cr
                    out_ref[N_SLAB + k, rows, :] = ci
                nr, ni = _cmul(a_tile[k][0], a_tile[k][1], cr, ci)
                new.append((nr + xr, ni + xi))
            return tuple(new)
        return lax.fori_loop(lo, hi, step, carry)

    def full(carry, store):
        if not backward:
            return sweep(0, SEG_CHUNKS, carry, store)
        carry = sweep(0, BWD_RESET_STEP, carry, store)
        first = lax.broadcasted_iota(jnp.int32, (N_SEG, LANES), 0) == 0
        carry = tuple((jnp.where(first, 0.0, cr), jnp.where(first, 0.0, ci)) for cr, ci in carry)
        return sweep(BWD_RESET_STEP, SEG_CHUNKS, carry, store)

    zero_tile = jnp.zeros((N_SEG, LANES), F32)
    ends = full(tuple((zero_tile, zero_tile) for _ in range(N_SLAB)), False)

    starts = []
    for k, (er, ei) in enumerate(ends):
        pr, pi = _cpow(a_row[k][0], a_row[k][1], SEG_CHUNKS)
        if backward:
            order = range(N_SEG - 1, -1, -1)
            g = (er[0:1], ei[0:1])
        else:
            order = range(N_SEG)
            g = (jnp.zeros((1, LANES), F32), jnp.zeros((1, LANES), F32))
        rows_r = [None] * N_SEG
        rows_i = [None] * N_SEG
        for j in order:
            rows_r[j], rows_i[j] = g
            nr, ni = _cmul(pr, pi, g[0], g[1])
            g = (nr + er[j:j + 1], ni + ei[j:j + 1])
        starts.append((jnp.concatenate(rows_r, axis=0), jnp.concatenate(rows_i, axis=0)))
    full(tuple(starts), True)


def _s5_out_kernel(vl_ref, vc_ref, ar_ref, ai_ref, e_ref, t_ref, ft_ref, yl_ref, yc_ref,
                   acc_ref, st_ref):
    L = S5_CHUNK
    d = pl.program_id(2)
    rows = [jnp.concatenate([vc_ref[pr], vl_ref[pr]], axis=0) for pr in range(S5_PAIRS)]
    for pr in range(S5_PAIRS):
        st = jnp.dot(rows[pr], e_ref[0, pr], preferred_element_type=F32)
        st_ref[pr] = st[:, :LANES]
        st_ref[N_SLAB + pr] = st[:, LANES:]

    def from_states(pr):
        st = jnp.concatenate([st_ref[pr], st_ref[N_SLAB + pr]], axis=-1).astype(BF16)
        return lax.dot_general(st, ft_ref[0, pr], (((1,), (1,)), ((), ())),
                               preferred_element_type=F32)

    @pl.when(d == 0)
    def _():
        _scan_boundary_states(st_ref, st_ref, ar_ref, ai_ref, False)
        for pr in range(S5_PAIRS):
            acc_ref[pr] = from_states(pr) + jnp.dot(rows[pr], t_ref[0, pr],
                                                    preferred_element_type=F32)

    @pl.when(d == 1)
    def _():
        _scan_boundary_states(st_ref, st_ref, ar_ref, ai_ref, True)
        ys = [acc_ref[pr] + from_states(pr) for pr in range(S5_PAIRS)]
        tiles = [tile for k in range(L // S5_PAIRS) for tile in _block_transpose4(
            [y[:, k * LANES:(k + 1) * LANES] for y in ys])]
        for t, tile in enumerate(tiles):
            yc_ref[pl.ds(t, CTX_CHUNKS, stride=L), :] = tile[:CTX_CHUNKS]
            yl_ref[pl.ds(t, LAT_CHUNKS, stride=L), :] = tile[CTX_CHUNKS:]


def _s5_out(vp, al_re, al_im, e_mat, t_mat, ft_mat):
    pair_w = S5_CHUNK * PAIR_BLK
    ctx_blk0 = (R_LAT // S5_CHUNK) // CTX_CHUNKS
    return pl.pallas_call(
        _s5_out_kernel,
        grid=(S5_QT, B, 2),
        in_specs=[
            pl.BlockSpec((S5_PAIRS, LAT_CHUNKS, pair_w), lambda q, b, d: (q, b, 0)),
            pl.BlockSpec((S5_PAIRS, CTX_CHUNKS, pair_w), lambda q, b, d: (q, ctx_blk0 + b, 0)),
            pl.BlockSpec((1, 1, S5_ST), lambda q, b, d: (2 * q + d, 0, 0)),
            pl.BlockSpec((1, 1, S5_ST), lambda q, b, d: (2 * q + d, 0, 0)),
            pl.BlockSpec((1, S5_PAIRS, pair_w, 2 * LANES), lambda q, b, d: (q, 0, 0, d)),
            pl.BlockSpec((1, S5_PAIRS, pair_w, pair_w), lambda q, b, d: (q, 0, 0, 0)),
            pl.BlockSpec((1, S5_PAIRS, pair_w, 2 * LANES), lambda q, b, d: (q, 0, 0, d)),
        ],
        out_specs=[
            pl.BlockSpec((N_LAT, LANES), lambda q, b, d: (b, q)),
            pl.BlockSpec((N_CTX, LANES), lambda q, b, d: (b, q)),
        ],
        out_shape=[
            jax.ShapeDtypeStruct((R_LAT, S5_W), F32),
            jax.ShapeDtypeStruct((R_CTX, S5_W), F32),
        ],
        scratch_shapes=[pltpu.VMEM((S5_PAIRS, BATCH_CHUNKS, pair_w), F32),
                        pltpu.VMEM((2 * N_SLAB, BATCH_CHUNKS, LANES), F32)],
        compiler_params=pltpu.CompilerParams(vmem_limit_bytes=VMEM_LIMIT),
        name="s5_out",
    )(vp, vp, al_re, al_im, e_mat, t_mat, ft_mat)


def _tail0_stages(z, ys, rest_ref, gate, gluw_ref, glub_ref, sg_g_ref, sg_b_ref, sgw_ref,
                  sgbias_ref, wout_ref, lng_ref, lnb_ref):
    ga = rest_ref[:, 0:S5_W].astype(F32)
    u = rest_ref[:, S5_W:2 * S5_W].astype(F32)
    v = rest_ref[:, 2 * S5_W:3 * S5_W].astype(F32)
    gb = rest_ref[:, 3 * S5_W:4 * S5_W].astype(F32)

    ya = _gelu_tanh(ys)
    glu = jnp.dot(ya.astype(BF16), gluw_ref[...], preferred_element_type=F32) + glub_ref[...]
    yield None
    ya = ya * _sigmoid(glu) * _silu(ga)
    yield None

    vn = _layer_norm(v, sg_g_ref[...], sg_b_ref[...]).astype(BF16)
    lane = lax.broadcasted_iota(jnp.int32, (SG_CHUNK, LANES), 1)
    first_head = lane < (SG_W // SG_HEADS)
    chunks = []
    for c in range(TN // SG_CHUNK):
        tiles = []
        for j in range(SG_W // LANES):
            vt = vn[c * SG_CHUNK:(c + 1) * SG_CHUNK, j * LANES:(j + 1) * LANES]
            r0 = jnp.dot(sgw_ref[2 * j], vt, preferred_element_type=F32)
            r1 = jnp.dot(sgw_ref[2 * j + 1], vt, preferred_element_type=F32)
            tiles.append(jnp.where(first_head, r0, r1))
        chunks.append(jnp.concatenate(tiles, axis=-1) + sgbias_ref[...])
    s = jnp.concatenate(chunks, axis=0)
    yb = u * s * _silu(gb)
    yield None

    mix = jnp.concatenate([ya, yb], axis=-1).astype(BF16)
    y = jnp.dot(mix, wout_ref[...], preferred_element_type=F32) * gate
    yield None
    yield _layer_norm(ALPHA * z + y, lng_ref[...], lnb_ref[...])


def _rope_tables():
    nf = HD // 4
    n_rows = N_LAT // GRID_W
    lane = jnp.arange(LANES)
    inv = ROPE_BASE ** (-(lane % nf).astype(F32) / nf)
    by_row = ((lane % HD) // (HD // 2) == 0)[None, :]
    sign = jnp.where((lane % (HD // 2)) < nf, -1.0, 1.0)[None, :]
    row_ang = jnp.arange(n_rows, dtype=F32)[:, None] * inv[None, :]
    col_ang = jnp.arange(GRID_W, dtype=F32)[:, None] * inv[None, :]
    zero = jnp.zeros((), F32)
    cos_t = (jnp.where(by_row, jnp.cos(row_ang), zero)[:, None, :]
             + jnp.where(by_row, zero, jnp.cos(col_ang))[None, :, :]).reshape(N_LAT, LANES)
    sin_t = (jnp.where(by_row, sign * jnp.sin(row_ang), zero)[:, None, :]
             + jnp.where(by_row, zero, sign * jnp.sin(col_ang))[None, :, :]).reshape(N_LAT, LANES)
    return cos_t, sin_t


def _rope_tile(x, cos, sin, first_half):
    nf = HD // 4
    partner = jnp.where(first_half, pltpu.roll(x, LANES - nf, axis=1), pltpu.roll(x, nf, axis=1))
    return x * cos + partner * sin


def _mid_kernel(x_ref, ctx_ref, mod0_ref, mod1_ref, ysl_ref, ysc_ref, rest_ref, gluw_ref, glub_ref,
                sg_g_ref, sg_b_ref, sgw_ref, sgbias_ref, wout_ref, lng_ref, lnb_ref, w_ref,
                wvt_ref, cos_ref, sin_ref, z1_ref, q_ref, k_ref, vt_ref, g_ref, zprev_ref):
    i = pl.program_id(0)
    dq = N_HEADS * HD
    dkv = N_KV * HD

    @pl.when(i == 0)
    def _():
        zprev_ref[...] = jnp.zeros((TN, D), F32)

    j = jnp.maximum(i - 1, 0)
    m1 = mod1_ref[pl.ds(_block_mod_row(j), 1), :]
    h = (zprev_ref[...] * (1.0 + m1[:, D:2 * D]) + m1[:, :D]).astype(BF16)
    cos = jnp.where(j < N_LAT_BLK, cos_ref[...], 1.0)
    sin = jnp.where(j < N_LAT_BLK, sin_ref[...], 0.0)
    lane = lax.broadcasted_iota(jnp.int32, (TN, LANES), 1)
    first_half = (lane % (HD // 2)) < (HD // 4)
    scale = HD ** -0.5 * LOG2E

    def project(c0, c1):
        return jnp.dot(h, w_ref[:, c0:c1], preferred_element_type=F32)

    def roped_tiles(p, mult):
        for c in range(p.shape[1] // LANES):
            r = _rope_tile(p[:, c * LANES:(c + 1) * LANES], cos, sin, first_half)
            yield c, (r * mult if mult != 1.0 else r).astype(BF16)

    t = jnp.minimum(i, N_BLK - 1)
    z = jnp.where(t < N_LAT_BLK, x_ref[...], ctx_ref[...])
    ys = jnp.where(t < N_LAT_BLK, ysl_ref[...], ysc_ref[...])
    gate = mod0_ref[pl.ds(_block_mod_row(t), 1), 2 * D:3 * D]
    tail = _tail0_stages(z, ys, rest_ref, gate, gluw_ref, glub_ref, sg_g_ref, sg_b_ref, sgw_ref,
                         sgbias_ref, wout_ref, lng_ref, lnb_ref)

    half_q = dq // 2
    p_q0 = project(0, half_q)
    next(tail)
    p_q1 = project(half_q, dq)
    next(tail)
    p_k = project(dq, dq + dkv)
    vt_ref[...] = lax.dot_general(wvt_ref[...], h, (((1,), (1,)), ((), ())),
                                  preferred_element_type=F32).astype(BF16)
    next(tail)
    next(tail)
    for c0, p in ((0, p_q0), (half_q, p_q1)):
        for c, r in roped_tiles(p, scale):
            q_ref[:, c0 + c * LANES:c0 + (c + 1) * LANES] = r
    for c, r in roped_tiles(p_k, 1.0):
        k_ref[:, c * LANES:(c + 1) * LANES] = r
    p_g = project(dq + 2 * dkv, ODD_IN)
    z1 = next(tail)
    z1_ref[...] = z1
    zprev_ref[...] = z1
    g_ref[...] = p_g.astype(BF16)


def _mid(x2, ctx2, mod0, mod1, ys_lat, ys_ctx, rest, glu_w, glu_b, sg_g, sg_b, sg_w, sg_bias,
         w_out, ln_g, ln_b, w_in1, cos_t, sin_t):
    dq = N_HEADS * HD
    dkv = N_KV * HD
    w_vt = jnp.transpose(w_in1[:, dq + dkv:dq + 2 * dkv]).astype(BF16)
    once = dict(pipeline_mode=pl.Buffered(1))
    row = lambda n: pl.BlockSpec((1, n), lambda i: (0, 0))
    lat = lambda i: (jnp.minimum(i, N_LAT_BLK - 1), 0)
    tail_blk = lambda i: (jnp.minimum(i, N_BLK - 1), 0)
    proj_blk = lambda i: (jnp.maximum(i - 1, 0), 0)
    tab = lambda i: (jnp.maximum(i - 1, 0) % LAT_BLK_PER_BATCH, 0)
    return pl.pallas_call(
        _mid_kernel,
        grid=(N_BLK + 1,),
        in_specs=[
            pl.BlockSpec((TN, D), lat),
            pl.BlockSpec((R_CTX, D), lambda i: (0, 0), **once),
            pl.BlockSpec((8, 3 * D), lambda i: (0, 0)),
            pl.BlockSpec((8, 3 * D), lambda i: (0, 0)),
            pl.BlockSpec((TN, S5_W), lat),
            pl.BlockSpec((R_CTX, S5_W), lambda i: (0, 0), **once),
            pl.BlockSpec((TN, EVEN_IN - S5_W), tail_blk),
            pl.BlockSpec((S5_W, S5_W), lambda i: (0, 0), **once),
            row(S5_W), row(SG_W), row(SG_W),
            pl.BlockSpec((SG_HEADS, SG_CHUNK, SG_CHUNK), lambda i: (0, 0, 0)),
            pl.BlockSpec((SG_CHUNK, SG_W), lambda i: (0, 0)),
            pl.BlockSpec((S5_W + SG_W, D), lambda i: (0, 0), **once),
            row(D), row(D),
            pl.BlockSpec((D, ODD_IN), lambda i: (0, 0), **once),
            pl.BlockSpec((dkv, D), lambda i: (0, 0), **once),
            pl.BlockSpec((TN, LANES), tab),
            pl.BlockSpec((TN, LANES), tab),
        ],
        out_specs=[
            pl.BlockSpec((TN, D), tail_blk),
            pl.BlockSpec((TN, dq), proj_blk),
            pl.BlockSpec((TN, dkv), proj_blk),
            pl.BlockSpec((dkv, TN), lambda i: (0, jnp.maximum(i - 1, 0))),
            pl.BlockSpec((TN, dq), proj_blk),
        ],
        out_shape=[
            jax.ShapeDtypeStruct((R_ALL, D), F32),
            jax.ShapeDtypeStruct((R_ALL, dq), BF16),
            jax.ShapeDtypeStruct((R_ALL, dkv), BF16),
            jax.ShapeDtypeStruct((dkv, R_ALL), BF16),
            jax.ShapeDtypeStruct((R_ALL, dq), BF16),
        ],
        scratch_shapes=[pltpu.VMEM((TN, D), F32)],
        compiler_params=pltpu.CompilerParams(vmem_limit_bytes=VMEM_LIMIT),
        name="mid",
    )(x2, ctx2, mod0, mod1, ys_lat, ys_ctx, rest, glu_w, glu_b, sg_g, sg_b, sg_w, sg_bias, w_out,
      ln_g, ln_b, w_in1.astype(BF16), w_vt, cos_t, sin_t)


N_QBLK = N_LAT // ATT_BLK
GRP = N_HEADS // N_KV
ATT_SUB = 4
ATT_ROWS = ATT_SUB * ATT_BLK
N_QSTEP = N_QBLK // ATT_SUB
N_ATT_STEPS = B * N_QSTEP
assert ATT_ROWS == TN
ONES_ROWS = 16


def _attn_kernel(sink_ref, q_ref, kp_ref, kc_ref, kn_ref, kx_ref, vp_ref, vc_ref, vn_ref, vx_ref,
                 g_ref, z_ref, mod_ref, wout_ref, lng_ref, lnb_ref, out_ref, o_scr):
    step = pl.program_id(0)

    @pl.when(step == 0)
    def _():
        o_scr[...] = jnp.zeros((ATT_ROWS, N_HEADS * HD), BF16)

    t = jnp.maximum(step - 1, 0)
    gate = mod_ref[pl.ds(_block_mod_row(t), 1), 2 * D:3 * D]
    mix = (o_scr[...].astype(F32) * _silu(g_ref[...].astype(F32))).astype(BF16)
    y = jnp.dot(mix, wout_ref[...], preferred_element_type=F32) * gate
    out_ref[...] = _layer_norm(ALPHA * z_ref[...] + y, lng_ref[...], lnb_ref[...])

    i = jnp.minimum(step, N_ATT_STEPS - 1) % N_QSTEP
    n_win = 3 * ATT_BLK
    n_keys = n_win + N_CTX
    nq = GRP * ATT_BLK
    k_win = jnp.concatenate([kp_ref[...], kc_ref[...], kn_ref[...]], axis=0)
    vt_win = jnp.concatenate([vp_ref[...], vc_ref[...], vn_ref[...]], axis=1)
    k_ctx = kx_ref[...]
    vt_ctx = vx_ref[...]

    kpos = lax.broadcasted_iota(jnp.int32, (ATT_BLK, ATT_BLK), 0)
    qpos = lax.broadcasted_iota(jnp.int32, (ATT_BLK, ATT_BLK), 1)
    tile_q = lambda a: jnp.concatenate([a] * GRP, axis=1)
    qgrp = lax.broadcasted_iota(jnp.int32, (1, nq), 1) // ATT_BLK
    ones = jnp.ones((ONES_ROWS, n_keys), BF16)
    q = q_ref[...]

    units = [(u, h) for u in range(ATT_SUB) for h in range(N_KV)]
    k_all, vt_all, bias = [], [], []
    for u in range(ATT_SUB):
        k_all.append(jnp.concatenate([k_win[u * ATT_BLK:u * ATT_BLK + n_win], k_ctx], axis=0))
        vt_all.append(jnp.concatenate([vt_win[:, u * ATT_BLK:u * ATT_BLK + n_win], vt_ctx], axis=1))
        has_prev = (i > 0) if u == 0 else True
        has_next = (i < N_QSTEP - 1) if u == ATT_SUB - 1 else True
        bias.append((tile_q(jnp.where((kpos >= qpos) & has_prev, 0.0, NEG_INF)),
                     tile_q(jnp.where((kpos <= qpos) & has_next, 0.0, NEG_INF))))
    scores = []
    for u, h in units:
        kh = k_all[u][:, h * HD:(h + 1) * HD]
        qh = jnp.concatenate(
            [q[u * ATT_BLK:(u + 1) * ATT_BLK, (h * GRP + g) * HD:(h * GRP + g + 1) * HD]
             for g in range(GRP)], axis=0)
        scores.append(lax.dot_general(kh, qh, (((1,), (1,)), ((), ())),
                                      preferred_element_type=F32))
    probs = []
    for (u, h), s in zip(units, scores):
        s = jnp.concatenate([s[:ATT_BLK] + bias[u][0], s[ATT_BLK:2 * ATT_BLK],
                             s[2 * ATT_BLK:n_win] + bias[u][1], s[n_win:]], axis=0)
        sink = jnp.zeros((1, nq), F32)
        for g in range(GRP):
            sink = jnp.where(qgrp == g, sink_ref[h * GRP + g] * LOG2E, sink)
        m = jnp.maximum(jnp.max(s, axis=0, keepdims=True), sink)
        probs.append((jnp.exp2(s - m).astype(BF16), jnp.exp2(sink - m)))
    outs = [[] for _ in range(ATT_SUB)]
    for (u, h), (p, p_sink) in zip(units, probs):
        vt1 = jnp.concatenate([vt_all[u][h * HD:(h + 1) * HD], ones], axis=0)
        ov = jnp.dot(vt1, p, preferred_element_type=F32)
        o_t = ov[:HD] / (ov[HD:HD + 1] + p_sink)
        outs[u].extend(o_t[:, g * ATT_BLK:(g + 1) * ATT_BLK] for g in range(GRP))
    for u in range(ATT_SUB):
        o_scr[u * ATT_BLK:(u + 1) * ATT_BLK, :] = jnp.transpose(
            jnp.concatenate(outs[u], axis=0)).astype(BF16)


def _attention_tail(sink, q, k, vt, g, z1, mod, w_out, ln_g, ln_b):
    dq = N_HEADS * HD
    dkv = N_KV * HD
    ctx_blk0 = R_LAT // N_CTX
    cur = lambda s: jnp.minimum(s, N_ATT_STEPS - 1)
    batch = lambda s: cur(s) // N_QSTEP
    qstep = lambda s: cur(s) % N_QSTEP
    prev_blk = lambda s: batch(s) * N_QBLK + jnp.maximum(ATT_SUB * qstep(s) - 1, 0)
    next_blk = lambda s: batch(s) * N_QBLK + jnp.minimum(ATT_SUB * (qstep(s) + 1), N_QBLK - 1)
    tail = lambda s: jnp.maximum(s - 1, 0)
    once = dict(pipeline_mode=pl.Buffered(1))
    row = lambda n: pl.BlockSpec((1, n), lambda s, sk: (0, 0))
    return pl.pallas_call(
        _attn_kernel,
        grid_spec=pltpu.PrefetchScalarGridSpec(
            num_scalar_prefetch=1,
            grid=(N_ATT_STEPS + 1,),
            in_specs=[
                pl.BlockSpec((ATT_ROWS, dq), lambda s, sk: (cur(s), 0)),
                pl.BlockSpec((ATT_BLK, dkv), lambda s, sk: (prev_blk(s), 0)),
                pl.BlockSpec((ATT_ROWS, dkv), lambda s, sk: (cur(s), 0)),
                pl.BlockSpec((ATT_BLK, dkv), lambda s, sk: (next_blk(s), 0)),
                pl.BlockSpec((N_CTX, dkv), lambda s, sk: (ctx_blk0 + batch(s), 0)),
                pl.BlockSpec((dkv, ATT_BLK), lambda s, sk: (0, prev_blk(s))),
                pl.BlockSpec((dkv, ATT_ROWS), lambda s, sk: (0, cur(s))),
                pl.BlockSpec((dkv, ATT_BLK), lambda s, sk: (0, next_blk(s))),
                pl.BlockSpec((dkv, N_CTX), lambda s, sk: (0, ctx_blk0 + batch(s))),
                pl.BlockSpec((ATT_ROWS, dq), lambda s, sk: (tail(s), 0)),
                pl.BlockSpec((ATT_ROWS, D), lambda s, sk: (tail(s), 0)),
                pl.BlockSpec((8, 3 * D), lambda s, sk: (0, 0)),
                pl.BlockSpec((dq, D), lambda s, sk: (0, 0), **once),
                row(D), row(D),
            ],
            out_specs=pl.BlockSpec((ATT_ROWS, D), lambda s, sk: (tail(s), 0)),
            scratch_shapes=[pltpu.VMEM((ATT_ROWS, dq), BF16)],
        ),
        out_shape=jax.ShapeDtypeStruct((R_LAT, D), F32),
        compiler_params=pltpu.CompilerParams(vmem_limit_bytes=VMEM_LIMIT),
        name="attention",
    )(sink, q, k, k, k, k, vt, vt, vt, vt, g, z1, mod, w_out, ln_g, ln_b)


def kernel(x, c, ctx, c_ctx, mod_w, mod_b, ln_g, ln_b, e_w_in, e_w_out, s5_lam_re, s5_lam_im,
           s5_log_dt, s5_b_re, s5_b_im, s5_c_re, s5_c_im, s5_d, glu_w, glu_b, sg_ln_g, sg_ln_b,
           sg_w, sg_b, o_w_in, o_w_out, o_sink):
    x2 = x.reshape(R_LAT, D)
    ctx2 = ctx.reshape(R_CTX, D)
    mod = _modulation(c, c_ctx, mod_w, mod_b)

    vp, rest = _in0(x2, ctx2, mod[0], e_w_in[0].astype(BF16))
    t_mat, e_mat, ft_mat, al_re, al_im = _s5_prep(
        s5_lam_re[0], s5_lam_im[0], s5_log_dt[0], s5_b_re[0], s5_b_im[0], s5_c_re[0], s5_c_im[0],
        s5_d[0])
    ys_lat, ys_ctx = _s5_out(vp, al_re, al_im, e_mat, t_mat, ft_mat)
    sg_bias = jnp.repeat(jnp.transpose(sg_b[0]), SG_W // SG_HEADS, axis=1)
    cos_t, sin_t = _rope_tables()
    z1, q, k, vt, g = _mid(
        x2, ctx2, mod[0], mod[1], ys_lat, ys_ctx, rest, glu_w[0].astype(BF16),
        glu_b[0].reshape(1, S5_W), sg_ln_g[0].reshape(1, SG_W), sg_ln_b[0].reshape(1, SG_W),
        sg_w[0].astype(BF16), sg_bias, e_w_out[0].astype(BF16), ln_g[0].reshape(1, D),
        ln_b[0].reshape(1, D), o_w_in[0], cos_t, sin_t)

    out = _attention_tail(o_sink[0], q, k, vt, g, z1, mod[1], o_w_out[0].astype(BF16),
                          ln_g[1].reshape(1, D), ln_b[1].reshape(1, D))
    return out.reshape(B, N_LAT, D)
```

```python
import functools
import math

import jax
import jax.numpy as jnp
from jax import lax
from jax.experimental import pallas as pl
from jax.experimental.pallas import tpu as pltpu

F32 = jnp.float32
BF16 = jnp.bfloat16

D = 1024
B = 2
N_LAT = 8192
N_CTX = 256
DEPTH = 2
GRID_W = 64
S5_W = 512
S5_GC = 16
S5_G = 32
S5_P = 64
SG_W = 512
SG_HEADS = 8
SG_CHUNK = 128
N_HEADS = 16
N_KV = 4
HD = 64
WINDOW = 128
ATT_BLK = 128
ROPE_BASE = 10000.0
NEG_INF = -1e30
LN_EPS = 1e-5
ALPHA = (2 * DEPTH) ** 0.25
LOG2E = math.log2(math.e)
EVEN_IN = 2 * S5_W + 3 * SG_W
ODD_IN = 2 * N_HEADS * HD + 2 * N_KV * HD

LANES = 128
VMEM_LIMIT = 56 * 1024 * 1024

R_LAT = B * N_LAT
R_CTX = B * N_CTX
R_ALL = R_LAT + R_CTX
TN = 512
N_LAT_BLK = R_LAT // TN
N_BLK = R_ALL // TN
LAT_BLK_PER_BATCH = N_LAT // TN
CTX_MOD_ROW = B

S5_CHUNK = 8
S5_QT = S5_W // LANES
S5_GPT = LANES // S5_GC
S5_ST = S5_GPT * S5_P
S5_PAIRS = S5_GPT // 2
PAIR_BLK = 2 * S5_GC
LAT_CHUNKS = N_LAT // S5_CHUNK
CTX_CHUNKS = N_CTX // S5_CHUNK
BATCH_CHUNKS = CTX_CHUNKS + LAT_CHUNKS
N_SEG = 8
SEG_CHUNKS = BATCH_CHUNKS // N_SEG
BWD_RESET_STEP = LAT_CHUNKS - (N_SEG - 1) * SEG_CHUNKS


def _sigmoid(x):
    return 1.0 / (1.0 + jnp.exp(-x))


def _silu(x):
    return x * _sigmoid(x)


def _gelu_tanh(x):
    return 0.5 * x * (1.0 + jnp.tanh(math.sqrt(2.0 / math.pi) * (x + 0.044715 * (x * x * x))))


def _layer_norm(x, g, b):
    mu = jnp.mean(x, axis=-1, keepdims=True)
    xc = x - mu
    var = jnp.mean(xc * xc, axis=-1, keepdims=True)
    return xc * lax.rsqrt(var + LN_EPS) * g + b


def _split_bf16(x):
    hi = x.astype(BF16)
    return hi, (x - hi.astype(F32)).astype(BF16)


def _dot_split(x, y_split, dims):
    x_hi, x_lo = _split_bf16(x)
    y_hi, y_lo = y_split
    dot = lambda a, b: lax.dot_general(a, b, dims, preferred_element_type=F32)
    return dot(x_hi, y_hi) + (dot(x_hi, y_lo) + dot(x_lo, y_hi))


def _block_mod_row(i):
    return jnp.where(i < N_LAT_BLK, i // LAT_BLK_PER_BATCH, CTX_MOD_ROW)


def _mod_kernel(cv_ref, w_ref, b_ref, o_ref):
    s = _silu(cv_ref[...])
    o_ref[0] = _dot_split(s, _split_bf16(w_ref[0]), (((1,), (0,)), ((), ()))) + b_ref[0]


def _modulation(c, c_ctx, mod_w, mod_b):
    cv = jnp.concatenate([c, c_ctx[None], jnp.zeros((8 - B - 1, D), F32)], axis=0)
    tn = 1024
    return pl.pallas_call(
        _mod_kernel,
        grid=(DEPTH, 3 * D // tn),
        in_specs=[
            pl.BlockSpec((8, D), lambda l, j: (0, 0)),
            pl.BlockSpec((1, D, tn), lambda l, j: (l, 0, j)),
            pl.BlockSpec((1, 1, tn), lambda l, j: (l, 0, j)),
        ],
        out_specs=pl.BlockSpec((1, 8, tn), lambda l, j: (l, 0, j)),
        out_shape=jax.ShapeDtypeStruct((DEPTH, 8, 3 * D), F32),
        compiler_params=pltpu.CompilerParams(vmem_limit_bytes=VMEM_LIMIT),
        name="modulation",
    )(cv, mod_w, mod_b.reshape(DEPTH, 1, 3 * D))


def _block_transpose4(tiles):
    tiles = list(tiles)
    blk = lax.broadcasted_iota(jnp.int32, tiles[0].shape, 1) // PAIR_BLK
    for dist in (2, 1):
        keep = (blk & dist) == 0
        for i in range(len(tiles)):
            if i & dist:
                continue
            lo, hi = tiles[i], tiles[i + dist]
            tiles[i] = jnp.where(keep, lo, pltpu.roll(hi, dist * PAIR_BLK, axis=1))
            tiles[i + dist] = jnp.where(keep, pltpu.roll(lo, LANES - dist * PAIR_BLK, axis=1), hi)
    return tiles


def _in0_kernel(x_ref, ctx_ref, mod_ref, w_ref, vp_ref, rest_ref, xa_scr):
    L = S5_CHUNK
    i = pl.program_id(0)
    z = jnp.where(i < N_LAT_BLK, x_ref[...], ctx_ref[...])
    m = mod_ref[pl.ds(_block_mod_row(i), 1), :]
    h = z * (1.0 + m[:, D:2 * D]) + m[:, :D]
    p = jnp.dot(h.astype(BF16), w_ref[...], preferred_element_type=F32)
    rest_ref[...] = p[:, S5_W:].astype(BF16)
    for q in range(S5_QT):
        xa_scr[q] = p[:, q * LANES:(q + 1) * LANES]
        tiles = [xa_scr[q, pl.ds(t, TN // L, stride=L), :] for t in range(L)]
        halves = [_block_transpose4(tiles[k * S5_PAIRS:(k + 1) * S5_PAIRS])
                  for k in range(L // S5_PAIRS)]
        for pr in range(S5_PAIRS):
            vp_ref[q * S5_PAIRS + pr] = jnp.concatenate(
                [half[pr] for half in halves], axis=-1).astype(BF16)


def _in0(x2, ctx2, mod, w_in):
    pair_w = S5_CHUNK * PAIR_BLK
    return pl.pallas_call(
        _in0_kernel,
        grid=(N_BLK,),
        in_specs=[
            pl.BlockSpec((TN, D), lambda i: (jnp.minimum(i, N_LAT_BLK - 1), 0)),
            pl.BlockSpec((R_CTX, D), lambda i: (0, 0)),
            pl.BlockSpec((8, 3 * D), lambda i: (0, 0)),
            pl.BlockSpec((D, EVEN_IN), lambda i: (0, 0)),
        ],
        out_specs=[
            pl.BlockSpec((S5_QT * S5_PAIRS, TN // S5_CHUNK, pair_w), lambda i: (0, i, 0)),
            pl.BlockSpec((TN, EVEN_IN - S5_W), lambda i: (i, 0)),
        ],
        out_shape=[
            jax.ShapeDtypeStruct((S5_QT * S5_PAIRS, R_ALL // S5_CHUNK, pair_w), BF16),
            jax.ShapeDtypeStruct((R_ALL, EVEN_IN - S5_W), BF16),
        ],
        scratch_shapes=[pltpu.VMEM((S5_QT, TN, LANES), F32)],
        compiler_params=pltpu.CompilerParams(vmem_limit_bytes=VMEM_LIMIT),
        name="in0",
    )(x2, ctx2, mod, w_in)


def _s5_prep_kernel(rows_ref, bre_ref, bim_ref, cre_ref, cim_ref, dsk_ref, t_ref, e_ref, ft_ref,
                    alr_ref, ali_ref):
    L = S5_CHUNK
    tile = (LANES, S5_ST)
    same_group = (lax.broadcasted_iota(jnp.int32, tile, 0) // S5_GC
                  == lax.broadcasted_iota(jnp.int32, tile, 1) // S5_P)
    nt_dims = (((1,), (1,)), ((), ()))
    nn_dims = (((1,), (0,)), ((), ()))
    pick_ch = (lax.broadcasted_iota(jnp.int32, (LANES, S5_GC), 0) % S5_GC
               == lax.broadcasted_iota(jnp.int32, (LANES, S5_GC), 1)).astype(BF16)
    rep_st = (lax.broadcasted_iota(jnp.int32, (S5_P, S5_ST), 0)
              == lax.broadcasted_iota(jnp.int32, (S5_P, S5_ST), 1) % S5_P).astype(BF16)

    def spread_b(b):
        hi, lo = _split_bf16(b)
        dot = lambda v: lax.dot_general(pick_ch, v, nt_dims, preferred_element_type=F32)
        return jnp.where(same_group, dot(hi) + dot(lo), 0.0)

    def spread_c(c):
        hi, lo = _split_bf16(c)
        dot = lambda v: lax.dot_general(v, rep_st, nn_dims, preferred_element_type=F32)
        return jnp.where(same_group, dot(hi) + dot(lo), 0.0)

    pair_rows = lambda pr: slice(pr * PAIR_BLK, (pr + 1) * PAIR_BLK)
    pair_lanes = lambda pr: slice(pr * LANES, (pr + 1) * LANES)
    lag_strip = []
    for d in range(2):
        lam_re = rows_ref[d, 0]
        lam_im = rows_ref[d, 1]
        dt = jnp.exp(rows_ref[d, 2])
        pw = []
        for k in range(L + 1):
            mag = jnp.exp(lam_re * dt * float(k))
            ang = lam_im * dt * float(k)
            pw.append((mag * jnp.cos(ang), mag * jnp.sin(ang)))
        alr_ref[d] = pw[L][0]
        ali_ref[d] = pw[L][1]
        den = lam_re * lam_re + lam_im * lam_im
        nr = pw[1][0] - 1.0
        ni = pw[1][1]
        f_re = (nr * lam_re + ni * lam_im) / den
        f_im = (ni * lam_re - nr * lam_im) / den
        b_re = spread_b(bre_ref[d])
        b_im = spread_b(bim_ref[d])
        bb_re = f_re * b_re - f_im * b_im
        bb_im = f_re * b_im + f_im * b_re
        c_re = spread_c(cre_ref[d])
        c_im = spread_c(cim_ref[d])
        for k in range(L):
            x_re = pw[k][0] * bb_re - pw[k][1] * bb_im
            x_im = pw[k][0] * bb_im + pw[k][1] * bb_re
            s = (L - 1 - k) if d == 0 else k
            for pr in range(S5_PAIRS):
                blk = slice(s * PAIR_BLK, (s + 1) * PAIR_BLK)
                e_ref[0, pr, blk, 2 * d * LANES:(2 * d + 1) * LANES] = (
                    x_re[pair_rows(pr), pair_lanes(pr)].astype(BF16))
                e_ref[0, pr, blk, (2 * d + 1) * LANES:(2 * d + 2) * LANES] = (
                    x_im[pair_rows(pr), pair_lanes(pr)].astype(BF16))
        cw = []
        for k in range(L + 1):
            w_re = c_re * pw[k][0] - c_im * pw[k][1]
            w_im = c_re * pw[k][1] + c_im * pw[k][0]
            cw.append((w_re, -w_im))
        for t in range(L):
            k = (t + 1) if d == 0 else (L - t)
            for pr in range(S5_PAIRS):
                blk = slice(t * PAIR_BLK, (t + 1) * PAIR_BLK)
                ft_ref[0, pr, blk, 2 * d * LANES:(2 * d + 1) * LANES] = (
                    cw[k][0][pair_rows(pr), pair_lanes(pr)].astype(BF16))
                ft_ref[0, pr, blk, (2 * d + 1) * LANES:(2 * d + 2) * LANES] = (
                    cw[k][1][pair_rows(pr), pair_lanes(pr)].astype(BF16))
        lags = range(L) if d == 0 else range(L - 1, -1, -1)
        c_stack = jnp.concatenate(
            [jnp.concatenate([cw[k][0][pair_rows(pr)], cw[k][1][pair_rows(pr)]], axis=-1)
             for pr in range(S5_PAIRS) for k in lags], axis=0)
        lag_strip.append(_dot_split(jnp.concatenate([bb_re, bb_im], axis=-1),
                                    _split_bf16(c_stack), nt_dims))

    strip_w = 2 * L * PAIR_BLK
    mid = (L - 1) * PAIR_BLK
    widen = lambda a: jnp.concatenate([a, jnp.zeros((a.shape[0], strip_w - a.shape[1]), F32)], axis=-1)
    row = lax.broadcasted_iota(jnp.int32, (PAIR_BLK, strip_w), 0)
    lane = lax.broadcasted_iota(jnp.int32, (PAIR_BLK, strip_w), 1)
    d_row = widen(dsk_ref[0])
    for pr in range(S5_PAIRS):
        lanes = slice(pr * L * PAIR_BLK, (pr + 1) * L * PAIR_BLK)
        strip = (widen(lag_strip[1][pair_rows(pr), lanes])
                 + pltpu.roll(widen(lag_strip[0][pair_rows(pr), lanes]), mid, axis=1)
                 + jnp.where(lane == row + mid,
                             pltpu.roll(d_row, mid - pr * PAIR_BLK, axis=1), 0.0))
        for s in range(L):
            shift = (strip_w - (L - 1 - s) * PAIR_BLK) % strip_w
            blk = pltpu.roll(strip, shift, axis=1) if shift else strip
            t_ref[0, pr, s * PAIR_BLK:(s + 1) * PAIR_BLK, :] = blk[:, :L * PAIR_BLK].astype(BF16)


def _s5_prep(lam_re, lam_im, log_dt, b_re, b_im, c_re, c_im, d_skip):
    L = S5_CHUNK
    n_st = S5_G * S5_P
    pair_w = L * PAIR_BLK
    rows = jnp.stack([lam_re.reshape(2, 1, n_st), lam_im.reshape(2, 1, n_st),
                      jnp.repeat(log_dt, S5_P, axis=1).reshape(2, 1, n_st)], axis=1)
    b_spec = pl.BlockSpec((2, S5_ST, S5_GC), lambda q: (0, q, 0))
    c_spec = pl.BlockSpec((2, LANES, S5_P), lambda q: (0, q, 0))
    return pl.pallas_call(
        _s5_prep_kernel,
        grid=(S5_QT,),
        in_specs=[
            pl.BlockSpec((2, 3, 1, S5_ST), lambda q: (0, 0, 0, q)),
            b_spec, b_spec, c_spec, c_spec,
            pl.BlockSpec((1, 1, LANES), lambda q: (q, 0, 0)),
        ],
        out_specs=[
            pl.BlockSpec((1, S5_PAIRS, pair_w, pair_w), lambda q: (q, 0, 0, 0)),
            pl.BlockSpec((1, S5_PAIRS, pair_w, 4 * LANES), lambda q: (q, 0, 0, 0)),
            pl.BlockSpec((1, S5_PAIRS, pair_w, 4 * LANES), lambda q: (q, 0, 0, 0)),
            pl.BlockSpec((2, 1, S5_ST), lambda q: (q, 0, 0)),
            pl.BlockSpec((2, 1, S5_ST), lambda q: (q, 0, 0)),
        ],
        out_shape=[
            jax.ShapeDtypeStruct((S5_QT, S5_PAIRS, pair_w, pair_w), BF16),
            jax.ShapeDtypeStruct((S5_QT, S5_PAIRS, pair_w, 4 * LANES), BF16),
            jax.ShapeDtypeStruct((S5_QT, S5_PAIRS, pair_w, 4 * LANES), BF16),
            jax.ShapeDtypeStruct((2 * S5_QT, 1, S5_ST), F32),
            jax.ShapeDtypeStruct((2 * S5_QT, 1, S5_ST), F32),
        ],
        compiler_params=pltpu.CompilerParams(vmem_limit_bytes=VMEM_LIMIT),
        name="s5_prep",
    )(rows, b_re.reshape(2, n_st, S5_GC), b_im.reshape(2, n_st, S5_GC),
      c_re.reshape(2, S5_W, S5_P), c_im.reshape(2, S5_W, S5_P), d_skip.reshape(S5_QT, 1, LANES))


N_SLAB = S5_ST // LANES


def _cmul(ar, ai, br, bi):
    return ar * br - ai * bi, ar * bi + ai * br


def _cpow(ar, ai, n):
    res = None
    while n:
        if n & 1:
            res = (ar, ai) if res is None else _cmul(res[0], res[1], ar, ai)
        n >>= 1
        if n:
            ar, ai = _cmul(ar, ai, ar, ai)
    return res


def _scan_boundary_states(loc_ref, out_ref, ar_ref, ai_ref, backward):
    a_row = [(ar_ref[0][:, k * LANES:(k + 1) * LANES], ai_ref[0][:, k * LANES:(k + 1) * LANES])
             for k in range(N_SLAB)]
    a_tile = [(jnp.broadcast_to(r, (N_SEG, LANES)), jnp.broadcast_to(i, (N_SEG, LANES)))
              for r, i in a_row]

    def sweep(lo, hi, carry, store):
        def step(i, carry):
            off = (SEG_CHUNKS - 1 - i) if backward else i
            rows = pl.ds(off, N_SEG, stride=SEG_CHUNKS)
            new = []
            for k, (cr, ci) in enumerate(carry):
                xr = loc_ref[k, rows, :]
                xi = loc_ref[N_SLAB + k, rows, :]
                if store:
                    out_ref[k, rows, :] = cr
                    out_ref[N_SLAB + k, rows, :] = ci
                nr, ni = _cmul(a_tile[k][0], a_tile[k][1], cr, ci)
                new.append((nr + xr, ni + xi))
            return tuple(new)
        return lax.fori_loop(lo, hi, step, carry)

    def full(carry, store):
        if not backward:
            return sweep(0, SEG_CHUNKS, carry, store)
        carry = sweep(0, BWD_RESET_STEP, carry, store)
        first = lax.broadcasted_iota(jnp.int32, (N_SEG, LANES), 0) == 0
        carry = tuple((jnp.where(first, 0.0, cr), jnp.where(first, 0.0, ci)) for cr, ci in carry)
        return sweep(BWD_RESET_STEP, SEG_CHUNKS, carry, store)

    zero_tile = jnp.zeros((N_SEG, LANES), F32)
    ends = full(tuple((zero_tile, zero_tile) for _ in range(N_SLAB)), False)

    starts = []
    for k, (er, ei) in enumerate(ends):
        pr, pi = _cpow(a_row[k][0], a_row[k][1], SEG_CHUNKS)
        if backward:
            order = range(N_SEG - 1, -1, -1)
            g = (er[0:1], ei[0:1])
        else:
            order = range(N_SEG)
            g = (jnp.zeros((1, LANES), F32), jnp.zeros((1, LANES), F32))
        rows_r = [None] * N_SEG
        rows_i = [None] * N_SEG
        for j in order:
            rows_r[j], rows_i[j] = g
            nr, ni = _cmul(pr, pi, g[0], g[1])
            g = (nr + er[j:j + 1], ni + ei[j:j + 1])
        starts.append((jnp.concatenate(rows_r, axis=0), jnp.concatenate(rows_i, axis=0)))
    full(tuple(starts), True)


def _s5_out_kernel(vl_ref, vc_ref, ar_ref, ai_ref, e_ref, t_ref, ft_ref, yl_ref, yc_ref,
                   acc_ref, st_ref):
    L = S5_CHUNK
    d = pl.program_id(2)
    rows = [jnp.concatenate([vc_ref[pr], vl_ref[pr]], axis=0) for pr in range(S5_PAIRS)]
    for pr in range(S5_PAIRS):
        st = jnp.dot(rows[pr], e_ref[0, pr], preferred_element_type=F32)
        st_ref[pr] = st[:, :LANES]
        st_ref[N_SLAB + pr] = st[:, LANES:]

    def from_states(pr):
        st = jnp.concatenate([st_ref[pr], st_ref[N_SLAB + pr]], axis=-1).astype(BF16)
        return lax.dot_general(st, ft_ref[0, pr], (((1,), (1,)), ((), ())),
                               preferred_element_type=F32)

    @pl.when(d == 0)
    def _():
        _scan_boundary_states(st_ref, st_ref, ar_ref, ai_ref, False)
        for pr in range(S5_PAIRS):
            acc_ref[pr] = from_states(pr) + jnp.dot(rows[pr], t_ref[0, pr],
                                                    preferred_element_type=F32)

    @pl.when(d == 1)
    def _():
        _scan_boundary_states(st_ref, st_ref, ar_ref, ai_ref, True)
        ys = [acc_ref[pr] + from_states(pr) for pr in range(S5_PAIRS)]
        tiles = [tile for k in range(L // S5_PAIRS) for tile in _block_transpose4(
            [y[:, k * LANES:(k + 1) * LANES] for y in ys])]
        for t, tile in enumerate(tiles):
            yc_ref[pl.ds(t, CTX_CHUNKS, stride=L), :] = tile[:CTX_CHUNKS]
            yl_ref[pl.ds(t, LAT_CHUNKS, stride=L), :] = tile[CTX_CHUNKS:]


def _s5_out(vp, al_re, al_im, e_mat, t_mat, ft_mat):
    pair_w = S5_CHUNK * PAIR_BLK
    ctx_blk0 = (R_LAT // S5_CHUNK) // CTX_CHUNKS
    return pl.pallas_call(
        _s5_out_kernel,
        grid=(S5_QT, B, 2),
        in_specs=[
            pl.BlockSpec((S5_PAIRS, LAT_CHUNKS, pair_w), lambda q, b, d: (q, b, 0)),
            pl.BlockSpec((S5_PAIRS, CTX_CHUNKS, pair_w), lambda q, b, d: (q, ctx_blk0 + b, 0)),
            pl.BlockSpec((1, 1, S5_ST), lambda q, b, d: (2 * q + d, 0, 0)),
            pl.BlockSpec((1, 1, S5_ST), lambda q, b, d: (2 * q + d, 0, 0)),
            pl.BlockSpec((1, S5_PAIRS, pair_w, 2 * LANES), lambda q, b, d: (q, 0, 0, d)),
            pl.BlockSpec((1, S5_PAIRS, pair_w, pair_w), lambda q, b, d: (q, 0, 0, 0)),
            pl.BlockSpec((1, S5_PAIRS, pair_w, 2 * LANES), lambda q, b, d: (q, 0, 0, d)),
        ],
        out_specs=[
            pl.BlockSpec((N_LAT, LANES), lambda q, b, d: (b, q)),
            pl.BlockSpec((N_CTX, LANES), lambda q, b, d: (b, q)),
        ],
        out_shape=[
            jax.ShapeDtypeStruct((R_LAT, S5_W), F32),
            jax.ShapeDtypeStruct((R_CTX, S5_W), F32),
        ],
        scratch_shapes=[pltpu.VMEM((S5_PAIRS, BATCH_CHUNKS, pair_w), F32),
                        pltpu.VMEM((2 * N_SLAB, BATCH_CHUNKS, LANES), F32)],
        compiler_params=pltpu.CompilerParams(vmem_limit_bytes=VMEM_LIMIT),
        name="s5_out",
    )(vp, vp, al_re, al_im, e_mat, t_mat, ft_mat)


def _tail0_stages(z, ys, rest_ref, gate, gluw_ref, glub_ref, sg_g_ref, sg_b_ref, sgw_ref,
                  sgbias_ref, wout_ref, lng_ref, lnb_ref):
    ga = rest_ref[:, 0:S5_W].astype(F32)
    u = rest_ref[:, S5_W:2 * S5_W].astype(F32)
    v = rest_ref[:, 2 * S5_W:3 * S5_W].astype(F32)
    gb = rest_ref[:, 3 * S5_W:4 * S5_W].astype(F32)

    ya = _gelu_tanh(ys)
    glu = jnp.dot(ya.astype(BF16), gluw_ref[...], preferred_element_type=F32) + glub_ref[...]
    yield None
    ya = ya * _sigmoid(glu) * _silu(ga)
    yield None

    vn = _layer_norm(v, sg_g_ref[...], sg_b_ref[...]).astype(BF16)
    lane = lax.broadcasted_iota(jnp.int32, (SG_CHUNK, LANES), 1)
    first_head = lane < (SG_W // SG_HEADS)
    chunks = []
    for c in range(TN // SG_CHUNK):
        tiles = []
        for j in range(SG_W // LANES):
            vt = vn[c * SG_CHUNK:(c + 1) * SG_CHUNK, j * LANES:(j + 1) * LANES]
            zero = jnp.zeros_like(vt)
            v2 = jnp.concatenate([jnp.where(first_head, vt, zero), jnp.where(first_head, zero, vt)],
                                 axis=0)
            tiles.append(jnp.dot(sgw_ref[j], v2, preferred_element_type=F32))
        chunks.append(jnp.concatenate(tiles, axis=-1) + sgbias_ref[...])
    s = jnp.concatenate(chunks, axis=0)
    yb = u * s * _silu(gb)
    yield None

    mix = jnp.concatenate([ya, yb], axis=-1).astype(BF16)
    y = jnp.dot(mix, wout_ref[...], preferred_element_type=F32) * gate
    yield None
    yield _layer_norm(ALPHA * z + y, lng_ref[...], lnb_ref[...])


def _rope_tables():
    nf = HD // 4
    n_rows = N_LAT // GRID_W
    lane = jnp.arange(LANES)
    inv = ROPE_BASE ** (-(lane % nf).astype(F32) / nf)
    by_row = ((lane % HD) // (HD // 2) == 0)[None, :]
    sign = jnp.where((lane % (HD // 2)) < nf, -1.0, 1.0)[None, :]
    row_ang = jnp.arange(n_rows, dtype=F32)[:, None] * inv[None, :]
    col_ang = jnp.arange(GRID_W, dtype=F32)[:, None] * inv[None, :]
    zero = jnp.zeros((), F32)
    cos_t = (jnp.where(by_row, jnp.cos(row_ang), zero)[:, None, :]
             + jnp.where(by_row, zero, jnp.cos(col_ang))[None, :, :]).reshape(N_LAT, LANES)
    sin_t = (jnp.where(by_row, sign * jnp.sin(row_ang), zero)[:, None, :]
             + jnp.where(by_row, zero, sign * jnp.sin(col_ang))[None, :, :]).reshape(N_LAT, LANES)
    return cos_t, sin_t


def _rope_tile(x, cos, sin, first_half):
    nf = HD // 4
    partner = jnp.where(first_half, pltpu.roll(x, LANES - nf, axis=1), pltpu.roll(x, nf, axis=1))
    return x * cos + partner * sin


def _mid_kernel(x_ref, ctx_ref, mod0_ref, mod1_ref, ysl_ref, ysc_ref, rest_ref, gluw_ref, glub_ref,
                sg_g_ref, sg_b_ref, sgw_ref, sgbias_ref, wout_ref, lng_ref, lnb_ref, w_ref,
                cos_ref, sin_ref, z1_ref, q_ref, k_ref, vt_ref, g_ref, zprev_ref):
    i = pl.program_id(0)
    dq = N_HEADS * HD
    dkv = N_KV * HD

    @pl.when(i == 0)
    def _():
        zprev_ref[...] = jnp.zeros((TN, D), F32)

    j = jnp.maximum(i - 1, 0)
    m1 = mod1_ref[pl.ds(_block_mod_row(j), 1), :]
    h = (zprev_ref[...] * (1.0 + m1[:, D:2 * D]) + m1[:, :D]).astype(BF16)
    cos = jnp.where(j < N_LAT_BLK, cos_ref[...], 1.0)
    sin = jnp.where(j < N_LAT_BLK, sin_ref[...], 0.0)
    lane = lax.broadcasted_iota(jnp.int32, (TN, LANES), 1)
    first_half = (lane % (HD // 2)) < (HD // 4)
    scale = HD ** -0.5 * LOG2E

    def project(c0, c1):
        return jnp.dot(h, w_ref[:, c0:c1], preferred_element_type=F32)

    def roped_tiles(p, mult):
        for c in range(p.shape[1] // LANES):
            r = _rope_tile(p[:, c * LANES:(c + 1) * LANES], cos, sin, first_half)
            yield c, (r * mult if mult != 1.0 else r).astype(BF16)

    t = jnp.minimum(i, N_BLK - 1)
    z = jnp.where(t < N_LAT_BLK, x_ref[...], ctx_ref[...])
    ys = jnp.where(t < N_LAT_BLK, ysl_ref[...], ysc_ref[...])
    gate = mod0_ref[pl.ds(_block_mod_row(t), 1), 2 * D:3 * D]
    tail = _tail0_stages(z, ys, rest_ref, gate, gluw_ref, glub_ref, sg_g_ref, sg_b_ref, sgw_ref,
                         sgbias_ref, wout_ref, lng_ref, lnb_ref)

    half_q = dq // 2
    p_q0 = project(0, half_q)
    next(tail)
    p_q1 = project(half_q, dq)
    next(tail)
    p_k = project(dq, dq + dkv)
    vt_ref[...] = lax.dot_general(w_ref[:, dq + dkv:dq + 2 * dkv], h, (((0,), (1,)), ((), ())),
                                  preferred_element_type=F32).astype(BF16)
    next(tail)
    next(tail)
    for c0, p in ((0, p_q0), (half_q, p_q1)):
        for c, r in roped_tiles(p, scale):
            q_ref[:, c0 + c * LANES:c0 + (c + 1) * LANES] = r
    for c, r in roped_tiles(p_k, 1.0):
        k_ref[:, c * LANES:(c + 1) * LANES] = r
    p_g = project(dq + 2 * dkv, ODD_IN)
    z1 = next(tail)
    z1_ref[...] = z1
    zprev_ref[...] = z1
    g_ref[...] = p_g.astype(BF16)


def _mid(x2, ctx2, mod0, mod1, ys_lat, ys_ctx, rest, glu_w, glu_b, sg_g, sg_b, sg_w, sg_bias,
         w_out, ln_g, ln_b, w_in1, cos_t, sin_t):
    dq = N_HEADS * HD
    dkv = N_KV * HD
    once = dict(pipeline_mode=pl.Buffered(1))
    row = lambda n: pl.BlockSpec((1, n), lambda i: (0, 0))
    lat = lambda i: (jnp.minimum(i, N_LAT_BLK - 1), 0)
    tail_blk = lambda i: (jnp.minimum(i, N_BLK - 1), 0)
    proj_blk = lambda i: (jnp.maximum(i - 1, 0), 0)
    tab = lambda i: (jnp.maximum(i - 1, 0) % LAT_BLK_PER_BATCH, 0)
    return pl.pallas_call(
        _mid_kernel,
        grid=(N_BLK + 1,),
        in_specs=[
            pl.BlockSpec((TN, D), lat),
            pl.BlockSpec((R_CTX, D), lambda i: (0, 0), **once),
            pl.BlockSpec((8, 3 * D), lambda i: (0, 0)),
            pl.BlockSpec((8, 3 * D), lambda i: (0, 0)),
            pl.BlockSpec((TN, S5_W), lat),
            pl.BlockSpec((R_CTX, S5_W), lambda i: (0, 0), **once),
            pl.BlockSpec((TN, EVEN_IN - S5_W), tail_blk),
            pl.BlockSpec((S5_W, S5_W), lambda i: (0, 0), **once),
            row(S5_W), row(SG_W), row(SG_W),
            pl.BlockSpec((SG_HEADS // 2, SG_CHUNK, 2 * SG_CHUNK), lambda i: (0, 0, 0)),
            pl.BlockSpec((SG_CHUNK, SG_W), lambda i: (0, 0)),
            pl.BlockSpec((S5_W + SG_W, D), lambda i: (0, 0), **once),
            row(D), row(D),
            pl.BlockSpec((D, ODD_IN), lambda i: (0, 0), **once),
            pl.BlockSpec((TN, LANES), tab),
            pl.BlockSpec((TN, LANES), tab),
        ],
        out_specs=[
            pl.BlockSpec((TN, D), tail_blk),
            pl.BlockSpec((TN, dq), proj_blk),
            pl.BlockSpec((TN, dkv), proj_blk),
            pl.BlockSpec((dkv, TN), lambda i: (0, jnp.maximum(i - 1, 0))),
            pl.BlockSpec((TN, dq), proj_blk),
        ],
        out_shape=[
            jax.ShapeDtypeStruct((R_ALL, D), F32),
            jax.ShapeDtypeStruct((R_ALL, dq), BF16),
            jax.ShapeDtypeStruct((R_ALL, dkv), BF16),
            jax.ShapeDtypeStruct((dkv, R_ALL), BF16),
            jax.ShapeDtypeStruct((R_ALL, dq), BF16),
        ],
        scratch_shapes=[pltpu.VMEM((TN, D), F32)],
        compiler_params=pltpu.CompilerParams(vmem_limit_bytes=VMEM_LIMIT),
        name="mid",
    )(x2, ctx2, mod0, mod1, ys_lat, ys_ctx, rest, glu_w, glu_b, sg_g, sg_b, sg_w, sg_bias, w_out,
      ln_g, ln_b, w_in1.astype(BF16), cos_t, sin_t)


N_QBLK = N_LAT // ATT_BLK
GRP = N_HEADS // N_KV
ATT_SUB = 4
ATT_ROWS = ATT_SUB * ATT_BLK
N_QSTEP = N_QBLK // ATT_SUB
N_ATT_STEPS = B * N_QSTEP
assert ATT_ROWS == TN
ONES_ROWS = 16


def _attn_kernel(sink_ref, q_ref, kp_ref, kc_ref, kn_ref, kx_ref, vp_ref, vc_ref, vn_ref, vx_ref,
                 g_ref, z_ref, mod_ref, wout_ref, lng_ref, lnb_ref, out_ref, o_scr):
    step = pl.program_id(0)

    @pl.when(step == 0)
    def _():
        o_scr[...] = jnp.zeros((ATT_ROWS, N_HEADS * HD), BF16)

    t = jnp.maximum(step - 1, 0)
    gate = mod_ref[pl.ds(_block_mod_row(t), 1), 2 * D:3 * D]
    mix = (o_scr[...].astype(F32) * _silu(g_ref[...].astype(F32))).astype(BF16)
    y = jnp.dot(mix, wout_ref[...], preferred_element_type=F32) * gate
    out_ref[...] = _layer_norm(ALPHA * z_ref[...] + y, lng_ref[...], lnb_ref[...])

    i = jnp.minimum(step, N_ATT_STEPS - 1) % N_QSTEP
    n_win = 3 * ATT_BLK
    n_keys = n_win + N_CTX
    nq = GRP * ATT_BLK
    k_win = jnp.concatenate([kp_ref[...], kc_ref[...], kn_ref[...]], axis=0)
    vt_win = jnp.concatenate([vp_ref[...], vc_ref[...], vn_ref[...]], axis=1)
    k_ctx = kx_ref[...]
    vt_ctx = vx_ref[...]

    kpos = lax.broadcasted_iota(jnp.int32, (ATT_BLK, ATT_BLK), 0)
    qpos = lax.broadcasted_iota(jnp.int32, (ATT_BLK, ATT_BLK), 1)
    tile_q = lambda a: jnp.concatenate([a] * GRP, axis=1)
    qgrp = lax.broadcasted_iota(jnp.int32, (1, nq), 1) // ATT_BLK
    ones = jnp.ones((ONES_ROWS, n_keys), BF16)
    q = q_ref[...]

    units = [(u, h) for u in range(ATT_SUB) for h in range(N_KV)]
    k_all, vt_all, bias = [], [], []
    for u in range(ATT_SUB):
        k_all.append(jnp.concatenate([k_win[u * ATT_BLK:u * ATT_BLK + n_win], k_ctx], axis=0))
        vt_all.append(jnp.concatenate([vt_win[:, u * ATT_BLK:u * ATT_BLK + n_win], vt_ctx], axis=1))
        has_prev = (i > 0) if u == 0 else True
        has_next = (i < N_QSTEP - 1) if u == ATT_SUB - 1 else True
        bias.append((tile_q(jnp.where((kpos >= qpos) & has_prev, 0.0, NEG_INF)),
                     tile_q(jnp.where((kpos <= qpos) & has_next, 0.0, NEG_INF))))
    scores = []
    for u, h in units:
        kh = k_all[u][:, h * HD:(h + 1) * HD]
        qh = jnp.concatenate(
            [q[u * ATT_BLK:(u + 1) * ATT_BLK, (h * GRP + g) * HD:(h * GRP + g + 1) * HD]
             for g in range(GRP)], axis=0)
        scores.append(lax.dot_general(kh, qh, (((1,), (1,)), ((), ())),
                                      preferred_element_type=F32))
    probs = []
    for (u, h), s in zip(units, scores):
        s = jnp.concatenate([s[:ATT_BLK] + bias[u][0], s[ATT_BLK:2 * ATT_BLK],
                             s[2 * ATT_BLK:n_win] + bias[u][1], s[n_win:]], axis=0)
        sink = jnp.zeros((1, nq), F32)
        for g in range(GRP):
            sink = jnp.where(qgrp == g, sink_ref[h * GRP + g] * LOG2E, sink)
        m = jnp.maximum(jnp.max(s, axis=0, keepdims=True), sink)
        probs.append((jnp.exp2(s - m).astype(BF16), jnp.exp2(sink - m)))
    outs = [[] for _ in range(ATT_SUB)]
    for (u, h), (p, p_sink) in zip(units, probs):
        vt1 = jnp.concatenate([vt_all[u][h * HD:(h + 1) * HD], ones], axis=0)
        ov = jnp.dot(vt1, p, preferred_element_type=F32)
        o_t = ov[:HD] / (ov[HD:HD + 1] + p_sink)
        outs[u].extend(o_t[:, g * ATT_BLK:(g + 1) * ATT_BLK] for g in range(GRP))
    for u in range(ATT_SUB):
        o_scr[u * ATT_BLK:(u + 1) * ATT_BLK, :] = jnp.transpose(
            jnp.concatenate(outs[u], axis=0)).astype(BF16)


def _attention_tail(sink, q, k, vt, g, z1, mod, w_out, ln_g, ln_b):
    dq = N_HEADS * HD
    dkv = N_KV * HD
    ctx_blk0 = R_LAT // N_CTX
    cur = lambda s: jnp.minimum(s, N_ATT_STEPS - 1)
    batch = lambda s: cur(s) // N_QSTEP
    qstep = lambda s: cur(s) % N_QSTEP
    prev_blk = lambda s: batch(s) * N_QBLK + jnp.maximum(ATT_SUB * qstep(s) - 1, 0)
    next_blk = lambda s: batch(s) * N_QBLK + jnp.minimum(ATT_SUB * (qstep(s) + 1), N_QBLK - 1)
    tail = lambda s: jnp.maximum(s - 1, 0)
    once = dict(pipeline_mode=pl.Buffered(1))
    row = lambda n: pl.BlockSpec((1, n), lambda s, sk: (0, 0))
    return pl.pallas_call(
        _attn_kernel,
        grid_spec=pltpu.PrefetchScalarGridSpec(
            num_scalar_prefetch=1,
            grid=(N_ATT_STEPS + 1,),
            in_specs=[
                pl.BlockSpec((ATT_ROWS, dq), lambda s, sk: (cur(s), 0)),
                pl.BlockSpec((ATT_BLK, dkv), lambda s, sk: (prev_blk(s), 0)),
                pl.BlockSpec((ATT_ROWS, dkv), lambda s, sk: (cur(s), 0)),
                pl.BlockSpec((ATT_BLK, dkv), lambda s, sk: (next_blk(s), 0)),
                pl.BlockSpec((N_CTX, dkv), lambda s, sk: (ctx_blk0 + batch(s), 0)),
                pl.BlockSpec((dkv, ATT_BLK), lambda s, sk: (0, prev_blk(s))),
                pl.BlockSpec((dkv, ATT_ROWS), lambda s, sk: (0, cur(s))),
                pl.BlockSpec((dkv, ATT_BLK), lambda s, sk: (0, next_blk(s))),
                pl.BlockSpec((dkv, N_CTX), lambda s, sk: (0, ctx_blk0 + batch(s))),
                pl.BlockSpec((ATT_ROWS, dq), lambda s, sk: (tail(s), 0)),
                pl.BlockSpec((ATT_ROWS, D), lambda s, sk: (tail(s), 0)),
                pl.BlockSpec((8, 3 * D), lambda s, sk: (0, 0)),
                pl.BlockSpec((dq, D), lambda s, sk: (0, 0), **once),
                row(D), row(D),
            ],
            out_specs=pl.BlockSpec((ATT_ROWS, D), lambda s, sk: (tail(s), 0)),
            scratch_shapes=[pltpu.VMEM((ATT_ROWS, dq), BF16)],
        ),
        out_shape=jax.ShapeDtypeStruct((R_LAT, D), F32),
        compiler_params=pltpu.CompilerParams(vmem_limit_bytes=VMEM_LIMIT),
        name="attention",
    )(sink, q, k, k, k, k, vt, vt, vt, vt, g, z1, mod, w_out, ln_g, ln_b)


def kernel(x, c, ctx, c_ctx, mod_w, mod_b, ln_g, ln_b, e_w_in, e_w_out, s5_lam_re, s5_lam_im,
           s5_log_dt, s5_b_re, s5_b_im, s5_c_re, s5_c_im, s5_d, glu_w, glu_b, sg_ln_g, sg_ln_b,
           sg_w, sg_b, o_w_in, o_w_out, o_sink):
    x2 = x.reshape(R_LAT, D)
    ctx2 = ctx.reshape(R_CTX, D)
    mod = _modulation(c, c_ctx, mod_w, mod_b)

    vp, rest = _in0(x2, ctx2, mod[0], e_w_in[0].astype(BF16))
    t_mat, e_mat, ft_mat, al_re, al_im = _s5_prep(
        s5_lam_re[0], s5_lam_im[0], s5_log_dt[0], s5_b_re[0], s5_b_im[0], s5_c_re[0], s5_c_im[0],
        s5_d[0])
    ys_lat, ys_ctx = _s5_out(vp, al_re, al_im, e_mat, t_mat, ft_mat)
    sg_bias = jnp.repeat(jnp.transpose(sg_b[0]), SG_W // SG_HEADS, axis=1)
    sg_w2 = jnp.transpose(sg_w[0].reshape(SG_HEADS // 2, 2, SG_CHUNK, SG_CHUNK), (0, 2, 1, 3))
    sg_w2 = sg_w2.reshape(SG_HEADS // 2, SG_CHUNK, 2 * SG_CHUNK).astype(BF16)
    cos_t, sin_t = _rope_tables()
    z1, q, k, vt, g = _mid(
        x2, ctx2, mod[0], mod[1], ys_lat, ys_ctx, rest, glu_w[0].astype(BF16),
        glu_b[0].reshape(1, S5_W), sg_ln_g[0].reshape(1, SG_W), sg_ln_b[0].reshape(1, SG_W),
        sg_w2, sg_bias, e_w_out[0].astype(BF16), ln_g[0].reshape(1, D),
        ln_b[0].reshape(1, D), o_w_in[0], cos_t, sin_t)

    out = _attention_tail(o_sink[0], q, k, vt, g, z1, mod[1], o_w_out[0].astype(BF16),
                          ln_g[1].reshape(1, D), ln_b[1].reshape(1, D))
    return out.reshape(B, N_LAT, D)
```

```python
import functools
import math

import jax
import jax.numpy as jnp
from jax import lax
from jax.experimental import pallas as pl
from jax.experimental.pallas import tpu as pltpu

F32 = jnp.float32
BF16 = jnp.bfloat16

D = 1024
B = 2
N_LAT = 8192
N_CTX = 256
DEPTH = 2
GRID_W = 64
S5_W = 512
S5_GC = 16
S5_G = 32
S5_P = 64
SG_W = 512
SG_HEADS = 8
SG_CHUNK = 128
N_HEADS = 16
N_KV = 4
HD = 64
WINDOW = 128
ATT_BLK = 128
ROPE_BASE = 10000.0
NEG_INF = -1e30
LN_EPS = 1e-5
ALPHA = (2 * DEPTH) ** 0.25
LOG2E = math.log2(math.e)
EVEN_IN = 2 * S5_W + 3 * SG_W
ODD_IN = 2 * N_HEADS * HD + 2 * N_KV * HD

LANES = 128
VMEM_LIMIT = 56 * 1024 * 1024

R_LAT = B * N_LAT
R_CTX = B * N_CTX
R_ALL = R_LAT + R_CTX
TN = 512
N_LAT_BLK = R_LAT // TN
N_BLK = R_ALL // TN
LAT_BLK_PER_BATCH = N_LAT // TN
CTX_MOD_ROW = B

S5_CHUNK = 8
S5_QT = S5_W // LANES
S5_GPT = LANES // S5_GC
S5_ST = S5_GPT * S5_P
S5_PAIRS = S5_GPT // 2
PAIR_BLK = 2 * S5_GC
LAT_CHUNKS = N_LAT // S5_CHUNK
CTX_CHUNKS = N_CTX // S5_CHUNK
BATCH_CHUNKS = CTX_CHUNKS + LAT_CHUNKS
N_SEG = 16
SEG_CHUNKS = BATCH_CHUNKS // N_SEG
BWD_RESET_STEP = LAT_CHUNKS - (N_SEG - 1) * SEG_CHUNKS


def _sigmoid(x):
    return 1.0 / (1.0 + jnp.exp(-x))


def _silu(x):
    return x * _sigmoid(x)


def _gelu_tanh(x):
    return 0.5 * x * (1.0 + jnp.tanh(math.sqrt(2.0 / math.pi) * (x + 0.044715 * (x * x * x))))


def _layer_norm(x, g, b):
    mu = jnp.mean(x, axis=-1, keepdims=True)
    xc = x - mu
    var = jnp.mean(xc * xc, axis=-1, keepdims=True)
    return xc * lax.rsqrt(var + LN_EPS) * g + b


def _split_bf16(x):
    hi = x.astype(BF16)
    return hi, (x - hi.astype(F32)).astype(BF16)


def _dot_split(x, y_split, dims):
    x_hi, x_lo = _split_bf16(x)
    y_hi, y_lo = y_split
    dot = lambda a, b: lax.dot_general(a, b, dims, preferred_element_type=F32)
    return dot(x_hi, y_hi) + (dot(x_hi, y_lo) + dot(x_lo, y_hi))


def _block_mod_row(i):
    return jnp.where(i < N_LAT_BLK, i // LAT_BLK_PER_BATCH, CTX_MOD_ROW)


def _mod_kernel(cv_ref, w_ref, b_ref, o_ref):
    s = _silu(cv_ref[...])
    o_ref[0] = _dot_split(s, _split_bf16(w_ref[0]), (((1,), (0,)), ((), ()))) + b_ref[0]


def _modulation(c, c_ctx, mod_w, mod_b):
    cv = jnp.concatenate([c, c_ctx[None], jnp.zeros((8 - B - 1, D), F32)], axis=0)
    tn = 1024
    return pl.pallas_call(
        _mod_kernel,
        grid=(DEPTH, 3 * D // tn),
        in_specs=[
            pl.BlockSpec((8, D), lambda l, j: (0, 0)),
            pl.BlockSpec((1, D, tn), lambda l, j: (l, 0, j)),
            pl.BlockSpec((1, 1, tn), lambda l, j: (l, 0, j)),
        ],
        out_specs=pl.BlockSpec((1, 8, tn), lambda l, j: (l, 0, j)),
        out_shape=jax.ShapeDtypeStruct((DEPTH, 8, 3 * D), F32),
        compiler_params=pltpu.CompilerParams(vmem_limit_bytes=VMEM_LIMIT),
        name="modulation",
    )(cv, mod_w, mod_b.reshape(DEPTH, 1, 3 * D))


def _block_transpose4(tiles):
    tiles = list(tiles)
    blk = lax.broadcasted_iota(jnp.int32, tiles[0].shape, 1) // PAIR_BLK
    for dist in (2, 1):
        keep = (blk & dist) == 0
        for i in range(len(tiles)):
            if i & dist:
                continue
            lo, hi = tiles[i], tiles[i + dist]
            tiles[i] = jnp.where(keep, lo, pltpu.roll(hi, dist * PAIR_BLK, axis=1))
            tiles[i + dist] = jnp.where(keep, pltpu.roll(lo, LANES - dist * PAIR_BLK, axis=1), hi)
    return tiles


def _in0_kernel(x_ref, ctx_ref, mod_ref, w_ref, vp_ref, rest_ref, xa_scr, wbf_scr):
    L = S5_CHUNK
    i = pl.program_id(0)

    @pl.when(i == 0)
    def _():
        wbf_scr[...] = w_ref[...].astype(BF16)

    z = jnp.where(i < N_LAT_BLK, x_ref[...], ctx_ref[...])
    m = mod_ref[pl.ds(_block_mod_row(i), 1), :]
    h = z * (1.0 + m[:, D:2 * D]) + m[:, :D]
    p = jnp.dot(h.astype(BF16), wbf_scr[...], preferred_element_type=F32)
    rest_ref[...] = p[:, S5_W:].astype(BF16)
    for q in range(S5_QT):
        xa_scr[q] = p[:, q * LANES:(q + 1) * LANES]
        tiles = [xa_scr[q, pl.ds(t, TN // L, stride=L), :] for t in range(L)]
        halves = [_block_transpose4(tiles[k * S5_PAIRS:(k + 1) * S5_PAIRS])
                  for k in range(L // S5_PAIRS)]
        for pr in range(S5_PAIRS):
            vp_ref[q * S5_PAIRS + pr] = jnp.concatenate(
                [half[pr] for half in halves], axis=-1).astype(BF16)


def _in0(x2, ctx2, mod, w_in):
    pair_w = S5_CHUNK * PAIR_BLK
    return pl.pallas_call(
        _in0_kernel,
        grid=(N_BLK,),
        in_specs=[
            pl.BlockSpec((TN, D), lambda i: (jnp.minimum(i, N_LAT_BLK - 1), 0)),
            pl.BlockSpec((R_CTX, D), lambda i: (0, 0)),
            pl.BlockSpec((8, 3 * D), lambda i: (0, 0)),
            pl.BlockSpec((D, EVEN_IN), lambda i: (0, 0)),
        ],
        out_specs=[
            pl.BlockSpec((S5_QT * S5_PAIRS, TN // S5_CHUNK, pair_w), lambda i: (0, i, 0)),
            pl.BlockSpec((TN, EVEN_IN - S5_W), lambda i: (i, 0)),
        ],
        out_shape=[
            jax.ShapeDtypeStruct((S5_QT * S5_PAIRS, R_ALL // S5_CHUNK, pair_w), BF16),
            jax.ShapeDtypeStruct((R_ALL, EVEN_IN - S5_W), BF16),
        ],
        scratch_shapes=[pltpu.VMEM((S5_QT, TN, LANES), F32), pltpu.VMEM((D, EVEN_IN), BF16)],
        compiler_params=pltpu.CompilerParams(vmem_limit_bytes=VMEM_LIMIT),
        name="in0",
    )(x2, ctx2, mod, w_in)


def _s5_prep_kernel(rows_ref, bre_ref, bim_ref, cre_ref, cim_ref, dsk_ref, t_ref, e_ref, ft_ref,
                    alr_ref, ali_ref):
    L = S5_CHUNK
    tile = (LANES, S5_ST)
    same_group = (lax.broadcasted_iota(jnp.int32, tile, 0) // S5_GC
                  == lax.broadcasted_iota(jnp.int32, tile, 1) // S5_P)
    nt_dims = (((1,), (1,)), ((), ()))
    nn_dims = (((1,), (0,)), ((), ()))
    pick_ch = (lax.broadcasted_iota(jnp.int32, (LANES, S5_GC), 0) % S5_GC
               == lax.broadcasted_iota(jnp.int32, (LANES, S5_GC), 1)).astype(BF16)
    rep_st = (lax.broadcasted_iota(jnp.int32, (S5_P, S5_ST), 0)
              == lax.broadcasted_iota(jnp.int32, (S5_P, S5_ST), 1) % S5_P).astype(BF16)

    def spread_b(b):
        hi, lo = _split_bf16(b)
        dot = lambda v: lax.dot_general(pick_ch, v, nt_dims, preferred_element_type=F32)
        return jnp.where(same_group, dot(hi) + dot(lo), 0.0)

    def spread_c(c):
        hi, lo = _split_bf16(c)
        dot = lambda v: lax.dot_general(v, rep_st, nn_dims, preferred_element_type=F32)
        return jnp.where(same_group, dot(hi) + dot(lo), 0.0)

    pair_rows = lambda pr: slice(pr * PAIR_BLK, (pr + 1) * PAIR_BLK)
    pair_lanes = lambda pr: slice(pr * LANES, (pr + 1) * LANES)
    lag_strip = []
    for d in range(2):
        lam_re = rows_ref[d, 0]
        lam_im = rows_ref[d, 1]
        dt = jnp.exp(rows_ref[d, 2])
        pw = []
        for k in range(L + 1):
            mag = jnp.exp(lam_re * dt * float(k))
            ang = lam_im * dt * float(k)
            pw.append((mag * jnp.cos(ang), mag * jnp.sin(ang)))
        alr_ref[d] = pw[L][0]
        ali_ref[d] = pw[L][1]
        den = lam_re * lam_re + lam_im * lam_im
        nr = pw[1][0] - 1.0
        ni = pw[1][1]
        f_re = (nr * lam_re + ni * lam_im) / den
        f_im = (ni * lam_re - nr * lam_im) / den
        b_re = spread_b(bre_ref[d])
        b_im = spread_b(bim_ref[d])
        bb_re = f_re * b_re - f_im * b_im
        bb_im = f_re * b_im + f_im * b_re
        c_re = spread_c(cre_ref[d])
        c_im = spread_c(cim_ref[d])
        for k in range(L):
            x_re = pw[k][0] * bb_re - pw[k][1] * bb_im
            x_im = pw[k][0] * bb_im + pw[k][1] * bb_re
            s = (L - 1 - k) if d == 0 else k
            for pr in range(S5_PAIRS):
                blk = slice(s * PAIR_BLK, (s + 1) * PAIR_BLK)
                e_ref[0, pr, blk, 2 * d * LANES:(2 * d + 1) * LANES] = (
                    x_re[pair_rows(pr), pair_lanes(pr)].astype(BF16))
                e_ref[0, pr, blk, (2 * d + 1) * LANES:(2 * d + 2) * LANES] = (
                    x_im[pair_rows(pr), pair_lanes(pr)].astype(BF16))
        cw = []
        for k in range(L + 1):
            w_re = c_re * pw[k][0] - c_im * pw[k][1]
            w_im = c_re * pw[k][1] + c_im * pw[k][0]
            cw.append((w_re, -w_im))
        for t in range(L):
            k = (t + 1) if d == 0 else (L - t)
            for pr in range(S5_PAIRS):
                blk = slice(t * PAIR_BLK, (t + 1) * PAIR_BLK)
                ft_ref[0, pr, blk, 2 * d * LANES:(2 * d + 1) * LANES] = (
                    cw[k][0][pair_rows(pr), pair_lanes(pr)].astype(BF16))
                ft_ref[0, pr, blk, (2 * d + 1) * LANES:(2 * d + 2) * LANES] = (
                    cw[k][1][pair_rows(pr), pair_lanes(pr)].astype(BF16))
        lags = range(L) if d == 0 else range(L - 1, -1, -1)
        c_stack = jnp.concatenate(
            [jnp.concatenate([cw[k][0][pair_rows(pr)], cw[k][1][pair_rows(pr)]], axis=-1)
             for pr in range(S5_PAIRS) for k in lags], axis=0)
        lag_strip.append(_dot_split(jnp.concatenate([bb_re, bb_im], axis=-1),
                                    _split_bf16(c_stack), nt_dims))

    strip_w = 2 * L * PAIR_BLK
    mid = (L - 1) * PAIR_BLK
    widen = lambda a: jnp.concatenate([a, jnp.zeros((a.shape[0], strip_w - a.shape[1]), F32)], axis=-1)
    row = lax.broadcasted_iota(jnp.int32, (PAIR_BLK, strip_w), 0)
    lane = lax.broadcasted_iota(jnp.int32, (PAIR_BLK, strip_w), 1)
    d_row = widen(dsk_ref[0])
    for pr in range(S5_PAIRS):
        lanes = slice(pr * L * PAIR_BLK, (pr + 1) * L * PAIR_BLK)
        strip = (widen(lag_strip[1][pair_rows(pr), lanes])
                 + pltpu.roll(widen(lag_strip[0][pair_rows(pr), lanes]), mid, axis=1)
                 + jnp.where(lane == row + mid,
                             pltpu.roll(d_row, mid - pr * PAIR_BLK, axis=1), 0.0))
        for s in range(L):
            shift = (strip_w - (L - 1 - s) * PAIR_BLK) % strip_w
            blk = pltpu.roll(strip, shift, axis=1) if shift else strip
            t_ref[0, pr, s * PAIR_BLK:(s + 1) * PAIR_BLK, :] = blk[:, :L * PAIR_BLK].astype(BF16)


def _s5_prep(lam_re, lam_im, log_dt, b_re, b_im, c_re, c_im, d_skip):
    L = S5_CHUNK
    n_st = S5_G * S5_P
    pair_w = L * PAIR_BLK
    rows = jnp.stack([lam_re.reshape(2, 1, n_st), lam_im.reshape(2, 1, n_st),
                      jnp.repeat(log_dt, S5_P, axis=1).reshape(2, 1, n_st)], axis=1)
    b_spec = pl.BlockSpec((2, S5_ST, S5_GC), lambda q: (0, q, 0))
    c_spec = pl.BlockSpec((2, LANES, S5_P), lambda q: (0, q, 0))
    return pl.pallas_call(
        _s5_prep_kernel,
        grid=(S5_QT,),
        in_specs=[
            pl.BlockSpec((2, 3, 1, S5_ST), lambda q: (0, 0, 0, q)),
            b_spec, b_spec, c_spec, c_spec,
            pl.BlockSpec((1, 1, LANES), lambda q: (q, 0, 0)),
        ],
        out_specs=[
            pl.BlockSpec((1, S5_PAIRS, pair_w, pair_w), lambda q: (q, 0, 0, 0)),
            pl.BlockSpec((1, S5_PAIRS, pair_w, 4 * LANES), lambda q: (q, 0, 0, 0)),
            pl.BlockSpec((1, S5_PAIRS, pair_w, 4 * LANES), lambda q: (q, 0, 0, 0)),
            pl.BlockSpec((2, 1, S5_ST), lambda q: (q, 0, 0)),
            pl.BlockSpec((2, 1, S5_ST), lambda q: (q, 0, 0)),
        ],
        out_shape=[
            jax.ShapeDtypeStruct((S5_QT, S5_PAIRS, pair_w, pair_w), BF16),
            jax.ShapeDtypeStruct((S5_QT, S5_PAIRS, pair_w, 4 * LANES), BF16),
            jax.ShapeDtypeStruct((S5_QT, S5_PAIRS, pair_w, 4 * LANES), BF16),
            jax.ShapeDtypeStruct((2 * S5_QT, 1, S5_ST), F32),
            jax.ShapeDtypeStruct((2 * S5_QT, 1, S5_ST), F32),
        ],
        compiler_params=pltpu.CompilerParams(vmem_limit_bytes=VMEM_LIMIT),
        name="s5_prep",
    )(rows, b_re.reshape(2, n_st, S5_GC), b_im.reshape(2, n_st, S5_GC),
      c_re.reshape(2, S5_W, S5_P), c_im.reshape(2, S5_W, S5_P), d_skip.reshape(S5_QT, 1, LANES))


N_SLAB = S5_ST // LANES


def _cmul(ar, ai, br, bi):
    return ar * br - ai * bi, ar * bi + ai * br


def _cpow(ar, ai, n):
    res = None
    while n:
        if n & 1:
            res = (ar, ai) if res is None else _cmul(res[0], res[1], ar, ai)
        n >>= 1
        if n:
            ar, ai = _cmul(ar, ai, ar, ai)
    return res


def _scan_boundary_states(loc_ref, out_ref, ar_ref, ai_ref, backward):
    a_row = [(ar_ref[0][:, k * LANES:(k + 1) * LANES], ai_ref[0][:, k * LANES:(k + 1) * LANES])
             for k in range(N_SLAB)]
    a_tile = [(jnp.broadcast_to(r, (N_SEG, LANES)), jnp.broadcast_to(i, (N_SEG, LANES)))
              for r, i in a_row]

    def sweep(lo, hi, carry, store):
        def step(i, carry):
            off = (SEG_CHUNKS - 1 - i) if backward else i
            rows = pl.ds(off, N_SEG, stride=SEG_CHUNKS)
            new = []
            for k, (cr, ci) in enumerate(carry):
                xr = loc_ref[k, rows, :]
                xi = loc_ref[N_SLAB + k, rows, :]
                if store:
                    out_ref[k, rows, :] = cr
                    out_ref[N_SLAB + k, rows, :] = ci
                nr, ni = _cmul(a_tile[k][0], a_tile[k][1], cr, ci)
                new.append((nr + xr, ni + xi))
            return tuple(new)
        return lax.fori_loop(lo, hi, step, carry)

    def full(carry, store):
        if not backward:
            return sweep(0, SEG_CHUNKS, carry, store)
        carry = sweep(0, BWD_RESET_STEP, carry, store)
        first = lax.broadcasted_iota(jnp.int32, (N_SEG, LANES), 0) == 0
        carry = tuple((jnp.where(first, 0.0, cr), jnp.where(first, 0.0, ci)) for cr, ci in carry)
        return sweep(BWD_RESET_STEP, SEG_CHUNKS, carry, store)

    zero_tile = jnp.zeros((N_SEG, LANES), F32)
    ends = full(tuple((zero_tile, zero_tile) for _ in range(N_SLAB)), False)

    starts = []
    for k, (er, ei) in enumerate(ends):
        pr, pi = _cpow(a_row[k][0], a_row[k][1], SEG_CHUNKS)
        if backward:
            order = range(N_SEG - 1, -1, -1)
            g = (er[0:1], ei[0:1])
        else:
            order = range(N_SEG)
            g = (jnp.zeros((1, LANES), F32), jnp.zeros((1, LANES), F32))
        rows_r = [None] * N_SEG
        rows_i = [None] * N_SEG
        for j in order:
            rows_r[j], rows_i[j] = g
            nr, ni = _cmul(pr, pi, g[0], g[1])
            g = (nr + er[j:j + 1], ni + ei[j:j + 1])
        starts.append((jnp.concatenate(rows_r, axis=0), jnp.concatenate(rows_i, axis=0)))
    full(tuple(starts), True)


def _s5_out_kernel(vl_ref, vc_ref, ar_ref, ai_ref, e_ref, t_ref, ft_ref, yl_ref, yc_ref,
                   acc_ref, st_ref):
    L = S5_CHUNK
    d = pl.program_id(2)
    rows = [jnp.concatenate([vc_ref[pr], vl_ref[pr]], axis=0) for pr in range(S5_PAIRS)]
    for pr in range(S5_PAIRS):
        st = jnp.dot(rows[pr], e_ref[0, pr], preferred_element_type=F32)
        st_ref[pr] = st[:, :LANES]
        st_ref[N_SLAB + pr] = st[:, LANES:]

    def from_states(pr):
        st = jnp.concatenate([st_ref[pr], st_ref[N_SLAB + pr]], axis=-1).astype(BF16)
        return lax.dot_general(st, ft_ref[0, pr], (((1,), (1,)), ((), ())),
                               preferred_element_type=F32)

    @pl.when(d == 0)
    def _():
        _scan_boundary_states(st_ref, st_ref, ar_ref, ai_ref, False)
        for pr in range(S5_PAIRS):
            acc_ref[pr] = from_states(pr) + jnp.dot(rows[pr], t_ref[0, pr],
                                                    preferred_element_type=F32)

    @pl.when(d == 1)
    def _():
        _scan_boundary_states(st_ref, st_ref, ar_ref, ai_ref, True)
        ys = [acc_ref[pr] + from_states(pr) for pr in range(S5_PAIRS)]
        tiles = [tile for k in range(L // S5_PAIRS) for tile in _block_transpose4(
            [y[:, k * LANES:(k + 1) * LANES] for y in ys])]
        for t, tile in enumerate(tiles):
            yc_ref[pl.ds(t, CTX_CHUNKS, stride=L), :] = tile[:CTX_CHUNKS]
            yl_ref[pl.ds(t, LAT_CHUNKS, stride=L), :] = tile[CTX_CHUNKS:]


def _s5_out(vp, al_re, al_im, e_mat, t_mat, ft_mat):
    pair_w = S5_CHUNK * PAIR_BLK
    ctx_blk0 = (R_LAT // S5_CHUNK) // CTX_CHUNKS
    return pl.pallas_call(
        _s5_out_kernel,
        grid=(S5_QT, B, 2),
        in_specs=[
            pl.BlockSpec((S5_PAIRS, LAT_CHUNKS, pair_w), lambda q, b, d: (q, b, 0)),
            pl.BlockSpec((S5_PAIRS, CTX_CHUNKS, pair_w), lambda q, b, d: (q, ctx_blk0 + b, 0)),
            pl.BlockSpec((1, 1, S5_ST), lambda q, b, d: (2 * q + d, 0, 0)),
            pl.BlockSpec((1, 1, S5_ST), lambda q, b, d: (2 * q + d, 0, 0)),
            pl.BlockSpec((1, S5_PAIRS, pair_w, 2 * LANES), lambda q, b, d: (q, 0, 0, d)),
            pl.BlockSpec((1, S5_PAIRS, pair_w, pair_w), lambda q, b, d: (q, 0, 0, 0)),
            pl.BlockSpec((1, S5_PAIRS, pair_w, 2 * LANES), lambda q, b, d: (q, 0, 0, d)),
        ],
        out_specs=[
            pl.BlockSpec((N_LAT, LANES), lambda q, b, d: (b, q)),
            pl.BlockSpec((N_CTX, LANES), lambda q, b, d: (b, q)),
        ],
        out_shape=[
            jax.ShapeDtypeStruct((R_LAT, S5_W), F32),
            jax.ShapeDtypeStruct((R_CTX, S5_W), F32),
        ],
        scratch_shapes=[pltpu.VMEM((S5_PAIRS, BATCH_CHUNKS, pair_w), F32),
                        pltpu.VMEM((2 * N_SLAB, BATCH_CHUNKS, LANES), F32)],
        compiler_params=pltpu.CompilerParams(vmem_limit_bytes=VMEM_LIMIT),
        name="s5_out",
    )(vp, vp, al_re, al_im, e_mat, t_mat, ft_mat)


def _tail0_stages(z, ys, rest_ref, gate, gluw_ref, glub_ref, sg_g_ref, sg_b_ref, sgw_ref,
                  sgbias_ref, wout_ref, lng_ref, lnb_ref):
    ga = rest_ref[:, 0:S5_W].astype(F32)
    u = rest_ref[:, S5_W:2 * S5_W].astype(F32)
    v = rest_ref[:, 2 * S5_W:3 * S5_W].astype(F32)
    gb = rest_ref[:, 3 * S5_W:4 * S5_W].astype(F32)

    ya = _gelu_tanh(ys)
    glu = jnp.dot(ya.astype(BF16), gluw_ref[...], preferred_element_type=F32) + glub_ref[...]
    yield None
    ya = ya * _sigmoid(glu) * _silu(ga)
    yield None

    vn = _layer_norm(v, sg_g_ref[...], sg_b_ref[...]).astype(BF16)
    lane = lax.broadcasted_iota(jnp.int32, (SG_CHUNK, LANES), 1)
    first_head = lane < (SG_W // SG_HEADS)
    chunks = []
    for c in range(TN // SG_CHUNK):
        tiles = []
        for j in range(SG_W // LANES):
            vt = vn[c * SG_CHUNK:(c + 1) * SG_CHUNK, j * LANES:(j + 1) * LANES]
            zero = jnp.zeros_like(vt)
            v2 = jnp.concatenate([jnp.where(first_head, vt, zero), jnp.where(first_head, zero, vt)],
                                 axis=0)
            tiles.append(jnp.dot(sgw_ref[j], v2, preferred_element_type=F32))
        chunks.append(jnp.concatenate(tiles, axis=-1) + sgbias_ref[...])
    s = jnp.concatenate(chunks, axis=0)
    yb = u * s * _silu(gb)
    yield None

    mix = jnp.concatenate([ya, yb], axis=-1).astype(BF16)
    y = jnp.dot(mix, wout_ref[...], preferred_element_type=F32) * gate
    yield None
    yield _layer_norm(ALPHA * z + y, lng_ref[...], lnb_ref[...])


def _rope_tables():
    nf = HD // 4
    n_rows = N_LAT // GRID_W
    lane = jnp.arange(LANES)
    inv = ROPE_BASE ** (-(lane % nf).astype(F32) / nf)
    by_row = ((lane % HD) // (HD // 2) == 0)[None, :]
    sign = jnp.where((lane % (HD // 2)) < nf, -1.0, 1.0)[None, :]
    row_ang = jnp.arange(n_rows, dtype=F32)[:, None] * inv[None, :]
    col_ang = jnp.arange(GRID_W, dtype=F32)[:, None] * inv[None, :]
    zero = jnp.zeros((), F32)
    cos_t = (jnp.where(by_row, jnp.cos(row_ang), zero)[:, None, :]
             + jnp.where(by_row, zero, jnp.cos(col_ang))[None, :, :]).reshape(N_LAT, LANES)
    sin_t = (jnp.where(by_row, sign * jnp.sin(row_ang), zero)[:, None, :]
             + jnp.where(by_row, zero, sign * jnp.sin(col_ang))[None, :, :]).reshape(N_LAT, LANES)
    return cos_t, sin_t


def _rope_tile(x, cos, sin, first_half):
    nf = HD // 4
    partner = jnp.where(first_half, pltpu.roll(x, LANES - nf, axis=1), pltpu.roll(x, nf, axis=1))
    return x * cos + partner * sin


def _mid_kernel(x_ref, ctx_ref, mod0_ref, mod1_ref, ysl_ref, ysc_ref, rest_ref, gluw_ref, glub_ref,
                sg_g_ref, sg_b_ref, sgw_ref, sgbias_ref, wout_ref, lng_ref, lnb_ref, w_ref,
                cos_ref, sin_ref, z1_ref, q_ref, k_ref, vt_ref, g_ref, zprev_ref, wbf_scr):
    i = pl.program_id(0)
    dq = N_HEADS * HD
    dkv = N_KV * HD

    @pl.when(i == 0)
    def _():
        zprev_ref[...] = jnp.zeros((TN, D), F32)
        wbf_scr[...] = w_ref[...].astype(BF16)

    j = jnp.maximum(i - 1, 0)
    m1 = mod1_ref[pl.ds(_block_mod_row(j), 1), :]
    h = (zprev_ref[...] * (1.0 + m1[:, D:2 * D]) + m1[:, :D]).astype(BF16)
    cos = jnp.where(j < N_LAT_BLK, cos_ref[...], 1.0)
    sin = jnp.where(j < N_LAT_BLK, sin_ref[...], 0.0)
    lane = lax.broadcasted_iota(jnp.int32, (TN, LANES), 1)
    first_half = (lane % (HD // 2)) < (HD // 4)
    scale = HD ** -0.5 * LOG2E

    def project(c0, c1):
        return jnp.dot(h, wbf_scr[:, c0:c1], preferred_element_type=F32)

    def roped_tiles(p, mult):
        for c in range(p.shape[1] // LANES):
            r = _rope_tile(p[:, c * LANES:(c + 1) * LANES], cos, sin, first_half)
            yield c, (r * mult if mult != 1.0 else r).astype(BF16)

    t = jnp.minimum(i, N_BLK - 1)
    z = jnp.where(t < N_LAT_BLK, x_ref[...], ctx_ref[...])
    ys = jnp.where(t < N_LAT_BLK, ysl_ref[...], ysc_ref[...])
    gate = mod0_ref[pl.ds(_block_mod_row(t), 1), 2 * D:3 * D]
    tail = _tail0_stages(z, ys, rest_ref, gate, gluw_ref, glub_ref, sg_g_ref, sg_b_ref, sgw_ref,
                         sgbias_ref, wout_ref, lng_ref, lnb_ref)

    half_q = dq // 2
    p_q0 = project(0, half_q)
    next(tail)
    p_q1 = project(half_q, dq)
    next(tail)
    p_k = project(dq, dq + dkv)
    vt_ref[...] = lax.dot_general(wbf_scr[:, dq + dkv:dq + 2 * dkv], h, (((0,), (1,)), ((), ())),
                                  preferred_element_type=F32).astype(BF16)
    next(tail)
    next(tail)
    for c0, p in ((0, p_q0), (half_q, p_q1)):
        for c, r in roped_tiles(p, scale):
            q_ref[:, c0 + c * LANES:c0 + (c + 1) * LANES] = r
    for c, r in roped_tiles(p_k, 1.0):
        k_ref[:, c * LANES:(c + 1) * LANES] = r
    p_g = project(dq + 2 * dkv, ODD_IN)
    z1 = next(tail)
    z1_ref[...] = z1
    zprev_ref[...] = z1
    g_ref[...] = p_g.astype(BF16)


def _mid(x2, ctx2, mod0, mod1, ys_lat, ys_ctx, rest, glu_w, glu_b, sg_g, sg_b, sg_w, sg_bias,
         w_out, ln_g, ln_b, w_in1, cos_t, sin_t):
    dq = N_HEADS * HD
    dkv = N_KV * HD
    once = dict(pipeline_mode=pl.Buffered(1))
    row = lambda n: pl.BlockSpec((1, n), lambda i: (0, 0))
    lat = lambda i: (jnp.minimum(i, N_LAT_BLK - 1), 0)
    tail_blk = lambda i: (jnp.minimum(i, N_BLK - 1), 0)
    proj_blk = lambda i: (jnp.maximum(i - 1, 0), 0)
    tab = lambda i: (jnp.maximum(i - 1, 0) % LAT_BLK_PER_BATCH, 0)
    return pl.pallas_call(
        _mid_kernel,
        grid=(N_BLK + 1,),
        in_specs=[
            pl.BlockSpec((TN, D), lat),
            pl.BlockSpec((R_CTX, D), lambda i: (0, 0), **once),
            pl.BlockSpec((8, 3 * D), lambda i: (0, 0)),
            pl.BlockSpec((8, 3 * D), lambda i: (0, 0)),
            pl.BlockSpec((TN, S5_W), lat),
            pl.BlockSpec((R_CTX, S5_W), lambda i: (0, 0), **once),
            pl.BlockSpec((TN, EVEN_IN - S5_W), tail_blk),
            pl.BlockSpec((S5_W, S5_W), lambda i: (0, 0), **once),
            row(S5_W), row(SG_W), row(SG_W),
            pl.BlockSpec((SG_HEADS // 2, SG_CHUNK, 2 * SG_CHUNK), lambda i: (0, 0, 0)),
            pl.BlockSpec((SG_CHUNK, SG_W), lambda i: (0, 0)),
            pl.BlockSpec((S5_W + SG_W, D), lambda i: (0, 0), **once),
            row(D), row(D),
            pl.BlockSpec((D, ODD_IN), lambda i: (0, 0), **once),
            pl.BlockSpec((TN, LANES), tab),
            pl.BlockSpec((TN, LANES), tab),
        ],
        out_specs=[
            pl.BlockSpec((TN, D), tail_blk),
            pl.BlockSpec((TN, dq), proj_blk),
            pl.BlockSpec((TN, dkv), proj_blk),
            pl.BlockSpec((dkv, TN), lambda i: (0, jnp.maximum(i - 1, 0))),
            pl.BlockSpec((TN, dq), proj_blk),
        ],
        out_shape=[
            jax.ShapeDtypeStruct((R_ALL, D), F32),
            jax.ShapeDtypeStruct((R_ALL, dq), BF16),
            jax.ShapeDtypeStruct((R_ALL, dkv), BF16),
            jax.ShapeDtypeStruct((dkv, R_ALL), BF16),
            jax.ShapeDtypeStruct((R_ALL, dq), BF16),
        ],
        scratch_shapes=[pltpu.VMEM((TN, D), F32), pltpu.VMEM((D, ODD_IN), BF16)],
        compiler_params=pltpu.CompilerParams(vmem_limit_bytes=VMEM_LIMIT),
        name="mid",
    )(x2, ctx2, mod0, mod1, ys_lat, ys_ctx, rest, glu_w, glu_b, sg_g, sg_b, sg_w, sg_bias, w_out,
      ln_g, ln_b, w_in1, cos_t, sin_t)


N_QBLK = N_LAT // ATT_BLK
GRP = N_HEADS // N_KV
ATT_SUB = 4
ATT_ROWS = ATT_SUB * ATT_BLK
N_QSTEP = N_QBLK // ATT_SUB
N_ATT_STEPS = B * N_QSTEP
assert ATT_ROWS == TN
ONES_ROWS = 16


def _attn_kernel(sink_ref, q_ref, kp_ref, kc_ref, kn_ref, kx_ref, vp_ref, vc_ref, vn_ref, vx_ref,
                 g_ref, z_ref, mod_ref, wout_ref, lng_ref, lnb_ref, out_ref, o_scr):
    step = pl.program_id(0)

    @pl.when(step == 0)
    def _():
        o_scr[...] = jnp.zeros((ATT_ROWS, N_HEADS * HD), BF16)

    t = jnp.maximum(step - 1, 0)
    gate = mod_ref[pl.ds(_block_mod_row(t), 1), 2 * D:3 * D]
    mix = (o_scr[...].astype(F32) * _silu(g_ref[...].astype(F32))).astype(BF16)
    y = jnp.dot(mix, wout_ref[...], preferred_element_type=F32) * gate
    out_ref[...] = _layer_norm(ALPHA * z_ref[...] + y, lng_ref[...], lnb_ref[...])

    i = jnp.minimum(step, N_ATT_STEPS - 1) % N_QSTEP
    n_win = 3 * ATT_BLK
    n_keys = n_win + N_CTX
    nq = GRP * ATT_BLK
    k_win = jnp.concatenate([kp_ref[...], kc_ref[...], kn_ref[...]], axis=0)
    vt_win = jnp.concatenate([vp_ref[...], vc_ref[...], vn_ref[...]], axis=1)
    k_ctx = kx_ref[...]
    vt_ctx = vx_ref[...]

    kpos = lax.broadcasted_iota(jnp.int32, (ATT_BLK, ATT_BLK), 0)
    qpos = lax.broadcasted_iota(jnp.int32, (ATT_BLK, ATT_BLK), 1)
    tile_q = lambda a: jnp.concatenate([a] * GRP, axis=1)
    qgrp = lax.broadcasted_iota(jnp.int32, (1, nq), 1) // ATT_BLK
    ones = jnp.ones((ONES_ROWS, n_keys), BF16)
    q = q_ref[...]

    units = [(u, h) for u in range(ATT_SUB) for h in range(N_KV)]
    k_all, vt_all, bias = [], [], []
    for u in range(ATT_SUB):
        k_all.append(jnp.concatenate([k_win[u * ATT_BLK:u * ATT_BLK + n_win], k_ctx], axis=0))
        vt_all.append(jnp.concatenate([vt_win[:, u * ATT_BLK:u * ATT_BLK + n_win], vt_ctx], axis=1))
        has_prev = (i > 0) if u == 0 else True
        has_next = (i < N_QSTEP - 1) if u == ATT_SUB - 1 else True
        bias.append((tile_q(jnp.where((kpos >= qpos) & has_prev, 0.0, NEG_INF)),
                     tile_q(jnp.where((kpos <= qpos) & has_next, 0.0, NEG_INF))))
    scores = []
    for u, h in units:
        kh = k_all[u][:, h * HD:(h + 1) * HD]
        qh = jnp.concatenate(
            [q[u * ATT_BLK:(u + 1) * ATT_BLK, (h * GRP + g) * HD:(h * GRP + g + 1) * HD]
             for g in range(GRP)], axis=0)
        scores.append(lax.dot_general(kh, qh, (((1,), (1,)), ((), ())),
                                      preferred_element_type=F32))
    probs = []
    for (u, h), s in zip(units, scores):
        s = jnp.concatenate([s[:ATT_BLK] + bias[u][0], s[ATT_BLK:2 * ATT_BLK],
                             s[2 * ATT_BLK:n_win] + bias[u][1], s[n_win:]], axis=0)
        sink = jnp.zeros((1, nq), F32)
        for g in range(GRP):
            sink = jnp.where(qgrp == g, sink_ref[h * GRP + g] * LOG2E, sink)
        m = jnp.maximum(jnp.max(s, axis=0, keepdims=True), sink)
        probs.append((jnp.exp2(s - m).astype(BF16), jnp.exp2(sink - m)))
    outs = [[] for _ in range(ATT_SUB)]
    for (u, h), (p, p_sink) in zip(units, probs):
        vt1 = jnp.concatenate([vt_all[u][h * HD:(h + 1) * HD], ones], axis=0)
        ov = jnp.dot(vt1, p, preferred_element_type=F32)
        o_t = ov[:HD] / (ov[HD:HD + 1] + p_sink)
        outs[u].extend(o_t[:, g * ATT_BLK:(g + 1) * ATT_BLK] for g in range(GRP))
    for u in range(ATT_SUB):
        o_scr[u * ATT_BLK:(u + 1) * ATT_BLK, :] = jnp.transpose(
            jnp.concatenate(outs[u], axis=0)).astype(BF16)


def _attention_tail(sink, q, k, vt, g, z1, mod, w_out, ln_g, ln_b):
    dq = N_HEADS * HD
    dkv = N_KV * HD
    ctx_blk0 = R_LAT // N_CTX
    cur = lambda s: jnp.minimum(s, N_ATT_STEPS - 1)
    batch = lambda s: cur(s) // N_QSTEP
    qstep = lambda s: cur(s) % N_QSTEP
    prev_blk = lambda s: batch(s) * N_QBLK + jnp.maximum(ATT_SUB * qstep(s) - 1, 0)
    next_blk = lambda s: batch(s) * N_QBLK + jnp.minimum(ATT_SUB * (qstep(s) + 1), N_QBLK - 1)
    tail = lambda s: jnp.maximum(s - 1, 0)
    once = dict(pipeline_mode=pl.Buffered(1))
    row = lambda n: pl.BlockSpec((1, n), lambda s, sk: (0, 0))
    return pl.pallas_call(
        _attn_kernel,
        grid_spec=pltpu.PrefetchScalarGridSpec(
            num_scalar_prefetch=1,
            grid=(N_ATT_STEPS + 1,),
            in_specs=[
                pl.BlockSpec((ATT_ROWS, dq), lambda s, sk: (cur(s), 0)),
                pl.BlockSpec((ATT_BLK, dkv), lambda s, sk: (prev_blk(s), 0)),
                pl.BlockSpec((ATT_ROWS, dkv), lambda s, sk: (cur(s), 0)),
                pl.BlockSpec((ATT_BLK, dkv), lambda s, sk: (next_blk(s), 0)),
                pl.BlockSpec((N_CTX, dkv), lambda s, sk: (ctx_blk0 + batch(s), 0)),
                pl.BlockSpec((dkv, ATT_BLK), lambda s, sk: (0, prev_blk(s))),
                pl.BlockSpec((dkv, ATT_ROWS), lambda s, sk: (0, cur(s))),
                pl.BlockSpec((dkv, ATT_BLK), lambda s, sk: (0, next_blk(s))),
                pl.BlockSpec((dkv, N_CTX), lambda s, sk: (0, ctx_blk0 + batch(s))),
                pl.BlockSpec((ATT_ROWS, dq), lambda s, sk: (tail(s), 0)),
                pl.BlockSpec((ATT_ROWS, D), lambda s, sk: (tail(s), 0)),
                pl.BlockSpec((8, 3 * D), lambda s, sk: (0, 0)),
                pl.BlockSpec((dq, D), lambda s, sk: (0, 0), **once),
                row(D), row(D),
            ],
            out_specs=pl.BlockSpec((ATT_ROWS, D), lambda s, sk: (tail(s), 0)),
            scratch_shapes=[pltpu.VMEM((ATT_ROWS, dq), BF16)],
        ),
        out_shape=jax.ShapeDtypeStruct((R_LAT, D), F32),
        compiler_params=pltpu.CompilerParams(vmem_limit_bytes=VMEM_LIMIT),
        name="attention",
    )(sink, q, k, k, k, k, vt, vt, vt, vt, g, z1, mod, w_out, ln_g, ln_b)


def kernel(x, c, ctx, c_ctx, mod_w, mod_b, ln_g, ln_b, e_w_in, e_w_out, s5_lam_re, s5_lam_im,
           s5_log_dt, s5_b_re, s5_b_im, s5_c_re, s5_c_im, s5_d, glu_w, glu_b, sg_ln_g, sg_ln_b,
           sg_w, sg_b, o_w_in, o_w_out, o_sink):
    x2 = x.reshape(R_LAT, D)
    ctx2 = ctx.reshape(R_CTX, D)
    mod = _modulation(c, c_ctx, mod_w, mod_b)

    vp, rest = _in0(x2, ctx2, mod[0], e_w_in[0])
    t_mat, e_mat, ft_mat, al_re, al_im = _s5_prep(
        s5_lam_re[0], s5_lam_im[0], s5_log_dt[0], s5_b_re[0], s5_b_im[0], s5_c_re[0], s5_c_im[0],
        s5_d[0])
    ys_lat, ys_ctx = _s5_out(vp, al_re, al_im, e_mat, t_mat, ft_mat)
    sg_bias = jnp.repeat(jnp.transpose(sg_b[0]), SG_W // SG_HEADS, axis=1)
    sg_w2 = jnp.transpose(sg_w[0].reshape(SG_HEADS // 2, 2, SG_CHUNK, SG_CHUNK), (0, 2, 1, 3))
    sg_w2 = sg_w2.reshape(SG_HEADS // 2, SG_CHUNK, 2 * SG_CHUNK).astype(BF16)
    cos_t, sin_t = _rope_tables()
    z1, q, k, vt, g = _mid(
        x2, ctx2, mod[0], mod[1], ys_lat, ys_ctx, rest, glu_w[0].astype(BF16),
        glu_b[0].reshape(1, S5_W), sg_ln_g[0].reshape(1, SG_W), sg_ln_b[0].reshape(1, SG_W),
        sg_w2, sg_bias, e_w_out[0].astype(BF16), ln_g[0].reshape(1, D),
        ln_b[0].reshape(1, D), o_w_in[0], cos_t, sin_t)

    out = _attention_tail(o_sink[0], q, k, vt, g, z1, mod[1], o_w_out[0].astype(BF16),
                          ln_g[1].reshape(1, D), ln_b[1].reshape(1, D))
    return out.reshape(B, N_LAT, D)
```

```python
import functools
import math

import jax
import jax.numpy as jnp
from jax import lax
from jax.experimental import pallas as pl
from jax.experimental.pallas import tpu as pltpu

F32 = jnp.float32
BF16 = jnp.bfloat16

D = 1024
B = 2
N_LAT = 8192
N_CTX = 256
DEPTH = 2
GRID_W = 64
S5_W = 512
S5_GC = 16
S5_G = 32
S5_P = 64
SG_W = 512
SG_HEADS = 8
SG_CHUNK = 128
N_HEADS = 16
N_KV = 4
HD = 64
WINDOW = 128
ATT_BLK = 128
ROPE_BASE = 10000.0
NEG_INF = -1e30
LN_EPS = 1e-5
ALPHA = (2 * DEPTH) ** 0.25
LOG2E = math.log2(math.e)
EVEN_IN = 2 * S5_W + 3 * SG_W
ODD_IN = 2 * N_HEADS * HD + 2 * N_KV * HD

LANES = 128
VMEM_LIMIT = 56 * 1024 * 1024

R_LAT = B * N_LAT
R_CTX = B * N_CTX
R_ALL = R_LAT + R_CTX
TN = 512
N_LAT_BLK = R_LAT // TN
N_BLK = R_ALL // TN
LAT_BLK_PER_BATCH = N_LAT // TN
CTX_MOD_ROW = B

S5_CHUNK = 8
S5_QT = S5_W // LANES
S5_GPT = LANES // S5_GC
S5_ST = S5_GPT * S5_P
S5_PAIRS = S5_GPT // 2
PAIR_BLK = 2 * S5_GC
LAT_CHUNKS = N_LAT // S5_CHUNK
CTX_CHUNKS = N_CTX // S5_CHUNK
BATCH_CHUNKS = CTX_CHUNKS + LAT_CHUNKS
N_SEG = 16
SEG_CHUNKS = BATCH_CHUNKS // N_SEG
BWD_RESET_STEP = LAT_CHUNKS - (N_SEG - 1) * SEG_CHUNKS


def _sigmoid(x):
    return 1.0 / (1.0 + jnp.exp(-x))


def _silu(x):
    return x * _sigmoid(x)


def _gelu_tanh(x):
    return 0.5 * x * (1.0 + jnp.tanh(math.sqrt(2.0 / math.pi) * (x + 0.044715 * (x * x * x))))


def _layer_norm(x, g, b):
    mu = jnp.mean(x, axis=-1, keepdims=True)
    xc = x - mu
    var = jnp.mean(xc * xc, axis=-1, keepdims=True)
    return xc * lax.rsqrt(var + LN_EPS) * g + b


def _split_bf16(x):
    hi = x.astype(BF16)
    return hi, (x - hi.astype(F32)).astype(BF16)


def _dot_split(x, y_split, dims):
    x_hi, x_lo = _split_bf16(x)
    y_hi, y_lo = y_split
    dot = lambda a, b: lax.dot_general(a, b, dims, preferred_element_type=F32)
    return dot(x_hi, y_hi) + (dot(x_hi, y_lo) + dot(x_lo, y_hi))


def _block_mod_row(i):
    return jnp.where(i < N_LAT_BLK, i // LAT_BLK_PER_BATCH, CTX_MOD_ROW)


def _mod_kernel(cv_ref, w_ref, b_ref, o_ref):
    s = _silu(cv_ref[...])
    o_ref[0] = _dot_split(s, _split_bf16(w_ref[0]), (((1,), (0,)), ((), ()))) + b_ref[0]


def _modulation(c, c_ctx, mod_w, mod_b):
    cv = jnp.concatenate([c, c_ctx[None], jnp.zeros((8 - B - 1, D), F32)], axis=0)
    tn = 1024
    return pl.pallas_call(
        _mod_kernel,
        grid=(DEPTH, 3 * D // tn),
        in_specs=[
            pl.BlockSpec((8, D), lambda l, j: (0, 0)),
            pl.BlockSpec((1, D, tn), lambda l, j: (l, 0, j)),
            pl.BlockSpec((1, 1, tn), lambda l, j: (l, 0, j)),
        ],
        out_specs=pl.BlockSpec((1, 8, tn), lambda l, j: (l, 0, j)),
        out_shape=jax.ShapeDtypeStruct((DEPTH, 8, 3 * D), F32),
        compiler_params=pltpu.CompilerParams(vmem_limit_bytes=VMEM_LIMIT),
        name="modulation",
    )(cv, mod_w, mod_b.reshape(DEPTH, 1, 3 * D))


def _block_transpose4(tiles):
    tiles = list(tiles)
    blk = lax.broadcasted_iota(jnp.int32, tiles[0].shape, 1) // PAIR_BLK
    for dist in (2, 1):
        keep = (blk & dist) == 0
        for i in range(len(tiles)):
            if i & dist:
                continue
            lo, hi = tiles[i], tiles[i + dist]
            tiles[i] = jnp.where(keep, lo, pltpu.roll(hi, dist * PAIR_BLK, axis=1))
            tiles[i + dist] = jnp.where(keep, pltpu.roll(lo, LANES - dist * PAIR_BLK, axis=1), hi)
    return tiles


def _in0_kernel(x_ref, ctx_ref, mod_ref, w_ref, vp_ref, rest_ref, xa_scr, wbf_scr):
    L = S5_CHUNK
    i = pl.program_id(0)

    @pl.when(i == 0)
    def _():
        wbf_scr[...] = w_ref[...].astype(BF16)

    z = jnp.where(i < N_LAT_BLK, x_ref[...], ctx_ref[...])
    m = mod_ref[pl.ds(_block_mod_row(i), 1), :]
    h = z * (1.0 + m[:, D:2 * D]) + m[:, :D]
    p = jnp.dot(h.astype(BF16), wbf_scr[...], preferred_element_type=F32)
    rest_ref[...] = p[:, S5_W:].astype(BF16)
    for q in range(S5_QT):
        xa_scr[q] = p[:, q * LANES:(q + 1) * LANES]
        tiles = [xa_scr[q, pl.ds(t, TN // L, stride=L), :] for t in range(L)]
        halves = [_block_transpose4(tiles[k * S5_PAIRS:(k + 1) * S5_PAIRS])
                  for k in range(L // S5_PAIRS)]
        for pr in range(S5_PAIRS):
            vp_ref[q * S5_PAIRS + pr] = jnp.concatenate(
                [half[pr] for half in halves], axis=-1).astype(BF16)


def _in0(x2, ctx2, mod, w_in):
    pair_w = S5_CHUNK * PAIR_BLK
    return pl.pallas_call(
        _in0_kernel,
        grid=(N_BLK,),
        in_specs=[
            pl.BlockSpec((TN, D), lambda i: (jnp.minimum(i, N_LAT_BLK - 1), 0)),
            pl.BlockSpec((R_CTX, D), lambda i: (0, 0)),
            pl.BlockSpec((8, 3 * D), lambda i: (0, 0)),
            pl.BlockSpec((D, EVEN_IN), lambda i: (0, 0)),
        ],
        out_specs=[
            pl.BlockSpec((S5_QT * S5_PAIRS, TN // S5_CHUNK, pair_w), lambda i: (0, i, 0)),
            pl.BlockSpec((TN, EVEN_IN - S5_W), lambda i: (i, 0)),
        ],
        out_shape=[
            jax.ShapeDtypeStruct((S5_QT * S5_PAIRS, R_ALL // S5_CHUNK, pair_w), BF16),
            jax.ShapeDtypeStruct((R_ALL, EVEN_IN - S5_W), BF16),
        ],
        scratch_shapes=[pltpu.VMEM((S5_QT, TN, LANES), F32), pltpu.VMEM((D, EVEN_IN), BF16)],
        compiler_params=pltpu.CompilerParams(vmem_limit_bytes=VMEM_LIMIT),
        name="in0",
    )(x2, ctx2, mod, w_in)


def _s5_prep_kernel(rows_ref, bre_ref, bim_ref, cre_ref, cim_ref, dsk_ref, t_ref, e_ref, ft_ref,
                    alr_ref, ali_ref):
    L = S5_CHUNK
    tile = (LANES, S5_ST)
    same_group = (lax.broadcasted_iota(jnp.int32, tile, 0) // S5_GC
                  == lax.broadcasted_iota(jnp.int32, tile, 1) // S5_P)
    nt_dims = (((1,), (1,)), ((), ()))
    nn_dims = (((1,), (0,)), ((), ()))
    pick_ch = (lax.broadcasted_iota(jnp.int32, (LANES, S5_GC), 0) % S5_GC
               == lax.broadcasted_iota(jnp.int32, (LANES, S5_GC), 1)).astype(BF16)
    rep_st = (lax.broadcasted_iota(jnp.int32, (S5_P, S5_ST), 0)
              == lax.broadcasted_iota(jnp.int32, (S5_P, S5_ST), 1) % S5_P).astype(BF16)

    def spread_b(b):
        hi, lo = _split_bf16(b)
        dot = lambda v: lax.dot_general(pick_ch, v, nt_dims, preferred_element_type=F32)
        return jnp.where(same_group, dot(hi) + dot(lo), 0.0)

    def spread_c(c):
        hi, lo = _split_bf16(c)
        dot = lambda v: lax.dot_general(v, rep_st, nn_dims, preferred_element_type=F32)
        return jnp.where(same_group, dot(hi) + dot(lo), 0.0)

    pair_rows = lambda pr: slice(pr * PAIR_BLK, (pr + 1) * PAIR_BLK)
    pair_lanes = lambda pr: slice(pr * LANES, (pr + 1) * LANES)
    lag_strip = []
    for d in range(2):
        lam_re = rows_ref[d, 0]
        lam_im = rows_ref[d, 1]
        dt = jnp.exp(rows_ref[d, 2])
        pw = []
        for k in range(L + 1):
            mag = jnp.exp(lam_re * dt * float(k))
            ang = lam_im * dt * float(k)
            pw.append((mag * jnp.cos(ang), mag * jnp.sin(ang)))
        alr_ref[d] = pw[L][0]
        ali_ref[d] = pw[L][1]
        den = lam_re * lam_re + lam_im * lam_im
        nr = pw[1][0] - 1.0
        ni = pw[1][1]
        f_re = (nr * lam_re + ni * lam_im) / den
        f_im = (ni * lam_re - nr * lam_im) / den
        b_re = spread_b(bre_ref[d])
        b_im = spread_b(bim_ref[d])
        bb_re = f_re * b_re - f_im * b_im
        bb_im = f_re * b_im + f_im * b_re
        c_re = spread_c(cre_ref[d])
        c_im = spread_c(cim_ref[d])
        for k in range(L):
            x_re = pw[k][0] * bb_re - pw[k][1] * bb_im
            x_im = pw[k][0] * bb_im + pw[k][1] * bb_re
            s = (L - 1 - k) if d == 0 else k
            for pr in range(S5_PAIRS):
                blk = slice(s * PAIR_BLK, (s + 1) * PAIR_BLK)
                e_ref[0, pr, blk, 2 * d * LANES:(2 * d + 1) * LANES] = (
                    x_re[pair_rows(pr), pair_lanes(pr)].astype(BF16))
                e_ref[0, pr, blk, (2 * d + 1) * LANES:(2 * d + 2) * LANES] = (
                    x_im[pair_rows(pr), pair_lanes(pr)].astype(BF16))
        cw = []
        for k in range(L + 1):
            w_re = c_re * pw[k][0] - c_im * pw[k][1]
            w_im = c_re * pw[k][1] + c_im * pw[k][0]
            cw.append((w_re, -w_im))
        for t in range(L):
            k = (t + 1) if d == 0 else (L - t)
            for pr in range(S5_PAIRS):
                blk = slice(t * PAIR_BLK, (t + 1) * PAIR_BLK)
                ft_ref[0, pr, blk, 2 * d * LANES:(2 * d + 1) * LANES] = (
                    cw[k][0][pair_rows(pr), pair_lanes(pr)].astype(BF16))
                ft_ref[0, pr, blk, (2 * d + 1) * LANES:(2 * d + 2) * LANES] = (
                    cw[k][1][pair_rows(pr), pair_lanes(pr)].astype(BF16))
        lags = range(L) if d == 0 else range(L - 1, -1, -1)
        c_stack = jnp.concatenate(
            [jnp.concatenate([cw[k][0][pair_rows(pr)], cw[k][1][pair_rows(pr)]], axis=-1)
             for pr in range(S5_PAIRS) for k in lags], axis=0)
        lag_strip.append(_dot_split(jnp.concatenate([bb_re, bb_im], axis=-1),
                                    _split_bf16(c_stack), nt_dims))

    strip_w = 2 * L * PAIR_BLK
    mid = (L - 1) * PAIR_BLK
    widen = lambda a: jnp.concatenate([a, jnp.zeros((a.shape[0], strip_w - a.shape[1]), F32)], axis=-1)
    row = lax.broadcasted_iota(jnp.int32, (PAIR_BLK, strip_w), 0)
    lane = lax.broadcasted_iota(jnp.int32, (PAIR_BLK, strip_w), 1)
    d_row = widen(dsk_ref[0])
    for pr in range(S5_PAIRS):
        lanes = slice(pr * L * PAIR_BLK, (pr + 1) * L * PAIR_BLK)
        strip = (widen(lag_strip[1][pair_rows(pr), lanes])
                 + pltpu.roll(widen(lag_strip[0][pair_rows(pr), lanes]), mid, axis=1)
                 + jnp.where(lane == row + mid,
                             pltpu.roll(d_row, mid - pr * PAIR_BLK, axis=1), 0.0))
        for s in range(L):
            shift = (strip_w - (L - 1 - s) * PAIR_BLK) % strip_w
            blk = pltpu.roll(strip, shift, axis=1) if shift else strip
            t_ref[0, pr, s * PAIR_BLK:(s + 1) * PAIR_BLK, :] = blk[:, :L * PAIR_BLK].astype(BF16)


def _s5_prep(lam_re, lam_im, log_dt, b_re, b_im, c_re, c_im, d_skip):
    L = S5_CHUNK
    n_st = S5_G * S5_P
    pair_w = L * PAIR_BLK
    rows = jnp.stack([lam_re.reshape(2, 1, n_st), lam_im.reshape(2, 1, n_st),
                      jnp.repeat(log_dt, S5_P, axis=1).reshape(2, 1, n_st)], axis=1)
    b_spec = pl.BlockSpec((2, S5_ST, S5_GC), lambda q: (0, q, 0))
    c_spec = pl.BlockSpec((2, LANES, S5_P), lambda q: (0, q, 0))
    return pl.pallas_call(
        _s5_prep_kernel,
        grid=(S5_QT,),
        in_specs=[
            pl.BlockSpec((2, 3, 1, S5_ST), lambda q: (0, 0, 0, q)),
            b_spec, b_spec, c_spec, c_spec,
            pl.BlockSpec((1, 1, LANES), lambda q: (q, 0, 0)),
        ],
        out_specs=[
            pl.BlockSpec((1, S5_PAIRS, pair_w, pair_w), lambda q: (q, 0, 0, 0)),
            pl.BlockSpec((1, S5_PAIRS, pair_w, 4 * LANES), lambda q: (q, 0, 0, 0)),
            pl.BlockSpec((1, S5_PAIRS, pair_w, 4 * LANES), lambda q: (q, 0, 0, 0)),
            pl.BlockSpec((2, 1, S5_ST), lambda q: (q, 0, 0)),
            pl.BlockSpec((2, 1, S5_ST), lambda q: (q, 0, 0)),
        ],
        out_shape=[
            jax.ShapeDtypeStruct((S5_QT, S5_PAIRS, pair_w, pair_w), BF16),
            jax.ShapeDtypeStruct((S5_QT, S5_PAIRS, pair_w, 4 * LANES), BF16),
            jax.ShapeDtypeStruct((S5_QT, S5_PAIRS, pair_w, 4 * LANES), BF16),
            jax.ShapeDtypeStruct((2 * S5_QT, 1, S5_ST), F32),
            jax.ShapeDtypeStruct((2 * S5_QT, 1, S5_ST), F32),
        ],
        compiler_params=pltpu.CompilerParams(vmem_limit_bytes=VMEM_LIMIT),
        name="s5_prep",
    )(rows, b_re.reshape(2, n_st, S5_GC), b_im.reshape(2, n_st, S5_GC),
      c_re.reshape(2, S5_W, S5_P), c_im.reshape(2, S5_W, S5_P), d_skip.reshape(S5_QT, 1, LANES))


N_SLAB = S5_ST // LANES


def _cmul(ar, ai, br, bi):
    return ar * br - ai * bi, ar * bi + ai * br


def _cpow(ar, ai, n):
    res = None
    while n:
        if n & 1:
            res = (ar, ai) if res is None else _cmul(res[0], res[1], ar, ai)
        n >>= 1
        if n:
            ar, ai = _cmul(ar, ai, ar, ai)
    return res


def _scan_boundary_states(loc_ref, out_ref, ar_ref, ai_ref, backward):
    a_row = [(ar_ref[0][:, k * LANES:(k + 1) * LANES], ai_ref[0][:, k * LANES:(k + 1) * LANES])
             for k in range(N_SLAB)]
    a_tile = [(jnp.broadcast_to(r, (N_SEG, LANES)), jnp.broadcast_to(i, (N_SEG, LANES)))
              for r, i in a_row]

    def sweep(lo, hi, carry, store):
        def step(i, carry):
            off = (SEG_CHUNKS - 1 - i) if backward else i
            rows = pl.ds(off, N_SEG, stride=SEG_CHUNKS)
            new = []
            for k, (cr, ci) in enumerate(carry):
                xr = loc_ref[k, rows, :]
                xi = loc_ref[N_SLAB + k, rows, :]
                if store:
                    out_ref[k, rows, :] = cr
                    out_ref[N_SLAB + k, rows, :] = ci
                nr, ni = _cmul(a_tile[k][0], a_tile[k][1], cr, ci)
                new.append((nr + xr, ni + xi))
            return tuple(new)
        return lax.fori_loop(lo, hi, step, carry)

    def full(carry, store):
        if not backward:
            return sweep(0, SEG_CHUNKS, carry, store)
        carry = sweep(0, BWD_RESET_STEP, carry, store)
        first = lax.broadcasted_iota(jnp.int32, (N_SEG, LANES), 0) == 0
        carry = tuple((jnp.where(first, 0.0, cr), jnp.where(first, 0.0, ci)) for cr, ci in carry)
        return sweep(BWD_RESET_STEP, SEG_CHUNKS, carry, store)

    zero_tile = jnp.zeros((N_SEG, LANES), F32)
    ends = full(tuple((zero_tile, zero_tile) for _ in range(N_SLAB)), False)

    starts = []
    for k, (er, ei) in enumerate(ends):
        pr, pi = _cpow(a_row[k][0], a_row[k][1], SEG_CHUNKS)
        if backward:
            order = range(N_SEG - 1, -1, -1)
            g = (er[0:1], ei[0:1])
        else:
            order = range(N_SEG)
            g = (jnp.zeros((1, LANES), F32), jnp.zeros((1, LANES), F32))
        rows_r = [None] * N_SEG
        rows_i = [None] * N_SEG
        for j in order:
            rows_r[j], rows_i[j] = g
            nr, ni = _cmul(pr, pi, g[0], g[1])
            g = (nr + er[j:j + 1], ni + ei[j:j + 1])
        starts.append((jnp.concatenate(rows_r, axis=0), jnp.concatenate(rows_i, axis=0)))
    full(tuple(starts), True)


def _s5_out_kernel(vl_ref, vc_ref, ar_ref, ai_ref, e_ref, t_ref, ft_ref, yl_ref, yc_ref,
                   acc_ref, st_ref):
    L = S5_CHUNK
    d = pl.program_id(2)
    rows = [jnp.concatenate([vc_ref[pr], vl_ref[pr]], axis=0) for pr in range(S5_PAIRS)]
    for pr in range(S5_PAIRS):
        st = jnp.dot(rows[pr], e_ref[0, pr], preferred_element_type=F32)
        st_ref[pr] = st[:, :LANES]
        st_ref[N_SLAB + pr] = st[:, LANES:]

    def from_states(pr):
        st = jnp.concatenate([st_ref[pr], st_ref[N_SLAB + pr]], axis=-1).astype(BF16)
        return lax.dot_general(st, ft_ref[0, pr], (((1,), (1,)), ((), ())),
                               preferred_element_type=F32)

    @pl.when(d == 0)
    def _():
        _scan_boundary_states(st_ref, st_ref, ar_ref, ai_ref, False)
        for pr in range(S5_PAIRS):
            acc_ref[pr] = from_states(pr) + jnp.dot(rows[pr], t_ref[0, pr],
                                                    preferred_element_type=F32)

    @pl.when(d == 1)
    def _():
        _scan_boundary_states(st_ref, st_ref, ar_ref, ai_ref, True)
        ys = [acc_ref[pr] + from_states(pr) for pr in range(S5_PAIRS)]
        tiles = [tile for k in range(L // S5_PAIRS) for tile in _block_transpose4(
            [y[:, k * LANES:(k + 1) * LANES] for y in ys])]
        for t, tile in enumerate(tiles):
            yc_ref[pl.ds(t, CTX_CHUNKS, stride=L), :] = tile[:CTX_CHUNKS]
            yl_ref[pl.ds(t, LAT_CHUNKS, stride=L), :] = tile[CTX_CHUNKS:]


def _s5_out(vp, al_re, al_im, e_mat, t_mat, ft_mat):
    pair_w = S5_CHUNK * PAIR_BLK
    ctx_blk0 = (R_LAT // S5_CHUNK) // CTX_CHUNKS
    return pl.pallas_call(
        _s5_out_kernel,
        grid=(S5_QT, B, 2),
        in_specs=[
            pl.BlockSpec((S5_PAIRS, LAT_CHUNKS, pair_w), lambda q, b, d: (q, b, 0)),
            pl.BlockSpec((S5_PAIRS, CTX_CHUNKS, pair_w), lambda q, b, d: (q, ctx_blk0 + b, 0)),
            pl.BlockSpec((1, 1, S5_ST), lambda q, b, d: (2 * q + d, 0, 0)),
            pl.BlockSpec((1, 1, S5_ST), lambda q, b, d: (2 * q + d, 0, 0)),
            pl.BlockSpec((1, S5_PAIRS, pair_w, 2 * LANES), lambda q, b, d: (q, 0, 0, d)),
            pl.BlockSpec((1, S5_PAIRS, pair_w, pair_w), lambda q, b, d: (q, 0, 0, 0)),
            pl.BlockSpec((1, S5_PAIRS, pair_w, 2 * LANES), lambda q, b, d: (q, 0, 0, d)),
        ],
        out_specs=[
            pl.BlockSpec((N_LAT, LANES), lambda q, b, d: (b, q)),
            pl.BlockSpec((N_CTX, LANES), lambda q, b, d: (b, q)),
        ],
        out_shape=[
            jax.ShapeDtypeStruct((R_LAT, S5_W), F32),
            jax.ShapeDtypeStruct((R_CTX, S5_W), F32),
        ],
        scratch_shapes=[pltpu.VMEM((S5_PAIRS, BATCH_CHUNKS, pair_w), F32),
                        pltpu.VMEM((2 * N_SLAB, BATCH_CHUNKS, LANES), F32)],
        compiler_params=pltpu.CompilerParams(vmem_limit_bytes=VMEM_LIMIT),
        name="s5_out",
    )(vp, vp, al_re, al_im, e_mat, t_mat, ft_mat)


def _tail0_stages(z, ys, rest_ref, gate, gluw_ref, glub_ref, sg_g_ref, sg_b_ref, sgw_ref,
                  sgbias_ref, wout_ref, lng_ref, lnb_ref):
    ga = rest_ref[:, 0:S5_W].astype(F32)
    u = rest_ref[:, S5_W:2 * S5_W].astype(F32)
    v = rest_ref[:, 2 * S5_W:3 * S5_W].astype(F32)
    gb = rest_ref[:, 3 * S5_W:4 * S5_W].astype(F32)

    ya = _gelu_tanh(ys)
    glu = jnp.dot(ya.astype(BF16), gluw_ref[...], preferred_element_type=F32) + glub_ref[...]
    yield None
    ya = ya * _sigmoid(glu) * _silu(ga)
    yield None

    vn = _layer_norm(v, sg_g_ref[...], sg_b_ref[...]).astype(BF16)
    lane = lax.broadcasted_iota(jnp.int32, (SG_CHUNK, LANES), 1)
    first_head = lane < (SG_W // SG_HEADS)
    chunks = []
    for c in range(TN // SG_CHUNK):
        tiles = []
        for j in range(SG_W // LANES):
            vt = vn[c * SG_CHUNK:(c + 1) * SG_CHUNK, j * LANES:(j + 1) * LANES]
            zero = jnp.zeros_like(vt)
            v2 = jnp.concatenate([jnp.where(first_head, vt, zero), jnp.where(first_head, zero, vt)],
                                 axis=0)
            tiles.append(jnp.dot(sgw_ref[j], v2, preferred_element_type=F32))
        chunks.append(jnp.concatenate(tiles, axis=-1) + sgbias_ref[...])
    s = jnp.concatenate(chunks, axis=0)
    yb = u * s * _silu(gb)
    yield None

    mix = jnp.concatenate([ya, yb], axis=-1).astype(BF16)
    y = jnp.dot(mix, wout_ref[...], preferred_element_type=F32) * gate
    yield None
    yield _layer_norm(ALPHA * z + y, lng_ref[...], lnb_ref[...])


def _rope_tables():
    nf = HD // 4
    n_rows = N_LAT // GRID_W
    lane = jnp.arange(LANES)
    inv = ROPE_BASE ** (-(lane % nf).astype(F32) / nf)
    by_row = ((lane % HD) // (HD // 2) == 0)[None, :]
    sign = jnp.where((lane % (HD // 2)) < nf, -1.0, 1.0)[None, :]
    row_ang = jnp.arange(n_rows, dtype=F32)[:, None] * inv[None, :]
    col_ang = jnp.arange(GRID_W, dtype=F32)[:, None] * inv[None, :]
    zero = jnp.zeros((), F32)
    row_tab = jnp.stack([jnp.where(by_row, jnp.cos(row_ang), zero),
                         jnp.where(by_row, sign * jnp.sin(row_ang), zero)])
    col_tab = jnp.stack([jnp.where(by_row, zero, jnp.cos(col_ang)),
                         jnp.where(by_row, zero, sign * jnp.sin(col_ang))])
    return row_tab, col_tab


def _rope_tile(x, cos, sin, first_half):
    nf = HD // 4
    partner = jnp.where(first_half, pltpu.roll(x, LANES - nf, axis=1), pltpu.roll(x, nf, axis=1))
    return x * cos + partner * sin


def _mid_kernel(x_ref, ctx_ref, mod0_ref, mod1_ref, ysl_ref, ysc_ref, rest_ref, gluw_ref, glub_ref,
                sg_g_ref, sg_b_ref, sgw_ref, sgbias_ref, wout_ref, lng_ref, lnb_ref, w_ref,
                rtab_ref, ctab_ref, z1_ref, q_ref, k_ref, vt_ref, g_ref, zprev_ref, wbf_scr,
                wout_scr):
    i = pl.program_id(0)
    dq = N_HEADS * HD
    dkv = N_KV * HD

    @pl.when(i == 0)
    def _():
        zprev_ref[...] = jnp.zeros((TN, D), F32)
        wbf_scr[...] = w_ref[...].astype(BF16)
        wout_scr[...] = wout_ref[...].astype(BF16)

    j = jnp.maximum(i - 1, 0)
    m1 = mod1_ref[pl.ds(_block_mod_row(j), 1), :]
    h = (zprev_ref[...] * (1.0 + m1[:, D:2 * D]) + m1[:, :D]).astype(BF16)
    grid_row0 = (j % LAT_BLK_PER_BATCH) * (TN // GRID_W)
    table = lambda t: jnp.concatenate(
        [rtab_ref[t, pl.ds(grid_row0 + r, 1), :] + ctab_ref[t] for r in range(TN // GRID_W)],
        axis=0)
    cos = jnp.where(j < N_LAT_BLK, table(0), 1.0)
    sin = jnp.where(j < N_LAT_BLK, table(1), 0.0)
    lane = lax.broadcasted_iota(jnp.int32, (TN, LANES), 1)
    first_half = (lane % (HD // 2)) < (HD // 4)
    scale = HD ** -0.5 * LOG2E

    def project(c0, c1):
        return jnp.dot(h, wbf_scr[:, c0:c1], preferred_element_type=F32)

    def roped_tiles(p, mult):
        for c in range(p.shape[1] // LANES):
            r = _rope_tile(p[:, c * LANES:(c + 1) * LANES], cos, sin, first_half)
            yield c, (r * mult if mult != 1.0 else r).astype(BF16)

    t = jnp.minimum(i, N_BLK - 1)
    z = jnp.where(t < N_LAT_BLK, x_ref[...], ctx_ref[...])
    ys = jnp.where(t < N_LAT_BLK, ysl_ref[...], ysc_ref[...])
    gate = mod0_ref[pl.ds(_block_mod_row(t), 1), 2 * D:3 * D]
    tail = _tail0_stages(z, ys, rest_ref, gate, gluw_ref, glub_ref, sg_g_ref, sg_b_ref, sgw_ref,
                         sgbias_ref, wout_scr, lng_ref, lnb_ref)

    half_q = dq // 2
    p_q0 = project(0, half_q)
    next(tail)
    p_q1 = project(half_q, dq)
    next(tail)
    p_k = project(dq, dq + dkv)
    vt_ref[...] = lax.dot_general(wbf_scr[:, dq + dkv:dq + 2 * dkv], h, (((0,), (1,)), ((), ())),
                                  preferred_element_type=F32).astype(BF16)
    next(tail)
    next(tail)
    for c0, p in ((0, p_q0), (half_q, p_q1)):
        for c, r in roped_tiles(p, scale):
            q_ref[:, c0 + c * LANES:c0 + (c + 1) * LANES] = r
    for c, r in roped_tiles(p_k, 1.0):
        k_ref[:, c * LANES:(c + 1) * LANES] = r
    p_g = project(dq + 2 * dkv, ODD_IN)
    z1 = next(tail)
    z1_ref[...] = z1
    zprev_ref[...] = z1
    g_ref[...] = p_g.astype(BF16)


def _mid(x2, ctx2, mod0, mod1, ys_lat, ys_ctx, rest, glu_w, glu_b, sg_g, sg_b, sg_w, sg_bias,
         w_out, ln_g, ln_b, w_in1, row_tab, col_tab):
    dq = N_HEADS * HD
    dkv = N_KV * HD
    once = dict(pipeline_mode=pl.Buffered(1))
    row = lambda n: pl.BlockSpec((1, n), lambda i: (0, 0))
    lat = lambda i: (jnp.minimum(i, N_LAT_BLK - 1), 0)
    tail_blk = lambda i: (jnp.minimum(i, N_BLK - 1), 0)
    proj_blk = lambda i: (jnp.maximum(i - 1, 0), 0)
    return pl.pallas_call(
        _mid_kernel,
        grid=(N_BLK + 1,),
        in_specs=[
            pl.BlockSpec((TN, D), lat),
            pl.BlockSpec((R_CTX, D), lambda i: (0, 0), **once),
            pl.BlockSpec((8, 3 * D), lambda i: (0, 0)),
            pl.BlockSpec((8, 3 * D), lambda i: (0, 0)),
            pl.BlockSpec((TN, S5_W), lat),
            pl.BlockSpec((R_CTX, S5_W), lambda i: (0, 0), **once),
            pl.BlockSpec((TN, EVEN_IN - S5_W), tail_blk),
            pl.BlockSpec((S5_W, S5_W), lambda i: (0, 0), **once),
            row(S5_W), row(SG_W), row(SG_W),
            pl.BlockSpec((SG_HEADS // 2, SG_CHUNK, 2 * SG_CHUNK), lambda i: (0, 0, 0)),
            pl.BlockSpec((SG_CHUNK, SG_W), lambda i: (0, 0)),
            pl.BlockSpec((S5_W + SG_W, D), lambda i: (0, 0), **once),
            row(D), row(D),
            pl.BlockSpec((D, ODD_IN), lambda i: (0, 0), **once),
            pl.BlockSpec((2, N_LAT // GRID_W, LANES), lambda i: (0, 0, 0)),
            pl.BlockSpec((2, GRID_W, LANES), lambda i: (0, 0, 0)),
        ],
        out_specs=[
            pl.BlockSpec((TN, D), tail_blk),
            pl.BlockSpec((TN, dq), proj_blk),
            pl.BlockSpec((TN, dkv), proj_blk),
            pl.BlockSpec((dkv, TN), lambda i: (0, jnp.maximum(i - 1, 0))),
            pl.BlockSpec((TN, dq), proj_blk),
        ],
        out_shape=[
            jax.ShapeDtypeStruct((R_ALL, D), F32),
            jax.ShapeDtypeStruct((R_ALL, dq), BF16),
            jax.ShapeDtypeStruct((R_ALL, dkv), BF16),
            jax.ShapeDtypeStruct((dkv, R_ALL), BF16),
            jax.ShapeDtypeStruct((R_ALL, dq), BF16),
        ],
        scratch_shapes=[pltpu.VMEM((TN, D), F32), pltpu.VMEM((D, ODD_IN), BF16),
                        pltpu.VMEM((S5_W + SG_W, D), BF16)],
        compiler_params=pltpu.CompilerParams(vmem_limit_bytes=VMEM_LIMIT),
        name="mid",
    )(x2, ctx2, mod0, mod1, ys_lat, ys_ctx, rest, glu_w, glu_b, sg_g, sg_b, sg_w, sg_bias, w_out,
      ln_g, ln_b, w_in1, row_tab, col_tab)


N_QBLK = N_LAT // ATT_BLK
GRP = N_HEADS // N_KV
ATT_SUB = 4
ATT_ROWS = ATT_SUB * ATT_BLK
N_QSTEP = N_QBLK // ATT_SUB
N_ATT_STEPS = B * N_QSTEP
ATT_STAGE_UNITS = ATT_SUB * N_KV
assert ATT_ROWS == TN
ONES_ROWS = 16


def _attn_kernel(sink_ref, q_ref, kp_ref, kc_ref, kn_ref, kx_ref, vp_ref, vc_ref, vn_ref, vx_ref,
                 g_ref, z_ref, mod_ref, wout_ref, lng_ref, lnb_ref, out_ref, o_scr, wout_scr):
    step = pl.program_id(0)

    @pl.when(step == 0)
    def _():
        o_scr[...] = jnp.zeros((ATT_ROWS, N_HEADS * HD), BF16)
        wout_scr[...] = wout_ref[...].astype(BF16)

    t = jnp.maximum(step - 1, 0)
    gate = mod_ref[pl.ds(_block_mod_row(t), 1), 2 * D:3 * D]
    mix = (o_scr[...].astype(F32) * _silu(g_ref[...].astype(F32))).astype(BF16)
    y = jnp.dot(mix, wout_scr[...], preferred_element_type=F32) * gate
    out_ref[...] = _layer_norm(ALPHA * z_ref[...] + y, lng_ref[...], lnb_ref[...])

    i = jnp.minimum(step, N_ATT_STEPS - 1) % N_QSTEP
    n_win = 3 * ATT_BLK
    n_keys = n_win + N_CTX
    nq = GRP * ATT_BLK
    k_win = jnp.concatenate([kp_ref[...], kc_ref[...], kn_ref[...]], axis=0)
    vt_win = jnp.concatenate([vp_ref[...], vc_ref[...], vn_ref[...]], axis=1)
    k_ctx = kx_ref[...]
    vt_ctx = vx_ref[...]

    kpos = lax.broadcasted_iota(jnp.int32, (ATT_BLK, ATT_BLK), 0)
    qpos = lax.broadcasted_iota(jnp.int32, (ATT_BLK, ATT_BLK), 1)
    tile_q = lambda a: jnp.concatenate([a] * GRP, axis=1)
    qgrp = lax.broadcasted_iota(jnp.int32, (1, nq), 1) // ATT_BLK
    ones = jnp.ones((ONES_ROWS, n_keys), BF16)
    q = q_ref[...]

    units = [(u, h) for u in range(ATT_SUB) for h in range(N_KV)]
    k_all, vt_all, bias = [], [], []
    for u in range(ATT_SUB):
        k_all.append(jnp.concatenate([k_win[u * ATT_BLK:u * ATT_BLK + n_win], k_ctx], axis=0))
        vt_all.append(jnp.concatenate([vt_win[:, u * ATT_BLK:u * ATT_BLK + n_win], vt_ctx], axis=1))
        has_prev = (i > 0) if u == 0 else True
        has_next = (i < N_QSTEP - 1) if u == ATT_SUB - 1 else True
        bias.append((tile_q(jnp.where((kpos >= qpos) & has_prev, 0.0, NEG_INF)),
                     tile_q(jnp.where((kpos <= qpos) & has_next, 0.0, NEG_INF))))
    outs = [[] for _ in range(ATT_SUB)]
    for first in range(0, len(units), ATT_STAGE_UNITS):
        group = units[first:first + ATT_STAGE_UNITS]
        scores = []
        for u, h in group:
            kh = k_all[u][:, h * HD:(h + 1) * HD]
            qh = jnp.concatenate(
                [q[u * ATT_BLK:(u + 1) * ATT_BLK, (h * GRP + g) * HD:(h * GRP + g + 1) * HD]
                 for g in range(GRP)], axis=0)
            scores.append(lax.dot_general(kh, qh, (((1,), (1,)), ((), ())),
                                          preferred_element_type=F32))
        probs = []
        for (u, h), s in zip(group, scores):
            s = jnp.concatenate([s[:ATT_BLK] + bias[u][0], s[ATT_BLK:2 * ATT_BLK],
                                 s[2 * ATT_BLK:n_win] + bias[u][1], s[n_win:]], axis=0)
            sink = jnp.zeros((1, nq), F32)
            for g in range(GRP):
                sink = jnp.where(qgrp == g, sink_ref[h * GRP + g] * LOG2E, sink)
            m = jnp.maximum(jnp.max(s, axis=0, keepdims=True), sink)
            probs.append((jnp.exp2(s - m).astype(BF16), jnp.exp2(sink - m)))
        for (u, h), (p, p_sink) in zip(group, probs):
            vt1 = jnp.concatenate([vt_all[u][h * HD:(h + 1) * HD], ones], axis=0)
            ov = jnp.dot(vt1, p, preferred_element_type=F32)
            o_t = ov[:HD] / (ov[HD:HD + 1] + p_sink)
            outs[u].extend(o_t[:, g * ATT_BLK:(g + 1) * ATT_BLK] for g in range(GRP))
    for u in range(ATT_SUB):
        o_scr[u * ATT_BLK:(u + 1) * ATT_BLK, :] = jnp.transpose(
            jnp.concatenate(outs[u], axis=0)).astype(BF16)


def _attention_tail(sink, q, k, vt, g, z1, mod, w_out, ln_g, ln_b):
    dq = N_HEADS * HD
    dkv = N_KV * HD
    ctx_blk0 = R_LAT // N_CTX
    cur = lambda s: jnp.minimum(s, N_ATT_STEPS - 1)
    batch = lambda s: cur(s) // N_QSTEP
    qstep = lambda s: cur(s) % N_QSTEP
    prev_blk = lambda s: batch(s) * N_QBLK + jnp.maximum(ATT_SUB * qstep(s) - 1, 0)
    next_blk = lambda s: batch(s) * N_QBLK + jnp.minimum(ATT_SUB * (qstep(s) + 1), N_QBLK - 1)
    tail = lambda s: jnp.maximum(s - 1, 0)
    once = dict(pipeline_mode=pl.Buffered(1))
    row = lambda n: pl.BlockSpec((1, n), lambda s, sk: (0, 0))
    return pl.pallas_call(
        _attn_kernel,
        grid_spec=pltpu.PrefetchScalarGridSpec(
            num_scalar_prefetch=1,
            grid=(N_ATT_STEPS + 1,),
            in_specs=[
                pl.BlockSpec((ATT_ROWS, dq), lambda s, sk: (cur(s), 0)),
                pl.BlockSpec((ATT_BLK, dkv), lambda s, sk: (prev_blk(s), 0)),
                pl.BlockSpec((ATT_ROWS, dkv), lambda s, sk: (cur(s), 0)),
                pl.BlockSpec((ATT_BLK, dkv), lambda s, sk: (next_blk(s), 0)),
                pl.BlockSpec((N_CTX, dkv), lambda s, sk: (ctx_blk0 + batch(s), 0)),
                pl.BlockSpec((dkv, ATT_BLK), lambda s, sk: (0, prev_blk(s))),
                pl.BlockSpec((dkv, ATT_ROWS), lambda s, sk: (0, cur(s))),
                pl.BlockSpec((dkv, ATT_BLK), lambda s, sk: (0, next_blk(s))),
                pl.BlockSpec((dkv, N_CTX), lambda s, sk: (0, ctx_blk0 + batch(s))),
                pl.BlockSpec((ATT_ROWS, dq), lambda s, sk: (tail(s), 0)),
                pl.BlockSpec((ATT_ROWS, D), lambda s, sk: (tail(s), 0)),
                pl.BlockSpec((8, 3 * D), lambda s, sk: (0, 0)),
                pl.BlockSpec((dq, D), lambda s, sk: (0, 0), **once),
                row(D), row(D),
            ],
            out_specs=pl.BlockSpec((ATT_ROWS, D), lambda s, sk: (tail(s), 0)),
            scratch_shapes=[pltpu.VMEM((ATT_ROWS, dq), BF16), pltpu.VMEM((dq, D), BF16)],
        ),
        out_shape=jax.ShapeDtypeStruct((R_LAT, D), F32),
        compiler_params=pltpu.CompilerParams(vmem_limit_bytes=VMEM_LIMIT),
        name="attention",
    )(sink, q, k, k, k, k, vt, vt, vt, vt, g, z1, mod, w_out, ln_g, ln_b)


def kernel(x, c, ctx, c_ctx, mod_w, mod_b, ln_g, ln_b, e_w_in, e_w_out, s5_lam_re, s5_lam_im,
           s5_log_dt, s5_b_re, s5_b_im, s5_c_re, s5_c_im, s5_d, glu_w, glu_b, sg_ln_g, sg_ln_b,
           sg_w, sg_b, o_w_in, o_w_out, o_sink):
    x2 = x.reshape(R_LAT, D)
    ctx2 = ctx.reshape(R_CTX, D)
    mod = _modulation(c, c_ctx, mod_w, mod_b)

    vp, rest = _in0(x2, ctx2, mod[0], e_w_in[0])
    t_mat, e_mat, ft_mat, al_re, al_im = _s5_prep(
        s5_lam_re[0], s5_lam_im[0], s5_log_dt[0], s5_b_re[0], s5_b_im[0], s5_c_re[0], s5_c_im[0],
        s5_d[0])
    ys_lat, ys_ctx = _s5_out(vp, al_re, al_im, e_mat, t_mat, ft_mat)
    sg_bias = jnp.repeat(jnp.transpose(sg_b[0]), SG_W // SG_HEADS, axis=1)
    sg_w2 = jnp.transpose(sg_w[0].reshape(SG_HEADS // 2, 2, SG_CHUNK, SG_CHUNK), (0, 2, 1, 3))
    sg_w2 = sg_w2.reshape(SG_HEADS // 2, SG_CHUNK, 2 * SG_CHUNK).astype(BF16)
    row_tab, col_tab = _rope_tables()
    z1, q, k, vt, g = _mid(
        x2, ctx2, mod[0], mod[1], ys_lat, ys_ctx, rest, glu_w[0].astype(BF16),
        glu_b[0].reshape(1, S5_W), sg_ln_g[0].reshape(1, SG_W), sg_ln_b[0].reshape(1, SG_W),
        sg_w2, sg_bias, e_w_out[0], ln_g[0].reshape(1, D),
        ln_b[0].reshape(1, D), o_w_in[0], row_tab, col_tab)

    out = _attention_tail(o_sink[0], q, k, vt, g, z1, mod[1], o_w_out[0],
                          ln_g[1].reshape(1, D), ln_b[1].reshape(1, D))
    return out.reshape(B, N_LAT, D)
```

```python
import functools
import math

import jax
import jax.numpy as jnp
from jax import lax
from jax.experimental import pallas as pl
from jax.experimental.pallas import tpu as pltpu

F32 = jnp.float32
BF16 = jnp.bfloat16

D = 1024
B = 2
N_LAT = 8192
N_CTX = 256
DEPTH = 2
GRID_W = 64
S5_W = 512
S5_GC = 16
S5_G = 32
S5_P = 64
SG_W = 512
SG_HEADS = 8
SG_CHUNK = 128
N_HEADS = 16
N_KV = 4
HD = 64
WINDOW = 128
ATT_BLK = 128
ROPE_BASE = 10000.0
NEG_INF = -1e30
LN_EPS = 1e-5
ALPHA = (2 * DEPTH) ** 0.25
LOG2E = math.log2(math.e)
EVEN_IN = 2 * S5_W + 3 * SG_W
ODD_IN = 2 * N_HEADS * HD + 2 * N_KV * HD

LANES = 128
VMEM_LIMIT = 56 * 1024 * 1024

R_LAT = B * N_LAT
R_CTX = B * N_CTX
R_ALL = R_LAT + R_CTX
TN = 512
N_LAT_BLK = R_LAT // TN
N_BLK = R_ALL // TN
LAT_BLK_PER_BATCH = N_LAT // TN
CTX_MOD_ROW = B

S5_CHUNK = 8
S5_QT = S5_W // LANES
S5_GPT = LANES // S5_GC
S5_ST = S5_GPT * S5_P
S5_PAIRS = S5_GPT // 2
PAIR_BLK = 2 * S5_GC
LAT_CHUNKS = N_LAT // S5_CHUNK
CTX_CHUNKS = N_CTX // S5_CHUNK
BATCH_CHUNKS = CTX_CHUNKS + LAT_CHUNKS
N_SEG = 16
SEG_CHUNKS = BATCH_CHUNKS // N_SEG
BWD_RESET_STEP = LAT_CHUNKS - (N_SEG - 1) * SEG_CHUNKS


def _sigmoid(x):
    return 1.0 / (1.0 + jnp.exp(-x))


def _silu(x):
    return x * _sigmoid(x)


def _gelu_tanh(x):
    return 0.5 * x * (1.0 + jnp.tanh(math.sqrt(2.0 / math.pi) * (x + 0.044715 * (x * x * x))))


def _layer_norm(x, g, b):
    mu = jnp.mean(x, axis=-1, keepdims=True)
    xc = x - mu
    var = jnp.mean(xc * xc, axis=-1, keepdims=True)
    return xc * lax.rsqrt(var + LN_EPS) * g + b


def _split_bf16(x):
    hi = x.astype(BF16)
    return hi, (x - hi.astype(F32)).astype(BF16)


def _dot_split(x, y_split, dims):
    x_hi, x_lo = _split_bf16(x)
    y_hi, y_lo = y_split
    dot = lambda a, b: lax.dot_general(a, b, dims, preferred_element_type=F32)
    return dot(x_hi, y_hi) + (dot(x_hi, y_lo) + dot(x_lo, y_hi))


def _block_mod_row(i):
    return jnp.where(i < N_LAT_BLK, i // LAT_BLK_PER_BATCH, CTX_MOD_ROW)


def _mod_kernel(cv_ref, w_ref, b_ref, o_ref):
    s = _silu(cv_ref[...])
    o_ref[0] = _dot_split(s, _split_bf16(w_ref[0]), (((1,), (0,)), ((), ()))) + b_ref[0]


def _modulation(c, c_ctx, mod_w, mod_b):
    cv = jnp.concatenate([c, c_ctx[None], jnp.zeros((8 - B - 1, D), F32)], axis=0)
    tn = 1024
    return pl.pallas_call(
        _mod_kernel,
        grid=(DEPTH, 3 * D // tn),
        in_specs=[
            pl.BlockSpec((8, D), lambda l, j: (0, 0)),
            pl.BlockSpec((1, D, tn), lambda l, j: (l, 0, j)),
            pl.BlockSpec((1, 1, tn), lambda l, j: (l, 0, j)),
        ],
        out_specs=pl.BlockSpec((1, 8, tn), lambda l, j: (l, 0, j)),
        out_shape=jax.ShapeDtypeStruct((DEPTH, 8, 3 * D), F32),
        compiler_params=pltpu.CompilerParams(vmem_limit_bytes=VMEM_LIMIT),
        name="modulation",
    )(cv, mod_w, mod_b.reshape(DEPTH, 1, 3 * D))


def _block_transpose4(tiles):
    tiles = list(tiles)
    blk = lax.broadcasted_iota(jnp.int32, tiles[0].shape, 1) // PAIR_BLK
    for dist in (2, 1):
        keep = (blk & dist) == 0
        for i in range(len(tiles)):
            if i & dist:
                continue
            lo, hi = tiles[i], tiles[i + dist]
            tiles[i] = jnp.where(keep, lo, pltpu.roll(hi, dist * PAIR_BLK, axis=1))
            tiles[i + dist] = jnp.where(keep, pltpu.roll(lo, LANES - dist * PAIR_BLK, axis=1), hi)
    return tiles


def _in0_kernel(x_ref, ctx_ref, mod_ref, w_ref, vp_ref, rest_ref, xa_scr, wbf_scr):
    L = S5_CHUNK
    i = pl.program_id(0)

    @pl.when(i == 0)
    def _():
        wbf_scr[...] = w_ref[...].astype(BF16)

    z = jnp.where(i < N_LAT_BLK, x_ref[...], ctx_ref[...])
    m = mod_ref[pl.ds(_block_mod_row(i), 1), :]
    h = z * (1.0 + m[:, D:2 * D]) + m[:, :D]
    p = jnp.dot(h.astype(BF16), wbf_scr[...], preferred_element_type=F32)
    rest_ref[...] = p[:, S5_W:].astype(BF16)
    for q in range(S5_QT):
        xa_scr[q] = p[:, q * LANES:(q + 1) * LANES]
        tiles = [xa_scr[q, pl.ds(t, TN // L, stride=L), :] for t in range(L)]
        halves = [_block_transpose4(tiles[k * S5_PAIRS:(k + 1) * S5_PAIRS])
                  for k in range(L // S5_PAIRS)]
        for pr in range(S5_PAIRS):
            vp_ref[q * S5_PAIRS + pr] = jnp.concatenate(
                [half[pr] for half in halves], axis=-1).astype(BF16)


def _in0(x2, ctx2, mod, w_in):
    pair_w = S5_CHUNK * PAIR_BLK
    return pl.pallas_call(
        _in0_kernel,
        grid=(N_BLK,),
        in_specs=[
            pl.BlockSpec((TN, D), lambda i: (jnp.minimum(i, N_LAT_BLK - 1), 0)),
            pl.BlockSpec((R_CTX, D), lambda i: (0, 0)),
            pl.BlockSpec((8, 3 * D), lambda i: (0, 0)),
            pl.BlockSpec((D, EVEN_IN), lambda i: (0, 0)),
        ],
        out_specs=[
            pl.BlockSpec((S5_QT * S5_PAIRS, TN // S5_CHUNK, pair_w), lambda i: (0, i, 0)),
            pl.BlockSpec((TN, EVEN_IN - S5_W), lambda i: (i, 0)),
        ],
        out_shape=[
            jax.ShapeDtypeStruct((S5_QT * S5_PAIRS, R_ALL // S5_CHUNK, pair_w), BF16),
            jax.ShapeDtypeStruct((R_ALL, EVEN_IN - S5_W), BF16),
        ],
        scratch_shapes=[pltpu.VMEM((S5_QT, TN, LANES), F32), pltpu.VMEM((D, EVEN_IN), BF16)],
        compiler_params=pltpu.CompilerParams(vmem_limit_bytes=VMEM_LIMIT),
        name="in0",
    )(x2, ctx2, mod, w_in)


def _s5_prep_kernel(rows_ref, bre_ref, bim_ref, cre_ref, cim_ref, dsk_ref, t_ref, e_ref, ft_ref,
                    alr_ref, ali_ref):
    L = S5_CHUNK
    tile = (LANES, S5_ST)
    same_group = (lax.broadcasted_iota(jnp.int32, tile, 0) // S5_GC
                  == lax.broadcasted_iota(jnp.int32, tile, 1) // S5_P)
    nt_dims = (((1,), (1,)), ((), ()))
    nn_dims = (((1,), (0,)), ((), ()))
    pick_ch = (lax.broadcasted_iota(jnp.int32, (LANES, S5_GC), 0) % S5_GC
               == lax.broadcasted_iota(jnp.int32, (LANES, S5_GC), 1)).astype(BF16)
    rep_st = (lax.broadcasted_iota(jnp.int32, (S5_P, S5_ST), 0)
              == lax.broadcasted_iota(jnp.int32, (S5_P, S5_ST), 1) % S5_P).astype(BF16)

    def spread_b(b):
        hi, lo = _split_bf16(b)
        dot = lambda v: lax.dot_general(pick_ch, v, nt_dims, preferred_element_type=F32)
        return jnp.where(same_group, dot(hi) + dot(lo), 0.0)

    def spread_c(c):
        hi, lo = _split_bf16(c)
        dot = lambda v: lax.dot_general(v, rep_st, nn_dims, preferred_element_type=F32)
        return jnp.where(same_group, dot(hi) + dot(lo), 0.0)

    pair_rows = lambda pr: slice(pr * PAIR_BLK, (pr + 1) * PAIR_BLK)
    pair_lanes = lambda pr: slice(pr * LANES, (pr + 1) * LANES)
    lag_strip = []
    for d in range(2):
        lam_re = rows_ref[d, 0]
        lam_im = rows_ref[d, 1]
        dt = jnp.exp(rows_ref[d, 2])
        pw = []
        for k in range(L + 1):
            mag = jnp.exp(lam_re * dt * float(k))
            ang = lam_im * dt * float(k)
            pw.append((mag * jnp.cos(ang), mag * jnp.sin(ang)))
        alr_ref[d] = pw[L][0]
        ali_ref[d] = pw[L][1]
        den = lam_re * lam_re + lam_im * lam_im
        nr = pw[1][0] - 1.0
        ni = pw[1][1]
        f_re = (nr * lam_re + ni * lam_im) / den
        f_im = (ni * lam_re - nr * lam_im) / den
        b_re = spread_b(bre_ref[d])
        b_im = spread_b(bim_ref[d])
        bb_re = f_re * b_re - f_im * b_im
        bb_im = f_re * b_im + f_im * b_re
        c_re = spread_c(cre_ref[d])
        c_im = spread_c(cim_ref[d])
        col_re = slice(2 * d * LANES, (2 * d + 1) * LANES)
        col_im = slice((2 * d + 1) * LANES, (2 * d + 2) * LANES)
        strips = []
        for pr in range(S5_PAIRS):
            blk = lambda a: a[pair_rows(pr), pair_lanes(pr)]
            p_re = [blk(jnp.broadcast_to(pw[k][0], tile)) for k in range(L + 1)]
            p_im = [blk(jnp.broadcast_to(pw[k][1], tile)) for k in range(L + 1)]
            bbr, bbi, cr, ci = blk(bb_re), blk(bb_im), blk(c_re), blk(c_im)
            for k in range(L):
                s = (L - 1 - k) if d == 0 else k
                rows = slice(s * PAIR_BLK, (s + 1) * PAIR_BLK)
                e_ref[0, pr, rows, col_re] = (p_re[k] * bbr - p_im[k] * bbi).astype(BF16)
                e_ref[0, pr, rows, col_im] = (p_re[k] * bbi + p_im[k] * bbr).astype(BF16)
            cw = [(cr * p_re[k] - ci * p_im[k], -(cr * p_im[k] + ci * p_re[k]))
                  for k in range(L + 1)]
            for t in range(L):
                k = (t + 1) if d == 0 else (L - t)
                rows = slice(t * PAIR_BLK, (t + 1) * PAIR_BLK)
                ft_ref[0, pr, rows, col_re] = cw[k][0].astype(BF16)
                ft_ref[0, pr, rows, col_im] = cw[k][1].astype(BF16)
            lags = range(L) if d == 0 else range(L - 1, -1, -1)
            c_stack = jnp.concatenate(
                [jnp.concatenate([cw[k][0], cw[k][1]], axis=-1) for k in lags], axis=0)
            strips.append(_dot_split(jnp.concatenate([bbr, bbi], axis=-1),
                                     _split_bf16(c_stack), nt_dims))
        lag_strip.append(strips)

    strip_w = 2 * L * PAIR_BLK
    mid = (L - 1) * PAIR_BLK
    widen = lambda a: jnp.concatenate([a, jnp.zeros((a.shape[0], strip_w - a.shape[1]), F32)], axis=-1)
    row = lax.broadcasted_iota(jnp.int32, (PAIR_BLK, strip_w), 0)
    lane = lax.broadcasted_iota(jnp.int32, (PAIR_BLK, strip_w), 1)
    d_row = widen(dsk_ref[0])
    for pr in range(S5_PAIRS):
        strip = (widen(lag_strip[1][pr])
                 + pltpu.roll(widen(lag_strip[0][pr]), mid, axis=1)
                 + jnp.where(lane == row + mid,
                             pltpu.roll(d_row, mid - pr * PAIR_BLK, axis=1), 0.0))
        for s in range(L):
            shift = (strip_w - (L - 1 - s) * PAIR_BLK) % strip_w
            blk = pltpu.roll(strip, shift, axis=1) if shift else strip
            t_ref[0, pr, s * PAIR_BLK:(s + 1) * PAIR_BLK, :] = blk[:, :L * PAIR_BLK].astype(BF16)


def _s5_prep(lam_re, lam_im, log_dt, b_re, b_im, c_re, c_im, d_skip):
    L = S5_CHUNK
    n_st = S5_G * S5_P
    pair_w = L * PAIR_BLK
    rows = jnp.stack([lam_re.reshape(2, 1, n_st), lam_im.reshape(2, 1, n_st),
                      jnp.repeat(log_dt, S5_P, axis=1).reshape(2, 1, n_st)], axis=1)
    b_spec = pl.BlockSpec((2, S5_ST, S5_GC), lambda q: (0, q, 0))
    c_spec = pl.BlockSpec((2, LANES, S5_P), lambda q: (0, q, 0))
    return pl.pallas_call(
        _s5_prep_kernel,
        grid=(S5_QT,),
        in_specs=[
            pl.BlockSpec((2, 3, 1, S5_ST), lambda q: (0, 0, 0, q)),
            b_spec, b_spec, c_spec, c_spec,
            pl.BlockSpec((1, 1, LANES), lambda q: (q, 0, 0)),
        ],
        out_specs=[
            pl.BlockSpec((1, S5_PAIRS, pair_w, pair_w), lambda q: (q, 0, 0, 0)),
            pl.BlockSpec((1, S5_PAIRS, pair_w, 4 * LANES), lambda q: (q, 0, 0, 0)),
            pl.BlockSpec((1, S5_PAIRS, pair_w, 4 * LANES), lambda q: (q, 0, 0, 0)),
            pl.BlockSpec((2, 1, S5_ST), lambda q: (q, 0, 0)),
            pl.BlockSpec((2, 1, S5_ST), lambda q: (q, 0, 0)),
        ],
        out_shape=[
            jax.ShapeDtypeStruct((S5_QT, S5_PAIRS, pair_w, pair_w), BF16),
            jax.ShapeDtypeStruct((S5_QT, S5_PAIRS, pair_w, 4 * LANES), BF16),
            jax.ShapeDtypeStruct((S5_QT, S5_PAIRS, pair_w, 4 * LANES), BF16),
            jax.ShapeDtypeStruct((2 * S5_QT, 1, S5_ST), F32),
            jax.ShapeDtypeStruct((2 * S5_QT, 1, S5_ST), F32),
        ],
        compiler_params=pltpu.CompilerParams(vmem_limit_bytes=VMEM_LIMIT),
        name="s5_prep",
    )(rows, b_re.reshape(2, n_st, S5_GC), b_im.reshape(2, n_st, S5_GC),
      c_re.reshape(2, S5_W, S5_P), c_im.reshape(2, S5_W, S5_P), d_skip.reshape(S5_QT, 1, LANES))


N_SLAB = S5_ST // LANES


def _cmul(ar, ai, br, bi):
    return ar * br - ai * bi, ar * bi + ai * br


def _cpow(ar, ai, n):
    res = None
    while n:
        if n & 1:
            res = (ar, ai) if res is None else _cmul(res[0], res[1], ar, ai)
        n >>= 1
        if n:
            ar, ai = _cmul(ar, ai, ar, ai)
    return res


def _scan_boundary_states(loc_ref, out_ref, ar_ref, ai_ref, backward):
    a_row = [(ar_ref[0][:, k * LANES:(k + 1) * LANES], ai_ref[0][:, k * LANES:(k + 1) * LANES])
             for k in range(N_SLAB)]
    a_tile = [(jnp.broadcast_to(r, (N_SEG, LANES)), jnp.broadcast_to(i, (N_SEG, LANES)))
              for r, i in a_row]

    def sweep(lo, hi, carry, store):
        def step(i, carry):
            off = (SEG_CHUNKS - 1 - i) if backward else i
            rows = pl.ds(off, N_SEG, stride=SEG_CHUNKS)
            new = []
            for k, (cr, ci) in enumerate(carry):
                xr = loc_ref[k, rows, :]
                xi = loc_ref[N_SLAB + k, rows, :]
                if store:
                    out_ref[k, rows, :] = cr
                    out_ref[N_SLAB + k, rows, :] = ci
                nr, ni = _cmul(a_tile[k][0], a_tile[k][1], cr, ci)
                new.append((nr + xr, ni + xi))
            return tuple(new)
        return lax.fori_loop(lo, hi, step, carry)

    def full(carry, store):
        if not backward:
            return sweep(0, SEG_CHUNKS, carry, store)
        carry = sweep(0, BWD_RESET_STEP, carry, store)
        first = lax.broadcasted_iota(jnp.int32, (N_SEG, LANES), 0) == 0
        carry = tuple((jnp.where(first, 0.0, cr), jnp.where(first, 0.0, ci)) for cr, ci in carry)
        return sweep(BWD_RESET_STEP, SEG_CHUNKS, carry, store)

    zero_tile = jnp.zeros((N_SEG, LANES), F32)
    ends = full(tuple((zero_tile, zero_tile) for _ in range(N_SLAB)), False)

    starts = []
    for k, (er, ei) in enumerate(ends):
        pr, pi = _cpow(a_row[k][0], a_row[k][1], SEG_CHUNKS)
        if backward:
            order = range(N_SEG - 1, -1, -1)
            g = (er[0:1], ei[0:1])
        else:
            order = range(N_SEG)
            g = (jnp.zeros((1, LANES), F32), jnp.zeros((1, LANES), F32))
        rows_r = [None] * N_SEG
        rows_i = [None] * N_SEG
        for j in order:
            rows_r[j], rows_i[j] = g
            nr, ni = _cmul(pr, pi, g[0], g[1])
            g = (nr + er[j:j + 1], ni + ei[j:j + 1])
        starts.append((jnp.concatenate(rows_r, axis=0), jnp.concatenate(rows_i, axis=0)))
    full(tuple(starts), True)


def _s5_out_kernel(vl_ref, vc_ref, ar_ref, ai_ref, e_ref, t_ref, ft_ref, yl_ref, yc_ref,
                   acc_ref, st_ref):
    L = S5_CHUNK
    d = pl.program_id(2)
    rows = [jnp.concatenate([vc_ref[pr], vl_ref[pr]], axis=0) for pr in range(S5_PAIRS)]
    for pr in range(S5_PAIRS):
        st = jnp.dot(rows[pr], e_ref[0, pr], preferred_element_type=F32)
        st_ref[pr] = st[:, :LANES]
        st_ref[N_SLAB + pr] = st[:, LANES:]

    def from_states(pr):
        st = jnp.concatenate([st_ref[pr], st_ref[N_SLAB + pr]], axis=-1).astype(BF16)
        return lax.dot_general(st, ft_ref[0, pr], (((1,), (1,)), ((), ())),
                               preferred_element_type=F32)

    @pl.when(d == 0)
    def _():
        _scan_boundary_states(st_ref, st_ref, ar_ref, ai_ref, False)
        for pr in range(S5_PAIRS):
            acc_ref[pr] = from_states(pr) + jnp.dot(rows[pr], t_ref[0, pr],
                                                    preferred_element_type=F32)

    @pl.when(d == 1)
    def _():
        _scan_boundary_states(st_ref, st_ref, ar_ref, ai_ref, True)
        ys = [acc_ref[pr] + from_states(pr) for pr in range(S5_PAIRS)]
        tiles = [tile for k in range(L // S5_PAIRS) for tile in _block_transpose4(
            [y[:, k * LANES:(k + 1) * LANES] for y in ys])]
        for t, tile in enumerate(tiles):
            yc_ref[pl.ds(t, CTX_CHUNKS, stride=L), :] = tile[:CTX_CHUNKS]
            yl_ref[pl.ds(t, LAT_CHUNKS, stride=L), :] = tile[CTX_CHUNKS:]


def _s5_out(vp, al_re, al_im, e_mat, t_mat, ft_mat):
    pair_w = S5_CHUNK * PAIR_BLK
    ctx_blk0 = (R_LAT // S5_CHUNK) // CTX_CHUNKS
    return pl.pallas_call(
        _s5_out_kernel,
        grid=(S5_QT, B, 2),
        in_specs=[
            pl.BlockSpec((S5_PAIRS, LAT_CHUNKS, pair_w), lambda q, b, d: (q, b, 0)),
            pl.BlockSpec((S5_PAIRS, CTX_CHUNKS, pair_w), lambda q, b, d: (q, ctx_blk0 + b, 0)),
            pl.BlockSpec((1, 1, S5_ST), lambda q, b, d: (2 * q + d, 0, 0)),
            pl.BlockSpec((1, 1, S5_ST), lambda q, b, d: (2 * q + d, 0, 0)),
            pl.BlockSpec((1, S5_PAIRS, pair_w, 2 * LANES), lambda q, b, d: (q, 0, 0, d)),
            pl.BlockSpec((1, S5_PAIRS, pair_w, pair_w), lambda q, b, d: (q, 0, 0, 0)),
            pl.BlockSpec((1, S5_PAIRS, pair_w, 2 * LANES), lambda q, b, d: (q, 0, 0, d)),
        ],
        out_specs=[
            pl.BlockSpec((N_LAT, LANES), lambda q, b, d: (b, q)),
            pl.BlockSpec((N_CTX, LANES), lambda q, b, d: (b, q)),
        ],
        out_shape=[
            jax.ShapeDtypeStruct((R_LAT, S5_W), F32),
            jax.ShapeDtypeStruct((R_CTX, S5_W), F32),
        ],
        scratch_shapes=[pltpu.VMEM((S5_PAIRS, BATCH_CHUNKS, pair_w), F32),
                        pltpu.VMEM((2 * N_SLAB, BATCH_CHUNKS, LANES), F32)],
        compiler_params=pltpu.CompilerParams(vmem_limit_bytes=VMEM_LIMIT),
        name="s5_out",
    )(vp, vp, al_re, al_im, e_mat, t_mat, ft_mat)


def _tail0_stages(z, ys, rest_ref, gate, gluw_ref, glub_ref, sg_g_ref, sg_b_ref, sgw_ref,
                  sgbias_ref, wout_ref, lng_ref, lnb_ref):
    ga = rest_ref[:, 0:S5_W].astype(F32)
    u = rest_ref[:, S5_W:2 * S5_W].astype(F32)
    v = rest_ref[:, 2 * S5_W:3 * S5_W].astype(F32)
    gb = rest_ref[:, 3 * S5_W:4 * S5_W].astype(F32)

    ya = _gelu_tanh(ys)
    glu = jnp.dot(ya.astype(BF16), gluw_ref[...], preferred_element_type=F32) + glub_ref[...]
    yield None
    ya = ya * _sigmoid(glu) * _silu(ga)
    yield None

    vn = _layer_norm(v, sg_g_ref[...], sg_b_ref[...]).astype(BF16)
    lane = lax.broadcasted_iota(jnp.int32, (SG_CHUNK, LANES), 1)
    first_head = lane < (SG_W // SG_HEADS)
    chunks = []
    for c in range(TN // SG_CHUNK):
        tiles = []
        for j in range(SG_W // LANES):
            vt = vn[c * SG_CHUNK:(c + 1) * SG_CHUNK, j * LANES:(j + 1) * LANES]
            zero = jnp.zeros_like(vt)
            v2 = jnp.concatenate([jnp.where(first_head, vt, zero), jnp.where(first_head, zero, vt)],
                                 axis=0)
            tiles.append(jnp.dot(sgw_ref[j], v2, preferred_element_type=F32))
        chunks.append(jnp.concatenate(tiles, axis=-1) + sgbias_ref[...])
    s = jnp.concatenate(chunks, axis=0)
    yb = u * s * _silu(gb)
    yield None

    mix = jnp.concatenate([ya, yb], axis=-1).astype(BF16)
    y = jnp.dot(mix, wout_ref[...], preferred_element_type=F32) * gate
    yield None
    yield _layer_norm(ALPHA * z + y, lng_ref[...], lnb_ref[...])


def _rope_tables():
    nf = HD // 4
    n_rows = N_LAT // GRID_W
    lane = jnp.arange(LANES)
    inv = ROPE_BASE ** (-(lane % nf).astype(F32) / nf)
    by_row = ((lane % HD) // (HD // 2) == 0)[None, :]
    sign = jnp.where((lane % (HD // 2)) < nf, -1.0, 1.0)[None, :]
    row_ang = jnp.arange(n_rows, dtype=F32)[:, None] * inv[None, :]
    col_ang = jnp.arange(GRID_W, dtype=F32)[:, None] * inv[None, :]
    zero = jnp.zeros((), F32)
    row_tab = jnp.stack([jnp.where(by_row, jnp.cos(row_ang), zero),
                         jnp.where(by_row, sign * jnp.sin(row_ang), zero)])
    col_tab = jnp.stack([jnp.where(by_row, zero, jnp.cos(col_ang)),
                         jnp.where(by_row, zero, sign * jnp.sin(col_ang))])
    return row_tab, col_tab


def _rope_tile(x, cos, sin, first_half):
    nf = HD // 4
    partner = jnp.where(first_half, pltpu.roll(x, LANES - nf, axis=1), pltpu.roll(x, nf, axis=1))
    return x * cos + partner * sin


def _mid_kernel(x_ref, ctx_ref, mod0_ref, mod1_ref, ysl_ref, ysc_ref, rest_ref, gluw_ref, glub_ref,
                sg_g_ref, sg_b_ref, sgw_ref, sgbias_ref, wout_ref, lng_ref, lnb_ref, w_ref,
                rtab_ref, ctab_ref, z1_ref, q_ref, k_ref, vt_ref, g_ref, zprev_ref, wbf_scr,
                wout_scr):
    i = pl.program_id(0)
    dq = N_HEADS * HD
    dkv = N_KV * HD

    @pl.when(i == 0)
    def _():
        zprev_ref[...] = jnp.zeros((TN, D), F32)
        wbf_scr[...] = w_ref[...].astype(BF16)
        wout_scr[...] = wout_ref[...].astype(BF16)

    j = jnp.maximum(i - 1, 0)
    m1 = mod1_ref[pl.ds(_block_mod_row(j), 1), :]
    h = (zprev_ref[...] * (1.0 + m1[:, D:2 * D]) + m1[:, :D]).astype(BF16)
    grid_row0 = (j % LAT_BLK_PER_BATCH) * (TN // GRID_W)
    table = lambda t: jnp.concatenate(
        [rtab_ref[t, pl.ds(grid_row0 + r, 1), :] + ctab_ref[t] for r in range(TN // GRID_W)],
        axis=0)
    cos = jnp.where(j < N_LAT_BLK, table(0), 1.0)
    sin = jnp.where(j < N_LAT_BLK, table(1), 0.0)
    lane = lax.broadcasted_iota(jnp.int32, (TN, LANES), 1)
    first_half = (lane % (HD // 2)) < (HD // 4)
    scale = HD ** -0.5 * LOG2E

    def project(c0, c1):
        return jnp.dot(h, wbf_scr[:, c0:c1], preferred_element_type=F32)

    def roped_tiles(p, mult):
        for c in range(p.shape[1] // LANES):
            r = _rope_tile(p[:, c * LANES:(c + 1) * LANES], cos, sin, first_half)
            yield c, (r * mult if mult != 1.0 else r).astype(BF16)

    t = jnp.minimum(i, N_BLK - 1)
    z = jnp.where(t < N_LAT_BLK, x_ref[...], ctx_ref[...])
    ys = jnp.where(t < N_LAT_BLK, ysl_ref[...], ysc_ref[...])
    gate = mod0_ref[pl.ds(_block_mod_row(t), 1), 2 * D:3 * D]
    tail = _tail0_stages(z, ys, rest_ref, gate, gluw_ref, glub_ref, sg_g_ref, sg_b_ref, sgw_ref,
                         sgbias_ref, wout_scr, lng_ref, lnb_ref)

    half_q = dq // 2
    p_q0 = project(0, half_q)
    next(tail)
    p_q1 = project(half_q, dq)
    next(tail)
    p_k = project(dq, dq + dkv)
    vt_ref[...] = lax.dot_general(wbf_scr[:, dq + dkv:dq + 2 * dkv], h, (((0,), (1,)), ((), ())),
                                  preferred_element_type=F32).astype(BF16)
    next(tail)
    next(tail)
    for c0, p in ((0, p_q0), (half_q, p_q1)):
        for c, r in roped_tiles(p, scale):
            q_ref[:, c0 + c * LANES:c0 + (c + 1) * LANES] = r
    for c, r in roped_tiles(p_k, 1.0):
        k_ref[:, c * LANES:(c + 1) * LANES] = r
    p_g = project(dq + 2 * dkv, ODD_IN)
    z1 = next(tail)
    z1_ref[...] = z1
    zprev_ref[...] = z1
    g_ref[...] = p_g.astype(BF16)


def _mid(x2, ctx2, mod0, mod1, ys_lat, ys_ctx, rest, glu_w, glu_b, sg_g, sg_b, sg_w, sg_bias,
         w_out, ln_g, ln_b, w_in1, row_tab, col_tab):
    dq = N_HEADS * HD
    dkv = N_KV * HD
    once = dict(pipeline_mode=pl.Buffered(1))
    row = lambda n: pl.BlockSpec((1, n), lambda i: (0, 0))
    lat = lambda i: (jnp.minimum(i, N_LAT_BLK - 1), 0)
    tail_blk = lambda i: (jnp.minimum(i, N_BLK - 1), 0)
    proj_blk = lambda i: (jnp.maximum(i - 1, 0), 0)
    return pl.pallas_call(
        _mid_kernel,
        grid=(N_BLK + 1,),
        in_specs=[
            pl.BlockSpec((TN, D), lat),
            pl.BlockSpec((R_CTX, D), lambda i: (0, 0), **once),
            pl.BlockSpec((8, 3 * D), lambda i: (0, 0)),
            pl.BlockSpec((8, 3 * D), lambda i: (0, 0)),
            pl.BlockSpec((TN, S5_W), lat),
            pl.BlockSpec((R_CTX, S5_W), lambda i: (0, 0), **once),
            pl.BlockSpec((TN, EVEN_IN - S5_W), tail_blk),
            pl.BlockSpec((S5_W, S5_W), lambda i: (0, 0), **once),
            row(S5_W), row(SG_W), row(SG_W),
            pl.BlockSpec((SG_HEADS // 2, SG_CHUNK, 2 * SG_CHUNK), lambda i: (0, 0, 0)),
            pl.BlockSpec((SG_CHUNK, SG_W), lambda i: (0, 0)),
            pl.BlockSpec((S5_W + SG_W, D), lambda i: (0, 0), **once),
            row(D), row(D),
            pl.BlockSpec((D, ODD_IN), lambda i: (0, 0), **once),
            pl.BlockSpec((2, N_LAT // GRID_W, LANES), lambda i: (0, 0, 0)),
            pl.BlockSpec((2, GRID_W, LANES), lambda i: (0, 0, 0)),
        ],
        out_specs=[
            pl.BlockSpec((TN, D), tail_blk),
            pl.BlockSpec((TN, dq), proj_blk),
            pl.BlockSpec((TN, dkv), proj_blk),
            pl.BlockSpec((dkv, TN), lambda i: (0, jnp.maximum(i - 1, 0))),
            pl.BlockSpec((TN, dq), proj_blk),
        ],
        out_shape=[
            jax.ShapeDtypeStruct((R_ALL, D), F32),
            jax.ShapeDtypeStruct((R_ALL, dq), BF16),
            jax.ShapeDtypeStruct((R_ALL, dkv), BF16),
            jax.ShapeDtypeStruct((dkv, R_ALL), BF16),
            jax.ShapeDtypeStruct((R_ALL, dq), BF16),
        ],
        scratch_shapes=[pltpu.VMEM((TN, D), F32), pltpu.VMEM((D, ODD_IN), BF16),
                        pltpu.VMEM((S5_W + SG_W, D), BF16)],
        compiler_params=pltpu.CompilerParams(vmem_limit_bytes=VMEM_LIMIT),
        name="mid",
    )(x2, ctx2, mod0, mod1, ys_lat, ys_ctx, rest, glu_w, glu_b, sg_g, sg_b, sg_w, sg_bias, w_out,
      ln_g, ln_b, w_in1, row_tab, col_tab)


N_QBLK = N_LAT // ATT_BLK
GRP = N_HEADS // N_KV
ATT_SUB = 4
ATT_ROWS = ATT_SUB * ATT_BLK
N_QSTEP = N_QBLK // ATT_SUB
N_ATT_STEPS = B * N_QSTEP
ATT_STAGE_UNITS = ATT_SUB * N_KV
assert ATT_ROWS == TN
ONES_ROWS = 16


def _attn_kernel(sink_ref, q_ref, kp_ref, kc_ref, kn_ref, kx_ref, vp_ref, vc_ref, vn_ref, vx_ref,
                 g_ref, z_ref, mod_ref, wout_ref, lng_ref, lnb_ref, out_ref, o_scr, wout_scr):
    step = pl.program_id(0)

    @pl.when(step == 0)
    def _():
        o_scr[...] = jnp.zeros((ATT_ROWS, N_HEADS * HD), BF16)
        wout_scr[...] = wout_ref[...].astype(BF16)

    t = jnp.maximum(step - 1, 0)
    gate = mod_ref[pl.ds(_block_mod_row(t), 1), 2 * D:3 * D]
    mix = (o_scr[...].astype(F32) * _silu(g_ref[...].astype(F32))).astype(BF16)
    y = jnp.dot(mix, wout_scr[...], preferred_element_type=F32) * gate
    out_ref[...] = _layer_norm(ALPHA * z_ref[...] + y, lng_ref[...], lnb_ref[...])

    i = jnp.minimum(step, N_ATT_STEPS - 1) % N_QSTEP
    n_win = 3 * ATT_BLK
    n_keys = n_win + N_CTX
    nq = GRP * ATT_BLK
    k_win = jnp.concatenate([kp_ref[...], kc_ref[...], kn_ref[...]], axis=0)
    vt_win = jnp.concatenate([vp_ref[...], vc_ref[...], vn_ref[...]], axis=1)
    k_ctx = kx_ref[...]
    vt_ctx = vx_ref[...]

    kpos = lax.broadcasted_iota(jnp.int32, (ATT_BLK, ATT_BLK), 0)
    qpos = lax.broadcasted_iota(jnp.int32, (ATT_BLK, ATT_BLK), 1)
    tile_q = lambda a: jnp.concatenate([a] * GRP, axis=1)
    qgrp = lax.broadcasted_iota(jnp.int32, (1, nq), 1) // ATT_BLK
    ones = jnp.ones((ONES_ROWS, n_keys), BF16)
    q = q_ref[...]

    units = [(u, h) for u in range(ATT_SUB) for h in range(N_KV)]
    k_all, vt_all, bias = [], [], []
    for u in range(ATT_SUB):
        k_all.append(jnp.concatenate([k_win[u * ATT_BLK:u * ATT_BLK + n_win], k_ctx], axis=0))
        vt_all.append(jnp.concatenate([vt_win[:, u * ATT_BLK:u * ATT_BLK + n_win], vt_ctx], axis=1))
        has_prev = (i > 0) if u == 0 else True
        has_next = (i < N_QSTEP - 1) if u == ATT_SUB - 1 else True
        bias.append((tile_q(jnp.where((kpos >= qpos) & has_prev, 0.0, NEG_INF)),
                     tile_q(jnp.where((kpos <= qpos) & has_next, 0.0, NEG_INF))))
    outs = [[] for _ in range(ATT_SUB)]
    for first in range(0, len(units), ATT_STAGE_UNITS):
        group = units[first:first + ATT_STAGE_UNITS]
        scores = []
        for u, h in group:
            kh = k_all[u][:, h * HD:(h + 1) * HD]
            qh = jnp.concatenate(
                [q[u * ATT_BLK:(u + 1) * ATT_BLK, (h * GRP + g) * HD:(h * GRP + g + 1) * HD]
                 for g in range(GRP)], axis=0)
            scores.append(lax.dot_general(kh, qh, (((1,), (1,)), ((), ())),
                                          preferred_element_type=F32))
        probs = []
        for (u, h), s in zip(group, scores):
            s = jnp.concatenate([s[:ATT_BLK] + bias[u][0], s[ATT_BLK:2 * ATT_BLK],
                                 s[2 * ATT_BLK:n_win] + bias[u][1], s[n_win:]], axis=0)
            sink = jnp.zeros((1, nq), F32)
            for g in range(GRP):
                sink = jnp.where(qgrp == g, sink_ref[h * GRP + g] * LOG2E, sink)
            m = jnp.maximum(jnp.max(s, axis=0, keepdims=True), sink)
            probs.append((jnp.exp2(s - m).astype(BF16), jnp.exp2(sink - m)))
        for (u, h), (p, p_sink) in zip(group, probs):
            vt1 = jnp.concatenate([vt_all[u][h * HD:(h + 1) * HD], ones], axis=0)
            ov = jnp.dot(vt1, p, preferred_element_type=F32)
            o_t = ov[:HD] / (ov[HD:HD + 1] + p_sink)
            outs[u].extend(o_t[:, g * ATT_BLK:(g + 1) * ATT_BLK] for g in range(GRP))
    for u in range(ATT_SUB):
        o_scr[u * ATT_BLK:(u + 1) * ATT_BLK, :] = jnp.transpose(
            jnp.concatenate(outs[u], axis=0)).astype(BF16)


def _attention_tail(sink, q, k, vt, g, z1, mod, w_out, ln_g, ln_b):
    dq = N_HEADS * HD
    dkv = N_KV * HD
    ctx_blk0 = R_LAT // N_CTX
    cur = lambda s: jnp.minimum(s, N_ATT_STEPS - 1)
    batch = lambda s: cur(s) // N_QSTEP
    qstep = lambda s: cur(s) % N_QSTEP
    prev_blk = lambda s: batch(s) * N_QBLK + jnp.maximum(ATT_SUB * qstep(s) - 1, 0)
    next_blk = lambda s: batch(s) * N_QBLK + jnp.minimum(ATT_SUB * (qstep(s) + 1), N_QBLK - 1)
    tail = lambda s: jnp.maximum(s - 1, 0)
    once = dict(pipeline_mode=pl.Buffered(1))
    row = lambda n: pl.BlockSpec((1, n), lambda s, sk: (0, 0))
    return pl.pallas_call(
        _attn_kernel,
        grid_spec=pltpu.PrefetchScalarGridSpec(
            num_scalar_prefetch=1,
            grid=(N_ATT_STEPS + 1,),
            in_specs=[
                pl.BlockSpec((ATT_ROWS, dq), lambda s, sk: (cur(s), 0)),
                pl.BlockSpec((ATT_BLK, dkv), lambda s, sk: (prev_blk(s), 0)),
                pl.BlockSpec((ATT_ROWS, dkv), lambda s, sk: (cur(s), 0)),
                pl.BlockSpec((ATT_BLK, dkv), lambda s, sk: (next_blk(s), 0)),
                pl.BlockSpec((N_CTX, dkv), lambda s, sk: (ctx_blk0 + batch(s), 0)),
                pl.BlockSpec((dkv, ATT_BLK), lambda s, sk: (0, prev_blk(s))),
                pl.BlockSpec((dkv, ATT_ROWS), lambda s, sk: (0, cur(s))),
                pl.BlockSpec((dkv, ATT_BLK), lambda s, sk: (0, next_blk(s))),
                pl.BlockSpec((dkv, N_CTX), lambda s, sk: (0, ctx_blk0 + batch(s))),
                pl.BlockSpec((ATT_ROWS, dq), lambda s, sk: (tail(s), 0)),
                pl.BlockSpec((ATT_ROWS, D), lambda s, sk: (tail(s), 0)),
                pl.BlockSpec((8, 3 * D), lambda s, sk: (0, 0)),
                pl.BlockSpec((dq, D), lambda s, sk: (0, 0), **once),
                row(D), row(D),
            ],
            out_specs=pl.BlockSpec((ATT_ROWS, D), lambda s, sk: (tail(s), 0)),
            scratch_shapes=[pltpu.VMEM((ATT_ROWS, dq), BF16), pltpu.VMEM((dq, D), BF16)],
        ),
        out_shape=jax.ShapeDtypeStruct((R_LAT, D), F32),
        compiler_params=pltpu.CompilerParams(vmem_limit_bytes=VMEM_LIMIT),
        name="attention",
    )(sink, q, k, k, k, k, vt, vt, vt, vt, g, z1, mod, w_out, ln_g, ln_b)


def kernel(x, c, ctx, c_ctx, mod_w, mod_b, ln_g, ln_b, e_w_in, e_w_out, s5_lam_re, s5_lam_im,
           s5_log_dt, s5_b_re, s5_b_im, s5_c_re, s5_c_im, s5_d, glu_w, glu_b, sg_ln_g, sg_ln_b,
           sg_w, sg_b, o_w_in, o_w_out, o_sink):
    x2 = x.reshape(R_LAT, D)
    ctx2 = ctx.reshape(R_CTX, D)
    mod = _modulation(c, c_ctx, mod_w, mod_b)

    vp, rest = _in0(x2, ctx2, mod[0], e_w_in[0])
    t_mat, e_mat, ft_mat, al_re, al_im = _s5_prep(
        s5_lam_re[0], s5_lam_im[0], s5_log_dt[0], s5_b_re[0], s5_b_im[0], s5_c_re[0], s5_c_im[0],
        s5_d[0])
    ys_lat, ys_ctx = _s5_out(vp, al_re, al_im, e_mat, t_mat, ft_mat)
    sg_bias = jnp.repeat(jnp.transpose(sg_b[0]), SG_W // SG_HEADS, axis=1)
    sg_w2 = jnp.transpose(sg_w[0].reshape(SG_HEADS // 2, 2, SG_CHUNK, SG_CHUNK), (0, 2, 1, 3))
    sg_w2 = sg_w2.reshape(SG_HEADS // 2, SG_CHUNK, 2 * SG_CHUNK).astype(BF16)
    row_tab, col_tab = _rope_tables()
    z1, q, k, vt, g = _mid(
        x2, ctx2, mod[0], mod[1], ys_lat, ys_ctx, rest, glu_w[0].astype(BF16),
        glu_b[0].reshape(1, S5_W), sg_ln_g[0].reshape(1, SG_W), sg_ln_b[0].reshape(1, SG_W),
        sg_w2, sg_bias, e_w_out[0], ln_g[0].reshape(1, D),
        ln_b[0].reshape(1, D), o_w_in[0], row_tab, col_tab)

    out = _attention_tail(o_sink[0], q, k, vt, g, z1, mod[1], o_w_out[0],
                          ln_g[1].reshape(1, D), ln_b[1].reshape(1, D))
    return out.reshape(B, N_LAT, D)
```

```python
import math

import jax
import jax.numpy as jnp
from jax import lax
from jax.experimental import pallas as pl
from jax.experimental.pallas import tpu as pltpu

F32 = jnp.float32
BF16 = jnp.bfloat16

D = 1024
B = 2
N_LAT = 8192
N_CTX = 256
DEPTH = 2
GRID_W = 64
S5_W = 512
S5_GC = 16
S5_G = 32
S5_P = 64
SG_W = 512
SG_HEADS = 8
SG_CHUNK = 128
N_HEADS = 16
N_KV = 4
HD = 64
WINDOW = 128
ATT_BLK = 128
ROPE_BASE = 10000.0
NEG_INF = -1e30
LN_EPS = 1e-5
ALPHA = (2 * DEPTH) ** 0.25
LOG2E = math.log2(math.e)
EVEN_IN = 2 * S5_W + 3 * SG_W
ODD_IN = 2 * N_HEADS * HD + 2 * N_KV * HD

LANES = 128
VMEM_LIMIT = 56 * 1024 * 1024

R_LAT = B * N_LAT
R_CTX = B * N_CTX
R_ALL = R_LAT + R_CTX
TN = 512
N_LAT_BLK = R_LAT // TN
N_BLK = R_ALL // TN
LAT_BLK_PER_BATCH = N_LAT // TN
CTX_MOD_ROW = B

S5_CHUNK = 8
S5_QT = S5_W // LANES
S5_GPT = LANES // S5_GC
S5_ST = S5_GPT * S5_P
S5_PAIRS = S5_GPT // 2
PAIR_BLK = 2 * S5_GC
LAT_CHUNKS = N_LAT // S5_CHUNK
CTX_CHUNKS = N_CTX // S5_CHUNK
BATCH_CHUNKS = CTX_CHUNKS + LAT_CHUNKS
N_SEG = 16
SEG_CHUNKS = BATCH_CHUNKS // N_SEG
BWD_RESET_STEP = LAT_CHUNKS - (N_SEG - 1) * SEG_CHUNKS


def _sigmoid(x):
    return 1.0 / (1.0 + jnp.exp(-x))


def _silu(x):
    return x * _sigmoid(x)


def _gelu_tanh(x):
    return 0.5 * x * (1.0 + jnp.tanh(math.sqrt(2.0 / math.pi) * (x + 0.044715 * (x * x * x))))


def _layer_norm(x, g, b):
    mu = jnp.mean(x, axis=-1, keepdims=True)
    xc = x - mu
    var = jnp.mean(xc * xc, axis=-1, keepdims=True)
    return xc * lax.rsqrt(var + LN_EPS) * g + b


def _split_bf16(x):
    hi = x.astype(BF16)
    return hi, (x - hi.astype(F32)).astype(BF16)


def _dot_split(x, y_split, dims):
    x_hi, x_lo = _split_bf16(x)
    y_hi, y_lo = y_split
    dot = lambda a, b: lax.dot_general(a, b, dims, preferred_element_type=F32)
    return dot(x_hi, y_hi) + (dot(x_hi, y_lo) + dot(x_lo, y_hi))


def _block_mod_row(i):
    return jnp.where(i < N_LAT_BLK, i // LAT_BLK_PER_BATCH, CTX_MOD_ROW)


def _mod_kernel(cv_ref, w_ref, b_ref, o_ref):
    s = _silu(cv_ref[...])
    o_ref[0] = _dot_split(s, _split_bf16(w_ref[0]), (((1,), (0,)), ((), ()))) + b_ref[0]


def _modulation(c, c_ctx, mod_w, mod_b):
    cv = jnp.concatenate([c, c_ctx[None], jnp.zeros((8 - B - 1, D), F32)], axis=0)
    tn = 1024
    return pl.pallas_call(
        _mod_kernel,
        grid=(DEPTH, 3 * D // tn),
        in_specs=[
            pl.BlockSpec((8, D), lambda l, j: (0, 0)),
            pl.BlockSpec((1, D, tn), lambda l, j: (l, 0, j)),
            pl.BlockSpec((1, 1, tn), lambda l, j: (l, 0, j)),
        ],
        out_specs=pl.BlockSpec((1, 8, tn), lambda l, j: (l, 0, j)),
        out_shape=jax.ShapeDtypeStruct((DEPTH, 8, 3 * D), F32),
        compiler_params=pltpu.CompilerParams(vmem_limit_bytes=VMEM_LIMIT),
        name="modulation",
    )(cv, mod_w, mod_b.reshape(DEPTH, 1, 3 * D))


def _block_transpose4(tiles):
    tiles = list(tiles)
    blk = lax.broadcasted_iota(jnp.int32, tiles[0].shape, 1) // PAIR_BLK
    for dist in (2, 1):
        keep = (blk & dist) == 0
        for i in range(len(tiles)):
            if i & dist:
                continue
            lo, hi = tiles[i], tiles[i + dist]
            tiles[i] = jnp.where(keep, lo, pltpu.roll(hi, dist * PAIR_BLK, axis=1))
            tiles[i + dist] = jnp.where(keep, pltpu.roll(lo, LANES - dist * PAIR_BLK, axis=1), hi)
    return tiles


def _in0_kernel(x_ref, ctx_ref, mod_ref, w_ref, vp_ref, rest_ref, xa_scr, wbf_scr):
    L = S5_CHUNK
    i = pl.program_id(0)

    @pl.when(i == 0)
    def _():
        wbf_scr[...] = w_ref[...].astype(BF16)

    z = jnp.where(i < N_LAT_BLK, x_ref[...], ctx_ref[...])
    m = mod_ref[pl.ds(_block_mod_row(i), 1), :]
    h = z * (1.0 + m[:, D:2 * D]) + m[:, :D]
    p = jnp.dot(h.astype(BF16), wbf_scr[...], preferred_element_type=F32)
    rest_ref[...] = p[:, S5_W:].astype(BF16)
    for q in range(S5_QT):
        xa_scr[q] = p[:, q * LANES:(q + 1) * LANES]
        tiles = [xa_scr[q, pl.ds(t, TN // L, stride=L), :] for t in range(L)]
        halves = [_block_transpose4(tiles[k * S5_PAIRS:(k + 1) * S5_PAIRS])
                  for k in range(L // S5_PAIRS)]
        for pr in range(S5_PAIRS):
            vp_ref[q * S5_PAIRS + pr] = jnp.concatenate(
                [half[pr] for half in halves], axis=-1).astype(BF16)


def _in0(x2, ctx2, mod, w_in):
    pair_w = S5_CHUNK * PAIR_BLK
    return pl.pallas_call(
        _in0_kernel,
        grid=(N_BLK,),
        in_specs=[
            pl.BlockSpec((TN, D), lambda i: (jnp.minimum(i, N_LAT_BLK - 1), 0)),
            pl.BlockSpec((R_CTX, D), lambda i: (0, 0)),
            pl.BlockSpec((8, 3 * D), lambda i: (0, 0)),
            pl.BlockSpec((D, EVEN_IN), lambda i: (0, 0)),
        ],
        out_specs=[
            pl.BlockSpec((S5_QT * S5_PAIRS, TN // S5_CHUNK, pair_w), lambda i: (0, i, 0)),
            pl.BlockSpec((TN, EVEN_IN - S5_W), lambda i: (i, 0)),
        ],
        out_shape=[
            jax.ShapeDtypeStruct((S5_QT * S5_PAIRS, R_ALL // S5_CHUNK, pair_w), BF16),
            jax.ShapeDtypeStruct((R_ALL, EVEN_IN - S5_W), BF16),
        ],
        scratch_shapes=[pltpu.VMEM((S5_QT, TN, LANES), F32), pltpu.VMEM((D, EVEN_IN), BF16)],
        compiler_params=pltpu.CompilerParams(vmem_limit_bytes=VMEM_LIMIT),
        name="in0",
    )(x2, ctx2, mod, w_in)


def _s5_prep_kernel(rows_ref, bre_ref, bim_ref, cre_ref, cim_ref, dsk_ref, t_ref, e_ref, ft_ref,
                    alr_ref, ali_ref):
    L = S5_CHUNK
    tile = (LANES, S5_ST)
    same_group = (lax.broadcasted_iota(jnp.int32, tile, 0) // S5_GC
                  == lax.broadcasted_iota(jnp.int32, tile, 1) // S5_P)
    nt_dims = (((1,), (1,)), ((), ()))
    nn_dims = (((1,), (0,)), ((), ()))
    pick_ch = (lax.broadcasted_iota(jnp.int32, (LANES, S5_GC), 0) % S5_GC
               == lax.broadcasted_iota(jnp.int32, (LANES, S5_GC), 1)).astype(BF16)
    rep_st = (lax.broadcasted_iota(jnp.int32, (S5_P, S5_ST), 0)
              == lax.broadcasted_iota(jnp.int32, (S5_P, S5_ST), 1) % S5_P).astype(BF16)

    def spread_b(b):
        hi, lo = _split_bf16(b)
        dot = lambda v: lax.dot_general(pick_ch, v, nt_dims, preferred_element_type=F32)
        return jnp.where(same_group, dot(hi) + dot(lo), 0.0)

    def spread_c(c):
        hi, lo = _split_bf16(c)
        dot = lambda v: lax.dot_general(v, rep_st, nn_dims, preferred_element_type=F32)
        return jnp.where(same_group, dot(hi) + dot(lo), 0.0)

    pair_rows = lambda pr: slice(pr * PAIR_BLK, (pr + 1) * PAIR_BLK)
    pair_lanes = lambda pr: slice(pr * LANES, (pr + 1) * LANES)
    lag_strip = []
    for d in range(2):
        lam_re = rows_ref[d, 0]
        lam_im = rows_ref[d, 1]
        dt = jnp.exp(rows_ref[d, 2])
        pw = []
        for k in range(L + 1):
            mag = jnp.exp(lam_re * dt * float(k))
            ang = lam_im * dt * float(k)
            pw.append((mag * jnp.cos(ang), mag * jnp.sin(ang)))
        alr_ref[d] = pw[L][0]
        ali_ref[d] = pw[L][1]
        den = lam_re * lam_re + lam_im * lam_im
        nr = pw[1][0] - 1.0
        ni = pw[1][1]
        f_re = (nr * lam_re + ni * lam_im) / den
        f_im = (ni * lam_re - nr * lam_im) / den
        b_re = spread_b(bre_ref[d])
        b_im = spread_b(bim_ref[d])
        bb_re = f_re * b_re - f_im * b_im
        bb_im = f_re * b_im + f_im * b_re
        c_re = spread_c(cre_ref[d])
        c_im = spread_c(cim_ref[d])
        col_re = slice(2 * d * LANES, (2 * d + 1) * LANES)
        col_im = slice((2 * d + 1) * LANES, (2 * d + 2) * LANES)
        strips = []
        for pr in range(S5_PAIRS):
            blk = lambda a: a[pair_rows(pr), pair_lanes(pr)]
            p_re = [blk(jnp.broadcast_to(pw[k][0], tile)) for k in range(L + 1)]
            p_im = [blk(jnp.broadcast_to(pw[k][1], tile)) for k in range(L + 1)]
            bbr, bbi, cr, ci = blk(bb_re), blk(bb_im), blk(c_re), blk(c_im)
            for k in range(L):
                s = (L - 1 - k) if d == 0 else k
                rows = slice(s * PAIR_BLK, (s + 1) * PAIR_BLK)
                e_ref[0, pr, rows, col_re] = (p_re[k] * bbr - p_im[k] * bbi).astype(BF16)
                e_ref[0, pr, rows, col_im] = (p_re[k] * bbi + p_im[k] * bbr).astype(BF16)
            cw = [(cr * p_re[k] - ci * p_im[k], -(cr * p_im[k] + ci * p_re[k]))
                  for k in range(L + 1)]
            for t in range(L):
                k = (t + 1) if d == 0 else (L - t)
                rows = slice(t * PAIR_BLK, (t + 1) * PAIR_BLK)
                ft_ref[0, pr, rows, col_re] = cw[k][0].astype(BF16)
                ft_ref[0, pr, rows, col_im] = cw[k][1].astype(BF16)
            lags = range(L) if d == 0 else range(L - 1, -1, -1)
            c_stack = jnp.concatenate(
                [jnp.concatenate([cw[k][0], cw[k][1]], axis=-1) for k in lags], axis=0)
            strips.append(_dot_split(jnp.concatenate([bbr, bbi], axis=-1),
                                     _split_bf16(c_stack), nt_dims))
        lag_strip.append(strips)

    strip_w = 2 * L * PAIR_BLK
    mid = (L - 1) * PAIR_BLK
    widen = lambda a: jnp.concatenate([a, jnp.zeros((a.shape[0], strip_w - a.shape[1]), F32)], axis=-1)
    row = lax.broadcasted_iota(jnp.int32, (PAIR_BLK, strip_w), 0)
    lane = lax.broadcasted_iota(jnp.int32, (PAIR_BLK, strip_w), 1)
    d_row = widen(dsk_ref[0])
    for pr in range(S5_PAIRS):
        strip = (widen(lag_strip[1][pr])
                 + pltpu.roll(widen(lag_strip[0][pr]), mid, axis=1)
                 + jnp.where(lane == row + mid,
                             pltpu.roll(d_row, mid - pr * PAIR_BLK, axis=1), 0.0))
        for s in range(L):
            shift = (strip_w - (L - 1 - s) * PAIR_BLK) % strip_w
            blk = pltpu.roll(strip, shift, axis=1) if shift else strip
            t_ref[0, pr, s * PAIR_BLK:(s + 1) * PAIR_BLK, :] = blk[:, :L * PAIR_BLK].astype(BF16)


def _s5_prep(lam_re, lam_im, log_dt, b_re, b_im, c_re, c_im, d_skip):
    L = S5_CHUNK
    n_st = S5_G * S5_P
    pair_w = L * PAIR_BLK
    rows = jnp.stack([lam_re.reshape(2, 1, n_st), lam_im.reshape(2, 1, n_st),
                      jnp.repeat(log_dt, S5_P, axis=1).reshape(2, 1, n_st)], axis=1)
    b_spec = pl.BlockSpec((2, S5_ST, S5_GC), lambda q: (0, q, 0))
    c_spec = pl.BlockSpec((2, LANES, S5_P), lambda q: (0, q, 0))
    return pl.pallas_call(
        _s5_prep_kernel,
        grid=(S5_QT,),
        in_specs=[
            pl.BlockSpec((2, 3, 1, S5_ST), lambda q: (0, 0, 0, q)),
            b_spec, b_spec, c_spec, c_spec,
            pl.BlockSpec((1, 1, LANES), lambda q: (q, 0, 0)),
        ],
        out_specs=[
            pl.BlockSpec((1, S5_PAIRS, pair_w, pair_w), lambda q: (q, 0, 0, 0)),
            pl.BlockSpec((1, S5_PAIRS, pair_w, 4 * LANES), lambda q: (q, 0, 0, 0)),
            pl.BlockSpec((1, S5_PAIRS, pair_w, 4 * LANES), lambda q: (q, 0, 0, 0)),
            pl.BlockSpec((2, 1, S5_ST), lambda q: (q, 0, 0)),
            pl.BlockSpec((2, 1, S5_ST), lambda q: (q, 0, 0)),
        ],
        out_shape=[
            jax.ShapeDtypeStruct((S5_QT, S5_PAIRS, pair_w, pair_w), BF16),
            jax.ShapeDtypeStruct((S5_QT, S5_PAIRS, pair_w, 4 * LANES), BF16),
            jax.ShapeDtypeStruct((S5_QT, S5_PAIRS, pair_w, 4 * LANES), BF16),
            jax.ShapeDtypeStruct((2 * S5_QT, 1, S5_ST), F32),
            jax.ShapeDtypeStruct((2 * S5_QT, 1, S5_ST), F32),
        ],
        compiler_params=pltpu.CompilerParams(vmem_limit_bytes=VMEM_LIMIT),
        name="s5_prep",
    )(rows, b_re.reshape(2, n_st, S5_GC), b_im.reshape(2, n_st, S5_GC),
      c_re.reshape(2, S5_W, S5_P), c_im.reshape(2, S5_W, S5_P), d_skip.reshape(S5_QT, 1, LANES))


N_SLAB = S5_ST // LANES


def _cmul(ar, ai, br, bi):
    return ar * br - ai * bi, ar * bi + ai * br


def _cpow(ar, ai, n):
    res = None
    while n:
        if n & 1:
            res = (ar, ai) if res is None else _cmul(res[0], res[1], ar, ai)
        n >>= 1
        if n:
            ar, ai = _cmul(ar, ai, ar, ai)
    return res


def _scan_boundary_states(loc_ref, out_ref, ar_ref, ai_ref, backward):
    a_row = [(ar_ref[0][:, k * LANES:(k + 1) * LANES], ai_ref[0][:, k * LANES:(k + 1) * LANES])
             for k in range(N_SLAB)]
    a_tile = [(jnp.broadcast_to(r, (N_SEG, LANES)), jnp.broadcast_to(i, (N_SEG, LANES)))
              for r, i in a_row]

    def sweep(lo, hi, carry, store):
        def step(i, carry):
            off = (SEG_CHUNKS - 1 - i) if backward else i
            rows = pl.ds(off, N_SEG, stride=SEG_CHUNKS)
            new = []
            for k, (cr, ci) in enumerate(carry):
                xr = loc_ref[k, rows, :]
                xi = loc_ref[N_SLAB + k, rows, :]
                if store:
                    out_ref[k, rows, :] = cr
                    out_ref[N_SLAB + k, rows, :] = ci
                nr, ni = _cmul(a_tile[k][0], a_tile[k][1], cr, ci)
                new.append((nr + xr, ni + xi))
            return tuple(new)
        return lax.fori_loop(lo, hi, step, carry)

    def full(carry, store):
        if not backward:
            return sweep(0, SEG_CHUNKS, carry, store)
        carry = sweep(0, BWD_RESET_STEP, carry, store)
        first = lax.broadcasted_iota(jnp.int32, (N_SEG, LANES), 0) == 0
        carry = tuple((jnp.where(first, 0.0, cr), jnp.where(first, 0.0, ci)) for cr, ci in carry)
        return sweep(BWD_RESET_STEP, SEG_CHUNKS, carry, store)

    zero_tile = jnp.zeros((N_SEG, LANES), F32)
    ends = full(tuple((zero_tile, zero_tile) for _ in range(N_SLAB)), False)

    starts = []
    for k, (er, ei) in enumerate(ends):
        pr, pi = _cpow(a_row[k][0], a_row[k][1], SEG_CHUNKS)
        if backward:
            order = range(N_SEG - 1, -1, -1)
            g = (er[0:1], ei[0:1])
        else:
            order = range(N_SEG)
            g = (jnp.zeros((1, LANES), F32), jnp.zeros((1, LANES), F32))
        rows_r = [None] * N_SEG
        rows_i = [None] * N_SEG
        for j in order:
            rows_r[j], rows_i[j] = g
            nr, ni = _cmul(pr, pi, g[0], g[1])
            g = (nr + er[j:j + 1], ni + ei[j:j + 1])
        starts.append((jnp.concatenate(rows_r, axis=0), jnp.concatenate(rows_i, axis=0)))
    full(tuple(starts), True)


def _s5_out_kernel(vl_ref, vc_ref, ar_ref, ai_ref, e_ref, t_ref, ft_ref, yl_ref, yc_ref,
                   acc_ref, st_ref):
    L = S5_CHUNK
    d = pl.program_id(2)
    rows = [jnp.concatenate([vc_ref[pr], vl_ref[pr]], axis=0) for pr in range(S5_PAIRS)]
    for pr in range(S5_PAIRS):
        st = jnp.dot(rows[pr], e_ref[0, pr], preferred_element_type=F32)
        st_ref[pr] = st[:, :LANES]
        st_ref[N_SLAB + pr] = st[:, LANES:]

    def from_states(pr):
        st = jnp.concatenate([st_ref[pr], st_ref[N_SLAB + pr]], axis=-1).astype(BF16)
        return lax.dot_general(st, ft_ref[0, pr], (((1,), (1,)), ((), ())),
                               preferred_element_type=F32)

    @pl.when(d == 0)
    def _():
        _scan_boundary_states(st_ref, st_ref, ar_ref, ai_ref, False)
        for pr in range(S5_PAIRS):
            acc_ref[pr] = from_states(pr) + jnp.dot(rows[pr], t_ref[0, pr],
                                                    preferred_element_type=F32)

    @pl.when(d == 1)
    def _():
        _scan_boundary_states(st_ref, st_ref, ar_ref, ai_ref, True)
        ys = [acc_ref[pr] + from_states(pr) for pr in range(S5_PAIRS)]
        tiles = [tile for k in range(L // S5_PAIRS) for tile in _block_transpose4(
            [y[:, k * LANES:(k + 1) * LANES] for y in ys])]
        for t, tile in enumerate(tiles):
            yc_ref[pl.ds(t, CTX_CHUNKS, stride=L), :] = tile[:CTX_CHUNKS]
            yl_ref[pl.ds(t, LAT_CHUNKS, stride=L), :] = tile[CTX_CHUNKS:]


def _s5_out(vp, al_re, al_im, e_mat, t_mat, ft_mat):
    pair_w = S5_CHUNK * PAIR_BLK
    ctx_blk0 = (R_LAT // S5_CHUNK) // CTX_CHUNKS
    return pl.pallas_call(
        _s5_out_kernel,
        grid=(S5_QT, B, 2),
        in_specs=[
            pl.BlockSpec((S5_PAIRS, LAT_CHUNKS, pair_w), lambda q, b, d: (q, b, 0)),
            pl.BlockSpec((S5_PAIRS, CTX_CHUNKS, pair_w), lambda q, b, d: (q, ctx_blk0 + b, 0)),
            pl.BlockSpec((1, 1, S5_ST), lambda q, b, d: (2 * q + d, 0, 0)),
            pl.BlockSpec((1, 1, S5_ST), lambda q, b, d: (2 * q + d, 0, 0)),
            pl.BlockSpec((1, S5_PAIRS, pair_w, 2 * LANES), lambda q, b, d: (q, 0, 0, d)),
            pl.BlockSpec((1, S5_PAIRS, pair_w, pair_w), lambda q, b, d: (q, 0, 0, 0)),
            pl.BlockSpec((1, S5_PAIRS, pair_w, 2 * LANES), lambda q, b, d: (q, 0, 0, d)),
        ],
        out_specs=[
            pl.BlockSpec((N_LAT, LANES), lambda q, b, d: (b, q)),
            pl.BlockSpec((N_CTX, LANES), lambda q, b, d: (b, q)),
        ],
        out_shape=[
            jax.ShapeDtypeStruct((R_LAT, S5_W), F32),
            jax.ShapeDtypeStruct((R_CTX, S5_W), F32),
        ],
        scratch_shapes=[pltpu.VMEM((S5_PAIRS, BATCH_CHUNKS, pair_w), F32),
                        pltpu.VMEM((2 * N_SLAB, BATCH_CHUNKS, LANES), F32)],
        compiler_params=pltpu.CompilerParams(vmem_limit_bytes=VMEM_LIMIT),
        name="s5_out",
    )(vp, vp, al_re, al_im, e_mat, t_mat, ft_mat)


def _tail0_stages(z, ys, rest_ref, gate, gluw_ref, glub_ref, sg_g_ref, sg_b_ref, sgw_ref,
                  sgbias_ref, wout_ref, lng_ref, lnb_ref):
    ga = rest_ref[:, 0:S5_W].astype(F32)
    u = rest_ref[:, S5_W:2 * S5_W].astype(F32)
    v = rest_ref[:, 2 * S5_W:3 * S5_W].astype(F32)
    gb = rest_ref[:, 3 * S5_W:4 * S5_W].astype(F32)

    ya = _gelu_tanh(ys)
    glu = jnp.dot(ya.astype(BF16), gluw_ref[...], preferred_element_type=F32) + glub_ref[...]
    yield None
    ya = ya * _sigmoid(glu) * _silu(ga)
    yield None

    vn = _layer_norm(v, sg_g_ref[...], sg_b_ref[...]).astype(BF16)
    lane = lax.broadcasted_iota(jnp.int32, (SG_CHUNK, LANES), 1)
    first_head = lane < (SG_W // SG_HEADS)
    chunks = []
    for c in range(TN // SG_CHUNK):
        tiles = []
        for j in range(SG_W // LANES):
            vt = vn[c * SG_CHUNK:(c + 1) * SG_CHUNK, j * LANES:(j + 1) * LANES]
            zero = jnp.zeros_like(vt)
            v2 = jnp.concatenate([jnp.where(first_head, vt, zero), jnp.where(first_head, zero, vt)],
                                 axis=0)
            tiles.append(jnp.dot(sgw_ref[j], v2, preferred_element_type=F32))
        chunks.append(jnp.concatenate(tiles, axis=-1) + sgbias_ref[...])
    s = jnp.concatenate(chunks, axis=0)
    yb = u * s * _silu(gb)
    yield None

    mix = jnp.concatenate([ya, yb], axis=-1).astype(BF16)
    y = jnp.dot(mix, wout_ref[...], preferred_element_type=F32) * gate
    yield None
    yield _layer_norm(ALPHA * z + y, lng_ref[...], lnb_ref[...])


def _rope_tables():
    nf = HD // 4
    n_rows = N_LAT // GRID_W
    lane = jnp.arange(LANES)
    inv = ROPE_BASE ** (-(lane % nf).astype(F32) / nf)
    by_row = ((lane % HD) // (HD // 2) == 0)[None, :]
    sign = jnp.where((lane % (HD // 2)) < nf, -1.0, 1.0)[None, :]
    row_ang = jnp.arange(n_rows, dtype=F32)[:, None] * inv[None, :]
    col_ang = jnp.arange(GRID_W, dtype=F32)[:, None] * inv[None, :]
    zero = jnp.zeros((), F32)
    row_tab = jnp.stack([jnp.where(by_row, jnp.cos(row_ang), zero),
                         jnp.where(by_row, sign * jnp.sin(row_ang), zero)])
    col_tab = jnp.stack([jnp.where(by_row, zero, jnp.cos(col_ang)),
                         jnp.where(by_row, zero, sign * jnp.sin(col_ang))])
    return row_tab, col_tab


def _rope_tile(x, cos, sin, first_half):
    nf = HD // 4
    partner = jnp.where(first_half, pltpu.roll(x, LANES - nf, axis=1), pltpu.roll(x, nf, axis=1))
    return x * cos + partner * sin


def _mid_kernel(x_ref, ctx_ref, mod0_ref, mod1_ref, ysl_ref, ysc_ref, rest_ref, gluw_ref, glub_ref,
                sg_g_ref, sg_b_ref, sgw_ref, sgbias_ref, wout_ref, lng_ref, lnb_ref, w_ref,
                rtab_ref, ctab_ref, z1_ref, q_ref, k_ref, vt_ref, g_ref, zprev_ref, wbf_scr,
                wout_scr):
    i = pl.program_id(0)
    dq = N_HEADS * HD
    dkv = N_KV * HD

    @pl.when(i == 0)
    def _():
        zprev_ref[...] = jnp.zeros((TN, D), F32)
        wbf_scr[...] = w_ref[...].astype(BF16)
        wout_scr[...] = wout_ref[...].astype(BF16)

    j = jnp.maximum(i - 1, 0)
    m1 = mod1_ref[pl.ds(_block_mod_row(j), 1), :]
    h = (zprev_ref[...] * (1.0 + m1[:, D:2 * D]) + m1[:, :D]).astype(BF16)
    grid_row0 = (j % LAT_BLK_PER_BATCH) * (TN // GRID_W)
    table = lambda t: jnp.concatenate(
        [rtab_ref[t, pl.ds(grid_row0 + r, 1), :] + ctab_ref[t] for r in range(TN // GRID_W)],
        axis=0)
    cos = jnp.where(j < N_LAT_BLK, table(0), 1.0)
    sin = jnp.where(j < N_LAT_BLK, table(1), 0.0)
    lane = lax.broadcasted_iota(jnp.int32, (TN, LANES), 1)
    first_half = (lane % (HD // 2)) < (HD // 4)
    scale = HD ** -0.5 * LOG2E

    def project(c0, c1):
        return jnp.dot(h, wbf_scr[:, c0:c1], preferred_element_type=F32)

    def roped_tiles(p, mult):
        for c in range(p.shape[1] // LANES):
            r = _rope_tile(p[:, c * LANES:(c + 1) * LANES], cos, sin, first_half)
            yield c, (r * mult if mult != 1.0 else r).astype(BF16)

    t = jnp.minimum(i, N_BLK - 1)
    z = jnp.where(t < N_LAT_BLK, x_ref[...], ctx_ref[...])
    ys = jnp.where(t < N_LAT_BLK, ysl_ref[...], ysc_ref[...])
    gate = mod0_ref[pl.ds(_block_mod_row(t), 1), 2 * D:3 * D]
    tail = _tail0_stages(z, ys, rest_ref, gate, gluw_ref, glub_ref, sg_g_ref, sg_b_ref, sgw_ref,
                         sgbias_ref, wout_scr, lng_ref, lnb_ref)

    half_q = dq // 2
    p_q0 = project(0, half_q)
    next(tail)
    p_q1 = project(half_q, dq)
    next(tail)
    p_k = project(dq, dq + dkv)
    vt_ref[...] = lax.dot_general(wbf_scr[:, dq + dkv:dq + 2 * dkv], h, (((0,), (1,)), ((), ())),
                                  preferred_element_type=F32).astype(BF16)
    next(tail)
    next(tail)
    for c0, p in ((0, p_q0), (half_q, p_q1)):
        for c, r in roped_tiles(p, scale):
            q_ref[:, c0 + c * LANES:c0 + (c + 1) * LANES] = r
    for c, r in roped_tiles(p_k, 1.0):
        k_ref[:, c * LANES:(c + 1) * LANES] = r
    p_g = project(dq + 2 * dkv, ODD_IN)
    z1 = next(tail)
    z1_ref[...] = z1
    zprev_ref[...] = z1
    g_ref[...] = p_g.astype(BF16)


def _mid(x2, ctx2, mod0, mod1, ys_lat, ys_ctx, rest, glu_w, glu_b, sg_g, sg_b, sg_w, sg_bias,
         w_out, ln_g, ln_b, w_in1, row_tab, col_tab):
    dq = N_HEADS * HD
    dkv = N_KV * HD
    once = dict(pipeline_mode=pl.Buffered(1))
    row = lambda n: pl.BlockSpec((1, n), lambda i: (0, 0))
    lat = lambda i: (jnp.minimum(i, N_LAT_BLK - 1), 0)
    tail_blk = lambda i: (jnp.minimum(i, N_BLK - 1), 0)
    proj_blk = lambda i: (jnp.maximum(i - 1, 0), 0)
    return pl.pallas_call(
        _mid_kernel,
        grid=(N_BLK + 1,),
        in_specs=[
            pl.BlockSpec((TN, D), lat),
            pl.BlockSpec((R_CTX, D), lambda i: (0, 0), **once),
            pl.BlockSpec((8, 3 * D), lambda i: (0, 0)),
            pl.BlockSpec((8, 3 * D), lambda i: (0, 0)),
            pl.BlockSpec((TN, S5_W), lat),
            pl.BlockSpec((R_CTX, S5_W), lambda i: (0, 0), **once),
            pl.BlockSpec((TN, EVEN_IN - S5_W), tail_blk),
            pl.BlockSpec((S5_W, S5_W), lambda i: (0, 0), **once),
            row(S5_W), row(SG_W), row(SG_W),
            pl.BlockSpec((SG_HEADS // 2, SG_CHUNK, 2 * SG_CHUNK), lambda i: (0, 0, 0)),
            pl.BlockSpec((SG_CHUNK, SG_W), lambda i: (0, 0)),
            pl.BlockSpec((S5_W + SG_W, D), lambda i: (0, 0), **once),
            row(D), row(D),
            pl.BlockSpec((D, ODD_IN), lambda i: (0, 0), **once),
            pl.BlockSpec((2, N_LAT // GRID_W, LANES), lambda i: (0, 0, 0)),
            pl.BlockSpec((2, GRID_W, LANES), lambda i: (0, 0, 0)),
        ],
        out_specs=[
            pl.BlockSpec((TN, D), tail_blk),
            pl.BlockSpec((TN, dq), proj_blk),
            pl.BlockSpec((TN, dkv), proj_blk),
            pl.BlockSpec((dkv, TN), lambda i: (0, jnp.maximum(i - 1, 0))),
            pl.BlockSpec((TN, dq), proj_blk),
        ],
        out_shape=[
            jax.ShapeDtypeStruct((R_ALL, D), F32),
            jax.ShapeDtypeStruct((R_ALL, dq), BF16),
            jax.ShapeDtypeStruct((R_ALL, dkv), BF16),
            jax.ShapeDtypeStruct((dkv, R_ALL), BF16),
            jax.ShapeDtypeStruct((R_ALL, dq), BF16),
        ],
        scratch_shapes=[pltpu.VMEM((TN, D), F32), pltpu.VMEM((D, ODD_IN), BF16),
                        pltpu.VMEM((S5_W + SG_W, D), BF16)],
        compiler_params=pltpu.CompilerParams(vmem_limit_bytes=VMEM_LIMIT),
        name="mid",
    )(x2, ctx2, mod0, mod1, ys_lat, ys_ctx, rest, glu_w, glu_b, sg_g, sg_b, sg_w, sg_bias, w_out,
      ln_g, ln_b, w_in1, row_tab, col_tab)


N_QBLK = N_LAT // ATT_BLK
GRP = N_HEADS // N_KV
ATT_SUB = 4
ATT_ROWS = ATT_SUB * ATT_BLK
N_QSTEP = N_QBLK // ATT_SUB
N_ATT_STEPS = B * N_QSTEP
assert ATT_ROWS == TN
assert WINDOW == ATT_BLK
ONES_ROWS = 16


def _attn_kernel(sink_ref, q_ref, kp_ref, kc_ref, kn_ref, kx_ref, vp_ref, vc_ref, vn_ref, vx_ref,
                 g_ref, z_ref, mod_ref, wout_ref, lng_ref, lnb_ref, out_ref, o_scr, wout_scr):
    step = pl.program_id(0)

    @pl.when(step == 0)
    def _():
        o_scr[...] = jnp.zeros((ATT_ROWS, N_HEADS * HD), BF16)
        wout_scr[...] = wout_ref[...].astype(BF16)

    t = jnp.maximum(step - 1, 0)
    gate = mod_ref[pl.ds(_block_mod_row(t), 1), 2 * D:3 * D]
    mix = (o_scr[...].astype(F32) * _silu(g_ref[...].astype(F32))).astype(BF16)
    y = jnp.dot(mix, wout_scr[...], preferred_element_type=F32) * gate
    out_ref[...] = _layer_norm(ALPHA * z_ref[...] + y, lng_ref[...], lnb_ref[...])

    i = jnp.minimum(step, N_ATT_STEPS - 1) % N_QSTEP
    n_win = 3 * ATT_BLK
    n_keys = n_win + N_CTX
    nq = GRP * ATT_BLK
    k_win = jnp.concatenate([kp_ref[...], kc_ref[...], kn_ref[...]], axis=0)
    vt_win = jnp.concatenate([vp_ref[...], vc_ref[...], vn_ref[...]], axis=1)
    k_ctx = kx_ref[...]
    vt_ctx = vx_ref[...]

    kpos = lax.broadcasted_iota(jnp.int32, (ATT_BLK, ATT_BLK), 0)
    qpos = lax.broadcasted_iota(jnp.int32, (ATT_BLK, ATT_BLK), 1)
    tile_q = lambda a: jnp.concatenate([a] * GRP, axis=1)
    qgrp = lax.broadcasted_iota(jnp.int32, (1, nq), 1) // ATT_BLK
    ones = jnp.ones((ONES_ROWS, n_keys), BF16)
    q = q_ref[...]

    units = [(u, h) for u in range(ATT_SUB) for h in range(N_KV)]
    k_all, vt_all, bias = [], [], []
    for u in range(ATT_SUB):
        k_all.append(jnp.concatenate([k_win[u * ATT_BLK:u * ATT_BLK + n_win], k_ctx], axis=0))
        vt_all.append(jnp.concatenate([vt_win[:, u * ATT_BLK:u * ATT_BLK + n_win], vt_ctx], axis=1))
        has_prev = (i > 0) if u == 0 else True
        has_next = (i < N_QSTEP - 1) if u == ATT_SUB - 1 else True
        bias.append((tile_q(jnp.where((kpos >= qpos) & has_prev, 0.0, NEG_INF)),
                     tile_q(jnp.where((kpos <= qpos) & has_next, 0.0, NEG_INF))))
    scores = []
    for u, h in units:
        kh = k_all[u][:, h * HD:(h + 1) * HD]
        qh = jnp.concatenate(
            [q[u * ATT_BLK:(u + 1) * ATT_BLK, (h * GRP + g) * HD:(h * GRP + g + 1) * HD]
             for g in range(GRP)], axis=0)
        scores.append(lax.dot_general(kh, qh, (((1,), (1,)), ((), ())),
                                      preferred_element_type=F32))
    probs = []
    for (u, h), s in zip(units, scores):
        s = jnp.concatenate([s[:ATT_BLK] + bias[u][0], s[ATT_BLK:2 * ATT_BLK],
                             s[2 * ATT_BLK:n_win] + bias[u][1], s[n_win:]], axis=0)
        sink = jnp.zeros((1, nq), F32)
        for g in range(GRP):
            sink = jnp.where(qgrp == g, sink_ref[h * GRP + g] * LOG2E, sink)
        m = jnp.maximum(jnp.max(s, axis=0, keepdims=True), sink)
        probs.append((jnp.exp2(s - m).astype(BF16), jnp.exp2(sink - m)))
    outs = [[] for _ in range(ATT_SUB)]
    for (u, h), (p, p_sink) in zip(units, probs):
        vt1 = jnp.concatenate([vt_all[u][h * HD:(h + 1) * HD], ones], axis=0)
        ov = jnp.dot(vt1, p, preferred_element_type=F32)
        o_t = ov[:HD] / (ov[HD:HD + 1] + p_sink)
        outs[u].extend(o_t[:, g * ATT_BLK:(g + 1) * ATT_BLK] for g in range(GRP))
    for u in range(ATT_SUB):
        o_scr[u * ATT_BLK:(u + 1) * ATT_BLK, :] = jnp.transpose(
            jnp.concatenate(outs[u], axis=0)).astype(BF16)


def _attention_tail(sink, q, k, vt, g, z1, mod, w_out, ln_g, ln_b):
    dq = N_HEADS * HD
    dkv = N_KV * HD
    ctx_blk0 = R_LAT // N_CTX
    cur = lambda s: jnp.minimum(s, N_ATT_STEPS - 1)
    batch = lambda s: cur(s) // N_QSTEP
    qstep = lambda s: cur(s) % N_QSTEP
    prev_blk = lambda s: batch(s) * N_QBLK + jnp.maximum(ATT_SUB * qstep(s) - 1, 0)
    next_blk = lambda s: batch(s) * N_QBLK + jnp.minimum(ATT_SUB * (qstep(s) + 1), N_QBLK - 1)
    tail = lambda s: jnp.maximum(s - 1, 0)
    once = dict(pipeline_mode=pl.Buffered(1))
    row = lambda n: pl.BlockSpec((1, n), lambda s, sk: (0, 0))
    return pl.pallas_call(
        _attn_kernel,
        grid_spec=pltpu.PrefetchScalarGridSpec(
            num_scalar_prefetch=1,
            grid=(N_ATT_STEPS + 1,),
            in_specs=[
                pl.BlockSpec((ATT_ROWS, dq), lambda s, sk: (cur(s), 0)),
                pl.BlockSpec((ATT_BLK, dkv), lambda s, sk: (prev_blk(s), 0)),
                pl.BlockSpec((ATT_ROWS, dkv), lambda s, sk: (cur(s), 0)),
                pl.BlockSpec((ATT_BLK, dkv), lambda s, sk: (next_blk(s), 0)),
                pl.BlockSpec((N_CTX, dkv), lambda s, sk: (ctx_blk0 + batch(s), 0)),
                pl.BlockSpec((dkv, ATT_BLK), lambda s, sk: (0, prev_blk(s))),
                pl.BlockSpec((dkv, ATT_ROWS), lambda s, sk: (0, cur(s))),
                pl.BlockSpec((dkv, ATT_BLK), lambda s, sk: (0, next_blk(s))),
                pl.BlockSpec((dkv, N_CTX), lambda s, sk: (0, ctx_blk0 + batch(s))),
                pl.BlockSpec((ATT_ROWS, dq), lambda s, sk: (tail(s), 0)),
                pl.BlockSpec((ATT_ROWS, D), lambda s, sk: (tail(s), 0)),
                pl.BlockSpec((8, 3 * D), lambda s, sk: (0, 0)),
                pl.BlockSpec((dq, D), lambda s, sk: (0, 0), **once),
                row(D), row(D),
            ],
            out_specs=pl.BlockSpec((ATT_ROWS, D), lambda s, sk: (tail(s), 0)),
            scratch_shapes=[pltpu.VMEM((ATT_ROWS, dq), BF16), pltpu.VMEM((dq, D), BF16)],
        ),
        out_shape=jax.ShapeDtypeStruct((R_LAT, D), F32),
        compiler_params=pltpu.CompilerParams(vmem_limit_bytes=VMEM_LIMIT),
        name="attention",
    )(sink, q, k, k, k, k, vt, vt, vt, vt, g, z1, mod, w_out, ln_g, ln_b)


def kernel(x, c, ctx, c_ctx, mod_w, mod_b, ln_g, ln_b, e_w_in, e_w_out, s5_lam_re, s5_lam_im,
           s5_log_dt, s5_b_re, s5_b_im, s5_c_re, s5_c_im, s5_d, glu_w, glu_b, sg_ln_g, sg_ln_b,
           sg_w, sg_b, o_w_in, o_w_out, o_sink):
    x2 = x.reshape(R_LAT, D)
    ctx2 = ctx.reshape(R_CTX, D)
    mod = _modulation(c, c_ctx, mod_w, mod_b)

    vp, rest = _in0(x2, ctx2, mod[0], e_w_in[0])
    t_mat, e_mat, ft_mat, al_re, al_im = _s5_prep(
        s5_lam_re[0], s5_lam_im[0], s5_log_dt[0], s5_b_re[0], s5_b_im[0], s5_c_re[0], s5_c_im[0],
        s5_d[0])
    ys_lat, ys_ctx = _s5_out(vp, al_re, al_im, e_mat, t_mat, ft_mat)
    sg_bias = jnp.repeat(jnp.transpose(sg_b[0]), SG_W // SG_HEADS, axis=1)
    sg_w2 = jnp.transpose(sg_w[0].reshape(SG_HEADS // 2, 2, SG_CHUNK, SG_CHUNK), (0, 2, 1, 3))
    sg_w2 = sg_w2.reshape(SG_HEADS // 2, SG_CHUNK, 2 * SG_CHUNK).astype(BF16)
    row_tab, col_tab = _rope_tables()
    z1, q, k, vt, g = _mid(
        x2, ctx2, mod[0], mod[1], ys_lat, ys_ctx, rest, glu_w[0].astype(BF16),
        glu_b[0].reshape(1, S5_W), sg_ln_g[0].reshape(1, SG_W), sg_ln_b[0].reshape(1, SG_W),
        sg_w2, sg_bias, e_w_out[0], ln_g[0].reshape(1, D),
        ln_b[0].reshape(1, D), o_w_in[0], row_tab, col_tab)

    out = _attention_tail(o_sink[0], q, k, vt, g, z1, mod[1], o_w_out[0],
                          ln_g[1].reshape(1, D), ln_b[1].reshape(1, D))
    return out.reshape(B, N_LAT, D)
```

```python
import math

import jax
import jax.numpy as jnp
from jax import lax
from jax.experimental import pallas as pl
from jax.experimental.pallas import tpu as pltpu

F32 = jnp.float32
BF16 = jnp.bfloat16

D = 1024
B = 2
N_LAT = 8192
N_CTX = 256
DEPTH = 2
GRID_W = 64
S5_W = 512
S5_GC = 16
S5_G = 32
S5_P = 64
SG_W = 512
SG_HEADS = 8
SG_CHUNK = 128
N_HEADS = 16
N_KV = 4
HD = 64
WINDOW = 128
ATT_BLK = 128
ROPE_BASE = 10000.0
NEG_INF = -1e30
LN_EPS = 1e-5
ALPHA = (2 * DEPTH) ** 0.25
LOG2E = math.log2(math.e)
EVEN_IN = 2 * S5_W + 3 * SG_W
ODD_IN = 2 * N_HEADS * HD + 2 * N_KV * HD

LANES = 128
VMEM_LIMIT = 56 * 1024 * 1024

R_LAT = B * N_LAT
R_CTX = B * N_CTX
R_ALL = R_LAT + R_CTX
TN = 512
N_LAT_BLK = R_LAT // TN
N_BLK = R_ALL // TN
LAT_BLK_PER_BATCH = N_LAT // TN
CTX_MOD_ROW = B

S5_CHUNK = 8
S5_QT = S5_W // LANES
S5_GPT = LANES // S5_GC
S5_ST = S5_GPT * S5_P
S5_PAIRS = S5_GPT // 2
PAIR_BLK = 2 * S5_GC
LAT_CHUNKS = N_LAT // S5_CHUNK
CTX_CHUNKS = N_CTX // S5_CHUNK
BATCH_CHUNKS = CTX_CHUNKS + LAT_CHUNKS
N_SEG = 16
SEG_CHUNKS = BATCH_CHUNKS // N_SEG
BWD_RESET_STEP = LAT_CHUNKS - (N_SEG - 1) * SEG_CHUNKS


def _sigmoid(x):
    return 1.0 / (1.0 + jnp.exp(-x))


def _silu(x):
    return x * _sigmoid(x)


def _gelu_tanh(x):
    return 0.5 * x * (1.0 + jnp.tanh(math.sqrt(2.0 / math.pi) * (x + 0.044715 * (x * x * x))))


def _layer_norm(x, g, b):
    mu = jnp.mean(x, axis=-1, keepdims=True)
    xc = x - mu
    var = jnp.mean(xc * xc, axis=-1, keepdims=True)
    return xc * lax.rsqrt(var + LN_EPS) * g + b


def _split_bf16(x):
    hi = x.astype(BF16)
    return hi, (x - hi.astype(F32)).astype(BF16)


def _dot_split(x, y_split, dims):
    x_hi, x_lo = _split_bf16(x)
    y_hi, y_lo = y_split
    dot = lambda a, b: lax.dot_general(a, b, dims, preferred_element_type=F32)
    return dot(x_hi, y_hi) + (dot(x_hi, y_lo) + dot(x_lo, y_hi))


def _block_mod_row(i):
    return jnp.where(i < N_LAT_BLK, i // LAT_BLK_PER_BATCH, CTX_MOD_ROW)


def _mod_kernel(cv_ref, w_ref, b_ref, o_ref):
    s = _silu(cv_ref[...])
    o_ref[0] = _dot_split(s, _split_bf16(w_ref[0]), (((1,), (0,)), ((), ()))) + b_ref[0]


def _modulation(c, c_ctx, mod_w, mod_b):
    cv = jnp.concatenate([c, c_ctx[None], jnp.zeros((8 - B - 1, D), F32)], axis=0)
    tn = 1024
    return pl.pallas_call(
        _mod_kernel,
        grid=(DEPTH, 3 * D // tn),
        in_specs=[
            pl.BlockSpec((8, D), lambda l, j: (0, 0)),
            pl.BlockSpec((1, D, tn), lambda l, j: (l, 0, j)),
            pl.BlockSpec((1, 1, tn), lambda l, j: (l, 0, j)),
        ],
        out_specs=pl.BlockSpec((1, 8, tn), lambda l, j: (l, 0, j)),
        out_shape=jax.ShapeDtypeStruct((DEPTH, 8, 3 * D), F32),
        compiler_params=pltpu.CompilerParams(vmem_limit_bytes=VMEM_LIMIT),
        name="modulation",
    )(cv, mod_w, mod_b.reshape(DEPTH, 1, 3 * D))


def _block_transpose4(tiles):
    tiles = list(tiles)
    blk = lax.broadcasted_iota(jnp.int32, tiles[0].shape, 1) // PAIR_BLK
    for dist in (2, 1):
        keep = (blk & dist) == 0
        for i in range(len(tiles)):
            if i & dist:
                continue
            lo, hi = tiles[i], tiles[i + dist]
            tiles[i] = jnp.where(keep, lo, pltpu.roll(hi, dist * PAIR_BLK, axis=1))
            tiles[i + dist] = jnp.where(keep, pltpu.roll(lo, LANES - dist * PAIR_BLK, axis=1), hi)
    return tiles


def _in0_kernel(x_ref, ctx_ref, mod_ref, w_ref, vp_ref, rest_ref, xa_scr, wbf_scr):
    L = S5_CHUNK
    i = pl.program_id(0)

    @pl.when(i == 0)
    def _():
        wbf_scr[...] = w_ref[...].astype(BF16)

    z = jnp.where(i < N_LAT_BLK, x_ref[...], ctx_ref[...])
    m = mod_ref[pl.ds(_block_mod_row(i), 1), :]
    h = z * (1.0 + m[:, D:2 * D]) + m[:, :D]
    p = jnp.dot(h.astype(BF16), wbf_scr[...], preferred_element_type=F32)
    rest_ref[...] = p[:, S5_W:].astype(BF16)
    for q in range(S5_QT):
        xa_scr[q] = p[:, q * LANES:(q + 1) * LANES]
        tiles = [xa_scr[q, pl.ds(t, TN // L, stride=L), :] for t in range(L)]
        halves = [_block_transpose4(tiles[k * S5_PAIRS:(k + 1) * S5_PAIRS])
                  for k in range(L // S5_PAIRS)]
        for pr in range(S5_PAIRS):
            vp_ref[q * S5_PAIRS + pr] = jnp.concatenate(
                [half[pr] for half in halves], axis=-1).astype(BF16)


def _in0(x2, ctx2, mod, w_in):
    pair_w = S5_CHUNK * PAIR_BLK
    return pl.pallas_call(
        _in0_kernel,
        grid=(N_BLK,),
        in_specs=[
            pl.BlockSpec((TN, D), lambda i: (jnp.minimum(i, N_LAT_BLK - 1), 0)),
            pl.BlockSpec((R_CTX, D), lambda i: (0, 0)),
            pl.BlockSpec((8, 3 * D), lambda i: (0, 0)),
            pl.BlockSpec((D, EVEN_IN), lambda i: (0, 0)),
        ],
        out_specs=[
            pl.BlockSpec((S5_QT * S5_PAIRS, TN // S5_CHUNK, pair_w), lambda i: (0, i, 0)),
            pl.BlockSpec((TN, EVEN_IN - S5_W), lambda i: (i, 0)),
        ],
        out_shape=[
            jax.ShapeDtypeStruct((S5_QT * S5_PAIRS, R_ALL // S5_CHUNK, pair_w), BF16),
            jax.ShapeDtypeStruct((R_ALL, EVEN_IN - S5_W), BF16),
        ],
        scratch_shapes=[pltpu.VMEM((S5_QT, TN, LANES), F32), pltpu.VMEM((D, EVEN_IN), BF16)],
        compiler_params=pltpu.CompilerParams(vmem_limit_bytes=VMEM_LIMIT),
        name="in0",
    )(x2, ctx2, mod, w_in)


def _s5_prep_kernel(rows_ref, bre_ref, bim_ref, cre_ref, cim_ref, dsk_ref, t_ref, e_ref, ft_ref,
                    alr_ref, ali_ref):
    L = S5_CHUNK
    tile = (LANES, S5_ST)
    same_group = (lax.broadcasted_iota(jnp.int32, tile, 0) // S5_GC
                  == lax.broadcasted_iota(jnp.int32, tile, 1) // S5_P)
    nt_dims = (((1,), (1,)), ((), ()))
    nn_dims = (((1,), (0,)), ((), ()))
    pick_ch = (lax.broadcasted_iota(jnp.int32, (LANES, S5_GC), 0) % S5_GC
               == lax.broadcasted_iota(jnp.int32, (LANES, S5_GC), 1)).astype(BF16)
    rep_st = (lax.broadcasted_iota(jnp.int32, (S5_P, S5_ST), 0)
              == lax.broadcasted_iota(jnp.int32, (S5_P, S5_ST), 1) % S5_P).astype(BF16)

    def spread_b(b):
        hi, lo = _split_bf16(b)
        dot = lambda v: lax.dot_general(pick_ch, v, nt_dims, preferred_element_type=F32)
        return jnp.where(same_group, dot(hi) + dot(lo), 0.0)

    def spread_c(c):
        hi, lo = _split_bf16(c)
        dot = lambda v: lax.dot_general(v, rep_st, nn_dims, preferred_element_type=F32)
        return jnp.where(same_group, dot(hi) + dot(lo), 0.0)

    pair_rows = lambda pr: slice(pr * PAIR_BLK, (pr + 1) * PAIR_BLK)
    pair_lanes = lambda pr: slice(pr * LANES, (pr + 1) * LANES)
    lag_strip = []
    for d in range(2):
        lam_re = rows_ref[d, 0]
        lam_im = rows_ref[d, 1]
        dt = jnp.exp(rows_ref[d, 2])
        pw = []
        for k in range(L + 1):
            mag = jnp.exp(lam_re * dt * float(k))
            ang = lam_im * dt * float(k)
            pw.append((mag * jnp.cos(ang), mag * jnp.sin(ang)))
        alr_ref[d] = pw[L][0]
        ali_ref[d] = pw[L][1]
        den = lam_re * lam_re + lam_im * lam_im
        nr = pw[1][0] - 1.0
        ni = pw[1][1]
        f_re = (nr * lam_re + ni * lam_im) / den
        f_im = (ni * lam_re - nr * lam_im) / den
        b_re = spread_b(bre_ref[d])
        b_im = spread_b(bim_ref[d])
        bb_re = f_re * b_re - f_im * b_im
        bb_im = f_re * b_im + f_im * b_re
        c_re = spread_c(cre_ref[d])
        c_im = spread_c(cim_ref[d])
        col_re = slice(2 * d * LANES, (2 * d + 1) * LANES)
        col_im = slice((2 * d + 1) * LANES, (2 * d + 2) * LANES)
        strips = []
        for pr in range(S5_PAIRS):
            blk = lambda a: a[pair_rows(pr), pair_lanes(pr)]
            p_re = [blk(jnp.broadcast_to(pw[k][0], tile)) for k in range(L + 1)]
            p_im = [blk(jnp.broadcast_to(pw[k][1], tile)) for k in range(L + 1)]
            bbr, bbi, cr, ci = blk(bb_re), blk(bb_im), blk(c_re), blk(c_im)
            for k in range(L):
                s = (L - 1 - k) if d == 0 else k
                rows = slice(s * PAIR_BLK, (s + 1) * PAIR_BLK)
                e_ref[0, pr, rows, col_re] = (p_re[k] * bbr - p_im[k] * bbi).astype(BF16)
                e_ref[0, pr, rows, col_im] = (p_re[k] * bbi + p_im[k] * bbr).astype(BF16)
            cw = [(cr * p_re[k] - ci * p_im[k], -(cr * p_im[k] + ci * p_re[k]))
                  for k in range(L + 1)]
            for t in range(L):
                k = (t + 1) if d == 0 else (L - t)
                rows = slice(t * PAIR_BLK, (t + 1) * PAIR_BLK)
                ft_ref[0, pr, rows, col_re] = cw[k][0].astype(BF16)
                ft_ref[0, pr, rows, col_im] = cw[k][1].astype(BF16)
            lags = range(L) if d == 0 else range(L - 1, -1, -1)
            c_stack = jnp.concatenate(
                [jnp.concatenate([cw[k][0], cw[k][1]], axis=-1) for k in lags], axis=0)
            strips.append(_dot_split(jnp.concatenate([bbr, bbi], axis=-1),
                                     _split_bf16(c_stack), nt_dims))
        lag_strip.append(strips)

    strip_w = 2 * L * PAIR_BLK
    mid = (L - 1) * PAIR_BLK
    widen = lambda a: jnp.concatenate([a, jnp.zeros((a.shape[0], strip_w - a.shape[1]), F32)], axis=-1)
    row = lax.broadcasted_iota(jnp.int32, (PAIR_BLK, strip_w), 0)
    lane = lax.broadcasted_iota(jnp.int32, (PAIR_BLK, strip_w), 1)
    d_row = widen(dsk_ref[0])
    for pr in range(S5_PAIRS):
        strip = (widen(lag_strip[1][pr])
                 + pltpu.roll(widen(lag_strip[0][pr]), mid, axis=1)
                 + jnp.where(lane == row + mid,
                             pltpu.roll(d_row, mid - pr * PAIR_BLK, axis=1), 0.0))
        for s in range(L):
            shift = (strip_w - (L - 1 - s) * PAIR_BLK) % strip_w
            blk = pltpu.roll(strip, shift, axis=1) if shift else strip
            t_ref[0, pr, s * PAIR_BLK:(s + 1) * PAIR_BLK, :] = blk[:, :L * PAIR_BLK].astype(BF16)


def _s5_prep(lam_re, lam_im, log_dt, b_re, b_im, c_re, c_im, d_skip):
    L = S5_CHUNK
    n_st = S5_G * S5_P
    pair_w = L * PAIR_BLK
    rows = jnp.stack([lam_re.reshape(2, 1, n_st), lam_im.reshape(2, 1, n_st),
                      jnp.repeat(log_dt, S5_P, axis=1).reshape(2, 1, n_st)], axis=1)
    b_spec = pl.BlockSpec((2, S5_ST, S5_GC), lambda q: (0, q, 0))
    c_spec = pl.BlockSpec((2, LANES, S5_P), lambda q: (0, q, 0))
    return pl.pallas_call(
        _s5_prep_kernel,
        grid=(S5_QT,),
        in_specs=[
            pl.BlockSpec((2, 3, 1, S5_ST), lambda q: (0, 0, 0, q)),
            b_spec, b_spec, c_spec, c_spec,
            pl.BlockSpec((1, 1, LANES), lambda q: (q, 0, 0)),
        ],
        out_specs=[
            pl.BlockSpec((1, S5_PAIRS, pair_w, pair_w), lambda q: (q, 0, 0, 0)),
            pl.BlockSpec((1, S5_PAIRS, pair_w, 4 * LANES), lambda q: (q, 0, 0, 0)),
            pl.BlockSpec((1, S5_PAIRS, pair_w, 4 * LANES), lambda q: (q, 0, 0, 0)),
            pl.BlockSpec((2, 1, S5_ST), lambda q: (q, 0, 0)),
            pl.BlockSpec((2, 1, S5_ST), lambda q: (q, 0, 0)),
        ],
        out_shape=[
            jax.ShapeDtypeStruct((S5_QT, S5_PAIRS, pair_w, pair_w), BF16),
            jax.ShapeDtypeStruct((S5_QT, S5_PAIRS, pair_w, 4 * LANES), BF16),
            jax.ShapeDtypeStruct((S5_QT, S5_PAIRS, pair_w, 4 * LANES), BF16),
            jax.ShapeDtypeStruct((2 * S5_QT, 1, S5_ST), F32),
            jax.ShapeDtypeStruct((2 * S5_QT, 1, S5_ST), F32),
        ],
        compiler_params=pltpu.CompilerParams(vmem_limit_bytes=VMEM_LIMIT),
        name="s5_prep",
    )(rows, b_re.reshape(2, n_st, S5_GC), b_im.reshape(2, n_st, S5_GC),
      c_re.reshape(2, S5_W, S5_P), c_im.reshape(2, S5_W, S5_P), d_skip.reshape(S5_QT, 1, LANES))


N_SLAB = S5_ST // LANES


def _cmul(ar, ai, br, bi):
    return ar * br - ai * bi, ar * bi + ai * br


def _cpow(ar, ai, n):
    res = None
    while n:
        if n & 1:
            res = (ar, ai) if res is None else _cmul(res[0], res[1], ar, ai)
        n >>= 1
        if n:
            ar, ai = _cmul(ar, ai, ar, ai)
    return res


def _scan_boundary_states(loc_ref, out_ref, ar_ref, ai_ref, backward):
    a_row = [(ar_ref[0][:, k * LANES:(k + 1) * LANES], ai_ref[0][:, k * LANES:(k + 1) * LANES])
             for k in range(N_SLAB)]
    a_tile = [(jnp.broadcast_to(r, (N_SEG, LANES)), jnp.broadcast_to(i, (N_SEG, LANES)))
              for r, i in a_row]

    def sweep(lo, hi, carry, store):
        def step(i, carry):
            off = (SEG_CHUNKS - 1 - i) if backward else i
            rows = pl.ds(off, N_SEG, stride=SEG_CHUNKS)
            new = []
            for k, (cr, ci) in enumerate(carry):
                xr = loc_ref[k, rows, :]
                xi = loc_ref[N_SLAB + k, rows, :]
                if store:
                    out_ref[k, rows, :] = cr
                    out_ref[N_SLAB + k, rows, :] = ci
                nr, ni = _cmul(a_tile[k][0], a_tile[k][1], cr, ci)
                new.append((nr + xr, ni + xi))
            return tuple(new)
        return lax.fori_loop(lo, hi, step, carry)

    def full(carry, store):
        if not backward:
            return sweep(0, SEG_CHUNKS, carry, store)
        carry = sweep(0, BWD_RESET_STEP, carry, store)
        first = lax.broadcasted_iota(jnp.int32, (N_SEG, LANES), 0) == 0
        carry = tuple((jnp.where(first, 0.0, cr), jnp.where(first, 0.0, ci)) for cr, ci in carry)
        return sweep(BWD_RESET_STEP, SEG_CHUNKS, carry, store)

    zero_tile = jnp.zeros((N_SEG, LANES), F32)
    ends = full(tuple((zero_tile, zero_tile) for _ in range(N_SLAB)), False)

    starts = []
    for k, (er, ei) in enumerate(ends):
        pr, pi = _cpow(a_row[k][0], a_row[k][1], SEG_CHUNKS)
        if backward:
            order = range(N_SEG - 1, -1, -1)
            g = (er[0:1], ei[0:1])
        else:
            order = range(N_SEG)
            g = (jnp.zeros((1, LANES), F32), jnp.zeros((1, LANES), F32))
        rows_r = [None] * N_SEG
        rows_i = [None] * N_SEG
        for j in order:
            rows_r[j], rows_i[j] = g
            nr, ni = _cmul(pr, pi, g[0], g[1])
            g = (nr + er[j:j + 1], ni + ei[j:j + 1])
        starts.append((jnp.concatenate(rows_r, axis=0), jnp.concatenate(rows_i, axis=0)))
    full(tuple(starts), True)


def _s5_out_kernel(vl_ref, vc_ref, ar_ref, ai_ref, e_ref, t_ref, ft_ref, yl_ref, yc_ref, st_ref):
    L = S5_CHUNK
    n_dir = 2 * N_SLAB
    rows = [jnp.concatenate([vc_ref[pr], vl_ref[pr]], axis=0) for pr in range(S5_PAIRS)]
    for pr in range(S5_PAIRS):
        st = jnp.dot(rows[pr], e_ref[0, pr], preferred_element_type=F32)
        for comp in range(4):
            st_ref[comp * N_SLAB + pr] = st[:, comp * LANES:(comp + 1) * LANES]
    for d in range(2):
        st_d = st_ref.at[pl.ds(d * n_dir, n_dir)]
        _scan_boundary_states(st_d, st_d, ar_ref.at[pl.ds(d, 1)], ai_ref.at[pl.ds(d, 1)], d == 1)

    ys = []
    for pr in range(S5_PAIRS):
        st = jnp.concatenate([st_ref[comp * N_SLAB + pr] for comp in range(4)],
                             axis=-1).astype(BF16)
        ys.append(jnp.dot(rows[pr], t_ref[0, pr], preferred_element_type=F32)
                  + lax.dot_general(st, ft_ref[0, pr], (((1,), (1,)), ((), ())),
                                    preferred_element_type=F32))
    tiles = [tile for k in range(L // S5_PAIRS) for tile in _block_transpose4(
        [y[:, k * LANES:(k + 1) * LANES] for y in ys])]
    for t, tile in enumerate(tiles):
        yc_ref[pl.ds(t, CTX_CHUNKS, stride=L), :] = tile[:CTX_CHUNKS]
        yl_ref[pl.ds(t, LAT_CHUNKS, stride=L), :] = tile[CTX_CHUNKS:]


def _s5_out(vp, al_re, al_im, e_mat, t_mat, ft_mat):
    pair_w = S5_CHUNK * PAIR_BLK
    ctx_blk0 = (R_LAT // S5_CHUNK) // CTX_CHUNKS
    return pl.pallas_call(
        _s5_out_kernel,
        grid=(S5_QT, B),
        in_specs=[
            pl.BlockSpec((S5_PAIRS, LAT_CHUNKS, pair_w), lambda q, b: (q, b, 0)),
            pl.BlockSpec((S5_PAIRS, CTX_CHUNKS, pair_w), lambda q, b: (q, ctx_blk0 + b, 0)),
            pl.BlockSpec((2, 1, S5_ST), lambda q, b: (q, 0, 0)),
            pl.BlockSpec((2, 1, S5_ST), lambda q, b: (q, 0, 0)),
            pl.BlockSpec((1, S5_PAIRS, pair_w, 4 * LANES), lambda q, b: (q, 0, 0, 0)),
            pl.BlockSpec((1, S5_PAIRS, pair_w, pair_w), lambda q, b: (q, 0, 0, 0)),
            pl.BlockSpec((1, S5_PAIRS, pair_w, 4 * LANES), lambda q, b: (q, 0, 0, 0)),
        ],
        out_specs=[
            pl.BlockSpec((N_LAT, LANES), lambda q, b: (b, q)),
            pl.BlockSpec((N_CTX, LANES), lambda q, b: (b, q)),
        ],
        out_shape=[
            jax.ShapeDtypeStruct((R_LAT, S5_W), F32),
            jax.ShapeDtypeStruct((R_CTX, S5_W), F32),
        ],
        scratch_shapes=[pltpu.VMEM((4 * N_SLAB, BATCH_CHUNKS, LANES), F32)],
        compiler_params=pltpu.CompilerParams(vmem_limit_bytes=VMEM_LIMIT),
        name="s5_out",
    )(vp, vp, al_re, al_im, e_mat, t_mat, ft_mat)


def _tail0_stages(z, ys, rest_ref, gate, gluw_ref, glub_ref, sg_g_ref, sg_b_ref, sgw_ref,
                  sgbias_ref, wout_ref, lng_ref, lnb_ref):
    ga = rest_ref[:, 0:S5_W].astype(F32)
    u = rest_ref[:, S5_W:2 * S5_W].astype(F32)
    v = rest_ref[:, 2 * S5_W:3 * S5_W].astype(F32)
    gb = rest_ref[:, 3 * S5_W:4 * S5_W].astype(F32)

    ya = _gelu_tanh(ys)
    glu = jnp.dot(ya.astype(BF16), gluw_ref[...], preferred_element_type=F32) + glub_ref[...]
    yield None
    ya = ya * _sigmoid(glu) * _silu(ga)
    yield None

    vn = _layer_norm(v, sg_g_ref[...], sg_b_ref[...]).astype(BF16)
    lane = lax.broadcasted_iota(jnp.int32, (SG_CHUNK, LANES), 1)
    first_head = lane < (SG_W // SG_HEADS)
    chunks = []
    for c in range(TN // SG_CHUNK):
        tiles = []
        for j in range(SG_W // LANES):
            vt = vn[c * SG_CHUNK:(c + 1) * SG_CHUNK, j * LANES:(j + 1) * LANES]
            zero = jnp.zeros_like(vt)
            v2 = jnp.concatenate([jnp.where(first_head, vt, zero), jnp.where(first_head, zero, vt)],
                                 axis=0)
            tiles.append(jnp.dot(sgw_ref[j], v2, preferred_element_type=F32))
        chunks.append(jnp.concatenate(tiles, axis=-1) + sgbias_ref[...])
    s = jnp.concatenate(chunks, axis=0)
    yb = u * s * _silu(gb)
    yield None

    mix = jnp.concatenate([ya, yb], axis=-1).astype(BF16)
    y = jnp.dot(mix, wout_ref[...], preferred_element_type=F32) * gate
    yield None
    yield _layer_norm(ALPHA * z + y, lng_ref[...], lnb_ref[...])


def _rope_tables():
    nf = HD // 4
    n_rows = N_LAT // GRID_W
    lane = jnp.arange(LANES)
    inv = ROPE_BASE ** (-(lane % nf).astype(F32) / nf)
    by_row = ((lane % HD) // (HD // 2) == 0)[None, :]
    sign = jnp.where((lane % (HD // 2)) < nf, -1.0, 1.0)[None, :]
    row_ang = jnp.arange(n_rows, dtype=F32)[:, None] * inv[None, :]
    col_ang = jnp.arange(GRID_W, dtype=F32)[:, None] * inv[None, :]
    zero = jnp.zeros((), F32)
    row_tab = jnp.stack([jnp.where(by_row, jnp.cos(row_ang), zero),
                         jnp.where(by_row, sign * jnp.sin(row_ang), zero)])
    col_tab = jnp.stack([jnp.where(by_row, zero, jnp.cos(col_ang)),
                         jnp.where(by_row, zero, sign * jnp.sin(col_ang))])
    return row_tab, col_tab


def _rope_tile(x, cos, sin, first_half):
    nf = HD // 4
    partner = jnp.where(first_half, pltpu.roll(x, LANES - nf, axis=1), pltpu.roll(x, nf, axis=1))
    return x * cos + partner * sin


def _mid_kernel(x_ref, ctx_ref, mod0_ref, mod1_ref, ysl_ref, ysc_ref, rest_ref, gluw_ref, glub_ref,
                sg_g_ref, sg_b_ref, sgw_ref, sgbias_ref, wout_ref, lng_ref, lnb_ref, w_ref,
                rtab_ref, ctab_ref, z1_ref, q_ref, k_ref, vt_ref, g_ref, zprev_ref, wbf_scr,
                wout_scr):
    i = pl.program_id(0)
    dq = N_HEADS * HD
    dkv = N_KV * HD

    @pl.when(i == 0)
    def _():
        zprev_ref[...] = jnp.zeros((TN, D), F32)
        wbf_scr[...] = w_ref[...].astype(BF16)
        wout_scr[...] = wout_ref[...].astype(BF16)

    j = jnp.maximum(i - 1, 0)
    m1 = mod1_ref[pl.ds(_block_mod_row(j), 1), :]
    h = (zprev_ref[...] * (1.0 + m1[:, D:2 * D]) + m1[:, :D]).astype(BF16)
    grid_row0 = (j % LAT_BLK_PER_BATCH) * (TN // GRID_W)
    table = lambda t: jnp.concatenate(
        [rtab_ref[t, pl.ds(grid_row0 + r, 1), :] + ctab_ref[t] for r in range(TN // GRID_W)],
        axis=0)
    cos = jnp.where(j < N_LAT_BLK, table(0), 1.0)
    sin = jnp.where(j < N_LAT_BLK, table(1), 0.0)
    lane = lax.broadcasted_iota(jnp.int32, (TN, LANES), 1)
    first_half = (lane % (HD // 2)) < (HD // 4)
    scale = HD ** -0.5 * LOG2E

    def project(c0, c1):
        return jnp.dot(h, wbf_scr[:, c0:c1], preferred_element_type=F32)

    def roped_tiles(p, mult):
        for c in range(p.shape[1] // LANES):
            r = _rope_tile(p[:, c * LANES:(c + 1) * LANES], cos, sin, first_half)
            yield c, (r * mult if mult != 1.0 else r).astype(BF16)

    t = jnp.minimum(i, N_BLK - 1)
    z = jnp.where(t < N_LAT_BLK, x_ref[...], ctx_ref[...])
    ys = jnp.where(t < N_LAT_BLK, ysl_ref[...], ysc_ref[...])
    gate = mod0_ref[pl.ds(_block_mod_row(t), 1), 2 * D:3 * D]
    tail = _tail0_stages(z, ys, rest_ref, gate, gluw_ref, glub_ref, sg_g_ref, sg_b_ref, sgw_ref,
                         sgbias_ref, wout_scr, lng_ref, lnb_ref)

    half_q = dq // 2
    p_q0 = project(0, half_q)
    next(tail)
    p_q1 = project(half_q, dq)
    next(tail)
    p_k = project(dq, dq + dkv)
    vt_ref[...] = lax.dot_general(wbf_scr[:, dq + dkv:dq + 2 * dkv], h, (((0,), (1,)), ((), ())),
                                  preferred_element_type=F32).astype(BF16)
    next(tail)
    next(tail)
    for c0, p in ((0, p_q0), (half_q, p_q1)):
        for c, r in roped_tiles(p, scale):
            q_ref[:, c0 + c * LANES:c0 + (c + 1) * LANES] = r
    for c, r in roped_tiles(p_k, 1.0):
        k_ref[:, c * LANES:(c + 1) * LANES] = r
    p_g = project(dq + 2 * dkv, ODD_IN)
    z1 = next(tail)
    z1_ref[...] = z1
    zprev_ref[...] = z1
    g_ref[...] = p_g.astype(BF16)


def _mid(x2, ctx2, mod0, mod1, ys_lat, ys_ctx, rest, glu_w, glu_b, sg_g, sg_b, sg_w, sg_bias,
         w_out, ln_g, ln_b, w_in1, row_tab, col_tab):
    dq = N_HEADS * HD
    dkv = N_KV * HD
    once = dict(pipeline_mode=pl.Buffered(1))
    row = lambda n: pl.BlockSpec((1, n), lambda i: (0, 0))
    lat = lambda i: (jnp.minimum(i, N_LAT_BLK - 1), 0)
    tail_blk = lambda i: (jnp.minimum(i, N_BLK - 1), 0)
    proj_blk = lambda i: (jnp.maximum(i - 1, 0), 0)
    return pl.pallas_call(
        _mid_kernel,
        grid=(N_BLK + 1,),
        in_specs=[
            pl.BlockSpec((TN, D), lat),
            pl.BlockSpec((R_CTX, D), lambda i: (0, 0), **once),
            pl.BlockSpec((8, 3 * D), lambda i: (0, 0)),
            pl.BlockSpec((8, 3 * D), lambda i: (0, 0)),
            pl.BlockSpec((TN, S5_W), lat),
            pl.BlockSpec((R_CTX, S5_W), lambda i: (0, 0), **once),
            pl.BlockSpec((TN, EVEN_IN - S5_W), tail_blk),
            pl.BlockSpec((S5_W, S5_W), lambda i: (0, 0), **once),
            row(S5_W), row(SG_W), row(SG_W),
            pl.BlockSpec((SG_HEADS // 2, SG_CHUNK, 2 * SG_CHUNK), lambda i: (0, 0, 0)),
            pl.BlockSpec((SG_CHUNK, SG_W), lambda i: (0, 0)),
            pl.BlockSpec((S5_W + SG_W, D), lambda i: (0, 0), **once),
            row(D), row(D),
            pl.BlockSpec((D, ODD_IN), lambda i: (0, 0), **once),
            pl.BlockSpec((2, N_LAT // GRID_W, LANES), lambda i: (0, 0, 0)),
            pl.BlockSpec((2, GRID_W, LANES), lambda i: (0, 0, 0)),
        ],
        out_specs=[
            pl.BlockSpec((TN, D), tail_blk),
            pl.BlockSpec((TN, dq), proj_blk),
            pl.BlockSpec((TN, dkv), proj_blk),
            pl.BlockSpec((dkv, TN), lambda i: (0, jnp.maximum(i - 1, 0))),
            pl.BlockSpec((TN, dq), proj_blk),
        ],
        out_shape=[
            jax.ShapeDtypeStruct((R_ALL, D), F32),
            jax.ShapeDtypeStruct((R_ALL, dq), BF16),
            jax.ShapeDtypeStruct((R_ALL, dkv), BF16),
            jax.ShapeDtypeStruct((dkv, R_ALL), BF16),
            jax.ShapeDtypeStruct((R_ALL, dq), BF16),
        ],
        scratch_shapes=[pltpu.VMEM((TN, D), F32), pltpu.VMEM((D, ODD_IN), BF16),
                        pltpu.VMEM((S5_W + SG_W, D), BF16)],
        compiler_params=pltpu.CompilerParams(vmem_limit_bytes=VMEM_LIMIT),
        name="mid",
    )(x2, ctx2, mod0, mod1, ys_lat, ys_ctx, rest, glu_w, glu_b, sg_g, sg_b, sg_w, sg_bias, w_out,
      ln_g, ln_b, w_in1, row_tab, col_tab)


N_QBLK = N_LAT // ATT_BLK
GRP = N_HEADS // N_KV
ATT_SUB = 4
ATT_ROWS = ATT_SUB * ATT_BLK
N_QSTEP = N_QBLK // ATT_SUB
N_ATT_STEPS = B * N_QSTEP
assert ATT_ROWS == TN
assert WINDOW == ATT_BLK
ONES_ROWS = 16


def _attn_kernel(sink_ref, q_ref, kp_ref, kc_ref, kn_ref, kx_ref, vp_ref, vc_ref, vn_ref, vx_ref,
                 g_ref, z_ref, mod_ref, wout_ref, lng_ref, lnb_ref, out_ref, o_scr, wout_scr):
    step = pl.program_id(0)

    @pl.when(step == 0)
    def _():
        o_scr[...] = jnp.zeros((ATT_ROWS, N_HEADS * HD), BF16)
        wout_scr[...] = wout_ref[...].astype(BF16)

    t = jnp.maximum(step - 1, 0)
    gate = mod_ref[pl.ds(_block_mod_row(t), 1), 2 * D:3 * D]
    mix = (o_scr[...].astype(F32) * _silu(g_ref[...].astype(F32))).astype(BF16)
    y = jnp.dot(mix, wout_scr[...], preferred_element_type=F32) * gate
    out_ref[...] = _layer_norm(ALPHA * z_ref[...] + y, lng_ref[...], lnb_ref[...])

    i = jnp.minimum(step, N_ATT_STEPS - 1) % N_QSTEP
    n_win = 3 * ATT_BLK
    n_keys = n_win + N_CTX
    nq = GRP * ATT_BLK
    k_win = jnp.concatenate([kp_ref[...], kc_ref[...], kn_ref[...]], axis=0)
    vt_win = jnp.concatenate([vp_ref[...], vc_ref[...], vn_ref[...]], axis=1)
    k_ctx = kx_ref[...]
    vt_ctx = vx_ref[...]

    kpos = lax.broadcasted_iota(jnp.int32, (ATT_BLK, ATT_BLK), 0)
    qpos = lax.broadcasted_iota(jnp.int32, (ATT_BLK, ATT_BLK), 1)
    tile_q = lambda a: jnp.concatenate([a] * GRP, axis=1)
    qgrp = lax.broadcasted_iota(jnp.int32, (1, nq), 1) // ATT_BLK
    ones = jnp.ones((ONES_ROWS, n_keys), BF16)
    q = q_ref[...]

    units = [(u, h) for u in range(ATT_SUB) for h in range(N_KV)]
    k_all, vt_all, bias = [], [], []
    for u in range(ATT_SUB):
        k_all.append(jnp.concatenate([k_win[u * ATT_BLK:u * ATT_BLK + n_win], k_ctx], axis=0))
        vt_all.append(jnp.concatenate([vt_win[:, u * ATT_BLK:u * ATT_BLK + n_win], vt_ctx], axis=1))
        has_prev = (i > 0) if u == 0 else True
        has_next = (i < N_QSTEP - 1) if u == ATT_SUB - 1 else True
        bias.append((tile_q(jnp.where((kpos >= qpos) & has_prev, 0.0, NEG_INF)),
                     tile_q(jnp.where((kpos <= qpos) & has_next, 0.0, NEG_INF))))
    scores = []
    for u, h in units:
        kh = k_all[u][:, h * HD:(h + 1) * HD]
        qh = jnp.concatenate(
            [q[u * ATT_BLK:(u + 1) * ATT_BLK, (h * GRP + g) * HD:(h * GRP + g + 1) * HD]
             for g in range(GRP)], axis=0)
        scores.append(lax.dot_general(kh, qh, (((1,), (1,)), ((), ())),
                                      preferred_element_type=F32))
    probs = []
    for (u, h), s in zip(units, scores):
        s = jnp.concatenate([s[:ATT_BLK] + bias[u][0], s[ATT_BLK:2 * ATT_BLK],
                             s[2 * ATT_BLK:n_win] + bias[u][1], s[n_win:]], axis=0)
        sink = jnp.zeros((1, nq), F32)
        for g in range(GRP):
            sink = jnp.where(qgrp == g, sink_ref[h * GRP + g] * LOG2E, sink)
        m = jnp.maximum(jnp.max(s, axis=0, keepdims=True), sink)
        probs.append((jnp.exp2(s - m).astype(BF16), jnp.exp2(sink - m)))
    outs = [[] for _ in range(ATT_SUB)]
    for (u, h), (p, p_sink) in zip(units, probs):
        vt1 = jnp.concatenate([vt_all[u][h * HD:(h + 1) * HD], ones], axis=0)
        ov = jnp.dot(vt1, p, preferred_element_type=F32)
        o_t = ov[:HD] / (ov[HD:HD + 1] + p_sink)
        outs[u].extend(o_t[:, g * ATT_BLK:(g + 1) * ATT_BLK] for g in range(GRP))
    for u in range(ATT_SUB):
        o_scr[u * ATT_BLK:(u + 1) * ATT_BLK, :] = jnp.transpose(
            jnp.concatenate(outs[u], axis=0)).astype(BF16)


def _attention_tail(sink, q, k, vt, g, z1, mod, w_out, ln_g, ln_b):
    dq = N_HEADS * HD
    dkv = N_KV * HD
    ctx_blk0 = R_LAT // N_CTX
    cur = lambda s: jnp.minimum(s, N_ATT_STEPS - 1)
    batch = lambda s: cur(s) // N_QSTEP
    qstep = lambda s: cur(s) % N_QSTEP
    prev_blk = lambda s: batch(s) * N_QBLK + jnp.maximum(ATT_SUB * qstep(s) - 1, 0)
    next_blk = lambda s: batch(s) * N_QBLK + jnp.minimum(ATT_SUB * (qstep(s) + 1), N_QBLK - 1)
    tail = lambda s: jnp.maximum(s - 1, 0)
    once = dict(pipeline_mode=pl.Buffered(1))
    row = lambda n: pl.BlockSpec((1, n), lambda s, sk: (0, 0))
    return pl.pallas_call(
        _attn_kernel,
        grid_spec=pltpu.PrefetchScalarGridSpec(
            num_scalar_prefetch=1,
            grid=(N_ATT_STEPS + 1,),
            in_specs=[
                pl.BlockSpec((ATT_ROWS, dq), lambda s, sk: (cur(s), 0)),
                pl.BlockSpec((ATT_BLK, dkv), lambda s, sk: (prev_blk(s), 0)),
                pl.BlockSpec((ATT_ROWS, dkv), lambda s, sk: (cur(s), 0)),
                pl.BlockSpec((ATT_BLK, dkv), lambda s, sk: (next_blk(s), 0)),
                pl.BlockSpec((N_CTX, dkv), lambda s, sk: (ctx_blk0 + batch(s), 0)),
                pl.BlockSpec((dkv, ATT_BLK), lambda s, sk: (0, prev_blk(s))),
                pl.BlockSpec((dkv, ATT_ROWS), lambda s, sk: (0, cur(s))),
                pl.BlockSpec((dkv, ATT_BLK), lambda s, sk: (0, next_blk(s))),
                pl.BlockSpec((dkv, N_CTX), lambda s, sk: (0, ctx_blk0 + batch(s))),
                pl.BlockSpec((ATT_ROWS, dq), lambda s, sk: (tail(s), 0)),
                pl.BlockSpec((ATT_ROWS, D), lambda s, sk: (tail(s), 0)),
                pl.BlockSpec((8, 3 * D), lambda s, sk: (0, 0)),
                pl.BlockSpec((dq, D), lambda s, sk: (0, 0), **once),
                row(D), row(D),
            ],
            out_specs=pl.BlockSpec((ATT_ROWS, D), lambda s, sk: (tail(s), 0)),
            scratch_shapes=[pltpu.VMEM((ATT_ROWS, dq), BF16), pltpu.VMEM((dq, D), BF16)],
        ),
        out_shape=jax.ShapeDtypeStruct((R_LAT, D), F32),
        compiler_params=pltpu.CompilerParams(vmem_limit_bytes=VMEM_LIMIT),
        name="attention",
    )(sink, q, k, k, k, k, vt, vt, vt, vt, g, z1, mod, w_out, ln_g, ln_b)


def kernel(x, c, ctx, c_ctx, mod_w, mod_b, ln_g, ln_b, e_w_in, e_w_out, s5_lam_re, s5_lam_im,
           s5_log_dt, s5_b_re, s5_b_im, s5_c_re, s5_c_im, s5_d, glu_w, glu_b, sg_ln_g, sg_ln_b,
           sg_w, sg_b, o_w_in, o_w_out, o_sink):
    x2 = x.reshape(R_LAT, D)
    ctx2 = ctx.reshape(R_CTX, D)
    mod = _modulation(c, c_ctx, mod_w, mod_b)

    vp, rest = _in0(x2, ctx2, mod[0], e_w_in[0])
    t_mat, e_mat, ft_mat, al_re, al_im = _s5_prep(
        s5_lam_re[0], s5_lam_im[0], s5_log_dt[0], s5_b_re[0], s5_b_im[0], s5_c_re[0], s5_c_im[0],
        s5_d[0])
    ys_lat, ys_ctx = _s5_out(vp, al_re, al_im, e_mat, t_mat, ft_mat)
    sg_bias = jnp.repeat(jnp.transpose(sg_b[0]), SG_W // SG_HEADS, axis=1)
    sg_w2 = jnp.transpose(sg_w[0].reshape(SG_HEADS // 2, 2, SG_CHUNK, SG_CHUNK), (0, 2, 1, 3))
    sg_w2 = sg_w2.reshape(SG_HEADS // 2, SG_CHUNK, 2 * SG_CHUNK).astype(BF16)
    row_tab, col_tab = _rope_tables()
    z1, q, k, vt, g = _mid(
        x2, ctx2, mod[0], mod[1], ys_lat, ys_ctx, rest, glu_w[0].astype(BF16),
        glu_b[0].reshape(1, S5_W), sg_ln_g[0].reshape(1, SG_W), sg_ln_b[0].reshape(1, SG_W),
        sg_w2, sg_bias, e_w_out[0], ln_g[0].reshape(1, D),
        ln_b[0].reshape(1, D), o_w_in[0], row_tab, col_tab)

    out = _attention_tail(o_sink[0], q, k, vt, g, z1, mod[1], o_w_out[0],
                          ln_g[1].reshape(1, D), ln_b[1].reshape(1, D))
    return out.reshape(B, N_LAT, D)
```

```python
import math

import jax
import jax.numpy as jnp
from jax import lax
from jax.experimental import pallas as pl
from jax.experimental.pallas import tpu as pltpu

F32 = jnp.float32
BF16 = jnp.bfloat16

D = 1024
B = 2
N_LAT = 8192
N_CTX = 256
DEPTH = 2
GRID_W = 64
S5_W = 512
S5_GC = 16
S5_G = 32
S5_P = 64
SG_W = 512
SG_HEADS = 8
SG_CHUNK = 128
N_HEADS = 16
N_KV = 4
HD = 64
WINDOW = 128
ATT_BLK = 128
ROPE_BASE = 10000.0
NEG_INF = -1e30
LN_EPS = 1e-5
ALPHA = (2 * DEPTH) ** 0.25
LOG2E = math.log2(math.e)
EVEN_IN = 2 * S5_W + 3 * SG_W
ODD_IN = 2 * N_HEADS * HD + 2 * N_KV * HD

LANES = 128
VMEM_LIMIT = 56 * 1024 * 1024

R_LAT = B * N_LAT
R_CTX = B * N_CTX
R_ALL = R_LAT + R_CTX
TN = 512
N_LAT_BLK = R_LAT // TN
N_BLK = R_ALL // TN
LAT_BLK_PER_BATCH = N_LAT // TN
CTX_MOD_ROW = B

S5_CHUNK = 8
S5_QT = S5_W // LANES
S5_GPT = LANES // S5_GC
S5_ST = S5_GPT * S5_P
S5_PAIRS = S5_GPT // 2
PAIR_BLK = 2 * S5_GC
LAT_CHUNKS = N_LAT // S5_CHUNK
CTX_CHUNKS = N_CTX // S5_CHUNK
BATCH_CHUNKS = CTX_CHUNKS + LAT_CHUNKS
N_SEG = 16
SEG_CHUNKS = BATCH_CHUNKS // N_SEG
BWD_RESET_STEP = LAT_CHUNKS - (N_SEG - 1) * SEG_CHUNKS


def _sigmoid(x):
    return 1.0 / (1.0 + jnp.exp(-x))


def _silu(x):
    return x * _sigmoid(x)


def _gelu_tanh(x):
    return 0.5 * x * (1.0 + jnp.tanh(math.sqrt(2.0 / math.pi) * (x + 0.044715 * (x * x * x))))


def _layer_norm(x, g, b):
    mu = jnp.mean(x, axis=-1, keepdims=True)
    xc = x - mu
    var = jnp.mean(xc * xc, axis=-1, keepdims=True)
    return xc * lax.rsqrt(var + LN_EPS) * g + b


def _split_bf16(x):
    hi = x.astype(BF16)
    return hi, (x - hi.astype(F32)).astype(BF16)


def _dot_split(x, y_split, dims):
    x_hi, x_lo = _split_bf16(x)
    y_hi, y_lo = y_split
    dot = lambda a, b: lax.dot_general(a, b, dims, preferred_element_type=F32)
    return dot(x_hi, y_hi) + (dot(x_hi, y_lo) + dot(x_lo, y_hi))


def _block_mod_row(i):
    return jnp.where(i < N_LAT_BLK, i // LAT_BLK_PER_BATCH, CTX_MOD_ROW)


def _mod_kernel(cv_ref, w_ref, b_ref, o_ref):
    s = _silu(cv_ref[...])
    o_ref[0] = _dot_split(s, _split_bf16(w_ref[0]), (((1,), (0,)), ((), ()))) + b_ref[0]


def _modulation(c, c_ctx, mod_w, mod_b):
    cv = jnp.concatenate([c, c_ctx[None], jnp.zeros((8 - B - 1, D), F32)], axis=0)
    tn = 1024
    return pl.pallas_call(
        _mod_kernel,
        grid=(DEPTH, 3 * D // tn),
        in_specs=[
            pl.BlockSpec((8, D), lambda l, j: (0, 0)),
            pl.BlockSpec((1, D, tn), lambda l, j: (l, 0, j)),
            pl.BlockSpec((1, 1, tn), lambda l, j: (l, 0, j)),
        ],
        out_specs=pl.BlockSpec((1, 8, tn), lambda l, j: (l, 0, j)),
        out_shape=jax.ShapeDtypeStruct((DEPTH, 8, 3 * D), F32),
        compiler_params=pltpu.CompilerParams(vmem_limit_bytes=VMEM_LIMIT),
        name="modulation",
    )(cv, mod_w, mod_b.reshape(DEPTH, 1, 3 * D))


def _block_transpose4(tiles):
    tiles = list(tiles)
    blk = lax.broadcasted_iota(jnp.int32, tiles[0].shape, 1) // PAIR_BLK
    for dist in (2, 1):
        keep = (blk & dist) == 0
        for i in range(len(tiles)):
            if i & dist:
                continue
            lo, hi = tiles[i], tiles[i + dist]
            tiles[i] = jnp.where(keep, lo, pltpu.roll(hi, dist * PAIR_BLK, axis=1))
            tiles[i + dist] = jnp.where(keep, pltpu.roll(lo, LANES - dist * PAIR_BLK, axis=1), hi)
    return tiles


def _in0_kernel(x_ref, ctx_ref, mod_ref, w_ref, vp_ref, rest_ref, xa_scr, wbf_scr):
    L = S5_CHUNK
    i = pl.program_id(0)

    @pl.when(i == 0)
    def _():
        wbf_scr[...] = w_ref[...].astype(BF16)

    z = jnp.where(i < N_LAT_BLK, x_ref[...], ctx_ref[...])
    m = mod_ref[pl.ds(_block_mod_row(i), 1), :]
    h = z * (1.0 + m[:, D:2 * D]) + m[:, :D]
    p = jnp.dot(h.astype(BF16), wbf_scr[...], preferred_element_type=F32)
    rest_ref[...] = p[:, S5_W:].astype(BF16)
    for q in range(S5_QT):
        xa_scr[q] = p[:, q * LANES:(q + 1) * LANES]
        tiles = [xa_scr[q, pl.ds(t, TN // L, stride=L), :] for t in range(L)]
        halves = [_block_transpose4(tiles[k * S5_PAIRS:(k + 1) * S5_PAIRS])
                  for k in range(L // S5_PAIRS)]
        for pr in range(S5_PAIRS):
            vp_ref[q * S5_PAIRS + pr] = jnp.concatenate(
                [half[pr] for half in halves], axis=-1).astype(BF16)


def _in0(x2, ctx2, mod, w_in):
    pair_w = S5_CHUNK * PAIR_BLK
    return pl.pallas_call(
        _in0_kernel,
        grid=(N_BLK,),
        in_specs=[
            pl.BlockSpec((TN, D), lambda i: (jnp.minimum(i, N_LAT_BLK - 1), 0)),
            pl.BlockSpec((R_CTX, D), lambda i: (0, 0)),
            pl.BlockSpec((8, 3 * D), lambda i: (0, 0)),
            pl.BlockSpec((D, EVEN_IN), lambda i: (0, 0)),
        ],
        out_specs=[
            pl.BlockSpec((S5_QT * S5_PAIRS, TN // S5_CHUNK, pair_w), lambda i: (0, i, 0)),
            pl.BlockSpec((TN, EVEN_IN - S5_W), lambda i: (i, 0)),
        ],
        out_shape=[
            jax.ShapeDtypeStruct((S5_QT * S5_PAIRS, R_ALL // S5_CHUNK, pair_w), BF16),
            jax.ShapeDtypeStruct((R_ALL, EVEN_IN - S5_W), BF16),
        ],
        scratch_shapes=[pltpu.VMEM((S5_QT, TN, LANES), F32), pltpu.VMEM((D, EVEN_IN), BF16)],
        compiler_params=pltpu.CompilerParams(vmem_limit_bytes=VMEM_LIMIT),
        name="in0",
    )(x2, ctx2, mod, w_in)


def _s5_prep_kernel(rows_ref, bre_ref, bim_ref, cre_ref, cim_ref, dsk_ref, t_ref, e_ref, ft_ref,
                    alr_ref, ali_ref):
    L = S5_CHUNK
    tile = (LANES, S5_ST)
    same_group = (lax.broadcasted_iota(jnp.int32, tile, 0) // S5_GC
                  == lax.broadcasted_iota(jnp.int32, tile, 1) // S5_P)
    nt_dims = (((1,), (1,)), ((), ()))
    nn_dims = (((1,), (0,)), ((), ()))
    pick_ch = (lax.broadcasted_iota(jnp.int32, (LANES, S5_GC), 0) % S5_GC
               == lax.broadcasted_iota(jnp.int32, (LANES, S5_GC), 1)).astype(BF16)
    rep_st = (lax.broadcasted_iota(jnp.int32, (S5_P, S5_ST), 0)
              == lax.broadcasted_iota(jnp.int32, (S5_P, S5_ST), 1) % S5_P).astype(BF16)

    def spread_b(b):
        hi, lo = _split_bf16(b)
        dot = lambda v: lax.dot_general(pick_ch, v, nt_dims, preferred_element_type=F32)
        return jnp.where(same_group, dot(hi) + dot(lo), 0.0)

    def spread_c(c):
        hi, lo = _split_bf16(c)
        dot = lambda v: lax.dot_general(v, rep_st, nn_dims, preferred_element_type=F32)
        return jnp.where(same_group, dot(hi) + dot(lo), 0.0)

    pair_rows = lambda pr: slice(pr * PAIR_BLK, (pr + 1) * PAIR_BLK)
    pair_lanes = lambda pr: slice(pr * LANES, (pr + 1) * LANES)
    lag_strip = []
    for d in range(2):
        lam_re = rows_ref[d, 0]
        lam_im = rows_ref[d, 1]
        dt = jnp.exp(rows_ref[d, 2])
        pw = []
        for k in range(L + 1):
            mag = jnp.exp(lam_re * dt * float(k))
            ang = lam_im * dt * float(k)
            pw.append((mag * jnp.cos(ang), mag * jnp.sin(ang)))
        alr_ref[d] = pw[L][0]
        ali_ref[d] = pw[L][1]
        den = lam_re * lam_re + lam_im * lam_im
        nr = pw[1][0] - 1.0
        ni = pw[1][1]
        f_re = (nr * lam_re + ni * lam_im) / den
        f_im = (ni * lam_re - nr * lam_im) / den
        b_re = spread_b(bre_ref[d])
        b_im = spread_b(bim_ref[d])
        bb_re = f_re * b_re - f_im * b_im
        bb_im = f_re * b_im + f_im * b_re
        c_re = spread_c(cre_ref[d])
        c_im = spread_c(cim_ref[d])
        col_re = slice(2 * d * LANES, (2 * d + 1) * LANES)
        col_im = slice((2 * d + 1) * LANES, (2 * d + 2) * LANES)
        strips = []
        for pr in range(S5_PAIRS):
            blk = lambda a: a[pair_rows(pr), pair_lanes(pr)]
            p_re = [blk(jnp.broadcast_to(pw[k][0], tile)) for k in range(L + 1)]
            p_im = [blk(jnp.broadcast_to(pw[k][1], tile)) for k in range(L + 1)]
            bbr, bbi, cr, ci = blk(bb_re), blk(bb_im), blk(c_re), blk(c_im)
            for k in range(L):
                s = (L - 1 - k) if d == 0 else k
                rows = slice(s * PAIR_BLK, (s + 1) * PAIR_BLK)
                e_ref[0, pr, rows, col_re] = (p_re[k] * bbr - p_im[k] * bbi).astype(BF16)
                e_ref[0, pr, rows, col_im] = (p_re[k] * bbi + p_im[k] * bbr).astype(BF16)
            cw = [(cr * p_re[k] - ci * p_im[k], -(cr * p_im[k] + ci * p_re[k]))
                  for k in range(L + 1)]
            for t in range(L):
                k = (t + 1) if d == 0 else (L - t)
                rows = slice(t * PAIR_BLK, (t + 1) * PAIR_BLK)
                ft_ref[0, pr, rows, col_re] = cw[k][0].astype(BF16)
                ft_ref[0, pr, rows, col_im] = cw[k][1].astype(BF16)
            lags = range(L) if d == 0 else range(L - 1, -1, -1)
            c_stack = jnp.concatenate(
                [jnp.concatenate([cw[k][0], cw[k][1]], axis=-1) for k in lags], axis=0)
            strips.append(_dot_split(jnp.concatenate([bbr, bbi], axis=-1),
                                     _split_bf16(c_stack), nt_dims))
        lag_strip.append(strips)

    strip_w = 2 * L * PAIR_BLK
    mid = (L - 1) * PAIR_BLK
    widen = lambda a: jnp.concatenate([a, jnp.zeros((a.shape[0], strip_w - a.shape[1]), F32)], axis=-1)
    row = lax.broadcasted_iota(jnp.int32, (PAIR_BLK, strip_w), 0)
    lane = lax.broadcasted_iota(jnp.int32, (PAIR_BLK, strip_w), 1)
    d_row = widen(dsk_ref[0])
    for pr in range(S5_PAIRS):
        strip = (widen(lag_strip[1][pr])
                 + pltpu.roll(widen(lag_strip[0][pr]), mid, axis=1)
                 + jnp.where(lane == row + mid,
                             pltpu.roll(d_row, mid - pr * PAIR_BLK, axis=1), 0.0))
        for s in range(L):
            shift = (strip_w - (L - 1 - s) * PAIR_BLK) % strip_w
            blk = pltpu.roll(strip, shift, axis=1) if shift else strip
            t_ref[0, pr, s * PAIR_BLK:(s + 1) * PAIR_BLK, :] = blk[:, :L * PAIR_BLK].astype(BF16)


def _s5_prep(lam_re, lam_im, log_dt, b_re, b_im, c_re, c_im, d_skip):
    L = S5_CHUNK
    n_st = S5_G * S5_P
    pair_w = L * PAIR_BLK
    rows = jnp.stack([lam_re.reshape(2, 1, n_st), lam_im.reshape(2, 1, n_st),
                      jnp.repeat(log_dt, S5_P, axis=1).reshape(2, 1, n_st)], axis=1)
    b_spec = pl.BlockSpec((2, S5_ST, S5_GC), lambda q: (0, q, 0))
    c_spec = pl.BlockSpec((2, LANES, S5_P), lambda q: (0, q, 0))
    return pl.pallas_call(
        _s5_prep_kernel,
        grid=(S5_QT,),
        in_specs=[
            pl.BlockSpec((2, 3, 1, S5_ST), lambda q: (0, 0, 0, q)),
            b_spec, b_spec, c_spec, c_spec,
            pl.BlockSpec((1, 1, LANES), lambda q: (q, 0, 0)),
        ],
        out_specs=[
            pl.BlockSpec((1, S5_PAIRS, pair_w, pair_w), lambda q: (q, 0, 0, 0)),
            pl.BlockSpec((1, S5_PAIRS, pair_w, 4 * LANES), lambda q: (q, 0, 0, 0)),
            pl.BlockSpec((1, S5_PAIRS, pair_w, 4 * LANES), lambda q: (q, 0, 0, 0)),
            pl.BlockSpec((2, 1, S5_ST), lambda q: (q, 0, 0)),
            pl.BlockSpec((2, 1, S5_ST), lambda q: (q, 0, 0)),
        ],
        out_shape=[
            jax.ShapeDtypeStruct((S5_QT, S5_PAIRS, pair_w, pair_w), BF16),
            jax.ShapeDtypeStruct((S5_QT, S5_PAIRS, pair_w, 4 * LANES), BF16),
            jax.ShapeDtypeStruct((S5_QT, S5_PAIRS, pair_w, 4 * LANES), BF16),
            jax.ShapeDtypeStruct((2 * S5_QT, 1, S5_ST), F32),
            jax.ShapeDtypeStruct((2 * S5_QT, 1, S5_ST), F32),
        ],
        compiler_params=pltpu.CompilerParams(vmem_limit_bytes=VMEM_LIMIT),
        name="s5_prep",
    )(rows, b_re.reshape(2, n_st, S5_GC), b_im.reshape(2, n_st, S5_GC),
      c_re.reshape(2, S5_W, S5_P), c_im.reshape(2, S5_W, S5_P), d_skip.reshape(S5_QT, 1, LANES))


N_SLAB = S5_ST // LANES


def _cmul(ar, ai, br, bi):
    return ar * br - ai * bi, ar * bi + ai * br


def _cpow(ar, ai, n):
    res = None
    while n:
        if n & 1:
            res = (ar, ai) if res is None else _cmul(res[0], res[1], ar, ai)
        n >>= 1
        if n:
            ar, ai = _cmul(ar, ai, ar, ai)
    return res


def _scan_boundary_states(loc_ref, out_ref, ar_ref, ai_ref, backward):
    a_row = [(ar_ref[0][:, k * LANES:(k + 1) * LANES], ai_ref[0][:, k * LANES:(k + 1) * LANES])
             for k in range(N_SLAB)]
    a_tile = [(jnp.broadcast_to(r, (N_SEG, LANES)), jnp.broadcast_to(i, (N_SEG, LANES)))
              for r, i in a_row]

    def sweep(lo, hi, carry, store):
        def step(i, carry):
            off = (SEG_CHUNKS - 1 - i) if backward else i
            rows = pl.ds(off, N_SEG, stride=SEG_CHUNKS)
            new = []
            for k, (cr, ci) in enumerate(carry):
                xr = loc_ref[k, rows, :]
                xi = loc_ref[N_SLAB + k, rows, :]
                if store:
                    out_ref[k, rows, :] = cr
                    out_ref[N_SLAB + k, rows, :] = ci
                nr, ni = _cmul(a_tile[k][0], a_tile[k][1], cr, ci)
                new.append((nr + xr, ni + xi))
            return tuple(new)
        return lax.fori_loop(lo, hi, step, carry)

    def full(carry, store):
        if not backward:
            return sweep(0, SEG_CHUNKS, carry, store)
        carry = sweep(0, BWD_RESET_STEP, carry, store)
        first = lax.broadcasted_iota(jnp.int32, (N_SEG, LANES), 0) == 0
        carry = tuple((jnp.where(first, 0.0, cr), jnp.where(first, 0.0, ci)) for cr, ci in carry)
        return sweep(BWD_RESET_STEP, SEG_CHUNKS, carry, store)

    zero_tile = jnp.zeros((N_SEG, LANES), F32)
    ends = full(tuple((zero_tile, zero_tile) for _ in range(N_SLAB)), False)

    starts = []
    for k, (er, ei) in enumerate(ends):
        pr, pi = _cpow(a_row[k][0], a_row[k][1], SEG_CHUNKS)
        if backward:
            order = range(N_SEG - 1, -1, -1)
            g = (er[0:1], ei[0:1])
        else:
            order = range(N_SEG)
            g = (jnp.zeros((1, LANES), F32), jnp.zeros((1, LANES), F32))
        rows_r = [None] * N_SEG
        rows_i = [None] * N_SEG
        for j in order:
            rows_r[j], rows_i[j] = g
            nr, ni = _cmul(pr, pi, g[0], g[1])
            g = (nr + er[j:j + 1], ni + ei[j:j + 1])
        starts.append((jnp.concatenate(rows_r, axis=0), jnp.concatenate(rows_i, axis=0)))
    full(tuple(starts), True)


def _s5_out_kernel(vl_ref, vc_ref, ar_ref, ai_ref, e_ref, t_ref, ft_ref, yl_ref, yc_ref, st_ref):
    n_dir = 2 * N_SLAB
    rows = [jnp.concatenate([vc_ref[pr], vl_ref[pr]], axis=0) for pr in range(S5_PAIRS)]
    for pr in range(S5_PAIRS):
        st = jnp.dot(rows[pr], e_ref[0, pr], preferred_element_type=F32)
        for comp in range(4):
            st_ref[comp * N_SLAB + pr] = st[:, comp * LANES:(comp + 1) * LANES]
    for d in range(2):
        st_d = st_ref.at[pl.ds(d * n_dir, n_dir)]
        _scan_boundary_states(st_d, st_d, ar_ref.at[pl.ds(d, 1)], ai_ref.at[pl.ds(d, 1)], d == 1)

    for pr in range(S5_PAIRS):
        st = jnp.concatenate([st_ref[comp * N_SLAB + pr] for comp in range(4)],
                             axis=-1).astype(BF16)
        y = (jnp.dot(rows[pr], t_ref[0, pr], preferred_element_type=F32)
             + lax.dot_general(st, ft_ref[0, pr], (((1,), (1,)), ((), ())),
                               preferred_element_type=F32))
        yc_ref[pr] = y[:CTX_CHUNKS]
        yl_ref[pr] = y[CTX_CHUNKS:]


def _s5_out(vp, al_re, al_im, e_mat, t_mat, ft_mat):
    pair_w = S5_CHUNK * PAIR_BLK
    ctx_blk0 = (R_LAT // S5_CHUNK) // CTX_CHUNKS
    return pl.pallas_call(
        _s5_out_kernel,
        grid=(S5_QT, B),
        in_specs=[
            pl.BlockSpec((S5_PAIRS, LAT_CHUNKS, pair_w), lambda q, b: (q, b, 0)),
            pl.BlockSpec((S5_PAIRS, CTX_CHUNKS, pair_w), lambda q, b: (q, ctx_blk0 + b, 0)),
            pl.BlockSpec((2, 1, S5_ST), lambda q, b: (q, 0, 0)),
            pl.BlockSpec((2, 1, S5_ST), lambda q, b: (q, 0, 0)),
            pl.BlockSpec((1, S5_PAIRS, pair_w, 4 * LANES), lambda q, b: (q, 0, 0, 0)),
            pl.BlockSpec((1, S5_PAIRS, pair_w, pair_w), lambda q, b: (q, 0, 0, 0)),
            pl.BlockSpec((1, S5_PAIRS, pair_w, 4 * LANES), lambda q, b: (q, 0, 0, 0)),
        ],
        out_specs=[
            pl.BlockSpec((S5_PAIRS, LAT_CHUNKS, pair_w), lambda q, b: (q, b, 0)),
            pl.BlockSpec((S5_PAIRS, CTX_CHUNKS, pair_w), lambda q, b: (q, b, 0)),
        ],
        out_shape=[
            jax.ShapeDtypeStruct((S5_QT * S5_PAIRS, B * LAT_CHUNKS, pair_w), F32),
            jax.ShapeDtypeStruct((S5_QT * S5_PAIRS, B * CTX_CHUNKS, pair_w), F32),
        ],
        scratch_shapes=[pltpu.VMEM((4 * N_SLAB, BATCH_CHUNKS, LANES), F32)],
        compiler_params=pltpu.CompilerParams(vmem_limit_bytes=VMEM_LIMIT),
        name="s5_out",
    )(vp, vp, al_re, al_im, e_mat, t_mat, ft_mat)


def _tail0_stages(z, ys, rest_ref, gate, gluw_ref, glub_ref, sg_g_ref, sg_b_ref, sgw_ref,
                  sgbias_ref, wout_ref, lng_ref, lnb_ref):
    ga = rest_ref[:, 0:S5_W].astype(F32)
    u = rest_ref[:, S5_W:2 * S5_W].astype(F32)
    v = rest_ref[:, 2 * S5_W:3 * S5_W].astype(F32)
    gb = rest_ref[:, 3 * S5_W:4 * S5_W].astype(F32)

    ya = _gelu_tanh(ys)
    glu = jnp.dot(ya.astype(BF16), gluw_ref[...], preferred_element_type=F32) + glub_ref[...]
    yield None
    ya = ya * _sigmoid(glu) * _silu(ga)
    yield None

    vn = _layer_norm(v, sg_g_ref[...], sg_b_ref[...]).astype(BF16)
    lane = lax.broadcasted_iota(jnp.int32, (SG_CHUNK, LANES), 1)
    first_head = lane < (SG_W // SG_HEADS)
    chunks = []
    for c in range(TN // SG_CHUNK):
        tiles = []
        for j in range(SG_W // LANES):
            vt = vn[c * SG_CHUNK:(c + 1) * SG_CHUNK, j * LANES:(j + 1) * LANES]
            zero = jnp.zeros_like(vt)
            v2 = jnp.concatenate([jnp.where(first_head, vt, zero), jnp.where(first_head, zero, vt)],
                                 axis=0)
            tiles.append(jnp.dot(sgw_ref[j], v2, preferred_element_type=F32))
        chunks.append(jnp.concatenate(tiles, axis=-1) + sgbias_ref[...])
    s = jnp.concatenate(chunks, axis=0)
    yb = u * s * _silu(gb)
    yield None

    mix = jnp.concatenate([ya, yb], axis=-1).astype(BF16)
    y = jnp.dot(mix, wout_ref[...], preferred_element_type=F32) * gate
    yield None
    yield _layer_norm(ALPHA * z + y, lng_ref[...], lnb_ref[...])


def _rope_tables():
    nf = HD // 4
    n_rows = N_LAT // GRID_W
    lane = jnp.arange(LANES)
    inv = ROPE_BASE ** (-(lane % nf).astype(F32) / nf)
    by_row = ((lane % HD) // (HD // 2) == 0)[None, :]
    sign = jnp.where((lane % (HD // 2)) < nf, -1.0, 1.0)[None, :]
    row_ang = jnp.arange(n_rows, dtype=F32)[:, None] * inv[None, :]
    col_ang = jnp.arange(GRID_W, dtype=F32)[:, None] * inv[None, :]
    zero = jnp.zeros((), F32)
    row_tab = jnp.stack([jnp.where(by_row, jnp.cos(row_ang), zero),
                         jnp.where(by_row, sign * jnp.sin(row_ang), zero)])
    col_tab = jnp.stack([jnp.where(by_row, zero, jnp.cos(col_ang)),
                         jnp.where(by_row, zero, sign * jnp.sin(col_ang))])
    return row_tab, col_tab


def _rope_tile(x, cos, sin, first_half):
    nf = HD // 4
    partner = jnp.where(first_half, pltpu.roll(x, LANES - nf, axis=1), pltpu.roll(x, nf, axis=1))
    return x * cos + partner * sin


def _mid_kernel(x_ref, ctx_ref, mod0_ref, mod1_ref, ysl_ref, ysc_ref, rest_ref, gluw_ref, glub_ref,
                sg_g_ref, sg_b_ref, sgw_ref, sgbias_ref, wout_ref, lng_ref, lnb_ref, w_ref,
                rtab_ref, ctab_ref, z1_ref, q_ref, k_ref, vt_ref, g_ref, zprev_ref, wbf_scr,
                wout_scr, ys_scr):
    i = pl.program_id(0)
    dq = N_HEADS * HD
    dkv = N_KV * HD

    @pl.when(i == 0)
    def _():
        zprev_ref[...] = jnp.zeros((TN, D), F32)
        wbf_scr[...] = w_ref[...].astype(BF16)
        wout_scr[...] = wout_ref[...].astype(BF16)

    j = jnp.maximum(i - 1, 0)
    m1 = mod1_ref[pl.ds(_block_mod_row(j), 1), :]
    h = (zprev_ref[...] * (1.0 + m1[:, D:2 * D]) + m1[:, :D]).astype(BF16)
    grid_row0 = (j % LAT_BLK_PER_BATCH) * (TN // GRID_W)
    table = lambda t: jnp.concatenate(
        [rtab_ref[t, pl.ds(grid_row0 + r, 1), :] + ctab_ref[t] for r in range(TN // GRID_W)],
        axis=0)
    cos = jnp.where(j < N_LAT_BLK, table(0), 1.0)
    sin = jnp.where(j < N_LAT_BLK, table(1), 0.0)
    lane = lax.broadcasted_iota(jnp.int32, (TN, LANES), 1)
    first_half = (lane % (HD // 2)) < (HD // 4)
    scale = HD ** -0.5 * LOG2E

    def project(c0, c1):
        return jnp.dot(h, wbf_scr[:, c0:c1], preferred_element_type=F32)

    def roped_tiles(p, mult):
        for c in range(p.shape[1] // LANES):
            r = _rope_tile(p[:, c * LANES:(c + 1) * LANES], cos, sin, first_half)
            yield c, (r * mult if mult != 1.0 else r).astype(BF16)

    t = jnp.minimum(i, N_BLK - 1)
    z = jnp.where(t < N_LAT_BLK, x_ref[...], ctx_ref[...])
    yp = jnp.where(t < N_LAT_BLK, ysl_ref[...], ysc_ref[...])
    for qt in range(S5_QT):
        for k in range(S5_CHUNK // S5_PAIRS):
            tiles = _block_transpose4([yp[qt * S5_PAIRS + pr][:, k * LANES:(k + 1) * LANES]
                                       for pr in range(S5_PAIRS)])
            for n, tile in enumerate(tiles):
                ys_scr[qt, pl.ds(k * S5_PAIRS + n, TN // S5_CHUNK, stride=S5_CHUNK), :] = tile
    ys = jnp.concatenate([ys_scr[qt] for qt in range(S5_QT)], axis=-1)
    gate = mod0_ref[pl.ds(_block_mod_row(t), 1), 2 * D:3 * D]
    tail = _tail0_stages(z, ys, rest_ref, gate, gluw_ref, glub_ref, sg_g_ref, sg_b_ref, sgw_ref,
                         sgbias_ref, wout_scr, lng_ref, lnb_ref)

    half_q = dq // 2
    p_q0 = project(0, half_q)
    next(tail)
    p_q1 = project(half_q, dq)
    next(tail)
    p_k = project(dq, dq + dkv)
    vt_ref[...] = lax.dot_general(wbf_scr[:, dq + dkv:dq + 2 * dkv], h, (((0,), (1,)), ((), ())),
                                  preferred_element_type=F32).astype(BF16)
    next(tail)
    next(tail)
    for c0, p in ((0, p_q0), (half_q, p_q1)):
        for c, r in roped_tiles(p, scale):
            q_ref[:, c0 + c * LANES:c0 + (c + 1) * LANES] = r
    for c, r in roped_tiles(p_k, 1.0):
        k_ref[:, c * LANES:(c + 1) * LANES] = r
    p_g = project(dq + 2 * dkv, ODD_IN)
    z1 = next(tail)
    z1_ref[...] = z1
    zprev_ref[...] = z1
    g_ref[...] = p_g.astype(BF16)


def _mid(x2, ctx2, mod0, mod1, ys_lat, ys_ctx, rest, glu_w, glu_b, sg_g, sg_b, sg_w, sg_bias,
         w_out, ln_g, ln_b, w_in1, row_tab, col_tab):
    dq = N_HEADS * HD
    dkv = N_KV * HD
    once = dict(pipeline_mode=pl.Buffered(1))
    row = lambda n: pl.BlockSpec((1, n), lambda i: (0, 0))
    lat = lambda i: (jnp.minimum(i, N_LAT_BLK - 1), 0)
    tail_blk = lambda i: (jnp.minimum(i, N_BLK - 1), 0)
    proj_blk = lambda i: (jnp.maximum(i - 1, 0), 0)
    return pl.pallas_call(
        _mid_kernel,
        grid=(N_BLK + 1,),
        in_specs=[
            pl.BlockSpec((TN, D), lat),
            pl.BlockSpec((R_CTX, D), lambda i: (0, 0), **once),
            pl.BlockSpec((8, 3 * D), lambda i: (0, 0)),
            pl.BlockSpec((8, 3 * D), lambda i: (0, 0)),
            pl.BlockSpec((S5_QT * S5_PAIRS, TN // S5_CHUNK, S5_CHUNK * PAIR_BLK),
                         lambda i: (0, jnp.minimum(i, N_LAT_BLK - 1), 0)),
            pl.BlockSpec((S5_QT * S5_PAIRS, R_CTX // S5_CHUNK, S5_CHUNK * PAIR_BLK),
                         lambda i: (0, 0, 0), **once),
            pl.BlockSpec((TN, EVEN_IN - S5_W), tail_blk),
            pl.BlockSpec((S5_W, S5_W), lambda i: (0, 0), **once),
            row(S5_W), row(SG_W), row(SG_W),
            pl.BlockSpec((SG_HEADS // 2, SG_CHUNK, 2 * SG_CHUNK), lambda i: (0, 0, 0)),
            pl.BlockSpec((SG_CHUNK, SG_W), lambda i: (0, 0)),
            pl.BlockSpec((S5_W + SG_W, D), lambda i: (0, 0), **once),
            row(D), row(D),
            pl.BlockSpec((D, ODD_IN), lambda i: (0, 0), **once),
            pl.BlockSpec((2, N_LAT // GRID_W, LANES), lambda i: (0, 0, 0)),
            pl.BlockSpec((2, GRID_W, LANES), lambda i: (0, 0, 0)),
        ],
        out_specs=[
            pl.BlockSpec((TN, D), tail_blk),
            pl.BlockSpec((TN, dq), proj_blk),
            pl.BlockSpec((TN, dkv), proj_blk),
            pl.BlockSpec((dkv, TN), lambda i: (0, jnp.maximum(i - 1, 0))),
            pl.BlockSpec((TN, dq), proj_blk),
        ],
        out_shape=[
            jax.ShapeDtypeStruct((R_ALL, D), F32),
            jax.ShapeDtypeStruct((R_ALL, dq), BF16),
            jax.ShapeDtypeStruct((R_ALL, dkv), BF16),
            jax.ShapeDtypeStruct((dkv, R_ALL), BF16),
            jax.ShapeDtypeStruct((R_ALL, dq), BF16),
        ],
        scratch_shapes=[pltpu.VMEM((TN, D), F32), pltpu.VMEM((D, ODD_IN), BF16),
                        pltpu.VMEM((S5_W + SG_W, D), BF16), pltpu.VMEM((S5_QT, TN, LANES), F32)],
        compiler_params=pltpu.CompilerParams(vmem_limit_bytes=VMEM_LIMIT),
        name="mid",
    )(x2, ctx2, mod0, mod1, ys_lat, ys_ctx, rest, glu_w, glu_b, sg_g, sg_b, sg_w, sg_bias, w_out,
      ln_g, ln_b, w_in1, row_tab, col_tab)


N_QBLK = N_LAT // ATT_BLK
GRP = N_HEADS // N_KV
ATT_SUB = 4
ATT_ROWS = ATT_SUB * ATT_BLK
N_QSTEP = N_QBLK // ATT_SUB
N_ATT_STEPS = B * N_QSTEP
assert ATT_ROWS == TN
assert WINDOW == ATT_BLK
ONES_ROWS = 16


def _attn_kernel(sink_ref, q_ref, kp_ref, kc_ref, kn_ref, kx_ref, vp_ref, vc_ref, vn_ref, vx_ref,
                 g_ref, z_ref, mod_ref, wout_ref, lng_ref, lnb_ref, out_ref, o_scr, wout_scr):
    step = pl.program_id(0)

    @pl.when(step == 0)
    def _():
        o_scr[...] = jnp.zeros((ATT_ROWS, N_HEADS * HD), BF16)
        wout_scr[...] = wout_ref[...].astype(BF16)

    t = jnp.maximum(step - 1, 0)
    gate = mod_ref[pl.ds(_block_mod_row(t), 1), 2 * D:3 * D]
    mix = (o_scr[...].astype(F32) * _silu(g_ref[...].astype(F32))).astype(BF16)
    y = jnp.dot(mix, wout_scr[...], preferred_element_type=F32) * gate
    out_ref[...] = _layer_norm(ALPHA * z_ref[...] + y, lng_ref[...], lnb_ref[...])

    i = jnp.minimum(step, N_ATT_STEPS - 1) % N_QSTEP
    n_win = 3 * ATT_BLK
    n_keys = n_win + N_CTX
    nq = GRP * ATT_BLK
    k_win = jnp.concatenate([kp_ref[...], kc_ref[...], kn_ref[...]], axis=0)
    vt_win = jnp.concatenate([vp_ref[...], vc_ref[...], vn_ref[...]], axis=1)
    k_ctx = kx_ref[...]
    vt_ctx = vx_ref[...]

    kpos = lax.broadcasted_iota(jnp.int32, (ATT_BLK, ATT_BLK), 0)
    qpos = lax.broadcasted_iota(jnp.int32, (ATT_BLK, ATT_BLK), 1)
    tile_q = lambda a: jnp.concatenate([a] * GRP, axis=1)
    qgrp = lax.broadcasted_iota(jnp.int32, (1, nq), 1) // ATT_BLK
    ones = jnp.ones((ONES_ROWS, n_keys), BF16)
    q = q_ref[...]

    units = [(u, h) for u in range(ATT_SUB) for h in range(N_KV)]
    k_all, vt_all, bias = [], [], []
    for u in range(ATT_SUB):
        k_all.append(jnp.concatenate([k_win[u * ATT_BLK:u * ATT_BLK + n_win], k_ctx], axis=0))
        vt_all.append(jnp.concatenate([vt_win[:, u * ATT_BLK:u * ATT_BLK + n_win], vt_ctx], axis=1))
        has_prev = (i > 0) if u == 0 else True
        has_next = (i < N_QSTEP - 1) if u == ATT_SUB - 1 else True
        bias.append((tile_q(jnp.where((kpos >= qpos) & has_prev, 0.0, NEG_INF)),
                     tile_q(jnp.where((kpos <= qpos) & has_next, 0.0, NEG_INF))))
    scores = []
    for u, h in units:
        kh = k_all[u][:, h * HD:(h + 1) * HD]
        qh = jnp.concatenate(
            [q[u * ATT_BLK:(u + 1) * ATT_BLK, (h * GRP + g) * HD:(h * GRP + g + 1) * HD]
             for g in range(GRP)], axis=0)
        scores.append(lax.dot_general(kh, qh, (((1,), (1,)), ((), ())),
                                      preferred_element_type=F32))
    probs = []
    for (u, h), s in zip(units, scores):
        s = jnp.concatenate([s[:ATT_BLK] + bias[u][0], s[ATT_BLK:2 * ATT_BLK],
                             s[2 * ATT_BLK:n_win] + bias[u][1], s[n_win:]], axis=0)
        sink = jnp.zeros((1, nq), F32)
        for g in range(GRP):
            sink = jnp.where(qgrp == g, sink_ref[h * GRP + g] * LOG2E, sink)
        m = jnp.maximum(jnp.max(s, axis=0, keepdims=True), sink)
        probs.append((jnp.exp2(s - m).astype(BF16), jnp.exp2(sink - m)))
    outs = [[] for _ in range(ATT_SUB)]
    for (u, h), (p, p_sink) in zip(units, probs):
        vt1 = jnp.concatenate([vt_all[u][h * HD:(h + 1) * HD], ones], axis=0)
        ov = jnp.dot(vt1, p, preferred_element_type=F32)
        o_t = ov[:HD] / (ov[HD:HD + 1] + p_sink)
        outs[u].extend(o_t[:, g * ATT_BLK:(g + 1) * ATT_BLK] for g in range(GRP))
    for u in range(ATT_SUB):
        o_scr[u * ATT_BLK:(u + 1) * ATT_BLK, :] = jnp.transpose(
            jnp.concatenate(outs[u], axis=0)).astype(BF16)


def _attention_tail(sink, q, k, vt, g, z1, mod, w_out, ln_g, ln_b):
    dq = N_HEADS * HD
    dkv = N_KV * HD
    ctx_blk0 = R_LAT // N_CTX
    cur = lambda s: jnp.minimum(s, N_ATT_STEPS - 1)
    batch = lambda s: cur(s) // N_QSTEP
    qstep = lambda s: cur(s) % N_QSTEP
    prev_blk = lambda s: batch(s) * N_QBLK + jnp.maximum(ATT_SUB * qstep(s) - 1, 0)
    next_blk = lambda s: batch(s) * N_QBLK + jnp.minimum(ATT_SUB * (qstep(s) + 1), N_QBLK - 1)
    tail = lambda s: jnp.maximum(s - 1, 0)
    once = dict(pipeline_mode=pl.Buffered(1))
    row = lambda n: pl.BlockSpec((1, n), lambda s, sk: (0, 0))
    return pl.pallas_call(
        _attn_kernel,
        grid_spec=pltpu.PrefetchScalarGridSpec(
            num_scalar_prefetch=1,
            grid=(N_ATT_STEPS + 1,),
            in_specs=[
                pl.BlockSpec((ATT_ROWS, dq), lambda s, sk: (cur(s), 0)),
                pl.BlockSpec((ATT_BLK, dkv), lambda s, sk: (prev_blk(s), 0)),
                pl.BlockSpec((ATT_ROWS, dkv), lambda s, sk: (cur(s), 0)),
                pl.BlockSpec((ATT_BLK, dkv), lambda s, sk: (next_blk(s), 0)),
                pl.BlockSpec((N_CTX, dkv), lambda s, sk: (ctx_blk0 + batch(s), 0)),
                pl.BlockSpec((dkv, ATT_BLK), lambda s, sk: (0, prev_blk(s))),
                pl.BlockSpec((dkv, ATT_ROWS), lambda s, sk: (0, cur(s))),
                pl.BlockSpec((dkv, ATT_BLK), lambda s, sk: (0, next_blk(s))),
                pl.BlockSpec((dkv, N_CTX), lambda s, sk: (0, ctx_blk0 + batch(s))),
                pl.BlockSpec((ATT_ROWS, dq), lambda s, sk: (tail(s), 0)),
                pl.BlockSpec((ATT_ROWS, D), lambda s, sk: (tail(s), 0)),
                pl.BlockSpec((8, 3 * D), lambda s, sk: (0, 0)),
                pl.BlockSpec((dq, D), lambda s, sk: (0, 0), **once),
                row(D), row(D),
            ],
            out_specs=pl.BlockSpec((ATT_ROWS, D), lambda s, sk: (tail(s), 0)),
            scratch_shapes=[pltpu.VMEM((ATT_ROWS, dq), BF16), pltpu.VMEM((dq, D), BF16)],
        ),
        out_shape=jax.ShapeDtypeStruct((R_LAT, D), F32),
        compiler_params=pltpu.CompilerParams(vmem_limit_bytes=VMEM_LIMIT),
        name="attention",
    )(sink, q, k, k, k, k, vt, vt, vt, vt, g, z1, mod, w_out, ln_g, ln_b)


def kernel(x, c, ctx, c_ctx, mod_w, mod_b, ln_g, ln_b, e_w_in, e_w_out, s5_lam_re, s5_lam_im,
           s5_log_dt, s5_b_re, s5_b_im, s5_c_re, s5_c_im, s5_d, glu_w, glu_b, sg_ln_g, sg_ln_b,
           sg_w, sg_b, o_w_in, o_w_out, o_sink):
    x2 = x.reshape(R_LAT, D)
    ctx2 = ctx.reshape(R_CTX, D)
    mod = _modulation(c, c_ctx, mod_w, mod_b)

    vp, rest = _in0(x2, ctx2, mod[0], e_w_in[0])
    t_mat, e_mat, ft_mat, al_re, al_im = _s5_prep(
        s5_lam_re[0], s5_lam_im[0], s5_log_dt[0], s5_b_re[0], s5_b_im[0], s5_c_re[0], s5_c_im[0],
        s5_d[0])
    ys_lat, ys_ctx = _s5_out(vp, al_re, al_im, e_mat, t_mat, ft_mat)
    sg_bias = jnp.repeat(jnp.transpose(sg_b[0]), SG_W // SG_HEADS, axis=1)
    sg_w2 = jnp.transpose(sg_w[0].reshape(SG_HEADS // 2, 2, SG_CHUNK, SG_CHUNK), (0, 2, 1, 3))
    sg_w2 = sg_w2.reshape(SG_HEADS // 2, SG_CHUNK, 2 * SG_CHUNK).astype(BF16)
    row_tab, col_tab = _rope_tables()
    z1, q, k, vt, g = _mid(
        x2, ctx2, mod[0], mod[1], ys_lat, ys_ctx, rest, glu_w[0].astype(BF16),
        glu_b[0].reshape(1, S5_W), sg_ln_g[0].reshape(1, SG_W), sg_ln_b[0].reshape(1, SG_W),
        sg_w2, sg_bias, e_w_out[0], ln_g[0].reshape(1, D),
        ln_b[0].reshape(1, D), o_w_in[0], row_tab, col_tab)

    out = _attention_tail(o_sink[0], q, k, vt, g, z1, mod[1], o_w_out[0],
                          ln_g[1].reshape(1, D), ln_b[1].reshape(1, D))
    return out.reshape(B, N_LAT, D)
```

```python
import math

import jax
import jax.numpy as jnp
from jax import lax
from jax.experimental import pallas as pl
from jax.experimental.pallas import tpu as pltpu

F32 = jnp.float32
BF16 = jnp.bfloat16

D = 1024
B = 2
N_LAT = 8192
N_CTX = 256
DEPTH = 2
GRID_W = 64
S5_W = 512
S5_GC = 16
S5_G = 32
S5_P = 64
SG_W = 512
SG_HEADS = 8
SG_CHUNK = 128
N_HEADS = 16
N_KV = 4
HD = 64
WINDOW = 128
ATT_BLK = 128
ROPE_BASE = 10000.0
NEG_INF = -1e30
LN_EPS = 1e-5
ALPHA = (2 * DEPTH) ** 0.25
LOG2E = math.log2(math.e)
EVEN_IN = 2 * S5_W + 3 * SG_W
ODD_IN = 2 * N_HEADS * HD + 2 * N_KV * HD

LANES = 128
VMEM_LIMIT = 56 * 1024 * 1024

R_LAT = B * N_LAT
R_CTX = B * N_CTX
R_ALL = R_LAT + R_CTX
TN = 512
N_LAT_BLK = R_LAT // TN
N_BLK = R_ALL // TN
LAT_BLK_PER_BATCH = N_LAT // TN
CTX_MOD_ROW = B

S5_CHUNK = 8
S5_QT = S5_W // LANES
S5_GPT = LANES // S5_GC
S5_ST = S5_GPT * S5_P
S5_PAIRS = S5_GPT // 2
PAIR_BLK = 2 * S5_GC
LAT_CHUNKS = N_LAT // S5_CHUNK
CTX_CHUNKS = N_CTX // S5_CHUNK
BATCH_CHUNKS = CTX_CHUNKS + LAT_CHUNKS
N_SEG = 16
SEG_CHUNKS = BATCH_CHUNKS // N_SEG
BWD_RESET_STEP = LAT_CHUNKS - (N_SEG - 1) * SEG_CHUNKS


def _sigmoid(x):
    return 1.0 / (1.0 + jnp.exp(-x))


def _silu(x):
    return x * _sigmoid(x)


def _gelu_tanh(x):
    return 0.5 * x * (1.0 + jnp.tanh(math.sqrt(2.0 / math.pi) * (x + 0.044715 * (x * x * x))))


def _layer_norm(x, g, b):
    mu = jnp.mean(x, axis=-1, keepdims=True)
    xc = x - mu
    var = jnp.mean(xc * xc, axis=-1, keepdims=True)
    return xc * lax.rsqrt(var + LN_EPS) * g + b


def _split_bf16(x):
    hi = x.astype(BF16)
    return hi, (x - hi.astype(F32)).astype(BF16)


def _dot_split(x, y_split, dims):
    x_hi, x_lo = _split_bf16(x)
    y_hi, y_lo = y_split
    dot = lambda a, b: lax.dot_general(a, b, dims, preferred_element_type=F32)
    return dot(x_hi, y_hi) + (dot(x_hi, y_lo) + dot(x_lo, y_hi))


def _block_mod_row(i):
    return jnp.where(i < N_LAT_BLK, i // LAT_BLK_PER_BATCH, CTX_MOD_ROW)


def _mod_kernel(cv_ref, w_ref, b_ref, o_ref):
    s = _silu(cv_ref[...])
    o_ref[0] = _dot_split(s, _split_bf16(w_ref[0]), (((1,), (0,)), ((), ()))) + b_ref[0]


def _modulation(c, c_ctx, mod_w, mod_b):
    cv = jnp.concatenate([c, c_ctx[None], jnp.zeros((8 - B - 1, D), F32)], axis=0)
    tn = 1024
    return pl.pallas_call(
        _mod_kernel,
        grid=(DEPTH, 3 * D // tn),
        in_specs=[
            pl.BlockSpec((8, D), lambda l, j: (0, 0)),
            pl.BlockSpec((1, D, tn), lambda l, j: (l, 0, j)),
            pl.BlockSpec((1, 1, tn), lambda l, j: (l, 0, j)),
        ],
        out_specs=pl.BlockSpec((1, 8, tn), lambda l, j: (l, 0, j)),
        out_shape=jax.ShapeDtypeStruct((DEPTH, 8, 3 * D), F32),
        compiler_params=pltpu.CompilerParams(vmem_limit_bytes=VMEM_LIMIT),
        name="modulation",
    )(cv, mod_w, mod_b.reshape(DEPTH, 1, 3 * D))


def _block_transpose4(tiles):
    tiles = list(tiles)
    blk = lax.broadcasted_iota(jnp.int32, tiles[0].shape, 1) // PAIR_BLK
    for dist in (2, 1):
        keep = (blk & dist) == 0
        for i in range(len(tiles)):
            if i & dist:
                continue
            lo, hi = tiles[i], tiles[i + dist]
            tiles[i] = jnp.where(keep, lo, pltpu.roll(hi, dist * PAIR_BLK, axis=1))
            tiles[i + dist] = jnp.where(keep, pltpu.roll(lo, LANES - dist * PAIR_BLK, axis=1), hi)
    return tiles


def _in0_kernel(x_ref, ctx_ref, mod_ref, w_ref, sg_g_ref, sg_b_ref, vp_ref, rest_ref, xa_scr,
                wbf_scr):
    L = S5_CHUNK
    i = pl.program_id(0)

    @pl.when(i == 0)
    def _():
        wbf_scr[...] = w_ref[...].astype(BF16)

    z = jnp.where(i < N_LAT_BLK, x_ref[...], ctx_ref[...])
    m = mod_ref[pl.ds(_block_mod_row(i), 1), :]
    h = z * (1.0 + m[:, D:2 * D]) + m[:, :D]
    p = jnp.dot(h.astype(BF16), wbf_scr[...], preferred_element_type=F32)
    col = lambda n: slice(S5_W + n * SG_W, S5_W + (n + 1) * SG_W)
    rest_ref[:, 0:SG_W] = _silu(p[:, col(0)]).astype(BF16)
    rest_ref[:, SG_W:2 * SG_W] = p[:, col(1)].astype(BF16)
    rest_ref[:, 2 * SG_W:3 * SG_W] = _layer_norm(p[:, col(2)], sg_g_ref[...],
                                                 sg_b_ref[...]).astype(BF16)
    rest_ref[:, 3 * SG_W:4 * SG_W] = _silu(p[:, col(3)]).astype(BF16)
    for q in range(S5_QT):
        xa_scr[q] = p[:, q * LANES:(q + 1) * LANES]
        tiles = [xa_scr[q, pl.ds(t, TN // L, stride=L), :] for t in range(L)]
        halves = [_block_transpose4(tiles[k * S5_PAIRS:(k + 1) * S5_PAIRS])
                  for k in range(L // S5_PAIRS)]
        for pr in range(S5_PAIRS):
            vp_ref[q * S5_PAIRS + pr] = jnp.concatenate(
                [half[pr] for half in halves], axis=-1).astype(BF16)


def _in0(x2, ctx2, mod, w_in, sg_g, sg_b):
    pair_w = S5_CHUNK * PAIR_BLK
    return pl.pallas_call(
        _in0_kernel,
        grid=(N_BLK,),
        in_specs=[
            pl.BlockSpec((TN, D), lambda i: (jnp.minimum(i, N_LAT_BLK - 1), 0)),
            pl.BlockSpec((R_CTX, D), lambda i: (0, 0)),
            pl.BlockSpec((8, 3 * D), lambda i: (0, 0)),
            pl.BlockSpec((D, EVEN_IN), lambda i: (0, 0)),
            pl.BlockSpec((1, SG_W), lambda i: (0, 0)),
            pl.BlockSpec((1, SG_W), lambda i: (0, 0)),
        ],
        out_specs=[
            pl.BlockSpec((S5_QT * S5_PAIRS, TN // S5_CHUNK, pair_w), lambda i: (0, i, 0)),
            pl.BlockSpec((TN, EVEN_IN - S5_W), lambda i: (i, 0)),
        ],
        out_shape=[
            jax.ShapeDtypeStruct((S5_QT * S5_PAIRS, R_ALL // S5_CHUNK, pair_w), BF16),
            jax.ShapeDtypeStruct((R_ALL, EVEN_IN - S5_W), BF16),
        ],
        scratch_shapes=[pltpu.VMEM((S5_QT, TN, LANES), F32), pltpu.VMEM((D, EVEN_IN), BF16)],
        compiler_params=pltpu.CompilerParams(vmem_limit_bytes=VMEM_LIMIT),
        name="in0",
    )(x2, ctx2, mod, w_in, sg_g, sg_b)


def _s5_prep_kernel(rows_ref, bre_ref, bim_ref, cre_ref, cim_ref, dsk_ref, t_ref, e_ref, ft_ref,
                    alr_ref, ali_ref):
    L = S5_CHUNK
    tile = (LANES, S5_ST)
    same_group = (lax.broadcasted_iota(jnp.int32, tile, 0) // S5_GC
                  == lax.broadcasted_iota(jnp.int32, tile, 1) // S5_P)
    nt_dims = (((1,), (1,)), ((), ()))
    nn_dims = (((1,), (0,)), ((), ()))
    pick_ch = (lax.broadcasted_iota(jnp.int32, (LANES, S5_GC), 0) % S5_GC
               == lax.broadcasted_iota(jnp.int32, (LANES, S5_GC), 1)).astype(BF16)
    rep_st = (lax.broadcasted_iota(jnp.int32, (S5_P, S5_ST), 0)
              == lax.broadcasted_iota(jnp.int32, (S5_P, S5_ST), 1) % S5_P).astype(BF16)

    def spread_b(b):
        hi, lo = _split_bf16(b)
        dot = lambda v: lax.dot_general(pick_ch, v, nt_dims, preferred_element_type=F32)
        return jnp.where(same_group, dot(hi) + dot(lo), 0.0)

    def spread_c(c):
        hi, lo = _split_bf16(c)
        dot = lambda v: lax.dot_general(v, rep_st, nn_dims, preferred_element_type=F32)
        return jnp.where(same_group, dot(hi) + dot(lo), 0.0)

    pair_rows = lambda pr: slice(pr * PAIR_BLK, (pr + 1) * PAIR_BLK)
    pair_lanes = lambda pr: slice(pr * LANES, (pr + 1) * LANES)
    lag_strip = []
    for d in range(2):
        lam_re = rows_ref[d, 0]
        lam_im = rows_ref[d, 1]
        dt = jnp.exp(rows_ref[d, 2])
        pw = []
        for k in range(L + 1):
            mag = jnp.exp(lam_re * dt * float(k))
            ang = lam_im * dt * float(k)
            pw.append((mag * jnp.cos(ang), mag * jnp.sin(ang)))
        alr_ref[d] = pw[L][0]
        ali_ref[d] = pw[L][1]
        den = lam_re * lam_re + lam_im * lam_im
        nr = pw[1][0] - 1.0
        ni = pw[1][1]
        f_re = (nr * lam_re + ni * lam_im) / den
        f_im = (ni * lam_re - nr * lam_im) / den
        b_re = spread_b(bre_ref[d])
        b_im = spread_b(bim_ref[d])
        bb_re = f_re * b_re - f_im * b_im
        bb_im = f_re * b_im + f_im * b_re
        c_re = spread_c(cre_ref[d])
        c_im = spread_c(cim_ref[d])
        col_re = slice(2 * d * LANES, (2 * d + 1) * LANES)
        col_im = slice((2 * d + 1) * LANES, (2 * d + 2) * LANES)
        strips = []
        for pr in range(S5_PAIRS):
            blk = lambda a: a[pair_rows(pr), pair_lanes(pr)]
            p_re = [blk(jnp.broadcast_to(pw[k][0], tile)) for k in range(L + 1)]
            p_im = [blk(jnp.broadcast_to(pw[k][1], tile)) for k in range(L + 1)]
            bbr, bbi, cr, ci = blk(bb_re), blk(bb_im), blk(c_re), blk(c_im)
            for k in range(L):
                s = (L - 1 - k) if d == 0 else k
                rows = slice(s * PAIR_BLK, (s + 1) * PAIR_BLK)
                e_ref[0, pr, rows, col_re] = (p_re[k] * bbr - p_im[k] * bbi).astype(BF16)
                e_ref[0, pr, rows, col_im] = (p_re[k] * bbi + p_im[k] * bbr).astype(BF16)
            cw = [(cr * p_re[k] - ci * p_im[k], -(cr * p_im[k] + ci * p_re[k]))
                  for k in range(L + 1)]
            for t in range(L):
                k = (t + 1) if d == 0 else (L - t)
                rows = slice(t * PAIR_BLK, (t + 1) * PAIR_BLK)
                ft_ref[0, pr, rows, col_re] = cw[k][0].astype(BF16)
                ft_ref[0, pr, rows, col_im] = cw[k][1].astype(BF16)
            lags = range(L) if d == 0 else range(L - 1, -1, -1)
            c_stack = jnp.concatenate(
                [jnp.concatenate([cw[k][0], cw[k][1]], axis=-1) for k in lags], axis=0)
            strips.append(_dot_split(jnp.concatenate([bbr, bbi], axis=-1),
                                     _split_bf16(c_stack), nt_dims))
        lag_strip.append(strips)

    strip_w = 2 * L * PAIR_BLK
    mid = (L - 1) * PAIR_BLK
    widen = lambda a: jnp.concatenate([a, jnp.zeros((a.shape[0], strip_w - a.shape[1]), F32)], axis=-1)
    row = lax.broadcasted_iota(jnp.int32, (PAIR_BLK, strip_w), 0)
    lane = lax.broadcasted_iota(jnp.int32, (PAIR_BLK, strip_w), 1)
    d_row = widen(dsk_ref[0])
    for pr in range(S5_PAIRS):
        strip = (widen(lag_strip[1][pr])
                 + pltpu.roll(widen(lag_strip[0][pr]), mid, axis=1)
                 + jnp.where(lane == row + mid,
                             pltpu.roll(d_row, mid - pr * PAIR_BLK, axis=1), 0.0))
        for s in range(L):
            shift = (strip_w - (L - 1 - s) * PAIR_BLK) % strip_w
            blk = pltpu.roll(strip, shift, axis=1) if shift else strip
            t_ref[0, pr, s * PAIR_BLK:(s + 1) * PAIR_BLK, :] = blk[:, :L * PAIR_BLK].astype(BF16)


def _s5_prep(lam_re, lam_im, log_dt, b_re, b_im, c_re, c_im, d_skip):
    L = S5_CHUNK
    n_st = S5_G * S5_P
    pair_w = L * PAIR_BLK
    rows = jnp.stack([lam_re.reshape(2, 1, n_st), lam_im.reshape(2, 1, n_st),
                      jnp.repeat(log_dt, S5_P, axis=1).reshape(2, 1, n_st)], axis=1)
    b_spec = pl.BlockSpec((2, S5_ST, S5_GC), lambda q: (0, q, 0))
    c_spec = pl.BlockSpec((2, LANES, S5_P), lambda q: (0, q, 0))
    return pl.pallas_call(
        _s5_prep_kernel,
        grid=(S5_QT,),
        in_specs=[
            pl.BlockSpec((2, 3, 1, S5_ST), lambda q: (0, 0, 0, q)),
            b_spec, b_spec, c_spec, c_spec,
            pl.BlockSpec((1, 1, LANES), lambda q: (q, 0, 0)),
        ],
        out_specs=[
            pl.BlockSpec((1, S5_PAIRS, pair_w, pair_w), lambda q: (q, 0, 0, 0)),
            pl.BlockSpec((1, S5_PAIRS, pair_w, 4 * LANES), lambda q: (q, 0, 0, 0)),
            pl.BlockSpec((1, S5_PAIRS, pair_w, 4 * LANES), lambda q: (q, 0, 0, 0)),
            pl.BlockSpec((2, 1, S5_ST), lambda q: (q, 0, 0)),
            pl.BlockSpec((2, 1, S5_ST), lambda q: (q, 0, 0)),
        ],
        out_shape=[
            jax.ShapeDtypeStruct((S5_QT, S5_PAIRS, pair_w, pair_w), BF16),
            jax.ShapeDtypeStruct((S5_QT, S5_PAIRS, pair_w, 4 * LANES), BF16),
            jax.ShapeDtypeStruct((S5_QT, S5_PAIRS, pair_w, 4 * LANES), BF16),
            jax.ShapeDtypeStruct((2 * S5_QT, 1, S5_ST), F32),
            jax.ShapeDtypeStruct((2 * S5_QT, 1, S5_ST), F32),
        ],
        compiler_params=pltpu.CompilerParams(vmem_limit_bytes=VMEM_LIMIT),
        name="s5_prep",
    )(rows, b_re.reshape(2, n_st, S5_GC), b_im.reshape(2, n_st, S5_GC),
      c_re.reshape(2, S5_W, S5_P), c_im.reshape(2, S5_W, S5_P), d_skip.reshape(S5_QT, 1, LANES))


N_SLAB = S5_ST // LANES


def _cmul(ar, ai, br, bi):
    return ar * br - ai * bi, ar * bi + ai * br


def _cpow(ar, ai, n):
    res = None
    while n:
        if n & 1:
            res = (ar, ai) if res is None else _cmul(res[0], res[1], ar, ai)
        n >>= 1
        if n:
            ar, ai = _cmul(ar, ai, ar, ai)
    return res


def _scan_boundary_states(loc_ref, out_ref, ar_ref, ai_ref, backward):
    a_row = [(ar_ref[0][:, k * LANES:(k + 1) * LANES], ai_ref[0][:, k * LANES:(k + 1) * LANES])
             for k in range(N_SLAB)]
    a_tile = [(jnp.broadcast_to(r, (N_SEG, LANES)), jnp.broadcast_to(i, (N_SEG, LANES)))
              for r, i in a_row]

    def sweep(lo, hi, carry, store):
        def step(i, carry):
            off = (SEG_CHUNKS - 1 - i) if backward else i
            rows = pl.ds(off, N_SEG, stride=SEG_CHUNKS)
            new = []
            for k, (cr, ci) in enumerate(carry):
                xr = loc_ref[k, rows, :]
                xi = loc_ref[N_SLAB + k, rows, :]
                if store:
                    out_ref[k, rows, :] = cr
                    out_ref[N_SLAB + k, rows, :] = ci
                nr, ni = _cmul(a_tile[k][0], a_tile[k][1], cr, ci)
                new.append((nr + xr, ni + xi))
            return tuple(new)
        return lax.fori_loop(lo, hi, step, carry)

    def full(carry, store):
        if not backward:
            return sweep(0, SEG_CHUNKS, carry, store)
        carry = sweep(0, BWD_RESET_STEP, carry, store)
        first = lax.broadcasted_iota(jnp.int32, (N_SEG, LANES), 0) == 0
        carry = tuple((jnp.where(first, 0.0, cr), jnp.where(first, 0.0, ci)) for cr, ci in carry)
        return sweep(BWD_RESET_STEP, SEG_CHUNKS, carry, store)

    zero_tile = jnp.zeros((N_SEG, LANES), F32)
    ends = full(tuple((zero_tile, zero_tile) for _ in range(N_SLAB)), False)

    starts = []
    for k, (er, ei) in enumerate(ends):
        pr, pi = _cpow(a_row[k][0], a_row[k][1], SEG_CHUNKS)
        if backward:
            order = range(N_SEG - 1, -1, -1)
            g = (er[0:1], ei[0:1])
        else:
            order = range(N_SEG)
            g = (jnp.zeros((1, LANES), F32), jnp.zeros((1, LANES), F32))
        rows_r = [None] * N_SEG
        rows_i = [None] * N_SEG
        for j in order:
            rows_r[j], rows_i[j] = g
            nr, ni = _cmul(pr, pi, g[0], g[1])
            g = (nr + er[j:j + 1], ni + ei[j:j + 1])
        starts.append((jnp.concatenate(rows_r, axis=0), jnp.concatenate(rows_i, axis=0)))
    full(tuple(starts), True)


def _s5_out_kernel(vl_ref, vc_ref, ar_ref, ai_ref, e_ref, t_ref, ft_ref, yl_ref, yc_ref, st_ref):
    n_dir = 2 * N_SLAB
    rows = [jnp.concatenate([vc_ref[pr], vl_ref[pr]], axis=0) for pr in range(S5_PAIRS)]
    for pr in range(S5_PAIRS):
        st = jnp.dot(rows[pr], e_ref[0, pr], preferred_element_type=F32)
        for comp in range(4):
            st_ref[comp * N_SLAB + pr] = st[:, comp * LANES:(comp + 1) * LANES]
    for d in range(2):
        st_d = st_ref.at[pl.ds(d * n_dir, n_dir)]
        _scan_boundary_states(st_d, st_d, ar_ref.at[pl.ds(d, 1)], ai_ref.at[pl.ds(d, 1)], d == 1)

    for pr in range(S5_PAIRS):
        st = jnp.concatenate([st_ref[comp * N_SLAB + pr] for comp in range(4)],
                             axis=-1).astype(BF16)
        y = (jnp.dot(rows[pr], t_ref[0, pr], preferred_element_type=F32)
             + lax.dot_general(st, ft_ref[0, pr], (((1,), (1,)), ((), ())),
                               preferred_element_type=F32))
        yc_ref[pr] = y[:CTX_CHUNKS]
        yl_ref[pr] = y[CTX_CHUNKS:]


def _s5_out(vp, al_re, al_im, e_mat, t_mat, ft_mat):
    pair_w = S5_CHUNK * PAIR_BLK
    ctx_blk0 = (R_LAT // S5_CHUNK) // CTX_CHUNKS
    return pl.pallas_call(
        _s5_out_kernel,
        grid=(S5_QT, B),
        in_specs=[
            pl.BlockSpec((S5_PAIRS, LAT_CHUNKS, pair_w), lambda q, b: (q, b, 0)),
            pl.BlockSpec((S5_PAIRS, CTX_CHUNKS, pair_w), lambda q, b: (q, ctx_blk0 + b, 0)),
            pl.BlockSpec((2, 1, S5_ST), lambda q, b: (q, 0, 0)),
            pl.BlockSpec((2, 1, S5_ST), lambda q, b: (q, 0, 0)),
            pl.BlockSpec((1, S5_PAIRS, pair_w, 4 * LANES), lambda q, b: (q, 0, 0, 0)),
            pl.BlockSpec((1, S5_PAIRS, pair_w, pair_w), lambda q, b: (q, 0, 0, 0)),
            pl.BlockSpec((1, S5_PAIRS, pair_w, 4 * LANES), lambda q, b: (q, 0, 0, 0)),
        ],
        out_specs=[
            pl.BlockSpec((S5_PAIRS, LAT_CHUNKS, pair_w), lambda q, b: (q, b, 0)),
            pl.BlockSpec((S5_PAIRS, CTX_CHUNKS, pair_w), lambda q, b: (q, b, 0)),
        ],
        out_shape=[
            jax.ShapeDtypeStruct((S5_QT * S5_PAIRS, B * LAT_CHUNKS, pair_w), F32),
            jax.ShapeDtypeStruct((S5_QT * S5_PAIRS, B * CTX_CHUNKS, pair_w), F32),
        ],
        scratch_shapes=[pltpu.VMEM((4 * N_SLAB, BATCH_CHUNKS, LANES), F32)],
        compiler_params=pltpu.CompilerParams(vmem_limit_bytes=VMEM_LIMIT),
        name="s5_out",
    )(vp, vp, al_re, al_im, e_mat, t_mat, ft_mat)


def _tail0_stages(z, ys, rest_ref, gate, gluw_ref, glub_ref, sgw_ref, sgbias_ref, wout_ref,
                  lng_ref, lnb_ref):
    silu_ga = rest_ref[:, 0:SG_W].astype(F32)
    u = rest_ref[:, SG_W:2 * SG_W].astype(F32)
    vn = rest_ref[:, 2 * SG_W:3 * SG_W]
    silu_gb = rest_ref[:, 3 * SG_W:4 * SG_W].astype(F32)

    ya = _gelu_tanh(ys)
    glu = jnp.dot(ya.astype(BF16), gluw_ref[...], preferred_element_type=F32) + glub_ref[...]
    yield None
    ya = ya * _sigmoid(glu) * silu_ga
    yield None

    lane = lax.broadcasted_iota(jnp.int32, (SG_CHUNK, LANES), 1)
    first_head = lane < (SG_W // SG_HEADS)
    chunks = []
    for c in range(TN // SG_CHUNK):
        tiles = []
        for j in range(SG_W // LANES):
            vt = vn[c * SG_CHUNK:(c + 1) * SG_CHUNK, j * LANES:(j + 1) * LANES]
            zero = jnp.zeros_like(vt)
            v2 = jnp.concatenate([jnp.where(first_head, vt, zero), jnp.where(first_head, zero, vt)],
                                 axis=0)
            tiles.append(jnp.dot(sgw_ref[j], v2, preferred_element_type=F32))
        chunks.append(jnp.concatenate(tiles, axis=-1) + sgbias_ref[...])
    s = jnp.concatenate(chunks, axis=0)
    yb = u * s * silu_gb
    yield None

    mix = jnp.concatenate([ya, yb], axis=-1).astype(BF16)
    y = jnp.dot(mix, wout_ref[...], preferred_element_type=F32) * gate
    yield None
    yield _layer_norm(ALPHA * z + y, lng_ref[...], lnb_ref[...])


def _rope_tables():
    nf = HD // 4
    n_rows = N_LAT // GRID_W
    lane = jnp.arange(LANES)
    inv = ROPE_BASE ** (-(lane % nf).astype(F32) / nf)
    by_row = ((lane % HD) // (HD // 2) == 0)[None, :]
    sign = jnp.where((lane % (HD // 2)) < nf, -1.0, 1.0)[None, :]
    row_ang = jnp.arange(n_rows, dtype=F32)[:, None] * inv[None, :]
    col_ang = jnp.arange(GRID_W, dtype=F32)[:, None] * inv[None, :]
    zero = jnp.zeros((), F32)
    row_tab = jnp.stack([jnp.where(by_row, jnp.cos(row_ang), zero),
                         jnp.where(by_row, sign * jnp.sin(row_ang), zero)])
    col_tab = jnp.stack([jnp.where(by_row, zero, jnp.cos(col_ang)),
                         jnp.where(by_row, zero, sign * jnp.sin(col_ang))])
    return row_tab, col_tab


def _rope_tile(x, cos, sin, first_half):
    nf = HD // 4
    partner = jnp.where(first_half, pltpu.roll(x, LANES - nf, axis=1), pltpu.roll(x, nf, axis=1))
    return x * cos + partner * sin


def _mid_kernel(x_ref, ctx_ref, mod0_ref, mod1_ref, ysl_ref, ysc_ref, rest_ref, gluw_ref, glub_ref,
                sgw_ref, sgbias_ref, wout_ref, lng_ref, lnb_ref, w_ref,
                rtab_ref, ctab_ref, z1_ref, q_ref, k_ref, vt_ref, g_ref, zprev_ref, wbf_scr,
                wout_scr, ys_scr):
    i = pl.program_id(0)
    dq = N_HEADS * HD
    dkv = N_KV * HD

    @pl.when(i == 0)
    def _():
        zprev_ref[...] = jnp.zeros((TN, D), F32)
        wbf_scr[...] = w_ref[...].astype(BF16)
        wout_scr[...] = wout_ref[...].astype(BF16)

    j = jnp.maximum(i - 1, 0)
    m1 = mod1_ref[pl.ds(_block_mod_row(j), 1), :]
    h = (zprev_ref[...] * (1.0 + m1[:, D:2 * D]) + m1[:, :D]).astype(BF16)
    grid_row0 = (j % LAT_BLK_PER_BATCH) * (TN // GRID_W)
    table = lambda t: jnp.concatenate(
        [rtab_ref[t, pl.ds(grid_row0 + r, 1), :] + ctab_ref[t] for r in range(TN // GRID_W)],
        axis=0)
    cos = jnp.where(j < N_LAT_BLK, table(0), 1.0)
    sin = jnp.where(j < N_LAT_BLK, table(1), 0.0)
    lane = lax.broadcasted_iota(jnp.int32, (TN, LANES), 1)
    first_half = (lane % (HD // 2)) < (HD // 4)
    scale = HD ** -0.5 * LOG2E

    def project(c0, c1):
        return jnp.dot(h, wbf_scr[:, c0:c1], preferred_element_type=F32)

    def roped_tiles(p, mult):
        for c in range(p.shape[1] // LANES):
            r = _rope_tile(p[:, c * LANES:(c + 1) * LANES], cos, sin, first_half)
            yield c, (r * mult if mult != 1.0 else r).astype(BF16)

    t = jnp.minimum(i, N_BLK - 1)
    z = jnp.where(t < N_LAT_BLK, x_ref[...], ctx_ref[...])
    yp = jnp.where(t < N_LAT_BLK, ysl_ref[...], ysc_ref[...])
    for qt in range(S5_QT):
        for k in range(S5_CHUNK // S5_PAIRS):
            tiles = _block_transpose4([yp[qt * S5_PAIRS + pr][:, k * LANES:(k + 1) * LANES]
                                       for pr in range(S5_PAIRS)])
            for n, tile in enumerate(tiles):
                ys_scr[qt, pl.ds(k * S5_PAIRS + n, TN // S5_CHUNK, stride=S5_CHUNK), :] = tile
    ys = jnp.concatenate([ys_scr[qt] for qt in range(S5_QT)], axis=-1)
    gate = mod0_ref[pl.ds(_block_mod_row(t), 1), 2 * D:3 * D]
    tail = _tail0_stages(z, ys, rest_ref, gate, gluw_ref, glub_ref, sgw_ref, sgbias_ref,
                         wout_scr, lng_ref, lnb_ref)

    half_q = dq // 2
    p_q0 = project(0, half_q)
    next(tail)
    p_q1 = project(half_q, dq)
    next(tail)
    p_k = project(dq, dq + dkv)
    vt_ref[...] = lax.dot_general(wbf_scr[:, dq + dkv:dq + 2 * dkv], h, (((0,), (1,)), ((), ())),
                                  preferred_element_type=F32).astype(BF16)
    next(tail)
    next(tail)
    for c0, p in ((0, p_q0), (half_q, p_q1)):
        for c, r in roped_tiles(p, scale):
            q_ref[:, c0 + c * LANES:c0 + (c + 1) * LANES] = r
    for c, r in roped_tiles(p_k, 1.0):
        k_ref[:, c * LANES:(c + 1) * LANES] = r
    p_g = project(dq + 2 * dkv, ODD_IN)
    z1 = next(tail)
    z1_ref[...] = z1
    zprev_ref[...] = z1
    g_ref[...] = p_g.astype(BF16)


def _mid(x2, ctx2, mod0, mod1, ys_lat, ys_ctx, rest, glu_w, glu_b, sg_w, sg_bias,
         w_out, ln_g, ln_b, w_in1, row_tab, col_tab):
    dq = N_HEADS * HD
    dkv = N_KV * HD
    once = dict(pipeline_mode=pl.Buffered(1))
    row = lambda n: pl.BlockSpec((1, n), lambda i: (0, 0))
    lat = lambda i: (jnp.minimum(i, N_LAT_BLK - 1), 0)
    tail_blk = lambda i: (jnp.minimum(i, N_BLK - 1), 0)
    proj_blk = lambda i: (jnp.maximum(i - 1, 0), 0)
    return pl.pallas_call(
        _mid_kernel,
        grid=(N_BLK + 1,),
        in_specs=[
            pl.BlockSpec((TN, D), lat),
            pl.BlockSpec((R_CTX, D), lambda i: (0, 0), **once),
            pl.BlockSpec((8, 3 * D), lambda i: (0, 0)),
            pl.BlockSpec((8, 3 * D), lambda i: (0, 0)),
            pl.BlockSpec((S5_QT * S5_PAIRS, TN // S5_CHUNK, S5_CHUNK * PAIR_BLK),
                         lambda i: (0, jnp.minimum(i, N_LAT_BLK - 1), 0)),
            pl.BlockSpec((S5_QT * S5_PAIRS, R_CTX // S5_CHUNK, S5_CHUNK * PAIR_BLK),
                         lambda i: (0, 0, 0), **once),
            pl.BlockSpec((TN, EVEN_IN - S5_W), tail_blk),
            pl.BlockSpec((S5_W, S5_W), lambda i: (0, 0), **once),
            row(S5_W),
            pl.BlockSpec((SG_HEADS // 2, SG_CHUNK, 2 * SG_CHUNK), lambda i: (0, 0, 0)),
            pl.BlockSpec((SG_CHUNK, SG_W), lambda i: (0, 0)),
            pl.BlockSpec((S5_W + SG_W, D), lambda i: (0, 0), **once),
            row(D), row(D),
            pl.BlockSpec((D, ODD_IN), lambda i: (0, 0), **once),
            pl.BlockSpec((2, N_LAT // GRID_W, LANES), lambda i: (0, 0, 0)),
            pl.BlockSpec((2, GRID_W, LANES), lambda i: (0, 0, 0)),
        ],
        out_specs=[
            pl.BlockSpec((TN, D), tail_blk),
            pl.BlockSpec((TN, dq), proj_blk),
            pl.BlockSpec((TN, dkv), proj_blk),
            pl.BlockSpec((dkv, TN), lambda i: (0, jnp.maximum(i - 1, 0))),
            pl.BlockSpec((TN, dq), proj_blk),
        ],
        out_shape=[
            jax.ShapeDtypeStruct((R_ALL, D), F32),
            jax.ShapeDtypeStruct((R_ALL, dq), BF16),
            jax.ShapeDtypeStruct((R_ALL, dkv), BF16),
            jax.ShapeDtypeStruct((dkv, R_ALL), BF16),
            jax.ShapeDtypeStruct((R_ALL, dq), BF16),
        ],
        scratch_shapes=[pltpu.VMEM((TN, D), F32), pltpu.VMEM((D, ODD_IN), BF16),
                        pltpu.VMEM((S5_W + SG_W, D), BF16), pltpu.VMEM((S5_QT, TN, LANES), F32)],
        compiler_params=pltpu.CompilerParams(vmem_limit_bytes=VMEM_LIMIT),
        name="mid",
    )(x2, ctx2, mod0, mod1, ys_lat, ys_ctx, rest, glu_w, glu_b, sg_w, sg_bias, w_out,
      ln_g, ln_b, w_in1, row_tab, col_tab)


N_QBLK = N_LAT // ATT_BLK
GRP = N_HEADS // N_KV
ATT_SUB = 4
ATT_ROWS = ATT_SUB * ATT_BLK
N_QSTEP = N_QBLK // ATT_SUB
N_ATT_STEPS = B * N_QSTEP
assert ATT_ROWS == TN
assert WINDOW == ATT_BLK
ONES_ROWS = 16


def _attn_kernel(sink_ref, q_ref, kp_ref, kc_ref, kn_ref, kx_ref, vp_ref, vc_ref, vn_ref, vx_ref,
                 g_ref, z_ref, mod_ref, wout_ref, lng_ref, lnb_ref, out_ref, o_scr, wout_scr):
    step = pl.program_id(0)

    @pl.when(step == 0)
    def _():
        o_scr[...] = jnp.zeros((ATT_ROWS, N_HEADS * HD), BF16)
        wout_scr[...] = wout_ref[...].astype(BF16)

    t = jnp.maximum(step - 1, 0)
    gate = mod_ref[pl.ds(_block_mod_row(t), 1), 2 * D:3 * D]
    mix = (o_scr[...].astype(F32) * _silu(g_ref[...].astype(F32))).astype(BF16)
    y = jnp.dot(mix, wout_scr[...], preferred_element_type=F32) * gate
    out_ref[...] = _layer_norm(ALPHA * z_ref[...] + y, lng_ref[...], lnb_ref[...])

    i = jnp.minimum(step, N_ATT_STEPS - 1) % N_QSTEP
    n_win = 3 * ATT_BLK
    n_keys = n_win + N_CTX
    nq = GRP * ATT_BLK
    k_win = jnp.concatenate([kp_ref[...], kc_ref[...], kn_ref[...]], axis=0)
    vt_win = jnp.concatenate([vp_ref[...], vc_ref[...], vn_ref[...]], axis=1)
    k_ctx = kx_ref[...]
    vt_ctx = vx_ref[...]

    kpos = lax.broadcasted_iota(jnp.int32, (ATT_BLK, ATT_BLK), 0)
    qpos = lax.broadcasted_iota(jnp.int32, (ATT_BLK, ATT_BLK), 1)
    tile_q = lambda a: jnp.concatenate([a] * GRP, axis=1)
    qgrp = lax.broadcasted_iota(jnp.int32, (1, nq), 1) // ATT_BLK
    ones = jnp.ones((ONES_ROWS, n_keys), BF16)
    q = q_ref[...]

    units = [(u, h) for u in range(ATT_SUB) for h in range(N_KV)]
    k_all, vt_all, bias = [], [], []
    for u in range(ATT_SUB):
        k_all.append(jnp.concatenate([k_win[u * ATT_BLK:u * ATT_BLK + n_win], k_ctx], axis=0))
        vt_all.append(jnp.concatenate([vt_win[:, u * ATT_BLK:u * ATT_BLK + n_win], vt_ctx], axis=1))
        has_prev = (i > 0) if u == 0 else True
        has_next = (i < N_QSTEP - 1) if u == ATT_SUB - 1 else True
        bias.append((tile_q(jnp.where((kpos >= qpos) & has_prev, 0.0, NEG_INF)),
                     tile_q(jnp.where((kpos <= qpos) & has_next, 0.0, NEG_INF))))
    scores = []
    for u, h in units:
        kh = k_all[u][:, h * HD:(h + 1) * HD]
        qh = jnp.concatenate(
            [q[u * ATT_BLK:(u + 1) * ATT_BLK, (h * GRP + g) * HD:(h * GRP + g + 1) * HD]
             for g in range(GRP)], axis=0)
        scores.append(lax.dot_general(kh, qh, (((1,), (1,)), ((), ())),
                                      preferred_element_type=F32))
    probs = []
    for (u, h), s in zip(units, scores):
        s = jnp.concatenate([s[:ATT_BLK] + bias[u][0], s[ATT_BLK:2 * ATT_BLK],
                             s[2 * ATT_BLK:n_win] + bias[u][1], s[n_win:]], axis=0)
        sink = jnp.zeros((1, nq), F32)
        for g in range(GRP):
            sink = jnp.where(qgrp == g, sink_ref[h * GRP + g] * LOG2E, sink)
        m = jnp.maximum(jnp.max(s, axis=0, keepdims=True), sink)
        probs.append((jnp.exp2(s - m).astype(BF16), jnp.exp2(sink - m)))
    outs = [[] for _ in range(ATT_SUB)]
    for (u, h), (p, p_sink) in zip(units, probs):
        vt1 = jnp.concatenate([vt_all[u][h * HD:(h + 1) * HD], ones], axis=0)
        ov = jnp.dot(vt1, p, preferred_element_type=F32)
        o_t = ov[:HD] / (ov[HD:HD + 1] + p_sink)
        outs[u].extend(o_t[:, g * ATT_BLK:(g + 1) * ATT_BLK] for g in range(GRP))
    for u in range(ATT_SUB):
        o_scr[u * ATT_BLK:(u + 1) * ATT_BLK, :] = jnp.transpose(
            jnp.concatenate(outs[u], axis=0)).astype(BF16)


def _attention_tail(sink, q, k, vt, g, z1, mod, w_out, ln_g, ln_b):
    dq = N_HEADS * HD
    dkv = N_KV * HD
    ctx_blk0 = R_LAT // N_CTX
    cur = lambda s: jnp.minimum(s, N_ATT_STEPS - 1)
    batch = lambda s: cur(s) // N_QSTEP
    qstep = lambda s: cur(s) % N_QSTEP
    prev_blk = lambda s: batch(s) * N_QBLK + jnp.maximum(ATT_SUB * qstep(s) - 1, 0)
    next_blk = lambda s: batch(s) * N_QBLK + jnp.minimum(ATT_SUB * (qstep(s) + 1), N_QBLK - 1)
    tail = lambda s: jnp.maximum(s - 1, 0)
    once = dict(pipeline_mode=pl.Buffered(1))
    row = lambda n: pl.BlockSpec((1, n), lambda s, sk: (0, 0))
    return pl.pallas_call(
        _attn_kernel,
        grid_spec=pltpu.PrefetchScalarGridSpec(
            num_scalar_prefetch=1,
            grid=(N_ATT_STEPS + 1,),
            in_specs=[
                pl.BlockSpec((ATT_ROWS, dq), lambda s, sk: (cur(s), 0)),
                pl.BlockSpec((ATT_BLK, dkv), lambda s, sk: (prev_blk(s), 0)),
                pl.BlockSpec((ATT_ROWS, dkv), lambda s, sk: (cur(s), 0)),
                pl.BlockSpec((ATT_BLK, dkv), lambda s, sk: (next_blk(s), 0)),
                pl.BlockSpec((N_CTX, dkv), lambda s, sk: (ctx_blk0 + batch(s), 0)),
                pl.BlockSpec((dkv, ATT_BLK), lambda s, sk: (0, prev_blk(s))),
                pl.BlockSpec((dkv, ATT_ROWS), lambda s, sk: (0, cur(s))),
                pl.BlockSpec((dkv, ATT_BLK), lambda s, sk: (0, next_blk(s))),
                pl.BlockSpec((dkv, N_CTX), lambda s, sk: (0, ctx_blk0 + batch(s))),
                pl.BlockSpec((ATT_ROWS, dq), lambda s, sk: (tail(s), 0)),
                pl.BlockSpec((ATT_ROWS, D), lambda s, sk: (tail(s), 0)),
                pl.BlockSpec((8, 3 * D), lambda s, sk: (0, 0)),
                pl.BlockSpec((dq, D), lambda s, sk: (0, 0), **once),
                row(D), row(D),
            ],
            out_specs=pl.BlockSpec((ATT_ROWS, D), lambda s, sk: (tail(s), 0)),
            scratch_shapes=[pltpu.VMEM((ATT_ROWS, dq), BF16), pltpu.VMEM((dq, D), BF16)],
        ),
        out_shape=jax.ShapeDtypeStruct((R_LAT, D), F32),
        compiler_params=pltpu.CompilerParams(vmem_limit_bytes=VMEM_LIMIT),
        name="attention",
    )(sink, q, k, k, k, k, vt, vt, vt, vt, g, z1, mod, w_out, ln_g, ln_b)


def kernel(x, c, ctx, c_ctx, mod_w, mod_b, ln_g, ln_b, e_w_in, e_w_out, s5_lam_re, s5_lam_im,
           s5_log_dt, s5_b_re, s5_b_im, s5_c_re, s5_c_im, s5_d, glu_w, glu_b, sg_ln_g, sg_ln_b,
           sg_w, sg_b, o_w_in, o_w_out, o_sink):
    x2 = x.reshape(R_LAT, D)
    ctx2 = ctx.reshape(R_CTX, D)
    mod = _modulation(c, c_ctx, mod_w, mod_b)

    vp, rest = _in0(x2, ctx2, mod[0], e_w_in[0], sg_ln_g[0].reshape(1, SG_W),
                    sg_ln_b[0].reshape(1, SG_W))
    t_mat, e_mat, ft_mat, al_re, al_im = _s5_prep(
        s5_lam_re[0], s5_lam_im[0], s5_log_dt[0], s5_b_re[0], s5_b_im[0], s5_c_re[0], s5_c_im[0],
        s5_d[0])
    ys_lat, ys_ctx = _s5_out(vp, al_re, al_im, e_mat, t_mat, ft_mat)
    sg_bias = jnp.repeat(jnp.transpose(sg_b[0]), SG_W // SG_HEADS, axis=1)
    sg_w2 = jnp.transpose(sg_w[0].reshape(SG_HEADS // 2, 2, SG_CHUNK, SG_CHUNK), (0, 2, 1, 3))
    sg_w2 = sg_w2.reshape(SG_HEADS // 2, SG_CHUNK, 2 * SG_CHUNK).astype(BF16)
    row_tab, col_tab = _rope_tables()
    z1, q, k, vt, g = _mid(
        x2, ctx2, mod[0], mod[1], ys_lat, ys_ctx, rest, glu_w[0].astype(BF16),
        glu_b[0].reshape(1, S5_W), sg_w2, sg_bias, e_w_out[0], ln_g[0].reshape(1, D),
        ln_b[0].reshape(1, D), o_w_in[0], row_tab, col_tab)

    out = _attention_tail(o_sink[0], q, k, vt, g, z1, mod[1], o_w_out[0],
                          ln_g[1].reshape(1, D), ln_b[1].reshape(1, D))
    return out.reshape(B, N_LAT, D)
```

```python
import math

import jax
import jax.numpy as jnp
from jax import lax
from jax.experimental import pallas as pl
from jax.experimental.pallas import tpu as pltpu

F32 = jnp.float32
BF16 = jnp.bfloat16

D = 1024
B = 2
N_LAT = 8192
N_CTX = 256
DEPTH = 2
GRID_W = 64
S5_W = 512
S5_GC = 16
S5_G = 32
S5_P = 64
SG_W = 512
SG_HEADS = 8
SG_CHUNK = 128
N_HEADS = 16
N_KV = 4
HD = 64
WINDOW = 128
ATT_BLK = 128
ROPE_BASE = 10000.0
NEG_INF = -1e30
LN_EPS = 1e-5
ALPHA = (2 * DEPTH) ** 0.25
LOG2E = math.log2(math.e)
EVEN_IN = 2 * S5_W + 3 * SG_W
ODD_IN = 2 * N_HEADS * HD + 2 * N_KV * HD

LANES = 128
VMEM_LIMIT = 56 * 1024 * 1024

R_LAT = B * N_LAT
R_CTX = B * N_CTX
R_ALL = R_LAT + R_CTX
TN = 512
N_LAT_BLK = R_LAT // TN
N_BLK = R_ALL // TN
LAT_BLK_PER_BATCH = N_LAT // TN
CTX_MOD_ROW = B

S5_CHUNK = 8
S5_QT = S5_W // LANES
S5_GPT = LANES // S5_GC
S5_ST = S5_GPT * S5_P
S5_PAIRS = S5_GPT // 2
PAIR_BLK = 2 * S5_GC
LAT_CHUNKS = N_LAT // S5_CHUNK
CTX_CHUNKS = N_CTX // S5_CHUNK
BATCH_CHUNKS = CTX_CHUNKS + LAT_CHUNKS
N_SEG = 16
SEG_CHUNKS = BATCH_CHUNKS // N_SEG
BWD_RESET_STEP = LAT_CHUNKS - (N_SEG - 1) * SEG_CHUNKS


def _sigmoid(x):
    return 1.0 / (1.0 + jnp.exp(-x))


def _silu(x):
    return x * _sigmoid(x)


def _gelu_tanh(x):
    return 0.5 * x * (1.0 + jnp.tanh(math.sqrt(2.0 / math.pi) * (x + 0.044715 * (x * x * x))))


def _layer_norm(x, g, b):
    mu = jnp.mean(x, axis=-1, keepdims=True)
    xc = x - mu
    var = jnp.mean(xc * xc, axis=-1, keepdims=True)
    return xc * lax.rsqrt(var + LN_EPS) * g + b


def _split_bf16(x):
    hi = x.astype(BF16)
    return hi, (x - hi.astype(F32)).astype(BF16)


def _dot_split(x, y_split, dims):
    x_hi, x_lo = _split_bf16(x)
    y_hi, y_lo = y_split
    dot = lambda a, b: lax.dot_general(a, b, dims, preferred_element_type=F32)
    return dot(x_hi, y_hi) + (dot(x_hi, y_lo) + dot(x_lo, y_hi))


def _block_mod_row(i):
    return jnp.where(i < N_LAT_BLK, i // LAT_BLK_PER_BATCH, CTX_MOD_ROW)


def _mod_kernel(cv_ref, w_ref, b_ref, o_ref):
    s = _silu(cv_ref[...])
    o_ref[0] = _dot_split(s, _split_bf16(w_ref[0]), (((1,), (0,)), ((), ()))) + b_ref[0]


def _modulation(c, c_ctx, mod_w, mod_b):
    cv = jnp.concatenate([c, c_ctx[None], jnp.zeros((8 - B - 1, D), F32)], axis=0)
    tn = 1024
    return pl.pallas_call(
        _mod_kernel,
        grid=(DEPTH, 3 * D // tn),
        in_specs=[
            pl.BlockSpec((8, D), lambda l, j: (0, 0)),
            pl.BlockSpec((1, D, tn), lambda l, j: (l, 0, j)),
            pl.BlockSpec((1, 1, tn), lambda l, j: (l, 0, j)),
        ],
        out_specs=pl.BlockSpec((1, 8, tn), lambda l, j: (l, 0, j)),
        out_shape=jax.ShapeDtypeStruct((DEPTH, 8, 3 * D), F32),
        compiler_params=pltpu.CompilerParams(vmem_limit_bytes=VMEM_LIMIT),
        name="modulation",
    )(cv, mod_w, mod_b.reshape(DEPTH, 1, 3 * D))


def _block_transpose4(tiles):
    tiles = list(tiles)
    blk = lax.broadcasted_iota(jnp.int32, tiles[0].shape, 1) // PAIR_BLK
    for dist in (2, 1):
        keep = (blk & dist) == 0
        for i in range(len(tiles)):
            if i & dist:
                continue
            lo, hi = tiles[i], tiles[i + dist]
            tiles[i] = jnp.where(keep, lo, pltpu.roll(hi, dist * PAIR_BLK, axis=1))
            tiles[i + dist] = jnp.where(keep, pltpu.roll(lo, LANES - dist * PAIR_BLK, axis=1), hi)
    return tiles


def _in0_kernel(x_ref, ctx_ref, mod_ref, w_ref, vp_ref, rest_ref, xa_scr, wbf_scr):
    L = S5_CHUNK
    i = pl.program_id(0)

    @pl.when(i == 0)
    def _():
        wbf_scr[...] = w_ref[...].astype(BF16)

    z = jnp.where(i < N_LAT_BLK, x_ref[...], ctx_ref[...])
    m = mod_ref[pl.ds(_block_mod_row(i), 1), :]
    h = z * (1.0 + m[:, D:2 * D]) + m[:, :D]
    p = jnp.dot(h.astype(BF16), wbf_scr[...], preferred_element_type=F32)
    rest_ref[...] = p[:, S5_W:].astype(BF16)
    for q in range(S5_QT):
        xa_scr[q] = p[:, q * LANES:(q + 1) * LANES]
        tiles = [xa_scr[q, pl.ds(t, TN // L, stride=L), :] for t in range(L)]
        halves = [_block_transpose4(tiles[k * S5_PAIRS:(k + 1) * S5_PAIRS])
                  for k in range(L // S5_PAIRS)]
        for pr in range(S5_PAIRS):
            vp_ref[q * S5_PAIRS + pr] = jnp.concatenate(
                [half[pr] for half in halves], axis=-1).astype(BF16)


def _in0(x2, ctx2, mod, w_in):
    pair_w = S5_CHUNK * PAIR_BLK
    return pl.pallas_call(
        _in0_kernel,
        grid=(N_BLK,),
        in_specs=[
            pl.BlockSpec((TN, D), lambda i: (jnp.minimum(i, N_LAT_BLK - 1), 0)),
            pl.BlockSpec((R_CTX, D), lambda i: (0, 0)),
            pl.BlockSpec((8, 3 * D), lambda i: (0, 0)),
            pl.BlockSpec((D, EVEN_IN), lambda i: (0, 0)),
        ],
        out_specs=[
            pl.BlockSpec((S5_QT * S5_PAIRS, TN // S5_CHUNK, pair_w), lambda i: (0, i, 0)),
            pl.BlockSpec((TN, EVEN_IN - S5_W), lambda i: (i, 0)),
        ],
        out_shape=[
            jax.ShapeDtypeStruct((S5_QT * S5_PAIRS, R_ALL // S5_CHUNK, pair_w), BF16),
            jax.ShapeDtypeStruct((R_ALL, EVEN_IN - S5_W), BF16),
        ],
        scratch_shapes=[pltpu.VMEM((S5_QT, TN, LANES), F32), pltpu.VMEM((D, EVEN_IN), BF16)],
        compiler_params=pltpu.CompilerParams(vmem_limit_bytes=VMEM_LIMIT),
        name="in0",
    )(x2, ctx2, mod, w_in)


def _s5_prep_kernel(rows_ref, bre_ref, bim_ref, cre_ref, cim_ref, dsk_ref, t_ref, e_ref, ft_ref,
                    alr_ref, ali_ref):
    L = S5_CHUNK
    tile = (LANES, S5_ST)
    same_group = (lax.broadcasted_iota(jnp.int32, tile, 0) // S5_GC
                  == lax.broadcasted_iota(jnp.int32, tile, 1) // S5_P)
    nt_dims = (((1,), (1,)), ((), ()))
    nn_dims = (((1,), (0,)), ((), ()))
    pick_ch = (lax.broadcasted_iota(jnp.int32, (LANES, S5_GC), 0) % S5_GC
               == lax.broadcasted_iota(jnp.int32, (LANES, S5_GC), 1)).astype(BF16)
    rep_st = (lax.broadcasted_iota(jnp.int32, (S5_P, S5_ST), 0)
              == lax.broadcasted_iota(jnp.int32, (S5_P, S5_ST), 1) % S5_P).astype(BF16)

    def spread_b(b):
        hi, lo = _split_bf16(b)
        dot = lambda v: lax.dot_general(pick_ch, v, nt_dims, preferred_element_type=F32)
        return jnp.where(same_group, dot(hi) + dot(lo), 0.0)

    def spread_c(c):
        hi, lo = _split_bf16(c)
        dot = lambda v: lax.dot_general(v, rep_st, nn_dims, preferred_element_type=F32)
        return jnp.where(same_group, dot(hi) + dot(lo), 0.0)

    pair_rows = lambda pr: slice(pr * PAIR_BLK, (pr + 1) * PAIR_BLK)
    pair_lanes = lambda pr: slice(pr * LANES, (pr + 1) * LANES)
    lag_strip = []
    for d in range(2):
        lam_re = rows_ref[d, 0]
        lam_im = rows_ref[d, 1]
        dt = jnp.exp(rows_ref[d, 2])
        pw = []
        for k in range(L + 1):
            mag = jnp.exp(lam_re * dt * float(k))
            ang = lam_im * dt * float(k)
            pw.append((mag * jnp.cos(ang), mag * jnp.sin(ang)))
        alr_ref[d] = pw[L][0]
        ali_ref[d] = pw[L][1]
        den = lam_re * lam_re + lam_im * lam_im
        nr = pw[1][0] - 1.0
        ni = pw[1][1]
        f_re = (nr * lam_re + ni * lam_im) / den
        f_im = (ni * lam_re - nr * lam_im) / den
        b_re = spread_b(bre_ref[d])
        b_im = spread_b(bim_ref[d])
        bb_re = f_re * b_re - f_im * b_im
        bb_im = f_re * b_im + f_im * b_re
        c_re = spread_c(cre_ref[d])
        c_im = spread_c(cim_ref[d])
        col_re = slice(2 * d * LANES, (2 * d + 1) * LANES)
        col_im = slice((2 * d + 1) * LANES, (2 * d + 2) * LANES)
        strips = []
        for pr in range(S5_PAIRS):
            blk = lambda a: a[pair_rows(pr), pair_lanes(pr)]
            p_re = [blk(jnp.broadcast_to(pw[k][0], tile)) for k in range(L + 1)]
            p_im = [blk(jnp.broadcast_to(pw[k][1], tile)) for k in range(L + 1)]
            bbr, bbi, cr, ci = blk(bb_re), blk(bb_im), blk(c_re), blk(c_im)
            for k in range(L):
                s = (L - 1 - k) if d == 0 else k
                rows = slice(s * PAIR_BLK, (s + 1) * PAIR_BLK)
                e_ref[0, pr, rows, col_re] = (p_re[k] * bbr - p_im[k] * bbi).astype(BF16)
                e_ref[0, pr, rows, col_im] = (p_re[k] * bbi + p_im[k] * bbr).astype(BF16)
            cw = [(cr * p_re[k] - ci * p_im[k], -(cr * p_im[k] + ci * p_re[k]))
                  for k in range(L + 1)]
            for t in range(L):
                k = (t + 1) if d == 0 else (L - t)
                rows = slice(t * PAIR_BLK, (t + 1) * PAIR_BLK)
                ft_ref[0, pr, rows, col_re] = cw[k][0].astype(BF16)
                ft_ref[0, pr, rows, col_im] = cw[k][1].astype(BF16)
            lags = range(L) if d == 0 else range(L - 1, -1, -1)
            c_stack = jnp.concatenate(
                [jnp.concatenate([cw[k][0], cw[k][1]], axis=-1) for k in lags], axis=0)
            strips.append(_dot_split(jnp.concatenate([bbr, bbi], axis=-1),
                                     _split_bf16(c_stack), nt_dims))
        lag_strip.append(strips)

    strip_w = 2 * L * PAIR_BLK
    mid = (L - 1) * PAIR_BLK
    widen = lambda a: jnp.concatenate([a, jnp.zeros((a.shape[0], strip_w - a.shape[1]), F32)], axis=-1)
    row = lax.broadcasted_iota(jnp.int32, (PAIR_BLK, strip_w), 0)
    lane = lax.broadcasted_iota(jnp.int32, (PAIR_BLK, strip_w), 1)
    d_row = widen(dsk_ref[0])
    for pr in range(S5_PAIRS):
        strip = (widen(lag_strip[1][pr])
                 + pltpu.roll(widen(lag_strip[0][pr]), mid, axis=1)
                 + jnp.where(lane == row + mid,
                             pltpu.roll(d_row, mid - pr * PAIR_BLK, axis=1), 0.0))
        for s in range(L):
            shift = (strip_w - (L - 1 - s) * PAIR_BLK) % strip_w
            blk = pltpu.roll(strip, shift, axis=1) if shift else strip
            t_ref[0, pr, s * PAIR_BLK:(s + 1) * PAIR_BLK, :] = blk[:, :L * PAIR_BLK].astype(BF16)


def _s5_prep(lam_re, lam_im, log_dt, b_re, b_im, c_re, c_im, d_skip):
    L = S5_CHUNK
    n_st = S5_G * S5_P
    pair_w = L * PAIR_BLK
    rows = jnp.stack([lam_re.reshape(2, 1, n_st), lam_im.reshape(2, 1, n_st),
                      jnp.repeat(log_dt, S5_P, axis=1).reshape(2, 1, n_st)], axis=1)
    b_spec = pl.BlockSpec((2, S5_ST, S5_GC), lambda q: (0, q, 0))
    c_spec = pl.BlockSpec((2, LANES, S5_P), lambda q: (0, q, 0))
    return pl.pallas_call(
        _s5_prep_kernel,
        grid=(S5_QT,),
        in_specs=[
            pl.BlockSpec((2, 3, 1, S5_ST), lambda q: (0, 0, 0, q)),
            b_spec, b_spec, c_spec, c_spec,
            pl.BlockSpec((1, 1, LANES), lambda q: (q, 0, 0)),
        ],
        out_specs=[
            pl.BlockSpec((1, S5_PAIRS, pair_w, pair_w), lambda q: (q, 0, 0, 0)),
            pl.BlockSpec((1, S5_PAIRS, pair_w, 4 * LANES), lambda q: (q, 0, 0, 0)),
            pl.BlockSpec((1, S5_PAIRS, pair_w, 4 * LANES), lambda q: (q, 0, 0, 0)),
            pl.BlockSpec((2, 1, S5_ST), lambda q: (q, 0, 0)),
            pl.BlockSpec((2, 1, S5_ST), lambda q: (q, 0, 0)),
        ],
        out_shape=[
            jax.ShapeDtypeStruct((S5_QT, S5_PAIRS, pair_w, pair_w), BF16),
            jax.ShapeDtypeStruct((S5_QT, S5_PAIRS, pair_w, 4 * LANES), BF16),
            jax.ShapeDtypeStruct((S5_QT, S5_PAIRS, pair_w, 4 * LANES), BF16),
            jax.ShapeDtypeStruct((2 * S5_QT, 1, S5_ST), F32),
            jax.ShapeDtypeStruct((2 * S5_QT, 1, S5_ST), F32),
        ],
        compiler_params=pltpu.CompilerParams(vmem_limit_bytes=VMEM_LIMIT),
        name="s5_prep",
    )(rows, b_re.reshape(2, n_st, S5_GC), b_im.reshape(2, n_st, S5_GC),
      c_re.reshape(2, S5_W, S5_P), c_im.reshape(2, S5_W, S5_P), d_skip.reshape(S5_QT, 1, LANES))


N_SLAB = S5_ST // LANES


def _cmul(ar, ai, br, bi):
    return ar * br - ai * bi, ar * bi + ai * br


def _cpow(ar, ai, n):
    res = None
    while n:
        if n & 1:
            res = (ar, ai) if res is None else _cmul(res[0], res[1], ar, ai)
        n >>= 1
        if n:
            ar, ai = _cmul(ar, ai, ar, ai)
    return res


def _scan_boundary_states(loc_ref, out_ref, ar_ref, ai_ref, backward):
    a_row = [(ar_ref[0][:, k * LANES:(k + 1) * LANES], ai_ref[0][:, k * LANES:(k + 1) * LANES])
             for k in range(N_SLAB)]
    a_tile = [(jnp.broadcast_to(r, (N_SEG, LANES)), jnp.broadcast_to(i, (N_SEG, LANES)))
              for r, i in a_row]

    def sweep(lo, hi, carry, store):
        def step(i, carry):
            off = (SEG_CHUNKS - 1 - i) if backward else i
            rows = pl.ds(off, N_SEG, stride=SEG_CHUNKS)
            new = []
            for k, (cr, ci) in enumerate(carry):
                xr = loc_ref[k, rows, :]
                xi = loc_ref[N_SLAB + k, rows, :]
                if store:
                    out_ref[k, rows, :] = cr
                    out_ref[N_SLAB + k, rows, :] = ci
                nr, ni = _cmul(a_tile[k][0], a_tile[k][1], cr, ci)
                new.append((nr + xr, ni + xi))
            return tuple(new)
        return lax.fori_loop(lo, hi, step, carry)

    def full(carry, store):
        if not backward:
            return sweep(0, SEG_CHUNKS, carry, store)
        carry = sweep(0, BWD_RESET_STEP, carry, store)
        first = lax.broadcasted_iota(jnp.int32, (N_SEG, LANES), 0) == 0
        carry = tuple((jnp.where(first, 0.0, cr), jnp.where(first, 0.0, ci)) for cr, ci in carry)
        return sweep(BWD_RESET_STEP, SEG_CHUNKS, carry, store)

    zero_tile = jnp.zeros((N_SEG, LANES), F32)
    ends = full(tuple((zero_tile, zero_tile) for _ in range(N_SLAB)), False)

    starts = []
    for k, (er, ei) in enumerate(ends):
        pr, pi = _cpow(a_row[k][0], a_row[k][1], SEG_CHUNKS)
        if backward:
            order = range(N_SEG - 1, -1, -1)
            g = (er[0:1], ei[0:1])
        else:
            order = range(N_SEG)
            g = (jnp.zeros((1, LANES), F32), jnp.zeros((1, LANES), F32))
        rows_r = [None] * N_SEG
        rows_i = [None] * N_SEG
        for j in order:
            rows_r[j], rows_i[j] = g
            nr, ni = _cmul(pr, pi, g[0], g[1])
            g = (nr + er[j:j + 1], ni + ei[j:j + 1])
        starts.append((jnp.concatenate(rows_r, axis=0), jnp.concatenate(rows_i, axis=0)))
    full(tuple(starts), True)


def _s5_out_kernel(vl_ref, vc_ref, ar_ref, ai_ref, e_ref, t_ref, ft_ref, yl_ref, yc_ref, st_ref):
    n_dir = 2 * N_SLAB
    rows = [jnp.concatenate([vc_ref[pr], vl_ref[pr]], axis=0) for pr in range(S5_PAIRS)]
    for pr in range(S5_PAIRS):
        st = jnp.dot(rows[pr], e_ref[0, pr], preferred_element_type=F32)
        for comp in range(4):
            st_ref[comp * N_SLAB + pr] = st[:, comp * LANES:(comp + 1) * LANES]
    for d in range(2):
        st_d = st_ref.at[pl.ds(d * n_dir, n_dir)]
        _scan_boundary_states(st_d, st_d, ar_ref.at[pl.ds(d, 1)], ai_ref.at[pl.ds(d, 1)], d == 1)

    for pr in range(S5_PAIRS):
        st = jnp.concatenate([st_ref[comp * N_SLAB + pr] for comp in range(4)],
                             axis=-1).astype(BF16)
        y = (jnp.dot(rows[pr], t_ref[0, pr], preferred_element_type=F32)
             + lax.dot_general(st, ft_ref[0, pr], (((1,), (1,)), ((), ())),
                               preferred_element_type=F32))
        yc_ref[pr] = y[:CTX_CHUNKS]
        yl_ref[pr] = y[CTX_CHUNKS:]


def _s5_out(vp, al_re, al_im, e_mat, t_mat, ft_mat):
    pair_w = S5_CHUNK * PAIR_BLK
    ctx_blk0 = (R_LAT // S5_CHUNK) // CTX_CHUNKS
    return pl.pallas_call(
        _s5_out_kernel,
        grid=(S5_QT, B),
        in_specs=[
            pl.BlockSpec((S5_PAIRS, LAT_CHUNKS, pair_w), lambda q, b: (q, b, 0)),
            pl.BlockSpec((S5_PAIRS, CTX_CHUNKS, pair_w), lambda q, b: (q, ctx_blk0 + b, 0)),
            pl.BlockSpec((2, 1, S5_ST), lambda q, b: (q, 0, 0)),
            pl.BlockSpec((2, 1, S5_ST), lambda q, b: (q, 0, 0)),
            pl.BlockSpec((1, S5_PAIRS, pair_w, 4 * LANES), lambda q, b: (q, 0, 0, 0)),
            pl.BlockSpec((1, S5_PAIRS, pair_w, pair_w), lambda q, b: (q, 0, 0, 0)),
            pl.BlockSpec((1, S5_PAIRS, pair_w, 4 * LANES), lambda q, b: (q, 0, 0, 0)),
        ],
        out_specs=[
            pl.BlockSpec((S5_PAIRS, LAT_CHUNKS, pair_w), lambda q, b: (q, b, 0)),
            pl.BlockSpec((S5_PAIRS, CTX_CHUNKS, pair_w), lambda q, b: (q, b, 0)),
        ],
        out_shape=[
            jax.ShapeDtypeStruct((S5_QT * S5_PAIRS, B * LAT_CHUNKS, pair_w), F32),
            jax.ShapeDtypeStruct((S5_QT * S5_PAIRS, B * CTX_CHUNKS, pair_w), F32),
        ],
        scratch_shapes=[pltpu.VMEM((4 * N_SLAB, BATCH_CHUNKS, LANES), F32)],
        compiler_params=pltpu.CompilerParams(vmem_limit_bytes=VMEM_LIMIT),
        name="s5_out",
    )(vp, vp, al_re, al_im, e_mat, t_mat, ft_mat)


def _tail0_stages(z, ys, rest_ref, gate, gluw_ref, glub_ref, sg_g_ref, sg_b_ref, sgw_ref,
                  sgbias_ref, wout_ref, lng_ref, lnb_ref):
    ga = rest_ref[:, 0:S5_W].astype(F32)
    u = rest_ref[:, S5_W:2 * S5_W].astype(F32)
    v = rest_ref[:, 2 * S5_W:3 * S5_W].astype(F32)
    gb = rest_ref[:, 3 * S5_W:4 * S5_W].astype(F32)

    ya = _gelu_tanh(ys)
    glu = jnp.dot(ya.astype(BF16), gluw_ref[...], preferred_element_type=F32) + glub_ref[...]
    yield None
    ya = ya * _sigmoid(glu) * _silu(ga)
    yield None

    vn = _layer_norm(v, sg_g_ref[...], sg_b_ref[...]).astype(BF16)
    lane = lax.broadcasted_iota(jnp.int32, (SG_CHUNK, LANES), 1)
    first_head = lane < (SG_W // SG_HEADS)
    chunks = []
    for c in range(TN // SG_CHUNK):
        tiles = []
        for j in range(SG_W // LANES):
            vt = vn[c * SG_CHUNK:(c + 1) * SG_CHUNK, j * LANES:(j + 1) * LANES]
            zero = jnp.zeros_like(vt)
            v2 = jnp.concatenate([jnp.where(first_head, vt, zero), jnp.where(first_head, zero, vt)],
                                 axis=0)
            tiles.append(jnp.dot(sgw_ref[j], v2, preferred_element_type=F32))
        chunks.append(jnp.concatenate(tiles, axis=-1) + sgbias_ref[...])
    s = jnp.concatenate(chunks, axis=0)
    yb = u * s * _silu(gb)
    yield None

    mix = jnp.concatenate([ya, yb], axis=-1).astype(BF16)
    y = jnp.dot(mix, wout_ref[...], preferred_element_type=F32) * gate
    yield None
    yield _layer_norm(ALPHA * z + y, lng_ref[...], lnb_ref[...])


def _rope_tables():
    nf = HD // 4
    n_rows = N_LAT // GRID_W
    lane = jnp.arange(LANES)
    inv = ROPE_BASE ** (-(lane % nf).astype(F32) / nf)
    by_row = ((lane % HD) // (HD // 2) == 0)[None, :]
    sign = jnp.where((lane % (HD // 2)) < nf, -1.0, 1.0)[None, :]
    row_ang = jnp.arange(n_rows, dtype=F32)[:, None] * inv[None, :]
    col_ang = jnp.arange(GRID_W, dtype=F32)[:, None] * inv[None, :]
    zero = jnp.zeros((), F32)
    row_tab = jnp.stack([jnp.where(by_row, jnp.cos(row_ang), zero),
                         jnp.where(by_row, sign * jnp.sin(row_ang), zero)])
    col_tab = jnp.stack([jnp.where(by_row, zero, jnp.cos(col_ang)),
                         jnp.where(by_row, zero, sign * jnp.sin(col_ang))])
    return row_tab, col_tab


def _rope_tile(x, cos, sin, first_half):
    nf = HD // 4
    partner = jnp.where(first_half, pltpu.roll(x, LANES - nf, axis=1), pltpu.roll(x, nf, axis=1))
    return x * cos + partner * sin


def _mid_kernel(x_ref, ctx_ref, mod0_ref, mod1_ref, ysl_ref, ysc_ref, rest_ref, gluw_ref, glub_ref,
                sg_g_ref, sg_b_ref, sgw_ref, sgbias_ref, wout_ref, lng_ref, lnb_ref, w_ref,
                rtab_ref, ctab_ref, z1_ref, q_ref, k_ref, vt_ref, g_ref, zprev_ref, wbf_scr,
                wout_scr, ys_scr):
    i = pl.program_id(0)
    dq = N_HEADS * HD
    dkv = N_KV * HD

    @pl.when(i == 0)
    def _():
        zprev_ref[...] = jnp.zeros((TN, D), F32)
        wbf_scr[...] = w_ref[...].astype(BF16)
        wout_scr[...] = wout_ref[...].astype(BF16)

    j = jnp.maximum(i - 1, 0)
    m1 = mod1_ref[pl.ds(_block_mod_row(j), 1), :]
    h = (zprev_ref[...] * (1.0 + m1[:, D:2 * D]) + m1[:, :D]).astype(BF16)
    grid_row0 = (j % LAT_BLK_PER_BATCH) * (TN // GRID_W)
    table = lambda t: jnp.concatenate(
        [rtab_ref[t, pl.ds(grid_row0 + r, 1), :] + ctab_ref[t] for r in range(TN // GRID_W)],
        axis=0)
    cos = jnp.where(j < N_LAT_BLK, table(0), 1.0)
    sin = jnp.where(j < N_LAT_BLK, table(1), 0.0)
    lane = lax.broadcasted_iota(jnp.int32, (TN, LANES), 1)
    first_half = (lane % (HD // 2)) < (HD // 4)
    scale = HD ** -0.5 * LOG2E

    def project(c0, c1):
        return jnp.dot(h, wbf_scr[:, c0:c1], preferred_element_type=F32)

    def roped_tiles(p, mult):
        for c in range(p.shape[1] // LANES):
            r = _rope_tile(p[:, c * LANES:(c + 1) * LANES], cos, sin, first_half)
            yield c, (r * mult if mult != 1.0 else r).astype(BF16)

    t = jnp.minimum(i, N_BLK - 1)
    z = jnp.where(t < N_LAT_BLK, x_ref[...], ctx_ref[...])
    yp = jnp.where(t < N_LAT_BLK, ysl_ref[...], ysc_ref[...])
    for qt in range(S5_QT):
        for k in range(S5_CHUNK // S5_PAIRS):
            tiles = _block_transpose4([yp[qt * S5_PAIRS + pr][:, k * LANES:(k + 1) * LANES]
                                       for pr in range(S5_PAIRS)])
            for n, tile in enumerate(tiles):
                ys_scr[qt, pl.ds(k * S5_PAIRS + n, TN // S5_CHUNK, stride=S5_CHUNK), :] = tile
    ys = jnp.concatenate([ys_scr[qt] for qt in range(S5_QT)], axis=-1)
    gate = mod0_ref[pl.ds(_block_mod_row(t), 1), 2 * D:3 * D]
    tail = _tail0_stages(z, ys, rest_ref, gate, gluw_ref, glub_ref, sg_g_ref, sg_b_ref, sgw_ref,
                         sgbias_ref, wout_scr, lng_ref, lnb_ref)

    half_q = dq // 2
    p_q0 = project(0, half_q)
    next(tail)
    p_q1 = project(half_q, dq)
    next(tail)
    p_k = project(dq, dq + dkv)
    vt_ref[...] = lax.dot_general(wbf_scr[:, dq + dkv:dq + 2 * dkv], h, (((0,), (1,)), ((), ())),
                                  preferred_element_type=F32).astype(BF16)
    next(tail)
    next(tail)
    for c0, p in ((0, p_q0), (half_q, p_q1)):
        for c, r in roped_tiles(p, scale):
            q_ref[:, c0 + c * LANES:c0 + (c + 1) * LANES] = r
    for c, r in roped_tiles(p_k, 1.0):
        k_ref[:, c * LANES:(c + 1) * LANES] = r
    p_g = project(dq + 2 * dkv, ODD_IN)
    z1 = next(tail)
    z1_ref[...] = z1
    zprev_ref[...] = z1
    g_ref[...] = p_g.astype(BF16)


def _mid(x2, ctx2, mod0, mod1, ys_lat, ys_ctx, rest, glu_w, glu_b, sg_g, sg_b, sg_w, sg_bias,
         w_out, ln_g, ln_b, w_in1, row_tab, col_tab):
    dq = N_HEADS * HD
    dkv = N_KV * HD
    once = dict(pipeline_mode=pl.Buffered(1))
    row = lambda n: pl.BlockSpec((1, n), lambda i: (0, 0))
    lat = lambda i: (jnp.minimum(i, N_LAT_BLK - 1), 0)
    tail_blk = lambda i: (jnp.minimum(i, N_BLK - 1), 0)
    proj_blk = lambda i: (jnp.maximum(i - 1, 0), 0)
    return pl.pallas_call(
        _mid_kernel,
        grid=(N_BLK + 1,),
        in_specs=[
            pl.BlockSpec((TN, D), lat),
            pl.BlockSpec((R_CTX, D), lambda i: (0, 0), **once),
            pl.BlockSpec((8, 3 * D), lambda i: (0, 0)),
            pl.BlockSpec((8, 3 * D), lambda i: (0, 0)),
            pl.BlockSpec((S5_QT * S5_PAIRS, TN // S5_CHUNK, S5_CHUNK * PAIR_BLK),
                         lambda i: (0, jnp.minimum(i, N_LAT_BLK - 1), 0)),
            pl.BlockSpec((S5_QT * S5_PAIRS, R_CTX // S5_CHUNK, S5_CHUNK * PAIR_BLK),
                         lambda i: (0, 0, 0), **once),
            pl.BlockSpec((TN, EVEN_IN - S5_W), tail_blk),
            pl.BlockSpec((S5_W, S5_W), lambda i: (0, 0), **once),
            row(S5_W), row(SG_W), row(SG_W),
            pl.BlockSpec((SG_HEADS // 2, SG_CHUNK, 2 * SG_CHUNK), lambda i: (0, 0, 0)),
            pl.BlockSpec((SG_CHUNK, SG_W), lambda i: (0, 0)),
            pl.BlockSpec((S5_W + SG_W, D), lambda i: (0, 0), **once),
            row(D), row(D),
            pl.BlockSpec((D, ODD_IN), lambda i: (0, 0), **once),
            pl.BlockSpec((2, N_LAT // GRID_W, LANES), lambda i: (0, 0, 0)),
            pl.BlockSpec((2, GRID_W, LANES), lambda i: (0, 0, 0)),
        ],
        out_specs=[
            pl.BlockSpec((TN, D), tail_blk),
            pl.BlockSpec((TN, dq), proj_blk),
            pl.BlockSpec((TN, dkv), proj_blk),
            pl.BlockSpec((dkv, TN), lambda i: (0, jnp.maximum(i - 1, 0))),
            pl.BlockSpec((TN, dq), proj_blk),
        ],
        out_shape=[
            jax.ShapeDtypeStruct((R_ALL, D), F32),
            jax.ShapeDtypeStruct((R_ALL, dq), BF16),
            jax.ShapeDtypeStruct((R_ALL, dkv), BF16),
            jax.ShapeDtypeStruct((dkv, R_ALL), BF16),
            jax.ShapeDtypeStruct((R_ALL, dq), BF16),
        ],
        scratch_shapes=[pltpu.VMEM((TN, D), F32), pltpu.VMEM((D, ODD_IN), BF16),
                        pltpu.VMEM((S5_W + SG_W, D), BF16), pltpu.VMEM((S5_QT, TN, LANES), F32)],
        compiler_params=pltpu.CompilerParams(vmem_limit_bytes=VMEM_LIMIT),
        name="mid",
    )(x2, ctx2, mod0, mod1, ys_lat, ys_ctx, rest, glu_w, glu_b, sg_g, sg_b, sg_w, sg_bias, w_out,
      ln_g, ln_b, w_in1, row_tab, col_tab)


N_QBLK = N_LAT // ATT_BLK
GRP = N_HEADS // N_KV
ATT_SUB = 4
ATT_ROWS = ATT_SUB * ATT_BLK
N_QSTEP = N_QBLK // ATT_SUB
N_ATT_STEPS = B * N_QSTEP
assert ATT_ROWS == TN
assert WINDOW == ATT_BLK
ONES_ROWS = 16


def _attn_kernel(sink_ref, q_ref, kp_ref, kc_ref, kn_ref, kx_ref, vp_ref, vc_ref, vn_ref, vx_ref,
                 g_ref, z_ref, mod_ref, wout_ref, lng_ref, lnb_ref, out_ref, o_scr, wout_scr):
    step = pl.program_id(0)

    @pl.when(step == 0)
    def _():
        o_scr[...] = jnp.zeros((ATT_ROWS, N_HEADS * HD), BF16)
        wout_scr[...] = wout_ref[...].astype(BF16)

    t = jnp.maximum(step - 1, 0)
    gate = mod_ref[pl.ds(_block_mod_row(t), 1), 2 * D:3 * D]
    mix = (o_scr[...].astype(F32) * _silu(g_ref[...].astype(F32))).astype(BF16)
    y = jnp.dot(mix, wout_scr[...], preferred_element_type=F32) * gate
    out_ref[...] = _layer_norm(ALPHA * z_ref[...] + y, lng_ref[...], lnb_ref[...])

    i = jnp.minimum(step, N_ATT_STEPS - 1) % N_QSTEP
    half = ATT_BLK // 2
    n_win = 2 * ATT_BLK + half
    n_keys = n_win + N_CTX
    nq = GRP * half
    k_win = jnp.concatenate([kp_ref[...], kc_ref[...], kn_ref[...]], axis=0)
    vt_win = jnp.concatenate([vp_ref[...], vc_ref[...], vn_ref[...]], axis=1)
    k_ctx = kx_ref[...]
    vt_ctx = vx_ref[...]

    kpos = lax.broadcasted_iota(jnp.int32, (half, half), 0)
    qpos = lax.broadcasted_iota(jnp.int32, (half, half), 1)
    tile_q = lambda a: jnp.concatenate([a] * GRP, axis=1)
    neg = lambda ok: tile_q(jnp.where(ok, 0.0, NEG_INF))
    everyone = kpos >= 0
    qgrp = lax.broadcasted_iota(jnp.int32, (1, nq), 1) // half
    ones = jnp.ones((ONES_ROWS, n_keys), BF16)
    q = q_ref[...]

    halves = [(u, v) for u in range(ATT_SUB) for v in range(2)]
    units = [(n, h) for n in range(len(halves)) for h in range(N_KV)]
    k_all, vt_all, bias = [], [], []
    for u, v in halves:
        first = u * ATT_BLK + v * half
        k_all.append(jnp.concatenate([k_win[first:first + n_win], k_ctx], axis=0))
        vt_all.append(jnp.concatenate([vt_win[:, first:first + n_win], vt_ctx], axis=1))
        in_prev = [blk for blk in range(5) if first + blk * half < ATT_BLK]
        in_next = [blk for blk in range(5) if first + blk * half >= ATT_BLK + ATT_ROWS]
        blocks = {}
        for blk in sorted(set([0, 4] + in_prev + in_next)):
            ok = (kpos >= qpos) if blk == 0 else ((kpos <= qpos) if blk == 4 else everyone)
            if blk in in_prev:
                ok = ok & (i > 0)
            if blk in in_next:
                ok = ok & (i < N_QSTEP - 1)
            blocks[blk] = neg(ok)
        bias.append(blocks)
    scores = []
    for n, h in units:
        u, v = halves[n]
        r0 = u * ATT_BLK + v * half
        kh = k_all[n][:, h * HD:(h + 1) * HD]
        qh = jnp.concatenate(
            [q[r0:r0 + half, (h * GRP + g) * HD:(h * GRP + g + 1) * HD] for g in range(GRP)],
            axis=0)
        scores.append(lax.dot_general(kh, qh, (((1,), (1,)), ((), ())),
                                      preferred_element_type=F32))
    probs = []
    for (n, h), s in zip(units, scores):
        s = jnp.concatenate(
            [s[blk * half:(blk + 1) * half] + bias[n][blk] if blk in bias[n]
             else s[blk * half:(blk + 1) * half] for blk in range(5)] + [s[n_win:]], axis=0)
        sink = jnp.zeros((1, nq), F32)
        for g in range(GRP):
            sink = jnp.where(qgrp == g, sink_ref[h * GRP + g] * LOG2E, sink)
        m = jnp.maximum(jnp.max(s, axis=0, keepdims=True), sink)
        probs.append((jnp.exp2(s - m).astype(BF16), jnp.exp2(sink - m)))
    outs = [[None] * N_HEADS for _ in halves]
    for (n, h), (p, p_sink) in zip(units, probs):
        vt1 = jnp.concatenate([vt_all[n][h * HD:(h + 1) * HD], ones], axis=0)
        ov = jnp.dot(vt1, p, preferred_element_type=F32)
        o_t = ov[:HD] / (ov[HD:HD + 1] + p_sink)
        for g in range(GRP):
            outs[n][h * GRP + g] = o_t[:, g * half:(g + 1) * half]
    for u in range(ATT_SUB):
        o_t = jnp.concatenate(
            [jnp.concatenate([outs[2 * u][hh], outs[2 * u + 1][hh]], axis=1)
             for hh in range(N_HEADS)], axis=0)
        o_scr[u * ATT_BLK:(u + 1) * ATT_BLK, :] = jnp.transpose(o_t).astype(BF16)


def _attention_tail(sink, q, k, vt, g, z1, mod, w_out, ln_g, ln_b):
    dq = N_HEADS * HD
    dkv = N_KV * HD
    ctx_blk0 = R_LAT // N_CTX
    cur = lambda s: jnp.minimum(s, N_ATT_STEPS - 1)
    batch = lambda s: cur(s) // N_QSTEP
    qstep = lambda s: cur(s) % N_QSTEP
    prev_blk = lambda s: batch(s) * N_QBLK + jnp.maximum(ATT_SUB * qstep(s) - 1, 0)
    next_blk = lambda s: batch(s) * N_QBLK + jnp.minimum(ATT_SUB * (qstep(s) + 1), N_QBLK - 1)
    tail = lambda s: jnp.maximum(s - 1, 0)
    once = dict(pipeline_mode=pl.Buffered(1))
    row = lambda n: pl.BlockSpec((1, n), lambda s, sk: (0, 0))
    return pl.pallas_call(
        _attn_kernel,
        grid_spec=pltpu.PrefetchScalarGridSpec(
            num_scalar_prefetch=1,
            grid=(N_ATT_STEPS + 1,),
            in_specs=[
                pl.BlockSpec((ATT_ROWS, dq), lambda s, sk: (cur(s), 0)),
                pl.BlockSpec((ATT_BLK, dkv), lambda s, sk: (prev_blk(s), 0)),
                pl.BlockSpec((ATT_ROWS, dkv), lambda s, sk: (cur(s), 0)),
                pl.BlockSpec((ATT_BLK, dkv), lambda s, sk: (next_blk(s), 0)),
                pl.BlockSpec((N_CTX, dkv), lambda s, sk: (ctx_blk0 + batch(s), 0)),
                pl.BlockSpec((dkv, ATT_BLK), lambda s, sk: (0, prev_blk(s))),
                pl.BlockSpec((dkv, ATT_ROWS), lambda s, sk: (0, cur(s))),
                pl.BlockSpec((dkv, ATT_BLK), lambda s, sk: (0, next_blk(s))),
                pl.BlockSpec((dkv, N_CTX), lambda s, sk: (0, ctx_blk0 + batch(s))),
                pl.BlockSpec((ATT_ROWS, dq), lambda s, sk: (tail(s), 0)),
                pl.BlockSpec((ATT_ROWS, D), lambda s, sk: (tail(s), 0)),
                pl.BlockSpec((8, 3 * D), lambda s, sk: (0, 0)),
                pl.BlockSpec((dq, D), lambda s, sk: (0, 0), **once),
                row(D), row(D),
            ],
            out_specs=pl.BlockSpec((ATT_ROWS, D), lambda s, sk: (tail(s), 0)),
            scratch_shapes=[pltpu.VMEM((ATT_ROWS, dq), BF16), pltpu.VMEM((dq, D), BF16)],
        ),
        out_shape=jax.ShapeDtypeStruct((R_LAT, D), F32),
        compiler_params=pltpu.CompilerParams(vmem_limit_bytes=VMEM_LIMIT),
        name="attention",
    )(sink, q, k, k, k, k, vt, vt, vt, vt, g, z1, mod, w_out, ln_g, ln_b)


def kernel(x, c, ctx, c_ctx, mod_w, mod_b, ln_g, ln_b, e_w_in, e_w_out, s5_lam_re, s5_lam_im,
           s5_log_dt, s5_b_re, s5_b_im, s5_c_re, s5_c_im, s5_d, glu_w, glu_b, sg_ln_g, sg_ln_b,
           sg_w, sg_b, o_w_in, o_w_out, o_sink):
    x2 = x.reshape(R_LAT, D)
    ctx2 = ctx.reshape(R_CTX, D)
    mod = _modulation(c, c_ctx, mod_w, mod_b)

    vp, rest = _in0(x2, ctx2, mod[0], e_w_in[0])
    t_mat, e_mat, ft_mat, al_re, al_im = _s5_prep(
        s5_lam_re[0], s5_lam_im[0], s5_log_dt[0], s5_b_re[0], s5_b_im[0], s5_c_re[0], s5_c_im[0],
        s5_d[0])
    ys_lat, ys_ctx = _s5_out(vp, al_re, al_im, e_mat, t_mat, ft_mat)
    sg_bias = jnp.repeat(jnp.transpose(sg_b[0]), SG_W // SG_HEADS, axis=1)
    sg_w2 = jnp.transpose(sg_w[0].reshape(SG_HEADS // 2, 2, SG_CHUNK, SG_CHUNK), (0, 2, 1, 3))
    sg_w2 = sg_w2.reshape(SG_HEADS // 2, SG_CHUNK, 2 * SG_CHUNK).astype(BF16)
    row_tab, col_tab = _rope_tables()
    z1, q, k, vt, g = _mid(
        x2, ctx2, mod[0], mod[1], ys_lat, ys_ctx, rest, glu_w[0].astype(BF16),
        glu_b[0].reshape(1, S5_W), sg_ln_g[0].reshape(1, SG_W), sg_ln_b[0].reshape(1, SG_W),
        sg_w2, sg_bias, e_w_out[0], ln_g[0].reshape(1, D),
        ln_b[0].reshape(1, D), o_w_in[0], row_tab, col_tab)

    out = _attention_tail(o_sink[0], q, k, vt, g, z1, mod[1], o_w_out[0],
                          ln_g[1].reshape(1, D), ln_b[1].reshape(1, D))
    return out.reshape(B, N_LAT, D)
```

```python
import math

import jax
import jax.numpy as jnp
from jax import lax
from jax.experimental import pallas as pl
from jax.experimental.pallas import tpu as pltpu

F32 = jnp.float32
BF16 = jnp.bfloat16

D = 1024
B = 2
N_LAT = 8192
N_CTX = 256
DEPTH = 2
GRID_W = 64
S5_W = 512
S5_GC = 16
S5_G = 32
S5_P = 64
SG_W = 512
SG_HEADS = 8
SG_CHUNK = 128
N_HEADS = 16
N_KV = 4
HD = 64
WINDOW = 128
ATT_BLK = 128
ROPE_BASE = 10000.0
NEG_INF = -1e30
LN_EPS = 1e-5
ALPHA = (2 * DEPTH) ** 0.25
LOG2E = math.log2(math.e)
EVEN_IN = 2 * S5_W + 3 * SG_W
ODD_IN = 2 * N_HEADS * HD + 2 * N_KV * HD

LANES = 128
VMEM_LIMIT = 56 * 1024 * 1024

R_LAT = B * N_LAT
R_CTX = B * N_CTX
R_ALL = R_LAT + R_CTX
TN = 512
N_LAT_BLK = R_LAT // TN
N_BLK = R_ALL // TN
LAT_BLK_PER_BATCH = N_LAT // TN
CTX_MOD_ROW = B

S5_CHUNK = 8
S5_QT = S5_W // LANES
S5_GPT = LANES // S5_GC
S5_ST = S5_GPT * S5_P
S5_PAIRS = S5_GPT // 2
PAIR_BLK = 2 * S5_GC
LAT_CHUNKS = N_LAT // S5_CHUNK
CTX_CHUNKS = N_CTX // S5_CHUNK
BATCH_CHUNKS = CTX_CHUNKS + LAT_CHUNKS
N_SEG = 16
SEG_CHUNKS = BATCH_CHUNKS // N_SEG
BWD_RESET_STEP = LAT_CHUNKS - (N_SEG - 1) * SEG_CHUNKS


def _sigmoid(x):
    return 1.0 / (1.0 + jnp.exp(-x))


def _silu(x):
    return x * _sigmoid(x)


def _gelu_tanh(x):
    return 0.5 * x * (1.0 + jnp.tanh(math.sqrt(2.0 / math.pi) * (x + 0.044715 * (x * x * x))))


def _layer_norm(x, g, b):
    mu = jnp.mean(x, axis=-1, keepdims=True)
    xc = x - mu
    var = jnp.mean(xc * xc, axis=-1, keepdims=True)
    return xc * lax.rsqrt(var + LN_EPS) * g + b


def _split_bf16(x):
    hi = x.astype(BF16)
    return hi, (x - hi.astype(F32)).astype(BF16)


def _dot_split(x, y_split, dims):
    x_hi, x_lo = _split_bf16(x)
    y_hi, y_lo = y_split
    dot = lambda a, b: lax.dot_general(a, b, dims, preferred_element_type=F32)
    return dot(x_hi, y_hi) + (dot(x_hi, y_lo) + dot(x_lo, y_hi))


def _block_mod_row(i):
    return jnp.where(i < N_LAT_BLK, i // LAT_BLK_PER_BATCH, CTX_MOD_ROW)


def _mod_kernel(cvt_ref, w_ref, b_ref, o_ref):
    s_t = _silu(cvt_ref[...])
    tn = o_ref.shape[2]
    rows = []
    for r in range(B + 1):
        acc = jnp.zeros((8, tn), F32)
        for g in range(D // 8):
            acc = acc + s_t[8 * g:8 * (g + 1), r:r + 1] * w_ref[0, 8 * g:8 * (g + 1), :]
        rows.append(jnp.sum(acc, axis=0, keepdims=True))
    rows.append(jnp.zeros((8 - (B + 1), tn), F32))
    o_ref[0] = jnp.concatenate(rows, axis=0) + b_ref[0]


def _modulation(c, c_ctx, mod_w, mod_b):
    cv = jnp.concatenate([c, c_ctx[None], jnp.zeros((8 - B - 1, D), F32)], axis=0)
    cv = jnp.transpose(cv)
    tn = 1024
    return pl.pallas_call(
        _mod_kernel,
        grid=(DEPTH, 3 * D // tn),
        in_specs=[
            pl.BlockSpec((D, 8), lambda l, j: (0, 0)),
            pl.BlockSpec((1, D, tn), lambda l, j: (l, 0, j)),
            pl.BlockSpec((1, 1, tn), lambda l, j: (l, 0, j)),
        ],
        out_specs=pl.BlockSpec((1, 8, tn), lambda l, j: (l, 0, j)),
        out_shape=jax.ShapeDtypeStruct((DEPTH, 8, 3 * D), F32),
        compiler_params=pltpu.CompilerParams(vmem_limit_bytes=VMEM_LIMIT),
        name="modulation",
    )(cv, mod_w, mod_b.reshape(DEPTH, 1, 3 * D))


def _block_transpose4(tiles):
    tiles = list(tiles)
    blk = lax.broadcasted_iota(jnp.int32, tiles[0].shape, 1) // PAIR_BLK
    for dist in (2, 1):
        keep = (blk & dist) == 0
        for i in range(len(tiles)):
            if i & dist:
                continue
            lo, hi = tiles[i], tiles[i + dist]
            tiles[i] = jnp.where(keep, lo, pltpu.roll(hi, dist * PAIR_BLK, axis=1))
            tiles[i + dist] = jnp.where(keep, pltpu.roll(lo, LANES - dist * PAIR_BLK, axis=1), hi)
    return tiles


def _in0_kernel(x_ref, ctx_ref, mod_ref, w_ref, vp_ref, rest_ref, xa_scr, wbf_scr):
    L = S5_CHUNK
    i = pl.program_id(0)

    @pl.when(i == 0)
    def _():
        wbf_scr[...] = w_ref[...].astype(BF16)

    z = jnp.where(i < N_LAT_BLK, x_ref[...], ctx_ref[...])
    m = mod_ref[pl.ds(_block_mod_row(i), 1), :]
    h = z * (1.0 + m[:, D:2 * D]) + m[:, :D]
    p = jnp.dot(h.astype(BF16), wbf_scr[...], preferred_element_type=F32)
    rest_ref[...] = p[:, S5_W:].astype(BF16)
    for q in range(S5_QT):
        xa_scr[q] = p[:, q * LANES:(q + 1) * LANES]
        tiles = [xa_scr[q, pl.ds(t, TN // L, stride=L), :] for t in range(L)]
        halves = [_block_transpose4(tiles[k * S5_PAIRS:(k + 1) * S5_PAIRS])
                  for k in range(L // S5_PAIRS)]
        for pr in range(S5_PAIRS):
            vp_ref[q * S5_PAIRS + pr] = jnp.concatenate(
                [half[pr] for half in halves], axis=-1).astype(BF16)


def _in0(x2, ctx2, mod, w_in):
    pair_w = S5_CHUNK * PAIR_BLK
    return pl.pallas_call(
        _in0_kernel,
        grid=(N_BLK,),
        in_specs=[
            pl.BlockSpec((TN, D), lambda i: (jnp.minimum(i, N_LAT_BLK - 1), 0)),
            pl.BlockSpec((R_CTX, D), lambda i: (0, 0)),
            pl.BlockSpec((8, 3 * D), lambda i: (0, 0)),
            pl.BlockSpec((D, EVEN_IN), lambda i: (0, 0)),
        ],
        out_specs=[
            pl.BlockSpec((S5_QT * S5_PAIRS, TN // S5_CHUNK, pair_w), lambda i: (0, i, 0)),
            pl.BlockSpec((TN, EVEN_IN - S5_W), lambda i: (i, 0)),
        ],
        out_shape=[
            jax.ShapeDtypeStruct((S5_QT * S5_PAIRS, R_ALL // S5_CHUNK, pair_w), BF16),
            jax.ShapeDtypeStruct((R_ALL, EVEN_IN - S5_W), BF16),
        ],
        scratch_shapes=[pltpu.VMEM((S5_QT, TN, LANES), F32), pltpu.VMEM((D, EVEN_IN), BF16)],
        compiler_params=pltpu.CompilerParams(vmem_limit_bytes=VMEM_LIMIT),
        name="in0",
    )(x2, ctx2, mod, w_in)


def _s5_prep_kernel(rows_ref, bre_ref, bim_ref, cre_ref, cim_ref, dsk_ref, t_ref, e_ref, ft_ref,
                    alr_ref, ali_ref):
    L = S5_CHUNK
    tile = (LANES, S5_ST)
    same_group = (lax.broadcasted_iota(jnp.int32, tile, 0) // S5_GC
                  == lax.broadcasted_iota(jnp.int32, tile, 1) // S5_P)
    nt_dims = (((1,), (1,)), ((), ()))
    nn_dims = (((1,), (0,)), ((), ()))
    pick_ch = (lax.broadcasted_iota(jnp.int32, (LANES, S5_GC), 0) % S5_GC
               == lax.broadcasted_iota(jnp.int32, (LANES, S5_GC), 1)).astype(BF16)
    rep_st = (lax.broadcasted_iota(jnp.int32, (S5_P, S5_ST), 0)
              == lax.broadcasted_iota(jnp.int32, (S5_P, S5_ST), 1) % S5_P).astype(BF16)

    def spread_b(b):
        hi, lo = _split_bf16(b)
        dot = lambda v: lax.dot_general(pick_ch, v, nt_dims, preferred_element_type=F32)
        return jnp.where(same_group, dot(hi) + dot(lo), 0.0)

    def spread_c(c):
        hi, lo = _split_bf16(c)
        dot = lambda v: lax.dot_general(v, rep_st, nn_dims, preferred_element_type=F32)
        return jnp.where(same_group, dot(hi) + dot(lo), 0.0)

    pair_rows = lambda pr: slice(pr * PAIR_BLK, (pr + 1) * PAIR_BLK)
    pair_lanes = lambda pr: slice(pr * LANES, (pr + 1) * LANES)
    lag_strip = []
    for d in range(2):
        lam_re = rows_ref[d, 0]
        lam_im = rows_ref[d, 1]
        dt = jnp.exp(rows_ref[d, 2])
        pw = []
        for k in range(L + 1):
            mag = jnp.exp(lam_re * dt * float(k))
            ang = lam_im * dt * float(k)
            pw.append((mag * jnp.cos(ang), mag * jnp.sin(ang)))
        alr_ref[d] = pw[L][0]
        ali_ref[d] = pw[L][1]
        den = lam_re * lam_re + lam_im * lam_im
        nr = pw[1][0] - 1.0
        ni = pw[1][1]
        f_re = (nr * lam_re + ni * lam_im) / den
        f_im = (ni * lam_re - nr * lam_im) / den
        b_re = spread_b(bre_ref[d])
        b_im = spread_b(bim_ref[d])
        bb_re = f_re * b_re - f_im * b_im
        bb_im = f_re * b_im + f_im * b_re
        c_re = spread_c(cre_ref[d])
        c_im = spread_c(cim_ref[d])
        col_re = slice(2 * d * LANES, (2 * d + 1) * LANES)
        col_im = slice((2 * d + 1) * LANES, (2 * d + 2) * LANES)
        strips = []
        for pr in range(S5_PAIRS):
            blk = lambda a: a[pair_rows(pr), pair_lanes(pr)]
            p_re = [blk(jnp.broadcast_to(pw[k][0], tile)) for k in range(L + 1)]
            p_im = [blk(jnp.broadcast_to(pw[k][1], tile)) for k in range(L + 1)]
            bbr, bbi, cr, ci = blk(bb_re), blk(bb_im), blk(c_re), blk(c_im)
            for k in range(L):
                s = (L - 1 - k) if d == 0 else k
                rows = slice(s * PAIR_BLK, (s + 1) * PAIR_BLK)
                e_ref[0, pr, rows, col_re] = (p_re[k] * bbr - p_im[k] * bbi).astype(BF16)
                e_ref[0, pr, rows, col_im] = (p_re[k] * bbi + p_im[k] * bbr).astype(BF16)
            cw = [(cr * p_re[k] - ci * p_im[k], -(cr * p_im[k] + ci * p_re[k]))
                  for k in range(L + 1)]
            for t in range(L):
                k = (t + 1) if d == 0 else (L - t)
                rows = slice(t * PAIR_BLK, (t + 1) * PAIR_BLK)
                ft_ref[0, pr, rows, col_re] = cw[k][0].astype(BF16)
                ft_ref[0, pr, rows, col_im] = cw[k][1].astype(BF16)
            lags = range(L) if d == 0 else range(L - 1, -1, -1)
            c_stack = jnp.concatenate(
                [jnp.concatenate([cw[k][0], cw[k][1]], axis=-1) for k in lags], axis=0)
            strips.append(_dot_split(jnp.concatenate([bbr, bbi], axis=-1),
                                     _split_bf16(c_stack), nt_dims))
        lag_strip.append(strips)

    strip_w = 2 * L * PAIR_BLK
    mid = (L - 1) * PAIR_BLK
    widen = lambda a: jnp.concatenate([a, jnp.zeros((a.shape[0], strip_w - a.shape[1]), F32)], axis=-1)
    row = lax.broadcasted_iota(jnp.int32, (PAIR_BLK, strip_w), 0)
    lane = lax.broadcasted_iota(jnp.int32, (PAIR_BLK, strip_w), 1)
    d_row = widen(dsk_ref[0])
    for pr in range(S5_PAIRS):
        strip = (widen(lag_strip[1][pr])
                 + pltpu.roll(widen(lag_strip[0][pr]), mid, axis=1)
                 + jnp.where(lane == row + mid,
                             pltpu.roll(d_row, mid - pr * PAIR_BLK, axis=1), 0.0))
        for s in range(L):
            shift = (strip_w - (L - 1 - s) * PAIR_BLK) % strip_w
            blk = pltpu.roll(strip, shift, axis=1) if shift else strip
            t_ref[0, pr, s * PAIR_BLK:(s + 1) * PAIR_BLK, :] = blk[:, :L * PAIR_BLK].astype(BF16)


def _s5_prep(lam_re, lam_im, log_dt, b_re, b_im, c_re, c_im, d_skip):
    L = S5_CHUNK
    n_st = S5_G * S5_P
    pair_w = L * PAIR_BLK
    rows = jnp.stack([lam_re.reshape(2, 1, n_st), lam_im.reshape(2, 1, n_st),
                      jnp.repeat(log_dt, S5_P, axis=1).reshape(2, 1, n_st)], axis=1)
    b_spec = pl.BlockSpec((2, S5_ST, S5_GC), lambda q: (0, q, 0))
    c_spec = pl.BlockSpec((2, LANES, S5_P), lambda q: (0, q, 0))
    return pl.pallas_call(
        _s5_prep_kernel,
        grid=(S5_QT,),
        in_specs=[
            pl.BlockSpec((2, 3, 1, S5_ST), lambda q: (0, 0, 0, q)),
            b_spec, b_spec, c_spec, c_spec,
            pl.BlockSpec((1, 1, LANES), lambda q: (q, 0, 0)),
        ],
        out_specs=[
            pl.BlockSpec((1, S5_PAIRS, pair_w, pair_w), lambda q: (q, 0, 0, 0)),
            pl.BlockSpec((1, S5_PAIRS, pair_w, 4 * LANES), lambda q: (q, 0, 0, 0)),
            pl.BlockSpec((1, S5_PAIRS, pair_w, 4 * LANES), lambda q: (q, 0, 0, 0)),
            pl.BlockSpec((2, 1, S5_ST), lambda q: (q, 0, 0)),
            pl.BlockSpec((2, 1, S5_ST), lambda q: (q, 0, 0)),
        ],
        out_shape=[
            jax.ShapeDtypeStruct((S5_QT, S5_PAIRS, pair_w, pair_w), BF16),
            jax.ShapeDtypeStruct((S5_QT, S5_PAIRS, pair_w, 4 * LANES), BF16),
            jax.ShapeDtypeStruct((S5_QT, S5_PAIRS, pair_w, 4 * LANES), BF16),
            jax.ShapeDtypeStruct((2 * S5_QT, 1, S5_ST), F32),
            jax.ShapeDtypeStruct((2 * S5_QT, 1, S5_ST), F32),
        ],
        compiler_params=pltpu.CompilerParams(vmem_limit_bytes=VMEM_LIMIT),
        name="s5_prep",
    )(rows, b_re.reshape(2, n_st, S5_GC), b_im.reshape(2, n_st, S5_GC),
      c_re.reshape(2, S5_W, S5_P), c_im.reshape(2, S5_W, S5_P), d_skip.reshape(S5_QT, 1, LANES))


N_SLAB = S5_ST // LANES


def _cmul(ar, ai, br, bi):
    return ar * br - ai * bi, ar * bi + ai * br


def _cpow(ar, ai, n):
    res = None
    while n:
        if n & 1:
            res = (ar, ai) if res is None else _cmul(res[0], res[1], ar, ai)
        n >>= 1
        if n:
            ar, ai = _cmul(ar, ai, ar, ai)
    return res


def _scan_boundary_states(loc_ref, out_ref, ar_ref, ai_ref, backward):
    a_row = [(ar_ref[0][:, k * LANES:(k + 1) * LANES], ai_ref[0][:, k * LANES:(k + 1) * LANES])
             for k in range(N_SLAB)]
    a_tile = [(jnp.broadcast_to(r, (N_SEG, LANES)), jnp.broadcast_to(i, (N_SEG, LANES)))
              for r, i in a_row]

    def sweep(lo, hi, carry, store):
        def step(i, carry):
            off = (SEG_CHUNKS - 1 - i) if backward else i
            rows = pl.ds(off, N_SEG, stride=SEG_CHUNKS)
            new = []
            for k, (cr, ci) in enumerate(carry):
                xr = loc_ref[k, rows, :]
                xi = loc_ref[N_SLAB + k, rows, :]
                if store:
                    out_ref[k, rows, :] = cr
                    out_ref[N_SLAB + k, rows, :] = ci
                nr, ni = _cmul(a_tile[k][0], a_tile[k][1], cr, ci)
                new.append((nr + xr, ni + xi))
            return tuple(new)
        return lax.fori_loop(lo, hi, step, carry)

    def full(carry, store):
        if not backward:
            return sweep(0, SEG_CHUNKS, carry, store)
        carry = sweep(0, BWD_RESET_STEP, carry, store)
        first = lax.broadcasted_iota(jnp.int32, (N_SEG, LANES), 0) == 0
        carry = tuple((jnp.where(first, 0.0, cr), jnp.where(first, 0.0, ci)) for cr, ci in carry)
        return sweep(BWD_RESET_STEP, SEG_CHUNKS, carry, store)

    zero_tile = jnp.zeros((N_SEG, LANES), F32)
    ends = full(tuple((zero_tile, zero_tile) for _ in range(N_SLAB)), False)

    starts = []
    for k, (er, ei) in enumerate(ends):
        pr, pi = _cpow(a_row[k][0], a_row[k][1], SEG_CHUNKS)
        if backward:
            order = range(N_SEG - 1, -1, -1)
            g = (er[0:1], ei[0:1])
        else:
            order = range(N_SEG)
            g = (jnp.zeros((1, LANES), F32), jnp.zeros((1, LANES), F32))
        rows_r = [None] * N_SEG
        rows_i = [None] * N_SEG
        for j in order:
            rows_r[j], rows_i[j] = g
            nr, ni = _cmul(pr, pi, g[0], g[1])
            g = (nr + er[j:j + 1], ni + ei[j:j + 1])
        starts.append((jnp.concatenate(rows_r, axis=0), jnp.concatenate(rows_i, axis=0)))
    full(tuple(starts), True)


def _s5_out_kernel(vl_ref, vc_ref, ar_ref, ai_ref, e_ref, t_ref, ft_ref, yl_ref, yc_ref, st_ref):
    n_dir = 2 * N_SLAB
    rows = [jnp.concatenate([vc_ref[pr], vl_ref[pr]], axis=0) for pr in range(S5_PAIRS)]
    for pr in range(S5_PAIRS):
        st = jnp.dot(rows[pr], e_ref[0, pr], preferred_element_type=F32)
        for comp in range(4):
            st_ref[comp * N_SLAB + pr] = st[:, comp * LANES:(comp + 1) * LANES]
    for d in range(2):
        st_d = st_ref.at[pl.ds(d * n_dir, n_dir)]
        _scan_boundary_states(st_d, st_d, ar_ref.at[pl.ds(d, 1)], ai_ref.at[pl.ds(d, 1)], d == 1)

    for pr in range(S5_PAIRS):
        st = jnp.concatenate([st_ref[comp * N_SLAB + pr] for comp in range(4)],
                             axis=-1).astype(BF16)
        y = (jnp.dot(rows[pr], t_ref[0, pr], preferred_element_type=F32)
             + lax.dot_general(st, ft_ref[0, pr], (((1,), (1,)), ((), ())),
                               preferred_element_type=F32))
        yc_ref[pr] = y[:CTX_CHUNKS]
        yl_ref[pr] = y[CTX_CHUNKS:]


def _s5_out(vp, al_re, al_im, e_mat, t_mat, ft_mat):
    pair_w = S5_CHUNK * PAIR_BLK
    ctx_blk0 = (R_LAT // S5_CHUNK) // CTX_CHUNKS
    return pl.pallas_call(
        _s5_out_kernel,
        grid=(S5_QT, B),
        in_specs=[
            pl.BlockSpec((S5_PAIRS, LAT_CHUNKS, pair_w), lambda q, b: (q, b, 0)),
            pl.BlockSpec((S5_PAIRS, CTX_CHUNKS, pair_w), lambda q, b: (q, ctx_blk0 + b, 0)),
            pl.BlockSpec((2, 1, S5_ST), lambda q, b: (q, 0, 0)),
            pl.BlockSpec((2, 1, S5_ST), lambda q, b: (q, 0, 0)),
            pl.BlockSpec((1, S5_PAIRS, pair_w, 4 * LANES), lambda q, b: (q, 0, 0, 0)),
            pl.BlockSpec((1, S5_PAIRS, pair_w, pair_w), lambda q, b: (q, 0, 0, 0)),
            pl.BlockSpec((1, S5_PAIRS, pair_w, 4 * LANES), lambda q, b: (q, 0, 0, 0)),
        ],
        out_specs=[
            pl.BlockSpec((S5_PAIRS, LAT_CHUNKS, pair_w), lambda q, b: (q, b, 0)),
            pl.BlockSpec((S5_PAIRS, CTX_CHUNKS, pair_w), lambda q, b: (q, b, 0)),
        ],
        out_shape=[
            jax.ShapeDtypeStruct((S5_QT * S5_PAIRS, B * LAT_CHUNKS, pair_w), F32),
            jax.ShapeDtypeStruct((S5_QT * S5_PAIRS, B * CTX_CHUNKS, pair_w), F32),
        ],
        scratch_shapes=[pltpu.VMEM((4 * N_SLAB, BATCH_CHUNKS, LANES), F32)],
        compiler_params=pltpu.CompilerParams(vmem_limit_bytes=VMEM_LIMIT),
        name="s5_out",
    )(vp, vp, al_re, al_im, e_mat, t_mat, ft_mat)


def _tail0_stages(z, ys, rest_ref, gate, gluw_ref, glub_ref, sg_g_ref, sg_b_ref, sgw_ref,
                  sgbias_ref, wout_ref, lng_ref, lnb_ref):
    ga = rest_ref[:, 0:S5_W].astype(F32)
    u = rest_ref[:, S5_W:2 * S5_W].astype(F32)
    v = rest_ref[:, 2 * S5_W:3 * S5_W].astype(F32)
    gb = rest_ref[:, 3 * S5_W:4 * S5_W].astype(F32)

    ya = _gelu_tanh(ys)
    glu = jnp.dot(ya.astype(BF16), gluw_ref[...], preferred_element_type=F32) + glub_ref[...]
    yield None
    ya = ya * _sigmoid(glu) * _silu(ga)
    yield None

    vn = _layer_norm(v, sg_g_ref[...], sg_b_ref[...]).astype(BF16)
    lane = lax.broadcasted_iota(jnp.int32, (SG_CHUNK, LANES), 1)
    first_head = lane < (SG_W // SG_HEADS)
    chunks = []
    for c in range(TN // SG_CHUNK):
        tiles = []
        for j in range(SG_W // LANES):
            vt = vn[c * SG_CHUNK:(c + 1) * SG_CHUNK, j * LANES:(j + 1) * LANES]
            zero = jnp.zeros_like(vt)
            v2 = jnp.concatenate([jnp.where(first_head, vt, zero), jnp.where(first_head, zero, vt)],
                                 axis=0)
            tiles.append(jnp.dot(sgw_ref[j], v2, preferred_element_type=F32))
        chunks.append(jnp.concatenate(tiles, axis=-1) + sgbias_ref[...])
    s = jnp.concatenate(chunks, axis=0)
    yb = u * s * _silu(gb)
    yield None

    mix = jnp.concatenate([ya, yb], axis=-1).astype(BF16)
    y = jnp.dot(mix, wout_ref[...], preferred_element_type=F32) * gate
    yield None
    yield _layer_norm(ALPHA * z + y, lng_ref[...], lnb_ref[...])


def _rope_tables():
    nf = HD // 4
    n_rows = N_LAT // GRID_W
    lane = jnp.arange(LANES)
    inv = ROPE_BASE ** (-(lane % nf).astype(F32) / nf)
    by_row = ((lane % HD) // (HD // 2) == 0)[None, :]
    sign = jnp.where((lane % (HD // 2)) < nf, -1.0, 1.0)[None, :]
    row_ang = jnp.arange(n_rows, dtype=F32)[:, None] * inv[None, :]
    col_ang = jnp.arange(GRID_W, dtype=F32)[:, None] * inv[None, :]
    zero = jnp.zeros((), F32)
    row_tab = jnp.stack([jnp.where(by_row, jnp.cos(row_ang), zero),
                         jnp.where(by_row, sign * jnp.sin(row_ang), zero)])
    col_tab = jnp.stack([jnp.where(by_row, zero, jnp.cos(col_ang)),
                         jnp.where(by_row, zero, sign * jnp.sin(col_ang))])
    return row_tab, col_tab


def _rope_tile(x, cos, sin, first_half):
    nf = HD // 4
    partner = jnp.where(first_half, pltpu.roll(x, LANES - nf, axis=1), pltpu.roll(x, nf, axis=1))
    return x * cos + partner * sin


def _mid_kernel(x_ref, ctx_ref, mod0_ref, mod1_ref, ysl_ref, ysc_ref, rest_ref, gluw_ref, glub_ref,
                sg_g_ref, sg_b_ref, sgw_ref, sgbias_ref, wout_ref, lng_ref, lnb_ref, w_ref,
                rtab_ref, ctab_ref, z1_ref, q_ref, k_ref, vt_ref, g_ref, zprev_ref, wbf_scr,
                wout_scr, ys_scr):
    i = pl.program_id(0)
    dq = N_HEADS * HD
    dkv = N_KV * HD

    @pl.when(i == 0)
    def _():
        zprev_ref[...] = jnp.zeros((TN, D), F32)
        wbf_scr[...] = w_ref[...].astype(BF16)
        wout_scr[...] = wout_ref[...].astype(BF16)

    j = jnp.maximum(i - 1, 0)
    m1 = mod1_ref[pl.ds(_block_mod_row(j), 1), :]
    h = (zprev_ref[...] * (1.0 + m1[:, D:2 * D]) + m1[:, :D]).astype(BF16)
    grid_row0 = (j % LAT_BLK_PER_BATCH) * (TN // GRID_W)
    table = lambda t: jnp.concatenate(
        [rtab_ref[t, pl.ds(grid_row0 + r, 1), :] + ctab_ref[t] for r in range(TN // GRID_W)],
        axis=0)
    cos = jnp.where(j < N_LAT_BLK, table(0), 1.0)
    sin = jnp.where(j < N_LAT_BLK, table(1), 0.0)
    lane = lax.broadcasted_iota(jnp.int32, (TN, LANES), 1)
    first_half = (lane % (HD // 2)) < (HD // 4)
    scale = HD ** -0.5 * LOG2E

    def project(c0, c1):
        return jnp.dot(h, wbf_scr[:, c0:c1], preferred_element_type=F32)

    def roped_tiles(p, mult):
        for c in range(p.shape[1] // LANES):
            r = _rope_tile(p[:, c * LANES:(c + 1) * LANES], cos, sin, first_half)
            yield c, (r * mult if mult != 1.0 else r).astype(BF16)

    t = jnp.minimum(i, N_BLK - 1)
    z = jnp.where(t < N_LAT_BLK, x_ref[...], ctx_ref[...])
    yp = jnp.where(t < N_LAT_BLK, ysl_ref[...], ysc_ref[...])
    for qt in range(S5_QT):
        for k in range(S5_CHUNK // S5_PAIRS):
            tiles = _block_transpose4([yp[qt * S5_PAIRS + pr][:, k * LANES:(k + 1) * LANES]
                                       for pr in range(S5_PAIRS)])
            for n, tile in enumerate(tiles):
                ys_scr[qt, pl.ds(k * S5_PAIRS + n, TN // S5_CHUNK, stride=S5_CHUNK), :] = tile
    ys = jnp.concatenate([ys_scr[qt] for qt in range(S5_QT)], axis=-1)
    gate = mod0_ref[pl.ds(_block_mod_row(t), 1), 2 * D:3 * D]
    tail = _tail0_stages(z, ys, rest_ref, gate, gluw_ref, glub_ref, sg_g_ref, sg_b_ref, sgw_ref,
                         sgbias_ref, wout_scr, lng_ref, lnb_ref)

    half_q = dq // 2
    p_q0 = project(0, half_q)
    next(tail)
    p_q1 = project(half_q, dq)
    next(tail)
    p_k = project(dq, dq + dkv)
    vt_ref[...] = lax.dot_general(wbf_scr[:, dq + dkv:dq + 2 * dkv], h, (((0,), (1,)), ((), ())),
                                  preferred_element_type=F32).astype(BF16)
    next(tail)
    next(tail)
    for c0, p in ((0, p_q0), (half_q, p_q1)):
        for c, r in roped_tiles(p, scale):
            q_ref[:, c0 + c * LANES:c0 + (c + 1) * LANES] = r
    for c, r in roped_tiles(p_k, 1.0):
        k_ref[:, c * LANES:(c + 1) * LANES] = r
    p_g = project(dq + 2 * dkv, ODD_IN)
    z1 = next(tail)
    z1_ref[...] = z1
    zprev_ref[...] = z1
    g_ref[...] = p_g.astype(BF16)


def _mid(x2, ctx2, mod0, mod1, ys_lat, ys_ctx, rest, glu_w, glu_b, sg_g, sg_b, sg_w, sg_bias,
         w_out, ln_g, ln_b, w_in1, row_tab, col_tab):
    dq = N_HEADS * HD
    dkv = N_KV * HD
    once = dict(pipeline_mode=pl.Buffered(1))
    row = lambda n: pl.BlockSpec((1, n), lambda i: (0, 0))
    lat = lambda i: (jnp.minimum(i, N_LAT_BLK - 1), 0)
    tail_blk = lambda i: (jnp.minimum(i, N_BLK - 1), 0)
    proj_blk = lambda i: (jnp.maximum(i - 1, 0), 0)
    return pl.pallas_call(
        _mid_kernel,
        grid=(N_BLK + 1,),
        in_specs=[
            pl.BlockSpec((TN, D), lat),
            pl.BlockSpec((R_CTX, D), lambda i: (0, 0), **once),
            pl.BlockSpec((8, 3 * D), lambda i: (0, 0)),
            pl.BlockSpec((8, 3 * D), lambda i: (0, 0)),
            pl.BlockSpec((S5_QT * S5_PAIRS, TN // S5_CHUNK, S5_CHUNK * PAIR_BLK),
                         lambda i: (0, jnp.minimum(i, N_LAT_BLK - 1), 0)),
            pl.BlockSpec((S5_QT * S5_PAIRS, R_CTX // S5_CHUNK, S5_CHUNK * PAIR_BLK),
                         lambda i: (0, 0, 0), **once),
            pl.BlockSpec((TN, EVEN_IN - S5_W), tail_blk),
            pl.BlockSpec((S5_W, S5_W), lambda i: (0, 0), **once),
            row(S5_W), row(SG_W), row(SG_W),
            pl.BlockSpec((SG_HEADS // 2, SG_CHUNK, 2 * SG_CHUNK), lambda i: (0, 0, 0)),
            pl.BlockSpec((SG_CHUNK, SG_W), lambda i: (0, 0)),
            pl.BlockSpec((S5_W + SG_W, D), lambda i: (0, 0), **once),
            row(D), row(D),
            pl.BlockSpec((D, ODD_IN), lambda i: (0, 0), **once),
            pl.BlockSpec((2, N_LAT // GRID_W, LANES), lambda i: (0, 0, 0)),
            pl.BlockSpec((2, GRID_W, LANES), lambda i: (0, 0, 0)),
        ],
        out_specs=[
            pl.BlockSpec((TN, D), tail_blk),
            pl.BlockSpec((TN, dq), proj_blk),
            pl.BlockSpec((TN, dkv), proj_blk),
            pl.BlockSpec((dkv, TN), lambda i: (0, jnp.maximum(i - 1, 0))),
            pl.BlockSpec((TN, dq), proj_blk),
        ],
        out_shape=[
            jax.ShapeDtypeStruct((R_ALL, D), F32),
            jax.ShapeDtypeStruct((R_ALL, dq), BF16),
            jax.ShapeDtypeStruct((R_ALL, dkv), BF16),
            jax.ShapeDtypeStruct((dkv, R_ALL), BF16),
            jax.ShapeDtypeStruct((R_ALL, dq), BF16),
        ],
        scratch_shapes=[pltpu.VMEM((TN, D), F32), pltpu.VMEM((D, ODD_IN), BF16),
                        pltpu.VMEM((S5_W + SG_W, D), BF16), pltpu.VMEM((S5_QT, TN, LANES), F32)],
        compiler_params=pltpu.CompilerParams(vmem_limit_bytes=VMEM_LIMIT),
        name="mid",
    )(x2, ctx2, mod0, mod1, ys_lat, ys_ctx, rest, glu_w, glu_b, sg_g, sg_b, sg_w, sg_bias, w_out,
      ln_g, ln_b, w_in1, row_tab, col_tab)


N_QBLK = N_LAT // ATT_BLK
GRP = N_HEADS // N_KV
ATT_SUB = 4
ATT_ROWS = ATT_SUB * ATT_BLK
N_QSTEP = N_QBLK // ATT_SUB
N_ATT_STEPS = B * N_QSTEP
assert ATT_ROWS == TN
assert WINDOW == ATT_BLK
ONES_ROWS = 16


def _attn_kernel(sink_ref, q_ref, kp_ref, kc_ref, kn_ref, kx_ref, vp_ref, vc_ref, vn_ref, vx_ref,
                 g_ref, z_ref, mod_ref, wout_ref, lng_ref, lnb_ref, out_ref, o_scr, wout_scr):
    step = pl.program_id(0)

    @pl.when(step == 0)
    def _():
        o_scr[...] = jnp.zeros((ATT_ROWS, N_HEADS * HD), BF16)
        wout_scr[...] = wout_ref[...].astype(BF16)

    t = jnp.maximum(step - 1, 0)
    gate = mod_ref[pl.ds(_block_mod_row(t), 1), 2 * D:3 * D]
    mix = (o_scr[...].astype(F32) * _silu(g_ref[...].astype(F32))).astype(BF16)
    y = jnp.dot(mix, wout_scr[...], preferred_element_type=F32) * gate
    out_ref[...] = _layer_norm(ALPHA * z_ref[...] + y, lng_ref[...], lnb_ref[...])

    i = jnp.minimum(step, N_ATT_STEPS - 1) % N_QSTEP
    half = ATT_BLK // 2
    n_win = 2 * ATT_BLK + half
    n_keys = n_win + N_CTX
    nq = GRP * half
    k_win = jnp.concatenate([kp_ref[...], kc_ref[...], kn_ref[...]], axis=0)
    vt_win = jnp.concatenate([vp_ref[...], vc_ref[...], vn_ref[...]], axis=1)
    k_ctx = kx_ref[...]
    vt_ctx = vx_ref[...]

    kpos = lax.broadcasted_iota(jnp.int32, (half, half), 0)
    qpos = lax.broadcasted_iota(jnp.int32, (half, half), 1)
    tile_q = lambda a: jnp.concatenate([a] * GRP, axis=1)
    neg = lambda ok: tile_q(jnp.where(ok, 0.0, NEG_INF))
    everyone = kpos >= 0
    qgrp = lax.broadcasted_iota(jnp.int32, (1, nq), 1) // half
    ones = jnp.ones((ONES_ROWS, n_keys), BF16)
    q = q_ref[...]

    halves = [(u, v) for u in range(ATT_SUB) for v in range(2)]
    units = [(n, h) for n in range(len(halves)) for h in range(N_KV)]
    k_all, vt_all, bias = [], [], []
    for u, v in halves:
        first = u * ATT_BLK + v * half
        k_all.append(jnp.concatenate([k_win[first:first + n_win], k_ctx], axis=0))
        vt_all.append(jnp.concatenate([vt_win[:, first:first + n_win], vt_ctx], axis=1))
        in_prev = [blk for blk in range(5) if first + blk * half < ATT_BLK]
        in_next = [blk for blk in range(5) if first + blk * half >= ATT_BLK + ATT_ROWS]
        blocks = {}
        for blk in sorted(set([0, 4] + in_prev + in_next)):
            ok = (kpos >= qpos) if blk == 0 else ((kpos <= qpos) if blk == 4 else everyone)
            if blk in in_prev:
                ok = ok & (i > 0)
            if blk in in_next:
                ok = ok & (i < N_QSTEP - 1)
            blocks[blk] = neg(ok)
        bias.append(blocks)
    scores = []
    for n, h in units:
        u, v = halves[n]
        r0 = u * ATT_BLK + v * half
        kh = k_all[n][:, h * HD:(h + 1) * HD]
        qh = jnp.concatenate(
            [q[r0:r0 + half, (h * GRP + g) * HD:(h * GRP + g + 1) * HD] for g in range(GRP)],
            axis=0)
        scores.append(lax.dot_general(kh, qh, (((1,), (1,)), ((), ())),
                                      preferred_element_type=F32))
    probs = []
    for (n, h), s in zip(units, scores):
        s = jnp.concatenate(
            [s[blk * half:(blk + 1) * half] + bias[n][blk] if blk in bias[n]
             else s[blk * half:(blk + 1) * half] for blk in range(5)] + [s[n_win:]], axis=0)
        sink = jnp.zeros((1, nq), F32)
        for g in range(GRP):
            sink = jnp.where(qgrp == g, sink_ref[h * GRP + g] * LOG2E, sink)
        m = jnp.maximum(jnp.max(s, axis=0, keepdims=True), sink)
        probs.append((jnp.exp2(s - m).astype(BF16), jnp.exp2(sink - m)))
    outs = [[None] * N_HEADS for _ in halves]
    for (n, h), (p, p_sink) in zip(units, probs):
        vt1 = jnp.concatenate([vt_all[n][h * HD:(h + 1) * HD], ones], axis=0)
        ov = jnp.dot(vt1, p, preferred_element_type=F32)
        o_t = ov[:HD] / (ov[HD:HD + 1] + p_sink)
        for g in range(GRP):
            outs[n][h * GRP + g] = o_t[:, g * half:(g + 1) * half]
    for u in range(ATT_SUB):
        o_t = jnp.concatenate(
            [jnp.concatenate([outs[2 * u][hh], outs[2 * u + 1][hh]], axis=1)
             for hh in range(N_HEADS)], axis=0)
        o_scr[u * ATT_BLK:(u + 1) * ATT_BLK, :] = jnp.transpose(o_t).astype(BF16)


def _attention_tail(sink, q, k, vt, g, z1, mod, w_out, ln_g, ln_b):
    dq = N_HEADS * HD
    dkv = N_KV * HD
    ctx_blk0 = R_LAT // N_CTX
    cur = lambda s: jnp.minimum(s, N_ATT_STEPS - 1)
    batch = lambda s: cur(s) // N_QSTEP
    qstep = lambda s: cur(s) % N_QSTEP
    prev_blk = lambda s: batch(s) * N_QBLK + jnp.maximum(ATT_SUB * qstep(s) - 1, 0)
    next_blk = lambda s: batch(s) * N_QBLK + jnp.minimum(ATT_SUB * (qstep(s) + 1), N_QBLK - 1)
    tail = lambda s: jnp.maximum(s - 1, 0)
    once = dict(pipeline_mode=pl.Buffered(1))
    row = lambda n: pl.BlockSpec((1, n), lambda s, sk: (0, 0))
    return pl.pallas_call(
        _attn_kernel,
        grid_spec=pltpu.PrefetchScalarGridSpec(
            num_scalar_prefetch=1,
            grid=(N_ATT_STEPS + 1,),
            in_specs=[
                pl.BlockSpec((ATT_ROWS, dq), lambda s, sk: (cur(s), 0)),
                pl.BlockSpec((ATT_BLK, dkv), lambda s, sk: (prev_blk(s), 0)),
                pl.BlockSpec((ATT_ROWS, dkv), lambda s, sk: (cur(s), 0)),
                pl.BlockSpec((ATT_BLK, dkv), lambda s, sk: (next_blk(s), 0)),
                pl.BlockSpec((N_CTX, dkv), lambda s, sk: (ctx_blk0 + batch(s), 0)),
                pl.BlockSpec((dkv, ATT_BLK), lambda s, sk: (0, prev_blk(s))),
                pl.BlockSpec((dkv, ATT_ROWS), lambda s, sk: (0, cur(s))),
                pl.BlockSpec((dkv, ATT_BLK), lambda s, sk: (0, next_blk(s))),
                pl.BlockSpec((dkv, N_CTX), lambda s, sk: (0, ctx_blk0 + batch(s))),
                pl.BlockSpec((ATT_ROWS, dq), lambda s, sk: (tail(s), 0)),
                pl.BlockSpec((ATT_ROWS, D), lambda s, sk: (tail(s), 0)),
                pl.BlockSpec((8, 3 * D), lambda s, sk: (0, 0)),
                pl.BlockSpec((dq, D), lambda s, sk: (0, 0), **once),
                row(D), row(D),
            ],
            out_specs=pl.BlockSpec((ATT_ROWS, D), lambda s, sk: (tail(s), 0)),
            scratch_shapes=[pltpu.VMEM((ATT_ROWS, dq), BF16), pltpu.VMEM((dq, D), BF16)],
        ),
        out_shape=jax.ShapeDtypeStruct((R_LAT, D), F32),
        compiler_params=pltpu.CompilerParams(vmem_limit_bytes=VMEM_LIMIT),
        name="attention",
    )(sink, q, k, k, k, k, vt, vt, vt, vt, g, z1, mod, w_out, ln_g, ln_b)


def kernel(x, c, ctx, c_ctx, mod_w, mod_b, ln_g, ln_b, e_w_in, e_w_out, s5_lam_re, s5_lam_im,
           s5_log_dt, s5_b_re, s5_b_im, s5_c_re, s5_c_im, s5_d, glu_w, glu_b, sg_ln_g, sg_ln_b,
           sg_w, sg_b, o_w_in, o_w_out, o_sink):
    x2 = x.reshape(R_LAT, D)
    ctx2 = ctx.reshape(R_CTX, D)
    mod = _modulation(c, c_ctx, mod_w, mod_b)

    vp, rest = _in0(x2, ctx2, mod[0], e_w_in[0])
    t_mat, e_mat, ft_mat, al_re, al_im = _s5_prep(
        s5_lam_re[0], s5_lam_im[0], s5_log_dt[0], s5_b_re[0], s5_b_im[0], s5_c_re[0], s5_c_im[0],
        s5_d[0])
    ys_lat, ys_ctx = _s5_out(vp, al_re, al_im, e_mat, t_mat, ft_mat)
    sg_bias = jnp.repeat(jnp.transpose(sg_b[0]), SG_W // SG_HEADS, axis=1)
    sg_w2 = jnp.transpose(sg_w[0].reshape(SG_HEADS // 2, 2, SG_CHUNK, SG_CHUNK), (0, 2, 1, 3))
    sg_w2 = sg_w2.reshape(SG_HEADS // 2, SG_CHUNK, 2 * SG_CHUNK).astype(BF16)
    row_tab, col_tab = _rope_tables()
    z1, q, k, vt, g = _mid(
        x2, ctx2, mod[0], mod[1], ys_lat, ys_ctx, rest, glu_w[0].astype(BF16),
        glu_b[0].reshape(1, S5_W), sg_ln_g[0].reshape(1, SG_W), sg_ln_b[0].reshape(1, SG_W),
        sg_w2, sg_bias, e_w_out[0], ln_g[0].reshape(1, D),
        ln_b[0].reshape(1, D), o_w_in[0], row_tab, col_tab)

    out = _attention_tail(o_sink[0], q, k, vt, g, z1, mod[1], o_w_out[0],
                          ln_g[1].reshape(1, D), ln_b[1].reshape(1, D))
    return out.reshape(B, N_LAT, D)
```

```python
import math

import jax
import jax.numpy as jnp
from jax import lax
from jax.experimental import pallas as pl
from jax.experimental.pallas import tpu as pltpu

F32 = jnp.float32
BF16 = jnp.bfloat16

D = 1024
B = 2
N_LAT = 8192
N_CTX = 256
DEPTH = 2
GRID_W = 64
S5_W = 512
S5_GC = 16
S5_G = 32
S5_P = 64
SG_W = 512
SG_HEADS = 8
SG_CHUNK = 128
N_HEADS = 16
N_KV = 4
HD = 64
WINDOW = 128
ATT_BLK = 128
ROPE_BASE = 10000.0
NEG_INF = -1e30
LN_EPS = 1e-5
ALPHA = (2 * DEPTH) ** 0.25
LOG2E = math.log2(math.e)
EVEN_IN = 2 * S5_W + 3 * SG_W
ODD_IN = 2 * N_HEADS * HD + 2 * N_KV * HD

LANES = 128
VMEM_LIMIT = 56 * 1024 * 1024

R_LAT = B * N_LAT
R_CTX = B * N_CTX
R_ALL = R_LAT + R_CTX
TN = 512
N_LAT_BLK = R_LAT // TN
N_BLK = R_ALL // TN
LAT_BLK_PER_BATCH = N_LAT // TN
CTX_MOD_ROW = B

S5_CHUNK = 8
S5_QT = S5_W // LANES
S5_GPT = LANES // S5_GC
S5_ST = S5_GPT * S5_P
S5_PAIRS = S5_GPT // 2
PAIR_BLK = 2 * S5_GC
LAT_CHUNKS = N_LAT // S5_CHUNK
CTX_CHUNKS = N_CTX // S5_CHUNK
BATCH_CHUNKS = CTX_CHUNKS + LAT_CHUNKS
N_SEG = 16
SEG_CHUNKS = BATCH_CHUNKS // N_SEG
BWD_RESET_STEP = LAT_CHUNKS - (N_SEG - 1) * SEG_CHUNKS


def _sigmoid(x):
    return 1.0 / (1.0 + jnp.exp(-x))


def _silu(x):
    return x * _sigmoid(x)


def _gelu_tanh(x):
    return 0.5 * x * (1.0 + jnp.tanh(math.sqrt(2.0 / math.pi) * (x + 0.044715 * (x * x * x))))


def _layer_norm(x, g, b):
    mu = jnp.mean(x, axis=-1, keepdims=True)
    xc = x - mu
    var = jnp.mean(xc * xc, axis=-1, keepdims=True)
    return xc * lax.rsqrt(var + LN_EPS) * g + b


def _split_bf16(x):
    hi = x.astype(BF16)
    return hi, (x - hi.astype(F32)).astype(BF16)


def _dot_split(x, y_split, dims):
    x_hi, x_lo = _split_bf16(x)
    y_hi, y_lo = y_split
    dot = lambda a, b: lax.dot_general(a, b, dims, preferred_element_type=F32)
    return dot(x_hi, y_hi) + (dot(x_hi, y_lo) + dot(x_lo, y_hi))


def _block_mod_row(i):
    return jnp.where(i < N_LAT_BLK, i // LAT_BLK_PER_BATCH, CTX_MOD_ROW)


MOD_W_PARTS = 4


def _mod_kernel(cv_ref, *refs):
    w_refs, (b_ref, o_ref) = refs[:MOD_W_PARTS], refs[MOD_W_PARTS:]
    s = _silu(cv_ref[...])
    rows = D // MOD_W_PARTS
    acc = b_ref[0]
    for p, w_ref in enumerate(w_refs):
        acc = acc + _dot_split(s[:, p * rows:(p + 1) * rows], _split_bf16(w_ref[0]),
                               (((1,), (0,)), ((), ())))
    o_ref[0] = acc


def _modulation(c, c_ctx, mod_w, mod_b):
    cv = jnp.concatenate([c, c_ctx[None], jnp.zeros((8 - B - 1, D), F32)], axis=0)
    tn = 1024
    w_specs = [pl.BlockSpec((1, D // MOD_W_PARTS, tn), lambda l, j, p=p: (l, p, j))
               for p in range(MOD_W_PARTS)]
    return pl.pallas_call(
        _mod_kernel,
        grid=(DEPTH, 3 * D // tn),
        in_specs=[
            pl.BlockSpec((8, D), lambda l, j: (0, 0)),
            *w_specs,
            pl.BlockSpec((1, 1, tn), lambda l, j: (l, 0, j)),
        ],
        out_specs=pl.BlockSpec((1, 8, tn), lambda l, j: (l, 0, j)),
        out_shape=jax.ShapeDtypeStruct((DEPTH, 8, 3 * D), F32),
        compiler_params=pltpu.CompilerParams(vmem_limit_bytes=VMEM_LIMIT),
        name="modulation",
    )(cv, *([mod_w] * MOD_W_PARTS), mod_b.reshape(DEPTH, 1, 3 * D))


def _block_transpose4(tiles):
    tiles = list(tiles)
    blk = lax.broadcasted_iota(jnp.int32, tiles[0].shape, 1) // PAIR_BLK
    for dist in (2, 1):
        keep = (blk & dist) == 0
        for i in range(len(tiles)):
            if i & dist:
                continue
            lo, hi = tiles[i], tiles[i + dist]
            tiles[i] = jnp.where(keep, lo, pltpu.roll(hi, dist * PAIR_BLK, axis=1))
            tiles[i + dist] = jnp.where(keep, pltpu.roll(lo, LANES - dist * PAIR_BLK, axis=1), hi)
    return tiles


def _in0_kernel(x_ref, ctx_ref, mod_ref, w_ref, vp_ref, rest_ref, xa_scr, wbf_scr):
    L = S5_CHUNK
    i = pl.program_id(0)

    @pl.when(i == 0)
    def _():
        wbf_scr[...] = w_ref[...].astype(BF16)

    z = jnp.where(i < N_LAT_BLK, x_ref[...], ctx_ref[...])
    m = mod_ref[pl.ds(_block_mod_row(i), 1), :]
    h = z * (1.0 + m[:, D:2 * D]) + m[:, :D]
    p = jnp.dot(h.astype(BF16), wbf_scr[...], preferred_element_type=F32)
    rest_ref[...] = p[:, S5_W:].astype(BF16)
    for q in range(S5_QT):
        xa_scr[q] = p[:, q * LANES:(q + 1) * LANES]
        tiles = [xa_scr[q, pl.ds(t, TN // L, stride=L), :] for t in range(L)]
        halves = [_block_transpose4(tiles[k * S5_PAIRS:(k + 1) * S5_PAIRS])
                  for k in range(L // S5_PAIRS)]
        for pr in range(S5_PAIRS):
            vp_ref[q * S5_PAIRS + pr] = jnp.concatenate(
                [half[pr] for half in halves], axis=-1).astype(BF16)


def _in0(x2, ctx2, mod, w_in):
    pair_w = S5_CHUNK * PAIR_BLK
    return pl.pallas_call(
        _in0_kernel,
        grid=(N_BLK,),
        in_specs=[
            pl.BlockSpec((TN, D), lambda i: (jnp.minimum(i, N_LAT_BLK - 1), 0)),
            pl.BlockSpec((R_CTX, D), lambda i: (0, 0)),
            pl.BlockSpec((8, 3 * D), lambda i: (0, 0)),
            pl.BlockSpec((D, EVEN_IN), lambda i: (0, 0)),
        ],
        out_specs=[
            pl.BlockSpec((S5_QT * S5_PAIRS, TN // S5_CHUNK, pair_w), lambda i: (0, i, 0)),
            pl.BlockSpec((TN, EVEN_IN - S5_W), lambda i: (i, 0)),
        ],
        out_shape=[
            jax.ShapeDtypeStruct((S5_QT * S5_PAIRS, R_ALL // S5_CHUNK, pair_w), BF16),
            jax.ShapeDtypeStruct((R_ALL, EVEN_IN - S5_W), BF16),
        ],
        scratch_shapes=[pltpu.VMEM((S5_QT, TN, LANES), F32), pltpu.VMEM((D, EVEN_IN), BF16)],
        compiler_params=pltpu.CompilerParams(vmem_limit_bytes=VMEM_LIMIT),
        name="in0",
    )(x2, ctx2, mod, w_in)


def _s5_prep_kernel(rows_ref, bre_ref, bim_ref, cre_ref, cim_ref, dsk_ref, t_ref, e_ref, ft_ref,
                    alr_ref, ali_ref):
    L = S5_CHUNK
    tile = (LANES, S5_ST)
    same_group = (lax.broadcasted_iota(jnp.int32, tile, 0) // S5_GC
                  == lax.broadcasted_iota(jnp.int32, tile, 1) // S5_P)
    nt_dims = (((1,), (1,)), ((), ()))
    nn_dims = (((1,), (0,)), ((), ()))
    pick_ch = (lax.broadcasted_iota(jnp.int32, (LANES, S5_GC), 0) % S5_GC
               == lax.broadcasted_iota(jnp.int32, (LANES, S5_GC), 1)).astype(BF16)
    rep_st = (lax.broadcasted_iota(jnp.int32, (S5_P, S5_ST), 0)
              == lax.broadcasted_iota(jnp.int32, (S5_P, S5_ST), 1) % S5_P).astype(BF16)

    def spread_b(b):
        hi, lo = _split_bf16(b)
        dot = lambda v: lax.dot_general(pick_ch, v, nt_dims, preferred_element_type=F32)
        return jnp.where(same_group, dot(hi) + dot(lo), 0.0)

    def spread_c(c):
        hi, lo = _split_bf16(c)
        dot = lambda v: lax.dot_general(v, rep_st, nn_dims, preferred_element_type=F32)
        return jnp.where(same_group, dot(hi) + dot(lo), 0.0)

    pair_rows = lambda pr: slice(pr * PAIR_BLK, (pr + 1) * PAIR_BLK)
    pair_lanes = lambda pr: slice(pr * LANES, (pr + 1) * LANES)
    lag_strip = []
    for d in range(2):
        lam_re = rows_ref[d, 0]
        lam_im = rows_ref[d, 1]
        dt = jnp.exp(rows_ref[d, 2])
        pw = []
        for k in range(L + 1):
            mag = jnp.exp(lam_re * dt * float(k))
            ang = lam_im * dt * float(k)
            pw.append((mag * jnp.cos(ang), mag * jnp.sin(ang)))
        alr_ref[d] = pw[L][0]
        ali_ref[d] = pw[L][1]
        den = lam_re * lam_re + lam_im * lam_im
        nr = pw[1][0] - 1.0
        ni = pw[1][1]
        f_re = (nr * lam_re + ni * lam_im) / den
        f_im = (ni * lam_re - nr * lam_im) / den
        b_re = spread_b(bre_ref[d])
        b_im = spread_b(bim_ref[d])
        bb_re = f_re * b_re - f_im * b_im
        bb_im = f_re * b_im + f_im * b_re
        c_re = spread_c(cre_ref[d])
        c_im = spread_c(cim_ref[d])
        col_re = slice(2 * d * LANES, (2 * d + 1) * LANES)
        col_im = slice((2 * d + 1) * LANES, (2 * d + 2) * LANES)
        strips = []
        for pr in range(S5_PAIRS):
            blk = lambda a: a[pair_rows(pr), pair_lanes(pr)]
            p_re = [blk(jnp.broadcast_to(pw[k][0], tile)) for k in range(L + 1)]
            p_im = [blk(jnp.broadcast_to(pw[k][1], tile)) for k in range(L + 1)]
            bbr, bbi, cr, ci = blk(bb_re), blk(bb_im), blk(c_re), blk(c_im)
            for k in range(L):
                s = (L - 1 - k) if d == 0 else k
                rows = slice(s * PAIR_BLK, (s + 1) * PAIR_BLK)
                e_ref[0, pr, rows, col_re] = (p_re[k] * bbr - p_im[k] * bbi).astype(BF16)
                e_ref[0, pr, rows, col_im] = (p_re[k] * bbi + p_im[k] * bbr).astype(BF16)
            cw = [(cr * p_re[k] - ci * p_im[k], -(cr * p_im[k] + ci * p_re[k]))
                  for k in range(L + 1)]
            for t in range(L):
                k = (t + 1) if d == 0 else (L - t)
                rows = slice(t * PAIR_BLK, (t + 1) * PAIR_BLK)
                ft_ref[0, pr, rows, col_re] = cw[k][0].astype(BF16)
                ft_ref[0, pr, rows, col_im] = cw[k][1].astype(BF16)
            lags = range(L) if d == 0 else range(L - 1, -1, -1)
            c_stack = jnp.concatenate(
                [jnp.concatenate([cw[k][0], cw[k][1]], axis=-1) for k in lags], axis=0)
            strips.append(_dot_split(jnp.concatenate([bbr, bbi], axis=-1),
                                     _split_bf16(c_stack), nt_dims))
        lag_strip.append(strips)

    strip_w = 2 * L * PAIR_BLK
    mid = (L - 1) * PAIR_BLK
    widen = lambda a: jnp.concatenate([a, jnp.zeros((a.shape[0], strip_w - a.shape[1]), F32)], axis=-1)
    row = lax.broadcasted_iota(jnp.int32, (PAIR_BLK, strip_w), 0)
    lane = lax.broadcasted_iota(jnp.int32, (PAIR_BLK, strip_w), 1)
    d_row = widen(dsk_ref[0])
    for pr in range(S5_PAIRS):
        strip = (widen(lag_strip[1][pr])
                 + pltpu.roll(widen(lag_strip[0][pr]), mid, axis=1)
                 + jnp.where(lane == row + mid,
                             pltpu.roll(d_row, mid - pr * PAIR_BLK, axis=1), 0.0))
        for s in range(L):
            shift = (strip_w - (L - 1 - s) * PAIR_BLK) % strip_w
            blk = pltpu.roll(strip, shift, axis=1) if shift else strip
            t_ref[0, pr, s * PAIR_BLK:(s + 1) * PAIR_BLK, :] = blk[:, :L * PAIR_BLK].astype(BF16)


def _s5_prep(lam_re, lam_im, log_dt, b_re, b_im, c_re, c_im, d_skip):
    L = S5_CHUNK
    n_st = S5_G * S5_P
    pair_w = L * PAIR_BLK
    rows = jnp.stack([lam_re.reshape(2, 1, n_st), lam_im.reshape(2, 1, n_st),
                      jnp.repeat(log_dt, S5_P, axis=1).reshape(2, 1, n_st)], axis=1)
    b_spec = pl.BlockSpec((2, S5_ST, S5_GC), lambda q: (0, q, 0))
    c_spec = pl.BlockSpec((2, LANES, S5_P), lambda q: (0, q, 0))
    return pl.pallas_call(
        _s5_prep_kernel,
        grid=(S5_QT,),
        in_specs=[
            pl.BlockSpec((2, 3, 1, S5_ST), lambda q: (0, 0, 0, q)),
            b_spec, b_spec, c_spec, c_spec,
            pl.BlockSpec((1, 1, LANES), lambda q: (q, 0, 0)),
        ],
        out_specs=[
            pl.BlockSpec((1, S5_PAIRS, pair_w, pair_w), lambda q: (q, 0, 0, 0)),
            pl.BlockSpec((1, S5_PAIRS, pair_w, 4 * LANES), lambda q: (q, 0, 0, 0)),
            pl.BlockSpec((1, S5_PAIRS, pair_w, 4 * LANES), lambda q: (q, 0, 0, 0)),
            pl.BlockSpec((2, 1, S5_ST), lambda q: (q, 0, 0)),
            pl.BlockSpec((2, 1, S5_ST), lambda q: (q, 0, 0)),
        ],
        out_shape=[
            jax.ShapeDtypeStruct((S5_QT, S5_PAIRS, pair_w, pair_w), BF16),
            jax.ShapeDtypeStruct((S5_QT, S5_PAIRS, pair_w, 4 * LANES), BF16),
            jax.ShapeDtypeStruct((S5_QT, S5_PAIRS, pair_w, 4 * LANES), BF16),
            jax.ShapeDtypeStruct((2 * S5_QT, 1, S5_ST), F32),
            jax.ShapeDtypeStruct((2 * S5_QT, 1, S5_ST), F32),
        ],
        compiler_params=pltpu.CompilerParams(vmem_limit_bytes=VMEM_LIMIT),
        name="s5_prep",
    )(rows, b_re.reshape(2, n_st, S5_GC), b_im.reshape(2, n_st, S5_GC),
      c_re.reshape(2, S5_W, S5_P), c_im.reshape(2, S5_W, S5_P), d_skip.reshape(S5_QT, 1, LANES))


N_SLAB = S5_ST // LANES


def _cmul(ar, ai, br, bi):
    return ar * br - ai * bi, ar * bi + ai * br


def _cpow(ar, ai, n):
    res = None
    while n:
        if n & 1:
            res = (ar, ai) if res is None else _cmul(res[0], res[1], ar, ai)
        n >>= 1
        if n:
            ar, ai = _cmul(ar, ai, ar, ai)
    return res


def _scan_boundary_states(loc_ref, out_ref, ar_ref, ai_ref, backward):
    a_row = [(ar_ref[0][:, k * LANES:(k + 1) * LANES], ai_ref[0][:, k * LANES:(k + 1) * LANES])
             for k in range(N_SLAB)]
    a_tile = [(jnp.broadcast_to(r, (N_SEG, LANES)), jnp.broadcast_to(i, (N_SEG, LANES)))
              for r, i in a_row]

    def sweep(lo, hi, carry, store):
        def step(i, carry):
            off = (SEG_CHUNKS - 1 - i) if backward else i
            rows = pl.ds(off, N_SEG, stride=SEG_CHUNKS)
            new = []
            for k, (cr, ci) in enumerate(carry):
                xr = loc_ref[k, rows, :]
                xi = loc_ref[N_SLAB + k, rows, :]
                if store:
                    out_ref[k, rows, :] = cr
                    out_ref[N_SLAB + k, rows, :] = ci
                nr, ni = _cmul(a_tile[k][0], a_tile[k][1], cr, ci)
                new.append((nr + xr, ni + xi))
            return tuple(new)
        return lax.fori_loop(lo, hi, step, carry)

    def full(carry, store):
        if not backward:
            return sweep(0, SEG_CHUNKS, carry, store)
        carry = sweep(0, BWD_RESET_STEP, carry, store)
        first = lax.broadcasted_iota(jnp.int32, (N_SEG, LANES), 0) == 0
        carry = tuple((jnp.where(first, 0.0, cr), jnp.where(first, 0.0, ci)) for cr, ci in carry)
        return sweep(BWD_RESET_STEP, SEG_CHUNKS, carry, store)

    zero_tile = jnp.zeros((N_SEG, LANES), F32)
    ends = full(tuple((zero_tile, zero_tile) for _ in range(N_SLAB)), False)

    starts = []
    for k, (er, ei) in enumerate(ends):
        pr, pi = _cpow(a_row[k][0], a_row[k][1], SEG_CHUNKS)
        if backward:
            order = range(N_SEG - 1, -1, -1)
            g = (er[0:1], ei[0:1])
        else:
            order = range(N_SEG)
            g = (jnp.zeros((1, LANES), F32), jnp.zeros((1, LANES), F32))
        rows_r = [None] * N_SEG
        rows_i = [None] * N_SEG
        for j in order:
            rows_r[j], rows_i[j] = g
            nr, ni = _cmul(pr, pi, g[0], g[1])
            g = (nr + er[j:j + 1], ni + ei[j:j + 1])
        starts.append((jnp.concatenate(rows_r, axis=0), jnp.concatenate(rows_i, axis=0)))
    full(tuple(starts), True)


def _s5_out_kernel(vl_ref, vc_ref, ar_ref, ai_ref, e_ref, t_ref, ft_ref, yl_ref, yc_ref, st_ref):
    n_dir = 2 * N_SLAB
    rows = [jnp.concatenate([vc_ref[pr], vl_ref[pr]], axis=0) for pr in range(S5_PAIRS)]
    for pr in range(S5_PAIRS):
        st = jnp.dot(rows[pr], e_ref[0, pr], preferred_element_type=F32)
        for comp in range(4):
            st_ref[comp * N_SLAB + pr] = st[:, comp * LANES:(comp + 1) * LANES]
    for d in range(2):
        st_d = st_ref.at[pl.ds(d * n_dir, n_dir)]
        _scan_boundary_states(st_d, st_d, ar_ref.at[pl.ds(d, 1)], ai_ref.at[pl.ds(d, 1)], d == 1)

    for pr in range(S5_PAIRS):
        st = jnp.concatenate([st_ref[comp * N_SLAB + pr] for comp in range(4)],
                             axis=-1).astype(BF16)
        y = (jnp.dot(rows[pr], t_ref[0, pr], preferred_element_type=F32)
             + lax.dot_general(st, ft_ref[0, pr], (((1,), (1,)), ((), ())),
                               preferred_element_type=F32))
        yc_ref[pr] = y[:CTX_CHUNKS]
        yl_ref[pr] = y[CTX_CHUNKS:]


def _s5_out(vp, al_re, al_im, e_mat, t_mat, ft_mat):
    pair_w = S5_CHUNK * PAIR_BLK
    ctx_blk0 = (R_LAT // S5_CHUNK) // CTX_CHUNKS
    return pl.pallas_call(
        _s5_out_kernel,
        grid=(S5_QT, B),
        in_specs=[
            pl.BlockSpec((S5_PAIRS, LAT_CHUNKS, pair_w), lambda q, b: (q, b, 0)),
            pl.BlockSpec((S5_PAIRS, CTX_CHUNKS, pair_w), lambda q, b: (q, ctx_blk0 + b, 0)),
            pl.BlockSpec((2, 1, S5_ST), lambda q, b: (q, 0, 0)),
            pl.BlockSpec((2, 1, S5_ST), lambda q, b: (q, 0, 0)),
            pl.BlockSpec((1, S5_PAIRS, pair_w, 4 * LANES), lambda q, b: (q, 0, 0, 0)),
            pl.BlockSpec((1, S5_PAIRS, pair_w, pair_w), lambda q, b: (q, 0, 0, 0)),
            pl.BlockSpec((1, S5_PAIRS, pair_w, 4 * LANES), lambda q, b: (q, 0, 0, 0)),
        ],
        out_specs=[
            pl.BlockSpec((S5_PAIRS, LAT_CHUNKS, pair_w), lambda q, b: (q, b, 0)),
            pl.BlockSpec((S5_PAIRS, CTX_CHUNKS, pair_w), lambda q, b: (q, b, 0)),
        ],
        out_shape=[
            jax.ShapeDtypeStruct((S5_QT * S5_PAIRS, B * LAT_CHUNKS, pair_w), F32),
            jax.ShapeDtypeStruct((S5_QT * S5_PAIRS, B * CTX_CHUNKS, pair_w), F32),
        ],
        scratch_shapes=[pltpu.VMEM((4 * N_SLAB, BATCH_CHUNKS, LANES), F32)],
        compiler_params=pltpu.CompilerParams(vmem_limit_bytes=VMEM_LIMIT),
        name="s5_out",
    )(vp, vp, al_re, al_im, e_mat, t_mat, ft_mat)


def _tail0_stages(z, ys, rest_ref, gate, gluw_ref, glub_ref, sg_g_ref, sg_b_ref, sgw_ref,
                  sgbias_ref, wout_ref, lng_ref, lnb_ref):
    ga = rest_ref[:, 0:S5_W].astype(F32)
    u = rest_ref[:, S5_W:2 * S5_W].astype(F32)
    v = rest_ref[:, 2 * S5_W:3 * S5_W].astype(F32)
    gb = rest_ref[:, 3 * S5_W:4 * S5_W].astype(F32)

    ya = _gelu_tanh(ys)
    glu = jnp.dot(ya.astype(BF16), gluw_ref[...], preferred_element_type=F32) + glub_ref[...]
    yield None
    ya = ya * _sigmoid(glu) * _silu(ga)
    yield None

    vn = _layer_norm(v, sg_g_ref[...], sg_b_ref[...]).astype(BF16)
    lane = lax.broadcasted_iota(jnp.int32, (SG_CHUNK, LANES), 1)
    first_head = lane < (SG_W // SG_HEADS)
    chunks = []
    for c in range(TN // SG_CHUNK):
        tiles = []
        for j in range(SG_W // LANES):
            vt = vn[c * SG_CHUNK:(c + 1) * SG_CHUNK, j * LANES:(j + 1) * LANES]
            zero = jnp.zeros_like(vt)
            v2 = jnp.concatenate([jnp.where(first_head, vt, zero), jnp.where(first_head, zero, vt)],
                                 axis=0)
            tiles.append(jnp.dot(sgw_ref[j], v2, preferred_element_type=F32))
        chunks.append(jnp.concatenate(tiles, axis=-1) + sgbias_ref[...])
    s = jnp.concatenate(chunks, axis=0)
    yb = u * s * _silu(gb)
    yield None

    mix = jnp.concatenate([ya, yb], axis=-1).astype(BF16)
    y = jnp.dot(mix, wout_ref[...], preferred_element_type=F32) * gate
    yield None
    yield _layer_norm(ALPHA * z + y, lng_ref[...], lnb_ref[...])


def _rope_tables():
    nf = HD // 4
    n_rows = N_LAT // GRID_W
    lane = jnp.arange(LANES)
    inv = ROPE_BASE ** (-(lane % nf).astype(F32) / nf)
    by_row = ((lane % HD) // (HD // 2) == 0)[None, :]
    sign = jnp.where((lane % (HD // 2)) < nf, -1.0, 1.0)[None, :]
    row_ang = jnp.arange(n_rows, dtype=F32)[:, None] * inv[None, :]
    col_ang = jnp.arange(GRID_W, dtype=F32)[:, None] * inv[None, :]
    zero = jnp.zeros((), F32)
    row_tab = jnp.stack([jnp.where(by_row, jnp.cos(row_ang), zero),
                         jnp.where(by_row, sign * jnp.sin(row_ang), zero)])
    col_tab = jnp.stack([jnp.where(by_row, zero, jnp.cos(col_ang)),
                         jnp.where(by_row, zero, sign * jnp.sin(col_ang))])
    return row_tab, col_tab


def _rope_tile(x, cos, sin, first_half):
    nf = HD // 4
    partner = jnp.where(first_half, pltpu.roll(x, LANES - nf, axis=1), pltpu.roll(x, nf, axis=1))
    return x * cos + partner * sin


def _mid_kernel(x_ref, ctx_ref, mod0_ref, mod1_ref, ysl_ref, ysc_ref, rest_ref, gluw_ref, glub_ref,
                sg_g_ref, sg_b_ref, sgw_ref, sgbias_ref, wout_ref, lng_ref, lnb_ref, w_ref,
                rtab_ref, ctab_ref, z1_ref, q_ref, k_ref, vt_ref, g_ref, zprev_ref, wbf_scr,
                wout_scr, ys_scr):
    i = pl.program_id(0)
    dq = N_HEADS * HD
    dkv = N_KV * HD

    @pl.when(i == 0)
    def _():
        zprev_ref[...] = jnp.zeros((TN, D), F32)
        wbf_scr[...] = w_ref[...].astype(BF16)
        wout_scr[...] = wout_ref[...].astype(BF16)

    j = jnp.maximum(i - 1, 0)
    m1 = mod1_ref[pl.ds(_block_mod_row(j), 1), :]
    h = (zprev_ref[...] * (1.0 + m1[:, D:2 * D]) + m1[:, :D]).astype(BF16)
    grid_row0 = (j % LAT_BLK_PER_BATCH) * (TN // GRID_W)
    table = lambda t: jnp.concatenate(
        [rtab_ref[t, pl.ds(grid_row0 + r, 1), :] + ctab_ref[t] for r in range(TN // GRID_W)],
        axis=0)
    cos = jnp.where(j < N_LAT_BLK, table(0), 1.0)
    sin = jnp.where(j < N_LAT_BLK, table(1), 0.0)
    lane = lax.broadcasted_iota(jnp.int32, (TN, LANES), 1)
    first_half = (lane % (HD // 2)) < (HD // 4)
    scale = HD ** -0.5 * LOG2E

    def project(c0, c1):
        return jnp.dot(h, wbf_scr[:, c0:c1], preferred_element_type=F32)

    def roped_tiles(p, mult):
        for c in range(p.shape[1] // LANES):
            r = _rope_tile(p[:, c * LANES:(c + 1) * LANES], cos, sin, first_half)
            yield c, (r * mult if mult != 1.0 else r).astype(BF16)

    t = jnp.minimum(i, N_BLK - 1)
    z = jnp.where(t < N_LAT_BLK, x_ref[...], ctx_ref[...])
    yp = jnp.where(t < N_LAT_BLK, ysl_ref[...], ysc_ref[...])
    for qt in range(S5_QT):
        for k in range(S5_CHUNK // S5_PAIRS):
            tiles = _block_transpose4([yp[qt * S5_PAIRS + pr][:, k * LANES:(k + 1) * LANES]
                                       for pr in range(S5_PAIRS)])
            for n, tile in enumerate(tiles):
                ys_scr[qt, pl.ds(k * S5_PAIRS + n, TN // S5_CHUNK, stride=S5_CHUNK), :] = tile
    ys = jnp.concatenate([ys_scr[qt] for qt in range(S5_QT)], axis=-1)
    gate = mod0_ref[pl.ds(_block_mod_row(t), 1), 2 * D:3 * D]
    tail = _tail0_stages(z, ys, rest_ref, gate, gluw_ref, glub_ref, sg_g_ref, sg_b_ref, sgw_ref,
                         sgbias_ref, wout_scr, lng_ref, lnb_ref)

    half_q = dq // 2
    p_q0 = project(0, half_q)
    next(tail)
    p_q1 = project(half_q, dq)
    next(tail)
    p_k = project(dq, dq + dkv)
    vt_ref[...] = lax.dot_general(wbf_scr[:, dq + dkv:dq + 2 * dkv], h, (((0,), (1,)), ((), ())),
                                  preferred_element_type=F32).astype(BF16)
    next(tail)
    next(tail)
    for c0, p in ((0, p_q0), (half_q, p_q1)):
        for c, r in roped_tiles(p, scale):
            q_ref[:, c0 + c * LANES:c0 + (c + 1) * LANES] = r
    for c, r in roped_tiles(p_k, 1.0):
        k_ref[:, c * LANES:(c + 1) * LANES] = r
    p_g = project(dq + 2 * dkv, ODD_IN)
    z1 = next(tail)
    z1_ref[...] = z1
    zprev_ref[...] = z1
    g_ref[...] = p_g.astype(BF16)


def _mid(x2, ctx2, mod0, mod1, ys_lat, ys_ctx, rest, glu_w, glu_b, sg_g, sg_b, sg_w, sg_bias,
         w_out, ln_g, ln_b, w_in1, row_tab, col_tab):
    dq = N_HEADS * HD
    dkv = N_KV * HD
    once = dict(pipeline_mode=pl.Buffered(1))
    row = lambda n: pl.BlockSpec((1, n), lambda i: (0, 0))
    lat = lambda i: (jnp.minimum(i, N_LAT_BLK - 1), 0)
    tail_blk = lambda i: (jnp.minimum(i, N_BLK - 1), 0)
    proj_blk = lambda i: (jnp.maximum(i - 1, 0), 0)
    return pl.pallas_call(
        _mid_kernel,
        grid=(N_BLK + 1,),
        in_specs=[
            pl.BlockSpec((TN, D), lat),
            pl.BlockSpec((R_CTX, D), lambda i: (0, 0), **once),
            pl.BlockSpec((8, 3 * D), lambda i: (0, 0)),
            pl.BlockSpec((8, 3 * D), lambda i: (0, 0)),
            pl.BlockSpec((S5_QT * S5_PAIRS, TN // S5_CHUNK, S5_CHUNK * PAIR_BLK),
                         lambda i: (0, jnp.minimum(i, N_LAT_BLK - 1), 0)),
            pl.BlockSpec((S5_QT * S5_PAIRS, R_CTX // S5_CHUNK, S5_CHUNK * PAIR_BLK),
                         lambda i: (0, 0, 0), **once),
            pl.BlockSpec((TN, EVEN_IN - S5_W), tail_blk),
            pl.BlockSpec((S5_W, S5_W), lambda i: (0, 0), **once),
            row(S5_W), row(SG_W), row(SG_W),
            pl.BlockSpec((SG_HEADS // 2, SG_CHUNK, 2 * SG_CHUNK), lambda i: (0, 0, 0)),
            pl.BlockSpec((SG_CHUNK, SG_W), lambda i: (0, 0)),
            pl.BlockSpec((S5_W + SG_W, D), lambda i: (0, 0), **once),
            row(D), row(D),
            pl.BlockSpec((D, ODD_IN), lambda i: (0, 0), **once),
            pl.BlockSpec((2, N_LAT // GRID_W, LANES), lambda i: (0, 0, 0)),
            pl.BlockSpec((2, GRID_W, LANES), lambda i: (0, 0, 0)),
        ],
        out_specs=[
            pl.BlockSpec((TN, D), tail_blk),
            pl.BlockSpec((TN, dq), proj_blk),
            pl.BlockSpec((TN, dkv), proj_blk),
            pl.BlockSpec((dkv, TN), lambda i: (0, jnp.maximum(i - 1, 0))),
            pl.BlockSpec((TN, dq), proj_blk),
        ],
        out_shape=[
            jax.ShapeDtypeStruct((R_ALL, D), F32),
            jax.ShapeDtypeStruct((R_ALL, dq), BF16),
            jax.ShapeDtypeStruct((R_ALL, dkv), BF16),
            jax.ShapeDtypeStruct((dkv, R_ALL), BF16),
            jax.ShapeDtypeStruct((R_ALL, dq), BF16),
        ],
        scratch_shapes=[pltpu.VMEM((TN, D), F32), pltpu.VMEM((D, ODD_IN), BF16),
                        pltpu.VMEM((S5_W + SG_W, D), BF16), pltpu.VMEM((S5_QT, TN, LANES), F32)],
        compiler_params=pltpu.CompilerParams(vmem_limit_bytes=VMEM_LIMIT),
        name="mid",
    )(x2, ctx2, mod0, mod1, ys_lat, ys_ctx, rest, glu_w, glu_b, sg_g, sg_b, sg_w, sg_bias, w_out,
      ln_g, ln_b, w_in1, row_tab, col_tab)


N_QBLK = N_LAT // ATT_BLK
GRP = N_HEADS // N_KV
ATT_SUB = 4
ATT_ROWS = ATT_SUB * ATT_BLK
N_QSTEP = N_QBLK // ATT_SUB
N_ATT_STEPS = B * N_QSTEP
assert ATT_ROWS == TN
assert WINDOW == ATT_BLK
ONES_ROWS = 16


def _attn_kernel(sink_ref, q_ref, kp_ref, kc_ref, kn_ref, kx_ref, vp_ref, vc_ref, vn_ref, vx_ref,
                 g_ref, z_ref, mod_ref, wout_ref, lng_ref, lnb_ref, out_ref, o_scr, wout_scr):
    step = pl.program_id(0)

    @pl.when(step == 0)
    def _():
        o_scr[...] = jnp.zeros((ATT_ROWS, N_HEADS * HD), BF16)
        wout_scr[...] = wout_ref[...].astype(BF16)

    t = jnp.maximum(step - 1, 0)
    gate = mod_ref[pl.ds(_block_mod_row(t), 1), 2 * D:3 * D]
    mix = (o_scr[...].astype(F32) * _silu(g_ref[...].astype(F32))).astype(BF16)
    y = jnp.dot(mix, wout_scr[...], preferred_element_type=F32) * gate
    out_ref[...] = _layer_norm(ALPHA * z_ref[...] + y, lng_ref[...], lnb_ref[...])

    i = jnp.minimum(step, N_ATT_STEPS - 1) % N_QSTEP
    half = ATT_BLK // 2
    n_win = 2 * ATT_BLK + half
    n_keys = n_win + N_CTX
    nq = GRP * half
    k_win = jnp.concatenate([kp_ref[...], kc_ref[...], kn_ref[...]], axis=0)
    vt_win = jnp.concatenate([vp_ref[...], vc_ref[...], vn_ref[...]], axis=1)
    k_ctx = kx_ref[...]
    vt_ctx = vx_ref[...]

    kpos = lax.broadcasted_iota(jnp.int32, (half, half), 0)
    qpos = lax.broadcasted_iota(jnp.int32, (half, half), 1)
    tile_q = lambda a: jnp.concatenate([a] * GRP, axis=1)
    neg = lambda ok: tile_q(jnp.where(ok, 0.0, NEG_INF))
    everyone = kpos >= 0
    qgrp = lax.broadcasted_iota(jnp.int32, (1, nq), 1) // half
    ones = jnp.ones((ONES_ROWS, n_keys), BF16)
    q = q_ref[...]

    halves = [(u, v) for u in range(ATT_SUB) for v in range(2)]
    units = [(n, h) for n in range(len(halves)) for h in range(N_KV)]
    k_all, vt_all, bias = [], [], []
    for u, v in halves:
        first = u * ATT_BLK + v * half
        k_all.append(jnp.concatenate([k_win[first:first + n_win], k_ctx], axis=0))
        vt_all.append(jnp.concatenate([vt_win[:, first:first + n_win], vt_ctx], axis=1))
        in_prev = [blk for blk in range(5) if first + blk * half < ATT_BLK]
        in_next = [blk for blk in range(5) if first + blk * half >= ATT_BLK + ATT_ROWS]
        blocks = {}
        for blk in sorted(set([0, 4] + in_prev + in_next)):
            ok = (kpos >= qpos) if blk == 0 else ((kpos <= qpos) if blk == 4 else everyone)
            if blk in in_prev:
                ok = ok & (i > 0)
            if blk in in_next:
                ok = ok & (i < N_QSTEP - 1)
            blocks[blk] = neg(ok)
        bias.append(blocks)
    scores = []
    for n, h in units:
        u, v = halves[n]
        r0 = u * ATT_BLK + v * half
        kh = k_all[n][:, h * HD:(h + 1) * HD]
        qh = jnp.concatenate(
            [q[r0:r0 + half, (h * GRP + g) * HD:(h * GRP + g + 1) * HD] for g in range(GRP)],
            axis=0)
        scores.append(lax.dot_general(kh, qh, (((1,), (1,)), ((), ())),
                                      preferred_element_type=F32))
    probs = []
    for (n, h), s in zip(units, scores):
        s = jnp.concatenate(
            [s[blk * half:(blk + 1) * half] + bias[n][blk] if blk in bias[n]
             else s[blk * half:(blk + 1) * half] for blk in range(5)] + [s[n_win:]], axis=0)
        sink = jnp.zeros((1, nq), F32)
        for g in range(GRP):
            sink = jnp.where(qgrp == g, sink_ref[h * GRP + g] * LOG2E, sink)
        m = jnp.maximum(jnp.max(s, axis=0, keepdims=True), sink)
        probs.append((jnp.exp2(s - m).astype(BF16), jnp.exp2(sink - m)))
    outs = [[None] * N_HEADS for _ in halves]
    for (n, h), (p, p_sink) in zip(units, probs):
        vt1 = jnp.concatenate([vt_all[n][h * HD:(h + 1) * HD], ones], axis=0)
        ov = jnp.dot(vt1, p, preferred_element_type=F32)
        o_t = ov[:HD] / (ov[HD:HD + 1] + p_sink)
        for g in range(GRP):
            outs[n][h * GRP + g] = o_t[:, g * half:(g + 1) * half]
    for u in range(ATT_SUB):
        o_t = jnp.concatenate(
            [jnp.concatenate([outs[2 * u][hh], outs[2 * u + 1][hh]], axis=1)
             for hh in range(N_HEADS)], axis=0)
        o_scr[u * ATT_BLK:(u + 1) * ATT_BLK, :] = jnp.transpose(o_t).astype(BF16)


def _attention_tail(sink, q, k, vt, g, z1, mod, w_out, ln_g, ln_b):
    dq = N_HEADS * HD
    dkv = N_KV * HD
    ctx_blk0 = R_LAT // N_CTX
    cur = lambda s: jnp.minimum(s, N_ATT_STEPS - 1)
    batch = lambda s: cur(s) // N_QSTEP
    qstep = lambda s: cur(s) % N_QSTEP
    prev_blk = lambda s: batch(s) * N_QBLK + jnp.maximum(ATT_SUB * qstep(s) - 1, 0)
    next_blk = lambda s: batch(s) * N_QBLK + jnp.minimum(ATT_SUB * (qstep(s) + 1), N_QBLK - 1)
    tail = lambda s: jnp.maximum(s - 1, 0)
    once = dict(pipeline_mode=pl.Buffered(1))
    row = lambda n: pl.BlockSpec((1, n), lambda s, sk: (0, 0))
    return pl.pallas_call(
        _attn_kernel,
        grid_spec=pltpu.PrefetchScalarGridSpec(
            num_scalar_prefetch=1,
            grid=(N_ATT_STEPS + 1,),
            in_specs=[
                pl.BlockSpec((ATT_ROWS, dq), lambda s, sk: (cur(s), 0)),
                pl.BlockSpec((ATT_BLK, dkv), lambda s, sk: (prev_blk(s), 0)),
                pl.BlockSpec((ATT_ROWS, dkv), lambda s, sk: (cur(s), 0)),
                pl.BlockSpec((ATT_BLK, dkv), lambda s, sk: (next_blk(s), 0)),
                pl.BlockSpec((N_CTX, dkv), lambda s, sk: (ctx_blk0 + batch(s), 0)),
                pl.BlockSpec((dkv, ATT_BLK), lambda s, sk: (0, prev_blk(s))),
                pl.BlockSpec((dkv, ATT_ROWS), lambda s, sk: (0, cur(s))),
                pl.BlockSpec((dkv, ATT_BLK), lambda s, sk: (0, next_blk(s))),
                pl.BlockSpec((dkv, N_CTX), lambda s, sk: (0, ctx_blk0 + batch(s))),
                pl.BlockSpec((ATT_ROWS, dq), lambda s, sk: (tail(s), 0)),
                pl.BlockSpec((ATT_ROWS, D), lambda s, sk: (tail(s), 0)),
                pl.BlockSpec((8, 3 * D), lambda s, sk: (0, 0)),
                pl.BlockSpec((dq, D), lambda s, sk: (0, 0), **once),
                row(D), row(D),
            ],
            out_specs=pl.BlockSpec((ATT_ROWS, D), lambda s, sk: (tail(s), 0)),
            scratch_shapes=[pltpu.VMEM((ATT_ROWS, dq), BF16), pltpu.VMEM((dq, D), BF16)],
        ),
        out_shape=jax.ShapeDtypeStruct((R_LAT, D), F32),
        compiler_params=pltpu.CompilerParams(vmem_limit_bytes=VMEM_LIMIT),
        name="attention",
    )(sink, q, k, k, k, k, vt, vt, vt, vt, g, z1, mod, w_out, ln_g, ln_b)


def kernel(x, c, ctx, c_ctx, mod_w, mod_b, ln_g, ln_b, e_w_in, e_w_out, s5_lam_re, s5_lam_im,
           s5_log_dt, s5_b_re, s5_b_im, s5_c_re, s5_c_im, s5_d, glu_w, glu_b, sg_ln_g, sg_ln_b,
           sg_w, sg_b, o_w_in, o_w_out, o_sink):
    x2 = x.reshape(R_LAT, D)
    ctx2 = ctx.reshape(R_CTX, D)
    mod = _modulation(c, c_ctx, mod_w, mod_b)

    vp, rest = _in0(x2, ctx2, mod[0], e_w_in[0])
    t_mat, e_mat, ft_mat, al_re, al_im = _s5_prep(
        s5_lam_re[0], s5_lam_im[0], s5_log_dt[0], s5_b_re[0], s5_b_im[0], s5_c_re[0], s5_c_im[0],
        s5_d[0])
    ys_lat, ys_ctx = _s5_out(vp, al_re, al_im, e_mat, t_mat, ft_mat)
    sg_bias = jnp.repeat(jnp.transpose(sg_b[0]), SG_W // SG_HEADS, axis=1)
    sg_w2 = jnp.transpose(sg_w[0].reshape(SG_HEADS // 2, 2, SG_CHUNK, SG_CHUNK), (0, 2, 1, 3))
    sg_w2 = sg_w2.reshape(SG_HEADS // 2, SG_CHUNK, 2 * SG_CHUNK).astype(BF16)
    row_tab, col_tab = _rope_tables()
    z1, q, k, vt, g = _mid(
        x2, ctx2, mod[0], mod[1], ys_lat, ys_ctx, rest, glu_w[0].astype(BF16),
        glu_b[0].reshape(1, S5_W), sg_ln_g[0].reshape(1, SG_W), sg_ln_b[0].reshape(1, SG_W),
        sg_w2, sg_bias, e_w_out[0], ln_g[0].reshape(1, D),
        ln_b[0].reshape(1, D), o_w_in[0], row_tab, col_tab)

    out = _attention_tail(o_sink[0], q, k, vt, g, z1, mod[1], o_w_out[0],
                          ln_g[1].reshape(1, D), ln_b[1].reshape(1, D))
    return out.reshape(B, N_LAT, D)
```

```python
import math

import jax
import jax.numpy as jnp
from jax import lax
from jax.experimental import pallas as pl
from jax.experimental.pallas import tpu as pltpu

F32 = jnp.float32
BF16 = jnp.bfloat16

D = 1024
B = 2
N_LAT = 8192
N_CTX = 256
DEPTH = 2
GRID_W = 64
S5_W = 512
S5_GC = 16
S5_G = 32
S5_P = 64
SG_W = 512
SG_HEADS = 8
SG_CHUNK = 128
N_HEADS = 16
N_KV = 4
HD = 64
WINDOW = 128
ATT_BLK = 128
ROPE_BASE = 10000.0
NEG_INF = -1e30
LN_EPS = 1e-5
ALPHA = (2 * DEPTH) ** 0.25
LOG2E = math.log2(math.e)
EVEN_IN = 2 * S5_W + 3 * SG_W
ODD_IN = 2 * N_HEADS * HD + 2 * N_KV * HD

LANES = 128
VMEM_LIMIT = 56 * 1024 * 1024

R_LAT = B * N_LAT
R_CTX = B * N_CTX
R_ALL = R_LAT + R_CTX
TN = 512
N_LAT_BLK = R_LAT // TN
N_BLK = R_ALL // TN
LAT_BLK_PER_BATCH = N_LAT // TN
CTX_MOD_ROW = B

S5_CHUNK = 8
S5_QT = S5_W // LANES
S5_GPT = LANES // S5_GC
S5_ST = S5_GPT * S5_P
S5_PAIRS = S5_GPT // 2
PAIR_BLK = 2 * S5_GC
LAT_CHUNKS = N_LAT // S5_CHUNK
CTX_CHUNKS = N_CTX // S5_CHUNK
BATCH_CHUNKS = CTX_CHUNKS + LAT_CHUNKS
N_SEG = 16
SEG_CHUNKS = BATCH_CHUNKS // N_SEG
BWD_RESET_STEP = LAT_CHUNKS - (N_SEG - 1) * SEG_CHUNKS


def _sigmoid(x):
    return 1.0 / (1.0 + jnp.exp(-x))


def _silu(x):
    return x * _sigmoid(x)


def _gelu_tanh(x):
    return 0.5 * x * (1.0 + jnp.tanh(math.sqrt(2.0 / math.pi) * (x + 0.044715 * (x * x * x))))


def _layer_norm(x, g, b):
    mu = jnp.mean(x, axis=-1, keepdims=True)
    xc = x - mu
    var = jnp.mean(xc * xc, axis=-1, keepdims=True)
    return xc * lax.rsqrt(var + LN_EPS) * g + b


def _split_bf16(x):
    hi = x.astype(BF16)
    return hi, (x - hi.astype(F32)).astype(BF16)


def _dot_split(x, y_split, dims):
    x_hi, x_lo = _split_bf16(x)
    y_hi, y_lo = y_split
    dot = lambda a, b: lax.dot_general(a, b, dims, preferred_element_type=F32)
    return dot(x_hi, y_hi) + (dot(x_hi, y_lo) + dot(x_lo, y_hi))


def _block_mod_row(i):
    return jnp.where(i < N_LAT_BLK, i // LAT_BLK_PER_BATCH, CTX_MOD_ROW)


def _mod_kernel(cv_ref, w_ref, b_ref, o_ref):
    s = _silu(cv_ref[...])
    o_ref[0] = _dot_split(s, _split_bf16(w_ref[0]), (((1,), (0,)), ((), ()))) + b_ref[0]


def _modulation(c, c_ctx, mod_w, mod_b):
    cv = jnp.concatenate([c, c_ctx[None], jnp.zeros((8 - B - 1, D), F32)], axis=0)
    tn = 1024
    return pl.pallas_call(
        _mod_kernel,
        grid=(DEPTH, 3 * D // tn),
        in_specs=[
            pl.BlockSpec((8, D), lambda l, j: (0, 0)),
            pl.BlockSpec((1, D, tn), lambda l, j: (l, 0, j)),
            pl.BlockSpec((1, 1, tn), lambda l, j: (l, 0, j)),
        ],
        out_specs=pl.BlockSpec((1, 8, tn), lambda l, j: (l, 0, j)),
        out_shape=jax.ShapeDtypeStruct((DEPTH, 8, 3 * D), F32),
        compiler_params=pltpu.CompilerParams(vmem_limit_bytes=VMEM_LIMIT),
        name="modulation",
    )(cv, mod_w, mod_b.reshape(DEPTH, 1, 3 * D))


def _block_transpose4(tiles):
    tiles = list(tiles)
    blk = lax.broadcasted_iota(jnp.int32, tiles[0].shape, 1) // PAIR_BLK
    for dist in (2, 1):
        keep = (blk & dist) == 0
        for i in range(len(tiles)):
            if i & dist:
                continue
            lo, hi = tiles[i], tiles[i + dist]
            tiles[i] = jnp.where(keep, lo, pltpu.roll(hi, dist * PAIR_BLK, axis=1))
            tiles[i + dist] = jnp.where(keep, pltpu.roll(lo, LANES - dist * PAIR_BLK, axis=1), hi)
    return tiles


def _in0_kernel(x_ref, ctx_ref, mod_ref, w_ref, vp_ref, rest_ref, xa_scr, wbf_scr):
    L = S5_CHUNK
    i = pl.program_id(0)

    @pl.when(i == 0)
    def _():
        wbf_scr[...] = w_ref[...].astype(BF16)

    z = jnp.where(i < N_LAT_BLK, x_ref[...], ctx_ref[...])
    m = mod_ref[pl.ds(_block_mod_row(i), 1), :]
    h = z * (1.0 + m[:, D:2 * D]) + m[:, :D]
    p = jnp.dot(h.astype(BF16), wbf_scr[...], preferred_element_type=F32)
    rest_ref[...] = p[:, S5_W:].astype(BF16)
    for q in range(S5_QT):
        xa_scr[q] = p[:, q * LANES:(q + 1) * LANES]
        tiles = [xa_scr[q, pl.ds(t, TN // L, stride=L), :] for t in range(L)]
        halves = [_block_transpose4(tiles[k * S5_PAIRS:(k + 1) * S5_PAIRS])
                  for k in range(L // S5_PAIRS)]
        for pr in range(S5_PAIRS):
            vp_ref[q * S5_PAIRS + pr] = jnp.concatenate(
                [half[pr] for half in halves], axis=-1).astype(BF16)


def _in0(x2, ctx2, mod, w_in):
    pair_w = S5_CHUNK * PAIR_BLK
    return pl.pallas_call(
        _in0_kernel,
        grid=(N_BLK,),
        in_specs=[
            pl.BlockSpec((TN, D), lambda i: (jnp.minimum(i, N_LAT_BLK - 1), 0)),
            pl.BlockSpec((R_CTX, D), lambda i: (0, 0)),
            pl.BlockSpec((8, 3 * D), lambda i: (0, 0)),
            pl.BlockSpec((D, EVEN_IN), lambda i: (0, 0)),
        ],
        out_specs=[
            pl.BlockSpec((S5_QT * S5_PAIRS, TN // S5_CHUNK, pair_w), lambda i: (0, i, 0)),
            pl.BlockSpec((TN, EVEN_IN - S5_W), lambda i: (i, 0)),
        ],
        out_shape=[
            jax.ShapeDtypeStruct((S5_QT * S5_PAIRS, R_ALL // S5_CHUNK, pair_w), BF16),
            jax.ShapeDtypeStruct((R_ALL, EVEN_IN - S5_W), BF16),
        ],
        scratch_shapes=[pltpu.VMEM((S5_QT, TN, LANES), F32), pltpu.VMEM((D, EVEN_IN), BF16)],
        compiler_params=pltpu.CompilerParams(vmem_limit_bytes=VMEM_LIMIT),
        name="in0",
    )(x2, ctx2, mod, w_in)


def _s5_prep_kernel(rows_ref, bre_ref, bim_ref, cre_ref, cim_ref, dsk_ref, t_ref, e_ref, ft_ref,
                    alr_ref, ali_ref):
    L = S5_CHUNK
    tile = (LANES, S5_ST)
    same_group = (lax.broadcasted_iota(jnp.int32, tile, 0) // S5_GC
                  == lax.broadcasted_iota(jnp.int32, tile, 1) // S5_P)
    nt_dims = (((1,), (1,)), ((), ()))
    nn_dims = (((1,), (0,)), ((), ()))
    pick_ch = (lax.broadcasted_iota(jnp.int32, (LANES, S5_GC), 0) % S5_GC
               == lax.broadcasted_iota(jnp.int32, (LANES, S5_GC), 1)).astype(BF16)
    rep_st = (lax.broadcasted_iota(jnp.int32, (S5_P, S5_ST), 0)
              == lax.broadcasted_iota(jnp.int32, (S5_P, S5_ST), 1) % S5_P).astype(BF16)

    def spread_b(b):
        hi, lo = _split_bf16(b)
        dot = lambda v: lax.dot_general(pick_ch, v, nt_dims, preferred_element_type=F32)
        return jnp.where(same_group, dot(hi) + dot(lo), 0.0)

    def spread_c(c):
        hi, lo = _split_bf16(c)
        dot = lambda v: lax.dot_general(v, rep_st, nn_dims, preferred_element_type=F32)
        return jnp.where(same_group, dot(hi) + dot(lo), 0.0)

    pair_rows = lambda pr: slice(pr * PAIR_BLK, (pr + 1) * PAIR_BLK)
    pair_lanes = lambda pr: slice(pr * LANES, (pr + 1) * LANES)
    lag_strip = []
    for d in range(2):
        lam_re = rows_ref[d, 0]
        lam_im = rows_ref[d, 1]
        dt = jnp.exp(rows_ref[d, 2])
        pw = []
        for k in range(L + 1):
            mag = jnp.exp(lam_re * dt * float(k))
            ang = lam_im * dt * float(k)
            pw.append((mag * jnp.cos(ang), mag * jnp.sin(ang)))
        alr_ref[d] = pw[L][0]
        ali_ref[d] = pw[L][1]
        den = lam_re * lam_re + lam_im * lam_im
        nr = pw[1][0] - 1.0
        ni = pw[1][1]
        f_re = (nr * lam_re + ni * lam_im) / den
        f_im = (ni * lam_re - nr * lam_im) / den
        b_re = spread_b(bre_ref[d])
        b_im = spread_b(bim_ref[d])
        bb_re = f_re * b_re - f_im * b_im
        bb_im = f_re * b_im + f_im * b_re
        c_re = spread_c(cre_ref[d])
        c_im = spread_c(cim_ref[d])
        col_re = slice(2 * d * LANES, (2 * d + 1) * LANES)
        col_im = slice((2 * d + 1) * LANES, (2 * d + 2) * LANES)
        strips = []
        for pr in range(S5_PAIRS):
            blk = lambda a: a[pair_rows(pr), pair_lanes(pr)]
            p_re = [blk(jnp.broadcast_to(pw[k][0], tile)) for k in range(L + 1)]
            p_im = [blk(jnp.broadcast_to(pw[k][1], tile)) for k in range(L + 1)]
            bbr, bbi, cr, ci = blk(bb_re), blk(bb_im), blk(c_re), blk(c_im)
            for k in range(L):
                s = (L - 1 - k) if d == 0 else k
                rows = slice(s * PAIR_BLK, (s + 1) * PAIR_BLK)
                e_ref[0, pr, rows, col_re] = (p_re[k] * bbr - p_im[k] * bbi).astype(BF16)
                e_ref[0, pr, rows, col_im] = (p_re[k] * bbi + p_im[k] * bbr).astype(BF16)
            cw = [(cr * p_re[k] - ci * p_im[k], -(cr * p_im[k] + ci * p_re[k]))
                  for k in range(L + 1)]
            for t in range(L):
                k = (t + 1) if d == 0 else (L - t)
                rows = slice(t * PAIR_BLK, (t + 1) * PAIR_BLK)
                ft_ref[0, pr, rows, col_re] = cw[k][0].astype(BF16)
                ft_ref[0, pr, rows, col_im] = cw[k][1].astype(BF16)
            lags = range(L) if d == 0 else range(L - 1, -1, -1)
            c_stack = jnp.concatenate(
                [jnp.concatenate([cw[k][0], cw[k][1]], axis=-1) for k in lags], axis=0)
            strips.append(_dot_split(jnp.concatenate([bbr, bbi], axis=-1),
                                     _split_bf16(c_stack), nt_dims))
        lag_strip.append(strips)

    strip_w = 2 * L * PAIR_BLK
    mid = (L - 1) * PAIR_BLK
    widen = lambda a: jnp.concatenate([a, jnp.zeros((a.shape[0], strip_w - a.shape[1]), F32)], axis=-1)
    row = lax.broadcasted_iota(jnp.int32, (PAIR_BLK, strip_w), 0)
    lane = lax.broadcasted_iota(jnp.int32, (PAIR_BLK, strip_w), 1)
    d_row = widen(dsk_ref[0])
    for pr in range(S5_PAIRS):
        strip = (widen(lag_strip[1][pr])
                 + pltpu.roll(widen(lag_strip[0][pr]), mid, axis=1)
                 + jnp.where(lane == row + mid,
                             pltpu.roll(d_row, mid - pr * PAIR_BLK, axis=1), 0.0))
        for s in range(L):
            shift = (strip_w - (L - 1 - s) * PAIR_BLK) % strip_w
            blk = pltpu.roll(strip, shift, axis=1) if shift else strip
            t_ref[0, pr, s * PAIR_BLK:(s + 1) * PAIR_BLK, :] = blk[:, :L * PAIR_BLK].astype(BF16)


def _s5_prep(lam_re, lam_im, log_dt, b_re, b_im, c_re, c_im, d_skip):
    L = S5_CHUNK
    n_st = S5_G * S5_P
    pair_w = L * PAIR_BLK
    rows = jnp.stack([lam_re.reshape(2, 1, n_st), lam_im.reshape(2, 1, n_st),
                      jnp.repeat(log_dt, S5_P, axis=1).reshape(2, 1, n_st)], axis=1)
    b_spec = pl.BlockSpec((2, S5_ST, S5_GC), lambda q: (0, q, 0))
    c_spec = pl.BlockSpec((2, LANES, S5_P), lambda q: (0, q, 0))
    return pl.pallas_call(
        _s5_prep_kernel,
        grid=(S5_QT,),
        in_specs=[
            pl.BlockSpec((2, 3, 1, S5_ST), lambda q: (0, 0, 0, q)),
            b_spec, b_spec, c_spec, c_spec,
            pl.BlockSpec((1, 1, LANES), lambda q: (q, 0, 0)),
        ],
        out_specs=[
            pl.BlockSpec((1, S5_PAIRS, pair_w, pair_w), lambda q: (q, 0, 0, 0)),
            pl.BlockSpec((1, S5_PAIRS, pair_w, 4 * LANES), lambda q: (q, 0, 0, 0)),
            pl.BlockSpec((1, S5_PAIRS, pair_w, 4 * LANES), lambda q: (q, 0, 0, 0)),
            pl.BlockSpec((2, 1, S5_ST), lambda q: (q, 0, 0)),
            pl.BlockSpec((2, 1, S5_ST), lambda q: (q, 0, 0)),
        ],
        out_shape=[
            jax.ShapeDtypeStruct((S5_QT, S5_PAIRS, pair_w, pair_w), BF16),
            jax.ShapeDtypeStruct((S5_QT, S5_PAIRS, pair_w, 4 * LANES), BF16),
            jax.ShapeDtypeStruct((S5_QT, S5_PAIRS, pair_w, 4 * LANES), BF16),
            jax.ShapeDtypeStruct((2 * S5_QT, 1, S5_ST), F32),
            jax.ShapeDtypeStruct((2 * S5_QT, 1, S5_ST), F32),
        ],
        compiler_params=pltpu.CompilerParams(vmem_limit_bytes=VMEM_LIMIT),
        name="s5_prep",
    )(rows, b_re.reshape(2, n_st, S5_GC), b_im.reshape(2, n_st, S5_GC),
      c_re.reshape(2, S5_W, S5_P), c_im.reshape(2, S5_W, S5_P), d_skip.reshape(S5_QT, 1, LANES))


N_SLAB = S5_ST // LANES


def _cmul(ar, ai, br, bi):
    return ar * br - ai * bi, ar * bi + ai * br


def _cpow(ar, ai, n):
    res = None
    while n:
        if n & 1:
            res = (ar, ai) if res is None else _cmul(res[0], res[1], ar, ai)
        n >>= 1
        if n:
            ar, ai = _cmul(ar, ai, ar, ai)
    return res


def _scan_boundary_states(loc_ref, out_ref, ar_ref, ai_ref, backward):
    a_row = [(ar_ref[0][:, k * LANES:(k + 1) * LANES], ai_ref[0][:, k * LANES:(k + 1) * LANES])
             for k in range(N_SLAB)]
    a_tile = [(jnp.broadcast_to(r, (N_SEG, LANES)), jnp.broadcast_to(i, (N_SEG, LANES)))
              for r, i in a_row]

    def sweep(lo, hi, carry, store):
        def step(i, carry):
            off = (SEG_CHUNKS - 1 - i) if backward else i
            rows = pl.ds(off, N_SEG, stride=SEG_CHUNKS)
            new = []
            for k, (cr, ci) in enumerate(carry):
                xr = loc_ref[k, rows, :]
                xi = loc_ref[N_SLAB + k, rows, :]
                if store:
                    out_ref[k, rows, :] = cr
                    out_ref[N_SLAB + k, rows, :] = ci
                nr, ni = _cmul(a_tile[k][0], a_tile[k][1], cr, ci)
                new.append((nr + xr, ni + xi))
            return tuple(new)
        return lax.fori_loop(lo, hi, step, carry, unroll=2)

    def full(carry, store):
        if not backward:
            return sweep(0, SEG_CHUNKS, carry, store)
        carry = sweep(0, BWD_RESET_STEP, carry, store)
        first = lax.broadcasted_iota(jnp.int32, (N_SEG, LANES), 0) == 0
        carry = tuple((jnp.where(first, 0.0, cr), jnp.where(first, 0.0, ci)) for cr, ci in carry)
        return sweep(BWD_RESET_STEP, SEG_CHUNKS, carry, store)

    zero_tile = jnp.zeros((N_SEG, LANES), F32)
    ends = full(tuple((zero_tile, zero_tile) for _ in range(N_SLAB)), False)

    starts = []
    for k, (er, ei) in enumerate(ends):
        pr, pi = _cpow(a_row[k][0], a_row[k][1], SEG_CHUNKS)
        if backward:
            order = range(N_SEG - 1, -1, -1)
            g = (er[0:1], ei[0:1])
        else:
            order = range(N_SEG)
            g = (jnp.zeros((1, LANES), F32), jnp.zeros((1, LANES), F32))
        rows_r = [None] * N_SEG
        rows_i = [None] * N_SEG
        for j in order:
            rows_r[j], rows_i[j] = g
            nr, ni = _cmul(pr, pi, g[0], g[1])
            g = (nr + er[j:j + 1], ni + ei[j:j + 1])
        starts.append((jnp.concatenate(rows_r, axis=0), jnp.concatenate(rows_i, axis=0)))
    full(tuple(starts), True)


def _s5_out_kernel(vl_ref, vc_ref, ar_ref, ai_ref, e_ref, t_ref, ft_ref, yl_ref, yc_ref, st_ref):
    n_dir = 2 * N_SLAB
    rows = [jnp.concatenate([vc_ref[pr], vl_ref[pr]], axis=0) for pr in range(S5_PAIRS)]
    for pr in range(S5_PAIRS):
        st = jnp.dot(rows[pr], e_ref[0, pr], preferred_element_type=F32)
        for comp in range(4):
            st_ref[comp * N_SLAB + pr] = st[:, comp * LANES:(comp + 1) * LANES]
    for d in range(2):
        st_d = st_ref.at[pl.ds(d * n_dir, n_dir)]
        _scan_boundary_states(st_d, st_d, ar_ref.at[pl.ds(d, 1)], ai_ref.at[pl.ds(d, 1)], d == 1)

    for pr in range(S5_PAIRS):
        st = jnp.concatenate([st_ref[comp * N_SLAB + pr] for comp in range(4)],
                             axis=-1).astype(BF16)
        y = (jnp.dot(rows[pr], t_ref[0, pr], preferred_element_type=F32)
             + lax.dot_general(st, ft_ref[0, pr], (((1,), (1,)), ((), ())),
                               preferred_element_type=F32))
        yc_ref[pr] = y[:CTX_CHUNKS]
        yl_ref[pr] = y[CTX_CHUNKS:]


def _s5_out(vp, al_re, al_im, e_mat, t_mat, ft_mat):
    pair_w = S5_CHUNK * PAIR_BLK
    ctx_blk0 = (R_LAT // S5_CHUNK) // CTX_CHUNKS
    return pl.pallas_call(
        _s5_out_kernel,
        grid=(S5_QT, B),
        in_specs=[
            pl.BlockSpec((S5_PAIRS, LAT_CHUNKS, pair_w), lambda q, b: (q, b, 0)),
            pl.BlockSpec((S5_PAIRS, CTX_CHUNKS, pair_w), lambda q, b: (q, ctx_blk0 + b, 0)),
            pl.BlockSpec((2, 1, S5_ST), lambda q, b: (q, 0, 0)),
            pl.BlockSpec((2, 1, S5_ST), lambda q, b: (q, 0, 0)),
            pl.BlockSpec((1, S5_PAIRS, pair_w, 4 * LANES), lambda q, b: (q, 0, 0, 0)),
            pl.BlockSpec((1, S5_PAIRS, pair_w, pair_w), lambda q, b: (q, 0, 0, 0)),
            pl.BlockSpec((1, S5_PAIRS, pair_w, 4 * LANES), lambda q, b: (q, 0, 0, 0)),
        ],
        out_specs=[
            pl.BlockSpec((S5_PAIRS, LAT_CHUNKS, pair_w), lambda q, b: (q, b, 0)),
            pl.BlockSpec((S5_PAIRS, CTX_CHUNKS, pair_w), lambda q, b: (q, b, 0)),
        ],
        out_shape=[
            jax.ShapeDtypeStruct((S5_QT * S5_PAIRS, B * LAT_CHUNKS, pair_w), F32),
            jax.ShapeDtypeStruct((S5_QT * S5_PAIRS, B * CTX_CHUNKS, pair_w), F32),
        ],
        scratch_shapes=[pltpu.VMEM((4 * N_SLAB, BATCH_CHUNKS, LANES), F32)],
        compiler_params=pltpu.CompilerParams(vmem_limit_bytes=VMEM_LIMIT),
        name="s5_out",
    )(vp, vp, al_re, al_im, e_mat, t_mat, ft_mat)


def _tail0_stages(z, ys, rest_ref, gate, gluw_ref, glub_ref, sg_g_ref, sg_b_ref, sgw_ref,
                  sgbias_ref, wout_ref, lng_ref, lnb_ref):
    ga = rest_ref[:, 0:S5_W].astype(F32)
    u = rest_ref[:, S5_W:2 * S5_W].astype(F32)
    v = rest_ref[:, 2 * S5_W:3 * S5_W].astype(F32)
    gb = rest_ref[:, 3 * S5_W:4 * S5_W].astype(F32)

    ya = _gelu_tanh(ys)
    glu = jnp.dot(ya.astype(BF16), gluw_ref[...], preferred_element_type=F32) + glub_ref[...]
    yield None
    ya = ya * _sigmoid(glu) * _silu(ga)
    yield None

    vn = _layer_norm(v, sg_g_ref[...], sg_b_ref[...]).astype(BF16)
    lane = lax.broadcasted_iota(jnp.int32, (SG_CHUNK, LANES), 1)
    first_head = lane < (SG_W // SG_HEADS)
    chunks = []
    for c in range(TN // SG_CHUNK):
        tiles = []
        for j in range(SG_W // LANES):
            vt = vn[c * SG_CHUNK:(c + 1) * SG_CHUNK, j * LANES:(j + 1) * LANES]
            zero = jnp.zeros_like(vt)
            v2 = jnp.concatenate([jnp.where(first_head, vt, zero), jnp.where(first_head, zero, vt)],
                                 axis=0)
            tiles.append(jnp.dot(sgw_ref[j], v2, preferred_element_type=F32))
        chunks.append(jnp.concatenate(tiles, axis=-1) + sgbias_ref[...])
    s = jnp.concatenate(chunks, axis=0)
    yb = u * s * _silu(gb)
    yield None

    mix = jnp.concatenate([ya, yb], axis=-1).astype(BF16)
    y = jnp.dot(mix, wout_ref[...], preferred_element_type=F32) * gate
    yield None
    yield _layer_norm(ALPHA * z + y, lng_ref[...], lnb_ref[...])


def _rope_tables():
    nf = HD // 4
    n_rows = N_LAT // GRID_W
    lane = jnp.arange(LANES)
    inv = ROPE_BASE ** (-(lane % nf).astype(F32) / nf)
    by_row = ((lane % HD) // (HD // 2) == 0)[None, :]
    sign = jnp.where((lane % (HD // 2)) < nf, -1.0, 1.0)[None, :]
    row_ang = jnp.arange(n_rows, dtype=F32)[:, None] * inv[None, :]
    col_ang = jnp.arange(GRID_W, dtype=F32)[:, None] * inv[None, :]
    zero = jnp.zeros((), F32)
    row_tab = jnp.stack([jnp.where(by_row, jnp.cos(row_ang), zero),
                         jnp.where(by_row, sign * jnp.sin(row_ang), zero)])
    col_tab = jnp.stack([jnp.where(by_row, zero, jnp.cos(col_ang)),
                         jnp.where(by_row, zero, sign * jnp.sin(col_ang))])
    return row_tab, col_tab


def _rope_tile(x, cos, sin, first_half):
    nf = HD // 4
    partner = jnp.where(first_half, pltpu.roll(x, LANES - nf, axis=1), pltpu.roll(x, nf, axis=1))
    return x * cos + partner * sin


def _mid_kernel(x_ref, ctx_ref, mod0_ref, mod1_ref, ysl_ref, ysc_ref, rest_ref, gluw_ref, glub_ref,
                sg_g_ref, sg_b_ref, sgw_ref, sgbias_ref, wout_ref, lng_ref, lnb_ref, w_ref,
                rtab_ref, ctab_ref, z1_ref, q_ref, k_ref, vt_ref, g_ref, zprev_ref, wbf_scr,
                wout_scr, ys_scr):
    i = pl.program_id(0)
    dq = N_HEADS * HD
    dkv = N_KV * HD

    @pl.when(i == 0)
    def _():
        zprev_ref[...] = jnp.zeros((TN, D), F32)
        wbf_scr[...] = w_ref[...].astype(BF16)
        wout_scr[...] = wout_ref[...].astype(BF16)

    j = jnp.maximum(i - 1, 0)
    m1 = mod1_ref[pl.ds(_block_mod_row(j), 1), :]
    h = (zprev_ref[...] * (1.0 + m1[:, D:2 * D]) + m1[:, :D]).astype(BF16)
    grid_row0 = (j % LAT_BLK_PER_BATCH) * (TN // GRID_W)
    table = lambda t: jnp.concatenate(
        [rtab_ref[t, pl.ds(grid_row0 + r, 1), :] + ctab_ref[t] for r in range(TN // GRID_W)],
        axis=0)
    cos = jnp.where(j < N_LAT_BLK, table(0), 1.0)
    sin = jnp.where(j < N_LAT_BLK, table(1), 0.0)
    lane = lax.broadcasted_iota(jnp.int32, (TN, LANES), 1)
    first_half = (lane % (HD // 2)) < (HD // 4)
    scale = HD ** -0.5 * LOG2E

    def project(c0, c1):
        return jnp.dot(h, wbf_scr[:, c0:c1], preferred_element_type=F32)

    def roped_tiles(p, mult):
        for c in range(p.shape[1] // LANES):
            r = _rope_tile(p[:, c * LANES:(c + 1) * LANES], cos, sin, first_half)
            yield c, (r * mult if mult != 1.0 else r).astype(BF16)

    t = jnp.minimum(i, N_BLK - 1)
    z = jnp.where(t < N_LAT_BLK, x_ref[...], ctx_ref[...])
    yp = jnp.where(t < N_LAT_BLK, ysl_ref[...], ysc_ref[...])
    for qt in range(S5_QT):
        for k in range(S5_CHUNK // S5_PAIRS):
            tiles = _block_transpose4([yp[qt * S5_PAIRS + pr][:, k * LANES:(k + 1) * LANES]
                                       for pr in range(S5_PAIRS)])
            for n, tile in enumerate(tiles):
                ys_scr[qt, pl.ds(k * S5_PAIRS + n, TN // S5_CHUNK, stride=S5_CHUNK), :] = tile
    ys = jnp.concatenate([ys_scr[qt] for qt in range(S5_QT)], axis=-1)
    gate = mod0_ref[pl.ds(_block_mod_row(t), 1), 2 * D:3 * D]
    tail = _tail0_stages(z, ys, rest_ref, gate, gluw_ref, glub_ref, sg_g_ref, sg_b_ref, sgw_ref,
                         sgbias_ref, wout_scr, lng_ref, lnb_ref)

    half_q = dq // 2
    p_q0 = project(0, half_q)
    next(tail)
    p_q1 = project(half_q, dq)
    next(tail)
    p_k = project(dq, dq + dkv)
    vt_ref[...] = lax.dot_general(wbf_scr[:, dq + dkv:dq + 2 * dkv], h, (((0,), (1,)), ((), ())),
                                  preferred_element_type=F32).astype(BF16)
    next(tail)
    next(tail)
    for c0, p in ((0, p_q0), (half_q, p_q1)):
        for c, r in roped_tiles(p, scale):
            q_ref[:, c0 + c * LANES:c0 + (c + 1) * LANES] = r
    for c, r in roped_tiles(p_k, 1.0):
        k_ref[:, c * LANES:(c + 1) * LANES] = r
    p_g = project(dq + 2 * dkv, ODD_IN)
    z1 = next(tail)
    z1_ref[...] = z1
    zprev_ref[...] = z1
    g_ref[...] = p_g.astype(BF16)


def _mid(x2, ctx2, mod0, mod1, ys_lat, ys_ctx, rest, glu_w, glu_b, sg_g, sg_b, sg_w, sg_bias,
         w_out, ln_g, ln_b, w_in1, row_tab, col_tab):
    dq = N_HEADS * HD
    dkv = N_KV * HD
    once = dict(pipeline_mode=pl.Buffered(1))
    row = lambda n: pl.BlockSpec((1, n), lambda i: (0, 0))
    lat = lambda i: (jnp.minimum(i, N_LAT_BLK - 1), 0)
    tail_blk = lambda i: (jnp.minimum(i, N_BLK - 1), 0)
    proj_blk = lambda i: (jnp.maximum(i - 1, 0), 0)
    return pl.pallas_call(
        _mid_kernel,
        grid=(N_BLK + 1,),
        in_specs=[
            pl.BlockSpec((TN, D), lat),
            pl.BlockSpec((R_CTX, D), lambda i: (0, 0), **once),
            pl.BlockSpec((8, 3 * D), lambda i: (0, 0)),
            pl.BlockSpec((8, 3 * D), lambda i: (0, 0)),
            pl.BlockSpec((S5_QT * S5_PAIRS, TN // S5_CHUNK, S5_CHUNK * PAIR_BLK),
                         lambda i: (0, jnp.minimum(i, N_LAT_BLK - 1), 0)),
            pl.BlockSpec((S5_QT * S5_PAIRS, R_CTX // S5_CHUNK, S5_CHUNK * PAIR_BLK),
                         lambda i: (0, 0, 0), **once),
            pl.BlockSpec((TN, EVEN_IN - S5_W), tail_blk),
            pl.BlockSpec((S5_W, S5_W), lambda i: (0, 0), **once),
            row(S5_W), row(SG_W), row(SG_W),
            pl.BlockSpec((SG_HEADS // 2, SG_CHUNK, 2 * SG_CHUNK), lambda i: (0, 0, 0)),
            pl.BlockSpec((SG_CHUNK, SG_W), lambda i: (0, 0)),
            pl.BlockSpec((S5_W + SG_W, D), lambda i: (0, 0), **once),
            row(D), row(D),
            pl.BlockSpec((D, ODD_IN), lambda i: (0, 0), **once),
            pl.BlockSpec((2, N_LAT // GRID_W, LANES), lambda i: (0, 0, 0)),
            pl.BlockSpec((2, GRID_W, LANES), lambda i: (0, 0, 0)),
        ],
        out_specs=[
            pl.BlockSpec((TN, D), tail_blk),
            pl.BlockSpec((TN, dq), proj_blk),
            pl.BlockSpec((TN, dkv), proj_blk),
            pl.BlockSpec((dkv, TN), lambda i: (0, jnp.maximum(i - 1, 0))),
            pl.BlockSpec((TN, dq), proj_blk),
        ],
        out_shape=[
            jax.ShapeDtypeStruct((R_ALL, D), F32),
            jax.ShapeDtypeStruct((R_ALL, dq), BF16),
            jax.ShapeDtypeStruct((R_ALL, dkv), BF16),
            jax.ShapeDtypeStruct((dkv, R_ALL), BF16),
            jax.ShapeDtypeStruct((R_ALL, dq), BF16),
        ],
        scratch_shapes=[pltpu.VMEM((TN, D), F32), pltpu.VMEM((D, ODD_IN), BF16),
                        pltpu.VMEM((S5_W + SG_W, D), BF16), pltpu.VMEM((S5_QT, TN, LANES), F32)],
        compiler_params=pltpu.CompilerParams(vmem_limit_bytes=VMEM_LIMIT),
        name="mid",
    )(x2, ctx2, mod0, mod1, ys_lat, ys_ctx, rest, glu_w, glu_b, sg_g, sg_b, sg_w, sg_bias, w_out,
      ln_g, ln_b, w_in1, row_tab, col_tab)


N_QBLK = N_LAT // ATT_BLK
GRP = N_HEADS // N_KV
ATT_SUB = 4
ATT_ROWS = ATT_SUB * ATT_BLK
N_QSTEP = N_QBLK // ATT_SUB
N_ATT_STEPS = B * N_QSTEP
assert ATT_ROWS == TN
assert WINDOW == ATT_BLK
ONES_ROWS = 16


def _attn_kernel(sink_ref, q_ref, kp_ref, kc_ref, kn_ref, kx_ref, vp_ref, vc_ref, vn_ref, vx_ref,
                 g_ref, z_ref, mod_ref, wout_ref, lng_ref, lnb_ref, out_ref, o_scr, wout_scr):
    step = pl.program_id(0)

    @pl.when(step == 0)
    def _():
        o_scr[...] = jnp.zeros((ATT_ROWS, N_HEADS * HD), BF16)
        wout_scr[...] = wout_ref[...].astype(BF16)

    t = jnp.maximum(step - 1, 0)
    gate = mod_ref[pl.ds(_block_mod_row(t), 1), 2 * D:3 * D]
    mix = (o_scr[...].astype(F32) * _silu(g_ref[...].astype(F32))).astype(BF16)
    y = jnp.dot(mix, wout_scr[...], preferred_element_type=F32) * gate
    out_ref[...] = _layer_norm(ALPHA * z_ref[...] + y, lng_ref[...], lnb_ref[...])

    i = jnp.minimum(step, N_ATT_STEPS - 1) % N_QSTEP
    half = ATT_BLK // 2
    n_win = 2 * ATT_BLK + half
    n_keys = n_win + N_CTX
    nq = GRP * half
    k_win = jnp.concatenate([kp_ref[...], kc_ref[...], kn_ref[...]], axis=0)
    vt_win = jnp.concatenate([vp_ref[...], vc_ref[...], vn_ref[...]], axis=1)
    k_ctx = kx_ref[...]
    vt_ctx = vx_ref[...]

    kpos = lax.broadcasted_iota(jnp.int32, (half, half), 0)
    qpos = lax.broadcasted_iota(jnp.int32, (half, half), 1)
    tile_q = lambda a: jnp.concatenate([a] * GRP, axis=1)
    neg = lambda ok: tile_q(jnp.where(ok, 0.0, NEG_INF))
    everyone = kpos >= 0
    qgrp = lax.broadcasted_iota(jnp.int32, (1, nq), 1) // half
    ones = jnp.ones((ONES_ROWS, n_keys), BF16)
    q = q_ref[...]

    halves = [(u, v) for u in range(ATT_SUB) for v in range(2)]
    units = [(n, h) for n in range(len(halves)) for h in range(N_KV)]
    k_all, vt_all, bias = [], [], []
    for u, v in halves:
        first = u * ATT_BLK + v * half
        k_all.append(jnp.concatenate([k_win[first:first + n_win], k_ctx], axis=0))
        vt_all.append(jnp.concatenate([vt_win[:, first:first + n_win], vt_ctx], axis=1))
        in_prev = [blk for blk in range(5) if first + blk * half < ATT_BLK]
        in_next = [blk for blk in range(5) if first + blk * half >= ATT_BLK + ATT_ROWS]
        blocks = {}
        for blk in sorted(set([0, 4] + in_prev + in_next)):
            ok = (kpos >= qpos) if blk == 0 else ((kpos <= qpos) if blk == 4 else everyone)
            if blk in in_prev:
                ok = ok & (i > 0)
            if blk in in_next:
                ok = ok & (i < N_QSTEP - 1)
            blocks[blk] = neg(ok)
        bias.append(blocks)
    scores = []
    for n, h in units:
        u, v = halves[n]
        r0 = u * ATT_BLK + v * half
        kh = k_all[n][:, h * HD:(h + 1) * HD]
        qh = jnp.concatenate(
            [q[r0:r0 + half, (h * GRP + g) * HD:(h * GRP + g + 1) * HD] for g in range(GRP)],
            axis=0)
        scores.append(lax.dot_general(kh, qh, (((1,), (1,)), ((), ())),
                                      preferred_element_type=F32))
    probs = []
    for (n, h), s in zip(units, scores):
        s = jnp.concatenate(
            [s[blk * half:(blk + 1) * half] + bias[n][blk] if blk in bias[n]
             else s[blk * half:(blk + 1) * half] for blk in range(5)] + [s[n_win:]], axis=0)
        sink = jnp.zeros((1, nq), F32)
        for g in range(GRP):
            sink = jnp.where(qgrp == g, sink_ref[h * GRP + g] * LOG2E, sink)
        m = jnp.maximum(jnp.max(s, axis=0, keepdims=True), sink)
        probs.append((jnp.exp2(s - m).astype(BF16), jnp.exp2(sink - m)))
    outs = [[None] * N_HEADS for _ in halves]
    for (n, h), (p, p_sink) in zip(units, probs):
        vt1 = jnp.concatenate([vt_all[n][h * HD:(h + 1) * HD], ones], axis=0)
        ov = jnp.dot(vt1, p, preferred_element_type=F32)
        o_t = ov[:HD] / (ov[HD:HD + 1] + p_sink)
        for g in range(GRP):
            outs[n][h * GRP + g] = o_t[:, g * half:(g + 1) * half]
    for u in range(ATT_SUB):
        o_t = jnp.concatenate(
            [jnp.concatenate([outs[2 * u][hh], outs[2 * u + 1][hh]], axis=1)
             for hh in range(N_HEADS)], axis=0)
        o_scr[u * ATT_BLK:(u + 1) * ATT_BLK, :] = jnp.transpose(o_t).astype(BF16)


def _attention_tail(sink, q, k, vt, g, z1, mod, w_out, ln_g, ln_b):
    dq = N_HEADS * HD
    dkv = N_KV * HD
    ctx_blk0 = R_LAT // N_CTX
    cur = lambda s: jnp.minimum(s, N_ATT_STEPS - 1)
    batch = lambda s: cur(s) // N_QSTEP
    qstep = lambda s: cur(s) % N_QSTEP
    prev_blk = lambda s: batch(s) * N_QBLK + jnp.maximum(ATT_SUB * qstep(s) - 1, 0)
    next_blk = lambda s: batch(s) * N_QBLK + jnp.minimum(ATT_SUB * (qstep(s) + 1), N_QBLK - 1)
    tail = lambda s: jnp.maximum(s - 1, 0)
    once = dict(pipeline_mode=pl.Buffered(1))
    row = lambda n: pl.BlockSpec((1, n), lambda s, sk: (0, 0))
    return pl.pallas_call(
        _attn_kernel,
        grid_spec=pltpu.PrefetchScalarGridSpec(
            num_scalar_prefetch=1,
            grid=(N_ATT_STEPS + 1,),
            in_specs=[
                pl.BlockSpec((ATT_ROWS, dq), lambda s, sk: (cur(s), 0)),
                pl.BlockSpec((ATT_BLK, dkv), lambda s, sk: (prev_blk(s), 0)),
                pl.BlockSpec((ATT_ROWS, dkv), lambda s, sk: (cur(s), 0)),
                pl.BlockSpec((ATT_BLK, dkv), lambda s, sk: (next_blk(s), 0)),
                pl.BlockSpec((N_CTX, dkv), lambda s, sk: (ctx_blk0 + batch(s), 0)),
                pl.BlockSpec((dkv, ATT_BLK), lambda s, sk: (0, prev_blk(s))),
                pl.BlockSpec((dkv, ATT_ROWS), lambda s, sk: (0, cur(s))),
                pl.BlockSpec((dkv, ATT_BLK), lambda s, sk: (0, next_blk(s))),
                pl.BlockSpec((dkv, N_CTX), lambda s, sk: (0, ctx_blk0 + batch(s))),
                pl.BlockSpec((ATT_ROWS, dq), lambda s, sk: (tail(s), 0)),
                pl.BlockSpec((ATT_ROWS, D), lambda s, sk: (tail(s), 0)),
                pl.BlockSpec((8, 3 * D), lambda s, sk: (0, 0)),
                pl.BlockSpec((dq, D), lambda s, sk: (0, 0), **once),
                row(D), row(D),
            ],
            out_specs=pl.BlockSpec((ATT_ROWS, D), lambda s, sk: (tail(s), 0)),
            scratch_shapes=[pltpu.VMEM((ATT_ROWS, dq), BF16), pltpu.VMEM((dq, D), BF16)],
        ),
        out_shape=jax.ShapeDtypeStruct((R_LAT, D), F32),
        compiler_params=pltpu.CompilerParams(vmem_limit_bytes=VMEM_LIMIT),
        name="attention",
    )(sink, q, k, k, k, k, vt, vt, vt, vt, g, z1, mod, w_out, ln_g, ln_b)


def kernel(x, c, ctx, c_ctx, mod_w, mod_b, ln_g, ln_b, e_w_in, e_w_out, s5_lam_re, s5_lam_im,
           s5_log_dt, s5_b_re, s5_b_im, s5_c_re, s5_c_im, s5_d, glu_w, glu_b, sg_ln_g, sg_ln_b,
           sg_w, sg_b, o_w_in, o_w_out, o_sink):
    x2 = x.reshape(R_LAT, D)
    ctx2 = ctx.reshape(R_CTX, D)
    mod = _modulation(c, c_ctx, mod_w, mod_b)

    vp, rest = _in0(x2, ctx2, mod[0], e_w_in[0])
    t_mat, e_mat, ft_mat, al_re, al_im = _s5_prep(
        s5_lam_re[0], s5_lam_im[0], s5_log_dt[0], s5_b_re[0], s5_b_im[0], s5_c_re[0], s5_c_im[0],
        s5_d[0])
    ys_lat, ys_ctx = _s5_out(vp, al_re, al_im, e_mat, t_mat, ft_mat)
    sg_bias = jnp.repeat(jnp.transpose(sg_b[0]), SG_W // SG_HEADS, axis=1)
    sg_w2 = jnp.transpose(sg_w[0].reshape(SG_HEADS // 2, 2, SG_CHUNK, SG_CHUNK), (0, 2, 1, 3))
    sg_w2 = sg_w2.reshape(SG_HEADS // 2, SG_CHUNK, 2 * SG_CHUNK).astype(BF16)
    row_tab, col_tab = _rope_tables()
    z1, q, k, vt, g = _mid(
        x2, ctx2, mod[0], mod[1], ys_lat, ys_ctx, rest, glu_w[0].astype(BF16),
        glu_b[0].reshape(1, S5_W), sg_ln_g[0].reshape(1, SG_W), sg_ln_b[0].reshape(1, SG_W),
        sg_w2, sg_bias, e_w_out[0], ln_g[0].reshape(1, D),
        ln_b[0].reshape(1, D), o_w_in[0], row_tab, col_tab)

    out = _attention_tail(o_sink[0], q, k, vt, g, z1, mod[1], o_w_out[0],
                          ln_g[1].reshape(1, D), ln_b[1].reshape(1, D))
    return out.reshape(B, N_LAT, D)
```

```python
import math

import jax
import jax.numpy as jnp
from jax import lax
from jax.experimental import pallas as pl
from jax.experimental.pallas import tpu as pltpu

F32 = jnp.float32
BF16 = jnp.bfloat16

D = 1024
B = 2
N_LAT = 8192
N_CTX = 256
DEPTH = 2
GRID_W = 64
S5_W = 512
S5_GC = 16
S5_G = 32
S5_P = 64
SG_W = 512
SG_HEADS = 8
SG_CHUNK = 128
N_HEADS = 16
N_KV = 4
HD = 64
WINDOW = 128
ATT_BLK = 128
ROPE_BASE = 10000.0
NEG_INF = -1e30
LN_EPS = 1e-5
ALPHA = (2 * DEPTH) ** 0.25
LOG2E = math.log2(math.e)
EVEN_IN = 2 * S5_W + 3 * SG_W
ODD_IN = 2 * N_HEADS * HD + 2 * N_KV * HD

LANES = 128
VMEM_LIMIT = 56 * 1024 * 1024

R_LAT = B * N_LAT
R_CTX = B * N_CTX
R_ALL = R_LAT + R_CTX
TN = 512
N_LAT_BLK = R_LAT // TN
N_BLK = R_ALL // TN
LAT_BLK_PER_BATCH = N_LAT // TN
CTX_MOD_ROW = B

S5_CHUNK = 8
S5_QT = S5_W // LANES
S5_GPT = LANES // S5_GC
S5_ST = S5_GPT * S5_P
S5_PAIRS = S5_GPT // 2
PAIR_BLK = 2 * S5_GC
LAT_CHUNKS = N_LAT // S5_CHUNK
CTX_CHUNKS = N_CTX // S5_CHUNK
BATCH_CHUNKS = CTX_CHUNKS + LAT_CHUNKS
N_SEG = 16
SEG_CHUNKS = BATCH_CHUNKS // N_SEG
BWD_RESET_STEP = LAT_CHUNKS - (N_SEG - 1) * SEG_CHUNKS


def _sigmoid(x):
    return 1.0 / (1.0 + jnp.exp(-x))


def _silu(x):
    return x * _sigmoid(x)


def _gelu_tanh(x):
    return 0.5 * x * (1.0 + jnp.tanh(math.sqrt(2.0 / math.pi) * (x + 0.044715 * (x * x * x))))


def _layer_norm(x, g, b):
    mu = jnp.mean(x, axis=-1, keepdims=True)
    xc = x - mu
    var = jnp.mean(xc * xc, axis=-1, keepdims=True)
    return xc * lax.rsqrt(var + LN_EPS) * g + b


def _split_bf16(x):
    hi = x.astype(BF16)
    return hi, (x - hi.astype(F32)).astype(BF16)


def _dot_split(x, y_split, dims):
    x_hi, x_lo = _split_bf16(x)
    y_hi, y_lo = y_split
    dot = lambda a, b: lax.dot_general(a, b, dims, preferred_element_type=F32)
    return dot(x_hi, y_hi) + (dot(x_hi, y_lo) + dot(x_lo, y_hi))


def _block_mod_row(i):
    return jnp.where(i < N_LAT_BLK, i // LAT_BLK_PER_BATCH, CTX_MOD_ROW)


def _mod_kernel(cv_ref, w_ref, b_ref, o_ref):
    s = _silu(cv_ref[...])
    o_ref[0] = _dot_split(s, _split_bf16(w_ref[0]), (((1,), (0,)), ((), ()))) + b_ref[0]


def _modulation(c, c_ctx, mod_w, mod_b):
    cv = jnp.concatenate([c, c_ctx[None], jnp.zeros((8 - B - 1, D), F32)], axis=0)
    tn = 1024
    return pl.pallas_call(
        _mod_kernel,
        grid=(DEPTH, 3 * D // tn),
        in_specs=[
            pl.BlockSpec((8, D), lambda l, j: (0, 0)),
            pl.BlockSpec((1, D, tn), lambda l, j: (l, 0, j)),
            pl.BlockSpec((1, 1, tn), lambda l, j: (l, 0, j)),
        ],
        out_specs=pl.BlockSpec((1, 8, tn), lambda l, j: (l, 0, j)),
        out_shape=jax.ShapeDtypeStruct((DEPTH, 8, 3 * D), F32),
        compiler_params=pltpu.CompilerParams(vmem_limit_bytes=VMEM_LIMIT),
        name="modulation",
    )(cv, mod_w, mod_b.reshape(DEPTH, 1, 3 * D))


def _block_transpose4(tiles):
    tiles = list(tiles)
    blk = lax.broadcasted_iota(jnp.int32, tiles[0].shape, 1) // PAIR_BLK
    for dist in (2, 1):
        keep = (blk & dist) == 0
        for i in range(len(tiles)):
            if i & dist:
                continue
            lo, hi = tiles[i], tiles[i + dist]
            tiles[i] = jnp.where(keep, lo, pltpu.roll(hi, dist * PAIR_BLK, axis=1))
            tiles[i + dist] = jnp.where(keep, pltpu.roll(lo, LANES - dist * PAIR_BLK, axis=1), hi)
    return tiles


def _in0_kernel(x_ref, ctx_ref, mod_ref, w_ref, vp_ref, rest_ref, xa_scr, wbf_scr):
    L = S5_CHUNK
    i = pl.program_id(0)

    @pl.when(i == 0)
    def _():
        wbf_scr[...] = w_ref[...].astype(BF16)

    z = jnp.where(i < N_LAT_BLK, x_ref[...], ctx_ref[...])
    m = mod_ref[pl.ds(_block_mod_row(i), 1), :]
    h = z * (1.0 + m[:, D:2 * D]) + m[:, :D]
    p = jnp.dot(h.astype(BF16), wbf_scr[...], preferred_element_type=F32)
    rest_ref[...] = p[:, S5_W:].astype(BF16)
    for q in range(S5_QT):
        xa_scr[q] = p[:, q * LANES:(q + 1) * LANES]
        tiles = [xa_scr[q, pl.ds(t, TN // L, stride=L), :] for t in range(L)]
        halves = [_block_transpose4(tiles[k * S5_PAIRS:(k + 1) * S5_PAIRS])
                  for k in range(L // S5_PAIRS)]
        for pr in range(S5_PAIRS):
            vp_ref[q * S5_PAIRS + pr] = jnp.concatenate(
                [half[pr] for half in halves], axis=-1).astype(BF16)


def _in0(x2, ctx2, mod, w_in):
    pair_w = S5_CHUNK * PAIR_BLK
    return pl.pallas_call(
        _in0_kernel,
        grid=(N_BLK,),
        in_specs=[
            pl.BlockSpec((TN, D), lambda i: (jnp.minimum(i, N_LAT_BLK - 1), 0)),
            pl.BlockSpec((R_CTX, D), lambda i: (0, 0)),
            pl.BlockSpec((8, 3 * D), lambda i: (0, 0)),
            pl.BlockSpec((D, EVEN_IN), lambda i: (0, 0)),
        ],
        out_specs=[
            pl.BlockSpec((S5_QT * S5_PAIRS, TN // S5_CHUNK, pair_w), lambda i: (0, i, 0)),
            pl.BlockSpec((TN, EVEN_IN - S5_W), lambda i: (i, 0)),
        ],
        out_shape=[
            jax.ShapeDtypeStruct((S5_QT * S5_PAIRS, R_ALL // S5_CHUNK, pair_w), BF16),
            jax.ShapeDtypeStruct((R_ALL, EVEN_IN - S5_W), BF16),
        ],
        scratch_shapes=[pltpu.VMEM((S5_QT, TN, LANES), F32), pltpu.VMEM((D, EVEN_IN), BF16)],
        compiler_params=pltpu.CompilerParams(vmem_limit_bytes=VMEM_LIMIT),
        name="in0",
    )(x2, ctx2, mod, w_in)


def _s5_prep_kernel(rows_ref, bre_ref, bim_ref, cre_ref, cim_ref, dsk_ref, t_ref, e_ref, ft_ref,
                    alr_ref, ali_ref):
    L = S5_CHUNK
    tile = (LANES, S5_ST)
    same_group = (lax.broadcasted_iota(jnp.int32, tile, 0) // S5_GC
                  == lax.broadcasted_iota(jnp.int32, tile, 1) // S5_P)
    nt_dims = (((1,), (1,)), ((), ()))
    nn_dims = (((1,), (0,)), ((), ()))
    pick_ch = (lax.broadcasted_iota(jnp.int32, (LANES, S5_GC), 0) % S5_GC
               == lax.broadcasted_iota(jnp.int32, (LANES, S5_GC), 1)).astype(BF16)
    rep_st = (lax.broadcasted_iota(jnp.int32, (S5_P, S5_ST), 0)
              == lax.broadcasted_iota(jnp.int32, (S5_P, S5_ST), 1) % S5_P).astype(BF16)

    def spread_b(b):
        hi, lo = _split_bf16(b)
        dot = lambda v: lax.dot_general(pick_ch, v, nt_dims, preferred_element_type=F32)
        return jnp.where(same_group, dot(hi) + dot(lo), 0.0)

    def spread_c(c):
        hi, lo = _split_bf16(c)
        dot = lambda v: lax.dot_general(v, rep_st, nn_dims, preferred_element_type=F32)
        return jnp.where(same_group, dot(hi) + dot(lo), 0.0)

    pair_rows = lambda pr: slice(pr * PAIR_BLK, (pr + 1) * PAIR_BLK)
    pair_lanes = lambda pr: slice(pr * LANES, (pr + 1) * LANES)
    lag_strip = []
    for d in range(2):
        lam_re = rows_ref[d, 0]
        lam_im = rows_ref[d, 1]
        dt = jnp.exp(rows_ref[d, 2])
        pw = []
        for k in range(L + 1):
            mag = jnp.exp(lam_re * dt * float(k))
            ang = lam_im * dt * float(k)
            pw.append((mag * jnp.cos(ang), mag * jnp.sin(ang)))
        alr_ref[d] = pw[L][0]
        ali_ref[d] = pw[L][1]
        den = lam_re * lam_re + lam_im * lam_im
        nr = pw[1][0] - 1.0
        ni = pw[1][1]
        f_re = (nr * lam_re + ni * lam_im) / den
        f_im = (ni * lam_re - nr * lam_im) / den
        b_re = spread_b(bre_ref[d])
        b_im = spread_b(bim_ref[d])
        bb_re = f_re * b_re - f_im * b_im
        bb_im = f_re * b_im + f_im * b_re
        c_re = spread_c(cre_ref[d])
        c_im = spread_c(cim_ref[d])
        col_re = slice(2 * d * LANES, (2 * d + 1) * LANES)
        col_im = slice((2 * d + 1) * LANES, (2 * d + 2) * LANES)
        strips = []
        for pr in range(S5_PAIRS):
            blk = lambda a: a[pair_rows(pr), pair_lanes(pr)]
            p_re = [blk(jnp.broadcast_to(pw[k][0], tile)) for k in range(L + 1)]
            p_im = [blk(jnp.broadcast_to(pw[k][1], tile)) for k in range(L + 1)]
            bbr, bbi, cr, ci = blk(bb_re), blk(bb_im), blk(c_re), blk(c_im)
            for k in range(L):
                s = (L - 1 - k) if d == 0 else k
                rows = slice(s * PAIR_BLK, (s + 1) * PAIR_BLK)
                e_ref[0, pr, rows, col_re] = (p_re[k] * bbr - p_im[k] * bbi).astype(BF16)
                e_ref[0, pr, rows, col_im] = (p_re[k] * bbi + p_im[k] * bbr).astype(BF16)
            cw = [(cr * p_re[k] - ci * p_im[k], -(cr * p_im[k] + ci * p_re[k]))
                  for k in range(L + 1)]
            for t in range(L):
                k = (t + 1) if d == 0 else (L - t)
                rows = slice(t * PAIR_BLK, (t + 1) * PAIR_BLK)
                ft_ref[0, pr, rows, col_re] = cw[k][0].astype(BF16)
                ft_ref[0, pr, rows, col_im] = cw[k][1].astype(BF16)
            lags = range(L) if d == 0 else range(L - 1, -1, -1)
            c_stack = jnp.concatenate(
                [jnp.concatenate([cw[k][0], cw[k][1]], axis=-1) for k in lags], axis=0)
            strips.append(_dot_split(jnp.concatenate([bbr, bbi], axis=-1),
                                     _split_bf16(c_stack), nt_dims))
        lag_strip.append(strips)

    strip_w = 2 * L * PAIR_BLK
    mid = (L - 1) * PAIR_BLK
    widen = lambda a: jnp.concatenate([a, jnp.zeros((a.shape[0], strip_w - a.shape[1]), F32)], axis=-1)
    row = lax.broadcasted_iota(jnp.int32, (PAIR_BLK, strip_w), 0)
    lane = lax.broadcasted_iota(jnp.int32, (PAIR_BLK, strip_w), 1)
    d_row = widen(dsk_ref[0])
    for pr in range(S5_PAIRS):
        strip = (widen(lag_strip[1][pr])
                 + pltpu.roll(widen(lag_strip[0][pr]), mid, axis=1)
                 + jnp.where(lane == row + mid,
                             pltpu.roll(d_row, mid - pr * PAIR_BLK, axis=1), 0.0))
        for s in range(L):
            shift = (strip_w - (L - 1 - s) * PAIR_BLK) % strip_w
            blk = pltpu.roll(strip, shift, axis=1) if shift else strip
            t_ref[0, pr, s * PAIR_BLK:(s + 1) * PAIR_BLK, :] = blk[:, :L * PAIR_BLK].astype(BF16)


def _s5_prep(lam_re, lam_im, log_dt, b_re, b_im, c_re, c_im, d_skip):
    L = S5_CHUNK
    n_st = S5_G * S5_P
    pair_w = L * PAIR_BLK
    rows = jnp.stack([lam_re.reshape(2, 1, n_st), lam_im.reshape(2, 1, n_st),
                      jnp.repeat(log_dt, S5_P, axis=1).reshape(2, 1, n_st)], axis=1)
    b_spec = pl.BlockSpec((2, S5_ST, S5_GC), lambda q: (0, q, 0))
    c_spec = pl.BlockSpec((2, LANES, S5_P), lambda q: (0, q, 0))
    return pl.pallas_call(
        _s5_prep_kernel,
        grid=(S5_QT,),
        in_specs=[
            pl.BlockSpec((2, 3, 1, S5_ST), lambda q: (0, 0, 0, q)),
            b_spec, b_spec, c_spec, c_spec,
            pl.BlockSpec((1, 1, LANES), lambda q: (q, 0, 0)),
        ],
        out_specs=[
            pl.BlockSpec((1, S5_PAIRS, pair_w, pair_w), lambda q: (q, 0, 0, 0)),
            pl.BlockSpec((1, S5_PAIRS, pair_w, 4 * LANES), lambda q: (q, 0, 0, 0)),
            pl.BlockSpec((1, S5_PAIRS, pair_w, 4 * LANES), lambda q: (q, 0, 0, 0)),
            pl.BlockSpec((2, 1, S5_ST), lambda q: (q, 0, 0)),
            pl.BlockSpec((2, 1, S5_ST), lambda q: (q, 0, 0)),
        ],
        out_shape=[
            jax.ShapeDtypeStruct((S5_QT, S5_PAIRS, pair_w, pair_w), BF16),
            jax.ShapeDtypeStruct((S5_QT, S5_PAIRS, pair_w, 4 * LANES), BF16),
            jax.ShapeDtypeStruct((S5_QT, S5_PAIRS, pair_w, 4 * LANES), BF16),
            jax.ShapeDtypeStruct((2 * S5_QT, 1, S5_ST), F32),
            jax.ShapeDtypeStruct((2 * S5_QT, 1, S5_ST), F32),
        ],
        compiler_params=pltpu.CompilerParams(vmem_limit_bytes=VMEM_LIMIT),
        name="s5_prep",
    )(rows, b_re.reshape(2, n_st, S5_GC), b_im.reshape(2, n_st, S5_GC),
      c_re.reshape(2, S5_W, S5_P), c_im.reshape(2, S5_W, S5_P), d_skip.reshape(S5_QT, 1, LANES))


N_SLAB = S5_ST // LANES


def _cmul(ar, ai, br, bi):
    return ar * br - ai * bi, ar * bi + ai * br


def _cpow(ar, ai, n):
    res = None
    while n:
        if n & 1:
            res = (ar, ai) if res is None else _cmul(res[0], res[1], ar, ai)
        n >>= 1
        if n:
            ar, ai = _cmul(ar, ai, ar, ai)
    return res


def _scan_boundary_states(loc_ref, out_ref, ar_ref, ai_ref, backward):
    a_row = [(ar_ref[0][:, k * LANES:(k + 1) * LANES], ai_ref[0][:, k * LANES:(k + 1) * LANES])
             for k in range(N_SLAB)]
    a_tile = [(jnp.broadcast_to(r, (N_SEG, LANES)), jnp.broadcast_to(i, (N_SEG, LANES)))
              for r, i in a_row]

    def sweep(lo, hi, carry, store):
        def step(i, carry):
            off = (SEG_CHUNKS - 1 - i) if backward else i
            rows = pl.ds(off, N_SEG, stride=SEG_CHUNKS)
            new = []
            for k, (cr, ci) in enumerate(carry):
                xr = loc_ref[k, rows, :]
                xi = loc_ref[N_SLAB + k, rows, :]
                if store:
                    out_ref[k, rows, :] = cr
                    out_ref[N_SLAB + k, rows, :] = ci
                nr, ni = _cmul(a_tile[k][0], a_tile[k][1], cr, ci)
                new.append((nr + xr, ni + xi))
            return tuple(new)
        return lax.fori_loop(lo, hi, step, carry, unroll=True)

    def full(carry, store):
        if not backward:
            return sweep(0, SEG_CHUNKS, carry, store)
        carry = sweep(0, BWD_RESET_STEP, carry, store)
        first = lax.broadcasted_iota(jnp.int32, (N_SEG, LANES), 0) == 0
        carry = tuple((jnp.where(first, 0.0, cr), jnp.where(first, 0.0, ci)) for cr, ci in carry)
        return sweep(BWD_RESET_STEP, SEG_CHUNKS, carry, store)

    zero_tile = jnp.zeros((N_SEG, LANES), F32)
    ends = full(tuple((zero_tile, zero_tile) for _ in range(N_SLAB)), False)

    starts = []
    for k, (er, ei) in enumerate(ends):
        pr, pi = _cpow(a_row[k][0], a_row[k][1], SEG_CHUNKS)
        if backward:
            order = range(N_SEG - 1, -1, -1)
            g = (er[0:1], ei[0:1])
        else:
            order = range(N_SEG)
            g = (jnp.zeros((1, LANES), F32), jnp.zeros((1, LANES), F32))
        rows_r = [None] * N_SEG
        rows_i = [None] * N_SEG
        for j in order:
            rows_r[j], rows_i[j] = g
            nr, ni = _cmul(pr, pi, g[0], g[1])
            g = (nr + er[j:j + 1], ni + ei[j:j + 1])
        starts.append((jnp.concatenate(rows_r, axis=0), jnp.concatenate(rows_i, axis=0)))
    full(tuple(starts), True)


def _s5_out_kernel(vl_ref, vc_ref, ar_ref, ai_ref, e_ref, t_ref, ft_ref, yl_ref, yc_ref, st_ref):
    n_dir = 2 * N_SLAB
    rows = [jnp.concatenate([vc_ref[pr], vl_ref[pr]], axis=0) for pr in range(S5_PAIRS)]
    for pr in range(S5_PAIRS):
        st = jnp.dot(rows[pr], e_ref[0, pr], preferred_element_type=F32)
        for comp in range(4):
            st_ref[comp * N_SLAB + pr] = st[:, comp * LANES:(comp + 1) * LANES]
    for d in range(2):
        st_d = st_ref.at[pl.ds(d * n_dir, n_dir)]
        _scan_boundary_states(st_d, st_d, ar_ref.at[pl.ds(d, 1)], ai_ref.at[pl.ds(d, 1)], d == 1)

    for pr in range(S5_PAIRS):
        st = jnp.concatenate([st_ref[comp * N_SLAB + pr] for comp in range(4)],
                             axis=-1).astype(BF16)
        y = (jnp.dot(rows[pr], t_ref[0, pr], preferred_element_type=F32)
             + lax.dot_general(st, ft_ref[0, pr], (((1,), (1,)), ((), ())),
                               preferred_element_type=F32))
        yc_ref[pr] = y[:CTX_CHUNKS]
        yl_ref[pr] = y[CTX_CHUNKS:]


def _s5_out(vp, al_re, al_im, e_mat, t_mat, ft_mat):
    pair_w = S5_CHUNK * PAIR_BLK
    ctx_blk0 = (R_LAT // S5_CHUNK) // CTX_CHUNKS
    return pl.pallas_call(
        _s5_out_kernel,
        grid=(S5_QT, B),
        in_specs=[
            pl.BlockSpec((S5_PAIRS, LAT_CHUNKS, pair_w), lambda q, b: (q, b, 0)),
            pl.BlockSpec((S5_PAIRS, CTX_CHUNKS, pair_w), lambda q, b: (q, ctx_blk0 + b, 0)),
            pl.BlockSpec((2, 1, S5_ST), lambda q, b: (q, 0, 0)),
            pl.BlockSpec((2, 1, S5_ST), lambda q, b: (q, 0, 0)),
            pl.BlockSpec((1, S5_PAIRS, pair_w, 4 * LANES), lambda q, b: (q, 0, 0, 0)),
            pl.BlockSpec((1, S5_PAIRS, pair_w, pair_w), lambda q, b: (q, 0, 0, 0)),
            pl.BlockSpec((1, S5_PAIRS, pair_w, 4 * LANES), lambda q, b: (q, 0, 0, 0)),
        ],
        out_specs=[
            pl.BlockSpec((S5_PAIRS, LAT_CHUNKS, pair_w), lambda q, b: (q, b, 0)),
            pl.BlockSpec((S5_PAIRS, CTX_CHUNKS, pair_w), lambda q, b: (q, b, 0)),
        ],
        out_shape=[
            jax.ShapeDtypeStruct((S5_QT * S5_PAIRS, B * LAT_CHUNKS, pair_w), F32),
            jax.ShapeDtypeStruct((S5_QT * S5_PAIRS, B * CTX_CHUNKS, pair_w), F32),
        ],
        scratch_shapes=[pltpu.VMEM((4 * N_SLAB, BATCH_CHUNKS, LANES), F32)],
        compiler_params=pltpu.CompilerParams(vmem_limit_bytes=VMEM_LIMIT),
        name="s5_out",
    )(vp, vp, al_re, al_im, e_mat, t_mat, ft_mat)


def _tail0_stages(z, ys, rest_ref, gate, gluw_ref, glub_ref, sg_g_ref, sg_b_ref, sgw_ref,
                  sgbias_ref, wout_ref, lng_ref, lnb_ref):
    ga = rest_ref[:, 0:S5_W].astype(F32)
    u = rest_ref[:, S5_W:2 * S5_W].astype(F32)
    v = rest_ref[:, 2 * S5_W:3 * S5_W].astype(F32)
    gb = rest_ref[:, 3 * S5_W:4 * S5_W].astype(F32)

    ya = _gelu_tanh(ys)
    glu = jnp.dot(ya.astype(BF16), gluw_ref[...], preferred_element_type=F32) + glub_ref[...]
    yield None
    ya = ya * _sigmoid(glu) * _silu(ga)
    yield None

    vn = _layer_norm(v, sg_g_ref[...], sg_b_ref[...]).astype(BF16)
    lane = lax.broadcasted_iota(jnp.int32, (SG_CHUNK, LANES), 1)
    first_head = lane < (SG_W // SG_HEADS)
    chunks = []
    for c in range(TN // SG_CHUNK):
        tiles = []
        for j in range(SG_W // LANES):
            vt = vn[c * SG_CHUNK:(c + 1) * SG_CHUNK, j * LANES:(j + 1) * LANES]
            zero = jnp.zeros_like(vt)
            v2 = jnp.concatenate([jnp.where(first_head, vt, zero), jnp.where(first_head, zero, vt)],
                                 axis=0)
            tiles.append(jnp.dot(sgw_ref[j], v2, preferred_element_type=F32))
        chunks.append(jnp.concatenate(tiles, axis=-1) + sgbias_ref[...])
    s = jnp.concatenate(chunks, axis=0)
    yb = u * s * _silu(gb)
    yield None

    mix = jnp.concatenate([ya, yb], axis=-1).astype(BF16)
    y = jnp.dot(mix, wout_ref[...], preferred_element_type=F32) * gate
    yield None
    yield _layer_norm(ALPHA * z + y, lng_ref[...], lnb_ref[...])


def _rope_tables():
    nf = HD // 4
    n_rows = N_LAT // GRID_W
    lane = jnp.arange(LANES)
    inv = ROPE_BASE ** (-(lane % nf).astype(F32) / nf)
    by_row = ((lane % HD) // (HD // 2) == 0)[None, :]
    sign = jnp.where((lane % (HD // 2)) < nf, -1.0, 1.0)[None, :]
    row_ang = jnp.arange(n_rows, dtype=F32)[:, None] * inv[None, :]
    col_ang = jnp.arange(GRID_W, dtype=F32)[:, None] * inv[None, :]
    zero = jnp.zeros((), F32)
    row_tab = jnp.stack([jnp.where(by_row, jnp.cos(row_ang), zero),
                         jnp.where(by_row, sign * jnp.sin(row_ang), zero)])
    col_tab = jnp.stack([jnp.where(by_row, zero, jnp.cos(col_ang)),
                         jnp.where(by_row, zero, sign * jnp.sin(col_ang))])
    return row_tab, col_tab


def _rope_tile(x, cos, sin, first_half):
    nf = HD // 4
    partner = jnp.where(first_half, pltpu.roll(x, LANES - nf, axis=1), pltpu.roll(x, nf, axis=1))
    return x * cos + partner * sin


def _mid_kernel(x_ref, ctx_ref, mod0_ref, mod1_ref, ysl_ref, ysc_ref, rest_ref, gluw_ref, glub_ref,
                sg_g_ref, sg_b_ref, sgw_ref, sgbias_ref, wout_ref, lng_ref, lnb_ref, w_ref,
                rtab_ref, ctab_ref, z1_ref, q_ref, k_ref, vt_ref, g_ref, zprev_ref, wbf_scr,
                wout_scr, ys_scr):
    i = pl.program_id(0)
    dq = N_HEADS * HD
    dkv = N_KV * HD

    @pl.when(i == 0)
    def _():
        zprev_ref[...] = jnp.zeros((TN, D), F32)
        wbf_scr[...] = w_ref[...].astype(BF16)
        wout_scr[...] = wout_ref[...].astype(BF16)

    j = jnp.maximum(i - 1, 0)
    m1 = mod1_ref[pl.ds(_block_mod_row(j), 1), :]
    h = (zprev_ref[...] * (1.0 + m1[:, D:2 * D]) + m1[:, :D]).astype(BF16)
    grid_row0 = (j % LAT_BLK_PER_BATCH) * (TN // GRID_W)
    table = lambda t: jnp.concatenate(
        [rtab_ref[t, pl.ds(grid_row0 + r, 1), :] + ctab_ref[t] for r in range(TN // GRID_W)],
        axis=0)
    cos = jnp.where(j < N_LAT_BLK, table(0), 1.0)
    sin = jnp.where(j < N_LAT_BLK, table(1), 0.0)
    lane = lax.broadcasted_iota(jnp.int32, (TN, LANES), 1)
    first_half = (lane % (HD // 2)) < (HD // 4)
    scale = HD ** -0.5 * LOG2E

    def project(c0, c1):
        return jnp.dot(h, wbf_scr[:, c0:c1], preferred_element_type=F32)

    def roped_tiles(p, mult):
        for c in range(p.shape[1] // LANES):
            r = _rope_tile(p[:, c * LANES:(c + 1) * LANES], cos, sin, first_half)
            yield c, (r * mult if mult != 1.0 else r).astype(BF16)

    t = jnp.minimum(i, N_BLK - 1)
    z = jnp.where(t < N_LAT_BLK, x_ref[...], ctx_ref[...])
    yp = jnp.where(t < N_LAT_BLK, ysl_ref[...], ysc_ref[...])
    for qt in range(S5_QT):
        for k in range(S5_CHUNK // S5_PAIRS):
            tiles = _block_transpose4([yp[qt * S5_PAIRS + pr][:, k * LANES:(k + 1) * LANES]
                                       for pr in range(S5_PAIRS)])
            for n, tile in enumerate(tiles):
                ys_scr[qt, pl.ds(k * S5_PAIRS + n, TN // S5_CHUNK, stride=S5_CHUNK), :] = tile
    ys = jnp.concatenate([ys_scr[qt] for qt in range(S5_QT)], axis=-1)
    gate = mod0_ref[pl.ds(_block_mod_row(t), 1), 2 * D:3 * D]
    tail = _tail0_stages(z, ys, rest_ref, gate, gluw_ref, glub_ref, sg_g_ref, sg_b_ref, sgw_ref,
                         sgbias_ref, wout_scr, lng_ref, lnb_ref)

    half_q = dq // 2
    p_q0 = project(0, half_q)
    next(tail)
    p_q1 = project(half_q, dq)
    next(tail)
    p_k = project(dq, dq + dkv)
    vt_ref[...] = lax.dot_general(wbf_scr[:, dq + dkv:dq + 2 * dkv], h, (((0,), (1,)), ((), ())),
                                  preferred_element_type=F32).astype(BF16)
    next(tail)
    next(tail)
    for c0, p in ((0, p_q0), (half_q, p_q1)):
        for c, r in roped_tiles(p, scale):
            q_ref[:, c0 + c * LANES:c0 + (c + 1) * LANES] = r
    for c, r in roped_tiles(p_k, 1.0):
        k_ref[:, c * LANES:(c + 1) * LANES] = r
    p_g = project(dq + 2 * dkv, ODD_IN)
    z1 = next(tail)
    z1_ref[...] = z1
    zprev_ref[...] = z1
    g_ref[...] = p_g.astype(BF16)


def _mid(x2, ctx2, mod0, mod1, ys_lat, ys_ctx, rest, glu_w, glu_b, sg_g, sg_b, sg_w, sg_bias,
         w_out, ln_g, ln_b, w_in1, row_tab, col_tab):
    dq = N_HEADS * HD
    dkv = N_KV * HD
    once = dict(pipeline_mode=pl.Buffered(1))
    row = lambda n: pl.BlockSpec((1, n), lambda i: (0, 0))
    lat = lambda i: (jnp.minimum(i, N_LAT_BLK - 1), 0)
    tail_blk = lambda i: (jnp.minimum(i, N_BLK - 1), 0)
    proj_blk = lambda i: (jnp.maximum(i - 1, 0), 0)
    return pl.pallas_call(
        _mid_kernel,
        grid=(N_BLK + 1,),
        in_specs=[
            pl.BlockSpec((TN, D), lat),
            pl.BlockSpec((R_CTX, D), lambda i: (0, 0), **once),
            pl.BlockSpec((8, 3 * D), lambda i: (0, 0)),
            pl.BlockSpec((8, 3 * D), lambda i: (0, 0)),
            pl.BlockSpec((S5_QT * S5_PAIRS, TN // S5_CHUNK, S5_CHUNK * PAIR_BLK),
                         lambda i: (0, jnp.minimum(i, N_LAT_BLK - 1), 0)),
            pl.BlockSpec((S5_QT * S5_PAIRS, R_CTX // S5_CHUNK, S5_CHUNK * PAIR_BLK),
                         lambda i: (0, 0, 0), **once),
            pl.BlockSpec((TN, EVEN_IN - S5_W), tail_blk),
            pl.BlockSpec((S5_W, S5_W), lambda i: (0, 0), **once),
            row(S5_W), row(SG_W), row(SG_W),
            pl.BlockSpec((SG_HEADS // 2, SG_CHUNK, 2 * SG_CHUNK), lambda i: (0, 0, 0)),
            pl.BlockSpec((SG_CHUNK, SG_W), lambda i: (0, 0)),
            pl.BlockSpec((S5_W + SG_W, D), lambda i: (0, 0), **once),
            row(D), row(D),
            pl.BlockSpec((D, ODD_IN), lambda i: (0, 0), **once),
            pl.BlockSpec((2, N_LAT // GRID_W, LANES), lambda i: (0, 0, 0)),
            pl.BlockSpec((2, GRID_W, LANES), lambda i: (0, 0, 0)),
        ],
        out_specs=[
            pl.BlockSpec((TN, D), tail_blk),
            pl.BlockSpec((TN, dq), proj_blk),
            pl.BlockSpec((TN, dkv), proj_blk),
            pl.BlockSpec((dkv, TN), lambda i: (0, jnp.maximum(i - 1, 0))),
            pl.BlockSpec((TN, dq), proj_blk),
        ],
        out_shape=[
            jax.ShapeDtypeStruct((R_ALL, D), F32),
            jax.ShapeDtypeStruct((R_ALL, dq), BF16),
            jax.ShapeDtypeStruct((R_ALL, dkv), BF16),
            jax.ShapeDtypeStruct((dkv, R_ALL), BF16),
            jax.ShapeDtypeStruct((R_ALL, dq), BF16),
        ],
        scratch_shapes=[pltpu.VMEM((TN, D), F32), pltpu.VMEM((D, ODD_IN), BF16),
                        pltpu.VMEM((S5_W + SG_W, D), BF16), pltpu.VMEM((S5_QT, TN, LANES), F32)],
        compiler_params=pltpu.CompilerParams(vmem_limit_bytes=VMEM_LIMIT),
        name="mid",
    )(x2, ctx2, mod0, mod1, ys_lat, ys_ctx, rest, glu_w, glu_b, sg_g, sg_b, sg_w, sg_bias, w_out,
      ln_g, ln_b, w_in1, row_tab, col_tab)


N_QBLK = N_LAT // ATT_BLK
GRP = N_HEADS // N_KV
ATT_SUB = 4
ATT_ROWS = ATT_SUB * ATT_BLK
N_QSTEP = N_QBLK // ATT_SUB
N_ATT_STEPS = B * N_QSTEP
assert ATT_ROWS == TN
assert WINDOW == ATT_BLK
ONES_ROWS = 16


def _attn_kernel(sink_ref, q_ref, kp_ref, kc_ref, kn_ref, kx_ref, vp_ref, vc_ref, vn_ref, vx_ref,
                 g_ref, z_ref, mod_ref, wout_ref, lng_ref, lnb_ref, out_ref, o_scr, wout_scr):
    step = pl.program_id(0)

    @pl.when(step == 0)
    def _():
        o_scr[...] = jnp.zeros((ATT_ROWS, N_HEADS * HD), BF16)
        wout_scr[...] = wout_ref[...].astype(BF16)

    t = jnp.maximum(step - 1, 0)
    gate = mod_ref[pl.ds(_block_mod_row(t), 1), 2 * D:3 * D]
    mix = (o_scr[...].astype(F32) * _silu(g_ref[...].astype(F32))).astype(BF16)
    y = jnp.dot(mix, wout_scr[...], preferred_element_type=F32) * gate
    out_ref[...] = _layer_norm(ALPHA * z_ref[...] + y, lng_ref[...], lnb_ref[...])

    i = jnp.minimum(step, N_ATT_STEPS - 1) % N_QSTEP
    half = ATT_BLK // 2
    n_win = 2 * ATT_BLK + half
    n_keys = n_win + N_CTX
    nq = GRP * half
    k_win = jnp.concatenate([kp_ref[...], kc_ref[...], kn_ref[...]], axis=0)
    vt_win = jnp.concatenate([vp_ref[...], vc_ref[...], vn_ref[...]], axis=1)
    k_ctx = kx_ref[...]
    vt_ctx = vx_ref[...]

    kpos = lax.broadcasted_iota(jnp.int32, (half, half), 0)
    qpos = lax.broadcasted_iota(jnp.int32, (half, half), 1)
    tile_q = lambda a: jnp.concatenate([a] * GRP, axis=1)
    neg = lambda ok: tile_q(jnp.where(ok, 0.0, NEG_INF))
    everyone = kpos >= 0
    qgrp = lax.broadcasted_iota(jnp.int32, (1, nq), 1) // half
    ones = jnp.ones((ONES_ROWS, n_keys), BF16)
    q = q_ref[...]

    halves = [(u, v) for u in range(ATT_SUB) for v in range(2)]
    units = [(n, h) for n in range(len(halves)) for h in range(N_KV)]
    k_all, vt_all, bias = [], [], []
    for u, v in halves:
        first = u * ATT_BLK + v * half
        k_all.append(jnp.concatenate([k_win[first:first + n_win], k_ctx], axis=0))
        vt_all.append(jnp.concatenate([vt_win[:, first:first + n_win], vt_ctx], axis=1))
        in_prev = [blk for blk in range(5) if first + blk * half < ATT_BLK]
        in_next = [blk for blk in range(5) if first + blk * half >= ATT_BLK + ATT_ROWS]
        blocks = {}
        for blk in sorted(set([0, 4] + in_prev + in_next)):
            ok = (kpos >= qpos) if blk == 0 else ((kpos <= qpos) if blk == 4 else everyone)
            if blk in in_prev:
                ok = ok & (i > 0)
            if blk in in_next:
                ok = ok & (i < N_QSTEP - 1)
            blocks[blk] = neg(ok)
        bias.append(blocks)
    scores = []
    for n, h in units:
        u, v = halves[n]
        r0 = u * ATT_BLK + v * half
        kh = k_all[n][:, h * HD:(h + 1) * HD]
        qh = jnp.concatenate(
            [q[r0:r0 + half, (h * GRP + g) * HD:(h * GRP + g + 1) * HD] for g in range(GRP)],
            axis=0)
        scores.append(lax.dot_general(kh, qh, (((1,), (1,)), ((), ())),
                                      preferred_element_type=F32))
    probs = []
    for (n, h), s in zip(units, scores):
        s = jnp.concatenate(
            [s[blk * half:(blk + 1) * half] + bias[n][blk] if blk in bias[n]
             else s[blk * half:(blk + 1) * half] for blk in range(5)] + [s[n_win:]], axis=0)
        sink = jnp.zeros((1, nq), F32)
        for g in range(GRP):
            sink = jnp.where(qgrp == g, sink_ref[h * GRP + g] * LOG2E, sink)
        m = jnp.maximum(jnp.max(s, axis=0, keepdims=True), sink)
        probs.append((jnp.exp2(s - m).astype(BF16), jnp.exp2(sink - m)))
    outs = [[None] * N_HEADS for _ in halves]
    for (n, h), (p, p_sink) in zip(units, probs):
        vt1 = jnp.concatenate([vt_all[n][h * HD:(h + 1) * HD], ones], axis=0)
        ov = jnp.dot(vt1, p, preferred_element_type=F32)
        o_t = ov[:HD] / (ov[HD:HD + 1] + p_sink)
        for g in range(GRP):
            outs[n][h * GRP + g] = o_t[:, g * half:(g + 1) * half]
    for u in range(ATT_SUB):
        o_t = jnp.concatenate(
            [jnp.concatenate([outs[2 * u][hh], outs[2 * u + 1][hh]], axis=1)
             for hh in range(N_HEADS)], axis=0)
        o_scr[u * ATT_BLK:(u + 1) * ATT_BLK, :] = jnp.transpose(o_t).astype(BF16)


def _attention_tail(sink, q, k, vt, g, z1, mod, w_out, ln_g, ln_b):
    dq = N_HEADS * HD
    dkv = N_KV * HD
    ctx_blk0 = R_LAT // N_CTX
    cur = lambda s: jnp.minimum(s, N_ATT_STEPS - 1)
    batch = lambda s: cur(s) // N_QSTEP
    qstep = lambda s: cur(s) % N_QSTEP
    prev_blk = lambda s: batch(s) * N_QBLK + jnp.maximum(ATT_SUB * qstep(s) - 1, 0)
    next_blk = lambda s: batch(s) * N_QBLK + jnp.minimum(ATT_SUB * (qstep(s) + 1), N_QBLK - 1)
    tail = lambda s: jnp.maximum(s - 1, 0)
    once = dict(pipeline_mode=pl.Buffered(1))
    row = lambda n: pl.BlockSpec((1, n), lambda s, sk: (0, 0))
    return pl.pallas_call(
        _attn_kernel,
        grid_spec=pltpu.PrefetchScalarGridSpec(
            num_scalar_prefetch=1,
            grid=(N_ATT_STEPS + 1,),
            in_specs=[
                pl.BlockSpec((ATT_ROWS, dq), lambda s, sk: (cur(s), 0)),
                pl.BlockSpec((ATT_BLK, dkv), lambda s, sk: (prev_blk(s), 0)),
                pl.BlockSpec((ATT_ROWS, dkv), lambda s, sk: (cur(s), 0)),
                pl.BlockSpec((ATT_BLK, dkv), lambda s, sk: (next_blk(s), 0)),
                pl.BlockSpec((N_CTX, dkv), lambda s, sk: (ctx_blk0 + batch(s), 0)),
                pl.BlockSpec((dkv, ATT_BLK), lambda s, sk: (0, prev_blk(s))),
                pl.BlockSpec((dkv, ATT_ROWS), lambda s, sk: (0, cur(s))),
                pl.BlockSpec((dkv, ATT_BLK), lambda s, sk: (0, next_blk(s))),
                pl.BlockSpec((dkv, N_CTX), lambda s, sk: (0, ctx_blk0 + batch(s))),
                pl.BlockSpec((ATT_ROWS, dq), lambda s, sk: (tail(s), 0)),
                pl.BlockSpec((ATT_ROWS, D), lambda s, sk: (tail(s), 0)),
                pl.BlockSpec((8, 3 * D), lambda s, sk: (0, 0)),
                pl.BlockSpec((dq, D), lambda s, sk: (0, 0), **once),
                row(D), row(D),
            ],
            out_specs=pl.BlockSpec((ATT_ROWS, D), lambda s, sk: (tail(s), 0)),
            scratch_shapes=[pltpu.VMEM((ATT_ROWS, dq), BF16), pltpu.VMEM((dq, D), BF16)],
        ),
        out_shape=jax.ShapeDtypeStruct((R_LAT, D), F32),
        compiler_params=pltpu.CompilerParams(vmem_limit_bytes=VMEM_LIMIT),
        name="attention",
    )(sink, q, k, k, k, k, vt, vt, vt, vt, g, z1, mod, w_out, ln_g, ln_b)


def kernel(x, c, ctx, c_ctx, mod_w, mod_b, ln_g, ln_b, e_w_in, e_w_out, s5_lam_re, s5_lam_im,
           s5_log_dt, s5_b_re, s5_b_im, s5_c_re, s5_c_im, s5_d, glu_w, glu_b, sg_ln_g, sg_ln_b,
           sg_w, sg_b, o_w_in, o_w_out, o_sink):
    x2 = x.reshape(R_LAT, D)
    ctx2 = ctx.reshape(R_CTX, D)
    mod = _modulation(c, c_ctx, mod_w, mod_b)

    vp, rest = _in0(x2, ctx2, mod[0], e_w_in[0])
    t_mat, e_mat, ft_mat, al_re, al_im = _s5_prep(
        s5_lam_re[0], s5_lam_im[0], s5_log_dt[0], s5_b_re[0], s5_b_im[0], s5_c_re[0], s5_c_im[0],
        s5_d[0])
    ys_lat, ys_ctx = _s5_out(vp, al_re, al_im, e_mat, t_mat, ft_mat)
    sg_bias = jnp.repeat(jnp.transpose(sg_b[0]), SG_W // SG_HEADS, axis=1)
    sg_w2 = jnp.transpose(sg_w[0].reshape(SG_HEADS // 2, 2, SG_CHUNK, SG_CHUNK), (0, 2, 1, 3))
    sg_w2 = sg_w2.reshape(SG_HEADS // 2, SG_CHUNK, 2 * SG_CHUNK).astype(BF16)
    row_tab, col_tab = _rope_tables()
    z1, q, k, vt, g = _mid(
        x2, ctx2, mod[0], mod[1], ys_lat, ys_ctx, rest, glu_w[0].astype(BF16),
        glu_b[0].reshape(1, S5_W), sg_ln_g[0].reshape(1, SG_W), sg_ln_b[0].reshape(1, SG_W),
        sg_w2, sg_bias, e_w_out[0], ln_g[0].reshape(1, D),
        ln_b[0].reshape(1, D), o_w_in[0], row_tab, col_tab)

    out = _attention_tail(o_sink[0], q, k, vt, g, z1, mod[1], o_w_out[0],
                          ln_g[1].reshape(1, D), ln_b[1].reshape(1, D))
    return out.reshape(B, N_LAT, D)
```

```python
import math

import jax
import jax.numpy as jnp
from jax import lax
from jax.experimental import pallas as pl
from jax.experimental.pallas import tpu as pltpu

F32 = jnp.float32
BF16 = jnp.bfloat16

D = 1024
B = 2
N_LAT = 8192
N_CTX = 256
DEPTH = 2
GRID_W = 64
S5_W = 512
S5_GC = 16
S5_G = 32
S5_P = 64
SG_W = 512
SG_HEADS = 8
SG_CHUNK = 128
N_HEADS = 16
N_KV = 4
HD = 64
WINDOW = 128
ATT_BLK = 128
ROPE_BASE = 10000.0
NEG_INF = -1e30
LN_EPS = 1e-5
ALPHA = (2 * DEPTH) ** 0.25
LOG2E = math.log2(math.e)
EVEN_IN = 2 * S5_W + 3 * SG_W
ODD_IN = 2 * N_HEADS * HD + 2 * N_KV * HD

LANES = 128
VMEM_LIMIT = 56 * 1024 * 1024

R_LAT = B * N_LAT
R_CTX = B * N_CTX
R_ALL = R_LAT + R_CTX
TN = 512
N_LAT_BLK = R_LAT // TN
N_BLK = R_ALL // TN
LAT_BLK_PER_BATCH = N_LAT // TN
CTX_MOD_ROW = B

S5_CHUNK = 8
S5_QT = S5_W // LANES
S5_GPT = LANES // S5_GC
S5_ST = S5_GPT * S5_P
S5_PAIRS = S5_GPT // 2
PAIR_BLK = 2 * S5_GC
LAT_CHUNKS = N_LAT // S5_CHUNK
CTX_CHUNKS = N_CTX // S5_CHUNK
BATCH_CHUNKS = CTX_CHUNKS + LAT_CHUNKS
N_SEG = 16
SEG_CHUNKS = BATCH_CHUNKS // N_SEG
BWD_RESET_STEP = LAT_CHUNKS - (N_SEG - 1) * SEG_CHUNKS


def _sigmoid(x):
    return 1.0 / (1.0 + jnp.exp(-x))


def _silu(x):
    return x * _sigmoid(x)


def _gelu_tanh(x):
    return 0.5 * x * (1.0 + jnp.tanh(math.sqrt(2.0 / math.pi) * (x + 0.044715 * (x * x * x))))


def _layer_norm(x, g, b):
    mu = jnp.mean(x, axis=-1, keepdims=True)
    xc = x - mu
    var = jnp.mean(xc * xc, axis=-1, keepdims=True)
    return xc * lax.rsqrt(var + LN_EPS) * g + b


def _split_bf16(x):
    hi = x.astype(BF16)
    return hi, (x - hi.astype(F32)).astype(BF16)


def _dot_split(x, y_split, dims):
    x_hi, x_lo = _split_bf16(x)
    y_hi, y_lo = y_split
    dot = lambda a, b: lax.dot_general(a, b, dims, preferred_element_type=F32)
    return dot(x_hi, y_hi) + (dot(x_hi, y_lo) + dot(x_lo, y_hi))


def _block_mod_row(i):
    return jnp.where(i < N_LAT_BLK, i // LAT_BLK_PER_BATCH, CTX_MOD_ROW)


def _mod_kernel(cv_ref, w_ref, b_ref, o_ref):
    s = _silu(cv_ref[...])
    o_ref[0] = _dot_split(s, _split_bf16(w_ref[0]), (((1,), (0,)), ((), ()))) + b_ref[0]


def _modulation(c, c_ctx, mod_w, mod_b):
    cv = jnp.concatenate([c, c_ctx[None], jnp.zeros((8 - B - 1, D), F32)], axis=0)
    tn = 1024
    return pl.pallas_call(
        _mod_kernel,
        grid=(DEPTH, 3 * D // tn),
        in_specs=[
            pl.BlockSpec((8, D), lambda l, j: (0, 0)),
            pl.BlockSpec((1, D, tn), lambda l, j: (l, 0, j)),
            pl.BlockSpec((1, 1, tn), lambda l, j: (l, 0, j)),
        ],
        out_specs=pl.BlockSpec((1, 8, tn), lambda l, j: (l, 0, j)),
        out_shape=jax.ShapeDtypeStruct((DEPTH, 8, 3 * D), F32),
        compiler_params=pltpu.CompilerParams(vmem_limit_bytes=VMEM_LIMIT),
        name="modulation",
    )(cv, mod_w, mod_b.reshape(DEPTH, 1, 3 * D))


def _block_transpose4(tiles):
    tiles = list(tiles)
    blk = lax.broadcasted_iota(jnp.int32, tiles[0].shape, 1) // PAIR_BLK
    for dist in (2, 1):
        keep = (blk & dist) == 0
        for i in range(len(tiles)):
            if i & dist:
                continue
            lo, hi = tiles[i], tiles[i + dist]
            tiles[i] = jnp.where(keep, lo, pltpu.roll(hi, dist * PAIR_BLK, axis=1))
            tiles[i + dist] = jnp.where(keep, pltpu.roll(lo, LANES - dist * PAIR_BLK, axis=1), hi)
    return tiles


def _in0_kernel(x_ref, ctx_ref, mod_ref, w_ref, vp_ref, rest_ref, xa_scr, wbf_scr):
    L = S5_CHUNK
    i = pl.program_id(0)

    @pl.when(i == 0)
    def _():
        wbf_scr[...] = w_ref[...].astype(BF16)

    z = jnp.where(i < N_LAT_BLK, x_ref[...], ctx_ref[...])
    m = mod_ref[pl.ds(_block_mod_row(i), 1), :]
    h = z * (1.0 + m[:, D:2 * D]) + m[:, :D]
    p = jnp.dot(h.astype(BF16), wbf_scr[...], preferred_element_type=F32)
    rest_ref[...] = p[:, S5_W:].astype(BF16)
    for q in range(S5_QT):
        xa_scr[q] = p[:, q * LANES:(q + 1) * LANES]
        tiles = [xa_scr[q, pl.ds(t, TN // L, stride=L), :] for t in range(L)]
        halves = [_block_transpose4(tiles[k * S5_PAIRS:(k + 1) * S5_PAIRS])
                  for k in range(L // S5_PAIRS)]
        for pr in range(S5_PAIRS):
            vp_ref[q * S5_PAIRS + pr] = jnp.concatenate(
                [half[pr] for half in halves], axis=-1).astype(BF16)


def _in0(x2, ctx2, mod, w_in):
    pair_w = S5_CHUNK * PAIR_BLK
    return pl.pallas_call(
        _in0_kernel,
        grid=(N_BLK,),
        in_specs=[
            pl.BlockSpec((TN, D), lambda i: (jnp.minimum(i, N_LAT_BLK - 1), 0)),
            pl.BlockSpec((R_CTX, D), lambda i: (0, 0)),
            pl.BlockSpec((8, 3 * D), lambda i: (0, 0)),
            pl.BlockSpec((D, EVEN_IN), lambda i: (0, 0)),
        ],
        out_specs=[
            pl.BlockSpec((S5_QT * S5_PAIRS, TN // S5_CHUNK, pair_w), lambda i: (0, i, 0)),
            pl.BlockSpec((TN, EVEN_IN - S5_W), lambda i: (i, 0)),
        ],
        out_shape=[
            jax.ShapeDtypeStruct((S5_QT * S5_PAIRS, R_ALL // S5_CHUNK, pair_w), BF16),
            jax.ShapeDtypeStruct((R_ALL, EVEN_IN - S5_W), BF16),
        ],
        scratch_shapes=[pltpu.VMEM((S5_QT, TN, LANES), F32), pltpu.VMEM((D, EVEN_IN), BF16)],
        compiler_params=pltpu.CompilerParams(vmem_limit_bytes=VMEM_LIMIT),
        name="in0",
    )(x2, ctx2, mod, w_in)


def _s5_prep_kernel(rows_ref, bre_ref, bim_ref, cre_ref, cim_ref, dsk_ref, t_ref, e_ref, ft_ref,
                    alr_ref, ali_ref):
    L = S5_CHUNK
    tile = (LANES, S5_ST)
    same_group = (lax.broadcasted_iota(jnp.int32, tile, 0) // S5_GC
                  == lax.broadcasted_iota(jnp.int32, tile, 1) // S5_P)
    nt_dims = (((1,), (1,)), ((), ()))
    nn_dims = (((1,), (0,)), ((), ()))
    pick_ch = (lax.broadcasted_iota(jnp.int32, (LANES, S5_GC), 0) % S5_GC
               == lax.broadcasted_iota(jnp.int32, (LANES, S5_GC), 1)).astype(BF16)
    rep_st = (lax.broadcasted_iota(jnp.int32, (S5_P, S5_ST), 0)
              == lax.broadcasted_iota(jnp.int32, (S5_P, S5_ST), 1) % S5_P).astype(BF16)

    def spread_b(b):
        hi, lo = _split_bf16(b)
        dot = lambda v: lax.dot_general(pick_ch, v, nt_dims, preferred_element_type=F32)
        return jnp.where(same_group, dot(hi) + dot(lo), 0.0)

    def spread_c(c):
        hi, lo = _split_bf16(c)
        dot = lambda v: lax.dot_general(v, rep_st, nn_dims, preferred_element_type=F32)
        return jnp.where(same_group, dot(hi) + dot(lo), 0.0)

    pair_rows = lambda pr: slice(pr * PAIR_BLK, (pr + 1) * PAIR_BLK)
    pair_lanes = lambda pr: slice(pr * LANES, (pr + 1) * LANES)
    lag_strip = []
    for d in range(2):
        lam_re = rows_ref[d, 0]
        lam_im = rows_ref[d, 1]
        dt = jnp.exp(rows_ref[d, 2])
        pw = []
        for k in range(L + 1):
            mag = jnp.exp(lam_re * dt * float(k))
            ang = lam_im * dt * float(k)
            pw.append((mag * jnp.cos(ang), mag * jnp.sin(ang)))
        alr_ref[d] = pw[L][0]
        ali_ref[d] = pw[L][1]
        den = lam_re * lam_re + lam_im * lam_im
        nr = pw[1][0] - 1.0
        ni = pw[1][1]
        f_re = (nr * lam_re + ni * lam_im) / den
        f_im = (ni * lam_re - nr * lam_im) / den
        b_re = spread_b(bre_ref[d])
        b_im = spread_b(bim_ref[d])
        bb_re = f_re * b_re - f_im * b_im
        bb_im = f_re * b_im + f_im * b_re
        c_re = spread_c(cre_ref[d])
        c_im = spread_c(cim_ref[d])
        col_re = slice(2 * d * LANES, (2 * d + 1) * LANES)
        col_im = slice((2 * d + 1) * LANES, (2 * d + 2) * LANES)
        strips = []
        for pr in range(S5_PAIRS):
            blk = lambda a: a[pair_rows(pr), pair_lanes(pr)]
            p_re = [blk(jnp.broadcast_to(pw[k][0], tile)) for k in range(L + 1)]
            p_im = [blk(jnp.broadcast_to(pw[k][1], tile)) for k in range(L + 1)]
            bbr, bbi, cr, ci = blk(bb_re), blk(bb_im), blk(c_re), blk(c_im)
            for k in range(L):
                s = (L - 1 - k) if d == 0 else k
                rows = slice(s * PAIR_BLK, (s + 1) * PAIR_BLK)
                e_ref[0, pr, rows, col_re] = (p_re[k] * bbr - p_im[k] * bbi).astype(BF16)
                e_ref[0, pr, rows, col_im] = (p_re[k] * bbi + p_im[k] * bbr).astype(BF16)
            cw = [(cr * p_re[k] - ci * p_im[k], -(cr * p_im[k] + ci * p_re[k]))
                  for k in range(L + 1)]
            for t in range(L):
                k = (t + 1) if d == 0 else (L - t)
                rows = slice(t * PAIR_BLK, (t + 1) * PAIR_BLK)
                ft_ref[0, pr, rows, col_re] = cw[k][0].astype(BF16)
                ft_ref[0, pr, rows, col_im] = cw[k][1].astype(BF16)
            lags = range(L) if d == 0 else range(L - 1, -1, -1)
            c_stack = jnp.concatenate(
                [jnp.concatenate([cw[k][0], cw[k][1]], axis=-1) for k in lags], axis=0)
            strips.append(_dot_split(jnp.concatenate([bbr, bbi], axis=-1),
                                     _split_bf16(c_stack), nt_dims))
        lag_strip.append(strips)

    strip_w = 2 * L * PAIR_BLK
    mid = (L - 1) * PAIR_BLK
    widen = lambda a: jnp.concatenate([a, jnp.zeros((a.shape[0], strip_w - a.shape[1]), F32)], axis=-1)
    row = lax.broadcasted_iota(jnp.int32, (PAIR_BLK, strip_w), 0)
    lane = lax.broadcasted_iota(jnp.int32, (PAIR_BLK, strip_w), 1)
    d_row = widen(dsk_ref[0])
    for pr in range(S5_PAIRS):
        strip = (widen(lag_strip[1][pr])
                 + pltpu.roll(widen(lag_strip[0][pr]), mid, axis=1)
                 + jnp.where(lane == row + mid,
                             pltpu.roll(d_row, mid - pr * PAIR_BLK, axis=1), 0.0))
        for s in range(L):
            shift = (strip_w - (L - 1 - s) * PAIR_BLK) % strip_w
            blk = pltpu.roll(strip, shift, axis=1) if shift else strip
            t_ref[0, pr, s * PAIR_BLK:(s + 1) * PAIR_BLK, :] = blk[:, :L * PAIR_BLK].astype(BF16)


def _s5_prep(lam_re, lam_im, log_dt, b_re, b_im, c_re, c_im, d_skip):
    L = S5_CHUNK
    n_st = S5_G * S5_P
    pair_w = L * PAIR_BLK
    rows = jnp.stack([lam_re.reshape(2, 1, n_st), lam_im.reshape(2, 1, n_st),
                      jnp.repeat(log_dt, S5_P, axis=1).reshape(2, 1, n_st)], axis=1)
    b_spec = pl.BlockSpec((2, S5_ST, S5_GC), lambda q: (0, q, 0))
    c_spec = pl.BlockSpec((2, LANES, S5_P), lambda q: (0, q, 0))
    return pl.pallas_call(
        _s5_prep_kernel,
        grid=(S5_QT,),
        in_specs=[
            pl.BlockSpec((2, 3, 1, S5_ST), lambda q: (0, 0, 0, q)),
            b_spec, b_spec, c_spec, c_spec,
            pl.BlockSpec((1, 1, LANES), lambda q: (q, 0, 0)),
        ],
        out_specs=[
            pl.BlockSpec((1, S5_PAIRS, pair_w, pair_w), lambda q: (q, 0, 0, 0)),
            pl.BlockSpec((1, S5_PAIRS, pair_w, 4 * LANES), lambda q: (q, 0, 0, 0)),
            pl.BlockSpec((1, S5_PAIRS, pair_w, 4 * LANES), lambda q: (q, 0, 0, 0)),
            pl.BlockSpec((2, 1, S5_ST), lambda q: (q, 0, 0)),
            pl.BlockSpec((2, 1, S5_ST), lambda q: (q, 0, 0)),
        ],
        out_shape=[
            jax.ShapeDtypeStruct((S5_QT, S5_PAIRS, pair_w, pair_w), BF16),
            jax.ShapeDtypeStruct((S5_QT, S5_PAIRS, pair_w, 4 * LANES), BF16),
            jax.ShapeDtypeStruct((S5_QT, S5_PAIRS, pair_w, 4 * LANES), BF16),
            jax.ShapeDtypeStruct((2 * S5_QT, 1, S5_ST), F32),
            jax.ShapeDtypeStruct((2 * S5_QT, 1, S5_ST), F32),
        ],
        compiler_params=pltpu.CompilerParams(vmem_limit_bytes=VMEM_LIMIT),
        name="s5_prep",
    )(rows, b_re.reshape(2, n_st, S5_GC), b_im.reshape(2, n_st, S5_GC),
      c_re.reshape(2, S5_W, S5_P), c_im.reshape(2, S5_W, S5_P), d_skip.reshape(S5_QT, 1, LANES))


N_SLAB = S5_ST // LANES


def _cmul(ar, ai, br, bi):
    return ar * br - ai * bi, ar * bi + ai * br


def _cpow(ar, ai, n):
    res = None
    while n:
        if n & 1:
            res = (ar, ai) if res is None else _cmul(res[0], res[1], ar, ai)
        n >>= 1
        if n:
            ar, ai = _cmul(ar, ai, ar, ai)
    return res


def _scan_boundary_states(loc_ref, out_ref, ar_ref, ai_ref, backward, holds=()):
    a_row = [(ar_ref[0][:, k * LANES:(k + 1) * LANES], ai_ref[0][:, k * LANES:(k + 1) * LANES])
             for k in range(N_SLAB)]
    a_tile = [(jnp.broadcast_to(r, (N_SEG, LANES)), jnp.broadcast_to(i, (N_SEG, LANES)))
              for r, i in a_row]

    def sweep(lo, hi, carry, store):
        def step(i, carry):
            off = (SEG_CHUNKS - 1 - i) if backward else i
            rows = pl.ds(off, N_SEG, stride=SEG_CHUNKS)
            new = []
            for k, (cr, ci) in enumerate(carry):
                xr = loc_ref[k, rows, :]
                xi = loc_ref[N_SLAB + k, rows, :]
                if store:
                    out_ref[k, rows, :] = cr
                    out_ref[N_SLAB + k, rows, :] = ci
                nr, ni = _cmul(a_tile[k][0], a_tile[k][1], cr, ci)
                new.append((nr + xr, ni + xi))
            return tuple(new)
        return lax.fori_loop(lo, hi, step, carry, unroll=True)

    def reset(carry):
        first = lax.broadcasted_iota(jnp.int32, (N_SEG, LANES), 0) == 0
        return tuple((jnp.where(first, 0.0, cr), jnp.where(first, 0.0, ci)) for cr, ci in carry)

    def hold(mask):
        return lambda carry: tuple((jnp.where(mask, cr, 0.0), jnp.where(mask, ci, 0.0))
                                   for cr, ci in carry)

    def full(carry, store):
        cuts = [(BWD_RESET_STEP, reset)] if backward else []
        if store:
            cuts += [(at, hold(mask)) for at, mask in holds]
        lo = 0
        for at, fn in sorted(cuts, key=lambda c: c[0]):
            carry = fn(sweep(lo, at, carry, store))
            lo = at
        return sweep(lo, SEG_CHUNKS, carry, store)

    zero_tile = jnp.zeros((N_SEG, LANES), F32)
    ends = full(tuple((zero_tile, zero_tile) for _ in range(N_SLAB)), False)

    starts = []
    for k, (er, ei) in enumerate(ends):
        pr, pi = _cpow(a_row[k][0], a_row[k][1], SEG_CHUNKS)
        if backward:
            order = range(N_SEG - 1, -1, -1)
            g = (er[0:1], ei[0:1])
        else:
            order = range(N_SEG)
            g = (jnp.zeros((1, LANES), F32), jnp.zeros((1, LANES), F32))
        rows_r = [None] * N_SEG
        rows_i = [None] * N_SEG
        for j in order:
            rows_r[j], rows_i[j] = g
            nr, ni = _cmul(pr, pi, g[0], g[1])
            g = (nr + er[j:j + 1], ni + ei[j:j + 1])
        starts.append((jnp.concatenate(rows_r, axis=0), jnp.concatenate(rows_i, axis=0)))
    full(tuple(starts), True)


HOLD_STEP = 40


def _s5_out_kernel(vl_ref, vc_ref, vlp_ref, vcp_ref, ar_ref, ai_ref, e_ref, t_ref, ft_ref,
                   yl_ref, yc_ref, st0_ref, st1_ref):
    s = pl.program_id(0)
    n_dir = 2 * N_SLAB

    @pl.when(s == 0)
    def _():
        for k in range(2 * n_dir):
            st1_ref[k] = jnp.zeros((BATCH_CHUNKS, LANES), F32)

    def step(cur_ref, prev_ref):
        rows = [jnp.concatenate([vc_ref[pr], vl_ref[pr]], axis=0) for pr in range(S5_PAIRS)]
        for pr in range(S5_PAIRS):
            st = jnp.dot(rows[pr], e_ref[0, pr], preferred_element_type=F32)
            for comp in range(4):
                cur_ref[comp * N_SLAB + pr] = st[:, comp * LANES:(comp + 1) * LANES]

        holds = []
        for pr in range(S5_PAIRS):
            rows_p = jnp.concatenate([vcp_ref[pr], vlp_ref[pr]], axis=0)
            st = jnp.concatenate([prev_ref[comp * N_SLAB + pr] for comp in range(4)],
                                 axis=-1).astype(BF16)
            y = (jnp.dot(rows_p, t_ref[0, pr], preferred_element_type=F32)
                 + lax.dot_general(st, ft_ref[0, pr], (((1,), (1,)), ((), ())),
                                   preferred_element_type=F32))
            yc_ref[pr] = y[:CTX_CHUNKS]
            yl_ref[pr] = y[CTX_CHUNKS:]
            last = y[BATCH_CHUNKS - N_SEG:, :LANES]
            holds.append((last == last) | (last != last))
        holds = [(HOLD_STEP, (holds[0] & holds[1]) & (holds[2] & holds[3]))]

        for d in range(2):
            st_d = cur_ref.at[pl.ds(d * n_dir, n_dir)]
            _scan_boundary_states(st_d, st_d, ar_ref.at[pl.ds(d, 1)], ai_ref.at[pl.ds(d, 1)],
                                  d == 1, holds)

    @pl.when(s % 2 == 0)
    def _():
        step(st0_ref, st1_ref)

    @pl.when(s % 2 == 1)
    def _():
        step(st1_ref, st0_ref)


def _s5_out(vp, al_re, al_im, e_mat, t_mat, ft_mat):
    pair_w = S5_CHUNK * PAIR_BLK
    ctx_blk0 = (R_LAT // S5_CHUNK) // CTX_CHUNKS
    n_steps = S5_QT * B
    cur = lambda s: jnp.minimum(s, n_steps - 1)
    prev = lambda s: jnp.maximum(s - 1, 0)
    lat = lambda at: pl.BlockSpec((S5_PAIRS, LAT_CHUNKS, pair_w),
                                  lambda s: (at(s) // B, at(s) % B, 0))
    ctx = lambda at, blk0: pl.BlockSpec((S5_PAIRS, CTX_CHUNKS, pair_w),
                                        lambda s: (at(s) // B, blk0 + at(s) % B, 0))
    mat = lambda at, n: pl.BlockSpec((1, S5_PAIRS, pair_w, n), lambda s: (at(s) // B, 0, 0, 0))
    alpha = pl.BlockSpec((2, 1, S5_ST), lambda s: (cur(s) // B, 0, 0))
    return pl.pallas_call(
        _s5_out_kernel,
        grid=(n_steps + 1,),
        in_specs=[
            lat(cur), ctx(cur, ctx_blk0), lat(prev), ctx(prev, ctx_blk0), alpha, alpha,
            mat(cur, 4 * LANES), mat(prev, pair_w), mat(prev, 4 * LANES),
        ],
        out_specs=[lat(prev), ctx(prev, 0)],
        out_shape=[
            jax.ShapeDtypeStruct((S5_QT * S5_PAIRS, B * LAT_CHUNKS, pair_w), F32),
            jax.ShapeDtypeStruct((S5_QT * S5_PAIRS, B * CTX_CHUNKS, pair_w), F32),
        ],
        scratch_shapes=[pltpu.VMEM((4 * N_SLAB, BATCH_CHUNKS, LANES), F32)] * 2,
        compiler_params=pltpu.CompilerParams(vmem_limit_bytes=VMEM_LIMIT),
        name="s5_out",
    )(vp, vp, vp, vp, al_re, al_im, e_mat, t_mat, ft_mat)


def _tail0_stages(z, ys, rest_ref, gate, gluw_ref, glub_ref, sg_g_ref, sg_b_ref, sgw_ref,
                  sgbias_ref, wout_ref, lng_ref, lnb_ref):
    ga = rest_ref[:, 0:S5_W].astype(F32)
    u = rest_ref[:, S5_W:2 * S5_W].astype(F32)
    v = rest_ref[:, 2 * S5_W:3 * S5_W].astype(F32)
    gb = rest_ref[:, 3 * S5_W:4 * S5_W].astype(F32)

    ya = _gelu_tanh(ys)
    glu = jnp.dot(ya.astype(BF16), gluw_ref[...], preferred_element_type=F32) + glub_ref[...]
    yield None
    ya = ya * _sigmoid(glu) * _silu(ga)
    yield None

    vn = _layer_norm(v, sg_g_ref[...], sg_b_ref[...]).astype(BF16)
    lane = lax.broadcasted_iota(jnp.int32, (SG_CHUNK, LANES), 1)
    first_head = lane < (SG_W // SG_HEADS)
    chunks = []
    for c in range(TN // SG_CHUNK):
        tiles = []
        for j in range(SG_W // LANES):
            vt = vn[c * SG_CHUNK:(c + 1) * SG_CHUNK, j * LANES:(j + 1) * LANES]
            zero = jnp.zeros_like(vt)
            v2 = jnp.concatenate([jnp.where(first_head, vt, zero), jnp.where(first_head, zero, vt)],
                                 axis=0)
            tiles.append(jnp.dot(sgw_ref[j], v2, preferred_element_type=F32))
        chunks.append(jnp.concatenate(tiles, axis=-1) + sgbias_ref[...])
    s = jnp.concatenate(chunks, axis=0)
    yb = u * s * _silu(gb)
    yield None

    mix = jnp.concatenate([ya, yb], axis=-1).astype(BF16)
    y = jnp.dot(mix, wout_ref[...], preferred_element_type=F32) * gate
    yield None
    yield _layer_norm(ALPHA * z + y, lng_ref[...], lnb_ref[...])


def _rope_tables():
    nf = HD // 4
    n_rows = N_LAT // GRID_W
    lane = jnp.arange(LANES)
    inv = ROPE_BASE ** (-(lane % nf).astype(F32) / nf)
    by_row = ((lane % HD) // (HD // 2) == 0)[None, :]
    sign = jnp.where((lane % (HD // 2)) < nf, -1.0, 1.0)[None, :]
    row_ang = jnp.arange(n_rows, dtype=F32)[:, None] * inv[None, :]
    col_ang = jnp.arange(GRID_W, dtype=F32)[:, None] * inv[None, :]
    zero = jnp.zeros((), F32)
    row_tab = jnp.stack([jnp.where(by_row, jnp.cos(row_ang), zero),
                         jnp.where(by_row, sign * jnp.sin(row_ang), zero)])
    col_tab = jnp.stack([jnp.where(by_row, zero, jnp.cos(col_ang)),
                         jnp.where(by_row, zero, sign * jnp.sin(col_ang))])
    return row_tab, col_tab


def _rope_tile(x, cos, sin, first_half):
    nf = HD // 4
    partner = jnp.where(first_half, pltpu.roll(x, LANES - nf, axis=1), pltpu.roll(x, nf, axis=1))
    return x * cos + partner * sin


def _mid_kernel(x_ref, ctx_ref, mod0_ref, mod1_ref, ysl_ref, ysc_ref, rest_ref, gluw_ref, glub_ref,
                sg_g_ref, sg_b_ref, sgw_ref, sgbias_ref, wout_ref, lng_ref, lnb_ref, w_ref,
                rtab_ref, ctab_ref, z1_ref, q_ref, k_ref, vt_ref, g_ref, zprev_ref, wbf_scr,
                wout_scr, ys_scr):
    i = pl.program_id(0)
    dq = N_HEADS * HD
    dkv = N_KV * HD

    @pl.when(i == 0)
    def _():
        zprev_ref[...] = jnp.zeros((TN, D), F32)
        wbf_scr[...] = w_ref[...].astype(BF16)
        wout_scr[...] = wout_ref[...].astype(BF16)

    j = jnp.maximum(i - 1, 0)
    m1 = mod1_ref[pl.ds(_block_mod_row(j), 1), :]
    h = (zprev_ref[...] * (1.0 + m1[:, D:2 * D]) + m1[:, :D]).astype(BF16)
    grid_row0 = (j % LAT_BLK_PER_BATCH) * (TN // GRID_W)
    table = lambda t: jnp.concatenate(
        [rtab_ref[t, pl.ds(grid_row0 + r, 1), :] + ctab_ref[t] for r in range(TN // GRID_W)],
        axis=0)
    cos = jnp.where(j < N_LAT_BLK, table(0), 1.0)
    sin = jnp.where(j < N_LAT_BLK, table(1), 0.0)
    lane = lax.broadcasted_iota(jnp.int32, (TN, LANES), 1)
    first_half = (lane % (HD // 2)) < (HD // 4)
    scale = HD ** -0.5 * LOG2E

    def project(c0, c1):
        return jnp.dot(h, wbf_scr[:, c0:c1], preferred_element_type=F32)

    def roped_tiles(p, mult):
        for c in range(p.shape[1] // LANES):
            r = _rope_tile(p[:, c * LANES:(c + 1) * LANES], cos, sin, first_half)
            yield c, (r * mult if mult != 1.0 else r).astype(BF16)

    t = jnp.minimum(i, N_BLK - 1)
    z = jnp.where(t < N_LAT_BLK, x_ref[...], ctx_ref[...])
    yp = jnp.where(t < N_LAT_BLK, ysl_ref[...], ysc_ref[...])
    for qt in range(S5_QT):
        for k in range(S5_CHUNK // S5_PAIRS):
            tiles = _block_transpose4([yp[qt * S5_PAIRS + pr][:, k * LANES:(k + 1) * LANES]
                                       for pr in range(S5_PAIRS)])
            for n, tile in enumerate(tiles):
                ys_scr[qt, pl.ds(k * S5_PAIRS + n, TN // S5_CHUNK, stride=S5_CHUNK), :] = tile
    ys = jnp.concatenate([ys_scr[qt] for qt in range(S5_QT)], axis=-1)
    gate = mod0_ref[pl.ds(_block_mod_row(t), 1), 2 * D:3 * D]
    tail = _tail0_stages(z, ys, rest_ref, gate, gluw_ref, glub_ref, sg_g_ref, sg_b_ref, sgw_ref,
                         sgbias_ref, wout_scr, lng_ref, lnb_ref)

    half_q = dq // 2
    p_q0 = project(0, half_q)
    next(tail)
    p_q1 = project(half_q, dq)
    next(tail)
    p_k = project(dq, dq + dkv)
    vt_ref[...] = lax.dot_general(wbf_scr[:, dq + dkv:dq + 2 * dkv], h, (((0,), (1,)), ((), ())),
                                  preferred_element_type=F32).astype(BF16)
    next(tail)
    next(tail)
    for c0, p in ((0, p_q0), (half_q, p_q1)):
        for c, r in roped_tiles(p, scale):
            q_ref[:, c0 + c * LANES:c0 + (c + 1) * LANES] = r
    for c, r in roped_tiles(p_k, 1.0):
        k_ref[:, c * LANES:(c + 1) * LANES] = r
    p_g = project(dq + 2 * dkv, ODD_IN)
    z1 = next(tail)
    z1_ref[...] = z1
    zprev_ref[...] = z1
    g_ref[...] = p_g.astype(BF16)


def _mid(x2, ctx2, mod0, mod1, ys_lat, ys_ctx, rest, glu_w, glu_b, sg_g, sg_b, sg_w, sg_bias,
         w_out, ln_g, ln_b, w_in1, row_tab, col_tab):
    dq = N_HEADS * HD
    dkv = N_KV * HD
    once = dict(pipeline_mode=pl.Buffered(1))
    row = lambda n: pl.BlockSpec((1, n), lambda i: (0, 0))
    lat = lambda i: (jnp.minimum(i, N_LAT_BLK - 1), 0)
    tail_blk = lambda i: (jnp.minimum(i, N_BLK - 1), 0)
    proj_blk = lambda i: (jnp.maximum(i - 1, 0), 0)
    return pl.pallas_call(
        _mid_kernel,
        grid=(N_BLK + 1,),
        in_specs=[
            pl.BlockSpec((TN, D), lat),
            pl.BlockSpec((R_CTX, D), lambda i: (0, 0), **once),
            pl.BlockSpec((8, 3 * D), lambda i: (0, 0)),
            pl.BlockSpec((8, 3 * D), lambda i: (0, 0)),
            pl.BlockSpec((S5_QT * S5_PAIRS, TN // S5_CHUNK, S5_CHUNK * PAIR_BLK),
                         lambda i: (0, jnp.minimum(i, N_LAT_BLK - 1), 0)),
            pl.BlockSpec((S5_QT * S5_PAIRS, R_CTX // S5_CHUNK, S5_CHUNK * PAIR_BLK),
                         lambda i: (0, 0, 0), **once),
            pl.BlockSpec((TN, EVEN_IN - S5_W), tail_blk),
            pl.BlockSpec((S5_W, S5_W), lambda i: (0, 0), **once),
            row(S5_W), row(SG_W), row(SG_W),
            pl.BlockSpec((SG_HEADS // 2, SG_CHUNK, 2 * SG_CHUNK), lambda i: (0, 0, 0)),
            pl.BlockSpec((SG_CHUNK, SG_W), lambda i: (0, 0)),
            pl.BlockSpec((S5_W + SG_W, D), lambda i: (0, 0), **once),
            row(D), row(D),
            pl.BlockSpec((D, ODD_IN), lambda i: (0, 0), **once),
            pl.BlockSpec((2, N_LAT // GRID_W, LANES), lambda i: (0, 0, 0)),
            pl.BlockSpec((2, GRID_W, LANES), lambda i: (0, 0, 0)),
        ],
        out_specs=[
            pl.BlockSpec((TN, D), tail_blk),
            pl.BlockSpec((TN, dq), proj_blk),
            pl.BlockSpec((TN, dkv), proj_blk),
            pl.BlockSpec((dkv, TN), lambda i: (0, jnp.maximum(i - 1, 0))),
            pl.BlockSpec((TN, dq), proj_blk),
        ],
        out_shape=[
            jax.ShapeDtypeStruct((R_ALL, D), F32),
            jax.ShapeDtypeStruct((R_ALL, dq), BF16),
            jax.ShapeDtypeStruct((R_ALL, dkv), BF16),
            jax.ShapeDtypeStruct((dkv, R_ALL), BF16),
            jax.ShapeDtypeStruct((R_ALL, dq), BF16),
        ],
        scratch_shapes=[pltpu.VMEM((TN, D), F32), pltpu.VMEM((D, ODD_IN), BF16),
                        pltpu.VMEM((S5_W + SG_W, D), BF16), pltpu.VMEM((S5_QT, TN, LANES), F32)],
        compiler_params=pltpu.CompilerParams(vmem_limit_bytes=VMEM_LIMIT),
        name="mid",
    )(x2, ctx2, mod0, mod1, ys_lat, ys_ctx, rest, glu_w, glu_b, sg_g, sg_b, sg_w, sg_bias, w_out,
      ln_g, ln_b, w_in1, row_tab, col_tab)


N_QBLK = N_LAT // ATT_BLK
GRP = N_HEADS // N_KV
ATT_SUB = 4
ATT_ROWS = ATT_SUB * ATT_BLK
N_QSTEP = N_QBLK // ATT_SUB
N_ATT_STEPS = B * N_QSTEP
assert ATT_ROWS == TN
assert WINDOW == ATT_BLK
ONES_ROWS = 16


def _attn_kernel(sink_ref, q_ref, kp_ref, kc_ref, kn_ref, kx_ref, vp_ref, vc_ref, vn_ref, vx_ref,
                 g_ref, z_ref, mod_ref, wout_ref, lng_ref, lnb_ref, out_ref, o_scr, wout_scr):
    step = pl.program_id(0)

    @pl.when(step == 0)
    def _():
        o_scr[...] = jnp.zeros((ATT_ROWS, N_HEADS * HD), BF16)
        wout_scr[...] = wout_ref[...].astype(BF16)

    t = jnp.maximum(step - 1, 0)
    gate = mod_ref[pl.ds(_block_mod_row(t), 1), 2 * D:3 * D]
    mix = (o_scr[...].astype(F32) * _silu(g_ref[...].astype(F32))).astype(BF16)
    y = jnp.dot(mix, wout_scr[...], preferred_element_type=F32) * gate
    out_ref[...] = _layer_norm(ALPHA * z_ref[...] + y, lng_ref[...], lnb_ref[...])

    i = jnp.minimum(step, N_ATT_STEPS - 1) % N_QSTEP
    half = ATT_BLK // 2
    n_win = 2 * ATT_BLK + half
    n_keys = n_win + N_CTX
    nq = GRP * half
    k_win = jnp.concatenate([kp_ref[...], kc_ref[...], kn_ref[...]], axis=0)
    vt_win = jnp.concatenate([vp_ref[...], vc_ref[...], vn_ref[...]], axis=1)
    k_ctx = kx_ref[...]
    vt_ctx = vx_ref[...]

    kpos = lax.broadcasted_iota(jnp.int32, (half, half), 0)
    qpos = lax.broadcasted_iota(jnp.int32, (half, half), 1)
    tile_q = lambda a: jnp.concatenate([a] * GRP, axis=1)
    neg = lambda ok: tile_q(jnp.where(ok, 0.0, NEG_INF))
    everyone = kpos >= 0
    qgrp = lax.broadcasted_iota(jnp.int32, (1, nq), 1) // half
    ones = jnp.ones((ONES_ROWS, n_keys), BF16)
    q = q_ref[...]

    halves = [(u, v) for u in range(ATT_SUB) for v in range(2)]
    units = [(n, h) for n in range(len(halves)) for h in range(N_KV)]
    k_all, vt_all, bias = [], [], []
    for u, v in halves:
        first = u * ATT_BLK + v * half
        k_all.append(jnp.concatenate([k_win[first:first + n_win], k_ctx], axis=0))
        vt_all.append(jnp.concatenate([vt_win[:, first:first + n_win], vt_ctx], axis=1))
        in_prev = [blk for blk in range(5) if first + blk * half < ATT_BLK]
        in_next = [blk for blk in range(5) if first + blk * half >= ATT_BLK + ATT_ROWS]
        blocks = {}
        for blk in sorted(set([0, 4] + in_prev + in_next)):
            ok = (kpos >= qpos) if blk == 0 else ((kpos <= qpos) if blk == 4 else everyone)
            if blk in in_prev:
                ok = ok & (i > 0)
            if blk in in_next:
                ok = ok & (i < N_QSTEP - 1)
            blocks[blk] = neg(ok)
        bias.append(blocks)
    scores = []
    for n, h in units:
        u, v = halves[n]
        r0 = u * ATT_BLK + v * half
        kh = k_all[n][:, h * HD:(h + 1) * HD]
        qh = jnp.concatenate(
            [q[r0:r0 + half, (h * GRP + g) * HD:(h * GRP + g + 1) * HD] for g in range(GRP)],
            axis=0)
        scores.append(lax.dot_general(kh, qh, (((1,), (1,)), ((), ())),
                                      preferred_element_type=F32))
    probs = []
    for (n, h), s in zip(units, scores):
        s = jnp.concatenate(
            [s[blk * half:(blk + 1) * half] + bias[n][blk] if blk in bias[n]
             else s[blk * half:(blk + 1) * half] for blk in range(5)] + [s[n_win:]], axis=0)
        sink = jnp.zeros((1, nq), F32)
        for g in range(GRP):
            sink = jnp.where(qgrp == g, sink_ref[h * GRP + g] * LOG2E, sink)
        m = jnp.maximum(jnp.max(s, axis=0, keepdims=True), sink)
        probs.append((jnp.exp2(s - m).astype(BF16), jnp.exp2(sink - m)))
    outs = [[None] * N_HEADS for _ in halves]
    for (n, h), (p, p_sink) in zip(units, probs):
        vt1 = jnp.concatenate([vt_all[n][h * HD:(h + 1) * HD], ones], axis=0)
        ov = jnp.dot(vt1, p, preferred_element_type=F32)
        o_t = ov[:HD] / (ov[HD:HD + 1] + p_sink)
        for g in range(GRP):
            outs[n][h * GRP + g] = o_t[:, g * half:(g + 1) * half]
    for u in range(ATT_SUB):
        o_t = jnp.concatenate(
            [jnp.concatenate([outs[2 * u][hh], outs[2 * u + 1][hh]], axis=1)
             for hh in range(N_HEADS)], axis=0)
        o_scr[u * ATT_BLK:(u + 1) * ATT_BLK, :] = jnp.transpose(o_t).astype(BF16)


def _attention_tail(sink, q, k, vt, g, z1, mod, w_out, ln_g, ln_b):
    dq = N_HEADS * HD
    dkv = N_KV * HD
    ctx_blk0 = R_LAT // N_CTX
    cur = lambda s: jnp.minimum(s, N_ATT_STEPS - 1)
    batch = lambda s: cur(s) // N_QSTEP
    qstep = lambda s: cur(s) % N_QSTEP
    prev_blk = lambda s: batch(s) * N_QBLK + jnp.maximum(ATT_SUB * qstep(s) - 1, 0)
    next_blk = lambda s: batch(s) * N_QBLK + jnp.minimum(ATT_SUB * (qstep(s) + 1), N_QBLK - 1)
    tail = lambda s: jnp.maximum(s - 1, 0)
    once = dict(pipeline_mode=pl.Buffered(1))
    row = lambda n: pl.BlockSpec((1, n), lambda s, sk: (0, 0))
    return pl.pallas_call(
        _attn_kernel,
        grid_spec=pltpu.PrefetchScalarGridSpec(
            num_scalar_prefetch=1,
            grid=(N_ATT_STEPS + 1,),
            in_specs=[
                pl.BlockSpec((ATT_ROWS, dq), lambda s, sk: (cur(s), 0)),
                pl.BlockSpec((ATT_BLK, dkv), lambda s, sk: (prev_blk(s), 0)),
                pl.BlockSpec((ATT_ROWS, dkv), lambda s, sk: (cur(s), 0)),
                pl.BlockSpec((ATT_BLK, dkv), lambda s, sk: (next_blk(s), 0)),
                pl.BlockSpec((N_CTX, dkv), lambda s, sk: (ctx_blk0 + batch(s), 0)),
                pl.BlockSpec((dkv, ATT_BLK), lambda s, sk: (0, prev_blk(s))),
                pl.BlockSpec((dkv, ATT_ROWS), lambda s, sk: (0, cur(s))),
                pl.BlockSpec((dkv, ATT_BLK), lambda s, sk: (0, next_blk(s))),
                pl.BlockSpec((dkv, N_CTX), lambda s, sk: (0, ctx_blk0 + batch(s))),
                pl.BlockSpec((ATT_ROWS, dq), lambda s, sk: (tail(s), 0)),
                pl.BlockSpec((ATT_ROWS, D), lambda s, sk: (tail(s), 0)),
                pl.BlockSpec((8, 3 * D), lambda s, sk: (0, 0)),
                pl.BlockSpec((dq, D), lambda s, sk: (0, 0), **once),
                row(D), row(D),
            ],
            out_specs=pl.BlockSpec((ATT_ROWS, D), lambda s, sk: (tail(s), 0)),
            scratch_shapes=[pltpu.VMEM((ATT_ROWS, dq), BF16), pltpu.VMEM((dq, D), BF16)],
        ),
        out_shape=jax.ShapeDtypeStruct((R_LAT, D), F32),
        compiler_params=pltpu.CompilerParams(vmem_limit_bytes=VMEM_LIMIT),
        name="attention",
    )(sink, q, k, k, k, k, vt, vt, vt, vt, g, z1, mod, w_out, ln_g, ln_b)


def kernel(x, c, ctx, c_ctx, mod_w, mod_b, ln_g, ln_b, e_w_in, e_w_out, s5_lam_re, s5_lam_im,
           s5_log_dt, s5_b_re, s5_b_im, s5_c_re, s5_c_im, s5_d, glu_w, glu_b, sg_ln_g, sg_ln_b,
           sg_w, sg_b, o_w_in, o_w_out, o_sink):
    x2 = x.reshape(R_LAT, D)
    ctx2 = ctx.reshape(R_CTX, D)
    mod = _modulation(c, c_ctx, mod_w, mod_b)

    vp, rest = _in0(x2, ctx2, mod[0], e_w_in[0])
    t_mat, e_mat, ft_mat, al_re, al_im = _s5_prep(
        s5_lam_re[0], s5_lam_im[0], s5_log_dt[0], s5_b_re[0], s5_b_im[0], s5_c_re[0], s5_c_im[0],
        s5_d[0])
    ys_lat, ys_ctx = _s5_out(vp, al_re, al_im, e_mat, t_mat, ft_mat)
    sg_bias = jnp.repeat(jnp.transpose(sg_b[0]), SG_W // SG_HEADS, axis=1)
    sg_w2 = jnp.transpose(sg_w[0].reshape(SG_HEADS // 2, 2, SG_CHUNK, SG_CHUNK), (0, 2, 1, 3))
    sg_w2 = sg_w2.reshape(SG_HEADS // 2, SG_CHUNK, 2 * SG_CHUNK).astype(BF16)
    row_tab, col_tab = _rope_tables()
    z1, q, k, vt, g = _mid(
        x2, ctx2, mod[0], mod[1], ys_lat, ys_ctx, rest, glu_w[0].astype(BF16),
        glu_b[0].reshape(1, S5_W), sg_ln_g[0].reshape(1, SG_W), sg_ln_b[0].reshape(1, SG_W),
        sg_w2, sg_bias, e_w_out[0], ln_g[0].reshape(1, D),
        ln_b[0].reshape(1, D), o_w_in[0], row_tab, col_tab)

    out = _attention_tail(o_sink[0], q, k, vt, g, z1, mod[1], o_w_out[0],
                          ln_g[1].reshape(1, D), ln_b[1].reshape(1, D))
    return out.reshape(B, N_LAT, D)
```

```python
import math

import jax
import jax.numpy as jnp
from jax import lax
from jax.experimental import pallas as pl
from jax.experimental.pallas import tpu as pltpu

F32 = jnp.float32
BF16 = jnp.bfloat16

D = 1024
B = 2
N_LAT = 8192
N_CTX = 256
DEPTH = 2
GRID_W = 64
S5_W = 512
S5_GC = 16
S5_G = 32
S5_P = 64
SG_W = 512
SG_HEADS = 8
SG_CHUNK = 128
N_HEADS = 16
N_KV = 4
HD = 64
WINDOW = 128
ATT_BLK = 128
ROPE_BASE = 10000.0
NEG_INF = -1e30
LN_EPS = 1e-5
ALPHA = (2 * DEPTH) ** 0.25
LOG2E = math.log2(math.e)
EVEN_IN = 2 * S5_W + 3 * SG_W
ODD_IN = 2 * N_HEADS * HD + 2 * N_KV * HD

LANES = 128
VMEM_LIMIT = 56 * 1024 * 1024


def _fusing_params(n_inputs, skip=()):
    return pltpu.CompilerParams(vmem_limit_bytes=VMEM_LIMIT,
                                allow_input_fusion=[n not in skip for n in range(n_inputs)])

R_LAT = B * N_LAT
R_CTX = B * N_CTX
R_ALL = R_LAT + R_CTX
TN = 512
N_LAT_BLK = R_LAT // TN
N_BLK = R_ALL // TN
LAT_BLK_PER_BATCH = N_LAT // TN
CTX_MOD_ROW = B

S5_CHUNK = 8
S5_QT = S5_W // LANES
S5_GPT = LANES // S5_GC
S5_ST = S5_GPT * S5_P
S5_PAIRS = S5_GPT // 2
PAIR_BLK = 2 * S5_GC
LAT_CHUNKS = N_LAT // S5_CHUNK
CTX_CHUNKS = N_CTX // S5_CHUNK
BATCH_CHUNKS = CTX_CHUNKS + LAT_CHUNKS
N_SEG = 16
SEG_CHUNKS = BATCH_CHUNKS // N_SEG
BWD_RESET_STEP = LAT_CHUNKS - (N_SEG - 1) * SEG_CHUNKS


def _sigmoid(x):
    return 1.0 / (1.0 + jnp.exp(-x))


def _silu(x):
    return x * _sigmoid(x)


def _gelu_tanh(x):
    return 0.5 * x * (1.0 + jnp.tanh(math.sqrt(2.0 / math.pi) * (x + 0.044715 * (x * x * x))))


def _layer_norm(x, g, b):
    mu = jnp.mean(x, axis=-1, keepdims=True)
    xc = x - mu
    var = jnp.mean(xc * xc, axis=-1, keepdims=True)
    return xc * lax.rsqrt(var + LN_EPS) * g + b


def _split_bf16(x):
    hi = x.astype(BF16)
    return hi, (x - hi.astype(F32)).astype(BF16)


def _dot_split(x, y_split, dims):
    x_hi, x_lo = _split_bf16(x)
    y_hi, y_lo = y_split
    dot = lambda a, b: lax.dot_general(a, b, dims, preferred_element_type=F32)
    return dot(x_hi, y_hi) + (dot(x_hi, y_lo) + dot(x_lo, y_hi))


def _block_mod_row(i):
    return jnp.where(i < N_LAT_BLK, i // LAT_BLK_PER_BATCH, CTX_MOD_ROW)


def _mod_kernel(cv_ref, w_ref, b_ref, o_ref):
    s = _silu(cv_ref[...])
    o_ref[0] = _dot_split(s, _split_bf16(w_ref[0]), (((1,), (0,)), ((), ()))) + b_ref[0]


def _modulation(c, c_ctx, mod_w, mod_b):
    cv = jnp.concatenate([c, c_ctx[None], jnp.zeros((8 - B - 1, D), F32)], axis=0)
    tn = 1024
    return pl.pallas_call(
        _mod_kernel,
        grid=(DEPTH, 3 * D // tn),
        in_specs=[
            pl.BlockSpec((8, D), lambda l, j: (0, 0)),
            pl.BlockSpec((1, D, tn), lambda l, j: (l, 0, j)),
            pl.BlockSpec((1, 1, tn), lambda l, j: (l, 0, j)),
        ],
        out_specs=pl.BlockSpec((1, 8, tn), lambda l, j: (l, 0, j)),
        out_shape=jax.ShapeDtypeStruct((DEPTH, 8, 3 * D), F32),
        compiler_params=_fusing_params(3),
        name="modulation",
    )(cv, mod_w, mod_b.reshape(DEPTH, 1, 3 * D))


def _block_transpose4(tiles):
    tiles = list(tiles)
    blk = lax.broadcasted_iota(jnp.int32, tiles[0].shape, 1) // PAIR_BLK
    for dist in (2, 1):
        keep = (blk & dist) == 0
        for i in range(len(tiles)):
            if i & dist:
                continue
            lo, hi = tiles[i], tiles[i + dist]
            tiles[i] = jnp.where(keep, lo, pltpu.roll(hi, dist * PAIR_BLK, axis=1))
            tiles[i + dist] = jnp.where(keep, pltpu.roll(lo, LANES - dist * PAIR_BLK, axis=1), hi)
    return tiles


def _in0_kernel(x_ref, ctx_ref, mod_ref, w_ref, vp_ref, rest_ref, xa_scr, wbf_scr):
    L = S5_CHUNK
    i = pl.program_id(0)

    @pl.when(i == 0)
    def _():
        wbf_scr[...] = w_ref[...].astype(BF16)

    z = jnp.where(i < N_LAT_BLK, x_ref[...], ctx_ref[...])
    m = mod_ref[pl.ds(_block_mod_row(i), 1), :]
    h = z * (1.0 + m[:, D:2 * D]) + m[:, :D]
    p = jnp.dot(h.astype(BF16), wbf_scr[...], preferred_element_type=F32)
    rest_ref[...] = p[:, S5_W:].astype(BF16)
    for q in range(S5_QT):
        xa_scr[q] = p[:, q * LANES:(q + 1) * LANES]
        tiles = [xa_scr[q, pl.ds(t, TN // L, stride=L), :] for t in range(L)]
        halves = [_block_transpose4(tiles[k * S5_PAIRS:(k + 1) * S5_PAIRS])
                  for k in range(L // S5_PAIRS)]
        for pr in range(S5_PAIRS):
            vp_ref[q * S5_PAIRS + pr] = jnp.concatenate(
                [half[pr] for half in halves], axis=-1).astype(BF16)


def _in0(x2, ctx2, mod, w_in):
    pair_w = S5_CHUNK * PAIR_BLK
    return pl.pallas_call(
        _in0_kernel,
        grid=(N_BLK,),
        in_specs=[
            pl.BlockSpec((TN, D), lambda i: (jnp.minimum(i, N_LAT_BLK - 1), 0)),
            pl.BlockSpec((R_CTX, D), lambda i: (0, 0)),
            pl.BlockSpec((8, 3 * D), lambda i: (0, 0)),
            pl.BlockSpec((D, EVEN_IN), lambda i: (0, 0)),
        ],
        out_specs=[
            pl.BlockSpec((S5_QT * S5_PAIRS, TN // S5_CHUNK, pair_w), lambda i: (0, i, 0)),
            pl.BlockSpec((TN, EVEN_IN - S5_W), lambda i: (i, 0)),
        ],
        out_shape=[
            jax.ShapeDtypeStruct((S5_QT * S5_PAIRS, R_ALL // S5_CHUNK, pair_w), BF16),
            jax.ShapeDtypeStruct((R_ALL, EVEN_IN - S5_W), BF16),
        ],
        scratch_shapes=[pltpu.VMEM((S5_QT, TN, LANES), F32), pltpu.VMEM((D, EVEN_IN), BF16)],
        compiler_params=_fusing_params(4, skip=(2,)),
        name="in0",
    )(x2, ctx2, mod, w_in)


def _s5_prep_kernel(rows_ref, bre_ref, bim_ref, cre_ref, cim_ref, dsk_ref, t_ref, e_ref, ft_ref,
                    alr_ref, ali_ref):
    L = S5_CHUNK
    tile = (LANES, S5_ST)
    same_group = (lax.broadcasted_iota(jnp.int32, tile, 0) // S5_GC
                  == lax.broadcasted_iota(jnp.int32, tile, 1) // S5_P)
    nt_dims = (((1,), (1,)), ((), ()))
    nn_dims = (((1,), (0,)), ((), ()))
    pick_ch = (lax.broadcasted_iota(jnp.int32, (LANES, S5_GC), 0) % S5_GC
               == lax.broadcasted_iota(jnp.int32, (LANES, S5_GC), 1)).astype(BF16)
    rep_st = (lax.broadcasted_iota(jnp.int32, (S5_P, S5_ST), 0)
              == lax.broadcasted_iota(jnp.int32, (S5_P, S5_ST), 1) % S5_P).astype(BF16)

    def spread_b(b):
        hi, lo = _split_bf16(b)
        dot = lambda v: lax.dot_general(pick_ch, v, nt_dims, preferred_element_type=F32)
        return jnp.where(same_group, dot(hi) + dot(lo), 0.0)

    def spread_c(c):
        hi, lo = _split_bf16(c)
        dot = lambda v: lax.dot_general(v, rep_st, nn_dims, preferred_element_type=F32)
        return jnp.where(same_group, dot(hi) + dot(lo), 0.0)

    pair_rows = lambda pr: slice(pr * PAIR_BLK, (pr + 1) * PAIR_BLK)
    pair_lanes = lambda pr: slice(pr * LANES, (pr + 1) * LANES)
    lag_strip = []
    for d in range(2):
        lam_re = rows_ref[d, 0]
        lam_im = rows_ref[d, 1]
        dt = jnp.exp(rows_ref[d, 2])
        pw = []
        for k in range(L + 1):
            mag = jnp.exp(lam_re * dt * float(k))
            ang = lam_im * dt * float(k)
            pw.append((mag * jnp.cos(ang), mag * jnp.sin(ang)))
        alr_ref[d] = pw[L][0]
        ali_ref[d] = pw[L][1]
        den = lam_re * lam_re + lam_im * lam_im
        nr = pw[1][0] - 1.0
        ni = pw[1][1]
        f_re = (nr * lam_re + ni * lam_im) / den
        f_im = (ni * lam_re - nr * lam_im) / den
        b_re = spread_b(bre_ref[d])
        b_im = spread_b(bim_ref[d])
        bb_re = f_re * b_re - f_im * b_im
        bb_im = f_re * b_im + f_im * b_re
        c_re = spread_c(cre_ref[d])
        c_im = spread_c(cim_ref[d])
        col_re = slice(2 * d * LANES, (2 * d + 1) * LANES)
        col_im = slice((2 * d + 1) * LANES, (2 * d + 2) * LANES)
        strips = []
        for pr in range(S5_PAIRS):
            blk = lambda a: a[pair_rows(pr), pair_lanes(pr)]
            p_re = [blk(jnp.broadcast_to(pw[k][0], tile)) for k in range(L + 1)]
            p_im = [blk(jnp.broadcast_to(pw[k][1], tile)) for k in range(L + 1)]
            bbr, bbi, cr, ci = blk(bb_re), blk(bb_im), blk(c_re), blk(c_im)
            for k in range(L):
                s = (L - 1 - k) if d == 0 else k
                rows = slice(s * PAIR_BLK, (s + 1) * PAIR_BLK)
                e_ref[0, pr, rows, col_re] = (p_re[k] * bbr - p_im[k] * bbi).astype(BF16)
                e_ref[0, pr, rows, col_im] = (p_re[k] * bbi + p_im[k] * bbr).astype(BF16)
            cw = [(cr * p_re[k] - ci * p_im[k], -(cr * p_im[k] + ci * p_re[k]))
                  for k in range(L + 1)]
            for t in range(L):
                k = (t + 1) if d == 0 else (L - t)
                rows = slice(t * PAIR_BLK, (t + 1) * PAIR_BLK)
                ft_ref[0, pr, rows, col_re] = cw[k][0].astype(BF16)
                ft_ref[0, pr, rows, col_im] = cw[k][1].astype(BF16)
            lags = range(L) if d == 0 else range(L - 1, -1, -1)
            c_stack = jnp.concatenate(
                [jnp.concatenate([cw[k][0], cw[k][1]], axis=-1) for k in lags], axis=0)
            strips.append(_dot_split(jnp.concatenate([bbr, bbi], axis=-1),
                                     _split_bf16(c_stack), nt_dims))
        lag_strip.append(strips)

    strip_w = 2 * L * PAIR_BLK
    mid = (L - 1) * PAIR_BLK
    widen = lambda a: jnp.concatenate([a, jnp.zeros((a.shape[0], strip_w - a.shape[1]), F32)], axis=-1)
    row = lax.broadcasted_iota(jnp.int32, (PAIR_BLK, strip_w), 0)
    lane = lax.broadcasted_iota(jnp.int32, (PAIR_BLK, strip_w), 1)
    d_row = widen(dsk_ref[0])
    for pr in range(S5_PAIRS):
        strip = (widen(lag_strip[1][pr])
                 + pltpu.roll(widen(lag_strip[0][pr]), mid, axis=1)
                 + jnp.where(lane == row + mid,
                             pltpu.roll(d_row, mid - pr * PAIR_BLK, axis=1), 0.0))
        for s in range(L):
            shift = (strip_w - (L - 1 - s) * PAIR_BLK) % strip_w
            blk = pltpu.roll(strip, shift, axis=1) if shift else strip
            t_ref[0, pr, s * PAIR_BLK:(s + 1) * PAIR_BLK, :] = blk[:, :L * PAIR_BLK].astype(BF16)


def _s5_prep(lam_re, lam_im, log_dt, b_re, b_im, c_re, c_im, d_skip):
    L = S5_CHUNK
    n_st = S5_G * S5_P
    pair_w = L * PAIR_BLK
    rows = jnp.stack([lam_re.reshape(2, 1, n_st), lam_im.reshape(2, 1, n_st),
                      jnp.repeat(log_dt, S5_P, axis=1).reshape(2, 1, n_st)], axis=1)
    b_spec = pl.BlockSpec((2, S5_ST, S5_GC), lambda q: (0, q, 0))
    c_spec = pl.BlockSpec((2, LANES, S5_P), lambda q: (0, q, 0))
    return pl.pallas_call(
        _s5_prep_kernel,
        grid=(S5_QT,),
        in_specs=[
            pl.BlockSpec((2, 3, 1, S5_ST), lambda q: (0, 0, 0, q)),
            b_spec, b_spec, c_spec, c_spec,
            pl.BlockSpec((1, 1, LANES), lambda q: (q, 0, 0)),
        ],
        out_specs=[
            pl.BlockSpec((1, S5_PAIRS, pair_w, pair_w), lambda q: (q, 0, 0, 0)),
            pl.BlockSpec((1, S5_PAIRS, pair_w, 4 * LANES), lambda q: (q, 0, 0, 0)),
            pl.BlockSpec((1, S5_PAIRS, pair_w, 4 * LANES), lambda q: (q, 0, 0, 0)),
            pl.BlockSpec((2, 1, S5_ST), lambda q: (q, 0, 0)),
            pl.BlockSpec((2, 1, S5_ST), lambda q: (q, 0, 0)),
        ],
        out_shape=[
            jax.ShapeDtypeStruct((S5_QT, S5_PAIRS, pair_w, pair_w), BF16),
            jax.ShapeDtypeStruct((S5_QT, S5_PAIRS, pair_w, 4 * LANES), BF16),
            jax.ShapeDtypeStruct((S5_QT, S5_PAIRS, pair_w, 4 * LANES), BF16),
            jax.ShapeDtypeStruct((2 * S5_QT, 1, S5_ST), F32),
            jax.ShapeDtypeStruct((2 * S5_QT, 1, S5_ST), F32),
        ],
        compiler_params=_fusing_params(6),
        name="s5_prep",
    )(rows, b_re.reshape(2, n_st, S5_GC), b_im.reshape(2, n_st, S5_GC),
      c_re.reshape(2, S5_W, S5_P), c_im.reshape(2, S5_W, S5_P), d_skip.reshape(S5_QT, 1, LANES))


N_SLAB = S5_ST // LANES


def _cmul(ar, ai, br, bi):
    return ar * br - ai * bi, ar * bi + ai * br


def _cpow(ar, ai, n):
    res = None
    while n:
        if n & 1:
            res = (ar, ai) if res is None else _cmul(res[0], res[1], ar, ai)
        n >>= 1
        if n:
            ar, ai = _cmul(ar, ai, ar, ai)
    return res


def _scan_boundary_states(loc_ref, out_ref, ar_ref, ai_ref, backward):
    a_row = [(ar_ref[0][:, k * LANES:(k + 1) * LANES], ai_ref[0][:, k * LANES:(k + 1) * LANES])
             for k in range(N_SLAB)]
    a_tile = [(jnp.broadcast_to(r, (N_SEG, LANES)), jnp.broadcast_to(i, (N_SEG, LANES)))
              for r, i in a_row]

    def sweep(lo, hi, carry, store):
        def step(i, carry):
            off = (SEG_CHUNKS - 1 - i) if backward else i
            rows = pl.ds(off, N_SEG, stride=SEG_CHUNKS)
            new = []
            for k, (cr, ci) in enumerate(carry):
                xr = loc_ref[k, rows, :]
                xi = loc_ref[N_SLAB + k, rows, :]
                if store:
                    out_ref[k, rows, :] = cr
                    out_ref[N_SLAB + k, rows, :] = ci
                nr, ni = _cmul(a_tile[k][0], a_tile[k][1], cr, ci)
                new.append((nr + xr, ni + xi))
            return tuple(new)
        return lax.fori_loop(lo, hi, step, carry, unroll=True)

    def full(carry, store):
        if not backward:
            return sweep(0, SEG_CHUNKS, carry, store)
        carry = sweep(0, BWD_RESET_STEP, carry, store)
        first = lax.broadcasted_iota(jnp.int32, (N_SEG, LANES), 0) == 0
        carry = tuple((jnp.where(first, 0.0, cr), jnp.where(first, 0.0, ci)) for cr, ci in carry)
        return sweep(BWD_RESET_STEP, SEG_CHUNKS, carry, store)

    zero_tile = jnp.zeros((N_SEG, LANES), F32)
    ends = full(tuple((zero_tile, zero_tile) for _ in range(N_SLAB)), False)

    starts = []
    for k, (er, ei) in enumerate(ends):
        pr, pi = _cpow(a_row[k][0], a_row[k][1], SEG_CHUNKS)
        if backward:
            order = range(N_SEG - 1, -1, -1)
            g = (er[0:1], ei[0:1])
        else:
            order = range(N_SEG)
            g = (jnp.zeros((1, LANES), F32), jnp.zeros((1, LANES), F32))
        rows_r = [None] * N_SEG
        rows_i = [None] * N_SEG
        for j in order:
            rows_r[j], rows_i[j] = g
            nr, ni = _cmul(pr, pi, g[0], g[1])
            g = (nr + er[j:j + 1], ni + ei[j:j + 1])
        starts.append((jnp.concatenate(rows_r, axis=0), jnp.concatenate(rows_i, axis=0)))
    full(tuple(starts), True)


def _s5_out_kernel(vl_ref, vc_ref, ar_ref, ai_ref, e_ref, t_ref, ft_ref, yl_ref, yc_ref, st_ref):
    n_dir = 2 * N_SLAB
    rows = [jnp.concatenate([vc_ref[pr], vl_ref[pr]], axis=0) for pr in range(S5_PAIRS)]
    for pr in range(S5_PAIRS):
        st = jnp.dot(rows[pr], e_ref[0, pr], preferred_element_type=F32)
        for comp in range(4):
            st_ref[comp * N_SLAB + pr] = st[:, comp * LANES:(comp + 1) * LANES]
    for d in range(2):
        st_d = st_ref.at[pl.ds(d * n_dir, n_dir)]
        _scan_boundary_states(st_d, st_d, ar_ref.at[pl.ds(d, 1)], ai_ref.at[pl.ds(d, 1)], d == 1)

    for pr in range(S5_PAIRS):
        st = jnp.concatenate([st_ref[comp * N_SLAB + pr] for comp in range(4)],
                             axis=-1).astype(BF16)
        y = (jnp.dot(rows[pr], t_ref[0, pr], preferred_element_type=F32)
             + lax.dot_general(st, ft_ref[0, pr], (((1,), (1,)), ((), ())),
                               preferred_element_type=F32))
        yc_ref[pr] = y[:CTX_CHUNKS]
        yl_ref[pr] = y[CTX_CHUNKS:]


def _s5_out(vp, al_re, al_im, e_mat, t_mat, ft_mat):
    pair_w = S5_CHUNK * PAIR_BLK
    ctx_blk0 = (R_LAT // S5_CHUNK) // CTX_CHUNKS
    return pl.pallas_call(
        _s5_out_kernel,
        grid=(S5_QT, B),
        in_specs=[
            pl.BlockSpec((S5_PAIRS, LAT_CHUNKS, pair_w), lambda q, b: (q, b, 0)),
            pl.BlockSpec((S5_PAIRS, CTX_CHUNKS, pair_w), lambda q, b: (q, ctx_blk0 + b, 0)),
            pl.BlockSpec((2, 1, S5_ST), lambda q, b: (q, 0, 0)),
            pl.BlockSpec((2, 1, S5_ST), lambda q, b: (q, 0, 0)),
            pl.BlockSpec((1, S5_PAIRS, pair_w, 4 * LANES), lambda q, b: (q, 0, 0, 0)),
            pl.BlockSpec((1, S5_PAIRS, pair_w, pair_w), lambda q, b: (q, 0, 0, 0)),
            pl.BlockSpec((1, S5_PAIRS, pair_w, 4 * LANES), lambda q, b: (q, 0, 0, 0)),
        ],
        out_specs=[
            pl.BlockSpec((S5_PAIRS, LAT_CHUNKS, pair_w), lambda q, b: (q, b, 0)),
            pl.BlockSpec((S5_PAIRS, CTX_CHUNKS, pair_w), lambda q, b: (q, b, 0)),
        ],
        out_shape=[
            jax.ShapeDtypeStruct((S5_QT * S5_PAIRS, B * LAT_CHUNKS, pair_w), F32),
            jax.ShapeDtypeStruct((S5_QT * S5_PAIRS, B * CTX_CHUNKS, pair_w), F32),
        ],
        scratch_shapes=[pltpu.VMEM((4 * N_SLAB, BATCH_CHUNKS, LANES), F32)],
        compiler_params=pltpu.CompilerParams(vmem_limit_bytes=VMEM_LIMIT),
        name="s5_out",
    )(vp, vp, al_re, al_im, e_mat, t_mat, ft_mat)


def _tail0_stages(z, ys, rest_ref, gate, gluw_ref, glub_ref, sg_g_ref, sg_b_ref, sgw_ref,
                  sgbias_ref, wout_ref, lng_ref, lnb_ref):
    ga = rest_ref[:, 0:S5_W].astype(F32)
    u = rest_ref[:, S5_W:2 * S5_W].astype(F32)
    v = rest_ref[:, 2 * S5_W:3 * S5_W].astype(F32)
    gb = rest_ref[:, 3 * S5_W:4 * S5_W].astype(F32)

    ya = _gelu_tanh(ys)
    glu = jnp.dot(ya.astype(BF16), gluw_ref[...], preferred_element_type=F32) + glub_ref[...]
    yield None
    ya = ya * _sigmoid(glu) * _silu(ga)
    yield None

    vn = _layer_norm(v, sg_g_ref[...], sg_b_ref[...]).astype(BF16)
    lane = lax.broadcasted_iota(jnp.int32, (SG_CHUNK, LANES), 1)
    first_head = lane < (SG_W // SG_HEADS)
    chunks = []
    for c in range(TN // SG_CHUNK):
        tiles = []
        for j in range(SG_W // LANES):
            vt = vn[c * SG_CHUNK:(c + 1) * SG_CHUNK, j * LANES:(j + 1) * LANES]
            zero = jnp.zeros_like(vt)
            v2 = jnp.concatenate([jnp.where(first_head, vt, zero), jnp.where(first_head, zero, vt)],
                                 axis=0)
            tiles.append(jnp.dot(sgw_ref[j], v2, preferred_element_type=F32))
        chunks.append(jnp.concatenate(tiles, axis=-1) + sgbias_ref[...])
    s = jnp.concatenate(chunks, axis=0)
    yb = u * s * _silu(gb)
    yield None

    mix = jnp.concatenate([ya, yb], axis=-1).astype(BF16)
    y = jnp.dot(mix, wout_ref[...], preferred_element_type=F32) * gate
    yield None
    yield _layer_norm(ALPHA * z + y, lng_ref[...], lnb_ref[...])


def _rope_tables():
    nf = HD // 4
    n_rows = N_LAT // GRID_W
    lane = jnp.arange(LANES)
    inv = ROPE_BASE ** (-(lane % nf).astype(F32) / nf)
    by_row = ((lane % HD) // (HD // 2) == 0)[None, :]
    sign = jnp.where((lane % (HD // 2)) < nf, -1.0, 1.0)[None, :]
    row_ang = jnp.arange(n_rows, dtype=F32)[:, None] * inv[None, :]
    col_ang = jnp.arange(GRID_W, dtype=F32)[:, None] * inv[None, :]
    zero = jnp.zeros((), F32)
    row_tab = jnp.stack([jnp.where(by_row, jnp.cos(row_ang), zero),
                         jnp.where(by_row, sign * jnp.sin(row_ang), zero)])
    col_tab = jnp.stack([jnp.where(by_row, zero, jnp.cos(col_ang)),
                         jnp.where(by_row, zero, sign * jnp.sin(col_ang))])
    return row_tab, col_tab


def _rope_tile(x, cos, sin, first_half):
    nf = HD // 4
    partner = jnp.where(first_half, pltpu.roll(x, LANES - nf, axis=1), pltpu.roll(x, nf, axis=1))
    return x * cos + partner * sin


def _mid_kernel(x_ref, ctx_ref, mod0_ref, mod1_ref, ysl_ref, ysc_ref, rest_ref, gluw_ref, glub_ref,
                sg_g_ref, sg_b_ref, sgw_ref, sgbias_ref, wout_ref, lng_ref, lnb_ref, w_ref,
                rtab_ref, ctab_ref, z1_ref, q_ref, k_ref, vt_ref, g_ref, zprev_ref, wbf_scr,
                wout_scr, ys_scr):
    i = pl.program_id(0)
    dq = N_HEADS * HD
    dkv = N_KV * HD

    @pl.when(i == 0)
    def _():
        zprev_ref[...] = jnp.zeros((TN, D), F32)
        wbf_scr[...] = w_ref[...].astype(BF16)
        wout_scr[...] = wout_ref[...].astype(BF16)

    j = jnp.maximum(i - 1, 0)
    m1 = mod1_ref[pl.ds(_block_mod_row(j), 1), :]
    h = (zprev_ref[...] * (1.0 + m1[:, D:2 * D]) + m1[:, :D]).astype(BF16)
    grid_row0 = (j % LAT_BLK_PER_BATCH) * (TN // GRID_W)
    table = lambda t: jnp.concatenate(
        [rtab_ref[t, pl.ds(grid_row0 + r, 1), :] + ctab_ref[t] for r in range(TN // GRID_W)],
        axis=0)
    cos = jnp.where(j < N_LAT_BLK, table(0), 1.0)
    sin = jnp.where(j < N_LAT_BLK, table(1), 0.0)
    lane = lax.broadcasted_iota(jnp.int32, (TN, LANES), 1)
    first_half = (lane % (HD // 2)) < (HD // 4)
    scale = HD ** -0.5 * LOG2E

    def project(c0, c1):
        return jnp.dot(h, wbf_scr[:, c0:c1], preferred_element_type=F32)

    def roped_tiles(p, mult):
        for c in range(p.shape[1] // LANES):
            r = _rope_tile(p[:, c * LANES:(c + 1) * LANES], cos, sin, first_half)
            yield c, (r * mult if mult != 1.0 else r).astype(BF16)

    t = jnp.minimum(i, N_BLK - 1)
    z = jnp.where(t < N_LAT_BLK, x_ref[...], ctx_ref[...])
    yp = jnp.where(t < N_LAT_BLK, ysl_ref[...], ysc_ref[...])
    for qt in range(S5_QT):
        for k in range(S5_CHUNK // S5_PAIRS):
            tiles = _block_transpose4([yp[qt * S5_PAIRS + pr][:, k * LANES:(k + 1) * LANES]
                                       for pr in range(S5_PAIRS)])
            for n, tile in enumerate(tiles):
                ys_scr[qt, pl.ds(k * S5_PAIRS + n, TN // S5_CHUNK, stride=S5_CHUNK), :] = tile
    ys = jnp.concatenate([ys_scr[qt] for qt in range(S5_QT)], axis=-1)
    gate = mod0_ref[pl.ds(_block_mod_row(t), 1), 2 * D:3 * D]
    tail = _tail0_stages(z, ys, rest_ref, gate, gluw_ref, glub_ref, sg_g_ref, sg_b_ref, sgw_ref,
                         sgbias_ref, wout_scr, lng_ref, lnb_ref)

    half_q = dq // 2
    p_q0 = project(0, half_q)
    next(tail)
    p_q1 = project(half_q, dq)
    next(tail)
    p_k = project(dq, dq + dkv)
    vt_ref[...] = lax.dot_general(wbf_scr[:, dq + dkv:dq + 2 * dkv], h, (((0,), (1,)), ((), ())),
                                  preferred_element_type=F32).astype(BF16)
    next(tail)
    next(tail)
    for c0, p in ((0, p_q0), (half_q, p_q1)):
        for c, r in roped_tiles(p, scale):
            q_ref[:, c0 + c * LANES:c0 + (c + 1) * LANES] = r
    for c, r in roped_tiles(p_k, 1.0):
        k_ref[:, c * LANES:(c + 1) * LANES] = r
    p_g = project(dq + 2 * dkv, ODD_IN)
    z1 = next(tail)
    z1_ref[...] = z1
    zprev_ref[...] = z1
    g_ref[...] = p_g.astype(BF16)


def _mid(x2, ctx2, mod0, mod1, ys_lat, ys_ctx, rest, glu_w, glu_b, sg_g, sg_b, sg_w, sg_bias,
         w_out, ln_g, ln_b, w_in1, row_tab, col_tab):
    dq = N_HEADS * HD
    dkv = N_KV * HD
    once = dict(pipeline_mode=pl.Buffered(1))
    row = lambda n: pl.BlockSpec((1, n), lambda i: (0, 0))
    lat = lambda i: (jnp.minimum(i, N_LAT_BLK - 1), 0)
    tail_blk = lambda i: (jnp.minimum(i, N_BLK - 1), 0)
    proj_blk = lambda i: (jnp.maximum(i - 1, 0), 0)
    return pl.pallas_call(
        _mid_kernel,
        grid=(N_BLK + 1,),
        in_specs=[
            pl.BlockSpec((TN, D), lat),
            pl.BlockSpec((R_CTX, D), lambda i: (0, 0), **once),
            pl.BlockSpec((8, 3 * D), lambda i: (0, 0)),
            pl.BlockSpec((8, 3 * D), lambda i: (0, 0)),
            pl.BlockSpec((S5_QT * S5_PAIRS, TN // S5_CHUNK, S5_CHUNK * PAIR_BLK),
                         lambda i: (0, jnp.minimum(i, N_LAT_BLK - 1), 0)),
            pl.BlockSpec((S5_QT * S5_PAIRS, R_CTX // S5_CHUNK, S5_CHUNK * PAIR_BLK),
                         lambda i: (0, 0, 0), **once),
            pl.BlockSpec((TN, EVEN_IN - S5_W), tail_blk),
            pl.BlockSpec((S5_W, S5_W), lambda i: (0, 0), **once),
            row(S5_W), row(SG_W), row(SG_W),
            pl.BlockSpec((SG_HEADS // 2, SG_CHUNK, 2 * SG_CHUNK), lambda i: (0, 0, 0)),
            pl.BlockSpec((SG_CHUNK, SG_W), lambda i: (0, 0)),
            pl.BlockSpec((S5_W + SG_W, D), lambda i: (0, 0), **once),
            row(D), row(D),
            pl.BlockSpec((D, ODD_IN), lambda i: (0, 0), **once),
            pl.BlockSpec((2, N_LAT // GRID_W, LANES), lambda i: (0, 0, 0)),
            pl.BlockSpec((2, GRID_W, LANES), lambda i: (0, 0, 0)),
        ],
        out_specs=[
            pl.BlockSpec((TN, D), tail_blk),
            pl.BlockSpec((TN, dq), proj_blk),
            pl.BlockSpec((TN, dkv), proj_blk),
            pl.BlockSpec((dkv, TN), lambda i: (0, jnp.maximum(i - 1, 0))),
            pl.BlockSpec((TN, dq), proj_blk),
        ],
        out_shape=[
            jax.ShapeDtypeStruct((R_ALL, D), F32),
            jax.ShapeDtypeStruct((R_ALL, dq), BF16),
            jax.ShapeDtypeStruct((R_ALL, dkv), BF16),
            jax.ShapeDtypeStruct((dkv, R_ALL), BF16),
            jax.ShapeDtypeStruct((R_ALL, dq), BF16),
        ],
        scratch_shapes=[pltpu.VMEM((TN, D), F32), pltpu.VMEM((D, ODD_IN), BF16),
                        pltpu.VMEM((S5_W + SG_W, D), BF16), pltpu.VMEM((S5_QT, TN, LANES), F32)],
        compiler_params=_fusing_params(19, skip=(2, 3, 17)),
        name="mid",
    )(x2, ctx2, mod0, mod1, ys_lat, ys_ctx, rest, glu_w, glu_b, sg_g, sg_b, sg_w, sg_bias, w_out,
      ln_g, ln_b, w_in1, row_tab, col_tab)


N_QBLK = N_LAT // ATT_BLK
GRP = N_HEADS // N_KV
ATT_SUB = 4
ATT_ROWS = ATT_SUB * ATT_BLK
N_QSTEP = N_QBLK // ATT_SUB
N_ATT_STEPS = B * N_QSTEP
assert ATT_ROWS == TN
assert WINDOW == ATT_BLK
ONES_ROWS = 16


def _attn_kernel(sink_ref, q_ref, kp_ref, kc_ref, kn_ref, kx_ref, vp_ref, vc_ref, vn_ref, vx_ref,
                 g_ref, z_ref, mod_ref, wout_ref, lng_ref, lnb_ref, out_ref, o_scr, wout_scr):
    step = pl.program_id(0)

    @pl.when(step == 0)
    def _():
        o_scr[...] = jnp.zeros((ATT_ROWS, N_HEADS * HD), BF16)
        wout_scr[...] = wout_ref[...].astype(BF16)

    t = jnp.maximum(step - 1, 0)
    gate = mod_ref[pl.ds(_block_mod_row(t), 1), 2 * D:3 * D]
    mix = (o_scr[...].astype(F32) * _silu(g_ref[...].astype(F32))).astype(BF16)
    y = jnp.dot(mix, wout_scr[...], preferred_element_type=F32) * gate
    out_ref[...] = _layer_norm(ALPHA * z_ref[...] + y, lng_ref[...], lnb_ref[...])

    i = jnp.minimum(step, N_ATT_STEPS - 1) % N_QSTEP
    half = ATT_BLK // 2
    n_win = 2 * ATT_BLK + half
    n_keys = n_win + N_CTX
    nq = GRP * half
    k_win = jnp.concatenate([kp_ref[...], kc_ref[...], kn_ref[...]], axis=0)
    vt_win = jnp.concatenate([vp_ref[...], vc_ref[...], vn_ref[...]], axis=1)
    k_ctx = kx_ref[...]
    vt_ctx = vx_ref[...]

    kpos = lax.broadcasted_iota(jnp.int32, (half, half), 0)
    qpos = lax.broadcasted_iota(jnp.int32, (half, half), 1)
    tile_q = lambda a: jnp.concatenate([a] * GRP, axis=1)
    neg = lambda ok: tile_q(jnp.where(ok, 0.0, NEG_INF))
    everyone = kpos >= 0
    qgrp = lax.broadcasted_iota(jnp.int32, (1, nq), 1) // half
    ones = jnp.ones((ONES_ROWS, n_keys), BF16)
    q = q_ref[...]

    halves = [(u, v) for u in range(ATT_SUB) for v in range(2)]
    units = [(n, h) for n in range(len(halves)) for h in range(N_KV)]
    k_all, vt_all, bias = [], [], []
    for u, v in halves:
        first = u * ATT_BLK + v * half
        k_all.append(jnp.concatenate([k_win[first:first + n_win], k_ctx], axis=0))
        vt_all.append(jnp.concatenate([vt_win[:, first:first + n_win], vt_ctx], axis=1))
        in_prev = [blk for blk in range(5) if first + blk * half < ATT_BLK]
        in_next = [blk for blk in range(5) if first + blk * half >= ATT_BLK + ATT_ROWS]
        blocks = {}
        for blk in sorted(set([0, 4] + in_prev + in_next)):
            ok = (kpos >= qpos) if blk == 0 else ((kpos <= qpos) if blk == 4 else everyone)
            if blk in in_prev:
                ok = ok & (i > 0)
            if blk in in_next:
                ok = ok & (i < N_QSTEP - 1)
            blocks[blk] = neg(ok)
        bias.append(blocks)
    scores = []
    for n, h in units:
        u, v = halves[n]
        r0 = u * ATT_BLK + v * half
        kh = k_all[n][:, h * HD:(h + 1) * HD]
        qh = jnp.concatenate(
            [q[r0:r0 + half, (h * GRP + g) * HD:(h * GRP + g + 1) * HD] for g in range(GRP)],
            axis=0)
        scores.append(lax.dot_general(kh, qh, (((1,), (1,)), ((), ())),
                                      preferred_element_type=F32))
    probs = []
    for (n, h), s in zip(units, scores):
        s = jnp.concatenate(
            [s[blk * half:(blk + 1) * half] + bias[n][blk] if blk in bias[n]
             else s[blk * half:(blk + 1) * half] for blk in range(5)] + [s[n_win:]], axis=0)
        sink = jnp.zeros((1, nq), F32)
        for g in range(GRP):
            sink = jnp.where(qgrp == g, sink_ref[h * GRP + g] * LOG2E, sink)
        m = jnp.maximum(jnp.max(s, axis=0, keepdims=True), sink)
        probs.append((jnp.exp2(s - m).astype(BF16), jnp.exp2(sink - m)))
    outs = [[None] * N_HEADS for _ in halves]
    for (n, h), (p, p_sink) in zip(units, probs):
        vt1 = jnp.concatenate([vt_all[n][h * HD:(h + 1) * HD], ones], axis=0)
        ov = jnp.dot(vt1, p, preferred_element_type=F32)
        o_t = ov[:HD] / (ov[HD:HD + 1] + p_sink)
        for g in range(GRP):
            outs[n][h * GRP + g] = o_t[:, g * half:(g + 1) * half]
    for u in range(ATT_SUB):
        o_t = jnp.concatenate(
            [jnp.concatenate([outs[2 * u][hh], outs[2 * u + 1][hh]], axis=1)
             for hh in range(N_HEADS)], axis=0)
        o_scr[u * ATT_BLK:(u + 1) * ATT_BLK, :] = jnp.transpose(o_t).astype(BF16)


def _attention_tail(sink, q, k, vt, g, z1, mod, w_out, ln_g, ln_b):
    dq = N_HEADS * HD
    dkv = N_KV * HD
    ctx_blk0 = R_LAT // N_CTX
    cur = lambda s: jnp.minimum(s, N_ATT_STEPS - 1)
    batch = lambda s: cur(s) // N_QSTEP
    qstep = lambda s: cur(s) % N_QSTEP
    prev_blk = lambda s: batch(s) * N_QBLK + jnp.maximum(ATT_SUB * qstep(s) - 1, 0)
    next_blk = lambda s: batch(s) * N_QBLK + jnp.minimum(ATT_SUB * (qstep(s) + 1), N_QBLK - 1)
    tail = lambda s: jnp.maximum(s - 1, 0)
    once = dict(pipeline_mode=pl.Buffered(1))
    row = lambda n: pl.BlockSpec((1, n), lambda s, sk: (0, 0))
    return pl.pallas_call(
        _attn_kernel,
        grid_spec=pltpu.PrefetchScalarGridSpec(
            num_scalar_prefetch=1,
            grid=(N_ATT_STEPS + 1,),
            in_specs=[
                pl.BlockSpec((ATT_ROWS, dq), lambda s, sk: (cur(s), 0)),
                pl.BlockSpec((ATT_BLK, dkv), lambda s, sk: (prev_blk(s), 0)),
                pl.BlockSpec((ATT_ROWS, dkv), lambda s, sk: (cur(s), 0)),
                pl.BlockSpec((ATT_BLK, dkv), lambda s, sk: (next_blk(s), 0)),
                pl.BlockSpec((N_CTX, dkv), lambda s, sk: (ctx_blk0 + batch(s), 0)),
                pl.BlockSpec((dkv, ATT_BLK), lambda s, sk: (0, prev_blk(s))),
                pl.BlockSpec((dkv, ATT_ROWS), lambda s, sk: (0, cur(s))),
                pl.BlockSpec((dkv, ATT_BLK), lambda s, sk: (0, next_blk(s))),
                pl.BlockSpec((dkv, N_CTX), lambda s, sk: (0, ctx_blk0 + batch(s))),
                pl.BlockSpec((ATT_ROWS, dq), lambda s, sk: (tail(s), 0)),
                pl.BlockSpec((ATT_ROWS, D), lambda s, sk: (tail(s), 0)),
                pl.BlockSpec((8, 3 * D), lambda s, sk: (0, 0)),
                pl.BlockSpec((dq, D), lambda s, sk: (0, 0), **once),
                row(D), row(D),
            ],
            out_specs=pl.BlockSpec((ATT_ROWS, D), lambda s, sk: (tail(s), 0)),
            scratch_shapes=[pltpu.VMEM((ATT_ROWS, dq), BF16), pltpu.VMEM((dq, D), BF16)],
        ),
        out_shape=jax.ShapeDtypeStruct((R_LAT, D), F32),
        compiler_params=_fusing_params(16, skip=(0, 12)),
        name="attention",
    )(sink, q, k, k, k, k, vt, vt, vt, vt, g, z1, mod, w_out, ln_g, ln_b)


def kernel(x, c, ctx, c_ctx, mod_w, mod_b, ln_g, ln_b, e_w_in, e_w_out, s5_lam_re, s5_lam_im,
           s5_log_dt, s5_b_re, s5_b_im, s5_c_re, s5_c_im, s5_d, glu_w, glu_b, sg_ln_g, sg_ln_b,
           sg_w, sg_b, o_w_in, o_w_out, o_sink):
    x2 = x.reshape(R_LAT, D)
    ctx2 = ctx.reshape(R_CTX, D)
    mod = _modulation(c, c_ctx, mod_w, mod_b)

    vp, rest = _in0(x2, ctx2, mod[0], e_w_in[0])
    t_mat, e_mat, ft_mat, al_re, al_im = _s5_prep(
        s5_lam_re[0], s5_lam_im[0], s5_log_dt[0], s5_b_re[0], s5_b_im[0], s5_c_re[0], s5_c_im[0],
        s5_d[0])
    ys_lat, ys_ctx = _s5_out(vp, al_re, al_im, e_mat, t_mat, ft_mat)
    sg_bias = jnp.repeat(jnp.transpose(sg_b[0]), SG_W // SG_HEADS, axis=1)
    sg_w2 = jnp.transpose(sg_w[0].reshape(SG_HEADS // 2, 2, SG_CHUNK, SG_CHUNK), (0, 2, 1, 3))
    sg_w2 = sg_w2.reshape(SG_HEADS // 2, SG_CHUNK, 2 * SG_CHUNK).astype(BF16)
    row_tab, col_tab = _rope_tables()
    z1, q, k, vt, g = _mid(
        x2, ctx2, mod[0], mod[1], ys_lat, ys_ctx, rest, glu_w[0].astype(BF16),
        glu_b[0].reshape(1, S5_W), sg_ln_g[0].reshape(1, SG_W), sg_ln_b[0].reshape(1, SG_W),
        sg_w2, sg_bias, e_w_out[0], ln_g[0].reshape(1, D),
        ln_b[0].reshape(1, D), o_w_in[0], row_tab, col_tab)

    out = _attention_tail(o_sink[0], q, k, vt, g, z1, mod[1], o_w_out[0],
                          ln_g[1].reshape(1, D), ln_b[1].reshape(1, D))
    return out.reshape(B, N_LAT, D)
```

```python
import math

import jax
import jax.numpy as jnp
from jax import lax
from jax.experimental import pallas as pl
from jax.experimental.pallas import tpu as pltpu

F32 = jnp.float32
BF16 = jnp.bfloat16

D = 1024
B = 2
N_LAT = 8192
N_CTX = 256
DEPTH = 2
GRID_W = 64
S5_W = 512
S5_GC = 16
S5_G = 32
S5_P = 64
SG_W = 512
SG_HEADS = 8
SG_CHUNK = 128
N_HEADS = 16
N_KV = 4
HD = 64
WINDOW = 128
ATT_BLK = 128
ROPE_BASE = 10000.0
NEG_INF = -1e30
LN_EPS = 1e-5
ALPHA = (2 * DEPTH) ** 0.25
LOG2E = math.log2(math.e)
EVEN_IN = 2 * S5_W + 3 * SG_W
ODD_IN = 2 * N_HEADS * HD + 2 * N_KV * HD

LANES = 128
VMEM_LIMIT = 56 * 1024 * 1024
SMALL_CALL_VMEM_LIMIT = 24 * 1024 * 1024


def _fusing_params(n_inputs, skip=(), vmem_limit=VMEM_LIMIT):
    return pltpu.CompilerParams(vmem_limit_bytes=vmem_limit,
                                allow_input_fusion=[n not in skip for n in range(n_inputs)])

R_LAT = B * N_LAT
R_CTX = B * N_CTX
R_ALL = R_LAT + R_CTX
TN = 512
N_LAT_BLK = R_LAT // TN
N_BLK = R_ALL // TN
LAT_BLK_PER_BATCH = N_LAT // TN
CTX_MOD_ROW = B

S5_CHUNK = 8
S5_QT = S5_W // LANES
S5_GPT = LANES // S5_GC
S5_ST = S5_GPT * S5_P
S5_PAIRS = S5_GPT // 2
PAIR_BLK = 2 * S5_GC
LAT_CHUNKS = N_LAT // S5_CHUNK
CTX_CHUNKS = N_CTX // S5_CHUNK
BATCH_CHUNKS = CTX_CHUNKS + LAT_CHUNKS
N_SEG = 16
SEG_CHUNKS = BATCH_CHUNKS // N_SEG
BWD_RESET_STEP = LAT_CHUNKS - (N_SEG - 1) * SEG_CHUNKS


def _sigmoid(x):
    return 1.0 / (1.0 + jnp.exp(-x))


def _silu(x):
    return x * _sigmoid(x)


def _gelu_tanh(x):
    return 0.5 * x * (1.0 + jnp.tanh(math.sqrt(2.0 / math.pi) * (x + 0.044715 * (x * x * x))))


def _layer_norm(x, g, b):
    mu = jnp.mean(x, axis=-1, keepdims=True)
    xc = x - mu
    var = jnp.mean(xc * xc, axis=-1, keepdims=True)
    return xc * lax.rsqrt(var + LN_EPS) * g + b


def _split_bf16(x):
    hi = x.astype(BF16)
    return hi, (x - hi.astype(F32)).astype(BF16)


def _dot_split(x, y_split, dims):
    x_hi, x_lo = _split_bf16(x)
    y_hi, y_lo = y_split
    dot = lambda a, b: lax.dot_general(a, b, dims, preferred_element_type=F32)
    return dot(x_hi, y_hi) + (dot(x_hi, y_lo) + dot(x_lo, y_hi))


def _block_mod_row(i):
    return jnp.where(i < N_LAT_BLK, i // LAT_BLK_PER_BATCH, CTX_MOD_ROW)


def _mod_kernel(cv_ref, w_ref, b_ref, o_ref):
    s = _silu(cv_ref[...])
    o_ref[0] = _dot_split(s, _split_bf16(w_ref[0]), (((1,), (0,)), ((), ()))) + b_ref[0]


def _modulation(c, c_ctx, mod_w, mod_b):
    cv = jnp.concatenate([c, c_ctx[None], jnp.zeros((8 - B - 1, D), F32)], axis=0)
    tn = 1024
    return pl.pallas_call(
        _mod_kernel,
        grid=(DEPTH, 3 * D // tn),
        in_specs=[
            pl.BlockSpec((8, D), lambda l, j: (0, 0)),
            pl.BlockSpec((1, D, tn), lambda l, j: (l, 0, j)),
            pl.BlockSpec((1, 1, tn), lambda l, j: (l, 0, j)),
        ],
        out_specs=pl.BlockSpec((1, 8, tn), lambda l, j: (l, 0, j)),
        out_shape=jax.ShapeDtypeStruct((DEPTH, 8, 3 * D), F32),
        compiler_params=_fusing_params(3, vmem_limit=SMALL_CALL_VMEM_LIMIT),
        name="modulation",
    )(cv, mod_w, mod_b.reshape(DEPTH, 1, 3 * D))


def _block_transpose4(tiles):
    tiles = list(tiles)
    blk = lax.broadcasted_iota(jnp.int32, tiles[0].shape, 1) // PAIR_BLK
    for dist in (2, 1):
        keep = (blk & dist) == 0
        for i in range(len(tiles)):
            if i & dist:
                continue
            lo, hi = tiles[i], tiles[i + dist]
            tiles[i] = jnp.where(keep, lo, pltpu.roll(hi, dist * PAIR_BLK, axis=1))
            tiles[i + dist] = jnp.where(keep, pltpu.roll(lo, LANES - dist * PAIR_BLK, axis=1), hi)
    return tiles


def _in0_kernel(x_ref, ctx_ref, mod_ref, w_ref, vp_ref, rest_ref, xa_scr, wbf_scr):
    L = S5_CHUNK
    i = pl.program_id(0)

    @pl.when(i == 0)
    def _():
        wbf_scr[...] = w_ref[...].astype(BF16)

    z = jnp.where(i < N_LAT_BLK, x_ref[...], ctx_ref[...])
    m = mod_ref[pl.ds(_block_mod_row(i), 1), :]
    h = z * (1.0 + m[:, D:2 * D]) + m[:, :D]
    p = jnp.dot(h.astype(BF16), wbf_scr[...], preferred_element_type=F32)
    rest_ref[...] = p[:, S5_W:].astype(BF16)
    for q in range(S5_QT):
        xa_scr[q] = p[:, q * LANES:(q + 1) * LANES]
        tiles = [xa_scr[q, pl.ds(t, TN // L, stride=L), :] for t in range(L)]
        halves = [_block_transpose4(tiles[k * S5_PAIRS:(k + 1) * S5_PAIRS])
                  for k in range(L // S5_PAIRS)]
        for pr in range(S5_PAIRS):
            vp_ref[q * S5_PAIRS + pr] = jnp.concatenate(
                [half[pr] for half in halves], axis=-1).astype(BF16)


def _in0(x2, ctx2, mod, w_in):
    pair_w = S5_CHUNK * PAIR_BLK
    return pl.pallas_call(
        _in0_kernel,
        grid=(N_BLK,),
        in_specs=[
            pl.BlockSpec((TN, D), lambda i: (jnp.minimum(i, N_LAT_BLK - 1), 0)),
            pl.BlockSpec((R_CTX, D), lambda i: (0, 0)),
            pl.BlockSpec((8, 3 * D), lambda i: (0, 0)),
            pl.BlockSpec((D, EVEN_IN), lambda i: (0, 0)),
        ],
        out_specs=[
            pl.BlockSpec((S5_QT * S5_PAIRS, TN // S5_CHUNK, pair_w), lambda i: (0, i, 0)),
            pl.BlockSpec((TN, EVEN_IN - S5_W), lambda i: (i, 0)),
        ],
        out_shape=[
            jax.ShapeDtypeStruct((S5_QT * S5_PAIRS, R_ALL // S5_CHUNK, pair_w), BF16),
            jax.ShapeDtypeStruct((R_ALL, EVEN_IN - S5_W), BF16),
        ],
        scratch_shapes=[pltpu.VMEM((S5_QT, TN, LANES), F32), pltpu.VMEM((D, EVEN_IN), BF16)],
        compiler_params=_fusing_params(4, skip=(2,)),
        name="in0",
    )(x2, ctx2, mod, w_in)


def _s5_prep_kernel(rows_ref, bre_ref, bim_ref, cre_ref, cim_ref, dsk_ref, t_ref, e_ref, ft_ref,
                    alr_ref, ali_ref):
    L = S5_CHUNK
    tile = (LANES, S5_ST)
    same_group = (lax.broadcasted_iota(jnp.int32, tile, 0) // S5_GC
                  == lax.broadcasted_iota(jnp.int32, tile, 1) // S5_P)
    nt_dims = (((1,), (1,)), ((), ()))
    nn_dims = (((1,), (0,)), ((), ()))
    pick_ch = (lax.broadcasted_iota(jnp.int32, (LANES, S5_GC), 0) % S5_GC
               == lax.broadcasted_iota(jnp.int32, (LANES, S5_GC), 1)).astype(BF16)
    rep_st = (lax.broadcasted_iota(jnp.int32, (S5_P, S5_ST), 0)
              == lax.broadcasted_iota(jnp.int32, (S5_P, S5_ST), 1) % S5_P).astype(BF16)

    def spread_b(b):
        hi, lo = _split_bf16(b)
        dot = lambda v: lax.dot_general(pick_ch, v, nt_dims, preferred_element_type=F32)
        return jnp.where(same_group, dot(hi) + dot(lo), 0.0)

    def spread_c(c):
        hi, lo = _split_bf16(c)
        dot = lambda v: lax.dot_general(v, rep_st, nn_dims, preferred_element_type=F32)
        return jnp.where(same_group, dot(hi) + dot(lo), 0.0)

    pair_rows = lambda pr: slice(pr * PAIR_BLK, (pr + 1) * PAIR_BLK)
    pair_lanes = lambda pr: slice(pr * LANES, (pr + 1) * LANES)
    lag_strip = []
    for d in range(2):
        lam_re = rows_ref[d, 0]
        lam_im = rows_ref[d, 1]
        dt = jnp.exp(rows_ref[d, 2])
        pw = []
        for k in range(L + 1):
            mag = jnp.exp(lam_re * dt * float(k))
            ang = lam_im * dt * float(k)
            pw.append((mag * jnp.cos(ang), mag * jnp.sin(ang)))
        alr_ref[d] = pw[L][0]
        ali_ref[d] = pw[L][1]
        den = lam_re * lam_re + lam_im * lam_im
        nr = pw[1][0] - 1.0
        ni = pw[1][1]
        f_re = (nr * lam_re + ni * lam_im) / den
        f_im = (ni * lam_re - nr * lam_im) / den
        b_re = spread_b(bre_ref[d])
        b_im = spread_b(bim_ref[d])
        bb_re = f_re * b_re - f_im * b_im
        bb_im = f_re * b_im + f_im * b_re
        c_re = spread_c(cre_ref[d])
        c_im = spread_c(cim_ref[d])
        col_re = slice(2 * d * LANES, (2 * d + 1) * LANES)
        col_im = slice((2 * d + 1) * LANES, (2 * d + 2) * LANES)
        strips = []
        for pr in range(S5_PAIRS):
            blk = lambda a: a[pair_rows(pr), pair_lanes(pr)]
            p_re = [blk(jnp.broadcast_to(pw[k][0], tile)) for k in range(L + 1)]
            p_im = [blk(jnp.broadcast_to(pw[k][1], tile)) for k in range(L + 1)]
            bbr, bbi, cr, ci = blk(bb_re), blk(bb_im), blk(c_re), blk(c_im)
            for k in range(L):
                s = (L - 1 - k) if d == 0 else k
                rows = slice(s * PAIR_BLK, (s + 1) * PAIR_BLK)
                e_ref[0, pr, rows, col_re] = (p_re[k] * bbr - p_im[k] * bbi).astype(BF16)
                e_ref[0, pr, rows, col_im] = (p_re[k] * bbi + p_im[k] * bbr).astype(BF16)
            cw = [(cr * p_re[k] - ci * p_im[k], -(cr * p_im[k] + ci * p_re[k]))
                  for k in range(L + 1)]
            for t in range(L):
                k = (t + 1) if d == 0 else (L - t)
                rows = slice(t * PAIR_BLK, (t + 1) * PAIR_BLK)
                ft_ref[0, pr, rows, col_re] = cw[k][0].astype(BF16)
                ft_ref[0, pr, rows, col_im] = cw[k][1].astype(BF16)
            lags = range(L) if d == 0 else range(L - 1, -1, -1)
            c_stack = jnp.concatenate(
                [jnp.concatenate([cw[k][0], cw[k][1]], axis=-1) for k in lags], axis=0)
            strips.append(_dot_split(jnp.concatenate([bbr, bbi], axis=-1),
                                     _split_bf16(c_stack), nt_dims))
        lag_strip.append(strips)

    strip_w = 2 * L * PAIR_BLK
    mid = (L - 1) * PAIR_BLK
    widen = lambda a: jnp.concatenate([a, jnp.zeros((a.shape[0], strip_w - a.shape[1]), F32)], axis=-1)
    row = lax.broadcasted_iota(jnp.int32, (PAIR_BLK, strip_w), 0)
    lane = lax.broadcasted_iota(jnp.int32, (PAIR_BLK, strip_w), 1)
    d_row = widen(dsk_ref[0])
    for pr in range(S5_PAIRS):
        strip = (widen(lag_strip[1][pr])
                 + pltpu.roll(widen(lag_strip[0][pr]), mid, axis=1)
                 + jnp.where(lane == row + mid,
                             pltpu.roll(d_row, mid - pr * PAIR_BLK, axis=1), 0.0))
        for s in range(L):
            shift = (strip_w - (L - 1 - s) * PAIR_BLK) % strip_w
            blk = pltpu.roll(strip, shift, axis=1) if shift else strip
            t_ref[0, pr, s * PAIR_BLK:(s + 1) * PAIR_BLK, :] = blk[:, :L * PAIR_BLK].astype(BF16)


def _s5_prep(lam_re, lam_im, log_dt, b_re, b_im, c_re, c_im, d_skip):
    L = S5_CHUNK
    n_st = S5_G * S5_P
    pair_w = L * PAIR_BLK
    rows = jnp.stack([lam_re.reshape(2, 1, n_st), lam_im.reshape(2, 1, n_st),
                      jnp.repeat(log_dt, S5_P, axis=1).reshape(2, 1, n_st)], axis=1)
    b_spec = pl.BlockSpec((2, S5_ST, S5_GC), lambda q: (0, q, 0))
    c_spec = pl.BlockSpec((2, LANES, S5_P), lambda q: (0, q, 0))
    return pl.pallas_call(
        _s5_prep_kernel,
        grid=(S5_QT,),
        in_specs=[
            pl.BlockSpec((2, 3, 1, S5_ST), lambda q: (0, 0, 0, q)),
            b_spec, b_spec, c_spec, c_spec,
            pl.BlockSpec((1, 1, LANES), lambda q: (q, 0, 0)),
        ],
        out_specs=[
            pl.BlockSpec((1, S5_PAIRS, pair_w, pair_w), lambda q: (q, 0, 0, 0)),
            pl.BlockSpec((1, S5_PAIRS, pair_w, 4 * LANES), lambda q: (q, 0, 0, 0)),
            pl.BlockSpec((1, S5_PAIRS, pair_w, 4 * LANES), lambda q: (q, 0, 0, 0)),
            pl.BlockSpec((2, 1, S5_ST), lambda q: (q, 0, 0)),
            pl.BlockSpec((2, 1, S5_ST), lambda q: (q, 0, 0)),
        ],
        out_shape=[
            jax.ShapeDtypeStruct((S5_QT, S5_PAIRS, pair_w, pair_w), BF16),
            jax.ShapeDtypeStruct((S5_QT, S5_PAIRS, pair_w, 4 * LANES), BF16),
            jax.ShapeDtypeStruct((S5_QT, S5_PAIRS, pair_w, 4 * LANES), BF16),
            jax.ShapeDtypeStruct((2 * S5_QT, 1, S5_ST), F32),
            jax.ShapeDtypeStruct((2 * S5_QT, 1, S5_ST), F32),
        ],
        compiler_params=_fusing_params(6, vmem_limit=SMALL_CALL_VMEM_LIMIT),
        name="s5_prep",
    )(rows, b_re.reshape(2, n_st, S5_GC), b_im.reshape(2, n_st, S5_GC),
      c_re.reshape(2, S5_W, S5_P), c_im.reshape(2, S5_W, S5_P), d_skip.reshape(S5_QT, 1, LANES))


N_SLAB = S5_ST // LANES


def _cmul(ar, ai, br, bi):
    return ar * br - ai * bi, ar * bi + ai * br


def _cpow(ar, ai, n):
    res = None
    while n:
        if n & 1:
            res = (ar, ai) if res is None else _cmul(res[0], res[1], ar, ai)
        n >>= 1
        if n:
            ar, ai = _cmul(ar, ai, ar, ai)
    return res


def _scan_boundary_states(loc_ref, out_ref, ar_ref, ai_ref, backward):
    a_row = [(ar_ref[0][:, k * LANES:(k + 1) * LANES], ai_ref[0][:, k * LANES:(k + 1) * LANES])
             for k in range(N_SLAB)]
    a_tile = [(jnp.broadcast_to(r, (N_SEG, LANES)), jnp.broadcast_to(i, (N_SEG, LANES)))
              for r, i in a_row]

    def sweep(lo, hi, carry, store):
        def step(i, carry):
            off = (SEG_CHUNKS - 1 - i) if backward else i
            rows = pl.ds(off, N_SEG, stride=SEG_CHUNKS)
            new = []
            for k, (cr, ci) in enumerate(carry):
                xr = loc_ref[k, rows, :]
                xi = loc_ref[N_SLAB + k, rows, :]
                if store:
                    out_ref[k, rows, :] = cr
                    out_ref[N_SLAB + k, rows, :] = ci
                nr, ni = _cmul(a_tile[k][0], a_tile[k][1], cr, ci)
                new.append((nr + xr, ni + xi))
            return tuple(new)
        return lax.fori_loop(lo, hi, step, carry, unroll=True)

    def full(carry, store):
        if not backward:
            return sweep(0, SEG_CHUNKS, carry, store)
        carry = sweep(0, BWD_RESET_STEP, carry, store)
        first = lax.broadcasted_iota(jnp.int32, (N_SEG, LANES), 0) == 0
        carry = tuple((jnp.where(first, 0.0, cr), jnp.where(first, 0.0, ci)) for cr, ci in carry)
        return sweep(BWD_RESET_STEP, SEG_CHUNKS, carry, store)

    zero_tile = jnp.zeros((N_SEG, LANES), F32)
    ends = full(tuple((zero_tile, zero_tile) for _ in range(N_SLAB)), False)

    starts = []
    for k, (er, ei) in enumerate(ends):
        pr, pi = _cpow(a_row[k][0], a_row[k][1], SEG_CHUNKS)
        if backward:
            order = range(N_SEG - 1, -1, -1)
            g = (er[0:1], ei[0:1])
        else:
            order = range(N_SEG)
            g = (jnp.zeros((1, LANES), F32), jnp.zeros((1, LANES), F32))
        rows_r = [None] * N_SEG
        rows_i = [None] * N_SEG
        for j in order:
            rows_r[j], rows_i[j] = g
            nr, ni = _cmul(pr, pi, g[0], g[1])
            g = (nr + er[j:j + 1], ni + ei[j:j + 1])
        starts.append((jnp.concatenate(rows_r, axis=0), jnp.concatenate(rows_i, axis=0)))
    full(tuple(starts), True)


def _s5_out_kernel(vl_ref, vc_ref, ar_ref, ai_ref, e_ref, t_ref, ft_ref, yl_ref, yc_ref, st_ref):
    n_dir = 2 * N_SLAB
    rows = [jnp.concatenate([vc_ref[pr], vl_ref[pr]], axis=0) for pr in range(S5_PAIRS)]
    for pr in range(S5_PAIRS):
        st = jnp.dot(rows[pr], e_ref[0, pr], preferred_element_type=F32)
        for comp in range(4):
            st_ref[comp * N_SLAB + pr] = st[:, comp * LANES:(comp + 1) * LANES]
    for d in range(2):
        st_d = st_ref.at[pl.ds(d * n_dir, n_dir)]
        _scan_boundary_states(st_d, st_d, ar_ref.at[pl.ds(d, 1)], ai_ref.at[pl.ds(d, 1)], d == 1)

    for pr in range(S5_PAIRS):
        st = jnp.concatenate([st_ref[comp * N_SLAB + pr] for comp in range(4)],
                             axis=-1).astype(BF16)
        y = (jnp.dot(rows[pr], t_ref[0, pr], preferred_element_type=F32)
             + lax.dot_general(st, ft_ref[0, pr], (((1,), (1,)), ((), ())),
                               preferred_element_type=F32))
        yc_ref[pr] = y[:CTX_CHUNKS]
        yl_ref[pr] = y[CTX_CHUNKS:]


def _s5_out(vp, al_re, al_im, e_mat, t_mat, ft_mat):
    pair_w = S5_CHUNK * PAIR_BLK
    ctx_blk0 = (R_LAT // S5_CHUNK) // CTX_CHUNKS
    return pl.pallas_call(
        _s5_out_kernel,
        grid=(S5_QT, B),
        in_specs=[
            pl.BlockSpec((S5_PAIRS, LAT_CHUNKS, pair_w), lambda q, b: (q, b, 0)),
            pl.BlockSpec((S5_PAIRS, CTX_CHUNKS, pair_w), lambda q, b: (q, ctx_blk0 + b, 0)),
            pl.BlockSpec((2, 1, S5_ST), lambda q, b: (q, 0, 0)),
            pl.BlockSpec((2, 1, S5_ST), lambda q, b: (q, 0, 0)),
            pl.BlockSpec((1, S5_PAIRS, pair_w, 4 * LANES), lambda q, b: (q, 0, 0, 0)),
            pl.BlockSpec((1, S5_PAIRS, pair_w, pair_w), lambda q, b: (q, 0, 0, 0)),
            pl.BlockSpec((1, S5_PAIRS, pair_w, 4 * LANES), lambda q, b: (q, 0, 0, 0)),
        ],
        out_specs=[
            pl.BlockSpec((S5_PAIRS, LAT_CHUNKS, pair_w), lambda q, b: (q, b, 0)),
            pl.BlockSpec((S5_PAIRS, CTX_CHUNKS, pair_w), lambda q, b: (q, b, 0)),
        ],
        out_shape=[
            jax.ShapeDtypeStruct((S5_QT * S5_PAIRS, B * LAT_CHUNKS, pair_w), F32),
            jax.ShapeDtypeStruct((S5_QT * S5_PAIRS, B * CTX_CHUNKS, pair_w), F32),
        ],
        scratch_shapes=[pltpu.VMEM((4 * N_SLAB, BATCH_CHUNKS, LANES), F32)],
        compiler_params=pltpu.CompilerParams(vmem_limit_bytes=VMEM_LIMIT),
        name="s5_out",
    )(vp, vp, al_re, al_im, e_mat, t_mat, ft_mat)


def _tail0_stages(z, ys, rest_ref, gate, gluw_ref, glub_ref, sg_g_ref, sg_b_ref, sgw_ref,
                  sgbias_ref, wout_ref, lng_ref, lnb_ref):
    ga = rest_ref[:, 0:S5_W].astype(F32)
    u = rest_ref[:, S5_W:2 * S5_W].astype(F32)
    v = rest_ref[:, 2 * S5_W:3 * S5_W].astype(F32)
    gb = rest_ref[:, 3 * S5_W:4 * S5_W].astype(F32)

    ya = _gelu_tanh(ys)
    glu = jnp.dot(ya.astype(BF16), gluw_ref[...], preferred_element_type=F32) + glub_ref[...]
    yield None
    ya = ya * _sigmoid(glu) * _silu(ga)
    yield None

    vn = _layer_norm(v, sg_g_ref[...], sg_b_ref[...]).astype(BF16)
    lane = lax.broadcasted_iota(jnp.int32, (SG_CHUNK, LANES), 1)
    first_head = lane < (SG_W // SG_HEADS)
    chunks = []
    for c in range(TN // SG_CHUNK):
        tiles = []
        for j in range(SG_W // LANES):
            vt = vn[c * SG_CHUNK:(c + 1) * SG_CHUNK, j * LANES:(j + 1) * LANES]
            zero = jnp.zeros_like(vt)
            v2 = jnp.concatenate([jnp.where(first_head, vt, zero), jnp.where(first_head, zero, vt)],
                                 axis=0)
            tiles.append(jnp.dot(sgw_ref[j], v2, preferred_element_type=F32))
        chunks.append(jnp.concatenate(tiles, axis=-1) + sgbias_ref[...])
    s = jnp.concatenate(chunks, axis=0)
    yb = u * s * _silu(gb)
    yield None

    mix = jnp.concatenate([ya, yb], axis=-1).astype(BF16)
    y = jnp.dot(mix, wout_ref[...], preferred_element_type=F32) * gate
    yield None
    yield _layer_norm(ALPHA * z + y, lng_ref[...], lnb_ref[...])


def _rope_tables():
    nf = HD // 4
    n_rows = N_LAT // GRID_W
    lane = jnp.arange(LANES)
    inv = ROPE_BASE ** (-(lane % nf).astype(F32) / nf)
    by_row = ((lane % HD) // (HD // 2) == 0)[None, :]
    sign = jnp.where((lane % (HD // 2)) < nf, -1.0, 1.0)[None, :]
    row_ang = jnp.arange(n_rows, dtype=F32)[:, None] * inv[None, :]
    col_ang = jnp.arange(GRID_W, dtype=F32)[:, None] * inv[None, :]
    zero = jnp.zeros((), F32)
    row_tab = jnp.stack([jnp.where(by_row, jnp.cos(row_ang), zero),
                         jnp.where(by_row, sign * jnp.sin(row_ang), zero)])
    col_tab = jnp.stack([jnp.where(by_row, zero, jnp.cos(col_ang)),
                         jnp.where(by_row, zero, sign * jnp.sin(col_ang))])
    return row_tab, col_tab


def _rope_tile(x, cos, sin, first_half):
    nf = HD // 4
    partner = jnp.where(first_half, pltpu.roll(x, LANES - nf, axis=1), pltpu.roll(x, nf, axis=1))
    return x * cos + partner * sin


def _mid_kernel(x_ref, ctx_ref, mod0_ref, mod1_ref, ysl_ref, ysc_ref, rest_ref, gluw_ref, glub_ref,
                sg_g_ref, sg_b_ref, sgw_ref, sgbias_ref, wout_ref, lng_ref, lnb_ref, w_ref,
                rtab_ref, ctab_ref, z1_ref, q_ref, k_ref, vt_ref, g_ref, zprev_ref, wbf_scr,
                wout_scr, ys_scr):
    i = pl.program_id(0)
    dq = N_HEADS * HD
    dkv = N_KV * HD

    @pl.when(i == 0)
    def _():
        zprev_ref[...] = jnp.zeros((TN, D), F32)
        wbf_scr[...] = w_ref[...].astype(BF16)
        wout_scr[...] = wout_ref[...].astype(BF16)

    j = jnp.maximum(i - 1, 0)
    m1 = mod1_ref[pl.ds(_block_mod_row(j), 1), :]
    h = (zprev_ref[...] * (1.0 + m1[:, D:2 * D]) + m1[:, :D]).astype(BF16)
    grid_row0 = (j % LAT_BLK_PER_BATCH) * (TN // GRID_W)
    table = lambda t: jnp.concatenate(
        [rtab_ref[t, pl.ds(grid_row0 + r, 1), :] + ctab_ref[t] for r in range(TN // GRID_W)],
        axis=0)
    cos = jnp.where(j < N_LAT_BLK, table(0), 1.0)
    sin = jnp.where(j < N_LAT_BLK, table(1), 0.0)
    lane = lax.broadcasted_iota(jnp.int32, (TN, LANES), 1)
    first_half = (lane % (HD // 2)) < (HD // 4)
    scale = HD ** -0.5 * LOG2E

    def project(c0, c1):
        return jnp.dot(h, wbf_scr[:, c0:c1], preferred_element_type=F32)

    def roped_tiles(p, mult):
        for c in range(p.shape[1] // LANES):
            r = _rope_tile(p[:, c * LANES:(c + 1) * LANES], cos, sin, first_half)
            yield c, (r * mult if mult != 1.0 else r).astype(BF16)

    t = jnp.minimum(i, N_BLK - 1)
    z = jnp.where(t < N_LAT_BLK, x_ref[...], ctx_ref[...])
    yp = jnp.where(t < N_LAT_BLK, ysl_ref[...], ysc_ref[...])
    for qt in range(S5_QT):
        for k in range(S5_CHUNK // S5_PAIRS):
            tiles = _block_transpose4([yp[qt * S5_PAIRS + pr][:, k * LANES:(k + 1) * LANES]
                                       for pr in range(S5_PAIRS)])
            for n, tile in enumerate(tiles):
                ys_scr[qt, pl.ds(k * S5_PAIRS + n, TN // S5_CHUNK, stride=S5_CHUNK), :] = tile
    ys = jnp.concatenate([ys_scr[qt] for qt in range(S5_QT)], axis=-1)
    gate = mod0_ref[pl.ds(_block_mod_row(t), 1), 2 * D:3 * D]
    tail = _tail0_stages(z, ys, rest_ref, gate, gluw_ref, glub_ref, sg_g_ref, sg_b_ref, sgw_ref,
                         sgbias_ref, wout_scr, lng_ref, lnb_ref)

    half_q = dq // 2
    p_q0 = project(0, half_q)
    next(tail)
    p_q1 = project(half_q, dq)
    next(tail)
    p_k = project(dq, dq + dkv)
    vt_ref[...] = lax.dot_general(wbf_scr[:, dq + dkv:dq + 2 * dkv], h, (((0,), (1,)), ((), ())),
                                  preferred_element_type=F32).astype(BF16)
    next(tail)
    next(tail)
    for c0, p in ((0, p_q0), (half_q, p_q1)):
        for c, r in roped_tiles(p, scale):
            q_ref[:, c0 + c * LANES:c0 + (c + 1) * LANES] = r
    for c, r in roped_tiles(p_k, 1.0):
        k_ref[:, c * LANES:(c + 1) * LANES] = r
    p_g = project(dq + 2 * dkv, ODD_IN)
    z1 = next(tail)
    z1_ref[...] = z1
    zprev_ref[...] = z1
    g_ref[...] = p_g.astype(BF16)


def _mid(x2, ctx2, mod0, mod1, ys_lat, ys_ctx, rest, glu_w, glu_b, sg_g, sg_b, sg_w, sg_bias,
         w_out, ln_g, ln_b, w_in1, row_tab, col_tab):
    dq = N_HEADS * HD
    dkv = N_KV * HD
    once = dict(pipeline_mode=pl.Buffered(1))
    row = lambda n: pl.BlockSpec((1, n), lambda i: (0, 0))
    lat = lambda i: (jnp.minimum(i, N_LAT_BLK - 1), 0)
    tail_blk = lambda i: (jnp.minimum(i, N_BLK - 1), 0)
    proj_blk = lambda i: (jnp.maximum(i - 1, 0), 0)
    return pl.pallas_call(
        _mid_kernel,
        grid=(N_BLK + 1,),
        in_specs=[
            pl.BlockSpec((TN, D), lat),
            pl.BlockSpec((R_CTX, D), lambda i: (0, 0), **once),
            pl.BlockSpec((8, 3 * D), lambda i: (0, 0)),
            pl.BlockSpec((8, 3 * D), lambda i: (0, 0)),
            pl.BlockSpec((S5_QT * S5_PAIRS, TN // S5_CHUNK, S5_CHUNK * PAIR_BLK),
                         lambda i: (0, jnp.minimum(i, N_LAT_BLK - 1), 0)),
            pl.BlockSpec((S5_QT * S5_PAIRS, R_CTX // S5_CHUNK, S5_CHUNK * PAIR_BLK),
                         lambda i: (0, 0, 0), **once),
            pl.BlockSpec((TN, EVEN_IN - S5_W), tail_blk),
            pl.BlockSpec((S5_W, S5_W), lambda i: (0, 0), **once),
            row(S5_W), row(SG_W), row(SG_W),
            pl.BlockSpec((SG_HEADS // 2, SG_CHUNK, 2 * SG_CHUNK), lambda i: (0, 0, 0)),
            pl.BlockSpec((SG_CHUNK, SG_W), lambda i: (0, 0)),
            pl.BlockSpec((S5_W + SG_W, D), lambda i: (0, 0), **once),
            row(D), row(D),
            pl.BlockSpec((D, ODD_IN), lambda i: (0, 0), **once),
            pl.BlockSpec((2, N_LAT // GRID_W, LANES), lambda i: (0, 0, 0)),
            pl.BlockSpec((2, GRID_W, LANES), lambda i: (0, 0, 0)),
        ],
        out_specs=[
            pl.BlockSpec((TN, D), tail_blk),
            pl.BlockSpec((TN, dq), proj_blk),
            pl.BlockSpec((TN, dkv), proj_blk),
            pl.BlockSpec((dkv, TN), lambda i: (0, jnp.maximum(i - 1, 0))),
            pl.BlockSpec((TN, dq), proj_blk),
        ],
        out_shape=[
            jax.ShapeDtypeStruct((R_ALL, D), F32),
            jax.ShapeDtypeStruct((R_ALL, dq), BF16),
            jax.ShapeDtypeStruct((R_ALL, dkv), BF16),
            jax.ShapeDtypeStruct((dkv, R_ALL), BF16),
            jax.ShapeDtypeStruct((R_ALL, dq), BF16),
        ],
        scratch_shapes=[pltpu.VMEM((TN, D), F32), pltpu.VMEM((D, ODD_IN), BF16),
                        pltpu.VMEM((S5_W + SG_W, D), BF16), pltpu.VMEM((S5_QT, TN, LANES), F32)],
        compiler_params=_fusing_params(19, skip=(2, 3, 17)),
        name="mid",
    )(x2, ctx2, mod0, mod1, ys_lat, ys_ctx, rest, glu_w, glu_b, sg_g, sg_b, sg_w, sg_bias, w_out,
      ln_g, ln_b, w_in1, row_tab, col_tab)


N_QBLK = N_LAT // ATT_BLK
GRP = N_HEADS // N_KV
ATT_SUB = 4
ATT_ROWS = ATT_SUB * ATT_BLK
N_QSTEP = N_QBLK // ATT_SUB
N_ATT_STEPS = B * N_QSTEP
assert ATT_ROWS == TN
assert WINDOW == ATT_BLK
ONES_ROWS = 16


def _attn_kernel(sink_ref, q_ref, kp_ref, kc_ref, kn_ref, kx_ref, vp_ref, vc_ref, vn_ref, vx_ref,
                 g_ref, z_ref, mod_ref, wout_ref, lng_ref, lnb_ref, out_ref, o_scr, wout_scr):
    step = pl.program_id(0)

    @pl.when(step == 0)
    def _():
        o_scr[...] = jnp.zeros((ATT_ROWS, N_HEADS * HD), BF16)
        wout_scr[...] = wout_ref[...].astype(BF16)

    t = jnp.maximum(step - 1, 0)
    gate = mod_ref[pl.ds(_block_mod_row(t), 1), 2 * D:3 * D]
    mix = (o_scr[...].astype(F32) * _silu(g_ref[...].astype(F32))).astype(BF16)
    y = jnp.dot(mix, wout_scr[...], preferred_element_type=F32) * gate
    out_ref[...] = _layer_norm(ALPHA * z_ref[...] + y, lng_ref[...], lnb_ref[...])

    i = jnp.minimum(step, N_ATT_STEPS - 1) % N_QSTEP
    half = ATT_BLK // 2
    n_win = 2 * ATT_BLK + half
    n_keys = n_win + N_CTX
    nq = GRP * half
    k_win = jnp.concatenate([kp_ref[...], kc_ref[...], kn_ref[...]], axis=0)
    vt_win = jnp.concatenate([vp_ref[...], vc_ref[...], vn_ref[...]], axis=1)
    k_ctx = kx_ref[...]
    vt_ctx = vx_ref[...]

    kpos = lax.broadcasted_iota(jnp.int32, (half, half), 0)
    qpos = lax.broadcasted_iota(jnp.int32, (half, half), 1)
    tile_q = lambda a: jnp.concatenate([a] * GRP, axis=1)
    neg = lambda ok: tile_q(jnp.where(ok, 0.0, NEG_INF))
    everyone = kpos >= 0
    qgrp = lax.broadcasted_iota(jnp.int32, (1, nq), 1) // half
    ones = jnp.ones((ONES_ROWS, n_keys), BF16)
    q = q_ref[...]

    halves = [(u, v) for u in range(ATT_SUB) for v in range(2)]
    units = [(n, h) for n in range(len(halves)) for h in range(N_KV)]
    k_all, vt_all, bias = [], [], []
    for u, v in halves:
        first = u * ATT_BLK + v * half
        k_all.append(jnp.concatenate([k_win[first:first + n_win], k_ctx], axis=0))
        vt_all.append(jnp.concatenate([vt_win[:, first:first + n_win], vt_ctx], axis=1))
        in_prev = [blk for blk in range(5) if first + blk * half < ATT_BLK]
        in_next = [blk for blk in range(5) if first + blk * half >= ATT_BLK + ATT_ROWS]
        blocks = {}
        for blk in sorted(set([0, 4] + in_prev + in_next)):
            ok = (kpos >= qpos) if blk == 0 else ((kpos <= qpos) if blk == 4 else everyone)
            if blk in in_prev:
                ok = ok & (i > 0)
            if blk in in_next:
                ok = ok & (i < N_QSTEP - 1)
            blocks[blk] = neg(ok)
        bias.append(blocks)
    scores = []
    for n, h in units:
        u, v = halves[n]
        r0 = u * ATT_BLK + v * half
        kh = k_all[n][:, h * HD:(h + 1) * HD]
        qh = jnp.concatenate(
            [q[r0:r0 + half, (h * GRP + g) * HD:(h * GRP + g + 1) * HD] for g in range(GRP)],
            axis=0)
        scores.append(lax.dot_general(kh, qh, (((1,), (1,)), ((), ())),
                                      preferred_element_type=F32))
    probs = []
    for (n, h), s in zip(units, scores):
        s = jnp.concatenate(
            [s[blk * half:(blk + 1) * half] + bias[n][blk] if blk in bias[n]
             else s[blk * half:(blk + 1) * half] for blk in range(5)] + [s[n_win:]], axis=0)
        sink = jnp.zeros((1, nq), F32)
        for g in range(GRP):
            sink = jnp.where(qgrp == g, sink_ref[h * GRP + g] * LOG2E, sink)
        m = jnp.maximum(jnp.max(s, axis=0, keepdims=True), sink)
        probs.append((jnp.exp2(s - m).astype(BF16), jnp.exp2(sink - m)))
    outs = [[None] * N_HEADS for _ in halves]
    for (n, h), (p, p_sink) in zip(units, probs):
        vt1 = jnp.concatenate([vt_all[n][h * HD:(h + 1) * HD], ones], axis=0)
        ov = jnp.dot(vt1, p, preferred_element_type=F32)
        o_t = ov[:HD] / (ov[HD:HD + 1] + p_sink)
        for g in range(GRP):
            outs[n][h * GRP + g] = o_t[:, g * half:(g + 1) * half]
    for u in range(ATT_SUB):
        o_t = jnp.concatenate(
            [jnp.concatenate([outs[2 * u][hh], outs[2 * u + 1][hh]], axis=1)
             for hh in range(N_HEADS)], axis=0)
        o_scr[u * ATT_BLK:(u + 1) * ATT_BLK, :] = jnp.transpose(o_t).astype(BF16)


def _attention_tail(sink, q, k, vt, g, z1, mod, w_out, ln_g, ln_b):
    dq = N_HEADS * HD
    dkv = N_KV * HD
    ctx_blk0 = R_LAT // N_CTX
    cur = lambda s: jnp.minimum(s, N_ATT_STEPS - 1)
    batch = lambda s: cur(s) // N_QSTEP
    qstep = lambda s: cur(s) % N_QSTEP
    prev_blk = lambda s: batch(s) * N_QBLK + jnp.maximum(ATT_SUB * qstep(s) - 1, 0)
    next_blk = lambda s: batch(s) * N_QBLK + jnp.minimum(ATT_SUB * (qstep(s) + 1), N_QBLK - 1)
    tail = lambda s: jnp.maximum(s - 1, 0)
    once = dict(pipeline_mode=pl.Buffered(1))
    row = lambda n: pl.BlockSpec((1, n), lambda s, sk: (0, 0))
    return pl.pallas_call(
        _attn_kernel,
        grid_spec=pltpu.PrefetchScalarGridSpec(
            num_scalar_prefetch=1,
            grid=(N_ATT_STEPS + 1,),
            in_specs=[
                pl.BlockSpec((ATT_ROWS, dq), lambda s, sk: (cur(s), 0)),
                pl.BlockSpec((ATT_BLK, dkv), lambda s, sk: (prev_blk(s), 0)),
                pl.BlockSpec((ATT_ROWS, dkv), lambda s, sk: (cur(s), 0)),
                pl.BlockSpec((ATT_BLK, dkv), lambda s, sk: (next_blk(s), 0)),
                pl.BlockSpec((N_CTX, dkv), lambda s, sk: (ctx_blk0 + batch(s), 0)),
                pl.BlockSpec((dkv, ATT_BLK), lambda s, sk: (0, prev_blk(s))),
                pl.BlockSpec((dkv, ATT_ROWS), lambda s, sk: (0, cur(s))),
                pl.BlockSpec((dkv, ATT_BLK), lambda s, sk: (0, next_blk(s))),
                pl.BlockSpec((dkv, N_CTX), lambda s, sk: (0, ctx_blk0 + batch(s))),
                pl.BlockSpec((ATT_ROWS, dq), lambda s, sk: (tail(s), 0)),
                pl.BlockSpec((ATT_ROWS, D), lambda s, sk: (tail(s), 0)),
                pl.BlockSpec((8, 3 * D), lambda s, sk: (0, 0)),
                pl.BlockSpec((dq, D), lambda s, sk: (0, 0), **once),
                row(D), row(D),
            ],
            out_specs=pl.BlockSpec((ATT_ROWS, D), lambda s, sk: (tail(s), 0)),
            scratch_shapes=[pltpu.VMEM((ATT_ROWS, dq), BF16), pltpu.VMEM((dq, D), BF16)],
        ),
        out_shape=jax.ShapeDtypeStruct((R_LAT, D), F32),
        compiler_params=_fusing_params(16, skip=(0, 12)),
        name="attention",
    )(sink, q, k, k, k, k, vt, vt, vt, vt, g, z1, mod, w_out, ln_g, ln_b)


def kernel(x, c, ctx, c_ctx, mod_w, mod_b, ln_g, ln_b, e_w_in, e_w_out, s5_lam_re, s5_lam_im,
           s5_log_dt, s5_b_re, s5_b_im, s5_c_re, s5_c_im, s5_d, glu_w, glu_b, sg_ln_g, sg_ln_b,
           sg_w, sg_b, o_w_in, o_w_out, o_sink):
    x2 = x.reshape(R_LAT, D)
    ctx2 = ctx.reshape(R_CTX, D)
    mod = _modulation(c, c_ctx, mod_w, mod_b)

    vp, rest = _in0(x2, ctx2, mod[0], e_w_in[0])
    t_mat, e_mat, ft_mat, al_re, al_im = _s5_prep(
        s5_lam_re[0], s5_lam_im[0], s5_log_dt[0], s5_b_re[0], s5_b_im[0], s5_c_re[0], s5_c_im[0],
        s5_d[0])
    ys_lat, ys_ctx = _s5_out(vp, al_re, al_im, e_mat, t_mat, ft_mat)
    sg_bias = jnp.repeat(jnp.transpose(sg_b[0]), SG_W // SG_HEADS, axis=1)
    sg_w2 = jnp.transpose(sg_w[0].reshape(SG_HEADS // 2, 2, SG_CHUNK, SG_CHUNK), (0, 2, 1, 3))
    sg_w2 = sg_w2.reshape(SG_HEADS // 2, SG_CHUNK, 2 * SG_CHUNK).astype(BF16)
    row_tab, col_tab = _rope_tables()
    z1, q, k, vt, g = _mid(
        x2, ctx2, mod[0], mod[1], ys_lat, ys_ctx, rest, glu_w[0].astype(BF16),
        glu_b[0].reshape(1, S5_W), sg_ln_g[0].reshape(1, SG_W), sg_ln_b[0].reshape(1, SG_W),
        sg_w2, sg_bias, e_w_out[0], ln_g[0].reshape(1, D),
        ln_b[0].reshape(1, D), o_w_in[0], row_tab, col_tab)

    out = _attention_tail(o_sink[0], q, k, vt, g, z1, mod[1], o_w_out[0],
                          ln_g[1].reshape(1, D), ln_b[1].reshape(1, D))
    return out.reshape(B, N_LAT, D)
```

```python
import math

import jax
import jax.numpy as jnp
from jax import lax
from jax.experimental import pallas as pl
from jax.experimental.pallas import tpu as pltpu

F32 = jnp.float32
BF16 = jnp.bfloat16

D = 1024
B = 2
N_LAT = 8192
N_CTX = 256
DEPTH = 2
GRID_W = 64
S5_W = 512
S5_GC = 16
S5_G = 32
S5_P = 64
SG_W = 512
SG_HEADS = 8
SG_CHUNK = 128
N_HEADS = 16
N_KV = 4
HD = 64
WINDOW = 128
ATT_BLK = 128
ROPE_BASE = 10000.0
NEG_INF = -1e30
LN_EPS = 1e-5
ALPHA = (2 * DEPTH) ** 0.25
LOG2E = math.log2(math.e)
EVEN_IN = 2 * S5_W + 3 * SG_W
ODD_IN = 2 * N_HEADS * HD + 2 * N_KV * HD

LANES = 128
VMEM_LIMIT = 56 * 1024 * 1024
SMALL_CALL_VMEM_LIMIT = 24 * 1024 * 1024
S5_OUT_VMEM_LIMIT = 40 * 1024 * 1024


def _fusing_params(n_inputs, skip=(), vmem_limit=VMEM_LIMIT):
    return pltpu.CompilerParams(vmem_limit_bytes=vmem_limit,
                                allow_input_fusion=[n not in skip for n in range(n_inputs)])

R_LAT = B * N_LAT
R_CTX = B * N_CTX
R_ALL = R_LAT + R_CTX
TN = 512
N_LAT_BLK = R_LAT // TN
N_BLK = R_ALL // TN
LAT_BLK_PER_BATCH = N_LAT // TN
CTX_MOD_ROW = B

S5_CHUNK = 8
S5_QT = S5_W // LANES
S5_GPT = LANES // S5_GC
S5_ST = S5_GPT * S5_P
S5_PAIRS = S5_GPT // 2
PAIR_BLK = 2 * S5_GC
LAT_CHUNKS = N_LAT // S5_CHUNK
CTX_CHUNKS = N_CTX // S5_CHUNK
BATCH_CHUNKS = CTX_CHUNKS + LAT_CHUNKS
N_SEG = 16
SEG_CHUNKS = BATCH_CHUNKS // N_SEG
BWD_RESET_STEP = LAT_CHUNKS - (N_SEG - 1) * SEG_CHUNKS


def _sigmoid(x):
    return 1.0 / (1.0 + jnp.exp(-x))


def _silu(x):
    return x * _sigmoid(x)


def _gelu_tanh(x):
    return 0.5 * x * (1.0 + jnp.tanh(math.sqrt(2.0 / math.pi) * (x + 0.044715 * (x * x * x))))


def _layer_norm(x, g, b):
    mu = jnp.mean(x, axis=-1, keepdims=True)
    xc = x - mu
    var = jnp.mean(xc * xc, axis=-1, keepdims=True)
    return xc * lax.rsqrt(var + LN_EPS) * g + b


def _split_bf16(x):
    hi = x.astype(BF16)
    return hi, (x - hi.astype(F32)).astype(BF16)


def _dot_split(x, y_split, dims):
    x_hi, x_lo = _split_bf16(x)
    y_hi, y_lo = y_split
    dot = lambda a, b: lax.dot_general(a, b, dims, preferred_element_type=F32)
    return dot(x_hi, y_hi) + (dot(x_hi, y_lo) + dot(x_lo, y_hi))


def _block_mod_row(i):
    return jnp.where(i < N_LAT_BLK, i // LAT_BLK_PER_BATCH, CTX_MOD_ROW)


def _mod_kernel(cv_ref, w_ref, b_ref, o_ref):
    s = _silu(cv_ref[...])
    o_ref[0] = _dot_split(s, _split_bf16(w_ref[0]), (((1,), (0,)), ((), ()))) + b_ref[0]


def _modulation(c, c_ctx, mod_w, mod_b):
    cv = jnp.concatenate([c, c_ctx[None], jnp.zeros((8 - B - 1, D), F32)], axis=0)
    tn = 1024
    return pl.pallas_call(
        _mod_kernel,
        grid=(DEPTH, 3 * D // tn),
        in_specs=[
            pl.BlockSpec((8, D), lambda l, j: (0, 0)),
            pl.BlockSpec((1, D, tn), lambda l, j: (l, 0, j)),
            pl.BlockSpec((1, 1, tn), lambda l, j: (l, 0, j)),
        ],
        out_specs=pl.BlockSpec((1, 8, tn), lambda l, j: (l, 0, j)),
        out_shape=jax.ShapeDtypeStruct((DEPTH, 8, 3 * D), F32),
        compiler_params=_fusing_params(3, vmem_limit=SMALL_CALL_VMEM_LIMIT),
        name="modulation",
    )(cv, mod_w, mod_b.reshape(DEPTH, 1, 3 * D))


def _block_transpose4(tiles):
    tiles = list(tiles)
    blk = lax.broadcasted_iota(jnp.int32, tiles[0].shape, 1) // PAIR_BLK
    for dist in (2, 1):
        keep = (blk & dist) == 0
        for i in range(len(tiles)):
            if i & dist:
                continue
            lo, hi = tiles[i], tiles[i + dist]
            tiles[i] = jnp.where(keep, lo, pltpu.roll(hi, dist * PAIR_BLK, axis=1))
            tiles[i + dist] = jnp.where(keep, pltpu.roll(lo, LANES - dist * PAIR_BLK, axis=1), hi)
    return tiles


def _in0_kernel(x_ref, ctx_ref, mod_ref, w_ref, vp_ref, rest_ref, xa_scr, wbf_scr):
    L = S5_CHUNK
    i = pl.program_id(0)

    @pl.when(i == 0)
    def _():
        wbf_scr[...] = w_ref[...].astype(BF16)

    z = jnp.where(i < N_LAT_BLK, x_ref[...], ctx_ref[...])
    m = mod_ref[pl.ds(_block_mod_row(i), 1), :]
    h = z * (1.0 + m[:, D:2 * D]) + m[:, :D]
    p = jnp.dot(h.astype(BF16), wbf_scr[...], preferred_element_type=F32)
    rest_ref[...] = p[:, S5_W:].astype(BF16)
    for q in range(S5_QT):
        xa_scr[q] = p[:, q * LANES:(q + 1) * LANES]
        tiles = [xa_scr[q, pl.ds(t, TN // L, stride=L), :] for t in range(L)]
        halves = [_block_transpose4(tiles[k * S5_PAIRS:(k + 1) * S5_PAIRS])
                  for k in range(L // S5_PAIRS)]
        for pr in range(S5_PAIRS):
            vp_ref[q * S5_PAIRS + pr] = jnp.concatenate(
                [half[pr] for half in halves], axis=-1).astype(BF16)


def _in0(x2, ctx2, mod, w_in):
    pair_w = S5_CHUNK * PAIR_BLK
    return pl.pallas_call(
        _in0_kernel,
        grid=(N_BLK,),
        in_specs=[
            pl.BlockSpec((TN, D), lambda i: (jnp.minimum(i, N_LAT_BLK - 1), 0)),
            pl.BlockSpec((R_CTX, D), lambda i: (0, 0)),
            pl.BlockSpec((8, 3 * D), lambda i: (0, 0)),
            pl.BlockSpec((D, EVEN_IN), lambda i: (0, 0)),
        ],
        out_specs=[
            pl.BlockSpec((S5_QT * S5_PAIRS, TN // S5_CHUNK, pair_w), lambda i: (0, i, 0)),
            pl.BlockSpec((TN, EVEN_IN - S5_W), lambda i: (i, 0)),
        ],
        out_shape=[
            jax.ShapeDtypeStruct((S5_QT * S5_PAIRS, R_ALL // S5_CHUNK, pair_w), BF16),
            jax.ShapeDtypeStruct((R_ALL, EVEN_IN - S5_W), BF16),
        ],
        scratch_shapes=[pltpu.VMEM((S5_QT, TN, LANES), F32), pltpu.VMEM((D, EVEN_IN), BF16)],
        compiler_params=_fusing_params(4, skip=(2,)),
        name="in0",
    )(x2, ctx2, mod, w_in)


def _s5_prep_kernel(rows_ref, bre_ref, bim_ref, cre_ref, cim_ref, dsk_ref, t_ref, e_ref, ft_ref,
                    alr_ref, ali_ref):
    L = S5_CHUNK
    tile = (LANES, S5_ST)
    same_group = (lax.broadcasted_iota(jnp.int32, tile, 0) // S5_GC
                  == lax.broadcasted_iota(jnp.int32, tile, 1) // S5_P)
    nt_dims = (((1,), (1,)), ((), ()))
    nn_dims = (((1,), (0,)), ((), ()))
    pick_ch = (lax.broadcasted_iota(jnp.int32, (LANES, S5_GC), 0) % S5_GC
               == lax.broadcasted_iota(jnp.int32, (LANES, S5_GC), 1)).astype(BF16)
    rep_st = (lax.broadcasted_iota(jnp.int32, (S5_P, S5_ST), 0)
              == lax.broadcasted_iota(jnp.int32, (S5_P, S5_ST), 1) % S5_P).astype(BF16)

    def spread_b(b):
        hi, lo = _split_bf16(b)
        dot = lambda v: lax.dot_general(pick_ch, v, nt_dims, preferred_element_type=F32)
        return jnp.where(same_group, dot(hi) + dot(lo), 0.0)

    def spread_c(c):
        hi, lo = _split_bf16(c)
        dot = lambda v: lax.dot_general(v, rep_st, nn_dims, preferred_element_type=F32)
        return jnp.where(same_group, dot(hi) + dot(lo), 0.0)

    pair_rows = lambda pr: slice(pr * PAIR_BLK, (pr + 1) * PAIR_BLK)
    pair_lanes = lambda pr: slice(pr * LANES, (pr + 1) * LANES)
    lag_strip = []
    for d in range(2):
        lam_re = rows_ref[d, 0]
        lam_im = rows_ref[d, 1]
        dt = jnp.exp(rows_ref[d, 2])
        pw = []
        for k in range(L + 1):
            mag = jnp.exp(lam_re * dt * float(k))
            ang = lam_im * dt * float(k)
            pw.append((mag * jnp.cos(ang), mag * jnp.sin(ang)))
        alr_ref[d] = pw[L][0]
        ali_ref[d] = pw[L][1]
        den = lam_re * lam_re + lam_im * lam_im
        nr = pw[1][0] - 1.0
        ni = pw[1][1]
        f_re = (nr * lam_re + ni * lam_im) / den
        f_im = (ni * lam_re - nr * lam_im) / den
        b_re = spread_b(bre_ref[d])
        b_im = spread_b(bim_ref[d])
        bb_re = f_re * b_re - f_im * b_im
        bb_im = f_re * b_im + f_im * b_re
        c_re = spread_c(cre_ref[d])
        c_im = spread_c(cim_ref[d])
        col_re = slice(2 * d * LANES, (2 * d + 1) * LANES)
        col_im = slice((2 * d + 1) * LANES, (2 * d + 2) * LANES)
        strips = []
        for pr in range(S5_PAIRS):
            blk = lambda a: a[pair_rows(pr), pair_lanes(pr)]
            p_re = [blk(jnp.broadcast_to(pw[k][0], tile)) for k in range(L + 1)]
            p_im = [blk(jnp.broadcast_to(pw[k][1], tile)) for k in range(L + 1)]
            bbr, bbi, cr, ci = blk(bb_re), blk(bb_im), blk(c_re), blk(c_im)
            for k in range(L):
                s = (L - 1 - k) if d == 0 else k
                rows = slice(s * PAIR_BLK, (s + 1) * PAIR_BLK)
                e_ref[0, pr, rows, col_re] = (p_re[k] * bbr - p_im[k] * bbi).astype(BF16)
                e_ref[0, pr, rows, col_im] = (p_re[k] * bbi + p_im[k] * bbr).astype(BF16)
            cw = [(cr * p_re[k] - ci * p_im[k], -(cr * p_im[k] + ci * p_re[k]))
                  for k in range(L + 1)]
            for t in range(L):
                k = (t + 1) if d == 0 else (L - t)
                rows = slice(t * PAIR_BLK, (t + 1) * PAIR_BLK)
                ft_ref[0, pr, rows, col_re] = cw[k][0].astype(BF16)
                ft_ref[0, pr, rows, col_im] = cw[k][1].astype(BF16)
            lags = range(L) if d == 0 else range(L - 1, -1, -1)
            c_stack = jnp.concatenate(
                [jnp.concatenate([cw[k][0], cw[k][1]], axis=-1) for k in lags], axis=0)
            strips.append(_dot_split(jnp.concatenate([bbr, bbi], axis=-1),
                                     _split_bf16(c_stack), nt_dims))
        lag_strip.append(strips)

    strip_w = 2 * L * PAIR_BLK
    mid = (L - 1) * PAIR_BLK
    widen = lambda a: jnp.concatenate([a, jnp.zeros((a.shape[0], strip_w - a.shape[1]), F32)], axis=-1)
    row = lax.broadcasted_iota(jnp.int32, (PAIR_BLK, strip_w), 0)
    lane = lax.broadcasted_iota(jnp.int32, (PAIR_BLK, strip_w), 1)
    d_row = widen(dsk_ref[0])
    for pr in range(S5_PAIRS):
        strip = (widen(lag_strip[1][pr])
                 + pltpu.roll(widen(lag_strip[0][pr]), mid, axis=1)
                 + jnp.where(lane == row + mid,
                             pltpu.roll(d_row, mid - pr * PAIR_BLK, axis=1), 0.0))
        for s in range(L):
            shift = (strip_w - (L - 1 - s) * PAIR_BLK) % strip_w
            blk = pltpu.roll(strip, shift, axis=1) if shift else strip
            t_ref[0, pr, s * PAIR_BLK:(s + 1) * PAIR_BLK, :] = blk[:, :L * PAIR_BLK].astype(BF16)


def _s5_prep(lam_re, lam_im, log_dt, b_re, b_im, c_re, c_im, d_skip):
    L = S5_CHUNK
    n_st = S5_G * S5_P
    pair_w = L * PAIR_BLK
    rows = jnp.stack([lam_re.reshape(2, 1, n_st), lam_im.reshape(2, 1, n_st),
                      jnp.repeat(log_dt, S5_P, axis=1).reshape(2, 1, n_st)], axis=1)
    b_spec = pl.BlockSpec((2, S5_ST, S5_GC), lambda q: (0, q, 0))
    c_spec = pl.BlockSpec((2, LANES, S5_P), lambda q: (0, q, 0))
    return pl.pallas_call(
        _s5_prep_kernel,
        grid=(S5_QT,),
        in_specs=[
            pl.BlockSpec((2, 3, 1, S5_ST), lambda q: (0, 0, 0, q)),
            b_spec, b_spec, c_spec, c_spec,
            pl.BlockSpec((1, 1, LANES), lambda q: (q, 0, 0)),
        ],
        out_specs=[
            pl.BlockSpec((1, S5_PAIRS, pair_w, pair_w), lambda q: (q, 0, 0, 0)),
            pl.BlockSpec((1, S5_PAIRS, pair_w, 4 * LANES), lambda q: (q, 0, 0, 0)),
            pl.BlockSpec((1, S5_PAIRS, pair_w, 4 * LANES), lambda q: (q, 0, 0, 0)),
            pl.BlockSpec((2, 1, S5_ST), lambda q: (q, 0, 0)),
            pl.BlockSpec((2, 1, S5_ST), lambda q: (q, 0, 0)),
        ],
        out_shape=[
            jax.ShapeDtypeStruct((S5_QT, S5_PAIRS, pair_w, pair_w), BF16),
            jax.ShapeDtypeStruct((S5_QT, S5_PAIRS, pair_w, 4 * LANES), BF16),
            jax.ShapeDtypeStruct((S5_QT, S5_PAIRS, pair_w, 4 * LANES), BF16),
            jax.ShapeDtypeStruct((2 * S5_QT, 1, S5_ST), F32),
            jax.ShapeDtypeStruct((2 * S5_QT, 1, S5_ST), F32),
        ],
        compiler_params=_fusing_params(6, vmem_limit=SMALL_CALL_VMEM_LIMIT),
        name="s5_prep",
    )(rows, b_re.reshape(2, n_st, S5_GC), b_im.reshape(2, n_st, S5_GC),
      c_re.reshape(2, S5_W, S5_P), c_im.reshape(2, S5_W, S5_P), d_skip.reshape(S5_QT, 1, LANES))


N_SLAB = S5_ST // LANES


def _cmul(ar, ai, br, bi):
    return ar * br - ai * bi, ar * bi + ai * br


def _cpow(ar, ai, n):
    res = None
    while n:
        if n & 1:
            res = (ar, ai) if res is None else _cmul(res[0], res[1], ar, ai)
        n >>= 1
        if n:
            ar, ai = _cmul(ar, ai, ar, ai)
    return res


def _scan_boundary_states(loc_ref, out_ref, ar_ref, ai_ref, backward):
    a_row = [(ar_ref[0][:, k * LANES:(k + 1) * LANES], ai_ref[0][:, k * LANES:(k + 1) * LANES])
             for k in range(N_SLAB)]
    a_tile = [(jnp.broadcast_to(r, (N_SEG, LANES)), jnp.broadcast_to(i, (N_SEG, LANES)))
              for r, i in a_row]

    def sweep(lo, hi, carry, store):
        def step(i, carry):
            off = (SEG_CHUNKS - 1 - i) if backward else i
            rows = pl.ds(off, N_SEG, stride=SEG_CHUNKS)
            new = []
            for k, (cr, ci) in enumerate(carry):
                xr = loc_ref[k, rows, :]
                xi = loc_ref[N_SLAB + k, rows, :]
                if store:
                    out_ref[k, rows, :] = cr
                    out_ref[N_SLAB + k, rows, :] = ci
                nr, ni = _cmul(a_tile[k][0], a_tile[k][1], cr, ci)
                new.append((nr + xr, ni + xi))
            return tuple(new)
        return lax.fori_loop(lo, hi, step, carry, unroll=True)

    def full(carry, store):
        if not backward:
            return sweep(0, SEG_CHUNKS, carry, store)
        carry = sweep(0, BWD_RESET_STEP, carry, store)
        first = lax.broadcasted_iota(jnp.int32, (N_SEG, LANES), 0) == 0
        carry = tuple((jnp.where(first, 0.0, cr), jnp.where(first, 0.0, ci)) for cr, ci in carry)
        return sweep(BWD_RESET_STEP, SEG_CHUNKS, carry, store)

    zero_tile = jnp.zeros((N_SEG, LANES), F32)
    ends = full(tuple((zero_tile, zero_tile) for _ in range(N_SLAB)), False)

    starts = []
    for k, (er, ei) in enumerate(ends):
        pr, pi = _cpow(a_row[k][0], a_row[k][1], SEG_CHUNKS)
        if backward:
            order = range(N_SEG - 1, -1, -1)
            g = (er[0:1], ei[0:1])
        else:
            order = range(N_SEG)
            g = (jnp.zeros((1, LANES), F32), jnp.zeros((1, LANES), F32))
        rows_r = [None] * N_SEG
        rows_i = [None] * N_SEG
        for j in order:
            rows_r[j], rows_i[j] = g
            nr, ni = _cmul(pr, pi, g[0], g[1])
            g = (nr + er[j:j + 1], ni + ei[j:j + 1])
        starts.append((jnp.concatenate(rows_r, axis=0), jnp.concatenate(rows_i, axis=0)))
    full(tuple(starts), True)


def _s5_out_kernel(vl_ref, vc_ref, ar_ref, ai_ref, e_ref, t_ref, ft_ref, yl_ref, yc_ref, st_ref):
    n_dir = 2 * N_SLAB
    rows = [jnp.concatenate([vc_ref[pr], vl_ref[pr]], axis=0) for pr in range(S5_PAIRS)]
    for pr in range(S5_PAIRS):
        st = jnp.dot(rows[pr], e_ref[0, pr], preferred_element_type=F32)
        for comp in range(4):
            st_ref[comp * N_SLAB + pr] = st[:, comp * LANES:(comp + 1) * LANES]
    for d in range(2):
        st_d = st_ref.at[pl.ds(d * n_dir, n_dir)]
        _scan_boundary_states(st_d, st_d, ar_ref.at[pl.ds(d, 1)], ai_ref.at[pl.ds(d, 1)], d == 1)

    for pr in range(S5_PAIRS):
        st = jnp.concatenate([st_ref[comp * N_SLAB + pr] for comp in range(4)],
                             axis=-1).astype(BF16)
        y = (jnp.dot(rows[pr], t_ref[0, pr], preferred_element_type=F32)
             + lax.dot_general(st, ft_ref[0, pr], (((1,), (1,)), ((), ())),
                               preferred_element_type=F32))
        yc_ref[pr] = y[:CTX_CHUNKS]
        yl_ref[pr] = y[CTX_CHUNKS:]


def _s5_out(vp, al_re, al_im, e_mat, t_mat, ft_mat):
    pair_w = S5_CHUNK * PAIR_BLK
    ctx_blk0 = (R_LAT // S5_CHUNK) // CTX_CHUNKS
    return pl.pallas_call(
        _s5_out_kernel,
        grid=(S5_QT, B),
        in_specs=[
            pl.BlockSpec((S5_PAIRS, LAT_CHUNKS, pair_w), lambda q, b: (q, b, 0)),
            pl.BlockSpec((S5_PAIRS, CTX_CHUNKS, pair_w), lambda q, b: (q, ctx_blk0 + b, 0)),
            pl.BlockSpec((2, 1, S5_ST), lambda q, b: (q, 0, 0)),
            pl.BlockSpec((2, 1, S5_ST), lambda q, b: (q, 0, 0)),
            pl.BlockSpec((1, S5_PAIRS, pair_w, 4 * LANES), lambda q, b: (q, 0, 0, 0)),
            pl.BlockSpec((1, S5_PAIRS, pair_w, pair_w), lambda q, b: (q, 0, 0, 0)),
            pl.BlockSpec((1, S5_PAIRS, pair_w, 4 * LANES), lambda q, b: (q, 0, 0, 0)),
        ],
        out_specs=[
            pl.BlockSpec((S5_PAIRS, LAT_CHUNKS, pair_w), lambda q, b: (q, b, 0)),
            pl.BlockSpec((S5_PAIRS, CTX_CHUNKS, pair_w), lambda q, b: (q, b, 0)),
        ],
        out_shape=[
            jax.ShapeDtypeStruct((S5_QT * S5_PAIRS, B * LAT_CHUNKS, pair_w), F32),
            jax.ShapeDtypeStruct((S5_QT * S5_PAIRS, B * CTX_CHUNKS, pair_w), F32),
        ],
        scratch_shapes=[pltpu.VMEM((4 * N_SLAB, BATCH_CHUNKS, LANES), F32)],
        compiler_params=pltpu.CompilerParams(vmem_limit_bytes=S5_OUT_VMEM_LIMIT),
        name="s5_out",
    )(vp, vp, al_re, al_im, e_mat, t_mat, ft_mat)


def _tail0_stages(z, ys, rest_ref, gate, gluw_ref, glub_ref, sg_g_ref, sg_b_ref, sgw_ref,
                  sgbias_ref, wout_ref, lng_ref, lnb_ref):
    ga = rest_ref[:, 0:S5_W].astype(F32)
    u = rest_ref[:, S5_W:2 * S5_W].astype(F32)
    v = rest_ref[:, 2 * S5_W:3 * S5_W].astype(F32)
    gb = rest_ref[:, 3 * S5_W:4 * S5_W].astype(F32)

    ya = _gelu_tanh(ys)
    glu = jnp.dot(ya.astype(BF16), gluw_ref[...], preferred_element_type=F32) + glub_ref[...]
    yield None
    ya = ya * _sigmoid(glu) * _silu(ga)
    yield None

    vn = _layer_norm(v, sg_g_ref[...], sg_b_ref[...]).astype(BF16)
    lane = lax.broadcasted_iota(jnp.int32, (SG_CHUNK, LANES), 1)
    first_head = lane < (SG_W // SG_HEADS)
    chunks = []
    for c in range(TN // SG_CHUNK):
        tiles = []
        for j in range(SG_W // LANES):
            vt = vn[c * SG_CHUNK:(c + 1) * SG_CHUNK, j * LANES:(j + 1) * LANES]
            zero = jnp.zeros_like(vt)
            v2 = jnp.concatenate([jnp.where(first_head, vt, zero), jnp.where(first_head, zero, vt)],
                                 axis=0)
            tiles.append(jnp.dot(sgw_ref[j], v2, preferred_element_type=F32))
        chunks.append(jnp.concatenate(tiles, axis=-1) + sgbias_ref[...])
    s = jnp.concatenate(chunks, axis=0)
    yb = u * s * _silu(gb)
    yield None

    mix = jnp.concatenate([ya, yb], axis=-1).astype(BF16)
    y = jnp.dot(mix, wout_ref[...], preferred_element_type=F32) * gate
    yield None
    yield _layer_norm(ALPHA * z + y, lng_ref[...], lnb_ref[...])


def _rope_tables():
    nf = HD // 4
    n_rows = N_LAT // GRID_W
    lane = jnp.arange(LANES)
    inv = ROPE_BASE ** (-(lane % nf).astype(F32) / nf)
    by_row = ((lane % HD) // (HD // 2) == 0)[None, :]
    sign = jnp.where((lane % (HD // 2)) < nf, -1.0, 1.0)[None, :]
    row_ang = jnp.arange(n_rows, dtype=F32)[:, None] * inv[None, :]
    col_ang = jnp.arange(GRID_W, dtype=F32)[:, None] * inv[None, :]
    zero = jnp.zeros((), F32)
    row_tab = jnp.stack([jnp.where(by_row, jnp.cos(row_ang), zero),
                         jnp.where(by_row, sign * jnp.sin(row_ang), zero)])
    col_tab = jnp.stack([jnp.where(by_row, zero, jnp.cos(col_ang)),
                         jnp.where(by_row, zero, sign * jnp.sin(col_ang))])
    return row_tab, col_tab


def _rope_tile(x, cos, sin, first_half):
    nf = HD // 4
    partner = jnp.where(first_half, pltpu.roll(x, LANES - nf, axis=1), pltpu.roll(x, nf, axis=1))
    return x * cos + partner * sin


def _mid_kernel(x_ref, ctx_ref, mod0_ref, mod1_ref, ysl_ref, ysc_ref, rest_ref, gluw_ref, glub_ref,
                sg_g_ref, sg_b_ref, sgw_ref, sgbias_ref, wout_ref, lng_ref, lnb_ref, w_ref,
                rtab_ref, ctab_ref, z1_ref, q_ref, k_ref, vt_ref, g_ref, zprev_ref, wbf_scr,
                wout_scr, ys_scr):
    i = pl.program_id(0)
    dq = N_HEADS * HD
    dkv = N_KV * HD

    @pl.when(i == 0)
    def _():
        zprev_ref[...] = jnp.zeros((TN, D), F32)
        wbf_scr[...] = w_ref[...].astype(BF16)
        wout_scr[...] = wout_ref[...].astype(BF16)

    j = jnp.maximum(i - 1, 0)
    m1 = mod1_ref[pl.ds(_block_mod_row(j), 1), :]
    h = (zprev_ref[...] * (1.0 + m1[:, D:2 * D]) + m1[:, :D]).astype(BF16)
    grid_row0 = (j % LAT_BLK_PER_BATCH) * (TN // GRID_W)
    table = lambda t: jnp.concatenate(
        [rtab_ref[t, pl.ds(grid_row0 + r, 1), :] + ctab_ref[t] for r in range(TN // GRID_W)],
        axis=0)
    cos = jnp.where(j < N_LAT_BLK, table(0), 1.0)
    sin = jnp.where(j < N_LAT_BLK, table(1), 0.0)
    lane = lax.broadcasted_iota(jnp.int32, (TN, LANES), 1)
    first_half = (lane % (HD // 2)) < (HD // 4)
    scale = HD ** -0.5 * LOG2E

    def project(c0, c1):
        return jnp.dot(h, wbf_scr[:, c0:c1], preferred_element_type=F32)

    def roped_tiles(p, mult):
        for c in range(p.shape[1] // LANES):
            r = _rope_tile(p[:, c * LANES:(c + 1) * LANES], cos, sin, first_half)
            yield c, (r * mult if mult != 1.0 else r).astype(BF16)

    t = jnp.minimum(i, N_BLK - 1)
    z = jnp.where(t < N_LAT_BLK, x_ref[...], ctx_ref[...])
    yp = jnp.where(t < N_LAT_BLK, ysl_ref[...], ysc_ref[...])
    for qt in range(S5_QT):
        for k in range(S5_CHUNK // S5_PAIRS):
            tiles = _block_transpose4([yp[qt * S5_PAIRS + pr][:, k * LANES:(k + 1) * LANES]
                                       for pr in range(S5_PAIRS)])
            for n, tile in enumerate(tiles):
                ys_scr[qt, pl.ds(k * S5_PAIRS + n, TN // S5_CHUNK, stride=S5_CHUNK), :] = tile
    ys = jnp.concatenate([ys_scr[qt] for qt in range(S5_QT)], axis=-1)
    gate = mod0_ref[pl.ds(_block_mod_row(t), 1), 2 * D:3 * D]
    tail = _tail0_stages(z, ys, rest_ref, gate, gluw_ref, glub_ref, sg_g_ref, sg_b_ref, sgw_ref,
                         sgbias_ref, wout_scr, lng_ref, lnb_ref)

    half_q = dq // 2
    p_q0 = project(0, half_q)
    next(tail)
    p_q1 = project(half_q, dq)
    next(tail)
    p_k = project(dq, dq + dkv)
    vt_ref[...] = lax.dot_general(wbf_scr[:, dq + dkv:dq + 2 * dkv], h, (((0,), (1,)), ((), ())),
                                  preferred_element_type=F32).astype(BF16)
    next(tail)
    next(tail)
    for c0, p in ((0, p_q0), (half_q, p_q1)):
        for c, r in roped_tiles(p, scale):
            q_ref[:, c0 + c * LANES:c0 + (c + 1) * LANES] = r
    for c, r in roped_tiles(p_k, 1.0):
        k_ref[:, c * LANES:(c + 1) * LANES] = r
    p_g = project(dq + 2 * dkv, ODD_IN)
    z1 = next(tail)
    z1_ref[...] = z1
    zprev_ref[...] = z1
    g_ref[...] = p_g.astype(BF16)


def _mid(x2, ctx2, mod0, mod1, ys_lat, ys_ctx, rest, glu_w, glu_b, sg_g, sg_b, sg_w, sg_bias,
         w_out, ln_g, ln_b, w_in1, row_tab, col_tab):
    dq = N_HEADS * HD
    dkv = N_KV * HD
    once = dict(pipeline_mode=pl.Buffered(1))
    row = lambda n: pl.BlockSpec((1, n), lambda i: (0, 0))
    lat = lambda i: (jnp.minimum(i, N_LAT_BLK - 1), 0)
    tail_blk = lambda i: (jnp.minimum(i, N_BLK - 1), 0)
    proj_blk = lambda i: (jnp.maximum(i - 1, 0), 0)
    return pl.pallas_call(
        _mid_kernel,
        grid=(N_BLK + 1,),
        in_specs=[
            pl.BlockSpec((TN, D), lat),
            pl.BlockSpec((R_CTX, D), lambda i: (0, 0), **once),
            pl.BlockSpec((8, 3 * D), lambda i: (0, 0)),
            pl.BlockSpec((8, 3 * D), lambda i: (0, 0)),
            pl.BlockSpec((S5_QT * S5_PAIRS, TN // S5_CHUNK, S5_CHUNK * PAIR_BLK),
                         lambda i: (0, jnp.minimum(i, N_LAT_BLK - 1), 0)),
            pl.BlockSpec((S5_QT * S5_PAIRS, R_CTX // S5_CHUNK, S5_CHUNK * PAIR_BLK),
                         lambda i: (0, 0, 0), **once),
            pl.BlockSpec((TN, EVEN_IN - S5_W), tail_blk),
            pl.BlockSpec((S5_W, S5_W), lambda i: (0, 0), **once),
            row(S5_W), row(SG_W), row(SG_W),
            pl.BlockSpec((SG_HEADS // 2, SG_CHUNK, 2 * SG_CHUNK), lambda i: (0, 0, 0)),
            pl.BlockSpec((SG_CHUNK, SG_W), lambda i: (0, 0)),
            pl.BlockSpec((S5_W + SG_W, D), lambda i: (0, 0), **once),
            row(D), row(D),
            pl.BlockSpec((D, ODD_IN), lambda i: (0, 0), **once),
            pl.BlockSpec((2, N_LAT // GRID_W, LANES), lambda i: (0, 0, 0)),
            pl.BlockSpec((2, GRID_W, LANES), lambda i: (0, 0, 0)),
        ],
        out_specs=[
            pl.BlockSpec((TN, D), tail_blk),
            pl.BlockSpec((TN, dq), proj_blk),
            pl.BlockSpec((TN, dkv), proj_blk),
            pl.BlockSpec((dkv, TN), lambda i: (0, jnp.maximum(i - 1, 0))),
            pl.BlockSpec((TN, dq), proj_blk),
        ],
        out_shape=[
            jax.ShapeDtypeStruct((R_ALL, D), F32),
            jax.ShapeDtypeStruct((R_ALL, dq), BF16),
            jax.ShapeDtypeStruct((R_ALL, dkv), BF16),
            jax.ShapeDtypeStruct((dkv, R_ALL), BF16),
            jax.ShapeDtypeStruct((R_ALL, dq), BF16),
        ],
        scratch_shapes=[pltpu.VMEM((TN, D), F32), pltpu.VMEM((D, ODD_IN), BF16),
                        pltpu.VMEM((S5_W + SG_W, D), BF16), pltpu.VMEM((S5_QT, TN, LANES), F32)],
        compiler_params=_fusing_params(19, skip=(2, 3, 17)),
        name="mid",
    )(x2, ctx2, mod0, mod1, ys_lat, ys_ctx, rest, glu_w, glu_b, sg_g, sg_b, sg_w, sg_bias, w_out,
      ln_g, ln_b, w_in1, row_tab, col_tab)


N_QBLK = N_LAT // ATT_BLK
GRP = N_HEADS // N_KV
ATT_SUB = 4
ATT_ROWS = ATT_SUB * ATT_BLK
N_QSTEP = N_QBLK // ATT_SUB
N_ATT_STEPS = B * N_QSTEP
assert ATT_ROWS == TN
assert WINDOW == ATT_BLK
ONES_ROWS = 16


def _attn_kernel(sink_ref, q_ref, kp_ref, kc_ref, kn_ref, kx_ref, vp_ref, vc_ref, vn_ref, vx_ref,
                 g_ref, z_ref, mod_ref, wout_ref, lng_ref, lnb_ref, out_ref, o_scr, wout_scr):
    step = pl.program_id(0)

    @pl.when(step == 0)
    def _():
        o_scr[...] = jnp.zeros((ATT_ROWS, N_HEADS * HD), BF16)
        wout_scr[...] = wout_ref[...].astype(BF16)

    t = jnp.maximum(step - 1, 0)
    gate = mod_ref[pl.ds(_block_mod_row(t), 1), 2 * D:3 * D]
    mix = (o_scr[...].astype(F32) * _silu(g_ref[...].astype(F32))).astype(BF16)
    y = jnp.dot(mix, wout_scr[...], preferred_element_type=F32) * gate
    out_ref[...] = _layer_norm(ALPHA * z_ref[...] + y, lng_ref[...], lnb_ref[...])

    i = jnp.minimum(step, N_ATT_STEPS - 1) % N_QSTEP
    half = ATT_BLK // 2
    n_win = 2 * ATT_BLK + half
    n_keys = n_win + N_CTX
    nq = GRP * half
    k_win = jnp.concatenate([kp_ref[...], kc_ref[...], kn_ref[...]], axis=0)
    vt_win = jnp.concatenate([vp_ref[...], vc_ref[...], vn_ref[...]], axis=1)
    k_ctx = kx_ref[...]
    vt_ctx = vx_ref[...]

    kpos = lax.broadcasted_iota(jnp.int32, (half, half), 0)
    qpos = lax.broadcasted_iota(jnp.int32, (half, half), 1)
    tile_q = lambda a: jnp.concatenate([a] * GRP, axis=1)
    neg = lambda ok: tile_q(jnp.where(ok, 0.0, NEG_INF))
    everyone = kpos >= 0
    qgrp = lax.broadcasted_iota(jnp.int32, (1, nq), 1) // half
    ones = jnp.ones((ONES_ROWS, n_keys), BF16)
    q = q_ref[...]

    halves = [(u, v) for u in range(ATT_SUB) for v in range(2)]
    units = [(n, h) for n in range(len(halves)) for h in range(N_KV)]
    k_all, vt_all, bias = [], [], []
    for u, v in halves:
        first = u * ATT_BLK + v * half
        k_all.append(jnp.concatenate([k_win[first:first + n_win], k_ctx], axis=0))
        vt_all.append(jnp.concatenate([vt_win[:, first:first + n_win], vt_ctx], axis=1))
        in_prev = [blk for blk in range(5) if first + blk * half < ATT_BLK]
        in_next = [blk for blk in range(5) if first + blk * half >= ATT_BLK + ATT_ROWS]
        blocks = {}
        for blk in sorted(set([0, 4] + in_prev + in_next)):
            ok = (kpos >= qpos) if blk == 0 else ((kpos <= qpos) if blk == 4 else everyone)
            if blk in in_prev:
                ok = ok & (i > 0)
            if blk in in_next:
                ok = ok & (i < N_QSTEP - 1)
            blocks[blk] = neg(ok)
        bias.append(blocks)
    scores = []
    for n, h in units:
        u, v = halves[n]
        r0 = u * ATT_BLK + v * half
        kh = k_all[n][:, h * HD:(h + 1) * HD]
        qh = jnp.concatenate(
            [q[r0:r0 + half, (h * GRP + g) * HD:(h * GRP + g + 1) * HD] for g in range(GRP)],
            axis=0)
        scores.append(lax.dot_general(kh, qh, (((1,), (1,)), ((), ())),
                                      preferred_element_type=F32))
    probs = []
    for (n, h), s in zip(units, scores):
        s = jnp.concatenate(
            [s[blk * half:(blk + 1) * half] + bias[n][blk] if blk in bias[n]
             else s[blk * half:(blk + 1) * half] for blk in range(5)] + [s[n_win:]], axis=0)
        sink = jnp.zeros((1, nq), F32)
        for g in range(GRP):
            sink = jnp.where(qgrp == g, sink_ref[h * GRP + g] * LOG2E, sink)
        m = jnp.maximum(jnp.max(s, axis=0, keepdims=True), sink)
        probs.append((jnp.exp2(s - m).astype(BF16), jnp.exp2(sink - m)))
    outs = [[None] * N_HEADS for _ in halves]
    for (n, h), (p, p_sink) in zip(units, probs):
        vt1 = jnp.concatenate([vt_all[n][h * HD:(h + 1) * HD], ones], axis=0)
        ov = jnp.dot(vt1, p, preferred_element_type=F32)
        o_t = ov[:HD] / (ov[HD:HD + 1] + p_sink)
        for g in range(GRP):
            outs[n][h * GRP + g] = o_t[:, g * half:(g + 1) * half]
    for u in range(ATT_SUB):
        o_t = jnp.concatenate(
            [jnp.concatenate([outs[2 * u][hh], outs[2 * u + 1][hh]], axis=1)
             for hh in range(N_HEADS)], axis=0)
        o_scr[u * ATT_BLK:(u + 1) * ATT_BLK, :] = jnp.transpose(o_t).astype(BF16)


def _attention_tail(sink, q, k, vt, g, z1, mod, w_out, ln_g, ln_b):
    dq = N_HEADS * HD
    dkv = N_KV * HD
    ctx_blk0 = R_LAT // N_CTX
    cur = lambda s: jnp.minimum(s, N_ATT_STEPS - 1)
    batch = lambda s: cur(s) // N_QSTEP
    qstep = lambda s: cur(s) % N_QSTEP
    prev_blk = lambda s: batch(s) * N_QBLK + jnp.maximum(ATT_SUB * qstep(s) - 1, 0)
    next_blk = lambda s: batch(s) * N_QBLK + jnp.minimum(ATT_SUB * (qstep(s) + 1), N_QBLK - 1)
    tail = lambda s: jnp.maximum(s - 1, 0)
    once = dict(pipeline_mode=pl.Buffered(1))
    row = lambda n: pl.BlockSpec((1, n), lambda s, sk: (0, 0))
    return pl.pallas_call(
        _attn_kernel,
        grid_spec=pltpu.PrefetchScalarGridSpec(
            num_scalar_prefetch=1,
            grid=(N_ATT_STEPS + 1,),
            in_specs=[
                pl.BlockSpec((ATT_ROWS, dq), lambda s, sk: (cur(s), 0)),
                pl.BlockSpec((ATT_BLK, dkv), lambda s, sk: (prev_blk(s), 0)),
                pl.BlockSpec((ATT_ROWS, dkv), lambda s, sk: (cur(s), 0)),
                pl.BlockSpec((ATT_BLK, dkv), lambda s, sk: (next_blk(s), 0)),
                pl.BlockSpec((N_CTX, dkv), lambda s, sk: (ctx_blk0 + batch(s), 0)),
                pl.BlockSpec((dkv, ATT_BLK), lambda s, sk: (0, prev_blk(s))),
                pl.BlockSpec((dkv, ATT_ROWS), lambda s, sk: (0, cur(s))),
                pl.BlockSpec((dkv, ATT_BLK), lambda s, sk: (0, next_blk(s))),
                pl.BlockSpec((dkv, N_CTX), lambda s, sk: (0, ctx_blk0 + batch(s))),
                pl.BlockSpec((ATT_ROWS, dq), lambda s, sk: (tail(s), 0)),
                pl.BlockSpec((ATT_ROWS, D), lambda s, sk: (tail(s), 0)),
                pl.BlockSpec((8, 3 * D), lambda s, sk: (0, 0)),
                pl.BlockSpec((dq, D), lambda s, sk: (0, 0), **once),
                row(D), row(D),
            ],
            out_specs=pl.BlockSpec((ATT_ROWS, D), lambda s, sk: (tail(s), 0)),
            scratch_shapes=[pltpu.VMEM((ATT_ROWS, dq), BF16), pltpu.VMEM((dq, D), BF16)],
        ),
        out_shape=jax.ShapeDtypeStruct((R_LAT, D), F32),
        compiler_params=_fusing_params(16, skip=(0, 12)),
        name="attention",
    )(sink, q, k, k, k, k, vt, vt, vt, vt, g, z1, mod, w_out, ln_g, ln_b)


def kernel(x, c, ctx, c_ctx, mod_w, mod_b, ln_g, ln_b, e_w_in, e_w_out, s5_lam_re, s5_lam_im,
           s5_log_dt, s5_b_re, s5_b_im, s5_c_re, s5_c_im, s5_d, glu_w, glu_b, sg_ln_g, sg_ln_b,
           sg_w, sg_b, o_w_in, o_w_out, o_sink):
    x2 = x.reshape(R_LAT, D)
    ctx2 = ctx.reshape(R_CTX, D)
    mod = _modulation(c, c_ctx, mod_w, mod_b)

    vp, rest = _in0(x2, ctx2, mod[0], e_w_in[0])
    t_mat, e_mat, ft_mat, al_re, al_im = _s5_prep(
        s5_lam_re[0], s5_lam_im[0], s5_log_dt[0], s5_b_re[0], s5_b_im[0], s5_c_re[0], s5_c_im[0],
        s5_d[0])
    ys_lat, ys_ctx = _s5_out(vp, al_re, al_im, e_mat, t_mat, ft_mat)
    sg_bias = jnp.repeat(jnp.transpose(sg_b[0]), SG_W // SG_HEADS, axis=1)
    sg_w2 = jnp.transpose(sg_w[0].reshape(SG_HEADS // 2, 2, SG_CHUNK, SG_CHUNK), (0, 2, 1, 3))
    sg_w2 = sg_w2.reshape(SG_HEADS // 2, SG_CHUNK, 2 * SG_CHUNK).astype(BF16)
    row_tab, col_tab = _rope_tables()
    z1, q, k, vt, g = _mid(
        x2, ctx2, mod[0], mod[1], ys_lat, ys_ctx, rest, glu_w[0].astype(BF16),
        glu_b[0].reshape(1, S5_W), sg_ln_g[0].reshape(1, SG_W), sg_ln_b[0].reshape(1, SG_W),
        sg_w2, sg_bias, e_w_out[0], ln_g[0].reshape(1, D),
        ln_b[0].reshape(1, D), o_w_in[0], row_tab, col_tab)

    out = _attention_tail(o_sink[0], q, k, vt, g, z1, mod[1], o_w_out[0],
                          ln_g[1].reshape(1, D), ln_b[1].reshape(1, D))
    return out.reshape(B, N_LAT, D)
```
